```python
import jax, jax.numpy as jnp
from jax import lax
import numpy as np

D_MODEL = 1024
BATCH = 8
SEQ = 2048
DEPTH = 4

N_MIXERS = 2
N_A_LAYERS = (DEPTH + 1) // 2
N_B_LAYERS = DEPTH // 2

DN_HEADS = 8
DN_HEAD_DIM = 128
DN_KEY = DN_HEADS * DN_HEAD_DIM
DN_VAL = DN_HEADS * DN_HEAD_DIM
DN_QKV = 2 * DN_KEY + DN_VAL
DN_IN = DN_QKV + DN_VAL + 2 * DN_HEADS
DN_CONV = 4
DN_CHUNK = 64

CF_CH = D_MODEL
CF_KERNEL = 31

FF_DIM = 4 * D_MODEL

ALPHA = (2.0 * DEPTH) ** 0.25
BETA_INIT = (8.0 * DEPTH) ** -0.25
N_MOD = 6
LN_EPS = 1e-5
RMS_EPS = 1e-6
L2_EPS = 1e-6

kernel_name = "hybrid_gdn_conformer_deepnorm_adaln"


def layer_norm(x, g, b):
    xf = x.astype(jnp.float32)
    mu = jnp.mean(xf, axis=-1, keepdims=True)
    var = jnp.mean(jnp.square(xf - mu), axis=-1, keepdims=True)
    return ((xf - mu) * lax.rsqrt(var + LN_EPS) * g + b).astype(x.dtype)


def l2_normalize(x):
    xf = x.astype(jnp.float32)
    return xf * lax.rsqrt(jnp.sum(xf * xf, axis=-1, keepdims=True) + L2_EPS)


def causal_depthwise_conv(x, w):
    k, ch = w.shape
    return lax.conv_general_dilated(
        x, w.astype(x.dtype)[:, None, :], window_strides=(1,), padding=((k - 1, 0),),
        dimension_numbers=('NWC', 'WIO', 'NWC'), feature_group_count=ch)


def gated_delta_rule(q, k, v, g, beta):
    b, s, h, dk = q.shape
    dv = v.shape[-1]
    c = DN_CHUNK
    n = s // c
    f32 = jnp.float32

    def chunks(t):
        t = t.astype(f32).reshape((b, n, c, h) + t.shape[3:])
        return jnp.moveaxis(t, 3, 1)

    q, k, v, g, beta = chunks(q), chunks(k), chunks(v), chunks(g), chunks(beta)
    q = q * (dk ** -0.5)
    gam = jnp.cumsum(g, axis=-1)
    causal = jnp.tril(jnp.ones((c, c), dtype=bool))
    strict = jnp.tril(jnp.ones((c, c), dtype=bool), -1)
    diff = gam[..., :, None] - gam[..., None, :]
    decay = jnp.exp(jnp.where(causal, diff, -jnp.inf))

    kb = k * beta[..., None]
    a_kk = jnp.where(strict, jnp.einsum('bhncd,bhnsd->bhncs', kb, k) * decay, 0.0)
    eye = jnp.eye(c, dtype=f32)
    t_inv = lax.linalg.triangular_solve(eye + a_kk, jnp.broadcast_to(eye, a_kk.shape),
                                        left_side=True, lower=True)
    u = jnp.matmul(t_inv, v * beta[..., None])
    w = jnp.matmul(t_inv, kb * jnp.exp(gam)[..., None])
    a_qk = jnp.einsum('bhncd,bhnsd->bhncs', q, k) * decay
    q_dec = q * jnp.exp(gam)[..., None]
    k_dec = k * jnp.exp(gam[..., -1:] - gam)[..., None]
    g_last = jnp.exp(gam[..., -1])

    def step(state, xs):
        u_i, w_i, a_i, qd_i, kd_i, gl_i = xs
        v_new = u_i - jnp.einsum('bhck,bhkv->bhcv', w_i, state)
        o_i = (jnp.einsum('bhck,bhkv->bhcv', qd_i, state)
               + jnp.einsum('bhcs,bhsv->bhcv', a_i, v_new))
        state = state * gl_i[..., None, None] + jnp.einsum('bhck,bhcv->bhkv', kd_i, v_new)
        return state, o_i

    xs = tuple(jnp.moveaxis(t, 2, 0) for t in (u, w, a_qk, q_dec, k_dec, g_last))
    s0 = jnp.zeros((b, h, dk, dv), f32)
    _, o = lax.scan(step, s0, xs)
    return jnp.transpose(o, (1, 0, 3, 2, 4)).reshape(b, s, h, dv)


def deltanet_mixer(h, w_in, conv_w, a_log, dt_bias, norm_w, w_out):
    b, s, _ = h.shape
    proj = h @ w_in
    qkv, z, bt, at = jnp.split(proj, [DN_QKV, DN_QKV + DN_VAL, DN_QKV + DN_VAL + DN_HEADS], axis=-1)
    qkv = jax.nn.silu(causal_depthwise_conv(qkv, conv_w))
    q, k, v = jnp.split(qkv, [DN_KEY, 2 * DN_KEY], axis=-1)
    q = l2_normalize(q.reshape(b, s, DN_HEADS, DN_HEAD_DIM))
    k = l2_normalize(k.reshape(b, s, DN_HEADS, DN_HEAD_DIM))
    v = v.reshape(b, s, DN_HEADS, DN_HEAD_DIM)
    beta = jax.nn.sigmoid(bt.astype(jnp.float32))
    g = -jnp.exp(a_log.astype(jnp.float32)) * jax.nn.softplus(at.astype(jnp.float32) + dt_bias)
    o = gated_delta_rule(q, k, v, g, beta)
    o = o * lax.rsqrt(jnp.mean(o * o, axis=-1, keepdims=True) + RMS_EPS) * norm_w
    o = o * jax.nn.silu(z.reshape(b, s, DN_HEADS, DN_HEAD_DIM).astype(jnp.float32))
    return o.reshape(b, s, DN_VAL).astype(h.dtype) @ w_out


def conformer_conv_mixer(h, w_in, dw_w, dw_b, ln_g, ln_b, w_out):
    val, gate = jnp.split(h @ w_in, 2, axis=-1)
    u = val * jax.nn.sigmoid(gate)
    u = causal_depthwise_conv(u, dw_w) + dw_b
    u = jax.nn.silu(layer_norm(u, ln_g, ln_b))
    return u @ w_out


def sq_relu_mlp(h, w1, w2):
    return jnp.square(jax.nn.relu(h @ w1)) @ w2


def _fwd_setup_inputs(seed: int = 0) -> dict:
    key = jax.random.key(seed)
    ks = jax.random.split(key, 24)
    nrm = jax.random.normal
    f32 = jnp.float32
    x = nrm(ks[0], (BATCH, SEQ, D_MODEL), f32)
    c = nrm(ks[1], (BATCH, D_MODEL), f32)
    ada_w = nrm(ks[2], (DEPTH, D_MODEL, N_MOD * D_MODEL), f32) * (0.1 * D_MODEL ** -0.5)
    ada_b = nrm(ks[3], (DEPTH, N_MOD * D_MODEL), f32) * 0.01
    ln_g = 1.0 + 0.01 * nrm(ks[4], (DEPTH, 2, D_MODEL), f32)
    ln_b = 0.01 * nrm(ks[5], (DEPTH, 2, D_MODEL), f32)
    dn_w_in = nrm(ks[6], (N_A_LAYERS, D_MODEL, DN_IN), f32) * D_MODEL ** -0.5
    dn_conv_w = nrm(ks[7], (N_A_LAYERS, DN_CONV, DN_QKV), f32) * DN_CONV ** -0.5
    dn_a_log = jnp.log(jax.random.uniform(ks[8], (N_A_LAYERS, DN_HEADS), f32, 1.0, 16.0))
    dt = jnp.exp(jax.random.uniform(ks[9], (N_A_LAYERS, DN_HEADS), f32, float(np.log(1e-3)), float(np.log(1e-1))))
    dn_dt_bias = dt + jnp.log(-jnp.expm1(-dt))
    dn_norm_w = 1.0 + 0.01 * nrm(ks[10], (N_A_LAYERS, DN_HEAD_DIM), f32)
    dn_w_out = nrm(ks[11], (N_A_LAYERS, DN_VAL, D_MODEL), f32) * (BETA_INIT * DN_VAL ** -0.5)
    cf_w_in = nrm(ks[12], (N_B_LAYERS, D_MODEL, 2 * CF_CH), f32) * D_MODEL ** -0.5
    cf_dw_w = nrm(ks[13], (N_B_LAYERS, CF_KERNEL, CF_CH), f32) * CF_KERNEL ** -0.5
    cf_dw_b = 0.01 * nrm(ks[14], (N_B_LAYERS, CF_CH), f32)
    cf_ln_g = 1.0 + 0.01 * nrm(ks[15], (N_B_LAYERS, CF_CH), f32)
    cf_ln_b = 0.01 * nrm(ks[16], (N_B_LAYERS, CF_CH), f32)
    cf_w_out = nrm(ks[17], (N_B_LAYERS, CF_CH, D_MODEL), f32) * (BETA_INIT * CF_CH ** -0.5)
    ff_w1 = nrm(ks[18], (DEPTH, D_MODEL, FF_DIM), f32) * D_MODEL ** -0.5
    ff_w2 = nrm(ks[19], (DEPTH, FF_DIM, D_MODEL), f32) * (BETA_INIT * FF_DIM ** -0.5)
    return {"x": x, "c": c, "ada_w": ada_w, "ada_b": ada_b, "ln_g": ln_g, "ln_b": ln_b,
            "dn_w_in": dn_w_in, "dn_conv_w": dn_conv_w, "dn_a_log": dn_a_log,
            "dn_dt_bias": dn_dt_bias, "dn_norm_w": dn_norm_w, "dn_w_out": dn_w_out,
            "cf_w_in": cf_w_in, "cf_dw_w": cf_dw_w, "cf_dw_b": cf_dw_b, "cf_ln_g": cf_ln_g,
            "cf_ln_b": cf_ln_b, "cf_w_out": cf_w_out, "ff_w1": ff_w1, "ff_w2": ff_w2}


def _fwd_reference(x, c, ada_w, ada_b, ln_g, ln_b, dn_w_in, dn_conv_w, dn_a_log, dn_dt_bias,
              dn_norm_w, dn_w_out, cf_w_in, cf_dw_w, cf_dw_b, cf_ln_g, cf_ln_b, cf_w_out,
              ff_w1, ff_w2):
    cond = jax.nn.silu(c)
    for i in range(DEPTH):
        mod = cond @ ada_w[i] + ada_b[i]
        sh1, sc1, gt1, sh2, sc2, gt2 = [m[:, None, :] for m in jnp.split(mod, N_MOD, axis=-1)]
        h = x * (1.0 + sc1) + sh1
        j = i // N_MIXERS
        if i % N_MIXERS == 0:
            y = deltanet_mixer(h, dn_w_in[j], dn_conv_w[j], dn_a_log[j], dn_dt_bias[j],
                               dn_norm_w[j], dn_w_out[j])
        else:
            y = conformer_conv_mixer(h, cf_w_in[j], cf_dw_w[j], cf_dw_b[j], cf_ln_g[j],
                                     cf_ln_b[j], cf_w_out[j])
        x = layer_norm(ALPHA * x + (1.0 + gt1) * y, ln_g[i, 0], ln_b[i, 0])
        h = x * (1.0 + sc2) + sh2
        x = layer_norm(ALPHA * x + (1.0 + gt2) * sq_relu_mlp(h, ff_w1[i], ff_w2[i]), ln_g[i, 1], ln_b[i, 1])
    return x


import jax as _jax
import jax.numpy as _jnp

TWIN_FORMAT = 'train_step'
FWD_PARAMS = ['x', 'c', 'ada_w', 'ada_b', 'ln_g', 'ln_b', 'dn_w_in', 'dn_conv_w', 'dn_a_log', 'dn_dt_bias', 'dn_norm_w', 'dn_w_out', 'cf_w_in', 'cf_dw_w', 'cf_dw_b', 'cf_ln_g', 'cf_ln_b', 'cf_w_out', 'ff_w1', 'ff_w2']
TWIN_WEIGHTS = ['ada_w', 'ada_b', 'ln_g', 'ln_b', 'dn_w_in', 'dn_conv_w', 'dn_a_log', 'dn_dt_bias', 'dn_norm_w', 'dn_w_out', 'cf_w_in', 'cf_dw_w', 'cf_dw_b', 'cf_ln_g', 'cf_ln_b', 'cf_w_out', 'ff_w1', 'ff_w2']
TWIN_DIFF_INPUT = 'x'
TWIN_INPUTS = ['x', 'c', 'ada_w', 'ada_b', 'ln_g', 'ln_b', 'dn_w_in', 'dn_conv_w', 'dn_a_log', 'dn_dt_bias', 'dn_norm_w', 'dn_w_out', 'cf_w_in', 'cf_dw_w', 'cf_dw_b', 'cf_ln_g', 'cf_ln_b', 'cf_w_out', 'ff_w1', 'ff_w2', 'loss_target', 'm_ada_w', 'm_ada_b', 'm_ln_g', 'm_ln_b', 'm_dn_w_in', 'm_dn_conv_w', 'm_dn_a_log', 'm_dn_dt_bias', 'm_dn_norm_w', 'm_dn_w_out', 'm_cf_w_in', 'm_cf_dw_w', 'm_cf_dw_b', 'm_cf_ln_g', 'm_cf_ln_b', 'm_cf_w_out', 'm_ff_w1', 'm_ff_w2', 'v_ada_w', 'v_ada_b', 'v_ln_g', 'v_ln_b', 'v_dn_w_in', 'v_dn_conv_w', 'v_dn_a_log', 'v_dn_dt_bias', 'v_dn_norm_w', 'v_dn_w_out', 'v_cf_w_in', 'v_cf_dw_w', 'v_cf_dw_b', 'v_cf_ln_g', 'v_cf_ln_b', 'v_cf_w_out', 'v_ff_w1', 'v_ff_w2']
TWIN_OUTPUTS = ['loss', 'grad_x', 'grad_ada_w', 'grad_ada_b', 'grad_ln_g', 'grad_ln_b', 'grad_dn_w_in', 'grad_dn_conv_w', 'grad_dn_a_log', 'grad_dn_dt_bias', 'grad_dn_norm_w', 'grad_dn_w_out', 'grad_cf_w_in', 'grad_cf_dw_w', 'grad_cf_dw_b', 'grad_cf_ln_g', 'grad_cf_ln_b', 'grad_cf_w_out', 'grad_ff_w1', 'grad_ff_w2', 'delta_ada_w', 'delta_ada_b', 'delta_ln_g', 'delta_ln_b', 'delta_dn_w_in', 'delta_dn_conv_w', 'delta_dn_a_log', 'delta_dn_dt_bias', 'delta_dn_norm_w', 'delta_dn_w_out', 'delta_cf_w_in', 'delta_cf_dw_w', 'delta_cf_dw_b', 'delta_cf_ln_g', 'delta_cf_ln_b', 'delta_cf_w_out', 'delta_ff_w1', 'delta_ff_w2', 'new_m_ada_w', 'new_m_ada_b', 'new_m_ln_g', 'new_m_ln_b', 'new_m_dn_w_in', 'new_m_dn_conv_w', 'new_m_dn_a_log', 'new_m_dn_dt_bias', 'new_m_dn_norm_w', 'new_m_dn_w_out', 'new_m_cf_w_in', 'new_m_cf_dw_w', 'new_m_cf_dw_b', 'new_m_cf_ln_g', 'new_m_cf_ln_b', 'new_m_cf_w_out', 'new_m_ff_w1', 'new_m_ff_w2', 'new_v_ada_w', 'new_v_ada_b', 'new_v_ln_g', 'new_v_ln_b', 'new_v_dn_w_in', 'new_v_dn_conv_w', 'new_v_dn_a_log', 'new_v_dn_dt_bias', 'new_v_dn_norm_w', 'new_v_dn_w_out', 'new_v_cf_w_in', 'new_v_cf_dw_w', 'new_v_cf_dw_b', 'new_v_cf_ln_g', 'new_v_cf_ln_b', 'new_v_cf_w_out', 'new_v_ff_w1', 'new_v_ff_w2']
TWIN_LEAF_KINDS = {'loss': 'loss', 'grad_x': 'grad_x', 'grad_ada_w': 'grad_w', 'grad_ada_b': 'grad_w', 'grad_ln_g': 'grad_w', 'grad_ln_b': 'grad_w', 'grad_dn_w_in': 'grad_w', 'grad_dn_conv_w': 'grad_w', 'grad_dn_a_log': 'grad_w', 'grad_dn_dt_bias': 'grad_w', 'grad_dn_norm_w': 'grad_w', 'grad_dn_w_out': 'grad_w', 'grad_cf_w_in': 'grad_w', 'grad_cf_dw_w': 'grad_w', 'grad_cf_dw_b': 'grad_w', 'grad_cf_ln_g': 'grad_w', 'grad_cf_ln_b': 'grad_w', 'grad_cf_w_out': 'grad_w', 'grad_ff_w1': 'grad_w', 'grad_ff_w2': 'grad_w', 'delta_ada_w': 'delta_w', 'delta_ada_b': 'delta_w', 'delta_ln_g': 'delta_w', 'delta_ln_b': 'delta_w', 'delta_dn_w_in': 'delta_w', 'delta_dn_conv_w': 'delta_w', 'delta_dn_a_log': 'delta_w', 'delta_dn_dt_bias': 'delta_w', 'delta_dn_norm_w': 'delta_w', 'delta_dn_w_out': 'delta_w', 'delta_cf_w_in': 'delta_w', 'delta_cf_dw_w': 'delta_w', 'delta_cf_dw_b': 'delta_w', 'delta_cf_ln_g': 'delta_w', 'delta_cf_ln_b': 'delta_w', 'delta_cf_w_out': 'delta_w', 'delta_ff_w1': 'delta_w', 'delta_ff_w2': 'delta_w', 'new_m_ada_w': 'new_m', 'new_m_ada_b': 'new_m', 'new_m_ln_g': 'new_m', 'new_m_ln_b': 'new_m', 'new_m_dn_w_in': 'new_m', 'new_m_dn_conv_w': 'new_m', 'new_m_dn_a_log': 'new_m', 'new_m_dn_dt_bias': 'new_m', 'new_m_dn_norm_w': 'new_m', 'new_m_dn_w_out': 'new_m', 'new_m_cf_w_in': 'new_m', 'new_m_cf_dw_w': 'new_m', 'new_m_cf_dw_b': 'new_m', 'new_m_cf_ln_g': 'new_m', 'new_m_cf_ln_b': 'new_m', 'new_m_cf_w_out': 'new_m', 'new_m_ff_w1': 'new_m', 'new_m_ff_w2': 'new_m', 'new_v_ada_w': 'new_v', 'new_v_ada_b': 'new_v', 'new_v_ln_g': 'new_v', 'new_v_ln_b': 'new_v', 'new_v_dn_w_in': 'new_v', 'new_v_dn_conv_w': 'new_v', 'new_v_dn_a_log': 'new_v', 'new_v_dn_dt_bias': 'new_v', 'new_v_dn_norm_w': 'new_v', 'new_v_dn_w_out': 'new_v', 'new_v_cf_w_in': 'new_v', 'new_v_cf_dw_w': 'new_v', 'new_v_cf_dw_b': 'new_v', 'new_v_cf_ln_g': 'new_v', 'new_v_cf_ln_b': 'new_v', 'new_v_cf_w_out': 'new_v', 'new_v_ff_w1': 'new_v', 'new_v_ff_w2': 'new_v'}


def _forward(args):
    return _fwd_reference(*[args[k] for k in FWD_PARAMS])


def _output_shape():
    out = _jax.eval_shape(lambda: _forward(_fwd_setup_inputs(0)))
    return out.shape, out.dtype

N_MICROBATCH = 1
ADAM_LR = 0.001
ADAM_B1 = 0.9
ADAM_B2 = 0.999
ADAM_EPS = 1e-08
ADAM_WD = 0.01
ADAM_STEP = 10
PER_EXAMPLE_BATCH_AXIS = {'x': 0, 'c': 0, 'loss_target': 0}
SHARED_INPUTS = []
_WEIGHT_DTYPES = {'ada_w': _jnp.float32, 'ada_b': _jnp.float32, 'ln_g': _jnp.float32, 'ln_b': _jnp.float32, 'dn_w_in': _jnp.float32, 'dn_conv_w': _jnp.float32, 'dn_a_log': _jnp.float32, 'dn_dt_bias': _jnp.float32, 'dn_norm_w': _jnp.float32, 'dn_w_out': _jnp.float32, 'cf_w_in': _jnp.float32, 'cf_dw_w': _jnp.float32, 'cf_dw_b': _jnp.float32, 'cf_ln_g': _jnp.float32, 'cf_ln_b': _jnp.float32, 'cf_w_out': _jnp.float32, 'ff_w1': _jnp.float32, 'ff_w2': _jnp.float32}
MOMENT_SCALE = {'ada_w': 2.220311e-02, 'ada_b': 3.805942e-02, 'ln_g': 5.677228e+00, 'ln_b': 1.363643e+00, 'dn_w_in': 1.636471e-02, 'dn_conv_w': 1.524953e-02, 'dn_a_log': 6.362662e-02, 'dn_dt_bias': 6.049272e-02, 'dn_norm_w': 5.593577e-02, 'dn_w_out': 4.617113e-02, 'cf_w_in': 1.529935e-02, 'cf_dw_w': 1.994526e-02, 'cf_dw_b': 4.327698e-02, 'cf_ln_g': 2.621655e-02, 'cf_ln_b': 2.574794e-02, 'cf_w_out': 4.820760e-02, 'ff_w1': 2.211726e-02, 'ff_w2': 9.696644e-02}


def _to_microbatches(a, axis):
    t = _jnp.moveaxis(a, axis, 0)
    t = t.reshape((N_MICROBATCH, t.shape[0] // N_MICROBATCH) + t.shape[1:])
    return _jnp.moveaxis(t, 1, axis + 1)


def setup_inputs(seed: int = 0) -> dict:
    inp = _fwd_setup_inputs(seed)
    key = _jax.random.fold_in(_jax.random.key(seed), 7919)
    shape, _ = _output_shape()
    out = dict(inp)
    out["loss_target"] = _jax.random.normal(_jax.random.fold_in(key, 0), shape, _jnp.float32)
    for i, name in enumerate(TWIN_WEIGHTS):
        w = inp[name].astype(_jnp.float32)
        if MOMENT_SCALE is None:
            s = _jnp.sqrt(_jnp.mean(_jnp.square(w)) + 1e-30)
        else:
            s = MOMENT_SCALE[name]
        km, kv = _jax.random.split(_jax.random.fold_in(key, i + 1))
        out[name] = w
        out["m_" + name] = s * _jax.random.normal(km, w.shape, _jnp.float32)
        out["v_" + name] = (s * s) * _jax.random.uniform(kv, w.shape, _jnp.float32, 0.5, 1.5)
    if N_MICROBATCH > 1:
        for name, axis in PER_EXAMPLE_BATCH_AXIS.items():
            out[name] = _to_microbatches(out[name], axis)
    return {'x': out['x'], 'c': out['c'], 'ada_w': out['ada_w'], 'ada_b': out['ada_b'], 'ln_g': out['ln_g'], 'ln_b': out['ln_b'], 'dn_w_in': out['dn_w_in'], 'dn_conv_w': out['dn_conv_w'], 'dn_a_log': out['dn_a_log'], 'dn_dt_bias': out['dn_dt_bias'], 'dn_norm_w': out['dn_norm_w'], 'dn_w_out': out['dn_w_out'], 'cf_w_in': out['cf_w_in'], 'cf_dw_w': out['cf_dw_w'], 'cf_dw_b': out['cf_dw_b'], 'cf_ln_g': out['cf_ln_g'], 'cf_ln_b': out['cf_ln_b'], 'cf_w_out': out['cf_w_out'], 'ff_w1': out['ff_w1'], 'ff_w2': out['ff_w2'], 'loss_target': out['loss_target'], 'm_ada_w': out['m_ada_w'], 'm_ada_b': out['m_ada_b'], 'm_ln_g': out['m_ln_g'], 'm_ln_b': out['m_ln_b'], 'm_dn_w_in': out['m_dn_w_in'], 'm_dn_conv_w': out['m_dn_conv_w'], 'm_dn_a_log': out['m_dn_a_log'], 'm_dn_dt_bias': out['m_dn_dt_bias'], 'm_dn_norm_w': out['m_dn_norm_w'], 'm_dn_w_out': out['m_dn_w_out'], 'm_cf_w_in': out['m_cf_w_in'], 'm_cf_dw_w': out['m_cf_dw_w'], 'm_cf_dw_b': out['m_cf_dw_b'], 'm_cf_ln_g': out['m_cf_ln_g'], 'm_cf_ln_b': out['m_cf_ln_b'], 'm_cf_w_out': out['m_cf_w_out'], 'm_ff_w1': out['m_ff_w1'], 'm_ff_w2': out['m_ff_w2'], 'v_ada_w': out['v_ada_w'], 'v_ada_b': out['v_ada_b'], 'v_ln_g': out['v_ln_g'], 'v_ln_b': out['v_ln_b'], 'v_dn_w_in': out['v_dn_w_in'], 'v_dn_conv_w': out['v_dn_conv_w'], 'v_dn_a_log': out['v_dn_a_log'], 'v_dn_dt_bias': out['v_dn_dt_bias'], 'v_dn_norm_w': out['v_dn_norm_w'], 'v_dn_w_out': out['v_dn_w_out'], 'v_cf_w_in': out['v_cf_w_in'], 'v_cf_dw_w': out['v_cf_dw_w'], 'v_cf_dw_b': out['v_cf_dw_b'], 'v_cf_ln_g': out['v_cf_ln_g'], 'v_cf_ln_b': out['v_cf_ln_b'], 'v_cf_w_out': out['v_cf_w_out'], 'v_ff_w1': out['v_ff_w1'], 'v_ff_w2': out['v_ff_w2']}


def _loss(weights, diff, rest, loss_target):
    with _jax.named_scope("forward"):
        args = {**rest, TWIN_DIFF_INPUT: diff, **{k: w.astype(_WEIGHT_DTYPES[k]) for k, w in weights.items()}}
        y = _forward(args)
    with _jax.named_scope("loss_head"):
        err = _jnp.square(y.astype(_jnp.float32) - loss_target)
        return 0.5 * _jnp.sum(_jnp.mean(err, axis=-1)) if err.ndim else 0.5 * err


def _adamw(w, g, m, v):
    m = ADAM_B1 * m + (1.0 - ADAM_B1) * g
    v = ADAM_B2 * v + (1.0 - ADAM_B2) * _jnp.square(g)
    m_hat = m / (1.0 - ADAM_B1 ** ADAM_STEP)
    v_hat = v / (1.0 - ADAM_B2 ** ADAM_STEP)
    delta = -ADAM_LR * (m_hat / (_jnp.sqrt(v_hat) + ADAM_EPS) + ADAM_WD * w)
    return delta, m, v


def reference(x, c, ada_w, ada_b, ln_g, ln_b, dn_w_in, dn_conv_w, dn_a_log, dn_dt_bias, dn_norm_w, dn_w_out, cf_w_in, cf_dw_w, cf_dw_b, cf_ln_g, cf_ln_b, cf_w_out, ff_w1, ff_w2, loss_target, m_ada_w, m_ada_b, m_ln_g, m_ln_b, m_dn_w_in, m_dn_conv_w, m_dn_a_log, m_dn_dt_bias, m_dn_norm_w, m_dn_w_out, m_cf_w_in, m_cf_dw_w, m_cf_dw_b, m_cf_ln_g, m_cf_ln_b, m_cf_w_out, m_ff_w1, m_ff_w2, v_ada_w, v_ada_b, v_ln_g, v_ln_b, v_dn_w_in, v_dn_conv_w, v_dn_a_log, v_dn_dt_bias, v_dn_norm_w, v_dn_w_out, v_cf_w_in, v_cf_dw_w, v_cf_dw_b, v_cf_ln_g, v_cf_ln_b, v_cf_w_out, v_ff_w1, v_ff_w2):
    given = dict(x=x, c=c, ada_w=ada_w, ada_b=ada_b, ln_g=ln_g, ln_b=ln_b, dn_w_in=dn_w_in, dn_conv_w=dn_conv_w, dn_a_log=dn_a_log, dn_dt_bias=dn_dt_bias, dn_norm_w=dn_norm_w, dn_w_out=dn_w_out, cf_w_in=cf_w_in, cf_dw_w=cf_dw_w, cf_dw_b=cf_dw_b, cf_ln_g=cf_ln_g, cf_ln_b=cf_ln_b, cf_w_out=cf_w_out, ff_w1=ff_w1, ff_w2=ff_w2, loss_target=loss_target, m_ada_w=m_ada_w, m_ada_b=m_ada_b, m_ln_g=m_ln_g, m_ln_b=m_ln_b, m_dn_w_in=m_dn_w_in, m_dn_conv_w=m_dn_conv_w, m_dn_a_log=m_dn_a_log, m_dn_dt_bias=m_dn_dt_bias, m_dn_norm_w=m_dn_norm_w, m_dn_w_out=m_dn_w_out, m_cf_w_in=m_cf_w_in, m_cf_dw_w=m_cf_dw_w, m_cf_dw_b=m_cf_dw_b, m_cf_ln_g=m_cf_ln_g, m_cf_ln_b=m_cf_ln_b, m_cf_w_out=m_cf_w_out, m_ff_w1=m_ff_w1, m_ff_w2=m_ff_w2, v_ada_w=v_ada_w, v_ada_b=v_ada_b, v_ln_g=v_ln_g, v_ln_b=v_ln_b, v_dn_w_in=v_dn_w_in, v_dn_conv_w=v_dn_conv_w, v_dn_a_log=v_dn_a_log, v_dn_dt_bias=v_dn_dt_bias, v_dn_norm_w=v_dn_norm_w, v_dn_w_out=v_dn_w_out, v_cf_w_in=v_cf_w_in, v_cf_dw_w=v_cf_dw_w, v_cf_dw_b=v_cf_dw_b, v_cf_ln_g=v_cf_ln_g, v_cf_ln_b=v_cf_ln_b, v_cf_w_out=v_cf_w_out, v_ff_w1=v_ff_w1, v_ff_w2=v_ff_w2)
    weights = {n: given[n] for n in TWIN_WEIGHTS}
    shared = {n: given[n] for n in SHARED_INPUTS}
    per_example = {n: given[n] for n in ['x', 'c']}
    grad_fn = _jax.value_and_grad(_loss, argnums=(0, 1))

    def one_microbatch(ex, loss_target):
        ex = dict(ex)
        diff = ex.pop(TWIN_DIFF_INPUT)
        return grad_fn(weights, diff, {**shared, **ex}, loss_target)

    if N_MICROBATCH == 1:
        loss, (grad_w, grad_x) = one_microbatch(per_example, given["loss_target"])
    else:
        def body(carry, xs):
            loss_sum, grad_sum = carry
            l_k, (gw_k, gx_k) = one_microbatch(xs[0], xs[1])
            with _jax.named_scope("update"):
                return (loss_sum + l_k, _jax.tree.map(_jnp.add, grad_sum, gw_k)), gx_k

        init = (_jnp.zeros((), _jnp.float32), _jax.tree.map(_jnp.zeros_like, weights))
        (loss, grad_w), grad_x = _jax.lax.scan(body, init, (per_example, given["loss_target"]))
    with _jax.named_scope("update"):
        delta_w, new_m, new_v = {}, {}, {}
        for n in TWIN_WEIGHTS:
            delta_w[n], new_m[n], new_v[n] = _adamw(weights[n], grad_w[n], given["m_" + n], given["v_" + n])
    return (loss, grad_x, *[grad_w[n] for n in TWIN_WEIGHTS], *[delta_w[n] for n in TWIN_WEIGHTS],
            *[new_m[n] for n in TWIN_WEIGHTS], *[new_v[n] for n in TWIN_WEIGHTS])
```

```python
import functools

import jax
import jax.numpy as jnp
from jax import lax
from jax.experimental import pallas as pl
from jax.experimental.pallas import tpu as pltpu

F32 = jnp.float32
_MXU = jnp.bfloat16
_HI = lax.Precision.HIGHEST

N_LAYERS = 4
ALPHA = (2.0 * N_LAYERS) ** 0.25
LN_EPS = 1e-5
RMS_EPS = 1e-6
L2_EPS = 1e-6
CHUNK = 64
ADAM_LR, ADAM_B1, ADAM_B2, ADAM_EPS, ADAM_WD, ADAM_STEP = 0.001, 0.9, 0.999, 1e-08, 0.01, 10

LANES = 128
TOKEN_BLOCK = 256
VMEM_BIG = 48 * 1024 * 1024

SDS = jax.ShapeDtypeStruct
MESH = pl.DeviceIdType.MESH


def _cparams(vmem=None):
    if vmem is None:
        return None
    return pltpu.CompilerParams(vmem_limit_bytes=vmem)


def _pcall(body, **kw):
    if kw.get("compiler_params", 1) is None:
        kw.pop("compiler_params")
    return pl.pallas_call(body, **kw)


def _full(arr):
    nd = arr.ndim
    return pl.BlockSpec(arr.shape, lambda *g: (0,) * nd)


def _bs(block, imap, lead=None):
    if lead is None:
        return pl.BlockSpec(block, imap)
    return pl.BlockSpec((None,) + tuple(block), lambda *g: (lead,) + tuple(imap(*g)))


def _split(a):
    return a if isinstance(a, tuple) else (a, None)


_GROUPS = {
    "xy": ([(1, 0, 0), (0, 1, 0), (1, 1, 0)], 4),
    "c": ([(0, 0, 1)], 2),
    "all": ([(1, 0, 0), (0, 1, 0), (1, 1, 0), (0, 0, 1), (1, 0, 1), (0, 1, 1), (1, 1, 1)], 8),
}


def _exchange(items, group, mode, name):
    masks, n = _GROUPS[group]
    npeer = len(masks)
    ni = len(items)
    arrs = [a for a, _ in items]
    out_shapes = []
    for a, ax in items:
        shp = list(a.shape)
        if mode == "gather":
            shp[ax] *= n
        else:
            shp[ax] //= n
            shp = [n] + shp
        out_shapes.append(SDS(tuple(shp), a.dtype))

    def body(*refs):
        ins, outs = refs[:ni], refs[ni:2 * ni]
        send_sems, recv_sems, local_sems = refs[2 * ni:]
        x, y, c = lax.axis_index("x"), lax.axis_index("y"), lax.axis_index("c")

        def slot(px, py, pc):
            if group == "xy":
                return 2 * px + py
            if group == "c":
                return pc
            return 4 * px + 2 * py + pc

        me = slot(x, y, c)

        def block(ref, ax, idx, size):
            ix = (slice(None),) * ax + (pl.ds(pl.multiple_of(idx * size, size), size),)
            return ref.at[ix]

        copies = []
        for it, (a, ax) in enumerate(items):
            in_ref, out_ref = ins[it], outs[it]
            if mode == "gather":
                size = a.shape[ax]
                src_own, dst_own = in_ref, block(out_ref, ax, me, size)
            else:
                size = a.shape[ax] // n
                src_own, dst_own = block(in_ref, ax, me, size), out_ref.at[me]
            own = pltpu.make_async_copy(src_own, dst_own, local_sems.at[it])
            own.start()
            copies.append(own)
            for k, m in enumerate(masks):
                peer = tuple((1 - v) if b else v for v, b in zip((x, y, c), m))
                if mode == "gather":
                    src, dst = in_ref, dst_own
                else:
                    src, dst = block(in_ref, ax, slot(*peer), size), out_ref.at[me]
                cp = pltpu.make_async_remote_copy(
                    src_ref=src, dst_ref=dst, send_sem=send_sems.at[it * npeer + k],
                    recv_sem=recv_sems.at[it * npeer + k], device_id=peer, device_id_type=MESH)
                cp.start()
                copies.append(cp)
        for cp in copies:
            cp.wait()

    any_spec = pl.BlockSpec(memory_space=pl.ANY)
    outs = _pcall(
        body, name=name, out_shape=tuple(out_shapes),
        in_specs=[any_spec] * ni, out_specs=tuple([any_spec] * ni),
        scratch_shapes=[pltpu.SemaphoreType.DMA((ni * npeer,)), pltpu.SemaphoreType.DMA((ni * npeer,)),
                        pltpu.SemaphoreType.DMA((ni,))],
    )(*arrs)
    return list(outs)


def _mm(a, b, mode, *, name, out_dtypes=(F32,), tm=512, tn=512, tk=512, a_fn=None, out_fn=None, aux=(),
        out_imap=None, out_shape=None):
    (a, a_lead), (b, b_lead) = _split(a), _split(b)
    ash, bsh = a.shape[-2:], b.shape[-2:]
    if mode == "nn":
        (M, K), (_, N) = ash, bsh
    elif mode == "nt":
        (M, K), (N, _) = ash, bsh
    else:
        (K, M), (_, N) = ash, bsh
    tm, tn, tk = min(tm, M), min(tn, N), min(tk, K)
    assert M % tm == 0 and N % tn == 0 and K % tk == 0, (name, M, N, K)
    nk = K // tk
    if mode == "tn":
        a_spec = _bs((tk, tm), lambda i, j, k: (k, i), a_lead)
    else:
        a_spec = _bs((tm, tk), lambda i, j, k: (i, k), a_lead)
    if mode == "nt":
        b_spec = _bs((tn, tk), lambda i, j, k: (j, k), b_lead)
    else:
        b_spec = _bs((tk, tn), lambda i, j, k: (k, j), b_lead)
    aux_arrs, aux_specs = [], []
    for arr, kind in aux:
        arr, lead = _split(arr)
        aux_arrs.append(arr)
        if kind == "mn":
            aux_specs.append(_bs((tm, tn), lambda i, j, k: (i, j), lead))
        else:
            aux_specs.append(_bs((1, tn), lambda i, j, k: (0, j), lead))
    na, no = len(aux_arrs), len(out_dtypes)
    dims = {"nn": (((1,), (0,)), ((), ())), "nt": (((1,), (1,)), ((), ())), "tn": (((0,), (0,)), ((), ()))}[mode]

    def body(a_ref, b_ref, *rest):
        aux_refs, o_refs, acc = rest[:na], rest[na:na + no], rest[na + no]
        k = pl.program_id(2)

        @pl.when(k == 0)
        def _():
            acc[...] = jnp.zeros_like(acc)

        av = a_ref[...]
        if a_fn is not None:
            av = a_fn(av.astype(F32))
        acc[...] += lax.dot_general(av.astype(_MXU), b_ref[...].astype(_MXU), dims, preferred_element_type=F32)

        @pl.when(k == nk - 1)
        def _():
            r = acc[...]
            outs = out_fn(r, *[x[...] for x in aux_refs]) if out_fn is not None else (r,)
            for o_ref, val in zip(o_refs, outs):
                o_ref[...] = val.astype(o_ref.dtype)

    if out_shape is None:
        out_shape = (M, N)
    if out_imap is None:
        o_spec = pl.BlockSpec((tm, tn), lambda i, j, k: (i, j))
    else:
        o_spec = pl.BlockSpec((None,) * (len(out_shape) - 2) + (tm, tn), out_imap)
    outs = _pcall(
        body, name=name, grid=(M // tm, N // tn, nk),
        in_specs=[a_spec, b_spec] + aux_specs, out_specs=tuple([o_spec] * no),
        out_shape=tuple(SDS(out_shape, dt) for dt in out_dtypes),
        scratch_shapes=[pltpu.VMEM((tm, tn), F32)],
        compiler_params=pltpu.CompilerParams(dimension_semantics=("parallel", "parallel", "arbitrary")),
    )(a, b, *aux_arrs)
    return outs[0] if no == 1 else outs


def _tok(S):
    ts = min(TOKEN_BLOCK, S)
    assert S % ts == 0
    return ts


def _row(ts, D):
    return pl.BlockSpec((ts, D), lambda i: (i, 0))


def _acc_rows(ref, i, rows):
    @pl.when(i == 0)
    def _():
        for r, v in enumerate(rows):
            ref[r:r + 1, :] = v

    @pl.when(i != 0)
    def _():
        for r, v in enumerate(rows):
            ref[r:r + 1, :] += v


def _modulate(x, mod, r_sh, r_sc, name):
    S, D = x.shape
    ts = _tok(S)

    def body(x_ref, m_ref, o_ref):
        o_ref[...] = (x_ref[...] * (1.0 + m_ref[r_sc:r_sc + 1, :]) + m_ref[r_sh:r_sh + 1, :]).astype(o_ref.dtype)

    return _pcall(body, name=name, grid=(S // ts,), in_specs=[_row(ts, D), _full(mod)],
                  out_specs=_row(ts, D), out_shape=SDS((S, D), _MXU))(x, mod)


def _modulate_bwd(x, mod, r_sc, dh, dxa, name):
    S, D = x.shape
    ts = _tok(S)

    def body(x_ref, m_ref, dh_ref, dxa_ref, dx_ref, dss_ref):
        dh_v = dh_ref[...]
        dx_ref[...] = dxa_ref[...] + dh_v * (1.0 + m_ref[r_sc:r_sc + 1, :])
        _acc_rows(dss_ref, pl.program_id(0),
                  [jnp.sum(dh_v, axis=0, keepdims=True), jnp.sum(dh_v * x_ref[...], axis=0, keepdims=True)])

    return _pcall(body, name=name, grid=(S // ts,),
                  in_specs=[_row(ts, D), _full(mod), _row(ts, D), _row(ts, D)],
                  out_specs=(_row(ts, D), pl.BlockSpec((2, D), lambda i: (0, 0))),
                  out_shape=(SDS((S, D), F32), SDS((2, D), F32)))(x, mod, dh, dxa)


def _norm_stats(z):
    mu = jnp.mean(z, axis=-1, keepdims=True)
    zc = z - mu
    var = jnp.mean(zc * zc, axis=-1, keepdims=True)
    rstd = lax.rsqrt(var + LN_EPS)
    return zc * rstd, rstd


def _norm_bwd(dxhat, xhat, rstd):
    return rstd * (dxhat - jnp.mean(dxhat, axis=-1, keepdims=True)
                   - xhat * jnp.mean(dxhat * xhat, axis=-1, keepdims=True))


def _combine(x, y, mod, r_gt, lnp, r_g, name):
    S, D = x.shape
    ts = _tok(S)

    def body(x_ref, y_ref, m_ref, l_ref, o_ref):
        z = ALPHA * x_ref[...] + (1.0 + m_ref[r_gt:r_gt + 1, :]) * y_ref[...]
        xhat, _ = _norm_stats(z)
        o_ref[...] = xhat * l_ref[r_g:r_g + 1, :] + l_ref[r_g + 1:r_g + 2, :]

    return _pcall(body, name=name, grid=(S // ts,), in_specs=[_row(ts, D), _row(ts, D), _full(mod), _full(lnp)],
                  out_specs=_row(ts, D), out_shape=SDS((S, D), F32))(x, y, mod, lnp)


def _combine_bwd(x, y, mod, r_gt, lnp, r_g, dout, name):
    S, D = x.shape
    ts = _tok(S)

    def body(x_ref, y_ref, m_ref, l_ref, do_ref, dxa_ref, dy_ref, dp_ref):
        gate = 1.0 + m_ref[r_gt:r_gt + 1, :]
        y_v, do_v = y_ref[...], do_ref[...]
        xhat, rstd = _norm_stats(ALPHA * x_ref[...] + gate * y_v)
        dz = _norm_bwd(do_v * l_ref[r_g:r_g + 1, :], xhat, rstd)
        dxa_ref[...] = ALPHA * dz
        dy_ref[...] = (gate * dz).astype(dy_ref.dtype)
        _acc_rows(dp_ref, pl.program_id(0),
                  [jnp.sum(dz * y_v, axis=0, keepdims=True), jnp.sum(do_v * xhat, axis=0, keepdims=True),
                   jnp.sum(do_v, axis=0, keepdims=True)])

    return _pcall(body, name=name, grid=(S // ts,),
                  in_specs=[_row(ts, D), _row(ts, D), _full(mod), _full(lnp), _row(ts, D)],
                  out_specs=(_row(ts, D), _row(ts, D), pl.BlockSpec((3, D), lambda i: (0, 0))),
                  out_shape=(SDS((S, D), F32), SDS((S, D), _MXU), SDS((3, D), F32)))(x, y, mod, lnp, dout)


def _sigmoid(t):
    return 1.0 / (1.0 + jnp.exp(-t))


def _ln_silu(u, lnp, name):
    S, D = u.shape
    ts = _tok(S)

    def body(u_ref, l_ref, o_ref):
        xhat, _ = _norm_stats(u_ref[...])
        t = xhat * l_ref[0:1, :] + l_ref[1:2, :]
        o_ref[...] = (t * _sigmoid(t)).astype(o_ref.dtype)

    return _pcall(body, name=name, grid=(S // ts,), in_specs=[_row(ts, D), _full(lnp)],
                  out_specs=_row(ts, D), out_shape=SDS((S, D), _MXU))(u, lnp)


def _ln_silu_bwd(u, lnp, dout, name):
    S, D = u.shape
    ts = _tok(S)

    def body(u_ref, l_ref, do_ref, du_ref, dp_ref):
        xhat, rstd = _norm_stats(u_ref[...])
        g = l_ref[0:1, :]
        t = xhat * g + l_ref[1:2, :]
        sg = _sigmoid(t)
        dt = do_ref[...] * (sg * (1.0 + t * (1.0 - sg)))
        du_ref[...] = _norm_bwd(dt * g, xhat, rstd)
        _acc_rows(dp_ref, pl.program_id(0),
                  [jnp.sum(dt * xhat, axis=0, keepdims=True), jnp.sum(dt, axis=0, keepdims=True)])

    return _pcall(body, name=name, grid=(S // ts,), in_specs=[_row(ts, D), _full(lnp), _row(ts, D)],
                  out_specs=(_row(ts, D), pl.BlockSpec((2, D), lambda i: (0, 0))),
                  out_shape=(SDS((S, D), F32), SDS((2, D), F32)))(u, lnp, dout)


def _loss_head(xf, tgt, name):
    S, D = xf.shape
    ts = _tok(S)

    def body(x_ref, t_ref, dx_ref, l_ref):
        err = x_ref[...] - t_ref[...]
        dx_ref[...] = err * (1.0 / D)
        part = jnp.sum(jnp.sum(err * err, axis=1, keepdims=True), axis=0, keepdims=True) * (0.5 / D)

        @pl.when(pl.program_id(0) == 0)
        def _():
            l_ref[...] = part

        @pl.when(pl.program_id(0) != 0)
        def _():
            l_ref[...] += part

    return _pcall(body, name=name, grid=(S // ts,), in_specs=[_row(ts, D), _row(ts, D)],
                  out_specs=(_row(ts, D), pl.BlockSpec((1, 1), lambda i: (0, 0))),
                  out_shape=(SDS((S, D), F32), SDS((1, 1), F32)))(xf, tgt)


def _shift_down(u, s, rows):
    if s == 0:
        return u
    return jnp.where(rows >= s, pltpu.roll(u, s, 0), 0.0)


def _shift_up(u, s, rows):
    if s == 0:
        return u
    n = u.shape[0]
    return jnp.where(rows < n - s, pltpu.roll(u, n - s, 0), 0.0)


def _dwconv(u, w_ref, taps, rows):
    acc = jnp.zeros_like(u)
    for j in range(taps):
        acc = acc + w_ref[j:j + 1, :] * _shift_down(u, taps - 1 - j, rows)
    return acc


def _dwconv_bwd(u, dy, w_ref, dw_ref, taps, rows):
    du = jnp.zeros_like(u)
    for j in range(taps):
        s = taps - 1 - j
        du = du + w_ref[j:j + 1, :] * _shift_up(dy, s, rows)
        dw_ref[j:j + 1, :] = jnp.sum(dy * _shift_down(u, s, rows), axis=0, keepdims=True)
    return du


def _col(S, j0=0):
    return pl.BlockSpec((S, LANES), lambda j: (0, j + j0))


def _conv_silu(pm, w, nblk, name):
    S = pm.shape[0]
    taps = w.shape[0]

    def body(u_ref, w_ref, o_ref):
        rows = lax.broadcasted_iota(jnp.int32, (S, LANES), 0)
        cv = _dwconv(u_ref[...], w_ref, taps, rows)
        o_ref[...] = cv * _sigmoid(cv)

    return _pcall(body, name=name, grid=(nblk,),
                  in_specs=[_col(S), pl.BlockSpec((taps, LANES), lambda j: (0, j))],
                  out_specs=_col(S), out_shape=SDS((S, nblk * LANES), F32),
                  compiler_params=_cparams(VMEM_BIG))(pm, w)


def _conv_silu_bwd(pm, w, dout, dpm, j0, name):
    S = pm.shape[0]
    taps = w.shape[0]
    nblk = dout.shape[1] // LANES

    def body(u_ref, w_ref, do_ref, dpm_in, du_ref, dw_ref):
        del dpm_in
        rows = lax.broadcasted_iota(jnp.int32, (S, LANES), 0)
        u = u_ref[...]
        cv = _dwconv(u, w_ref, taps, rows)
        sg = _sigmoid(cv)
        dc = do_ref[...] * (sg * (1.0 + cv * (1.0 - sg)))
        du_ref[...] = _dwconv_bwd(u, dc, w_ref, dw_ref, taps, rows)

    return _pcall(body, name=name, grid=(nblk,),
                  in_specs=[_col(S, j0), pl.BlockSpec((taps, LANES), lambda j: (0, j + j0)), _col(S),
                            pl.BlockSpec(memory_space=pl.ANY)],
                  out_specs=(_col(S, j0), pl.BlockSpec((taps, LANES), lambda j: (0, j))),
                  out_shape=(SDS(dpm.shape, F32), SDS((taps, nblk * LANES), F32)),
                  input_output_aliases={3: 0},
                  compiler_params=_cparams(VMEM_BIG))(pm, w, dout, dpm)


def _glu_conv(p, w, bias, name):
    S, C2 = p.shape
    nblk = C2 // 2 // LANES
    taps = w.shape[0]

    def body(v_ref, g_ref, w_ref, b_ref, o_ref):
        rows = lax.broadcasted_iota(jnp.int32, (S, LANES), 0)
        u = v_ref[...] * _sigmoid(g_ref[...])
        o_ref[...] = _dwconv(u, w_ref, taps, rows) + b_ref[...]

    return _pcall(body, name=name, grid=(nblk,),
                  in_specs=[_col(S), _col(S, nblk), pl.BlockSpec((taps, LANES), lambda j: (0, j)),
                            pl.BlockSpec((1, LANES), lambda j: (0, j))],
                  out_specs=_col(S), out_shape=SDS((S, nblk * LANES), F32),
                  compiler_params=_cparams(VMEM_BIG))(p, p, w, bias)


def _glu_conv_bwd(p, w, dout, name):
    S, C2 = p.shape
    nblk = C2 // 2 // LANES
    taps = w.shape[0]

    def body(v_ref, g_ref, w_ref, do_ref, dv_ref, dg_ref, dw_ref, db_ref):
        rows = lax.broadcasted_iota(jnp.int32, (S, LANES), 0)
        val, sg = v_ref[...], _sigmoid(g_ref[...])
        do_v = do_ref[...]
        du = _dwconv_bwd(val * sg, do_v, w_ref, dw_ref, taps, rows)
        dv_ref[...] = du * sg
        dg_ref[...] = du * val * sg * (1.0 - sg)
        db_ref[...] = jnp.sum(do_v, axis=0, keepdims=True)

    dval, dgate, dw, db = _pcall(
        body, name=name, grid=(nblk,),
        in_specs=[_col(S), _col(S, nblk), pl.BlockSpec((taps, LANES), lambda j: (0, j)), _col(S)],
        out_specs=(_col(S), _col(S), pl.BlockSpec((taps, LANES), lambda j: (0, j)),
                   pl.BlockSpec((1, LANES), lambda j: (0, j))),
        out_shape=(SDS((S, C2 // 2), F32), SDS((S, C2 // 2), F32), SDS((taps, C2 // 2), F32), SDS((1, C2 // 2), F32)),
        compiler_params=_cparams(VMEM_BIG))(p, p, w, dout)
    return dval, dgate, dw, db


def _log1p(e):
    u = 1.0 + e
    d = jnp.where(u == 1.0, 1.0, u - 1.0)
    return jnp.where(u == 1.0, e, jnp.log(u) * (e / d))


def _gate_parts(ps, prm, H):
    lane = lax.broadcasted_iota(jnp.int32, ps.shape, 1)
    is_b, is_g = lane < H, (lane >= H) & (lane < 2 * H)
    beta = _sigmoid(ps)
    t = ps + prm[1:2, :]
    sp = jnp.maximum(t, 0.0) + _log1p(jnp.exp(-jnp.abs(t)))
    na = -jnp.exp(prm[0:1, :])
    return is_b, is_g, beta, t, sp, na


def _gates(ps, prm, H, name):
    S = ps.shape[0]
    ts = _tok(S)

    def body(p_ref, r_ref, o_ref):
        is_b, is_g, beta, _, sp, na = _gate_parts(p_ref[...], r_ref[...], H)
        o_ref[...] = jnp.where(is_b, beta, jnp.where(is_g, na * sp, 0.0))

    return _pcall(body, name=name, grid=(S // ts,), in_specs=[_row(ts, LANES), _full(prm)],
                  out_specs=_row(ts, LANES), out_shape=SDS((S, LANES), F32))(ps, prm)


def _gates_bwd(ps, prm, dgates, H, name):
    S = ps.shape[0]
    ts = _tok(S)

    def body(p_ref, r_ref, dg_ref, dp_ref, dr_ref):
        is_b, is_g, beta, t, sp, na = _gate_parts(p_ref[...], r_ref[...], H)
        dg_v = dg_ref[...]
        dsp = jnp.where(is_g, dg_v * na * _sigmoid(t), 0.0)
        dp_ref[...] = jnp.where(is_b, dg_v * beta * (1.0 - beta), dsp)
        _acc_rows(dr_ref, pl.program_id(0),
                  [jnp.sum(jnp.where(is_g, dg_v * na * sp, 0.0), axis=0, keepdims=True),
                   jnp.sum(dsp, axis=0, keepdims=True)])

    return _pcall(body, name=name, grid=(S // ts,), in_specs=[_row(ts, LANES), _full(prm), _row(ts, LANES)],
                  out_specs=(_row(ts, LANES), pl.BlockSpec((2, LANES), lambda i: (0, 0))),
                  out_shape=(SDS((S, LANES), F32), SDS((2, LANES), F32)))(ps, prm, dgates)


def _dot(a, b, dims=(((1,), (0,)), ((), ()))):
    return lax.dot_general(a, b, dims, precision=_HI, preferred_element_type=F32)


_NT = (((1,), (1,)), ((), ()))
_TN = (((0,), (0,)), ((), ()))


def _delta_chunk(qr, kr, v, z, gates, nw, s_in, h, H):
    C, dk = qr.shape
    lane = lax.broadcasted_iota(jnp.int32, gates.shape, 1)
    beta = jnp.sum(jnp.where(lane == h, gates, 0.0), axis=-1, keepdims=True)
    g = jnp.sum(jnp.where(lane == h + H, gates, 0.0), axis=-1, keepdims=True)
    q = qr * lax.rsqrt(jnp.sum(qr * qr, axis=-1, keepdims=True) + L2_EPS) * (dk ** -0.5)
    k = kr * lax.rsqrt(jnp.sum(kr * kr, axis=-1, keepdims=True) + L2_EPS)
    ri = lax.broadcasted_iota(jnp.int32, (C, C), 0)
    ci = lax.broadcasted_iota(jnp.int32, (C, C), 1)
    causal, strict, eye = ri >= ci, ri > ci, ri == ci
    gam_row = jnp.sum(jnp.where(ri <= ci, g, 0.0), axis=0, keepdims=True)
    gam_col = jnp.sum(jnp.where(eye, gam_row, 0.0), axis=-1, keepdims=True)
    g_last = jnp.sum(g, axis=0, keepdims=True)
    decay = jnp.where(causal, jnp.exp(jnp.where(causal, gam_col - gam_row, 0.0)), 0.0)
    kb = k * beta
    a = jnp.where(strict, _dot(kb, k, _NT) * decay, 0.0)
    t_inv = jnp.where(eye, 1.0, 0.0) - a
    p = a
    for _ in range(max(C.bit_length() - 2, 0)):
        p = _dot(p, p)
        t_inv = t_inv + _dot(t_inv, p)
    eg = jnp.exp(gam_col)
    u = _dot(t_inv, v * beta)
    w = _dot(t_inv, kb * eg)
    a_qk = _dot(q, k, _NT) * decay
    v_new = u - _dot(w, s_in)
    o = _dot(q * eg, s_in) + _dot(a_qk, v_new)
    s_out = s_in * jnp.exp(g_last) + _dot(k * jnp.exp(g_last - gam_col), v_new, _TN)
    og = o * lax.rsqrt(jnp.mean(o * o, axis=-1, keepdims=True) + RMS_EPS) * nw * (z * _sigmoid(z))
    return og, s_out


def _delta_fwd(qkv, pm, gates, nw, H, name):
    S = qkv.shape[0]
    dk = qkv.shape[1] // (3 * H)
    N = S // CHUNK
    blk = lambda off: pl.BlockSpec((CHUNK, dk), lambda n, h: (n, h + off))

    def body(q_ref, k_ref, v_ref, z_ref, g_ref, nw_ref, og_ref, st_ref, s_scr):
        n, h = pl.program_id(0), pl.program_id(1)

        @pl.when(n == 0)
        def _():
            s_scr[h] = jnp.zeros((dk, dk), F32)

        s_in = s_scr[h]
        st_ref[...] = s_in
        og, s_out = _delta_chunk(q_ref[...], k_ref[...], v_ref[...], z_ref[...], g_ref[...], nw_ref[...],
                                 s_in, h, H)
        og_ref[...] = og.astype(og_ref.dtype)
        s_scr[h] = s_out

    return _pcall(
        body, name=name, grid=(N, H),
        in_specs=[blk(0), blk(H), blk(2 * H), blk(3 * H), pl.BlockSpec((CHUNK, LANES), lambda n, h: (n, 0)),
                  _full(nw)],
        out_specs=(blk(0), pl.BlockSpec((None, None, dk, dk), lambda n, h: (n, h, 0, 0))),
        out_shape=(SDS((S, H * dk), _MXU), SDS((N, H, dk, dk), F32)),
        scratch_shapes=[pltpu.VMEM((H, dk, dk), F32)],
    )(qkv, qkv, qkv, pm, gates, nw)


def _delta_bwd(qkv, pm, gates, nw, states, dog, H, name):
    S = qkv.shape[0]
    dk = qkv.shape[1] // (3 * H)
    N = S // CHUNK
    blk = lambda off: pl.BlockSpec((CHUNK, dk), lambda n, h: (N - 1 - n, h + off))
    gspec = pl.BlockSpec((CHUNK, LANES), lambda n, h: (N - 1 - n, 0))

    def body(q_ref, k_ref, v_ref, z_ref, g_ref, nw_ref, st_ref, do_ref,
             dq_ref, dk_ref, dv_ref, dz_ref, dg_ref, dnw_ref, ds_scr):
        n, h = pl.program_id(0), pl.program_id(1)

        @pl.when(n == 0)
        def _():
            ds_scr[h] = jnp.zeros((dk, dk), F32)

        fn = functools.partial(_delta_chunk, h=h, H=H)
        _, vjp = jax.vjp(fn, q_ref[...], k_ref[...], v_ref[...], z_ref[...], g_ref[...], nw_ref[...], st_ref[...])
        dq, dkk, dv, dz, dg, dnw, ds_in = vjp((do_ref[...].astype(F32), ds_scr[h]))
        dq_ref[...], dk_ref[...], dv_ref[...], dz_ref[...] = dq, dkk, dv, dz
        ds_scr[h] = ds_in

        @pl.when(h == 0)
        def _():
            dg_ref[...] = dg

        @pl.when(h != 0)
        def _():
            dg_ref[...] += dg

        first = (n == 0) & (h == 0)

        @pl.when(first)
        def _():
            dnw_ref[...] = dnw

        @pl.when(jnp.logical_not(first))
        def _():
            dnw_ref[...] += dnw

    return _pcall(
        body, name=name, grid=(N, H),
        in_specs=[blk(0), blk(H), blk(2 * H), blk(3 * H), gspec, _full(nw),
                  pl.BlockSpec((None, None, dk, dk), lambda n, h: (N - 1 - n, h, 0, 0)), blk(0)],
        out_specs=(blk(0), blk(0), blk(0), blk(3 * H), gspec, pl.BlockSpec((1, dk), lambda n, h: (0, 0))),
        out_shape=(SDS((S, H * dk), F32), SDS((S, H * dk), F32), SDS((S, H * dk), F32), SDS(pm.shape, F32),
                   SDS((S, LANES), F32), SDS((1, dk), F32)),
        scratch_shapes=[pltpu.VMEM((H, dk, dk), F32)],
        compiler_params=_cparams(VMEM_BIG),
    )(qkv, qkv, qkv, pm, gates, nw, states, dog)


def _rows_block(R, C):
    rb = R
    while rb * C * 4 > (1 << 20) and rb % 16 == 0:
        rb //= 2
    return rb


def _sum_slots(st, name):
    n, R, C = st.shape
    rb = _rows_block(R, C)

    def body(s_ref, o_ref):
        acc = s_ref[0]
        for q in range(1, n):
            acc = acc + s_ref[q]
        o_ref[...] = acc

    return _pcall(body, name=name, grid=(R // rb,), in_specs=[pl.BlockSpec((n, rb, C), lambda i: (0, i, 0))],
                  out_specs=pl.BlockSpec((rb, C), lambda i: (i, 0)), out_shape=SDS((R, C), F32))(st)


def _adamw(w, m, v, st, name):
    R, C = w.shape
    n = st.shape[0]
    rb = _rows_block(R, C)
    spec = pl.BlockSpec((rb, C), lambda i: (i, 0))

    def body(w_ref, m_ref, v_ref, s_ref, g_ref, d_ref, mo_ref, vo_ref):
        g = s_ref[0]
        for q in range(1, n):
            g = g + s_ref[q]
        m_new = ADAM_B1 * m_ref[...] + (1.0 - ADAM_B1) * g
        v_new = ADAM_B2 * v_ref[...] + (1.0 - ADAM_B2) * (g * g)
        m_hat = m_new / (1.0 - ADAM_B1 ** ADAM_STEP)
        v_hat = v_new / (1.0 - ADAM_B2 ** ADAM_STEP)
        g_ref[...] = g
        d_ref[...] = -ADAM_LR * (m_hat / (jnp.sqrt(v_hat) + ADAM_EPS) + ADAM_WD * w_ref[...])
        mo_ref[...] = m_new
        vo_ref[...] = v_new

    return _pcall(body, name=name, grid=(R // rb,),
                  in_specs=[spec, spec, spec, pl.BlockSpec((n, rb, C), lambda i: (0, i, 0))],
                  out_specs=(spec,) * 4, out_shape=(SDS((R, C), F32),) * 4)(w, m, v, st)


def _pack(arrs, rows=1):
    flat = jnp.concatenate([a.reshape(-1).astype(F32) for a in arrs])
    quantum = rows * LANES
    pad = (-flat.shape[0]) % quantum
    flat = jnp.pad(flat, (0, pad))
    return flat.reshape(rows, -1)


def _unpack(flat, shapes):
    flat = flat.reshape(-1)
    out, off = [], 0
    for shp in shapes:
        size = 1
        for d in shp:
            size *= d
        out.append(flat[off:off + size].reshape(shp))
        off += size
    return out


def _mlp_fwd(x1, mod, lnp, w1, w2, tag):
    h2 = _modulate(x1, mod, 3, 4, f"{tag}_mod")
    a1, a2 = _mm(h2, w1, "nn", name=f"{tag}_up", out_dtypes=(_MXU, _MXU),
                 out_fn=lambda r: (r, jnp.square(jnp.maximum(r, 0.0))))
    y2 = _mm(a2, w2, "nn", name=f"{tag}_down")
    x2 = _combine(x1, y2, mod, 5, lnp, 2, f"{tag}_ln")
    return x2, (x1, h2, a1, a2, y2)


def _mlp_bwd(dx2, saved, mod, lnp, w1, w2, tag):
    x1, h2, a1, a2, y2 = saved
    dxa, dy2, dp = _combine_bwd(x1, y2, mod, 5, lnp, 2, dx2, f"{tag}_ln_b")
    da1 = _mm(dy2, w2, "nt", name=f"{tag}_down_bx", out_dtypes=(_MXU,), aux=[(a1, "mn")],
              out_fn=lambda r, a: (r * (2.0 * jnp.maximum(a.astype(F32), 0.0)),))
    dw2 = _mm(a2, dy2, "tn", name=f"{tag}_down_bw")
    dw1 = _mm(h2, da1, "tn", name=f"{tag}_up_bw")
    dh2 = _mm(da1, w1, "nt", name=f"{tag}_up_bx")
    dx1, dss = _modulate_bwd(x1, mod, 4, dh2, dxa, f"{tag}_mod_b")
    return dx1, dw1, dw2, (dss, dp)


def _dn_fwd(x, mod, lnp, wts, H, tag):
    w_main, w_small, conv_w, prm, nw, w_out = wts
    h = _modulate(x, mod, 0, 1, f"{tag}_mod")
    pm = _mm(h, w_main, "nn", name=f"{tag}_in")
    ps = _mm(h, w_small, "nn", name=f"{tag}_in_s")
    nqkv = conv_w.shape[1] // LANES
    qkv = _conv_silu(pm, conv_w, nqkv, f"{tag}_conv")
    gates = _gates(ps, prm, H, f"{tag}_gates")
    og, states = _delta_fwd(qkv, pm, gates, nw, H, f"{tag}_delta")
    y = _mm(og, w_out, "nn", name=f"{tag}_out")
    x1 = _combine(x, y, mod, 2, lnp, 0, f"{tag}_ln")
    return x1, (x, h, pm, ps, qkv, gates, states, og, y)


def _dn_bwd(dx1, saved, mod, lnp, wts, H, tag):
    w_main, w_small, conv_w, prm, nw, w_out = wts
    x, h, pm, ps, qkv, gates, states, og, y = saved
    dxa, dy, dp = _combine_bwd(x, y, mod, 2, lnp, 0, dx1, f"{tag}_ln_b")
    dog = _mm(dy, w_out, "nt", name=f"{tag}_out_bx")
    dw_out = _mm(og, dy, "tn", name=f"{tag}_out_bw")
    dq, dk, dv, dpm, dgates, dnw = _delta_bwd(qkv, pm, gates, nw, states, dog, H, f"{tag}_delta_b")
    dps, dprm = _gates_bwd(ps, prm, dgates, H, f"{tag}_gates_b")
    dcw = []
    nb = dq.shape[1] // LANES
    for part, dpart in enumerate((dq, dk, dv)):
        dpm, dcw_p = _conv_silu_bwd(pm, conv_w, dpart, dpm, part * nb, f"{tag}_conv_b{part}")
        dcw.append(dcw_p)
    dconv_w = jnp.concatenate(dcw, axis=1)
    dw_main = _mm(h, dpm, "tn", name=f"{tag}_in_bw")
    dw_small = _mm(h, dps, "tn", name=f"{tag}_in_s_bw")
    dh_s = _mm(dps, w_small, "nt", name=f"{tag}_in_s_bx")
    dh = _mm(dpm, w_main, "nt", name=f"{tag}_in_bx", aux=[(dh_s, "mn")], out_fn=lambda r, e: (r + e,))
    dx, dss = _modulate_bwd(x, mod, 1, dh, dxa, f"{tag}_mod_b")
    return dx, (dw_main, dw_small, dconv_w, dprm, dnw, dw_out), (dss, dp)


def _cf_fwd(x, mod, lnp, wts, tag):
    w_in, dw_w, dw_b, cln, w_out = wts
    h = _modulate(x, mod, 0, 1, f"{tag}_mod")
    p = _mm(h, w_in, "nn", name=f"{tag}_in")
    u2 = _glu_conv(p, dw_w, dw_b, f"{tag}_conv")
    u3 = _ln_silu(u2, cln, f"{tag}_cln")
    y = _mm(u3, w_out, "nn", name=f"{tag}_out")
    x1 = _combine(x, y, mod, 2, lnp, 0, f"{tag}_ln")
    return x1, (x, h, p, u2, u3, y)


def _cf_bwd(dx1, saved, mod, lnp, wts, tag):
    w_in, dw_w, dw_b, cln, w_out = wts
    x, h, p, u2, u3, y = saved
    dxa, dy, dp = _combine_bwd(x, y, mod, 2, lnp, 0, dx1, f"{tag}_ln_b")
    du3 = _mm(dy, w_out, "nt", name=f"{tag}_out_bx")
    dw_out = _mm(u3, dy, "tn", name=f"{tag}_out_bw")
    du2, dcln = _ln_silu_bwd(u2, cln, du3, f"{tag}_cln_b")
    dval, dgate, ddw_w, ddw_b = _glu_conv_bwd(p, dw_w, du2, f"{tag}_conv_b")
    dpp = jnp.concatenate([dval, dgate], axis=1)
    dw_in = _mm(h, dpp, "tn", name=f"{tag}_in_bw")
    dh = _mm(dpp, w_in, "nt", name=f"{tag}_in_bx")
    dx, dss = _modulate_bwd(x, mod, 1, dh, dxa, f"{tag}_mod_b")
    return dx, (dw_in, ddw_w, ddw_b, dcln, dw_out), (dss, dp)


def _two_d(a):
    return a.reshape(-1, a.shape[-1])


def kernel(x, c, ada_w, ada_b, ln_g, ln_b, dn_w_in, dn_conv_w, dn_a_log, dn_dt_bias, dn_norm_w, dn_w_out, cf_w_in, cf_dw_w, cf_dw_b, cf_ln_g, cf_ln_b, cf_w_out, ff_w1, ff_w2, loss_target, m_ada_w, m_ada_b, m_ln_g, m_ln_b, m_dn_w_in, m_dn_conv_w, m_dn_a_log, m_dn_dt_bias, m_dn_norm_w, m_dn_w_out, m_cf_w_in, m_cf_dw_w, m_cf_dw_b, m_cf_ln_g, m_cf_ln_b, m_cf_w_out, m_ff_w1, m_ff_w2, v_ada_w, v_ada_b, v_ln_g, v_ln_b, v_dn_w_in, v_dn_conv_w, v_dn_a_log, v_dn_dt_bias, v_dn_norm_w, v_dn_w_out, v_cf_w_in, v_cf_dw_w, v_cf_dw_b, v_cf_ln_g, v_cf_ln_b, v_cf_w_out, v_ff_w1, v_ff_w2):
    ix, iy, ic = lax.axis_index("x"), lax.axis_index("y"), lax.axis_index("c")
    chip = 2 * ix + iy
    dev = 4 * ix + 2 * iy + ic
    S, D = x.shape[1], x.shape[2]
    L = ada_w.shape[0]
    LA, LB = dn_w_in.shape[0], cf_w_in.shape[0]
    H = dn_a_log.shape[1]
    NMOD = ada_b.shape[1] // D
    dn_in = dn_w_in.shape[2] * 4
    n_main = dn_in - 2 * H
    assert L == N_LAYERS and 2 * H <= LANES
    x0, tgt = x[0], loss_target[0]

    small_sharded = [ln_g, ln_b, dn_conv_w, cf_dw_w, cf_dw_b, cf_ln_g, cf_ln_b]
    small_axes = [2, 2, 2, 2, 1, 1, 1]
    packed_small = _pack(small_sharded, rows=8)[None]
    g_dn_in, g_dn_out, g_cf_in, g_cf_out, g_w1, g_w2, g_small = _exchange(
        [(dn_w_in.astype(_MXU)[None], 0), (dn_w_out.astype(_MXU), 1), (cf_w_in.astype(_MXU), 2),
         (cf_w_out.astype(_MXU), 1), (ff_w1.astype(_MXU), 2), (ff_w2.astype(_MXU), 1), (packed_small, 0)],
        "xy", "gather", "gather_weights")
    w_dn_in = jnp.transpose(g_dn_in, (1, 2, 0, 3)).reshape(LA, D, dn_in)
    w_dn_main = w_dn_in[:, :, :n_main]
    w_dn_small = jnp.pad(w_dn_in[:, :, n_main:], ((0, 0), (0, 0), (0, LANES - 2 * H)))
    shard_shapes = [a.shape for a in small_sharded]
    per_chip = [_unpack(g_small[q], shard_shapes) for q in range(4)]
    ln_g_f, ln_b_f, conv_w_f, dw_w_f, dw_b_f, cln_g_f, cln_b_f = [
        jnp.concatenate([per_chip[q][i] for q in range(4)], axis=small_axes[i]) for i in range(len(small_sharded))]

    c_all = _exchange([(c[None], 0)], "all", "gather", "gather_cond")[0].reshape(8, D)
    c_pad = jnp.pad(c_all, ((0, 8), (0, 0)))
    mod_sh = jnp.stack([_mm(c_pad, (ada_w, l), "nn", name=f"ada_{l}", a_fn=lambda t: t * _sigmoid(t))
                        for l in range(L)])
    mod_all = _exchange([(mod_sh, 2)], "xy", "gather", "gather_mod")[0]
    mod_mine = lax.dynamic_index_in_dim(mod_all, dev, axis=1, keepdims=False) + ada_b
    mods = mod_mine.reshape(L, NMOD, D)

    def lnp_of(l):
        return jnp.stack([ln_g_f[l, 0], ln_b_f[l, 0], ln_g_f[l, 1], ln_b_f[l, 1]])

    def dn_wts(j):
        prm = jnp.zeros((2, LANES), F32).at[0, H:2 * H].set(dn_a_log[j]).at[1, H:2 * H].set(dn_dt_bias[j])
        return ((w_dn_main, j), (w_dn_small, j), conv_w_f[j], prm, dn_norm_w[j][None], (g_dn_out, j))

    def cf_wts(j):
        return ((g_cf_in, j), dw_w_f[j], dw_b_f[j][None], jnp.stack([cln_g_f[j], cln_b_f[j]]), (g_cf_out, j))

    xs = x0
    saved = []
    for l in range(L):
        j = l // 2
        if l % 2 == 0:
            xs, sv_a = _dn_fwd(xs, mods[l], lnp_of(l), dn_wts(j), H, f"l{l}_dn")
        else:
            xs, sv_a = _cf_fwd(xs, mods[l], lnp_of(l), cf_wts(j), f"l{l}_cf")
        xs, sv_b = _mlp_fwd(xs, mods[l], lnp_of(l), (g_w1, l), (g_w2, l), f"l{l}_ff")
        saved.append((sv_a, sv_b))
    dx, loss_local = _loss_head(xs, tgt, "loss_head")
    loss = lax.psum(loss_local[0, 0], ("x", "y", "c"))

    gw1, gw2 = [None] * L, [None] * L
    g_dn = [None] * LA
    g_cf = [None] * LB
    dmods, dlns = [None] * L, [None] * L
    for l in reversed(range(L)):
        j = l // 2
        sv_a, sv_b = saved[l]
        dx, gw1[l], gw2[l], (dss2, dp2) = _mlp_bwd(dx, sv_b, mods[l], lnp_of(l), (g_w1, l), (g_w2, l), f"l{l}_ff")
        if l % 2 == 0:
            dx, g_dn[j], (dss1, dp1) = _dn_bwd(dx, sv_a, mods[l], lnp_of(l), dn_wts(j), H, f"l{l}_dn")
        else:
            dx, g_cf[j], (dss1, dp1) = _cf_bwd(dx, sv_a, mods[l], lnp_of(l), cf_wts(j), f"l{l}_cf")
        dmods[l] = jnp.concatenate([dss1, dp1[0:1], dss2, dp2[0:1]], axis=0)
        dlns[l] = (jnp.stack([dp1[1], dp2[1]]), jnp.stack([dp1[2], dp2[2]]))
    grad_x = dx[None]

    d_ln_g = jnp.stack([dlns[l][0] for l in range(L)])
    d_ln_b = jnp.stack([dlns[l][1] for l in range(L)])
    d_conv_w = jnp.stack([g_dn[j][2] for j in range(LA)])
    d_a_log = jnp.stack([g_dn[j][3][0, H:2 * H] for j in range(LA)])
    d_dt_bias = jnp.stack([g_dn[j][3][1, H:2 * H] for j in range(LA)])
    d_norm_w = jnp.stack([g_dn[j][4][0] for j in range(LA)])
    d_dw_w = jnp.stack([g_cf[j][1] for j in range(LB)])
    d_dw_b = jnp.stack([g_cf[j][2][0] for j in range(LB)])
    d_cln_g = jnp.stack([g_cf[j][3][0] for j in range(LB)])
    d_cln_b = jnp.stack([g_cf[j][3][1] for j in range(LB)])
    d_mod = jnp.stack(dmods).reshape(L, NMOD * D)
    small_full = [d_mod, d_ln_g, d_ln_b, d_conv_w, d_dw_w, d_dw_b, d_cln_g, d_cln_b, d_a_log, d_dt_bias, d_norm_w]
    small_all = _exchange([(_pack(small_full, rows=8)[None], 0)], "all", "gather", "gather_small_grads")[0]
    small_sum = _sum_slots(small_all, "sum_small_grads")
    (s_ada_b, s_ln_g, s_ln_b, s_conv_w, s_dw_w, s_dw_b, s_cln_g, s_cln_b, s_a_log, s_dt_bias, s_norm_w) = _unpack(
        small_sum, [a.shape for a in small_full])
    d_mod_all = small_all.reshape(8, -1)[:, :L * NMOD * D].reshape(8, L, NMOD * D)

    def shard(a, axis):
        size = a.shape[axis] // 4
        return lax.dynamic_slice_in_dim(a, chip * size, size, axis)

    ncol = ada_w.shape[2]
    d_mod_sh = jnp.pad(lax.dynamic_slice_in_dim(d_mod_all, chip * ncol, ncol, 2), ((0, 8), (0, 0), (0, 0)))
    g_ada_w = jnp.stack([_mm(c_pad, d_mod_sh[:, l], "tn", name=f"ada_bw_{l}", a_fn=lambda t: t * _sigmoid(t))
                         for l in range(L)])

    dn_in_full = jnp.stack([jnp.concatenate([g_dn[j][0], g_dn[j][1][:, :2 * H]], axis=1) for j in range(LA)])
    dn_in_st = jnp.transpose(dn_in_full.reshape(LA, D, 4, dn_in // 4), (2, 0, 1, 3))
    big = [(dn_in_st, 0), (jnp.stack([g_dn[j][5] for j in range(LA)]), 1),
           (jnp.stack([g_cf[j][0] for j in range(LB)]), 2), (jnp.stack([g_cf[j][4] for j in range(LB)]), 1),
           (jnp.stack(gw1), 2), (jnp.stack(gw2), 1)]
    recv = _exchange(big, "xy", "scatter", "scatter_grads")
    sums = [_sum_slots(r.reshape(4, -1, r.shape[-1]), f"sum_grads_{i}") for i, r in enumerate(recv)]
    both = _exchange([(s[None], 0) for s in sums], "c", "gather", "swap_sums")

    big_w = [(dn_w_in, m_dn_w_in, v_dn_w_in), (dn_w_out, m_dn_w_out, v_dn_w_out), (cf_w_in, m_cf_w_in, v_cf_w_in),
             (cf_w_out, m_cf_w_out, v_cf_w_out), (ff_w1, m_ff_w1, v_ff_w1), (ff_w2, m_ff_w2, v_ff_w2)]
    big_out = []
    for i, ((w, m, v), st) in enumerate(zip(big_w, both)):
        res = _adamw(_two_d(w), _two_d(m), _two_d(v), st, f"adamw_big_{i}")
        big_out.append([r.reshape(w.shape) for r in res])
    ada_out = [r.reshape(ada_w.shape) for r in _adamw(_two_d(ada_w), _two_d(m_ada_w), _two_d(v_ada_w),
                                                     _two_d(g_ada_w)[None], "adamw_ada_w")]

    small_w = [(ada_b, m_ada_b, v_ada_b, s_ada_b), (ln_g, m_ln_g, v_ln_g, shard(s_ln_g, 2)),
               (ln_b, m_ln_b, v_ln_b, shard(s_ln_b, 2)), (dn_conv_w, m_dn_conv_w, v_dn_conv_w, shard(s_conv_w, 2)),
               (dn_a_log, m_dn_a_log, v_dn_a_log, s_a_log), (dn_dt_bias, m_dn_dt_bias, v_dn_dt_bias, s_dt_bias),
               (dn_norm_w, m_dn_norm_w, v_dn_norm_w, s_norm_w), (cf_dw_w, m_cf_dw_w, v_cf_dw_w, shard(s_dw_w, 2)),
               (cf_dw_b, m_cf_dw_b, v_cf_dw_b, shard(s_dw_b, 1)), (cf_ln_g, m_cf_ln_g, v_cf_ln_g, shard(s_cln_g, 1)),
               (cf_ln_b, m_cf_ln_b, v_cf_ln_b, shard(s_cln_b, 1))]
    pk = [_pack([t[i] for t in small_w], rows=8) for i in range(4)]
    small_res = _adamw(pk[0], pk[1], pk[2], pk[3][None], "adamw_small")
    small_shapes = [t[0].shape for t in small_w]
    small_out = [_unpack(r, small_shapes) for r in small_res]

    def kind(k):
        sm = small_out[k]
        bg = [o[k] for o in big_out]
        return [ada_out[k], sm[0], sm[1], sm[2], bg[0], sm[3], sm[4], sm[5], sm[6], bg[1],
                bg[2], sm[7], sm[8], sm[9], sm[10], bg[3], bg[4], bg[5]]

    return (loss, grad_x, *kind(0), *kind(1), *kind(2), *kind(3))
```

```python
import functools

import jax
import jax.numpy as jnp
from jax import lax
from jax.experimental import pallas as pl
from jax.experimental.pallas import tpu as pltpu

F32 = jnp.float32
_MXU = jnp.bfloat16
_HI = lax.Precision.HIGHEST

N_LAYERS = 4
ALPHA = (2.0 * N_LAYERS) ** 0.25
LN_EPS = 1e-5
RMS_EPS = 1e-6
L2_EPS = 1e-6
CHUNK = 64
ADAM_LR, ADAM_B1, ADAM_B2, ADAM_EPS, ADAM_WD, ADAM_STEP = 0.001, 0.9, 0.999, 1e-08, 0.01, 10

LANES = 128
TOKEN_BLOCK = 256
VMEM_BIG = 48 * 1024 * 1024
MM_TILE = 1024

SDS = jax.ShapeDtypeStruct
MESH = pl.DeviceIdType.MESH


def _cparams(vmem=None):
    if vmem is None:
        return None
    return pltpu.CompilerParams(vmem_limit_bytes=vmem)


def _pcall(body, **kw):
    if kw.get("compiler_params", 1) is None:
        kw.pop("compiler_params")
    return pl.pallas_call(body, **kw)


def _full(arr):
    nd = arr.ndim
    return pl.BlockSpec(arr.shape, lambda *g: (0,) * nd)


def _bs(block, imap, lead=None):
    if lead is None:
        return pl.BlockSpec(block, imap)
    return pl.BlockSpec((None,) + tuple(block), lambda *g: (lead,) + tuple(imap(*g)))


def _split(a):
    return a if isinstance(a, tuple) else (a, None)


_GROUPS = {
    "xy": ([(1, 0, 0), (0, 1, 0), (1, 1, 0)], 4),
    "c": ([(0, 0, 1)], 2),
    "all": ([(1, 0, 0), (0, 1, 0), (1, 1, 0), (0, 0, 1), (1, 0, 1), (0, 1, 1), (1, 1, 1)], 8),
}


def _exchange(items, group, mode, name):
    masks, n = _GROUPS[group]
    npeer = len(masks)
    ni = len(items)
    arrs = [a for a, _ in items]
    out_shapes = []
    for a, ax in items:
        shp = list(a.shape)
        if mode == "gather":
            shp[ax] *= n
        else:
            shp[ax] //= n
            shp = [n] + shp
        out_shapes.append(SDS(tuple(shp), a.dtype))

    def body(*refs):
        ins, outs = refs[:ni], refs[ni:2 * ni]
        send_sems, recv_sems, local_sems = refs[2 * ni:]
        x, y, c = lax.axis_index("x"), lax.axis_index("y"), lax.axis_index("c")

        def slot(px, py, pc):
            if group == "xy":
                return 2 * px + py
            if group == "c":
                return pc
            return 4 * px + 2 * py + pc

        me = slot(x, y, c)

        def block(ref, ax, idx, size):
            ix = (slice(None),) * ax + (pl.ds(pl.multiple_of(idx * size, size), size),)
            return ref.at[ix]

        copies = []
        for it, (a, ax) in enumerate(items):
            in_ref, out_ref = ins[it], outs[it]
            if mode == "gather":
                size = a.shape[ax]
                src_own, dst_own = in_ref, block(out_ref, ax, me, size)
            else:
                size = a.shape[ax] // n
                src_own, dst_own = block(in_ref, ax, me, size), out_ref.at[me]
            own = pltpu.make_async_copy(src_own, dst_own, local_sems.at[it])
            own.start()
            copies.append(own)
            for k, m in enumerate(masks):
                peer = tuple((1 - v) if b else v for v, b in zip((x, y, c), m))
                if mode == "gather":
                    src, dst = in_ref, dst_own
                else:
                    src, dst = block(in_ref, ax, slot(*peer), size), out_ref.at[me]
                cp = pltpu.make_async_remote_copy(
                    src_ref=src, dst_ref=dst, send_sem=send_sems.at[it * npeer + k],
                    recv_sem=recv_sems.at[it * npeer + k], device_id=peer, device_id_type=MESH)
                cp.start()
                copies.append(cp)
        for cp in copies:
            cp.wait()

    any_spec = pl.BlockSpec(memory_space=pl.ANY)
    outs = _pcall(
        body, name=name, out_shape=tuple(out_shapes),
        in_specs=[any_spec] * ni, out_specs=tuple([any_spec] * ni),
        scratch_shapes=[pltpu.SemaphoreType.DMA((ni * npeer,)), pltpu.SemaphoreType.DMA((ni * npeer,)),
                        pltpu.SemaphoreType.DMA((ni,))],
    )(*arrs)
    return list(outs)


def _tile(n, cap):
    if n <= cap:
        return n
    t = cap - cap % LANES
    while n % t:
        t -= LANES
    return t


def _mm(a, b, mode, *, name, out_dtypes=(F32,), tm=MM_TILE, tn=MM_TILE, tk=MM_TILE, a_fn=None, out_fn=None, aux=(),
        out_imap=None, out_shape=None):
    (a, a_lead), (b, b_lead) = _split(a), _split(b)
    ash, bsh = a.shape[-2:], b.shape[-2:]
    if mode == "nn":
        (M, K), (_, N) = ash, bsh
    elif mode == "nt":
        (M, K), (N, _) = ash, bsh
    else:
        (K, M), (_, N) = ash, bsh
    tm, tn, tk = _tile(M, tm), _tile(N, tn), _tile(K, tk)
    nk = K // tk
    if mode == "tn":
        a_spec = _bs((tk, tm), lambda i, j, k: (k, i), a_lead)
    else:
        a_spec = _bs((tm, tk), lambda i, j, k: (i, k), a_lead)
    if mode == "nt":
        b_spec = _bs((tn, tk), lambda i, j, k: (j, k), b_lead)
    else:
        b_spec = _bs((tk, tn), lambda i, j, k: (k, j), b_lead)
    aux_arrs, aux_specs = [], []
    for arr, kind in aux:
        arr, lead = _split(arr)
        aux_arrs.append(arr)
        if kind == "mn":
            aux_specs.append(_bs((tm, tn), lambda i, j, k: (i, j), lead))
        else:
            aux_specs.append(_bs((1, tn), lambda i, j, k: (0, j), lead))
    na, no = len(aux_arrs), len(out_dtypes)
    dims = {"nn": (((1,), (0,)), ((), ())), "nt": (((1,), (1,)), ((), ())), "tn": (((0,), (0,)), ((), ()))}[mode]

    def finish(r, aux_refs, o_refs):
        outs = out_fn(r, *[x[...] for x in aux_refs]) if out_fn is not None else (r,)
        for o_ref, val in zip(o_refs, outs):
            o_ref[...] = val.astype(o_ref.dtype)

    def product(a_ref, b_ref):
        av = a_ref[...]
        if a_fn is not None:
            av = a_fn(av.astype(F32))
        return lax.dot_general(av.astype(_MXU), b_ref[...].astype(_MXU), dims, preferred_element_type=F32)

    def body_one(a_ref, b_ref, *rest):
        finish(product(a_ref, b_ref), rest[:na], rest[na:na + no])

    def body_acc(a_ref, b_ref, *rest):
        aux_refs, o_refs, acc = rest[:na], rest[na:na + no], rest[na + no]
        k = pl.program_id(2)

        @pl.when(k == 0)
        def _():
            acc[...] = product(a_ref, b_ref)

        @pl.when(k != 0)
        def _():
            acc[...] += product(a_ref, b_ref)

        @pl.when(k == nk - 1)
        def _():
            finish(acc[...], aux_refs, o_refs)

    if out_shape is None:
        out_shape = (M, N)
    if out_imap is None:
        o_spec = pl.BlockSpec((tm, tn), lambda i, j, k: (i, j))
    else:
        o_spec = pl.BlockSpec((None,) * (len(out_shape) - 2) + (tm, tn), out_imap)
    outs = _pcall(
        body_one if nk == 1 else body_acc, name=name, grid=(M // tm, N // tn, nk),
        in_specs=[a_spec, b_spec] + aux_specs, out_specs=tuple([o_spec] * no),
        out_shape=tuple(SDS(out_shape, dt) for dt in out_dtypes),
        scratch_shapes=[] if nk == 1 else [pltpu.VMEM((tm, tn), F32)],
        compiler_params=pltpu.CompilerParams(dimension_semantics=("parallel", "parallel", "arbitrary"),
                                             vmem_limit_bytes=VMEM_BIG),
    )(a, b, *aux_arrs)
    return outs[0] if no == 1 else outs


def _tok(S):
    ts = min(TOKEN_BLOCK, S)
    assert S % ts == 0
    return ts


def _row(ts, D):
    return pl.BlockSpec((ts, D), lambda i: (i, 0))


def _acc_rows(ref, i, rows):
    @pl.when(i == 0)
    def _():
        for r, v in enumerate(rows):
            ref[r:r + 1, :] = v

    @pl.when(i != 0)
    def _():
        for r, v in enumerate(rows):
            ref[r:r + 1, :] += v


def _modulate(x, mod, r_sh, r_sc, name):
    S, D = x.shape
    ts = _tok(S)

    def body(x_ref, m_ref, o_ref):
        o_ref[...] = (x_ref[...] * (1.0 + m_ref[r_sc:r_sc + 1, :]) + m_ref[r_sh:r_sh + 1, :]).astype(o_ref.dtype)

    return _pcall(body, name=name, grid=(S // ts,), in_specs=[_row(ts, D), _full(mod)],
                  out_specs=_row(ts, D), out_shape=SDS((S, D), _MXU))(x, mod)


def _modulate_bwd(x, mod, r_sc, dh, dxa, name):
    S, D = x.shape
    ts = _tok(S)

    def body(x_ref, m_ref, dh_ref, dxa_ref, dx_ref, dss_ref):
        dh_v = dh_ref[...]
        dx_ref[...] = dxa_ref[...] + dh_v * (1.0 + m_ref[r_sc:r_sc + 1, :])
        _acc_rows(dss_ref, pl.program_id(0),
                  [jnp.sum(dh_v, axis=0, keepdims=True), jnp.sum(dh_v * x_ref[...], axis=0, keepdims=True)])

    return _pcall(body, name=name, grid=(S // ts,),
                  in_specs=[_row(ts, D), _full(mod), _row(ts, D), _row(ts, D)],
                  out_specs=(_row(ts, D), pl.BlockSpec((2, D), lambda i: (0, 0))),
                  out_shape=(SDS((S, D), F32), SDS((2, D), F32)))(x, mod, dh, dxa)


def _norm_stats(z):
    mu = jnp.mean(z, axis=-1, keepdims=True)
    zc = z - mu
    var = jnp.mean(zc * zc, axis=-1, keepdims=True)
    rstd = lax.rsqrt(var + LN_EPS)
    return zc * rstd, rstd


def _norm_bwd(dxhat, xhat, rstd):
    return rstd * (dxhat - jnp.mean(dxhat, axis=-1, keepdims=True)
                   - xhat * jnp.mean(dxhat * xhat, axis=-1, keepdims=True))


def _combine(x, y, mod, r_gt, lnp, r_g, name):
    S, D = x.shape
    ts = _tok(S)

    def body(x_ref, y_ref, m_ref, l_ref, o_ref):
        z = ALPHA * x_ref[...] + (1.0 + m_ref[r_gt:r_gt + 1, :]) * y_ref[...]
        xhat, _ = _norm_stats(z)
        o_ref[...] = xhat * l_ref[r_g:r_g + 1, :] + l_ref[r_g + 1:r_g + 2, :]

    return _pcall(body, name=name, grid=(S // ts,), in_specs=[_row(ts, D), _row(ts, D), _full(mod), _full(lnp)],
                  out_specs=_row(ts, D), out_shape=SDS((S, D), F32))(x, y, mod, lnp)


def _combine_bwd(x, y, mod, r_gt, lnp, r_g, dout, name):
    S, D = x.shape
    ts = _tok(S)

    def body(x_ref, y_ref, m_ref, l_ref, do_ref, dxa_ref, dy_ref, dp_ref):
        gate = 1.0 + m_ref[r_gt:r_gt + 1, :]
        y_v, do_v = y_ref[...], do_ref[...]
        xhat, rstd = _norm_stats(ALPHA * x_ref[...] + gate * y_v)
        dz = _norm_bwd(do_v * l_ref[r_g:r_g + 1, :], xhat, rstd)
        dxa_ref[...] = ALPHA * dz
        dy_ref[...] = (gate * dz).astype(dy_ref.dtype)
        _acc_rows(dp_ref, pl.program_id(0),
                  [jnp.sum(dz * y_v, axis=0, keepdims=True), jnp.sum(do_v * xhat, axis=0, keepdims=True),
                   jnp.sum(do_v, axis=0, keepdims=True)])

    return _pcall(body, name=name, grid=(S // ts,),
                  in_specs=[_row(ts, D), _row(ts, D), _full(mod), _full(lnp), _row(ts, D)],
                  out_specs=(_row(ts, D), _row(ts, D), pl.BlockSpec((3, D), lambda i: (0, 0))),
                  out_shape=(SDS((S, D), F32), SDS((S, D), _MXU), SDS((3, D), F32)))(x, y, mod, lnp, dout)


def _sigmoid(t):
    return 1.0 / (1.0 + jnp.exp(-t))


def _ln_silu(u, lnp, name):
    S, D = u.shape
    ts = _tok(S)

    def body(u_ref, l_ref, o_ref):
        xhat, _ = _norm_stats(u_ref[...])
        t = xhat * l_ref[0:1, :] + l_ref[1:2, :]
        o_ref[...] = (t * _sigmoid(t)).astype(o_ref.dtype)

    return _pcall(body, name=name, grid=(S // ts,), in_specs=[_row(ts, D), _full(lnp)],
                  out_specs=_row(ts, D), out_shape=SDS((S, D), _MXU))(u, lnp)


def _ln_silu_bwd(u, lnp, dout, name):
    S, D = u.shape
    ts = _tok(S)

    def body(u_ref, l_ref, do_ref, du_ref, dp_ref):
        xhat, rstd = _norm_stats(u_ref[...])
        g = l_ref[0:1, :]
        t = xhat * g + l_ref[1:2, :]
        sg = _sigmoid(t)
        dt = do_ref[...] * (sg * (1.0 + t * (1.0 - sg)))
        du_ref[...] = _norm_bwd(dt * g, xhat, rstd)
        _acc_rows(dp_ref, pl.program_id(0),
                  [jnp.sum(dt * xhat, axis=0, keepdims=True), jnp.sum(dt, axis=0, keepdims=True)])

    return _pcall(body, name=name, grid=(S // ts,), in_specs=[_row(ts, D), _full(lnp), _row(ts, D)],
                  out_specs=(_row(ts, D), pl.BlockSpec((2, D), lambda i: (0, 0))),
                  out_shape=(SDS((S, D), F32), SDS((2, D), F32)))(u, lnp, dout)


def _loss_head(xf, tgt, name):
    S, D = xf.shape
    ts = _tok(S)

    def body(x_ref, t_ref, dx_ref, l_ref):
        err = x_ref[...] - t_ref[...]
        dx_ref[...] = err * (1.0 / D)
        part = jnp.sum(jnp.sum(err * err, axis=1, keepdims=True), axis=0, keepdims=True) * (0.5 / D)

        @pl.when(pl.program_id(0) == 0)
        def _():
            l_ref[...] = part

        @pl.when(pl.program_id(0) != 0)
        def _():
            l_ref[...] += part

    return _pcall(body, name=name, grid=(S // ts,), in_specs=[_row(ts, D), _row(ts, D)],
                  out_specs=(_row(ts, D), pl.BlockSpec((1, 1), lambda i: (0, 0))),
                  out_shape=(SDS((S, D), F32), SDS((1, 1), F32)))(xf, tgt)


def _shift_down(u, s, rows):
    if s == 0:
        return u
    return jnp.where(rows >= s, pltpu.roll(u, s, 0), 0.0)


def _shift_up(u, s, rows):
    if s == 0:
        return u
    n = u.shape[0]
    return jnp.where(rows < n - s, pltpu.roll(u, n - s, 0), 0.0)


def _dwconv(u, w_ref, taps, rows):
    acc = jnp.zeros_like(u)
    for j in range(taps):
        acc = acc + w_ref[j:j + 1, :] * _shift_down(u, taps - 1 - j, rows)
    return acc


def _dwconv_bwd(u, dy, w_ref, dw_ref, taps, rows):
    du = jnp.zeros_like(u)
    for j in range(taps):
        s = taps - 1 - j
        du = du + w_ref[j:j + 1, :] * _shift_up(dy, s, rows)
        dw_ref[j:j + 1, :] = jnp.sum(dy * _shift_down(u, s, rows), axis=0, keepdims=True)
    return du


def _col(S, j0=0):
    return pl.BlockSpec((S, LANES), lambda j: (0, j + j0))


def _conv_silu(pm, w, nblk, name):
    S = pm.shape[0]
    taps = w.shape[0]

    def body(u_ref, w_ref, o_ref):
        rows = lax.broadcasted_iota(jnp.int32, (S, LANES), 0)
        cv = _dwconv(u_ref[...], w_ref, taps, rows)
        o_ref[...] = cv * _sigmoid(cv)

    return _pcall(body, name=name, grid=(nblk,),
                  in_specs=[_col(S), pl.BlockSpec((taps, LANES), lambda j: (0, j))],
                  out_specs=_col(S), out_shape=SDS((S, nblk * LANES), F32),
                  compiler_params=_cparams(VMEM_BIG))(pm, w)


def _conv_silu_bwd(pm, w, dout, dpm, j0, name):
    S = pm.shape[0]
    taps = w.shape[0]
    nblk = dout.shape[1] // LANES

    def body(u_ref, w_ref, do_ref, dpm_in, du_ref, dw_ref):
        del dpm_in
        rows = lax.broadcasted_iota(jnp.int32, (S, LANES), 0)
        u = u_ref[...]
        cv = _dwconv(u, w_ref, taps, rows)
        sg = _sigmoid(cv)
        dc = do_ref[...] * (sg * (1.0 + cv * (1.0 - sg)))
        du_ref[...] = _dwconv_bwd(u, dc, w_ref, dw_ref, taps, rows)

    return _pcall(body, name=name, grid=(nblk,),
                  in_specs=[_col(S, j0), pl.BlockSpec((taps, LANES), lambda j: (0, j + j0)), _col(S),
                            pl.BlockSpec(memory_space=pl.ANY)],
                  out_specs=(_col(S, j0), pl.BlockSpec((taps, LANES), lambda j: (0, j))),
                  out_shape=(SDS(dpm.shape, F32), SDS((taps, nblk * LANES), F32)),
                  input_output_aliases={3: 0},
                  compiler_params=_cparams(VMEM_BIG))(pm, w, dout, dpm)


def _glu_conv(p, w, bias, name):
    S, C2 = p.shape
    nblk = C2 // 2 // LANES
    taps = w.shape[0]

    def body(v_ref, g_ref, w_ref, b_ref, o_ref):
        rows = lax.broadcasted_iota(jnp.int32, (S, LANES), 0)
        u = v_ref[...] * _sigmoid(g_ref[...])
        o_ref[...] = _dwconv(u, w_ref, taps, rows) + b_ref[...]

    return _pcall(body, name=name, grid=(nblk,),
                  in_specs=[_col(S), _col(S, nblk), pl.BlockSpec((taps, LANES), lambda j: (0, j)),
                            pl.BlockSpec((1, LANES), lambda j: (0, j))],
                  out_specs=_col(S), out_shape=SDS((S, nblk * LANES), F32),
                  compiler_params=_cparams(VMEM_BIG))(p, p, w, bias)


def _glu_conv_bwd(p, w, dout, name):
    S, C2 = p.shape
    nblk = C2 // 2 // LANES
    taps = w.shape[0]

    def body(v_ref, g_ref, w_ref, do_ref, dv_ref, dg_ref, dw_ref, db_ref):
        rows = lax.broadcasted_iota(jnp.int32, (S, LANES), 0)
        val, sg = v_ref[...], _sigmoid(g_ref[...])
        do_v = do_ref[...]
        du = _dwconv_bwd(val * sg, do_v, w_ref, dw_ref, taps, rows)
        dv_ref[...] = du * sg
        dg_ref[...] = du * val * sg * (1.0 - sg)
        db_ref[...] = jnp.sum(do_v, axis=0, keepdims=True)

    dval, dgate, dw, db = _pcall(
        body, name=name, grid=(nblk,),
        in_specs=[_col(S), _col(S, nblk), pl.BlockSpec((taps, LANES), lambda j: (0, j)), _col(S)],
        out_specs=(_col(S), _col(S), pl.BlockSpec((taps, LANES), lambda j: (0, j)),
                   pl.BlockSpec((1, LANES), lambda j: (0, j))),
        out_shape=(SDS((S, C2 // 2), F32), SDS((S, C2 // 2), F32), SDS((taps, C2 // 2), F32), SDS((1, C2 // 2), F32)),
        compiler_params=_cparams(VMEM_BIG))(p, p, w, dout)
    return dval, dgate, dw, db


def _log1p(e):
    u = 1.0 + e
    d = jnp.where(u == 1.0, 1.0, u - 1.0)
    return jnp.where(u == 1.0, e, jnp.log(u) * (e / d))


def _gate_parts(ps, prm, H):
    lane = lax.broadcasted_iota(jnp.int32, ps.shape, 1)
    is_b, is_g = lane < H, (lane >= H) & (lane < 2 * H)
    beta = _sigmoid(ps)
    t = ps + prm[1:2, :]
    sp = jnp.maximum(t, 0.0) + _log1p(jnp.exp(-jnp.abs(t)))
    na = -jnp.exp(prm[0:1, :])
    return is_b, is_g, beta, t, sp, na


def _gates(ps, prm, H, name):
    S = ps.shape[0]
    ts = _tok(S)

    def body(p_ref, r_ref, o_ref):
        is_b, is_g, beta, _, sp, na = _gate_parts(p_ref[...], r_ref[...], H)
        o_ref[...] = jnp.where(is_b, beta, jnp.where(is_g, na * sp, 0.0))

    return _pcall(body, name=name, grid=(S // ts,), in_specs=[_row(ts, LANES), _full(prm)],
                  out_specs=_row(ts, LANES), out_shape=SDS((S, LANES), F32))(ps, prm)


def _gates_bwd(ps, prm, dgates, H, name):
    S = ps.shape[0]
    ts = _tok(S)

    def body(p_ref, r_ref, dg_ref, dp_ref, dr_ref):
        is_b, is_g, beta, t, sp, na = _gate_parts(p_ref[...], r_ref[...], H)
        dg_v = dg_ref[...]
        dsp = jnp.where(is_g, dg_v * na * _sigmoid(t), 0.0)
        dp_ref[...] = jnp.where(is_b, dg_v * beta * (1.0 - beta), dsp)
        _acc_rows(dr_ref, pl.program_id(0),
                  [jnp.sum(jnp.where(is_g, dg_v * na * sp, 0.0), axis=0, keepdims=True),
                   jnp.sum(dsp, axis=0, keepdims=True)])

    return _pcall(body, name=name, grid=(S // ts,), in_specs=[_row(ts, LANES), _full(prm), _row(ts, LANES)],
                  out_specs=(_row(ts, LANES), pl.BlockSpec((2, LANES), lambda i: (0, 0))),
                  out_shape=(SDS((S, LANES), F32), SDS((2, LANES), F32)))(ps, prm, dgates)


_NN = (((1,), (0,)), ((), ()))
_NT = (((1,), (1,)), ((), ()))
_TN = (((0,), (0,)), ((), ()))


def _dot(a, b, dims=_NN):
    return lax.dot_general(a, b, dims, precision=_HI, preferred_element_type=F32)


def _mdot(a, b, dims):
    return lax.dot_general(a.astype(_MXU), b.astype(_MXU), dims, preferred_element_type=F32)


def _rounded_dot(dims, da_dims, db_dims, a_first):
    @jax.custom_vjp
    def f(a, b):
        return _mdot(a, b, dims)

    def fwd(a, b):
        return _mdot(a, b, dims), (a, b)

    def bwd(res, ct):
        a, b = res
        da = _mdot(ct, b, da_dims) if a_first[0] else _mdot(b, ct, da_dims)
        db = _mdot(ct, a, db_dims) if a_first[1] else _mdot(a, ct, db_dims)
        return da, db

    f.defvjp(fwd, bwd)
    return f


_mdot_nn = _rounded_dot(_NN, _NT, _TN, (True, False))
_mdot_nt = _rounded_dot(_NT, _NN, _TN, (True, True))
_mdot_tn = _rounded_dot(_TN, _NT, _NN, (False, False))


def _delta_chunk(qr, kr, v, z, gates, nw, s_in, h, H):
    C, dk = qr.shape
    lane = lax.broadcasted_iota(jnp.int32, gates.shape, 1)
    beta = jnp.sum(jnp.where(lane == h, gates, 0.0), axis=-1, keepdims=True)
    g = jnp.sum(jnp.where(lane == h + H, gates, 0.0), axis=-1, keepdims=True)
    q = qr * lax.rsqrt(jnp.sum(qr * qr, axis=-1, keepdims=True) + L2_EPS) * (dk ** -0.5)
    k = kr * lax.rsqrt(jnp.sum(kr * kr, axis=-1, keepdims=True) + L2_EPS)
    ri = lax.broadcasted_iota(jnp.int32, (C, C), 0)
    ci = lax.broadcasted_iota(jnp.int32, (C, C), 1)
    causal, strict, eye = ri >= ci, ri > ci, ri == ci
    gam_row = jnp.sum(jnp.where(ri <= ci, g, 0.0), axis=0, keepdims=True)
    gam_col = jnp.sum(jnp.where(eye, gam_row, 0.0), axis=-1, keepdims=True)
    g_last = jnp.sum(g, axis=0, keepdims=True)
    decay = jnp.where(causal, jnp.exp(jnp.where(causal, gam_col - gam_row, 0.0)), 0.0)
    kb = k * beta
    a = jnp.where(strict, _mdot_nt(kb, k) * decay, 0.0)
    t_inv = jnp.where(eye, 1.0, 0.0) - a
    p = a
    for _ in range(max(C.bit_length() - 2, 0)):
        p = _dot(p, p)
        t_inv = t_inv + _dot(t_inv, p)
    eg = jnp.exp(gam_col)
    u = _mdot_nn(t_inv, v * beta)
    w = _mdot_nn(t_inv, kb * eg)
    a_qk = _mdot_nt(q, k) * decay
    v_new = u - _mdot_nn(w, s_in)
    o = _mdot_nn(q * eg, s_in) + _mdot_nn(a_qk, v_new)
    s_out = s_in * jnp.exp(g_last) + _mdot_tn(k * jnp.exp(g_last - gam_col), v_new)
    og = o * lax.rsqrt(jnp.mean(o * o, axis=-1, keepdims=True) + RMS_EPS) * nw * (z * _sigmoid(z))
    return og, s_out


def _delta_fwd(qkv, pm, gates, nw, H, name):
    S = qkv.shape[0]
    hd = qkv.shape[1] // 3
    dk = hd // H
    N = S // CHUNK
    blk = lambda off: pl.BlockSpec((CHUNK, hd), lambda n: (n, off))

    def body(q_ref, k_ref, v_ref, z_ref, g_ref, nw_ref, og_ref, st_ref, s_scr):
        @pl.when(pl.program_id(0) == 0)
        def _():
            s_scr[...] = jnp.zeros_like(s_scr)

        gates_v, nw_v = g_ref[...], nw_ref[...]
        for h in range(H):
            sl = slice(h * dk, (h + 1) * dk)
            s_in = s_scr[h]
            st_ref[h] = s_in
            og, s_out = _delta_chunk(q_ref[:, sl], k_ref[:, sl], v_ref[:, sl], z_ref[:, sl], gates_v, nw_v,
                                     s_in, h, H)
            og_ref[:, sl] = og.astype(og_ref.dtype)
            s_scr[h] = s_out

    return _pcall(
        body, name=name, grid=(N,),
        in_specs=[blk(0), blk(1), blk(2), blk(3), pl.BlockSpec((CHUNK, LANES), lambda n: (n, 0)), _full(nw)],
        out_specs=(blk(0), pl.BlockSpec((None, H, dk, dk), lambda n: (n, 0, 0, 0))),
        out_shape=(SDS((S, hd), _MXU), SDS((N, H, dk, dk), F32)),
        scratch_shapes=[pltpu.VMEM((H, dk, dk), F32)],
        compiler_params=_cparams(VMEM_BIG),
    )(qkv, qkv, qkv, pm, gates, nw)


def _delta_bwd(qkv, pm, gates, nw, states, dog, H, name):
    S = qkv.shape[0]
    hd = qkv.shape[1] // 3
    dk = hd // H
    N = S // CHUNK
    blk = lambda off: pl.BlockSpec((CHUNK, hd), lambda n: (N - 1 - n, off))
    gspec = pl.BlockSpec((CHUNK, LANES), lambda n: (N - 1 - n, 0))

    def body(q_ref, k_ref, v_ref, z_ref, g_ref, nw_ref, st_ref, do_ref,
             dq_ref, dk_ref, dv_ref, dz_ref, dg_ref, dnw_ref, ds_scr):
        n = pl.program_id(0)

        @pl.when(n == 0)
        def _():
            ds_scr[...] = jnp.zeros_like(ds_scr)

        gates_v, nw_v = g_ref[...], nw_ref[...]
        dg_sum = jnp.zeros_like(gates_v)
        dnw_sum = jnp.zeros_like(nw_v)
        for h in range(H):
            sl = slice(h * dk, (h + 1) * dk)
            fn = functools.partial(_delta_chunk, h=h, H=H)
            _, vjp = jax.vjp(fn, q_ref[:, sl], k_ref[:, sl], v_ref[:, sl], z_ref[:, sl], gates_v, nw_v, st_ref[h])
            dq, dkk, dv, dz, dg, dnw, ds_in = vjp((do_ref[:, sl], ds_scr[h]))
            dq_ref[:, sl], dk_ref[:, sl], dv_ref[:, sl], dz_ref[:, sl] = dq, dkk, dv, dz
            ds_scr[h] = ds_in
            dg_sum = dg_sum + dg
            dnw_sum = dnw_sum + dnw
        dg_ref[...] = dg_sum

        @pl.when(n == 0)
        def _():
            dnw_ref[...] = dnw_sum

        @pl.when(n != 0)
        def _():
            dnw_ref[...] += dnw_sum

    return _pcall(
        body, name=name, grid=(N,),
        in_specs=[blk(0), blk(1), blk(2), blk(3), gspec, _full(nw),
                  pl.BlockSpec((None, H, dk, dk), lambda n: (N - 1 - n, 0, 0, 0)), blk(0)],
        out_specs=(blk(0), blk(0), blk(0), blk(3), gspec, pl.BlockSpec((1, dk), lambda n: (0, 0))),
        out_shape=(SDS((S, hd), F32), SDS((S, hd), F32), SDS((S, hd), F32), SDS(pm.shape, F32),
                   SDS((S, LANES), F32), SDS((1, dk), F32)),
        scratch_shapes=[pltpu.VMEM((H, dk, dk), F32)],
        compiler_params=_cparams(VMEM_BIG),
    )(qkv, qkv, qkv, pm, gates, nw, states, dog)


def _rows_block(R, C):
    rb = R
    while rb * C * 4 > (1 << 20) and rb % 16 == 0:
        rb //= 2
    return rb


def _sum_slots(st, name, out_dtype=F32):
    n, R, C = st.shape
    rb = _rows_block(R, C)

    def body(s_ref, o_ref):
        acc = s_ref[0].astype(F32)
        for q in range(1, n):
            acc = acc + s_ref[q].astype(F32)
        o_ref[...] = acc.astype(o_ref.dtype)

    return _pcall(body, name=name, grid=(R // rb,), in_specs=[pl.BlockSpec((n, rb, C), lambda i: (0, i, 0))],
                  out_specs=pl.BlockSpec((rb, C), lambda i: (i, 0)), out_shape=SDS((R, C), out_dtype))(st)


def _adamw(w, m, v, st, name):
    R, C = w.shape
    n = st.shape[0]
    rb = _rows_block(R, C)
    spec = pl.BlockSpec((rb, C), lambda i: (i, 0))

    def body(w_ref, m_ref, v_ref, s_ref, g_ref, d_ref, mo_ref, vo_ref):
        g = s_ref[0]
        for q in range(1, n):
            g = g + s_ref[q]
        m_new = ADAM_B1 * m_ref[...] + (1.0 - ADAM_B1) * g
        v_new = ADAM_B2 * v_ref[...] + (1.0 - ADAM_B2) * (g * g)
        m_hat = m_new / (1.0 - ADAM_B1 ** ADAM_STEP)
        v_hat = v_new / (1.0 - ADAM_B2 ** ADAM_STEP)
        g_ref[...] = g
        d_ref[...] = -ADAM_LR * (m_hat / (jnp.sqrt(v_hat) + ADAM_EPS) + ADAM_WD * w_ref[...])
        mo_ref[...] = m_new
        vo_ref[...] = v_new

    return _pcall(body, name=name, grid=(R // rb,),
                  in_specs=[spec, spec, spec, pl.BlockSpec((n, rb, C), lambda i: (0, i, 0))],
                  out_specs=(spec,) * 4, out_shape=(SDS((R, C), F32),) * 4)(w, m, v, st)


def _pack(arrs, rows=1):
    flat = jnp.concatenate([a.reshape(-1).astype(F32) for a in arrs])
    quantum = rows * LANES
    pad = (-flat.shape[0]) % quantum
    flat = jnp.pad(flat, (0, pad))
    return flat.reshape(rows, -1)


def _unpack(flat, shapes):
    flat = flat.reshape(-1)
    out, off = [], 0
    for shp in shapes:
        size = 1
        for d in shp:
            size *= d
        out.append(flat[off:off + size].reshape(shp))
        off += size
    return out


def _mlp_fwd(x1, mod, lnp, w1, w2, tag):
    h2 = _modulate(x1, mod, 3, 4, f"{tag}_mod")
    a1, a2 = _mm(h2, w1, "nn", name=f"{tag}_up", out_dtypes=(_MXU, _MXU),
                 out_fn=lambda r: (r, jnp.square(jnp.maximum(r, 0.0))))
    y2 = _mm(a2, w2, "nn", name=f"{tag}_down")
    x2 = _combine(x1, y2, mod, 5, lnp, 2, f"{tag}_ln")
    return x2, (x1, h2, a1, a2, y2)


def _mlp_bwd(dx2, saved, mod, lnp, w1, w2, tag):
    x1, h2, a1, a2, y2 = saved
    dxa, dy2, dp = _combine_bwd(x1, y2, mod, 5, lnp, 2, dx2, f"{tag}_ln_b")
    da1 = _mm(dy2, w2, "nt", name=f"{tag}_down_bx", out_dtypes=(_MXU,), aux=[(a1, "mn")],
              out_fn=lambda r, a: (r * (2.0 * jnp.maximum(a.astype(F32), 0.0)),))
    dw2 = _mm(a2, dy2, "tn", name=f"{tag}_down_bw")
    dw1 = _mm(h2, da1, "tn", name=f"{tag}_up_bw")
    dh2 = _mm(da1, w1, "nt", name=f"{tag}_up_bx")
    dx1, dss = _modulate_bwd(x1, mod, 4, dh2, dxa, f"{tag}_mod_b")
    return dx1, dw1, dw2, (dss, dp)


def _dn_fwd(x, mod, lnp, wts, H, tag):
    w_main, w_small, conv_w, prm, nw, w_out = wts
    h = _modulate(x, mod, 0, 1, f"{tag}_mod")
    pm = _mm(h, w_main, "nn", name=f"{tag}_in")
    ps = _mm(h, w_small, "nn", name=f"{tag}_in_s")
    nqkv = conv_w.shape[1] // LANES
    qkv = _conv_silu(pm, conv_w, nqkv, f"{tag}_conv")
    gates = _gates(ps, prm, H, f"{tag}_gates")
    og, states = _delta_fwd(qkv, pm, gates, nw, H, f"{tag}_delta")
    y = _mm(og, w_out, "nn", name=f"{tag}_out")
    x1 = _combine(x, y, mod, 2, lnp, 0, f"{tag}_ln")
    return x1, (x, h, pm, ps, qkv, gates, states, og, y)


def _dn_bwd(dx1, saved, mod, lnp, wts, H, tag):
    w_main, w_small, conv_w, prm, nw, w_out = wts
    x, h, pm, ps, qkv, gates, states, og, y = saved
    dxa, dy, dp = _combine_bwd(x, y, mod, 2, lnp, 0, dx1, f"{tag}_ln_b")
    dog = _mm(dy, w_out, "nt", name=f"{tag}_out_bx")
    dw_out = _mm(og, dy, "tn", name=f"{tag}_out_bw")
    dq, dk, dv, dpm, dgates, dnw = _delta_bwd(qkv, pm, gates, nw, states, dog, H, f"{tag}_delta_b")
    dps, dprm = _gates_bwd(ps, prm, dgates, H, f"{tag}_gates_b")
    dcw = []
    nb = dq.shape[1] // LANES
    for part, dpart in enumerate((dq, dk, dv)):
        dpm, dcw_p = _conv_silu_bwd(pm, conv_w, dpart, dpm, part * nb, f"{tag}_conv_b{part}")
        dcw.append(dcw_p)
    dconv_w = jnp.concatenate(dcw, axis=1)
    dw_main = _mm(h, dpm, "tn", name=f"{tag}_in_bw")
    dw_small = _mm(h, dps, "tn", name=f"{tag}_in_s_bw")
    dh_s = _mm(dps, w_small, "nt", name=f"{tag}_in_s_bx")
    dh = _mm(dpm, w_main, "nt", name=f"{tag}_in_bx", aux=[(dh_s, "mn")], out_fn=lambda r, e: (r + e,))
    dx, dss = _modulate_bwd(x, mod, 1, dh, dxa, f"{tag}_mod_b")
    return dx, (dw_main, dw_small, dconv_w, dprm, dnw, dw_out), (dss, dp)


def _cf_fwd(x, mod, lnp, wts, tag):
    w_in, dw_w, dw_b, cln, w_out = wts
    h = _modulate(x, mod, 0, 1, f"{tag}_mod")
    p = _mm(h, w_in, "nn", name=f"{tag}_in")
    u2 = _glu_conv(p, dw_w, dw_b, f"{tag}_conv")
    u3 = _ln_silu(u2, cln, f"{tag}_cln")
    y = _mm(u3, w_out, "nn", name=f"{tag}_out")
    x1 = _combine(x, y, mod, 2, lnp, 0, f"{tag}_ln")
    return x1, (x, h, p, u2, u3, y)


def _cf_bwd(dx1, saved, mod, lnp, wts, tag):
    w_in, dw_w, dw_b, cln, w_out = wts
    x, h, p, u2, u3, y = saved
    dxa, dy, dp = _combine_bwd(x, y, mod, 2, lnp, 0, dx1, f"{tag}_ln_b")
    du3 = _mm(dy, w_out, "nt", name=f"{tag}_out_bx")
    dw_out = _mm(u3, dy, "tn", name=f"{tag}_out_bw")
    du2, dcln = _ln_silu_bwd(u2, cln, du3, f"{tag}_cln_b")
    dval, dgate, ddw_w, ddw_b = _glu_conv_bwd(p, dw_w, du2, f"{tag}_conv_b")
    dpp = jnp.concatenate([dval, dgate], axis=1)
    dw_in = _mm(h, dpp, "tn", name=f"{tag}_in_bw")
    dh = _mm(dpp, w_in, "nt", name=f"{tag}_in_bx")
    dx, dss = _modulate_bwd(x, mod, 1, dh, dxa, f"{tag}_mod_b")
    return dx, (dw_in, ddw_w, ddw_b, dcln, dw_out), (dss, dp)


def _two_d(a):
    return a.reshape(-1, a.shape[-1])


def kernel(x, c, ada_w, ada_b, ln_g, ln_b, dn_w_in, dn_conv_w, dn_a_log, dn_dt_bias, dn_norm_w, dn_w_out, cf_w_in, cf_dw_w, cf_dw_b, cf_ln_g, cf_ln_b, cf_w_out, ff_w1, ff_w2, loss_target, m_ada_w, m_ada_b, m_ln_g, m_ln_b, m_dn_w_in, m_dn_conv_w, m_dn_a_log, m_dn_dt_bias, m_dn_norm_w, m_dn_w_out, m_cf_w_in, m_cf_dw_w, m_cf_dw_b, m_cf_ln_g, m_cf_ln_b, m_cf_w_out, m_ff_w1, m_ff_w2, v_ada_w, v_ada_b, v_ln_g, v_ln_b, v_dn_w_in, v_dn_conv_w, v_dn_a_log, v_dn_dt_bias, v_dn_norm_w, v_dn_w_out, v_cf_w_in, v_cf_dw_w, v_cf_dw_b, v_cf_ln_g, v_cf_ln_b, v_cf_w_out, v_ff_w1, v_ff_w2):
    ix, iy, ic = lax.axis_index("x"), lax.axis_index("y"), lax.axis_index("c")
    chip = 2 * ix + iy
    dev = 4 * ix + 2 * iy + ic
    S, D = x.shape[1], x.shape[2]
    L = ada_w.shape[0]
    LA, LB = dn_w_in.shape[0], cf_w_in.shape[0]
    H = dn_a_log.shape[1]
    NMOD = ada_b.shape[1] // D
    dn_in = dn_w_in.shape[2] * 4
    n_main = dn_in - 2 * H
    assert L == N_LAYERS and 2 * H <= LANES
    x0, tgt = x[0], loss_target[0]

    small_sharded = [ln_g, ln_b, dn_conv_w, cf_dw_w, cf_dw_b, cf_ln_g, cf_ln_b]
    small_axes = [2, 2, 2, 2, 1, 1, 1]
    packed_small = _pack(small_sharded, rows=8)[None]
    def my_layers(w):
        half = w.shape[0] // 2
        return lax.dynamic_slice_in_dim(w, ic * half, half, 0).astype(_MXU)

    h_dn_in, h_dn_out, h_cf_in, h_cf_out, h_w1, h_w2, g_small = _exchange(
        [(my_layers(dn_w_in)[None], 0), (my_layers(dn_w_out), 1), (my_layers(cf_w_in), 2),
         (my_layers(cf_w_out), 1), (my_layers(ff_w1), 2), (my_layers(ff_w2), 1), (packed_small, 0)],
        "xy", "gather", "gather_weights")
    g_dn_in, g_dn_out, g_cf_in, g_cf_out, g_w1, g_w2 = _exchange(
        [(h_dn_in, 1), (h_dn_out, 0), (h_cf_in, 0), (h_cf_out, 0), (h_w1, 0), (h_w2, 0)],
        "c", "gather", "share_weights")
    w_dn_in = jnp.transpose(g_dn_in, (1, 2, 0, 3)).reshape(LA, D, dn_in)
    w_dn_main = w_dn_in[:, :, :n_main]
    w_dn_small = jnp.pad(w_dn_in[:, :, n_main:], ((0, 0), (0, 0), (0, LANES - 2 * H)))
    shard_shapes = [a.shape for a in small_sharded]
    per_chip = [_unpack(g_small[q], shard_shapes) for q in range(4)]
    ln_g_f, ln_b_f, conv_w_f, dw_w_f, dw_b_f, cln_g_f, cln_b_f = [
        jnp.concatenate([per_chip[q][i] for q in range(4)], axis=small_axes[i]) for i in range(len(small_sharded))]

    c_all = _exchange([(c[None], 0)], "all", "gather", "gather_cond")[0].reshape(8, D)
    c_pad = jnp.pad(c_all, ((0, 8), (0, 0)))
    mod_sh = jnp.stack([_mm(c_pad, (ada_w, l), "nn", name=f"ada_{l}", a_fn=lambda t: t * _sigmoid(t))
                        for l in range(L)])
    mod_all = _exchange([(mod_sh, 2)], "xy", "gather", "gather_mod")[0]
    mod_mine = lax.dynamic_index_in_dim(mod_all, dev, axis=1, keepdims=False) + ada_b
    mods = mod_mine.reshape(L, NMOD, D)

    def lnp_of(l):
        return jnp.stack([ln_g_f[l, 0], ln_b_f[l, 0], ln_g_f[l, 1], ln_b_f[l, 1]])

    def dn_wts(j):
        prm = jnp.zeros((2, LANES), F32).at[0, H:2 * H].set(dn_a_log[j]).at[1, H:2 * H].set(dn_dt_bias[j])
        return ((w_dn_main, j), (w_dn_small, j), conv_w_f[j], prm, dn_norm_w[j][None], (g_dn_out, j))

    def cf_wts(j):
        return ((g_cf_in, j), dw_w_f[j], dw_b_f[j][None], jnp.stack([cln_g_f[j], cln_b_f[j]]), (g_cf_out, j))

    xs = x0
    saved = []
    for l in range(L):
        j = l // 2
        if l % 2 == 0:
            xs, sv_a = _dn_fwd(xs, mods[l], lnp_of(l), dn_wts(j), H, f"l{l}_dn")
        else:
            xs, sv_a = _cf_fwd(xs, mods[l], lnp_of(l), cf_wts(j), f"l{l}_cf")
        xs, sv_b = _mlp_fwd(xs, mods[l], lnp_of(l), (g_w1, l), (g_w2, l), f"l{l}_ff")
        saved.append((sv_a, sv_b))
    dx, loss_local = _loss_head(xs, tgt, "loss_head")
    loss = lax.psum(loss_local[0, 0], ("x", "y", "c"))

    gw1, gw2 = [None] * L, [None] * L
    g_dn = [None] * LA
    g_cf = [None] * LB
    dmods, dlns = [None] * L, [None] * L
    for l in reversed(range(L)):
        j = l // 2
        sv_a, sv_b = saved[l]
        dx, gw1[l], gw2[l], (dss2, dp2) = _mlp_bwd(dx, sv_b, mods[l], lnp_of(l), (g_w1, l), (g_w2, l), f"l{l}_ff")
        if l % 2 == 0:
            dx, g_dn[j], (dss1, dp1) = _dn_bwd(dx, sv_a, mods[l], lnp_of(l), dn_wts(j), H, f"l{l}_dn")
        else:
            dx, g_cf[j], (dss1, dp1) = _cf_bwd(dx, sv_a, mods[l], lnp_of(l), cf_wts(j), f"l{l}_cf")
        dmods[l] = jnp.concatenate([dss1, dp1[0:1], dss2, dp2[0:1]], axis=0)
        dlns[l] = (jnp.stack([dp1[1], dp2[1]]), jnp.stack([dp1[2], dp2[2]]))
    grad_x = dx[None]

    d_ln_g = jnp.stack([dlns[l][0] for l in range(L)])
    d_ln_b = jnp.stack([dlns[l][1] for l in range(L)])
    d_conv_w = jnp.stack([g_dn[j][2] for j in range(LA)])
    d_a_log = jnp.stack([g_dn[j][3][0, H:2 * H] for j in range(LA)])
    d_dt_bias = jnp.stack([g_dn[j][3][1, H:2 * H] for j in range(LA)])
    d_norm_w = jnp.stack([g_dn[j][4][0] for j in range(LA)])
    d_dw_w = jnp.stack([g_cf[j][1] for j in range(LB)])
    d_dw_b = jnp.stack([g_cf[j][2][0] for j in range(LB)])
    d_cln_g = jnp.stack([g_cf[j][3][0] for j in range(LB)])
    d_cln_b = jnp.stack([g_cf[j][3][1] for j in range(LB)])
    d_mod = jnp.stack(dmods).reshape(L, NMOD * D)
    small_full = [d_mod, d_ln_g, d_ln_b, d_conv_w, d_dw_w, d_dw_b, d_cln_g, d_cln_b, d_a_log, d_dt_bias, d_norm_w]
    small_all = _exchange([(_pack(small_full, rows=8)[None], 0)], "all", "gather", "gather_small_grads")[0]
    small_sum = _sum_slots(small_all, "sum_small_grads")
    (s_ada_b, s_ln_g, s_ln_b, s_conv_w, s_dw_w, s_dw_b, s_cln_g, s_cln_b, s_a_log, s_dt_bias, s_norm_w) = _unpack(
        small_sum, [a.shape for a in small_full])
    d_mod_all = small_all.reshape(8, -1)[:, :L * NMOD * D].reshape(8, L, NMOD * D)

    def shard(a, axis):
        size = a.shape[axis] // 4
        return lax.dynamic_slice_in_dim(a, chip * size, size, axis)

    ncol = ada_w.shape[2]
    d_mod_sh = jnp.pad(lax.dynamic_slice_in_dim(d_mod_all, chip * ncol, ncol, 2), ((0, 8), (0, 0), (0, 0)))
    g_ada_w = jnp.stack([_mm(c_pad, d_mod_sh[:, l], "tn", name=f"ada_bw_{l}", a_fn=lambda t: t * _sigmoid(t))
                         for l in range(L)])

    dn_in_full = jnp.stack([jnp.concatenate([g_dn[j][0], g_dn[j][1][:, :2 * H]], axis=1) for j in range(LA)])
    dn_in_st = jnp.transpose(dn_in_full.reshape(LA, D, 4, dn_in // 4), (2, 0, 1, 3))
    big = [(dn_in_st, 0), (jnp.stack([g_dn[j][5] for j in range(LA)]), 1),
           (jnp.stack([g_cf[j][0] for j in range(LB)]), 2), (jnp.stack([g_cf[j][4] for j in range(LB)]), 1),
           (jnp.stack(gw1), 2), (jnp.stack(gw2), 1)]
    layer_axis = [1, 0, 0, 0, 0, 0]
    pair = _exchange([(g, la) for (g, _), la in zip(big, layer_axis)], "c", "scatter", "pair_grads")
    pair_sums = [_sum_slots(r.reshape(2, -1, r.shape[-1]), f"sum_pair_{i}", _MXU).reshape(r.shape[1:])
                 for i, r in enumerate(pair)]
    recv = _exchange([(s, ax) for s, (_, ax) in zip(pair_sums, big)], "xy", "scatter", "scatter_grads")
    sums = [_sum_slots(r.reshape(4, -1, r.shape[-1]), f"sum_grads_{i}") for i, r in enumerate(recv)]
    halves = _exchange([(s[None], 0) for s in sums], "c", "gather", "swap_sums")
    both = [hv.reshape(1, -1, hv.shape[-1]) for hv in halves]

    big_w = [(dn_w_in, m_dn_w_in, v_dn_w_in), (dn_w_out, m_dn_w_out, v_dn_w_out), (cf_w_in, m_cf_w_in, v_cf_w_in),
             (cf_w_out, m_cf_w_out, v_cf_w_out), (ff_w1, m_ff_w1, v_ff_w1), (ff_w2, m_ff_w2, v_ff_w2)]
    big_out = []
    for i, ((w, m, v), st) in enumerate(zip(big_w, both)):
        res = _adamw(_two_d(w), _two_d(m), _two_d(v), st, f"adamw_big_{i}")
        big_out.append([r.reshape(w.shape) for r in res])
    ada_out = [r.reshape(ada_w.shape) for r in _adamw(_two_d(ada_w), _two_d(m_ada_w), _two_d(v_ada_w),
                                                     _two_d(g_ada_w)[None], "adamw_ada_w")]

    small_w = [(ada_b, m_ada_b, v_ada_b, s_ada_b), (ln_g, m_ln_g, v_ln_g, shard(s_ln_g, 2)),
               (ln_b, m_ln_b, v_ln_b, shard(s_ln_b, 2)), (dn_conv_w, m_dn_conv_w, v_dn_conv_w, shard(s_conv_w, 2)),
               (dn_a_log, m_dn_a_log, v_dn_a_log, s_a_log), (dn_dt_bias, m_dn_dt_bias, v_dn_dt_bias, s_dt_bias),
               (dn_norm_w, m_dn_norm_w, v_dn_norm_w, s_norm_w), (cf_dw_w, m_cf_dw_w, v_cf_dw_w, shard(s_dw_w, 2)),
               (cf_dw_b, m_cf_dw_b, v_cf_dw_b, shard(s_dw_b, 1)), (cf_ln_g, m_cf_ln_g, v_cf_ln_g, shard(s_cln_g, 1)),
               (cf_ln_b, m_cf_ln_b, v_cf_ln_b, shard(s_cln_b, 1))]
    pk = [_pack([t[i] for t in small_w], rows=8) for i in range(4)]
    small_res = _adamw(pk[0], pk[1], pk[2], pk[3][None], "adamw_small")
    small_shapes = [t[0].shape for t in small_w]
    small_out = [_unpack(r, small_shapes) for r in small_res]

    def kind(k):
        sm = small_out[k]
        bg = [o[k] for o in big_out]
        return [ada_out[k], sm[0], sm[1], sm[2], bg[0], sm[3], sm[4], sm[5], sm[6], bg[1],
                bg[2], sm[7], sm[8], sm[9], sm[10], bg[3], bg[4], bg[5]]

    return (loss, grad_x, *kind(0), *kind(1), *kind(2), *kind(3))
```

```python
import functools

import jax
import jax.numpy as jnp
from jax import lax
from jax.experimental import pallas as pl
from jax.experimental.pallas import tpu as pltpu

F32 = jnp.float32
_MXU = jnp.bfloat16
_HI = lax.Precision.HIGHEST

N_LAYERS = 4
ALPHA = (2.0 * N_LAYERS) ** 0.25
LN_EPS = 1e-5
RMS_EPS = 1e-6
L2_EPS = 1e-6
CHUNK = 64
ADAM_LR, ADAM_B1, ADAM_B2, ADAM_EPS, ADAM_WD, ADAM_STEP = 0.001, 0.9, 0.999, 1e-08, 0.01, 10

LANES = 128
TOKEN_BLOCK = 256
VMEM_BIG = 48 * 1024 * 1024
MM_TILE = 1024

SDS = jax.ShapeDtypeStruct
MESH = pl.DeviceIdType.MESH


def _cparams(vmem=None):
    if vmem is None:
        return None
    return pltpu.CompilerParams(vmem_limit_bytes=vmem)


def _pcall(body, **kw):
    if kw.get("compiler_params", 1) is None:
        kw.pop("compiler_params")
    return pl.pallas_call(body, **kw)


def _full(arr):
    nd = arr.ndim
    return pl.BlockSpec(arr.shape, lambda *g: (0,) * nd)


def _bs(block, imap, lead=None):
    if lead is None:
        return pl.BlockSpec(block, imap)
    return pl.BlockSpec((None,) + tuple(block), lambda *g: (lead,) + tuple(imap(*g)))


def _split(a):
    return a if isinstance(a, tuple) else (a, None)


_GROUPS = {
    "xy": ([(1, 0, 0), (0, 1, 0), (1, 1, 0)], 4),
    "c": ([(0, 0, 1)], 2),
    "all": ([(1, 0, 0), (0, 1, 0), (1, 1, 0), (0, 0, 1), (1, 0, 1), (0, 1, 1), (1, 1, 1)], 8),
}


def _exchange(items, group, mode, name, nsplit=1):
    masks, n = _GROUPS[group]
    npeer = len(masks)
    ni = len(items)
    arrs = [a for a, _ in items]
    out_shapes = []
    for a, ax in items:
        shp = list(a.shape)
        if mode == "gather":
            shp[ax] *= n
        else:
            shp[ax] //= n
            shp = [n] + shp
        out_shapes.append(SDS(tuple(shp), a.dtype))

    def body(*refs):
        ins, outs = refs[:ni], refs[ni:2 * ni]
        send_sems, recv_sems, local_sems = refs[2 * ni:]
        x, y, c = lax.axis_index("x"), lax.axis_index("y"), lax.axis_index("c")

        def slot(px, py, pc):
            if group == "xy":
                return 2 * px + py
            if group == "c":
                return pc
            return 4 * px + 2 * py + pc

        me = slot(x, y, c)

        def block(ref, ax, idx, size):
            ix = (slice(None),) * ax + (pl.ds(pl.multiple_of(idx * size, size), size),)
            return ref.at[ix]

        copies = []
        for it, (a, ax) in enumerate(items):
            in_ref, out_ref = ins[it], outs[it]
            if mode == "gather":
                size = a.shape[ax]
                src_own, dst_own = in_ref, block(out_ref, ax, me, size)
            else:
                size = a.shape[ax] // n
                src_own, dst_own = block(in_ref, ax, me, size), out_ref.at[me]
            own = pltpu.make_async_copy(src_own, dst_own, local_sems.at[it])
            own.start()
            copies.append(own)
            sax, ns, cs = splits[it]
            for k, m in enumerate(masks):
                peer = tuple((1 - v) if b else v for v, b in zip((x, y, c), m))
                if mode == "gather":
                    src, dst = in_ref, dst_own
                else:
                    src, dst = block(in_ref, ax, slot(*peer), size), out_ref.at[me]
                for j in range(ns):
                    piece = (slice(None),) * sax + (pl.ds(j * cs, cs),)
                    sem = (it * npeer + k) * nsplit + j
                    cp = pltpu.make_async_remote_copy(
                        src_ref=src.at[piece], dst_ref=dst.at[piece], send_sem=send_sems.at[sem],
                        recv_sem=recv_sems.at[sem], device_id=peer, device_id_type=MESH)
                    cp.start()
                    copies.append(cp)
        for cp in copies:
            cp.wait()

    splits = []
    for (a, ax), o in zip(items, out_shapes):
        bshape = a.shape if mode == "gather" else o.shape[1:]
        sax = max(range(len(bshape) - 1), key=lambda d: bshape[d])
        ns = nsplit if bshape[sax] % (nsplit * 16) == 0 else 1
        splits.append((sax, ns, bshape[sax] // ns))
    any_spec = pl.BlockSpec(memory_space=pl.ANY)
    nsem = ni * npeer * nsplit
    outs = _pcall(
        body, name=name, out_shape=tuple(out_shapes),
        in_specs=[any_spec] * ni, out_specs=tuple([any_spec] * ni),
        scratch_shapes=[pltpu.SemaphoreType.DMA((nsem,)), pltpu.SemaphoreType.DMA((nsem,)),
                        pltpu.SemaphoreType.DMA((ni,))],
    )(*arrs)
    return list(outs)


def _tile(n, cap):
    if n <= cap:
        return n
    t = cap - cap % LANES
    while n % t:
        t -= LANES
    return t


def _mm(a, b, mode, *, name, out_dtypes=(F32,), tm=MM_TILE, tn=MM_TILE, tk=MM_TILE, a_fn=None, out_fn=None, aux=(),
        out_imap=None, out_shape=None):
    (a, a_lead), (b, b_lead) = _split(a), _split(b)
    ash, bsh = a.shape[-2:], b.shape[-2:]
    if mode == "nn":
        (M, K), (_, N) = ash, bsh
    elif mode == "nt":
        (M, K), (N, _) = ash, bsh
    else:
        (K, M), (_, N) = ash, bsh
    tm, tn, tk = _tile(M, tm), _tile(N, tn), _tile(K, tk)
    nk = K // tk
    if mode == "tn":
        a_spec = _bs((tk, tm), lambda i, j, k: (k, i), a_lead)
    else:
        a_spec = _bs((tm, tk), lambda i, j, k: (i, k), a_lead)
    if mode == "nt":
        b_spec = _bs((tn, tk), lambda i, j, k: (j, k), b_lead)
    else:
        b_spec = _bs((tk, tn), lambda i, j, k: (k, j), b_lead)
    aux_arrs, aux_specs = [], []
    for arr, kind in aux:
        arr, lead = _split(arr)
        aux_arrs.append(arr)
        if kind == "mn":
            aux_specs.append(_bs((tm, tn), lambda i, j, k: (i, j), lead))
        else:
            aux_specs.append(_bs((1, tn), lambda i, j, k: (0, j), lead))
    na, no = len(aux_arrs), len(out_dtypes)
    dims = {"nn": (((1,), (0,)), ((), ())), "nt": (((1,), (1,)), ((), ())), "tn": (((0,), (0,)), ((), ()))}[mode]

    def finish(r, aux_refs, o_refs):
        outs = out_fn(r, *[x[...] for x in aux_refs]) if out_fn is not None else (r,)
        for o_ref, val in zip(o_refs, outs):
            o_ref[...] = val.astype(o_ref.dtype)

    def product(a_ref, b_ref):
        av = a_ref[...]
        if a_fn is not None:
            av = a_fn(av.astype(F32))
        return lax.dot_general(av.astype(_MXU), b_ref[...].astype(_MXU), dims, preferred_element_type=F32)

    def body_one(a_ref, b_ref, *rest):
        finish(product(a_ref, b_ref), rest[:na], rest[na:na + no])

    def body_acc(a_ref, b_ref, *rest):
        aux_refs, o_refs, acc = rest[:na], rest[na:na + no], rest[na + no]
        k = pl.program_id(2)

        @pl.when(k == 0)
        def _():
            acc[...] = product(a_ref, b_ref)

        @pl.when(k != 0)
        def _():
            acc[...] += product(a_ref, b_ref)

        @pl.when(k == nk - 1)
        def _():
            finish(acc[...], aux_refs, o_refs)

    if out_shape is None:
        out_shape = (M, N)
    if out_imap is None:
        o_spec = pl.BlockSpec((tm, tn), lambda i, j, k: (i, j))
    else:
        o_spec = pl.BlockSpec((None,) * (len(out_shape) - 2) + (tm, tn), out_imap)
    outs = _pcall(
        body_one if nk == 1 else body_acc, name=name, grid=(M // tm, N // tn, nk),
        in_specs=[a_spec, b_spec] + aux_specs, out_specs=tuple([o_spec] * no),
        out_shape=tuple(SDS(out_shape, dt) for dt in out_dtypes),
        scratch_shapes=[] if nk == 1 else [pltpu.VMEM((tm, tn), F32)],
        compiler_params=pltpu.CompilerParams(dimension_semantics=("parallel", "parallel", "arbitrary"),
                                             vmem_limit_bytes=VMEM_BIG),
    )(a, b, *aux_arrs)
    return outs[0] if no == 1 else outs


def _tok(S):
    ts = min(TOKEN_BLOCK, S)
    assert S % ts == 0
    return ts


def _row(ts, D):
    return pl.BlockSpec((ts, D), lambda i: (i, 0))


def _acc_rows(ref, i, rows):
    @pl.when(i == 0)
    def _():
        for r, v in enumerate(rows):
            ref[r:r + 1, :] = v

    @pl.when(i != 0)
    def _():
        for r, v in enumerate(rows):
            ref[r:r + 1, :] += v


def _modulate(x, mod, r_sh, r_sc, name):
    S, D = x.shape
    ts = _tok(S)

    def body(x_ref, m_ref, o_ref):
        o_ref[...] = (x_ref[...] * (1.0 + m_ref[r_sc:r_sc + 1, :]) + m_ref[r_sh:r_sh + 1, :]).astype(o_ref.dtype)

    return _pcall(body, name=name, grid=(S // ts,), in_specs=[_row(ts, D), _full(mod)],
                  out_specs=_row(ts, D), out_shape=SDS((S, D), _MXU))(x, mod)


def _modulate_bwd(x, mod, r_sc, dh, dxa, name):
    S, D = x.shape
    ts = _tok(S)

    def body(x_ref, m_ref, dh_ref, dxa_ref, dx_ref, dss_ref):
        dh_v = dh_ref[...]
        dx_ref[...] = dxa_ref[...] + dh_v * (1.0 + m_ref[r_sc:r_sc + 1, :])
        _acc_rows(dss_ref, pl.program_id(0),
                  [jnp.sum(dh_v, axis=0, keepdims=True), jnp.sum(dh_v * x_ref[...], axis=0, keepdims=True)])

    return _pcall(body, name=name, grid=(S // ts,),
                  in_specs=[_row(ts, D), _full(mod), _row(ts, D), _row(ts, D)],
                  out_specs=(_row(ts, D), pl.BlockSpec((2, D), lambda i: (0, 0))),
                  out_shape=(SDS((S, D), F32), SDS((2, D), F32)))(x, mod, dh, dxa)


def _norm_stats(z):
    mu = jnp.mean(z, axis=-1, keepdims=True)
    zc = z - mu
    var = jnp.mean(zc * zc, axis=-1, keepdims=True)
    rstd = lax.rsqrt(var + LN_EPS)
    return zc * rstd, rstd


def _norm_bwd(dxhat, xhat, rstd):
    return rstd * (dxhat - jnp.mean(dxhat, axis=-1, keepdims=True)
                   - xhat * jnp.mean(dxhat * xhat, axis=-1, keepdims=True))


def _combine(x, y, mod, r_gt, lnp, r_g, name):
    S, D = x.shape
    ts = _tok(S)

    def body(x_ref, y_ref, m_ref, l_ref, o_ref):
        z = ALPHA * x_ref[...] + (1.0 + m_ref[r_gt:r_gt + 1, :]) * y_ref[...]
        xhat, _ = _norm_stats(z)
        o_ref[...] = xhat * l_ref[r_g:r_g + 1, :] + l_ref[r_g + 1:r_g + 2, :]

    return _pcall(body, name=name, grid=(S // ts,), in_specs=[_row(ts, D), _row(ts, D), _full(mod), _full(lnp)],
                  out_specs=_row(ts, D), out_shape=SDS((S, D), F32))(x, y, mod, lnp)


def _combine_bwd(x, y, mod, r_gt, lnp, r_g, dout, name):
    S, D = x.shape
    ts = _tok(S)

    def body(x_ref, y_ref, m_ref, l_ref, do_ref, dxa_ref, dy_ref, dp_ref):
        gate = 1.0 + m_ref[r_gt:r_gt + 1, :]
        y_v, do_v = y_ref[...], do_ref[...]
        xhat, rstd = _norm_stats(ALPHA * x_ref[...] + gate * y_v)
        dz = _norm_bwd(do_v * l_ref[r_g:r_g + 1, :], xhat, rstd)
        dxa_ref[...] = ALPHA * dz
        dy_ref[...] = (gate * dz).astype(dy_ref.dtype)
        _acc_rows(dp_ref, pl.program_id(0),
                  [jnp.sum(dz * y_v, axis=0, keepdims=True), jnp.sum(do_v * xhat, axis=0, keepdims=True),
                   jnp.sum(do_v, axis=0, keepdims=True)])

    return _pcall(body, name=name, grid=(S // ts,),
                  in_specs=[_row(ts, D), _row(ts, D), _full(mod), _full(lnp), _row(ts, D)],
                  out_specs=(_row(ts, D), _row(ts, D), pl.BlockSpec((3, D), lambda i: (0, 0))),
                  out_shape=(SDS((S, D), F32), SDS((S, D), _MXU), SDS((3, D), F32)))(x, y, mod, lnp, dout)


def _sigmoid(t):
    return 1.0 / (1.0 + jnp.exp(-t))


def _ln_silu(u, lnp, name):
    S, D = u.shape
    ts = _tok(S)

    def body(u_ref, l_ref, o_ref):
        xhat, _ = _norm_stats(u_ref[...])
        t = xhat * l_ref[0:1, :] + l_ref[1:2, :]
        o_ref[...] = (t * _sigmoid(t)).astype(o_ref.dtype)

    return _pcall(body, name=name, grid=(S // ts,), in_specs=[_row(ts, D), _full(lnp)],
                  out_specs=_row(ts, D), out_shape=SDS((S, D), _MXU))(u, lnp)


def _ln_silu_bwd(u, lnp, dout, name):
    S, D = u.shape
    ts = _tok(S)

    def body(u_ref, l_ref, do_ref, du_ref, dp_ref):
        xhat, rstd = _norm_stats(u_ref[...])
        g = l_ref[0:1, :]
        t = xhat * g + l_ref[1:2, :]
        sg = _sigmoid(t)
        dt = do_ref[...] * (sg * (1.0 + t * (1.0 - sg)))
        du_ref[...] = _norm_bwd(dt * g, xhat, rstd)
        _acc_rows(dp_ref, pl.program_id(0),
                  [jnp.sum(dt * xhat, axis=0, keepdims=True), jnp.sum(dt, axis=0, keepdims=True)])

    return _pcall(body, name=name, grid=(S // ts,), in_specs=[_row(ts, D), _full(lnp), _row(ts, D)],
                  out_specs=(_row(ts, D), pl.BlockSpec((2, D), lambda i: (0, 0))),
                  out_shape=(SDS((S, D), F32), SDS((2, D), F32)))(u, lnp, dout)


def _loss_head(xf, tgt, name):
    S, D = xf.shape
    ts = _tok(S)

    def body(x_ref, t_ref, dx_ref, l_ref):
        err = x_ref[...] - t_ref[...]
        dx_ref[...] = err * (1.0 / D)
        part = jnp.sum(jnp.sum(err * err, axis=1, keepdims=True), axis=0, keepdims=True) * (0.5 / D)

        @pl.when(pl.program_id(0) == 0)
        def _():
            l_ref[...] = part

        @pl.when(pl.program_id(0) != 0)
        def _():
            l_ref[...] += part

    return _pcall(body, name=name, grid=(S // ts,), in_specs=[_row(ts, D), _row(ts, D)],
                  out_specs=(_row(ts, D), pl.BlockSpec((1, 1), lambda i: (0, 0))),
                  out_shape=(SDS((S, D), F32), SDS((1, 1), F32)))(xf, tgt)


def _shift_down(u, s, rows):
    if s == 0:
        return u
    return jnp.where(rows >= s, pltpu.roll(u, s, 0), 0.0)


def _shift_up(u, s, rows):
    if s == 0:
        return u
    n = u.shape[0]
    return jnp.where(rows < n - s, pltpu.roll(u, n - s, 0), 0.0)


def _dwconv(u, w_ref, taps, rows):
    acc = jnp.zeros_like(u)
    for j in range(taps):
        acc = acc + w_ref[j:j + 1, :] * _shift_down(u, taps - 1 - j, rows)
    return acc


def _dwconv_bwd(u, dy, w_ref, dw_ref, taps, rows):
    du = jnp.zeros_like(u)
    for j in range(taps):
        s = taps - 1 - j
        du = du + w_ref[j:j + 1, :] * _shift_up(dy, s, rows)
        dw_ref[j:j + 1, :] = jnp.sum(dy * _shift_down(u, s, rows), axis=0, keepdims=True)
    return du


def _col(S, j0=0):
    return pl.BlockSpec((S, LANES), lambda j: (0, j + j0))


def _conv_silu(pm, w, nblk, name):
    S = pm.shape[0]
    taps = w.shape[0]

    def body(u_ref, w_ref, o_ref):
        rows = lax.broadcasted_iota(jnp.int32, (S, LANES), 0)
        cv = _dwconv(u_ref[...], w_ref, taps, rows)
        o_ref[...] = cv * _sigmoid(cv)

    return _pcall(body, name=name, grid=(nblk,),
                  in_specs=[_col(S), pl.BlockSpec((taps, LANES), lambda j: (0, j))],
                  out_specs=_col(S), out_shape=SDS((S, nblk * LANES), F32),
                  compiler_params=_cparams(VMEM_BIG))(pm, w)


def _conv_silu_bwd(pm, w, dout, dpm, j0, name):
    S = pm.shape[0]
    taps = w.shape[0]
    nblk = dout.shape[1] // LANES

    def body(u_ref, w_ref, do_ref, dpm_in, du_ref, dw_ref):
        del dpm_in
        rows = lax.broadcasted_iota(jnp.int32, (S, LANES), 0)
        u = u_ref[...]
        cv = _dwconv(u, w_ref, taps, rows)
        sg = _sigmoid(cv)
        dc = do_ref[...] * (sg * (1.0 + cv * (1.0 - sg)))
        du_ref[...] = _dwconv_bwd(u, dc, w_ref, dw_ref, taps, rows)

    return _pcall(body, name=name, grid=(nblk,),
                  in_specs=[_col(S, j0), pl.BlockSpec((taps, LANES), lambda j: (0, j + j0)), _col(S),
                            pl.BlockSpec(memory_space=pl.ANY)],
                  out_specs=(_col(S, j0), pl.BlockSpec((taps, LANES), lambda j: (0, j))),
                  out_shape=(SDS(dpm.shape, F32), SDS((taps, nblk * LANES), F32)),
                  input_output_aliases={3: 0},
                  compiler_params=_cparams(VMEM_BIG))(pm, w, dout, dpm)


def _glu_conv(p, w, bias, name):
    S, C2 = p.shape
    nblk = C2 // 2 // LANES
    taps = w.shape[0]

    def body(v_ref, g_ref, w_ref, b_ref, o_ref):
        rows = lax.broadcasted_iota(jnp.int32, (S, LANES), 0)
        u = v_ref[...] * _sigmoid(g_ref[...])
        o_ref[...] = _dwconv(u, w_ref, taps, rows) + b_ref[...]

    return _pcall(body, name=name, grid=(nblk,),
                  in_specs=[_col(S), _col(S, nblk), pl.BlockSpec((taps, LANES), lambda j: (0, j)),
                            pl.BlockSpec((1, LANES), lambda j: (0, j))],
                  out_specs=_col(S), out_shape=SDS((S, nblk * LANES), F32),
                  compiler_params=_cparams(VMEM_BIG))(p, p, w, bias)


def _glu_conv_bwd(p, w, dout, name):
    S, C2 = p.shape
    nblk = C2 // 2 // LANES
    taps = w.shape[0]

    def body(v_ref, g_ref, w_ref, do_ref, dv_ref, dg_ref, dw_ref, db_ref):
        rows = lax.broadcasted_iota(jnp.int32, (S, LANES), 0)
        val, sg = v_ref[...], _sigmoid(g_ref[...])
        do_v = do_ref[...]
        du = _dwconv_bwd(val * sg, do_v, w_ref, dw_ref, taps, rows)
        dv_ref[...] = du * sg
        dg_ref[...] = du * val * sg * (1.0 - sg)
        db_ref[...] = jnp.sum(do_v, axis=0, keepdims=True)

    dval, dgate, dw, db = _pcall(
        body, name=name, grid=(nblk,),
        in_specs=[_col(S), _col(S, nblk), pl.BlockSpec((taps, LANES), lambda j: (0, j)), _col(S)],
        out_specs=(_col(S), _col(S), pl.BlockSpec((taps, LANES), lambda j: (0, j)),
                   pl.BlockSpec((1, LANES), lambda j: (0, j))),
        out_shape=(SDS((S, C2 // 2), F32), SDS((S, C2 // 2), F32), SDS((taps, C2 // 2), F32), SDS((1, C2 // 2), F32)),
        compiler_params=_cparams(VMEM_BIG))(p, p, w, dout)
    return dval, dgate, dw, db


def _log1p(e):
    u = 1.0 + e
    d = jnp.where(u == 1.0, 1.0, u - 1.0)
    return jnp.where(u == 1.0, e, jnp.log(u) * (e / d))


def _gate_parts(ps, prm, H):
    lane = lax.broadcasted_iota(jnp.int32, ps.shape, 1)
    is_b, is_g = lane < H, (lane >= H) & (lane < 2 * H)
    beta = _sigmoid(ps)
    t = ps + prm[1:2, :]
    sp = jnp.maximum(t, 0.0) + _log1p(jnp.exp(-jnp.abs(t)))
    na = -jnp.exp(prm[0:1, :])
    return is_b, is_g, beta, t, sp, na


def _gates(ps, prm, H, name):
    S = ps.shape[0]
    ts = _tok(S)

    def body(p_ref, r_ref, o_ref):
        is_b, is_g, beta, _, sp, na = _gate_parts(p_ref[...], r_ref[...], H)
        o_ref[...] = jnp.where(is_b, beta, jnp.where(is_g, na * sp, 0.0))

    return _pcall(body, name=name, grid=(S // ts,), in_specs=[_row(ts, LANES), _full(prm)],
                  out_specs=_row(ts, LANES), out_shape=SDS((S, LANES), F32))(ps, prm)


def _gates_bwd(ps, prm, dgates, H, name):
    S = ps.shape[0]
    ts = _tok(S)

    def body(p_ref, r_ref, dg_ref, dp_ref, dr_ref):
        is_b, is_g, beta, t, sp, na = _gate_parts(p_ref[...], r_ref[...], H)
        dg_v = dg_ref[...]
        dsp = jnp.where(is_g, dg_v * na * _sigmoid(t), 0.0)
        dp_ref[...] = jnp.where(is_b, dg_v * beta * (1.0 - beta), dsp)
        _acc_rows(dr_ref, pl.program_id(0),
                  [jnp.sum(jnp.where(is_g, dg_v * na * sp, 0.0), axis=0, keepdims=True),
                   jnp.sum(dsp, axis=0, keepdims=True)])

    return _pcall(body, name=name, grid=(S // ts,), in_specs=[_row(ts, LANES), _full(prm), _row(ts, LANES)],
                  out_specs=(_row(ts, LANES), pl.BlockSpec((2, LANES), lambda i: (0, 0))),
                  out_shape=(SDS((S, LANES), F32), SDS((2, LANES), F32)))(ps, prm, dgates)


_NN = (((2,), (1,)), ((0,), (0,)))
_NT = (((2,), (2,)), ((0,), (0,)))
_TN = (((1,), (1,)), ((0,), (0,)))


def _mdot(a, b, dims):
    return lax.dot_general(a.astype(_MXU), b.astype(_MXU), dims, preferred_element_type=F32)


def _mdot3(a, b, dims):
    ah, bh = a.astype(_MXU), b.astype(_MXU)
    al, bl = a - ah.astype(F32), b - bh.astype(F32)
    return _mdot(ah, bh, dims) + (_mdot(ah, bl, dims) + _mdot(al, bh, dims))


def _rounded_dot(dims, da_dims, db_dims, a_first, prod=_mdot):
    @jax.custom_vjp
    def f(a, b):
        return prod(a, b, dims)

    def fwd(a, b):
        return prod(a, b, dims), (a, b)

    def bwd(res, ct):
        a, b = res
        da = prod(ct, b, da_dims) if a_first[0] else prod(b, ct, da_dims)
        db = prod(ct, a, db_dims) if a_first[1] else prod(a, ct, db_dims)
        return da, db

    f.defvjp(fwd, bwd)
    return f


_mdot_nn = _rounded_dot(_NN, _NT, _TN, (True, False))
_mdot_nt = _rounded_dot(_NT, _NN, _TN, (True, True))
_mdot_tn = _rounded_dot(_TN, _NT, _NN, (False, False))
_dot = _rounded_dot(_NN, _NT, _TN, (True, False), _mdot3)


def _head_cols(gates, off, H):
    lane = lax.broadcasted_iota(jnp.int32, gates.shape, 1)
    return jnp.stack([jnp.sum(jnp.where(lane == off + h, gates, 0.0), axis=-1, keepdims=True) for h in range(H)])


def _delta_chunk(qr, kr, v, z, gates, nw, s_in):
    H, C, dk = qr.shape
    beta, g = _head_cols(gates, 0, H), _head_cols(gates, H, H)
    q = qr * lax.rsqrt(jnp.sum(qr * qr, axis=-1, keepdims=True) + L2_EPS) * (dk ** -0.5)
    k = kr * lax.rsqrt(jnp.sum(kr * kr, axis=-1, keepdims=True) + L2_EPS)
    ri = lax.broadcasted_iota(jnp.int32, (1, C, C), 1)
    ci = lax.broadcasted_iota(jnp.int32, (1, C, C), 2)
    causal, strict, eye = ri >= ci, ri > ci, ri == ci
    gam_row = jnp.sum(jnp.where(ri <= ci, g, 0.0), axis=1, keepdims=True)
    gam_col = jnp.sum(jnp.where(eye, gam_row, 0.0), axis=-1, keepdims=True)
    g_last = jnp.sum(g, axis=1, keepdims=True)
    decay = jnp.where(causal, jnp.exp(jnp.where(causal, gam_col - gam_row, 0.0)), 0.0)
    kb = k * beta
    a = jnp.where(strict, _mdot_nt(kb, k) * decay, 0.0)
    t_inv = jnp.where(eye, 1.0, 0.0) - a
    p = a
    for _ in range(max(C.bit_length() - 2, 0)):
        p = _dot(p, p)
        t_inv = t_inv + _dot(t_inv, p)
    eg = jnp.exp(gam_col)
    u = _mdot_nn(t_inv, v * beta)
    w = _mdot_nn(t_inv, kb * eg)
    a_qk = _mdot_nt(q, k) * decay
    v_new = u - _mdot_nn(w, s_in)
    o = _mdot_nn(q * eg, s_in) + _mdot_nn(a_qk, v_new)
    s_out = s_in * jnp.exp(g_last) + _mdot_tn(k * jnp.exp(g_last - gam_col), v_new)
    og = o * lax.rsqrt(jnp.mean(o * o, axis=-1, keepdims=True) + RMS_EPS) * nw * (z * _sigmoid(z))
    return og, s_out


def _heads(ref, H, dk):
    return jnp.stack([ref[:, h * dk:(h + 1) * dk].astype(F32) for h in range(H)])


def _put_heads(ref, val, dk):
    for h in range(val.shape[0]):
        ref[:, h * dk:(h + 1) * dk] = val[h].astype(ref.dtype)


def _delta_fwd(qkv, pm, gates, nw, H, name):
    S = qkv.shape[0]
    hd = qkv.shape[1] // 3
    dk = hd // H
    N = S // CHUNK
    blk = lambda off: pl.BlockSpec((CHUNK, hd), lambda n: (n, off))

    def body(q_ref, k_ref, v_ref, z_ref, g_ref, nw_ref, og_ref, st_ref, s_scr):
        @pl.when(pl.program_id(0) == 0)
        def _():
            s_scr[...] = jnp.zeros_like(s_scr)

        s_in = s_scr[...]
        st_ref[...] = s_in
        og, s_out = _delta_chunk(_heads(q_ref, H, dk), _heads(k_ref, H, dk), _heads(v_ref, H, dk),
                                 _heads(z_ref, H, dk), g_ref[...], nw_ref[...], s_in)
        _put_heads(og_ref, og, dk)
        s_scr[...] = s_out

    return _pcall(
        body, name=name, grid=(N,),
        in_specs=[blk(0), blk(1), blk(2), blk(3), pl.BlockSpec((CHUNK, LANES), lambda n: (n, 0)), _full(nw)],
        out_specs=(blk(0), pl.BlockSpec((None, H, dk, dk), lambda n: (n, 0, 0, 0))),
        out_shape=(SDS((S, hd), _MXU), SDS((N, H, dk, dk), F32)),
        scratch_shapes=[pltpu.VMEM((H, dk, dk), F32)],
        compiler_params=_cparams(VMEM_BIG),
    )(qkv, qkv, qkv, pm, gates, nw)


def _delta_bwd(qkv, pm, gates, nw, states, dog, H, name):
    S = qkv.shape[0]
    hd = qkv.shape[1] // 3
    dk = hd // H
    N = S // CHUNK
    blk = lambda off: pl.BlockSpec((CHUNK, hd), lambda n: (N - 1 - n, off))
    gspec = pl.BlockSpec((CHUNK, LANES), lambda n: (N - 1 - n, 0))

    def body(q_ref, k_ref, v_ref, z_ref, g_ref, nw_ref, st_ref, do_ref,
             dq_ref, dk_ref, dv_ref, dz_ref, dg_ref, dnw_ref, ds_scr):
        n = pl.program_id(0)

        @pl.when(n == 0)
        def _():
            ds_scr[...] = jnp.zeros_like(ds_scr)

        _, vjp = jax.vjp(_delta_chunk, _heads(q_ref, H, dk), _heads(k_ref, H, dk), _heads(v_ref, H, dk),
                         _heads(z_ref, H, dk), g_ref[...], nw_ref[...], st_ref[...])
        dq, dkk, dv, dz, dg, dnw, ds_in = vjp((_heads(do_ref, H, dk), ds_scr[...]))
        _put_heads(dq_ref, dq, dk)
        _put_heads(dk_ref, dkk, dk)
        _put_heads(dv_ref, dv, dk)
        _put_heads(dz_ref, dz, dk)
        ds_scr[...] = ds_in
        dg_ref[...] = dg

        @pl.when(n == 0)
        def _():
            dnw_ref[...] = dnw

        @pl.when(n != 0)
        def _():
            dnw_ref[...] += dnw

    return _pcall(
        body, name=name, grid=(N,),
        in_specs=[blk(0), blk(1), blk(2), blk(3), gspec, _full(nw),
                  pl.BlockSpec((None, H, dk, dk), lambda n: (N - 1 - n, 0, 0, 0)), blk(0)],
        out_specs=(blk(0), blk(0), blk(0), blk(3), gspec, pl.BlockSpec((1, dk), lambda n: (0, 0))),
        out_shape=(SDS((S, hd), F32), SDS((S, hd), F32), SDS((S, hd), F32), SDS(pm.shape, F32),
                   SDS((S, LANES), F32), SDS((1, dk), F32)),
        scratch_shapes=[pltpu.VMEM((H, dk, dk), F32)],
        compiler_params=_cparams(VMEM_BIG),
    )(qkv, qkv, qkv, pm, gates, nw, states, dog)


def _rows_block(R, C):
    rb = R
    while rb * C * 4 > (1 << 20) and rb % 16 == 0:
        rb //= 2
    return rb


def _sum_slots(st, name, out_dtype=F32):
    n, R, C = st.shape
    rb = _rows_block(R, C)

    def body(s_ref, o_ref):
        acc = s_ref[0].astype(F32)
        for q in range(1, n):
            acc = acc + s_ref[q].astype(F32)
        o_ref[...] = acc.astype(o_ref.dtype)

    return _pcall(body, name=name, grid=(R // rb,), in_specs=[pl.BlockSpec((n, rb, C), lambda i: (0, i, 0))],
                  out_specs=pl.BlockSpec((rb, C), lambda i: (i, 0)), out_shape=SDS((R, C), out_dtype))(st)


def _adamw(w, m, v, st, name):
    R, C = w.shape
    n = st.shape[0]
    rb = _rows_block(R, C)
    spec = pl.BlockSpec((rb, C), lambda i: (i, 0))

    def body(w_ref, m_ref, v_ref, s_ref, g_ref, d_ref, mo_ref, vo_ref):
        g = s_ref[0]
        for q in range(1, n):
            g = g + s_ref[q]
        m_new = ADAM_B1 * m_ref[...] + (1.0 - ADAM_B1) * g
        v_new = ADAM_B2 * v_ref[...] + (1.0 - ADAM_B2) * (g * g)
        m_hat = m_new / (1.0 - ADAM_B1 ** ADAM_STEP)
        v_hat = v_new / (1.0 - ADAM_B2 ** ADAM_STEP)
        g_ref[...] = g
        d_ref[...] = -ADAM_LR * (m_hat / (jnp.sqrt(v_hat) + ADAM_EPS) + ADAM_WD * w_ref[...])
        mo_ref[...] = m_new
        vo_ref[...] = v_new

    return _pcall(body, name=name, grid=(R // rb,),
                  in_specs=[spec, spec, spec, pl.BlockSpec((n, rb, C), lambda i: (0, i, 0))],
                  out_specs=(spec,) * 4, out_shape=(SDS((R, C), F32),) * 4)(w, m, v, st)


def _pack(arrs, rows=1):
    flat = jnp.concatenate([a.reshape(-1).astype(F32) for a in arrs])
    quantum = rows * LANES
    pad = (-flat.shape[0]) % quantum
    flat = jnp.pad(flat, (0, pad))
    return flat.reshape(rows, -1)


def _unpack(flat, shapes):
    flat = flat.reshape(-1)
    out, off = [], 0
    for shp in shapes:
        size = 1
        for d in shp:
            size *= d
        out.append(flat[off:off + size].reshape(shp))
        off += size
    return out


def _mlp_fwd(x1, mod, lnp, w1, w2, tag):
    h2 = _modulate(x1, mod, 3, 4, f"{tag}_mod")
    a1, a2 = _mm(h2, w1, "nn", name=f"{tag}_up", out_dtypes=(_MXU, _MXU),
                 out_fn=lambda r: (r, jnp.square(jnp.maximum(r, 0.0))))
    y2 = _mm(a2, w2, "nn", name=f"{tag}_down")
    x2 = _combine(x1, y2, mod, 5, lnp, 2, f"{tag}_ln")
    return x2, (x1, h2, a1, a2, y2)


def _mlp_bwd(dx2, saved, mod, lnp, w1, w2, tag):
    x1, h2, a1, a2, y2 = saved
    dxa, dy2, dp = _combine_bwd(x1, y2, mod, 5, lnp, 2, dx2, f"{tag}_ln_b")
    da1 = _mm(dy2, w2, "nt", name=f"{tag}_down_bx", out_dtypes=(_MXU,), aux=[(a1, "mn")],
              out_fn=lambda r, a: (r * (2.0 * jnp.maximum(a.astype(F32), 0.0)),))
    dw2 = _mm(a2, dy2, "tn", name=f"{tag}_down_bw")
    dw1 = _mm(h2, da1, "tn", name=f"{tag}_up_bw")
    dh2 = _mm(da1, w1, "nt", name=f"{tag}_up_bx")
    dx1, dss = _modulate_bwd(x1, mod, 4, dh2, dxa, f"{tag}_mod_b")
    return dx1, dw1, dw2, (dss, dp)


def _dn_fwd(x, mod, lnp, wts, H, tag):
    w_main, w_small, conv_w, prm, nw, w_out = wts
    h = _modulate(x, mod, 0, 1, f"{tag}_mod")
    pm = _mm(h, w_main, "nn", name=f"{tag}_in")
    ps = _mm(h, w_small, "nn", name=f"{tag}_in_s")
    nqkv = conv_w.shape[1] // LANES
    qkv = _conv_silu(pm, conv_w, nqkv, f"{tag}_conv")
    gates = _gates(ps, prm, H, f"{tag}_gates")
    og, states = _delta_fwd(qkv, pm, gates, nw, H, f"{tag}_delta")
    y = _mm(og, w_out, "nn", name=f"{tag}_out")
    x1 = _combine(x, y, mod, 2, lnp, 0, f"{tag}_ln")
    return x1, (x, h, pm, ps, qkv, gates, states, og, y)


def _dn_bwd(dx1, saved, mod, lnp, wts, H, tag):
    w_main, w_small, conv_w, prm, nw, w_out = wts
    x, h, pm, ps, qkv, gates, states, og, y = saved
    dxa, dy, dp = _combine_bwd(x, y, mod, 2, lnp, 0, dx1, f"{tag}_ln_b")
    dog = _mm(dy, w_out, "nt", name=f"{tag}_out_bx")
    dw_out = _mm(og, dy, "tn", name=f"{tag}_out_bw")
    dq, dk, dv, dpm, dgates, dnw = _delta_bwd(qkv, pm, gates, nw, states, dog, H, f"{tag}_delta_b")
    dps, dprm = _gates_bwd(ps, prm, dgates, H, f"{tag}_gates_b")
    dcw = []
    nb = dq.shape[1] // LANES
    for part, dpart in enumerate((dq, dk, dv)):
        dpm, dcw_p = _conv_silu_bwd(pm, conv_w, dpart, dpm, part * nb, f"{tag}_conv_b{part}")
        dcw.append(dcw_p)
    dconv_w = jnp.concatenate(dcw, axis=1)
    dw_main = _mm(h, dpm, "tn", name=f"{tag}_in_bw")
    dw_small = _mm(h, dps, "tn", name=f"{tag}_in_s_bw")
    dh_s = _mm(dps, w_small, "nt", name=f"{tag}_in_s_bx")
    dh = _mm(dpm, w_main, "nt", name=f"{tag}_in_bx", aux=[(dh_s, "mn")], out_fn=lambda r, e: (r + e,))
    dx, dss = _modulate_bwd(x, mod, 1, dh, dxa, f"{tag}_mod_b")
    return dx, (dw_main, dw_small, dconv_w, dprm, dnw, dw_out), (dss, dp)


def _cf_fwd(x, mod, lnp, wts, tag):
    w_in, dw_w, dw_b, cln, w_out = wts
    h = _modulate(x, mod, 0, 1, f"{tag}_mod")
    p = _mm(h, w_in, "nn", name=f"{tag}_in")
    u2 = _glu_conv(p, dw_w, dw_b, f"{tag}_conv")
    u3 = _ln_silu(u2, cln, f"{tag}_cln")
    y = _mm(u3, w_out, "nn", name=f"{tag}_out")
    x1 = _combine(x, y, mod, 2, lnp, 0, f"{tag}_ln")
    return x1, (x, h, p, u2, u3, y)


def _cf_bwd(dx1, saved, mod, lnp, wts, tag):
    w_in, dw_w, dw_b, cln, w_out = wts
    x, h, p, u2, u3, y = saved
    dxa, dy, dp = _combine_bwd(x, y, mod, 2, lnp, 0, dx1, f"{tag}_ln_b")
    du3 = _mm(dy, w_out, "nt", name=f"{tag}_out_bx")
    dw_out = _mm(u3, dy, "tn", name=f"{tag}_out_bw")
    du2, dcln = _ln_silu_bwd(u2, cln, du3, f"{tag}_cln_b")
    dval, dgate, ddw_w, ddw_b = _glu_conv_bwd(p, dw_w, du2, f"{tag}_conv_b")
    dpp = jnp.concatenate([dval, dgate], axis=1)
    dw_in = _mm(h, dpp, "tn", name=f"{tag}_in_bw")
    dh = _mm(dpp, w_in, "nt", name=f"{tag}_in_bx")
    dx, dss = _modulate_bwd(x, mod, 1, dh, dxa, f"{tag}_mod_b")
    return dx, (dw_in, ddw_w, ddw_b, dcln, dw_out), (dss, dp)


def _two_d(a):
    return a.reshape(-1, a.shape[-1])


def kernel(x, c, ada_w, ada_b, ln_g, ln_b, dn_w_in, dn_conv_w, dn_a_log, dn_dt_bias, dn_norm_w, dn_w_out, cf_w_in, cf_dw_w, cf_dw_b, cf_ln_g, cf_ln_b, cf_w_out, ff_w1, ff_w2, loss_target, m_ada_w, m_ada_b, m_ln_g, m_ln_b, m_dn_w_in, m_dn_conv_w, m_dn_a_log, m_dn_dt_bias, m_dn_norm_w, m_dn_w_out, m_cf_w_in, m_cf_dw_w, m_cf_dw_b, m_cf_ln_g, m_cf_ln_b, m_cf_w_out, m_ff_w1, m_ff_w2, v_ada_w, v_ada_b, v_ln_g, v_ln_b, v_dn_w_in, v_dn_conv_w, v_dn_a_log, v_dn_dt_bias, v_dn_norm_w, v_dn_w_out, v_cf_w_in, v_cf_dw_w, v_cf_dw_b, v_cf_ln_g, v_cf_ln_b, v_cf_w_out, v_ff_w1, v_ff_w2):
    ix, iy, ic = lax.axis_index("x"), lax.axis_index("y"), lax.axis_index("c")
    chip = 2 * ix + iy
    dev = 4 * ix + 2 * iy + ic
    S, D = x.shape[1], x.shape[2]
    L = ada_w.shape[0]
    LA, LB = dn_w_in.shape[0], cf_w_in.shape[0]
    H = dn_a_log.shape[1]
    NMOD = ada_b.shape[1] // D
    dn_in = dn_w_in.shape[2] * 4
    n_main = dn_in - 2 * H
    assert L == N_LAYERS and 2 * H <= LANES
    x0, tgt = x[0], loss_target[0]

    small_sharded = [ln_g, ln_b, dn_conv_w, cf_dw_w, cf_dw_b, cf_ln_g, cf_ln_b]
    small_axes = [2, 2, 2, 2, 1, 1, 1]
    packed_small = _pack(small_sharded, rows=8)[None]
    def my_layers(w):
        half = w.shape[0] // 2
        return lax.dynamic_slice_in_dim(w, ic * half, half, 0).astype(_MXU)

    h_dn_in, h_dn_out, h_cf_in, h_cf_out, h_w1, h_w2, g_small = _exchange(
        [(my_layers(dn_w_in)[None], 0), (my_layers(dn_w_out), 1), (my_layers(cf_w_in), 2),
         (my_layers(cf_w_out), 1), (my_layers(ff_w1), 2), (my_layers(ff_w2), 1), (packed_small, 0)],
        "xy", "gather", "gather_weights", nsplit=2)
    g_dn_in, g_dn_out, g_cf_in, g_cf_out, g_w1, g_w2 = _exchange(
        [(h_dn_in, 1), (h_dn_out, 0), (h_cf_in, 0), (h_cf_out, 0), (h_w1, 0), (h_w2, 0)],
        "c", "gather", "share_weights", nsplit=16)
    w_dn_in = jnp.transpose(g_dn_in, (1, 2, 0, 3)).reshape(LA, D, dn_in)
    w_dn_main = w_dn_in[:, :, :n_main]
    w_dn_small = jnp.pad(w_dn_in[:, :, n_main:], ((0, 0), (0, 0), (0, LANES - 2 * H)))
    shard_shapes = [a.shape for a in small_sharded]
    per_chip = [_unpack(g_small[q], shard_shapes) for q in range(4)]
    ln_g_f, ln_b_f, conv_w_f, dw_w_f, dw_b_f, cln_g_f, cln_b_f = [
        jnp.concatenate([per_chip[q][i] for q in range(4)], axis=small_axes[i]) for i in range(len(small_sharded))]

    c_all = _exchange([(c[None], 0)], "all", "gather", "gather_cond")[0].reshape(8, D)
    c_pad = jnp.pad(c_all, ((0, 8), (0, 0)))
    mod_sh = jnp.stack([_mm(c_pad, (ada_w, l), "nn", name=f"ada_{l}", a_fn=lambda t: t * _sigmoid(t))
                        for l in range(L)])
    mod_all = _exchange([(mod_sh, 2)], "xy", "gather", "gather_mod")[0]
    mod_mine = lax.dynamic_index_in_dim(mod_all, dev, axis=1, keepdims=False) + ada_b
    mods = mod_mine.reshape(L, NMOD, D)

    def lnp_of(l):
        return jnp.stack([ln_g_f[l, 0], ln_b_f[l, 0], ln_g_f[l, 1], ln_b_f[l, 1]])

    def dn_wts(j):
        prm = jnp.zeros((2, LANES), F32).at[0, H:2 * H].set(dn_a_log[j]).at[1, H:2 * H].set(dn_dt_bias[j])
        return ((w_dn_main, j), (w_dn_small, j), conv_w_f[j], prm, dn_norm_w[j][None], (g_dn_out, j))

    def cf_wts(j):
        return ((g_cf_in, j), dw_w_f[j], dw_b_f[j][None], jnp.stack([cln_g_f[j], cln_b_f[j]]), (g_cf_out, j))

    xs = x0
    saved = []
    for l in range(L):
        j = l // 2
        if l % 2 == 0:
            xs, sv_a = _dn_fwd(xs, mods[l], lnp_of(l), dn_wts(j), H, f"l{l}_dn")
        else:
            xs, sv_a = _cf_fwd(xs, mods[l], lnp_of(l), cf_wts(j), f"l{l}_cf")
        xs, sv_b = _mlp_fwd(xs, mods[l], lnp_of(l), (g_w1, l), (g_w2, l), f"l{l}_ff")
        saved.append((sv_a, sv_b))
    dx, loss_local = _loss_head(xs, tgt, "loss_head")
    loss = lax.psum(loss_local[0, 0], ("x", "y", "c"))

    gw1, gw2 = [None] * L, [None] * L
    g_dn = [None] * LA
    g_cf = [None] * LB
    dmods, dlns = [None] * L, [None] * L
    for l in reversed(range(L)):
        j = l // 2
        sv_a, sv_b = saved[l]
        dx, gw1[l], gw2[l], (dss2, dp2) = _mlp_bwd(dx, sv_b, mods[l], lnp_of(l), (g_w1, l), (g_w2, l), f"l{l}_ff")
        if l % 2 == 0:
            dx, g_dn[j], (dss1, dp1) = _dn_bwd(dx, sv_a, mods[l], lnp_of(l), dn_wts(j), H, f"l{l}_dn")
        else:
            dx, g_cf[j], (dss1, dp1) = _cf_bwd(dx, sv_a, mods[l], lnp_of(l), cf_wts(j), f"l{l}_cf")
        dmods[l] = jnp.concatenate([dss1, dp1[0:1], dss2, dp2[0:1]], axis=0)
        dlns[l] = (jnp.stack([dp1[1], dp2[1]]), jnp.stack([dp1[2], dp2[2]]))
    grad_x = dx[None]

    d_ln_g = jnp.stack([dlns[l][0] for l in range(L)])
    d_ln_b = jnp.stack([dlns[l][1] for l in range(L)])
    d_conv_w = jnp.stack([g_dn[j][2] for j in range(LA)])
    d_a_log = jnp.stack([g_dn[j][3][0, H:2 * H] for j in range(LA)])
    d_dt_bias = jnp.stack([g_dn[j][3][1, H:2 * H] for j in range(LA)])
    d_norm_w = jnp.stack([g_dn[j][4][0] for j in range(LA)])
    d_dw_w = jnp.stack([g_cf[j][1] for j in range(LB)])
    d_dw_b = jnp.stack([g_cf[j][2][0] for j in range(LB)])
    d_cln_g = jnp.stack([g_cf[j][3][0] for j in range(LB)])
    d_cln_b = jnp.stack([g_cf[j][3][1] for j in range(LB)])
    d_mod = jnp.stack(dmods).reshape(L, NMOD * D)
    small_full = [d_mod, d_ln_g, d_ln_b, d_conv_w, d_dw_w, d_dw_b, d_cln_g, d_cln_b, d_a_log, d_dt_bias, d_norm_w]
    small_all = _exchange([(_pack(small_full, rows=8)[None], 0)], "all", "gather", "gather_small_grads")[0]
    small_sum = _sum_slots(small_all, "sum_small_grads")
    (s_ada_b, s_ln_g, s_ln_b, s_conv_w, s_dw_w, s_dw_b, s_cln_g, s_cln_b, s_a_log, s_dt_bias, s_norm_w) = _unpack(
        small_sum, [a.shape for a in small_full])
    d_mod_all = small_all.reshape(8, -1)[:, :L * NMOD * D].reshape(8, L, NMOD * D)

    def shard(a, axis):
        size = a.shape[axis] // 4
        return lax.dynamic_slice_in_dim(a, chip * size, size, axis)

    ncol = ada_w.shape[2]
    d_mod_sh = jnp.pad(lax.dynamic_slice_in_dim(d_mod_all, chip * ncol, ncol, 2), ((0, 8), (0, 0), (0, 0)))
    g_ada_w = jnp.stack([_mm(c_pad, d_mod_sh[:, l], "tn", name=f"ada_bw_{l}", a_fn=lambda t: t * _sigmoid(t))
                         for l in range(L)])

    dn_in_full = jnp.stack([jnp.concatenate([g_dn[j][0], g_dn[j][1][:, :2 * H]], axis=1) for j in range(LA)])
    dn_in_st = jnp.transpose(dn_in_full.reshape(LA, D, 4, dn_in // 4), (2, 0, 1, 3))
    big = [(dn_in_st, 0), (jnp.stack([g_dn[j][5] for j in range(LA)]), 1),
           (jnp.stack([g_cf[j][0] for j in range(LB)]), 2), (jnp.stack([g_cf[j][4] for j in range(LB)]), 1),
           (jnp.stack(gw1), 2), (jnp.stack(gw2), 1)]
    layer_axis = [1, 0, 0, 0, 0, 0]
    pair = _exchange([(g, la) for (g, _), la in zip(big, layer_axis)], "c", "scatter", "pair_grads", nsplit=8)
    pair_sums = [_sum_slots(r.reshape(2, -1, r.shape[-1]), f"sum_pair_{i}", _MXU).reshape(r.shape[1:])
                 for i, r in enumerate(pair)]
    recv = _exchange([(s, ax) for s, (_, ax) in zip(pair_sums, big)], "xy", "scatter", "scatter_grads")
    sums = [_sum_slots(r.reshape(4, -1, r.shape[-1]), f"sum_grads_{i}") for i, r in enumerate(recv)]
    halves = _exchange([(s[None], 0) for s in sums], "c", "gather", "swap_sums", nsplit=4)
    both = [hv.reshape(1, -1, hv.shape[-1]) for hv in halves]

    big_w = [(dn_w_in, m_dn_w_in, v_dn_w_in), (dn_w_out, m_dn_w_out, v_dn_w_out), (cf_w_in, m_cf_w_in, v_cf_w_in),
             (cf_w_out, m_cf_w_out, v_cf_w_out), (ff_w1, m_ff_w1, v_ff_w1), (ff_w2, m_ff_w2, v_ff_w2)]
    big_out = []
    for i, ((w, m, v), st) in enumerate(zip(big_w, both)):
        res = _adamw(_two_d(w), _two_d(m), _two_d(v), st, f"adamw_big_{i}")
        big_out.append([r.reshape(w.shape) for r in res])
    ada_out = [r.reshape(ada_w.shape) for r in _adamw(_two_d(ada_w), _two_d(m_ada_w), _two_d(v_ada_w),
                                                     _two_d(g_ada_w)[None], "adamw_ada_w")]

    small_w = [(ada_b, m_ada_b, v_ada_b, s_ada_b), (ln_g, m_ln_g, v_ln_g, shard(s_ln_g, 2)),
               (ln_b, m_ln_b, v_ln_b, shard(s_ln_b, 2)), (dn_conv_w, m_dn_conv_w, v_dn_conv_w, shard(s_conv_w, 2)),
               (dn_a_log, m_dn_a_log, v_dn_a_log, s_a_log), (dn_dt_bias, m_dn_dt_bias, v_dn_dt_bias, s_dt_bias),
               (dn_norm_w, m_dn_norm_w, v_dn_norm_w, s_norm_w), (cf_dw_w, m_cf_dw_w, v_cf_dw_w, shard(s_dw_w, 2)),
               (cf_dw_b, m_cf_dw_b, v_cf_dw_b, shard(s_dw_b, 1)), (cf_ln_g, m_cf_ln_g, v_cf_ln_g, shard(s_cln_g, 1)),
               (cf_ln_b, m_cf_ln_b, v_cf_ln_b, shard(s_cln_b, 1))]
    pk = [_pack([t[i] for t in small_w], rows=8) for i in range(4)]
    small_res = _adamw(pk[0], pk[1], pk[2], pk[3][None], "adamw_small")
    small_shapes = [t[0].shape for t in small_w]
    small_out = [_unpack(r, small_shapes) for r in small_res]

    def kind(k):
        sm = small_out[k]
        bg = [o[k] for o in big_out]
        return [ada_out[k], sm[0], sm[1], sm[2], bg[0], sm[3], sm[4], sm[5], sm[6], bg[1],
                bg[2], sm[7], sm[8], sm[9], sm[10], bg[3], bg[4], bg[5]]

    return (loss, grad_x, *kind(0), *kind(1), *kind(2), *kind(3))
```

```python
import functools

import jax
import jax.numpy as jnp
from jax import lax
from jax.experimental import pallas as pl
from jax.experimental.pallas import tpu as pltpu

F32 = jnp.float32
_MXU = jnp.bfloat16
_HI = lax.Precision.HIGHEST

N_LAYERS = 4
ALPHA = (2.0 * N_LAYERS) ** 0.25
LN_EPS = 1e-5
RMS_EPS = 1e-6
L2_EPS = 1e-6
CHUNK = 64
ADAM_LR, ADAM_B1, ADAM_B2, ADAM_EPS, ADAM_WD, ADAM_STEP = 0.001, 0.9, 0.999, 1e-08, 0.01, 10

LANES = 128
TOKEN_BLOCK = 256
VMEM_BIG = 48 * 1024 * 1024
MM_TILE = 1024

SDS = jax.ShapeDtypeStruct
MESH = pl.DeviceIdType.MESH


def _cparams(vmem=None):
    if vmem is None:
        return None
    return pltpu.CompilerParams(vmem_limit_bytes=vmem)


def _pcall(body, **kw):
    if kw.get("compiler_params", 1) is None:
        kw.pop("compiler_params")
    return pl.pallas_call(body, **kw)


def _full(arr):
    nd = arr.ndim
    return pl.BlockSpec(arr.shape, lambda *g: (0,) * nd)


def _bs(block, imap, lead=None):
    if lead is None:
        return pl.BlockSpec(block, imap)
    return pl.BlockSpec((None,) + tuple(block), lambda *g: (lead,) + tuple(imap(*g)))


def _split(a):
    return a if isinstance(a, tuple) else (a, None)


_GROUPS = {
    "xy": ([(1, 0, 0), (0, 1, 0), (1, 1, 0)], 4),
    "c": ([(0, 0, 1)], 2),
    "all": ([(1, 0, 0), (0, 1, 0), (1, 1, 0), (0, 0, 1), (1, 0, 1), (0, 1, 1), (1, 1, 1)], 8),
}


def _exchange(items, group, mode, name, nsplit=1, keep_own=True):
    masks, n = _GROUPS[group]
    npeer = len(masks)
    ni = len(items)
    arrs = [a for a, _ in items]
    out_shapes = []
    for a, ax in items:
        shp = list(a.shape)
        if mode == "gather":
            shp[ax] *= n
        else:
            shp[ax] //= n
            shp = [n] + shp
        out_shapes.append(SDS(tuple(shp), a.dtype))

    def body(*refs):
        ins, outs = refs[:ni], refs[ni:2 * ni]
        send_sems, recv_sems, local_sems = refs[2 * ni:]
        x, y, c = lax.axis_index("x"), lax.axis_index("y"), lax.axis_index("c")

        def slot(px, py, pc):
            if group == "xy":
                return 2 * px + py
            if group == "c":
                return pc
            return 4 * px + 2 * py + pc

        me = slot(x, y, c)

        def block(ref, ax, idx, size):
            ix = (slice(None),) * ax + (pl.ds(pl.multiple_of(idx * size, size), size),)
            return ref.at[ix]

        copies = []
        for it, (a, ax) in enumerate(items):
            in_ref, out_ref = ins[it], outs[it]
            if mode == "gather":
                size = a.shape[ax]
                src_own, dst_own = in_ref, block(out_ref, ax, me, size)
            else:
                size = a.shape[ax] // n
                src_own, dst_own = block(in_ref, ax, me, size), out_ref.at[me]
            sax, ns, cs = splits[it]
            pieces = [(slice(None),) * sax + (pl.ds(j * cs, cs),) for j in range(ns)]
            if keep_own:
                for j, piece in enumerate(pieces):
                    own = pltpu.make_async_copy(src_own.at[piece], dst_own.at[piece], local_sems.at[it * nsplit + j])
                    own.start()
                    copies.append(own)
            for k, m in enumerate(masks):
                peer = tuple((1 - v) if b else v for v, b in zip((x, y, c), m))
                if mode == "gather":
                    src, dst = in_ref, dst_own
                else:
                    src, dst = block(in_ref, ax, slot(*peer), size), out_ref.at[me]
                for j, piece in enumerate(pieces):
                    sem = (it * npeer + k) * nsplit + j
                    cp = pltpu.make_async_remote_copy(
                        src_ref=src.at[piece], dst_ref=dst.at[piece], send_sem=send_sems.at[sem],
                        recv_sem=recv_sems.at[sem], device_id=peer, device_id_type=MESH)
                    cp.start()
                    copies.append(cp)
        for cp in copies:
            cp.wait()

    splits = []
    for (a, ax), o in zip(items, out_shapes):
        bshape = a.shape if mode == "gather" else o.shape[1:]
        sax = max(range(len(bshape) - 1), key=lambda d: bshape[d])
        ns = nsplit if bshape[sax] % (nsplit * 16) == 0 else 1
        splits.append((sax, ns, bshape[sax] // ns))
    any_spec = pl.BlockSpec(memory_space=pl.ANY)
    nsem = ni * npeer * nsplit
    outs = _pcall(
        body, name=name, out_shape=tuple(out_shapes),
        in_specs=[any_spec] * ni, out_specs=tuple([any_spec] * ni),
        scratch_shapes=[pltpu.SemaphoreType.DMA((nsem,)), pltpu.SemaphoreType.DMA((nsem,)),
                        pltpu.SemaphoreType.DMA((ni * nsplit,))],
    )(*arrs)
    return list(outs)


def _tile(n, cap):
    if n <= cap:
        return n
    t = cap - cap % LANES
    while n % t:
        t -= LANES
    return t


def _mm(a, b, mode, *, name, out_dtypes=(F32,), tm=MM_TILE, tn=MM_TILE, tk=MM_TILE, a_fn=None, out_fn=None, aux=(),
        out_imap=None, out_shape=None):
    (a, a_lead), (b, b_lead) = _split(a), _split(b)
    ash, bsh = a.shape[-2:], b.shape[-2:]
    if mode == "nn":
        (M, K), (_, N) = ash, bsh
    elif mode == "nt":
        (M, K), (N, _) = ash, bsh
    else:
        (K, M), (_, N) = ash, bsh
    tm, tn, tk = _tile(M, tm), _tile(N, tn), _tile(K, tk)
    nk = K // tk
    if mode == "tn":
        a_spec = _bs((tk, tm), lambda i, j, k: (k, i), a_lead)
    else:
        a_spec = _bs((tm, tk), lambda i, j, k: (i, k), a_lead)
    if mode == "nt":
        b_spec = _bs((tn, tk), lambda i, j, k: (j, k), b_lead)
    else:
        b_spec = _bs((tk, tn), lambda i, j, k: (k, j), b_lead)
    aux_arrs, aux_specs = [], []
    for arr, kind in aux:
        arr, lead = _split(arr)
        aux_arrs.append(arr)
        if kind == "mn":
            aux_specs.append(_bs((tm, tn), lambda i, j, k: (i, j), lead))
        else:
            aux_specs.append(_bs((1, tn), lambda i, j, k: (0, j), lead))
    na, no = len(aux_arrs), len(out_dtypes)
    dims = {"nn": (((1,), (0,)), ((), ())), "nt": (((1,), (1,)), ((), ())), "tn": (((0,), (0,)), ((), ()))}[mode]

    def finish(r, aux_refs, o_refs):
        outs = out_fn(r, *[x[...] for x in aux_refs]) if out_fn is not None else (r,)
        for o_ref, val in zip(o_refs, outs):
            o_ref[...] = val.astype(o_ref.dtype)

    def product(a_ref, b_ref):
        av = a_ref[...]
        if a_fn is not None:
            av = a_fn(av.astype(F32))
        return lax.dot_general(av.astype(_MXU), b_ref[...].astype(_MXU), dims, preferred_element_type=F32)

    def body_one(a_ref, b_ref, *rest):
        finish(product(a_ref, b_ref), rest[:na], rest[na:na + no])

    def body_acc(a_ref, b_ref, *rest):
        aux_refs, o_refs, acc = rest[:na], rest[na:na + no], rest[na + no]
        k = pl.program_id(2)

        @pl.when(k == 0)
        def _():
            acc[...] = product(a_ref, b_ref)

        @pl.when(k != 0)
        def _():
            acc[...] += product(a_ref, b_ref)

        @pl.when(k == nk - 1)
        def _():
            finish(acc[...], aux_refs, o_refs)

    if out_shape is None:
        out_shape = (M, N)
    if out_imap is None:
        o_spec = pl.BlockSpec((tm, tn), lambda i, j, k: (i, j))
    else:
        o_spec = pl.BlockSpec((None,) * (len(out_shape) - 2) + (tm, tn), out_imap)
    outs = _pcall(
        body_one if nk == 1 else body_acc, name=name, grid=(M // tm, N // tn, nk),
        in_specs=[a_spec, b_spec] + aux_specs, out_specs=tuple([o_spec] * no),
        out_shape=tuple(SDS(out_shape, dt) for dt in out_dtypes),
        scratch_shapes=[] if nk == 1 else [pltpu.VMEM((tm, tn), F32)],
        compiler_params=pltpu.CompilerParams(dimension_semantics=("parallel", "parallel", "arbitrary"),
                                             vmem_limit_bytes=VMEM_BIG),
    )(a, b, *aux_arrs)
    return outs[0] if no == 1 else outs


def _tok(S):
    ts = min(TOKEN_BLOCK, S)
    assert S % ts == 0
    return ts


def _row(ts, D):
    return pl.BlockSpec((ts, D), lambda i: (i, 0))


def _acc_rows(ref, i, rows):
    @pl.when(i == 0)
    def _():
        for r, v in enumerate(rows):
            ref[r:r + 1, :] = v

    @pl.when(i != 0)
    def _():
        for r, v in enumerate(rows):
            ref[r:r + 1, :] += v


def _modulate(x, mod, r_sh, r_sc, name):
    S, D = x.shape
    ts = _tok(S)

    def body(x_ref, m_ref, o_ref):
        o_ref[...] = (x_ref[...] * (1.0 + m_ref[r_sc:r_sc + 1, :]) + m_ref[r_sh:r_sh + 1, :]).astype(o_ref.dtype)

    return _pcall(body, name=name, grid=(S // ts,), in_specs=[_row(ts, D), _full(mod)],
                  out_specs=_row(ts, D), out_shape=SDS((S, D), _MXU))(x, mod)


def _modulate_bwd(x, mod, r_sc, dh, dxa, name):
    S, D = x.shape
    ts = _tok(S)

    def body(x_ref, m_ref, dh_ref, dxa_ref, dx_ref, dss_ref):
        dh_v = dh_ref[...]
        dx_ref[...] = dxa_ref[...] + dh_v * (1.0 + m_ref[r_sc:r_sc + 1, :])
        _acc_rows(dss_ref, pl.program_id(0),
                  [jnp.sum(dh_v, axis=0, keepdims=True), jnp.sum(dh_v * x_ref[...], axis=0, keepdims=True)])

    return _pcall(body, name=name, grid=(S // ts,),
                  in_specs=[_row(ts, D), _full(mod), _row(ts, D), _row(ts, D)],
                  out_specs=(_row(ts, D), pl.BlockSpec((2, D), lambda i: (0, 0))),
                  out_shape=(SDS((S, D), F32), SDS((2, D), F32)))(x, mod, dh, dxa)


def _norm_stats(z):
    mu = jnp.mean(z, axis=-1, keepdims=True)
    zc = z - mu
    var = jnp.mean(zc * zc, axis=-1, keepdims=True)
    rstd = lax.rsqrt(var + LN_EPS)
    return zc * rstd, rstd


def _norm_bwd(dxhat, xhat, rstd):
    return rstd * (dxhat - jnp.mean(dxhat, axis=-1, keepdims=True)
                   - xhat * jnp.mean(dxhat * xhat, axis=-1, keepdims=True))


def _combine(x, y, mod, r_gt, lnp, r_g, name):
    S, D = x.shape
    ts = _tok(S)

    def body(x_ref, y_ref, m_ref, l_ref, o_ref):
        z = ALPHA * x_ref[...] + (1.0 + m_ref[r_gt:r_gt + 1, :]) * y_ref[...]
        xhat, _ = _norm_stats(z)
        o_ref[...] = xhat * l_ref[r_g:r_g + 1, :] + l_ref[r_g + 1:r_g + 2, :]

    return _pcall(body, name=name, grid=(S // ts,), in_specs=[_row(ts, D), _row(ts, D), _full(mod), _full(lnp)],
                  out_specs=_row(ts, D), out_shape=SDS((S, D), F32))(x, y, mod, lnp)


def _combine_bwd(x, y, mod, r_gt, lnp, r_g, dout, name):
    S, D = x.shape
    ts = _tok(S)

    def body(x_ref, y_ref, m_ref, l_ref, do_ref, dxa_ref, dy_ref, dp_ref):
        gate = 1.0 + m_ref[r_gt:r_gt + 1, :]
        y_v, do_v = y_ref[...], do_ref[...]
        xhat, rstd = _norm_stats(ALPHA * x_ref[...] + gate * y_v)
        dz = _norm_bwd(do_v * l_ref[r_g:r_g + 1, :], xhat, rstd)
        dxa_ref[...] = ALPHA * dz
        dy_ref[...] = (gate * dz).astype(dy_ref.dtype)
        _acc_rows(dp_ref, pl.program_id(0),
                  [jnp.sum(dz * y_v, axis=0, keepdims=True), jnp.sum(do_v * xhat, axis=0, keepdims=True),
                   jnp.sum(do_v, axis=0, keepdims=True)])

    return _pcall(body, name=name, grid=(S // ts,),
                  in_specs=[_row(ts, D), _row(ts, D), _full(mod), _full(lnp), _row(ts, D)],
                  out_specs=(_row(ts, D), _row(ts, D), pl.BlockSpec((3, D), lambda i: (0, 0))),
                  out_shape=(SDS((S, D), F32), SDS((S, D), _MXU), SDS((3, D), F32)))(x, y, mod, lnp, dout)


def _sigmoid(t):
    return 1.0 / (1.0 + jnp.exp(-t))


def _ln_silu(u, lnp, name):
    S, D = u.shape
    ts = _tok(S)

    def body(u_ref, l_ref, o_ref):
        xhat, _ = _norm_stats(u_ref[...])
        t = xhat * l_ref[0:1, :] + l_ref[1:2, :]
        o_ref[...] = (t * _sigmoid(t)).astype(o_ref.dtype)

    return _pcall(body, name=name, grid=(S // ts,), in_specs=[_row(ts, D), _full(lnp)],
                  out_specs=_row(ts, D), out_shape=SDS((S, D), _MXU))(u, lnp)


def _ln_silu_bwd(u, lnp, dout, name):
    S, D = u.shape
    ts = _tok(S)

    def body(u_ref, l_ref, do_ref, du_ref, dp_ref):
        xhat, rstd = _norm_stats(u_ref[...])
        g = l_ref[0:1, :]
        t = xhat * g + l_ref[1:2, :]
        sg = _sigmoid(t)
        dt = do_ref[...] * (sg * (1.0 + t * (1.0 - sg)))
        du_ref[...] = _norm_bwd(dt * g, xhat, rstd)
        _acc_rows(dp_ref, pl.program_id(0),
                  [jnp.sum(dt * xhat, axis=0, keepdims=True), jnp.sum(dt, axis=0, keepdims=True)])

    return _pcall(body, name=name, grid=(S // ts,), in_specs=[_row(ts, D), _full(lnp), _row(ts, D)],
                  out_specs=(_row(ts, D), pl.BlockSpec((2, D), lambda i: (0, 0))),
                  out_shape=(SDS((S, D), F32), SDS((2, D), F32)))(u, lnp, dout)


def _loss_head(xf, tgt, name):
    S, D = xf.shape
    ts = _tok(S)

    def body(x_ref, t_ref, dx_ref, l_ref):
        err = x_ref[...] - t_ref[...]
        dx_ref[...] = err * (1.0 / D)
        part = jnp.sum(jnp.sum(err * err, axis=1, keepdims=True), axis=0, keepdims=True) * (0.5 / D)

        @pl.when(pl.program_id(0) == 0)
        def _():
            l_ref[...] = part

        @pl.when(pl.program_id(0) != 0)
        def _():
            l_ref[...] += part

    return _pcall(body, name=name, grid=(S // ts,), in_specs=[_row(ts, D), _row(ts, D)],
                  out_specs=(_row(ts, D), pl.BlockSpec((1, 1), lambda i: (0, 0))),
                  out_shape=(SDS((S, D), F32), SDS((1, 1), F32)))(xf, tgt)


def _shift_down(u, s, rows):
    if s == 0:
        return u
    return jnp.where(rows >= s, pltpu.roll(u, s, 0), 0.0)


def _shift_up(u, s, rows):
    if s == 0:
        return u
    n = u.shape[0]
    return jnp.where(rows < n - s, pltpu.roll(u, n - s, 0), 0.0)


def _dwconv(u, w_ref, taps, rows):
    acc = jnp.zeros_like(u)
    for j in range(taps):
        acc = acc + w_ref[j:j + 1, :] * _shift_down(u, taps - 1 - j, rows)
    return acc


def _dwconv_bwd(u, dy, w_ref, dw_ref, taps, rows):
    du = jnp.zeros_like(u)
    for j in range(taps):
        s = taps - 1 - j
        du = du + w_ref[j:j + 1, :] * _shift_up(dy, s, rows)
        dw_ref[j:j + 1, :] = jnp.sum(dy * _shift_down(u, s, rows), axis=0, keepdims=True)
    return du


def _col(S, j0=0):
    return pl.BlockSpec((S, LANES), lambda j: (0, j + j0))


def _conv_silu(pm, w, nblk, name):
    S = pm.shape[0]
    taps = w.shape[0]

    def body(u_ref, w_ref, o_ref):
        rows = lax.broadcasted_iota(jnp.int32, (S, LANES), 0)
        cv = _dwconv(u_ref[...], w_ref, taps, rows)
        o_ref[...] = cv * _sigmoid(cv)

    return _pcall(body, name=name, grid=(nblk,),
                  in_specs=[_col(S), pl.BlockSpec((taps, LANES), lambda j: (0, j))],
                  out_specs=_col(S), out_shape=SDS((S, nblk * LANES), F32),
                  compiler_params=_cparams(VMEM_BIG))(pm, w)


def _conv_silu_bwd(pm, w, dout, dpm, j0, name):
    S = pm.shape[0]
    taps = w.shape[0]
    nblk = dout.shape[1] // LANES

    def body(u_ref, w_ref, do_ref, dpm_in, du_ref, dw_ref):
        del dpm_in
        rows = lax.broadcasted_iota(jnp.int32, (S, LANES), 0)
        u = u_ref[...]
        cv = _dwconv(u, w_ref, taps, rows)
        sg = _sigmoid(cv)
        dc = do_ref[...] * (sg * (1.0 + cv * (1.0 - sg)))
        du_ref[...] = _dwconv_bwd(u, dc, w_ref, dw_ref, taps, rows)

    return _pcall(body, name=name, grid=(nblk,),
                  in_specs=[_col(S, j0), pl.BlockSpec((taps, LANES), lambda j: (0, j + j0)), _col(S),
                            pl.BlockSpec(memory_space=pl.ANY)],
                  out_specs=(_col(S, j0), pl.BlockSpec((taps, LANES), lambda j: (0, j))),
                  out_shape=(SDS(dpm.shape, F32), SDS((taps, nblk * LANES), F32)),
                  input_output_aliases={3: 0},
                  compiler_params=_cparams(VMEM_BIG))(pm, w, dout, dpm)


def _glu_conv(p, w, bias, name):
    S, C2 = p.shape
    nblk = C2 // 2 // LANES
    taps = w.shape[0]

    def body(v_ref, g_ref, w_ref, b_ref, o_ref):
        rows = lax.broadcasted_iota(jnp.int32, (S, LANES), 0)
        u = v_ref[...] * _sigmoid(g_ref[...])
        o_ref[...] = _dwconv(u, w_ref, taps, rows) + b_ref[...]

    return _pcall(body, name=name, grid=(nblk,),
                  in_specs=[_col(S), _col(S, nblk), pl.BlockSpec((taps, LANES), lambda j: (0, j)),
                            pl.BlockSpec((1, LANES), lambda j: (0, j))],
                  out_specs=_col(S), out_shape=SDS((S, nblk * LANES), F32),
                  compiler_params=_cparams(VMEM_BIG))(p, p, w, bias)


def _glu_conv_bwd(p, w, dout, name):
    S, C2 = p.shape
    nblk = C2 // 2 // LANES
    taps = w.shape[0]

    def body(v_ref, g_ref, w_ref, do_ref, dv_ref, dg_ref, dw_ref, db_ref):
        rows = lax.broadcasted_iota(jnp.int32, (S, LANES), 0)
        val, sg = v_ref[...], _sigmoid(g_ref[...])
        do_v = do_ref[...]
        du = _dwconv_bwd(val * sg, do_v, w_ref, dw_ref, taps, rows)
        dv_ref[...] = du * sg
        dg_ref[...] = du * val * sg * (1.0 - sg)
        db_ref[...] = jnp.sum(do_v, axis=0, keepdims=True)

    dval, dgate, dw, db = _pcall(
        body, name=name, grid=(nblk,),
        in_specs=[_col(S), _col(S, nblk), pl.BlockSpec((taps, LANES), lambda j: (0, j)), _col(S)],
        out_specs=(_col(S), _col(S), pl.BlockSpec((taps, LANES), lambda j: (0, j)),
                   pl.BlockSpec((1, LANES), lambda j: (0, j))),
        out_shape=(SDS((S, C2 // 2), F32), SDS((S, C2 // 2), F32), SDS((taps, C2 // 2), F32), SDS((1, C2 // 2), F32)),
        compiler_params=_cparams(VMEM_BIG))(p, p, w, dout)
    return dval, dgate, dw, db


def _log1p(e):
    u = 1.0 + e
    d = jnp.where(u == 1.0, 1.0, u - 1.0)
    return jnp.where(u == 1.0, e, jnp.log(u) * (e / d))


def _gate_parts(ps, prm, H):
    lane = lax.broadcasted_iota(jnp.int32, ps.shape, 1)
    is_b, is_g = lane < H, (lane >= H) & (lane < 2 * H)
    beta = _sigmoid(ps)
    t = ps + prm[1:2, :]
    sp = jnp.maximum(t, 0.0) + _log1p(jnp.exp(-jnp.abs(t)))
    na = -jnp.exp(prm[0:1, :])
    return is_b, is_g, beta, t, sp, na


def _gates(ps, prm, H, name):
    S = ps.shape[0]
    ts = _tok(S)

    def body(p_ref, r_ref, o_ref):
        is_b, is_g, beta, _, sp, na = _gate_parts(p_ref[...], r_ref[...], H)
        o_ref[...] = jnp.where(is_b, beta, jnp.where(is_g, na * sp, 0.0))

    return _pcall(body, name=name, grid=(S // ts,), in_specs=[_row(ts, LANES), _full(prm)],
                  out_specs=_row(ts, LANES), out_shape=SDS((S, LANES), F32))(ps, prm)


def _gates_bwd(ps, prm, dgates, H, name):
    S = ps.shape[0]
    ts = _tok(S)

    def body(p_ref, r_ref, dg_ref, dp_ref, dr_ref):
        is_b, is_g, beta, t, sp, na = _gate_parts(p_ref[...], r_ref[...], H)
        dg_v = dg_ref[...]
        dsp = jnp.where(is_g, dg_v * na * _sigmoid(t), 0.0)
        dp_ref[...] = jnp.where(is_b, dg_v * beta * (1.0 - beta), dsp)
        _acc_rows(dr_ref, pl.program_id(0),
                  [jnp.sum(jnp.where(is_g, dg_v * na * sp, 0.0), axis=0, keepdims=True),
                   jnp.sum(dsp, axis=0, keepdims=True)])

    return _pcall(body, name=name, grid=(S // ts,), in_specs=[_row(ts, LANES), _full(prm), _row(ts, LANES)],
                  out_specs=(_row(ts, LANES), pl.BlockSpec((2, LANES), lambda i: (0, 0))),
                  out_shape=(SDS((S, LANES), F32), SDS((2, LANES), F32)))(ps, prm, dgates)


_NN = (((2,), (1,)), ((0,), (0,)))
_NT = (((2,), (2,)), ((0,), (0,)))
_TN = (((1,), (1,)), ((0,), (0,)))


def _mdot(a, b, dims):
    return lax.dot_general(a.astype(_MXU), b.astype(_MXU), dims, preferred_element_type=F32)


def _mdot3(a, b, dims):
    ah, bh = a.astype(_MXU), b.astype(_MXU)
    al, bl = a - ah.astype(F32), b - bh.astype(F32)
    return _mdot(ah, bh, dims) + (_mdot(ah, bl, dims) + _mdot(al, bh, dims))


def _rounded_dot(dims, da_dims, db_dims, a_first, prod=_mdot):
    @jax.custom_vjp
    def f(a, b):
        return prod(a, b, dims)

    def fwd(a, b):
        return prod(a, b, dims), (a, b)

    def bwd(res, ct):
        a, b = res
        da = prod(ct, b, da_dims) if a_first[0] else prod(b, ct, da_dims)
        db = prod(ct, a, db_dims) if a_first[1] else prod(a, ct, db_dims)
        return da, db

    f.defvjp(fwd, bwd)
    return f


_mdot_nn = _rounded_dot(_NN, _NT, _TN, (True, False))
_mdot_nt = _rounded_dot(_NT, _NN, _TN, (True, True))
_mdot_tn = _rounded_dot(_TN, _NT, _NN, (False, False))
_dot = _rounded_dot(_NN, _NT, _TN, (True, False), _mdot3)


def _head_cols(gates, off, H):
    lane = lax.broadcasted_iota(jnp.int32, gates.shape, 1)
    cols = [jnp.sum(jnp.where(lane == off + h, gates, 0.0), axis=-1, keepdims=True) for h in range(H)]
    return jnp.concatenate([col[None] for col in cols], axis=0)


def _delta_chunk(qr, kr, v, z, gates, nw, s_in):
    H, C, dk = qr.shape
    beta, g = _head_cols(gates, 0, H), _head_cols(gates, H, H)
    q = qr * lax.rsqrt(jnp.sum(qr * qr, axis=-1, keepdims=True) + L2_EPS) * (dk ** -0.5)
    k = kr * lax.rsqrt(jnp.sum(kr * kr, axis=-1, keepdims=True) + L2_EPS)
    ri = lax.broadcasted_iota(jnp.int32, (1, C, C), 1)
    ci = lax.broadcasted_iota(jnp.int32, (1, C, C), 2)
    causal, strict, eye = ri >= ci, ri > ci, ri == ci
    gam_row = jnp.sum(jnp.where(ri <= ci, g, 0.0), axis=1, keepdims=True)
    gam_col = jnp.sum(jnp.where(eye, gam_row, 0.0), axis=-1, keepdims=True)
    g_last = jnp.sum(g, axis=1, keepdims=True)
    decay = jnp.where(causal, jnp.exp(jnp.where(causal, gam_col - gam_row, 0.0)), 0.0)
    kb = k * beta
    a = jnp.where(strict, _mdot_nt(kb, k) * decay, 0.0)
    t_inv = jnp.where(eye, 1.0, 0.0) - a
    p = a
    for _ in range(max(C.bit_length() - 2, 0)):
        p = _dot(p, p)
        t_inv = t_inv + _dot(t_inv, p)
    eg = jnp.exp(gam_col)
    u = _mdot_nn(t_inv, v * beta)
    w = _mdot_nn(t_inv, kb * eg)
    a_qk = _mdot_nt(q, k) * decay
    v_new = u - _mdot_nn(w, s_in)
    o = _mdot_nn(q * eg, s_in) + _mdot_nn(a_qk, v_new)
    s_out = s_in * jnp.exp(g_last) + _mdot_tn(k * jnp.exp(g_last - gam_col), v_new)
    og = o * lax.rsqrt(jnp.mean(o * o, axis=-1, keepdims=True) + RMS_EPS) * nw * (z * _sigmoid(z))
    return og, s_out


def _heads(ref, H, dk):
    return jnp.stack([ref[:, h * dk:(h + 1) * dk].astype(F32) for h in range(H)])


def _put_heads(ref, val, dk):
    for h in range(val.shape[0]):
        ref[:, h * dk:(h + 1) * dk] = val[h].astype(ref.dtype)


def _delta_fwd(qkv, pm, gates, nw, H, name):
    S = qkv.shape[0]
    hd = qkv.shape[1] // 3
    dk = hd // H
    N = S // CHUNK
    blk = lambda off: pl.BlockSpec((CHUNK, hd), lambda n: (n, off))

    def body(q_ref, k_ref, v_ref, z_ref, g_ref, nw_ref, og_ref, st_ref, s_scr):
        @pl.when(pl.program_id(0) == 0)
        def _():
            s_scr[...] = jnp.zeros_like(s_scr)

        s_in = s_scr[...]
        st_ref[...] = s_in
        og, s_out = _delta_chunk(_heads(q_ref, H, dk), _heads(k_ref, H, dk), _heads(v_ref, H, dk),
                                 _heads(z_ref, H, dk), g_ref[...], nw_ref[...], s_in)
        _put_heads(og_ref, og, dk)
        s_scr[...] = s_out

    return _pcall(
        body, name=name, grid=(N,),
        in_specs=[blk(0), blk(1), blk(2), blk(3), pl.BlockSpec((CHUNK, LANES), lambda n: (n, 0)), _full(nw)],
        out_specs=(blk(0), pl.BlockSpec((None, H, dk, dk), lambda n: (n, 0, 0, 0))),
        out_shape=(SDS((S, hd), _MXU), SDS((N, H, dk, dk), F32)),
        scratch_shapes=[pltpu.VMEM((H, dk, dk), F32)],
        compiler_params=_cparams(VMEM_BIG),
    )(qkv, qkv, qkv, pm, gates, nw)


def _delta_bwd(qkv, pm, gates, nw, states, dog, H, name):
    S = qkv.shape[0]
    hd = qkv.shape[1] // 3
    dk = hd // H
    N = S // CHUNK
    blk = lambda off: pl.BlockSpec((CHUNK, hd), lambda n: (N - 1 - n, off))
    gspec = pl.BlockSpec((CHUNK, LANES), lambda n: (N - 1 - n, 0))

    def body(q_ref, k_ref, v_ref, z_ref, g_ref, nw_ref, st_ref, do_ref,
             dq_ref, dk_ref, dv_ref, dz_ref, dg_ref, dnw_ref, ds_scr):
        n = pl.program_id(0)

        @pl.when(n == 0)
        def _():
            ds_scr[...] = jnp.zeros_like(ds_scr)

        _, vjp = jax.vjp(_delta_chunk, _heads(q_ref, H, dk), _heads(k_ref, H, dk), _heads(v_ref, H, dk),
                         _heads(z_ref, H, dk), g_ref[...], nw_ref[...], st_ref[...])
        dq, dkk, dv, dz, dg, dnw, ds_in = vjp((_heads(do_ref, H, dk), ds_scr[...]))
        _put_heads(dq_ref, dq, dk)
        _put_heads(dk_ref, dkk, dk)
        _put_heads(dv_ref, dv, dk)
        _put_heads(dz_ref, dz, dk)
        ds_scr[...] = ds_in
        dg_ref[...] = dg

        @pl.when(n == 0)
        def _():
            dnw_ref[...] = dnw

        @pl.when(n != 0)
        def _():
            dnw_ref[...] += dnw

    return _pcall(
        body, name=name, grid=(N,),
        in_specs=[blk(0), blk(1), blk(2), blk(3), gspec, _full(nw),
                  pl.BlockSpec((None, H, dk, dk), lambda n: (N - 1 - n, 0, 0, 0)), blk(0)],
        out_specs=(blk(0), blk(0), blk(0), blk(3), gspec, pl.BlockSpec((1, dk), lambda n: (0, 0))),
        out_shape=(SDS((S, hd), F32), SDS((S, hd), F32), SDS((S, hd), F32), SDS(pm.shape, F32),
                   SDS((S, LANES), F32), SDS((1, dk), F32)),
        scratch_shapes=[pltpu.VMEM((H, dk, dk), F32)],
        compiler_params=_cparams(VMEM_BIG),
    )(qkv, qkv, qkv, pm, gates, nw, states, dog)


def _rows_block(R, C):
    rb = R
    while rb * C * 4 > (1 << 20) and rb % 16 == 0:
        rb //= 2
    return rb


def _sum_slots(st, name, out_dtype=F32):
    n, R, C = st.shape
    rb = _rows_block(R, C)

    def body(s_ref, o_ref):
        acc = s_ref[0].astype(F32)
        for q in range(1, n):
            acc = acc + s_ref[q].astype(F32)
        o_ref[...] = acc.astype(o_ref.dtype)

    return _pcall(body, name=name, grid=(R // rb,), in_specs=[pl.BlockSpec((n, rb, C), lambda i: (0, i, 0))],
                  out_specs=pl.BlockSpec((rb, C), lambda i: (i, 0)), out_shape=SDS((R, C), out_dtype))(st)


def _sum_pair(own2, recv2, ic, name, out_dtype):
    _, R, C = own2.shape
    rb = _rows_block(R, C)

    def body(c_ref, a_ref, b_ref, o_ref):
        del c_ref
        o_ref[...] = (a_ref[...].astype(F32) + b_ref[...].astype(F32)).astype(o_ref.dtype)

    grid_spec = pltpu.PrefetchScalarGridSpec(
        num_scalar_prefetch=1, grid=(R // rb,),
        in_specs=[pl.BlockSpec((None, rb, C), lambda i, c: (c[0], i, 0)),
                  pl.BlockSpec((None, rb, C), lambda i, c: (1 - c[0], i, 0))],
        out_specs=pl.BlockSpec((rb, C), lambda i, c: (i, 0)))
    return _pcall(body, name=name, grid_spec=grid_spec, out_shape=SDS((R, C), out_dtype))(
        jnp.reshape(ic, (1,)).astype(jnp.int32), own2, recv2)


def _adamw(w, m, v, st, name):
    R, C = w.shape
    n = st.shape[0]
    rb = _rows_block(R, C)
    spec = pl.BlockSpec((rb, C), lambda i: (i, 0))

    def body(w_ref, m_ref, v_ref, s_ref, g_ref, d_ref, mo_ref, vo_ref):
        g = s_ref[0]
        for q in range(1, n):
            g = g + s_ref[q]
        m_new = ADAM_B1 * m_ref[...] + (1.0 - ADAM_B1) * g
        v_new = ADAM_B2 * v_ref[...] + (1.0 - ADAM_B2) * (g * g)
        m_hat = m_new / (1.0 - ADAM_B1 ** ADAM_STEP)
        v_hat = v_new / (1.0 - ADAM_B2 ** ADAM_STEP)
        g_ref[...] = g
        d_ref[...] = -ADAM_LR * (m_hat / (jnp.sqrt(v_hat) + ADAM_EPS) + ADAM_WD * w_ref[...])
        mo_ref[...] = m_new
        vo_ref[...] = v_new

    return _pcall(body, name=name, grid=(R // rb,),
                  in_specs=[spec, spec, spec, pl.BlockSpec((n, rb, C), lambda i: (0, i, 0))],
                  out_specs=(spec,) * 4, out_shape=(SDS((R, C), F32),) * 4)(w, m, v, st)


def _pack(arrs, rows=1):
    flat = jnp.concatenate([a.reshape(-1).astype(F32) for a in arrs])
    quantum = rows * LANES
    pad = (-flat.shape[0]) % quantum
    flat = jnp.pad(flat, (0, pad))
    return flat.reshape(rows, -1)


def _unpack(flat, shapes):
    flat = flat.reshape(-1)
    out, off = [], 0
    for shp in shapes:
        size = 1
        for d in shp:
            size *= d
        out.append(flat[off:off + size].reshape(shp))
        off += size
    return out


def _mlp_fwd(x1, mod, lnp, w1, w2, tag):
    h2 = _modulate(x1, mod, 3, 4, f"{tag}_mod")
    a1, a2 = _mm(h2, w1, "nn", name=f"{tag}_up", out_dtypes=(_MXU, _MXU),
                 out_fn=lambda r: (r, jnp.square(jnp.maximum(r, 0.0))))
    y2 = _mm(a2, w2, "nn", name=f"{tag}_down")
    x2 = _combine(x1, y2, mod, 5, lnp, 2, f"{tag}_ln")
    return x2, (x1, h2, a1, a2, y2)


def _mlp_bwd(dx2, saved, mod, lnp, w1, w2, tag):
    x1, h2, a1, a2, y2 = saved
    dxa, dy2, dp = _combine_bwd(x1, y2, mod, 5, lnp, 2, dx2, f"{tag}_ln_b")
    da1 = _mm(dy2, w2, "nt", name=f"{tag}_down_bx", out_dtypes=(_MXU,), aux=[(a1, "mn")],
              out_fn=lambda r, a: (r * (2.0 * jnp.maximum(a.astype(F32), 0.0)),))
    dw2 = _mm(a2, dy2, "tn", name=f"{tag}_down_bw")
    dw1 = _mm(h2, da1, "tn", name=f"{tag}_up_bw")
    dh2 = _mm(da1, w1, "nt", name=f"{tag}_up_bx")
    dx1, dss = _modulate_bwd(x1, mod, 4, dh2, dxa, f"{tag}_mod_b")
    return dx1, dw1, dw2, (dss, dp)


def _dn_fwd(x, mod, lnp, wts, H, tag):
    w_main, w_small, conv_w, prm, nw, w_out = wts
    h = _modulate(x, mod, 0, 1, f"{tag}_mod")
    pm = _mm(h, w_main, "nn", name=f"{tag}_in")
    ps = _mm(h, w_small, "nn", name=f"{tag}_in_s")
    nqkv = conv_w.shape[1] // LANES
    qkv = _conv_silu(pm, conv_w, nqkv, f"{tag}_conv")
    gates = _gates(ps, prm, H, f"{tag}_gates")
    og, states = _delta_fwd(qkv, pm, gates, nw, H, f"{tag}_delta")
    y = _mm(og, w_out, "nn", name=f"{tag}_out")
    x1 = _combine(x, y, mod, 2, lnp, 0, f"{tag}_ln")
    return x1, (x, h, pm, ps, qkv, gates, states, og, y)


def _dn_bwd(dx1, saved, mod, lnp, wts, H, tag):
    w_main, w_small, conv_w, prm, nw, w_out = wts
    x, h, pm, ps, qkv, gates, states, og, y = saved
    dxa, dy, dp = _combine_bwd(x, y, mod, 2, lnp, 0, dx1, f"{tag}_ln_b")
    dog = _mm(dy, w_out, "nt", name=f"{tag}_out_bx")
    dw_out = _mm(og, dy, "tn", name=f"{tag}_out_bw")
    dq, dk, dv, dpm, dgates, dnw = _delta_bwd(qkv, pm, gates, nw, states, dog, H, f"{tag}_delta_b")
    dps, dprm = _gates_bwd(ps, prm, dgates, H, f"{tag}_gates_b")
    dcw = []
    nb = dq.shape[1] // LANES
    for part, dpart in enumerate((dq, dk, dv)):
        dpm, dcw_p = _conv_silu_bwd(pm, conv_w, dpart, dpm, part * nb, f"{tag}_conv_b{part}")
        dcw.append(dcw_p)
    dconv_w = jnp.concatenate(dcw, axis=1)
    dw_main = _mm(h, dpm, "tn", name=f"{tag}_in_bw")
    dw_small = _mm(h, dps, "tn", name=f"{tag}_in_s_bw")
    dh_s = _mm(dps, w_small, "nt", name=f"{tag}_in_s_bx")
    dh = _mm(dpm, w_main, "nt", name=f"{tag}_in_bx", aux=[(dh_s, "mn")], out_fn=lambda r, e: (r + e,))
    dx, dss = _modulate_bwd(x, mod, 1, dh, dxa, f"{tag}_mod_b")
    return dx, (dw_main, dw_small, dconv_w, dprm, dnw, dw_out), (dss, dp)


def _cf_fwd(x, mod, lnp, wts, tag):
    w_in, dw_w, dw_b, cln, w_out = wts
    h = _modulate(x, mod, 0, 1, f"{tag}_mod")
    p = _mm(h, w_in, "nn", name=f"{tag}_in")
    u2 = _glu_conv(p, dw_w, dw_b, f"{tag}_conv")
    u3 = _ln_silu(u2, cln, f"{tag}_cln")
    y = _mm(u3, w_out, "nn", name=f"{tag}_out")
    x1 = _combine(x, y, mod, 2, lnp, 0, f"{tag}_ln")
    return x1, (x, h, p, u2, u3, y)


def _cf_bwd(dx1, saved, mod, lnp, wts, tag):
    w_in, dw_w, dw_b, cln, w_out = wts
    x, h, p, u2, u3, y = saved
    dxa, dy, dp = _combine_bwd(x, y, mod, 2, lnp, 0, dx1, f"{tag}_ln_b")
    du3 = _mm(dy, w_out, "nt", name=f"{tag}_out_bx")
    dw_out = _mm(u3, dy, "tn", name=f"{tag}_out_bw")
    du2, dcln = _ln_silu_bwd(u2, cln, du3, f"{tag}_cln_b")
    dval, dgate, ddw_w, ddw_b = _glu_conv_bwd(p, dw_w, du2, f"{tag}_conv_b")
    dpp = jnp.concatenate([dval, dgate], axis=1)
    dw_in = _mm(h, dpp, "tn", name=f"{tag}_in_bw")
    dh = _mm(dpp, w_in, "nt", name=f"{tag}_in_bx")
    dx, dss = _modulate_bwd(x, mod, 1, dh, dxa, f"{tag}_mod_b")
    return dx, (dw_in, ddw_w, ddw_b, dcln, dw_out), (dss, dp)


def _two_d(a):
    return a.reshape(-1, a.shape[-1])


def kernel(x, c, ada_w, ada_b, ln_g, ln_b, dn_w_in, dn_conv_w, dn_a_log, dn_dt_bias, dn_norm_w, dn_w_out, cf_w_in, cf_dw_w, cf_dw_b, cf_ln_g, cf_ln_b, cf_w_out, ff_w1, ff_w2, loss_target, m_ada_w, m_ada_b, m_ln_g, m_ln_b, m_dn_w_in, m_dn_conv_w, m_dn_a_log, m_dn_dt_bias, m_dn_norm_w, m_dn_w_out, m_cf_w_in, m_cf_dw_w, m_cf_dw_b, m_cf_ln_g, m_cf_ln_b, m_cf_w_out, m_ff_w1, m_ff_w2, v_ada_w, v_ada_b, v_ln_g, v_ln_b, v_dn_w_in, v_dn_conv_w, v_dn_a_log, v_dn_dt_bias, v_dn_norm_w, v_dn_w_out, v_cf_w_in, v_cf_dw_w, v_cf_dw_b, v_cf_ln_g, v_cf_ln_b, v_cf_w_out, v_ff_w1, v_ff_w2):
    ix, iy, ic = lax.axis_index("x"), lax.axis_index("y"), lax.axis_index("c")
    chip = 2 * ix + iy
    dev = 4 * ix + 2 * iy + ic
    S, D = x.shape[1], x.shape[2]
    L = ada_w.shape[0]
    LA, LB = dn_w_in.shape[0], cf_w_in.shape[0]
    H = dn_a_log.shape[1]
    NMOD = ada_b.shape[1] // D
    dn_in = dn_w_in.shape[2] * 4
    n_main = dn_in - 2 * H
    assert L == N_LAYERS and 2 * H <= LANES
    x0, tgt = x[0], loss_target[0]

    small_sharded = [ln_g, ln_b, dn_conv_w, cf_dw_w, cf_dw_b, cf_ln_g, cf_ln_b]
    small_axes = [2, 2, 2, 2, 1, 1, 1]
    packed_small = _pack(small_sharded, rows=8)[None]
    def my_layers(w):
        half = w.shape[0] // 2
        return lax.dynamic_slice_in_dim(w, ic * half, half, 0).astype(_MXU)

    h_dn_in, h_dn_out, h_cf_in, h_cf_out, h_w1, h_w2, g_small = _exchange(
        [(my_layers(dn_w_in)[None], 0), (my_layers(dn_w_out), 1), (my_layers(cf_w_in), 2),
         (my_layers(cf_w_out), 1), (my_layers(ff_w1), 2), (my_layers(ff_w2), 1), (packed_small, 0)],
        "xy", "gather", "gather_weights", nsplit=2)
    g_dn_in, g_dn_out, g_cf_in, g_cf_out, g_w1, g_w2 = _exchange(
        [(h_dn_in, 1), (h_dn_out, 0), (h_cf_in, 0), (h_cf_out, 0), (h_w1, 0), (h_w2, 0)],
        "c", "gather", "share_weights", nsplit=16)
    w_dn_in = jnp.transpose(g_dn_in, (1, 2, 0, 3)).reshape(LA, D, dn_in)
    w_dn_main = w_dn_in[:, :, :n_main]
    w_dn_small = jnp.pad(w_dn_in[:, :, n_main:], ((0, 0), (0, 0), (0, LANES - 2 * H)))
    shard_shapes = [a.shape for a in small_sharded]
    per_chip = [_unpack(g_small[q], shard_shapes) for q in range(4)]
    ln_g_f, ln_b_f, conv_w_f, dw_w_f, dw_b_f, cln_g_f, cln_b_f = [
        jnp.concatenate([per_chip[q][i] for q in range(4)], axis=small_axes[i]) for i in range(len(small_sharded))]

    c_all = _exchange([(c[None], 0)], "all", "gather", "gather_cond")[0].reshape(8, D)
    c_pad = jnp.pad(c_all, ((0, 8), (0, 0)))
    mod_sh = jnp.stack([_mm(c_pad, (ada_w, l), "nn", name=f"ada_{l}", a_fn=lambda t: t * _sigmoid(t))
                        for l in range(L)])
    mod_all = _exchange([(mod_sh, 2)], "xy", "gather", "gather_mod")[0]
    mod_mine = lax.dynamic_index_in_dim(mod_all, dev, axis=1, keepdims=False) + ada_b
    mods = mod_mine.reshape(L, NMOD, D)

    def lnp_of(l):
        return jnp.stack([ln_g_f[l, 0], ln_b_f[l, 0], ln_g_f[l, 1], ln_b_f[l, 1]])

    def dn_wts(j):
        prm = jnp.zeros((2, LANES), F32).at[0, H:2 * H].set(dn_a_log[j]).at[1, H:2 * H].set(dn_dt_bias[j])
        return ((w_dn_main, j), (w_dn_small, j), conv_w_f[j], prm, dn_norm_w[j][None], (g_dn_out, j))

    def cf_wts(j):
        return ((g_cf_in, j), dw_w_f[j], dw_b_f[j][None], jnp.stack([cln_g_f[j], cln_b_f[j]]), (g_cf_out, j))

    xs = x0
    saved = []
    for l in range(L):
        j = l // 2
        if l % 2 == 0:
            xs, sv_a = _dn_fwd(xs, mods[l], lnp_of(l), dn_wts(j), H, f"l{l}_dn")
        else:
            xs, sv_a = _cf_fwd(xs, mods[l], lnp_of(l), cf_wts(j), f"l{l}_cf")
        xs, sv_b = _mlp_fwd(xs, mods[l], lnp_of(l), (g_w1, l), (g_w2, l), f"l{l}_ff")
        saved.append((sv_a, sv_b))
    dx, loss_local = _loss_head(xs, tgt, "loss_head")
    loss = lax.psum(loss_local[0, 0], ("x", "y", "c"))

    gw1, gw2 = [None] * L, [None] * L
    g_dn = [None] * LA
    g_cf = [None] * LB
    dmods, dlns = [None] * L, [None] * L
    for l in reversed(range(L)):
        j = l // 2
        sv_a, sv_b = saved[l]
        dx, gw1[l], gw2[l], (dss2, dp2) = _mlp_bwd(dx, sv_b, mods[l], lnp_of(l), (g_w1, l), (g_w2, l), f"l{l}_ff")
        if l % 2 == 0:
            dx, g_dn[j], (dss1, dp1) = _dn_bwd(dx, sv_a, mods[l], lnp_of(l), dn_wts(j), H, f"l{l}_dn")
        else:
            dx, g_cf[j], (dss1, dp1) = _cf_bwd(dx, sv_a, mods[l], lnp_of(l), cf_wts(j), f"l{l}_cf")
        dmods[l] = jnp.concatenate([dss1, dp1[0:1], dss2, dp2[0:1]], axis=0)
        dlns[l] = (jnp.stack([dp1[1], dp2[1]]), jnp.stack([dp1[2], dp2[2]]))
    grad_x = dx[None]

    d_ln_g = jnp.stack([dlns[l][0] for l in range(L)])
    d_ln_b = jnp.stack([dlns[l][1] for l in range(L)])
    d_conv_w = jnp.stack([g_dn[j][2] for j in range(LA)])
    d_a_log = jnp.stack([g_dn[j][3][0, H:2 * H] for j in range(LA)])
    d_dt_bias = jnp.stack([g_dn[j][3][1, H:2 * H] for j in range(LA)])
    d_norm_w = jnp.stack([g_dn[j][4][0] for j in range(LA)])
    d_dw_w = jnp.stack([g_cf[j][1] for j in range(LB)])
    d_dw_b = jnp.stack([g_cf[j][2][0] for j in range(LB)])
    d_cln_g = jnp.stack([g_cf[j][3][0] for j in range(LB)])
    d_cln_b = jnp.stack([g_cf[j][3][1] for j in range(LB)])
    d_mod = jnp.stack(dmods).reshape(L, NMOD * D)
    small_full = [d_mod, d_ln_g, d_ln_b, d_conv_w, d_dw_w, d_dw_b, d_cln_g, d_cln_b, d_a_log, d_dt_bias, d_norm_w]
    small_all = _exchange([(_pack(small_full, rows=8)[None], 0)], "all", "gather", "gather_small_grads")[0]
    small_sum = _sum_slots(small_all, "sum_small_grads")
    (s_ada_b, s_ln_g, s_ln_b, s_conv_w, s_dw_w, s_dw_b, s_cln_g, s_cln_b, s_a_log, s_dt_bias, s_norm_w) = _unpack(
        small_sum, [a.shape for a in small_full])
    d_mod_all = small_all.reshape(8, -1)[:, :L * NMOD * D].reshape(8, L, NMOD * D)

    def shard(a, axis):
        size = a.shape[axis] // 4
        return lax.dynamic_slice_in_dim(a, chip * size, size, axis)

    ncol = ada_w.shape[2]
    d_mod_sh = jnp.pad(lax.dynamic_slice_in_dim(d_mod_all, chip * ncol, ncol, 2), ((0, 8), (0, 0), (0, 0)))
    g_ada_w = jnp.stack([_mm(c_pad, d_mod_sh[:, l], "tn", name=f"ada_bw_{l}", a_fn=lambda t: t * _sigmoid(t))
                         for l in range(L)])

    dn_in_full = jnp.stack([jnp.concatenate([g_dn[j][0], g_dn[j][1][:, :2 * H]], axis=1) for j in range(LA)])
    dn_in_st = jnp.transpose(dn_in_full.reshape(LA, D, 4, dn_in // 4), (0, 2, 1, 3))
    big = [(dn_in_st, 1), (jnp.stack([g_dn[j][5] for j in range(LA)]), 1),
           (jnp.stack([g_cf[j][0] for j in range(LB)]), 2), (jnp.stack([g_cf[j][4] for j in range(LB)]), 1),
           (jnp.stack(gw1), 2), (jnp.stack(gw2), 1)]
    pair = _exchange([(g, 0) for g, _ in big], "c", "scatter", "pair_grads", nsplit=8, keep_own=False)
    pair_sums = [_sum_pair(g.reshape(2, -1, g.shape[-1]), r.reshape(2, -1, r.shape[-1]), ic, f"sum_pair_{i}",
                           _MXU).reshape(r.shape[1:])
                 for i, ((g, _), r) in enumerate(zip(big, pair))]
    recv = _exchange([(s, ax) for s, (_, ax) in zip(pair_sums, big)], "xy", "scatter", "scatter_grads")
    sums = [_sum_slots(r.reshape(4, -1, r.shape[-1]), f"sum_grads_{i}") for i, r in enumerate(recv)]
    halves = _exchange([(s[None], 0) for s in sums], "c", "gather", "swap_sums", nsplit=4)
    both = [hv.reshape(1, -1, hv.shape[-1]) for hv in halves]

    big_w = [(dn_w_in, m_dn_w_in, v_dn_w_in), (dn_w_out, m_dn_w_out, v_dn_w_out), (cf_w_in, m_cf_w_in, v_cf_w_in),
             (cf_w_out, m_cf_w_out, v_cf_w_out), (ff_w1, m_ff_w1, v_ff_w1), (ff_w2, m_ff_w2, v_ff_w2)]
    big_out = []
    for i, ((w, m, v), st) in enumerate(zip(big_w, both)):
        res = _adamw(_two_d(w), _two_d(m), _two_d(v), st, f"adamw_big_{i}")
        big_out.append([r.reshape(w.shape) for r in res])
    ada_out = [r.reshape(ada_w.shape) for r in _adamw(_two_d(ada_w), _two_d(m_ada_w), _two_d(v_ada_w),
                                                     _two_d(g_ada_w)[None], "adamw_ada_w")]

    small_w = [(ada_b, m_ada_b, v_ada_b, s_ada_b), (ln_g, m_ln_g, v_ln_g, shard(s_ln_g, 2)),
               (ln_b, m_ln_b, v_ln_b, shard(s_ln_b, 2)), (dn_conv_w, m_dn_conv_w, v_dn_conv_w, shard(s_conv_w, 2)),
               (dn_a_log, m_dn_a_log, v_dn_a_log, s_a_log), (dn_dt_bias, m_dn_dt_bias, v_dn_dt_bias, s_dt_bias),
               (dn_norm_w, m_dn_norm_w, v_dn_norm_w, s_norm_w), (cf_dw_w, m_cf_dw_w, v_cf_dw_w, shard(s_dw_w, 2)),
               (cf_dw_b, m_cf_dw_b, v_cf_dw_b, shard(s_dw_b, 1)), (cf_ln_g, m_cf_ln_g, v_cf_ln_g, shard(s_cln_g, 1)),
               (cf_ln_b, m_cf_ln_b, v_cf_ln_b, shard(s_cln_b, 1))]
    pk = [_pack([t[i] for t in small_w], rows=8) for i in range(4)]
    small_res = _adamw(pk[0], pk[1], pk[2], pk[3][None], "adamw_small")
    small_shapes = [t[0].shape for t in small_w]
    small_out = [_unpack(r, small_shapes) for r in small_res]

    def kind(k):
        sm = small_out[k]
        bg = [o[k] for o in big_out]
        return [ada_out[k], sm[0], sm[1], sm[2], bg[0], sm[3], sm[4], sm[5], sm[6], bg[1],
                bg[2], sm[7], sm[8], sm[9], sm[10], bg[3], bg[4], bg[5]]

    return (loss, grad_x, *kind(0), *kind(1), *kind(2), *kind(3))
```

```python
import functools

import jax
import jax.numpy as jnp
from jax import lax
from jax.experimental import pallas as pl
from jax.experimental.pallas import tpu as pltpu

F32 = jnp.float32
_MXU = jnp.bfloat16
_HI = lax.Precision.HIGHEST

N_LAYERS = 4
ALPHA = (2.0 * N_LAYERS) ** 0.25
LN_EPS = 1e-5
RMS_EPS = 1e-6
L2_EPS = 1e-6
CHUNK = 64
ADAM_LR, ADAM_B1, ADAM_B2, ADAM_EPS, ADAM_WD, ADAM_STEP = 0.001, 0.9, 0.999, 1e-08, 0.01, 10

LANES = 128
TOKEN_BLOCK = 256
VMEM_BIG = 48 * 1024 * 1024
MM_TILE = 1024

SDS = jax.ShapeDtypeStruct
MESH = pl.DeviceIdType.MESH


def _cparams(vmem=None):
    if vmem is None:
        return None
    return pltpu.CompilerParams(vmem_limit_bytes=vmem)


def _pcall(body, **kw):
    if kw.get("compiler_params", 1) is None:
        kw.pop("compiler_params")
    return pl.pallas_call(body, **kw)


def _full(arr):
    nd = arr.ndim
    return pl.BlockSpec(arr.shape, lambda *g: (0,) * nd)


def _bs(block, imap, lead=None):
    if lead is None:
        return pl.BlockSpec(block, imap)
    return pl.BlockSpec((None,) + tuple(block), lambda *g: (lead,) + tuple(imap(*g)))


def _split(a):
    return a if isinstance(a, tuple) else (a, None)


_GROUPS = {
    "xy": ([(1, 0, 0), (0, 1, 0), (1, 1, 0)], 4),
    "c": ([(0, 0, 1)], 2),
    "all": ([(1, 0, 0), (0, 1, 0), (1, 1, 0), (0, 0, 1), (1, 0, 1), (0, 1, 1), (1, 1, 1)], 8),
}


def _exchange(items, group, mode, name, nsplit=1, keep_own=True):
    masks, n = _GROUPS[group]
    npeer = len(masks)
    ni = len(items)
    arrs = [a for a, _ in items]
    out_shapes = []
    for a, ax in items:
        shp = list(a.shape)
        if mode == "gather":
            shp[ax] *= n
        else:
            shp[ax] //= n
            shp = [n] + shp
        out_shapes.append(SDS(tuple(shp), a.dtype))

    def body(*refs):
        ins, outs = refs[:ni], refs[ni:2 * ni]
        send_sems, recv_sems, local_sems = refs[2 * ni:]
        x, y, c = lax.axis_index("x"), lax.axis_index("y"), lax.axis_index("c")

        def slot(px, py, pc):
            if group == "xy":
                return 2 * px + py
            if group == "c":
                return pc
            return 4 * px + 2 * py + pc

        me = slot(x, y, c)

        def block(ref, ax, idx, size):
            ix = (slice(None),) * ax + (pl.ds(pl.multiple_of(idx * size, size), size),)
            return ref.at[ix]

        copies = []
        for it, (a, ax) in enumerate(items):
            in_ref, out_ref = ins[it], outs[it]
            if mode == "gather":
                size = a.shape[ax]
                src_own, dst_own = in_ref, block(out_ref, ax, me, size)
            else:
                size = a.shape[ax] // n
                src_own, dst_own = block(in_ref, ax, me, size), out_ref.at[me]
            sax, ns, cs = splits[it]
            pieces = [(slice(None),) * sax + (pl.ds(j * cs, cs),) for j in range(ns)]
            if keep_own:
                for j, piece in enumerate(pieces):
                    own = pltpu.make_async_copy(src_own.at[piece], dst_own.at[piece], local_sems.at[it * nsplit + j])
                    own.start()
                    copies.append(own)
            for k, m in enumerate(masks):
                peer = tuple((1 - v) if b else v for v, b in zip((x, y, c), m))
                if mode == "gather":
                    src, dst = in_ref, dst_own
                else:
                    src, dst = block(in_ref, ax, slot(*peer), size), out_ref.at[me]
                for j, piece in enumerate(pieces):
                    sem = (it * npeer + k) * nsplit + j
                    cp = pltpu.make_async_remote_copy(
                        src_ref=src.at[piece], dst_ref=dst.at[piece], send_sem=send_sems.at[sem],
                        recv_sem=recv_sems.at[sem], device_id=peer, device_id_type=MESH)
                    cp.start()
                    copies.append(cp)
        for cp in copies:
            cp.wait()

    splits = []
    for (a, ax), o in zip(items, out_shapes):
        bshape = a.shape if mode == "gather" else o.shape[1:]
        sax = max(range(len(bshape) - 1), key=lambda d: bshape[d])
        ns = nsplit if bshape[sax] % (nsplit * 16) == 0 else 1
        splits.append((sax, ns, bshape[sax] // ns))
    any_spec = pl.BlockSpec(memory_space=pl.ANY)
    nsem = ni * npeer * nsplit
    outs = _pcall(
        body, name=name, out_shape=tuple(out_shapes),
        in_specs=[any_spec] * ni, out_specs=tuple([any_spec] * ni),
        scratch_shapes=[pltpu.SemaphoreType.DMA((nsem,)), pltpu.SemaphoreType.DMA((nsem,)),
                        pltpu.SemaphoreType.DMA((ni * nsplit,))],
    )(*arrs)
    return list(outs)


def _gather_two_level(items, name):
    masks = _GROUPS["xy"][0]
    ni, npeer = len(items), len(masks)

    def body(*refs):
        outs = refs[ni:2 * ni]
        a_send, a_recv, b_send, b_recv = refs[2 * ni:]
        x, y, c = lax.axis_index("x"), lax.axis_index("y"), lax.axis_index("c")
        sibling = (x, y, 1 - c)
        peers = [tuple((1 - v) if b else v for v, b in zip((x, y), m)) for m in masks]

        def blk(it, layer_half, chip):
            full, ax = items[it]
            half, size = full.shape[0] // 2, full.shape[ax] // 4
            ix = [slice(None)] * (ax + 1)
            ix[0] = pl.ds(layer_half * half, half)
            ix[ax] = pl.ds(pl.multiple_of(chip * size, size), size)
            return outs[it].at[tuple(ix)]

        def copy(src_dst, sems, k, to):
            return pltpu.make_async_remote_copy(src_ref=src_dst, dst_ref=src_dst, send_sem=sems[0].at[k],
                                                recv_sem=sems[1].at[k], device_id=to, device_id_type=MESH)

        first, passed = [], []
        for it in range(ni):
            for k, (px, py) in enumerate(peers):
                cp = copy(blk(it, c, 2 * x + y), (a_send, a_recv), it * npeer + k, (px, py, c))
                cp.start()
                first.append(cp)
        for it in range(ni):
            for k, (px, py) in enumerate(peers):
                landed = blk(it, c, 2 * px + py)
                copy(landed, (a_send, a_recv), it * npeer + k, (px, py, c)).wait_recv()
                cp = copy(landed, (b_send, b_recv), it * npeer + k, sibling)
                cp.start()
                passed.append(cp)
        for it in range(ni):
            for k, (px, py) in enumerate(peers):
                copy(blk(it, 1 - c, 2 * px + py), (b_send, b_recv), it * npeer + k, sibling).wait_recv()
        for cp in first + passed:
            cp.wait_send()

    any_spec = pl.BlockSpec(memory_space=pl.ANY)
    arrs = [a for a, _ in items]
    outs = _pcall(
        body, name=name, out_shape=tuple(SDS(a.shape, a.dtype) for a in arrs),
        in_specs=[any_spec] * ni, out_specs=tuple([any_spec] * ni),
        input_output_aliases={i: i for i in range(ni)},
        scratch_shapes=[pltpu.SemaphoreType.DMA((ni * npeer,)) for _ in range(4)],
    )(*arrs)
    return list(outs)


def _tile(n, cap):
    if n <= cap:
        return n
    t = cap - cap % LANES
    while n % t:
        t -= LANES
    return t


def _mm(a, b, mode, *, name, out_dtypes=(F32,), tm=MM_TILE, tn=MM_TILE, tk=MM_TILE, a_fn=None, out_fn=None, aux=(),
        out_imap=None, out_shape=None):
    (a, a_lead), (b, b_lead) = _split(a), _split(b)
    ash, bsh = a.shape[-2:], b.shape[-2:]
    if mode == "nn":
        (M, K), (_, N) = ash, bsh
    elif mode == "nt":
        (M, K), (N, _) = ash, bsh
    else:
        (K, M), (_, N) = ash, bsh
    tm, tn, tk = _tile(M, tm), _tile(N, tn), _tile(K, tk)
    nk = K // tk
    if mode == "tn":
        a_spec = _bs((tk, tm), lambda i, j, k: (k, i), a_lead)
    else:
        a_spec = _bs((tm, tk), lambda i, j, k: (i, k), a_lead)
    if mode == "nt":
        b_spec = _bs((tn, tk), lambda i, j, k: (j, k), b_lead)
    else:
        b_spec = _bs((tk, tn), lambda i, j, k: (k, j), b_lead)
    aux_arrs, aux_specs = [], []
    for arr, kind in aux:
        arr, lead = _split(arr)
        aux_arrs.append(arr)
        if kind == "mn":
            aux_specs.append(_bs((tm, tn), lambda i, j, k: (i, j), lead))
        else:
            aux_specs.append(_bs((1, tn), lambda i, j, k: (0, j), lead))
    na, no = len(aux_arrs), len(out_dtypes)
    dims = {"nn": (((1,), (0,)), ((), ())), "nt": (((1,), (1,)), ((), ())), "tn": (((0,), (0,)), ((), ()))}[mode]

    def finish(r, aux_refs, o_refs):
        outs = out_fn(r, *[x[...] for x in aux_refs]) if out_fn is not None else (r,)
        for o_ref, val in zip(o_refs, outs):
            o_ref[...] = val.astype(o_ref.dtype)

    def product(a_ref, b_ref):
        av = a_ref[...]
        if a_fn is not None:
            av = a_fn(av.astype(F32))
        return lax.dot_general(av.astype(_MXU), b_ref[...].astype(_MXU), dims, preferred_element_type=F32)

    def body_one(a_ref, b_ref, *rest):
        finish(product(a_ref, b_ref), rest[:na], rest[na:na + no])

    def body_acc(a_ref, b_ref, *rest):
        aux_refs, o_refs, acc = rest[:na], rest[na:na + no], rest[na + no]
        k = pl.program_id(2)

        @pl.when(k == 0)
        def _():
            acc[...] = product(a_ref, b_ref)

        @pl.when(k != 0)
        def _():
            acc[...] += product(a_ref, b_ref)

        @pl.when(k == nk - 1)
        def _():
            finish(acc[...], aux_refs, o_refs)

    if out_shape is None:
        out_shape = (M, N)
    if out_imap is None:
        o_spec = pl.BlockSpec((tm, tn), lambda i, j, k: (i, j))
    else:
        o_spec = pl.BlockSpec((None,) * (len(out_shape) - 2) + (tm, tn), out_imap)
    outs = _pcall(
        body_one if nk == 1 else body_acc, name=name, grid=(M // tm, N // tn, nk),
        in_specs=[a_spec, b_spec] + aux_specs, out_specs=tuple([o_spec] * no),
        out_shape=tuple(SDS(out_shape, dt) for dt in out_dtypes),
        scratch_shapes=[] if nk == 1 else [pltpu.VMEM((tm, tn), F32)],
        compiler_params=pltpu.CompilerParams(dimension_semantics=("parallel", "parallel", "arbitrary"),
                                             vmem_limit_bytes=VMEM_BIG),
    )(a, b, *aux_arrs)
    return outs[0] if no == 1 else outs


def _tok(S):
    ts = min(TOKEN_BLOCK, S)
    assert S % ts == 0
    return ts


def _row(ts, D):
    return pl.BlockSpec((ts, D), lambda i: (i, 0))


def _acc_rows(ref, i, rows):
    @pl.when(i == 0)
    def _():
        for r, v in enumerate(rows):
            ref[r:r + 1, :] = v

    @pl.when(i != 0)
    def _():
        for r, v in enumerate(rows):
            ref[r:r + 1, :] += v


def _modulate(x, mod, r_sh, r_sc, name):
    S, D = x.shape
    ts = _tok(S)

    def body(x_ref, m_ref, o_ref):
        o_ref[...] = (x_ref[...] * (1.0 + m_ref[r_sc:r_sc + 1, :]) + m_ref[r_sh:r_sh + 1, :]).astype(o_ref.dtype)

    return _pcall(body, name=name, grid=(S // ts,), in_specs=[_row(ts, D), _full(mod)],
                  out_specs=_row(ts, D), out_shape=SDS((S, D), _MXU))(x, mod)


def _modulate_bwd(x, mod, r_sc, dh, dxa, name):
    S, D = x.shape
    ts = _tok(S)

    def body(x_ref, m_ref, dh_ref, dxa_ref, dx_ref, dss_ref):
        dh_v = dh_ref[...]
        dx_ref[...] = dxa_ref[...] + dh_v * (1.0 + m_ref[r_sc:r_sc + 1, :])
        _acc_rows(dss_ref, pl.program_id(0),
                  [jnp.sum(dh_v, axis=0, keepdims=True), jnp.sum(dh_v * x_ref[...], axis=0, keepdims=True)])

    return _pcall(body, name=name, grid=(S // ts,),
                  in_specs=[_row(ts, D), _full(mod), _row(ts, D), _row(ts, D)],
                  out_specs=(_row(ts, D), pl.BlockSpec((2, D), lambda i: (0, 0))),
                  out_shape=(SDS((S, D), F32), SDS((2, D), F32)))(x, mod, dh, dxa)


def _norm_stats(z):
    mu = jnp.mean(z, axis=-1, keepdims=True)
    zc = z - mu
    var = jnp.mean(zc * zc, axis=-1, keepdims=True)
    rstd = lax.rsqrt(var + LN_EPS)
    return zc * rstd, rstd


def _norm_bwd(dxhat, xhat, rstd):
    return rstd * (dxhat - jnp.mean(dxhat, axis=-1, keepdims=True)
                   - xhat * jnp.mean(dxhat * xhat, axis=-1, keepdims=True))


def _combine(x, y, mod, r_gt, lnp, r_g, name):
    S, D = x.shape
    ts = _tok(S)

    def body(x_ref, y_ref, m_ref, l_ref, o_ref):
        z = ALPHA * x_ref[...] + (1.0 + m_ref[r_gt:r_gt + 1, :]) * y_ref[...]
        xhat, _ = _norm_stats(z)
        o_ref[...] = xhat * l_ref[r_g:r_g + 1, :] + l_ref[r_g + 1:r_g + 2, :]

    return _pcall(body, name=name, grid=(S // ts,), in_specs=[_row(ts, D), _row(ts, D), _full(mod), _full(lnp)],
                  out_specs=_row(ts, D), out_shape=SDS((S, D), F32))(x, y, mod, lnp)


def _combine_bwd(x, y, mod, r_gt, lnp, r_g, dout, name):
    S, D = x.shape
    ts = _tok(S)

    def body(x_ref, y_ref, m_ref, l_ref, do_ref, dxa_ref, dy_ref, dp_ref):
        gate = 1.0 + m_ref[r_gt:r_gt + 1, :]
        y_v, do_v = y_ref[...], do_ref[...]
        xhat, rstd = _norm_stats(ALPHA * x_ref[...] + gate * y_v)
        dz = _norm_bwd(do_v * l_ref[r_g:r_g + 1, :], xhat, rstd)
        dxa_ref[...] = ALPHA * dz
        dy_ref[...] = (gate * dz).astype(dy_ref.dtype)
        _acc_rows(dp_ref, pl.program_id(0),
                  [jnp.sum(dz * y_v, axis=0, keepdims=True), jnp.sum(do_v * xhat, axis=0, keepdims=True),
                   jnp.sum(do_v, axis=0, keepdims=True)])

    return _pcall(body, name=name, grid=(S // ts,),
                  in_specs=[_row(ts, D), _row(ts, D), _full(mod), _full(lnp), _row(ts, D)],
                  out_specs=(_row(ts, D), _row(ts, D), pl.BlockSpec((3, D), lambda i: (0, 0))),
                  out_shape=(SDS((S, D), F32), SDS((S, D), _MXU), SDS((3, D), F32)))(x, y, mod, lnp, dout)


def _sigmoid(t):
    return 1.0 / (1.0 + jnp.exp(-t))


def _ln_silu(u, lnp, name):
    S, D = u.shape
    ts = _tok(S)

    def body(u_ref, l_ref, o_ref):
        xhat, _ = _norm_stats(u_ref[...])
        t = xhat * l_ref[0:1, :] + l_ref[1:2, :]
        o_ref[...] = (t * _sigmoid(t)).astype(o_ref.dtype)

    return _pcall(body, name=name, grid=(S // ts,), in_specs=[_row(ts, D), _full(lnp)],
                  out_specs=_row(ts, D), out_shape=SDS((S, D), _MXU))(u, lnp)


def _ln_silu_bwd(u, lnp, dout, name):
    S, D = u.shape
    ts = _tok(S)

    def body(u_ref, l_ref, do_ref, du_ref, dp_ref):
        xhat, rstd = _norm_stats(u_ref[...])
        g = l_ref[0:1, :]
        t = xhat * g + l_ref[1:2, :]
        sg = _sigmoid(t)
        dt = do_ref[...] * (sg * (1.0 + t * (1.0 - sg)))
        du_ref[...] = _norm_bwd(dt * g, xhat, rstd)
        _acc_rows(dp_ref, pl.program_id(0),
                  [jnp.sum(dt * xhat, axis=0, keepdims=True), jnp.sum(dt, axis=0, keepdims=True)])

    return _pcall(body, name=name, grid=(S // ts,), in_specs=[_row(ts, D), _full(lnp), _row(ts, D)],
                  out_specs=(_row(ts, D), pl.BlockSpec((2, D), lambda i: (0, 0))),
                  out_shape=(SDS((S, D), F32), SDS((2, D), F32)))(u, lnp, dout)


def _loss_head(xf, tgt, name):
    S, D = xf.shape
    ts = _tok(S)

    def body(x_ref, t_ref, dx_ref, l_ref):
        err = x_ref[...] - t_ref[...]
        dx_ref[...] = err * (1.0 / D)
        part = jnp.sum(jnp.sum(err * err, axis=1, keepdims=True), axis=0, keepdims=True) * (0.5 / D)

        @pl.when(pl.program_id(0) == 0)
        def _():
            l_ref[...] = part

        @pl.when(pl.program_id(0) != 0)
        def _():
            l_ref[...] += part

    return _pcall(body, name=name, grid=(S // ts,), in_specs=[_row(ts, D), _row(ts, D)],
                  out_specs=(_row(ts, D), pl.BlockSpec((1, 1), lambda i: (0, 0))),
                  out_shape=(SDS((S, D), F32), SDS((1, 1), F32)))(xf, tgt)


def _shift_down(u, s, rows):
    if s == 0:
        return u
    return jnp.where(rows >= s, pltpu.roll(u, s, 0), 0.0)


def _shift_up(u, s, rows):
    if s == 0:
        return u
    n = u.shape[0]
    return jnp.where(rows < n - s, pltpu.roll(u, n - s, 0), 0.0)


def _dwconv(u, w_ref, taps, rows):
    acc = jnp.zeros_like(u)
    for j in range(taps):
        acc = acc + w_ref[j:j + 1, :] * _shift_down(u, taps - 1 - j, rows)
    return acc


def _dwconv_bwd(u, dy, w_ref, dw_ref, taps, rows):
    du = jnp.zeros_like(u)
    for j in range(taps):
        s = taps - 1 - j
        du = du + w_ref[j:j + 1, :] * _shift_up(dy, s, rows)
        dw_ref[j:j + 1, :] = jnp.sum(dy * _shift_down(u, s, rows), axis=0, keepdims=True)
    return du


def _col(S, j0=0):
    return pl.BlockSpec((S, LANES), lambda j: (0, j + j0))


def _conv_silu(pm, w, nblk, name):
    S = pm.shape[0]
    taps = w.shape[0]

    def body(u_ref, w_ref, o_ref):
        rows = lax.broadcasted_iota(jnp.int32, (S, LANES), 0)
        cv = _dwconv(u_ref[...], w_ref, taps, rows)
        o_ref[...] = cv * _sigmoid(cv)

    return _pcall(body, name=name, grid=(nblk,),
                  in_specs=[_col(S), pl.BlockSpec((taps, LANES), lambda j: (0, j))],
                  out_specs=_col(S), out_shape=SDS((S, nblk * LANES), F32),
                  compiler_params=_cparams(VMEM_BIG))(pm, w)


def _conv_silu_bwd(pm, w, dout, dpm, j0, name):
    S = pm.shape[0]
    taps = w.shape[0]
    nblk = dout.shape[1] // LANES

    def body(u_ref, w_ref, do_ref, dpm_in, du_ref, dw_ref):
        del dpm_in
        rows = lax.broadcasted_iota(jnp.int32, (S, LANES), 0)
        u = u_ref[...]
        cv = _dwconv(u, w_ref, taps, rows)
        sg = _sigmoid(cv)
        dc = do_ref[...] * (sg * (1.0 + cv * (1.0 - sg)))
        du_ref[...] = _dwconv_bwd(u, dc, w_ref, dw_ref, taps, rows)

    return _pcall(body, name=name, grid=(nblk,),
                  in_specs=[_col(S, j0), pl.BlockSpec((taps, LANES), lambda j: (0, j + j0)), _col(S),
                            pl.BlockSpec(memory_space=pl.ANY)],
                  out_specs=(_col(S, j0), pl.BlockSpec((taps, LANES), lambda j: (0, j))),
                  out_shape=(SDS(dpm.shape, F32), SDS((taps, nblk * LANES), F32)),
                  input_output_aliases={3: 0},
                  compiler_params=_cparams(VMEM_BIG))(pm, w, dout, dpm)


def _glu_conv(p, w, bias, name):
    S, C2 = p.shape
    nblk = C2 // 2 // LANES
    taps = w.shape[0]

    def body(v_ref, g_ref, w_ref, b_ref, o_ref):
        rows = lax.broadcasted_iota(jnp.int32, (S, LANES), 0)
        u = v_ref[...] * _sigmoid(g_ref[...])
        o_ref[...] = _dwconv(u, w_ref, taps, rows) + b_ref[...]

    return _pcall(body, name=name, grid=(nblk,),
                  in_specs=[_col(S), _col(S, nblk), pl.BlockSpec((taps, LANES), lambda j: (0, j)),
                            pl.BlockSpec((1, LANES), lambda j: (0, j))],
                  out_specs=_col(S), out_shape=SDS((S, nblk * LANES), F32),
                  compiler_params=_cparams(VMEM_BIG))(p, p, w, bias)


def _glu_conv_bwd(p, w, dout, name):
    S, C2 = p.shape
    nblk = C2 // 2 // LANES
    taps = w.shape[0]

    def body(v_ref, g_ref, w_ref, do_ref, dv_ref, dg_ref, dw_ref, db_ref):
        rows = lax.broadcasted_iota(jnp.int32, (S, LANES), 0)
        val, sg = v_ref[...], _sigmoid(g_ref[...])
        do_v = do_ref[...]
        du = _dwconv_bwd(val * sg, do_v, w_ref, dw_ref, taps, rows)
        dv_ref[...] = du * sg
        dg_ref[...] = du * val * sg * (1.0 - sg)
        db_ref[...] = jnp.sum(do_v, axis=0, keepdims=True)

    dval, dgate, dw, db = _pcall(
        body, name=name, grid=(nblk,),
        in_specs=[_col(S), _col(S, nblk), pl.BlockSpec((taps, LANES), lambda j: (0, j)), _col(S)],
        out_specs=(_col(S), _col(S), pl.BlockSpec((taps, LANES), lambda j: (0, j)),
                   pl.BlockSpec((1, LANES), lambda j: (0, j))),
        out_shape=(SDS((S, C2 // 2), F32), SDS((S, C2 // 2), F32), SDS((taps, C2 // 2), F32), SDS((1, C2 // 2), F32)),
        compiler_params=_cparams(VMEM_BIG))(p, p, w, dout)
    return dval, dgate, dw, db


def _log1p(e):
    u = 1.0 + e
    d = jnp.where(u == 1.0, 1.0, u - 1.0)
    return jnp.where(u == 1.0, e, jnp.log(u) * (e / d))


def _gate_parts(ps, prm, H):
    lane = lax.broadcasted_iota(jnp.int32, ps.shape, 1)
    is_b, is_g = lane < H, (lane >= H) & (lane < 2 * H)
    beta = _sigmoid(ps)
    t = ps + prm[1:2, :]
    sp = jnp.maximum(t, 0.0) + _log1p(jnp.exp(-jnp.abs(t)))
    na = -jnp.exp(prm[0:1, :])
    return is_b, is_g, beta, t, sp, na


def _gates(ps, prm, H, name):
    S = ps.shape[0]
    ts = _tok(S)

    def body(p_ref, r_ref, o_ref):
        is_b, is_g, beta, _, sp, na = _gate_parts(p_ref[...], r_ref[...], H)
        o_ref[...] = jnp.where(is_b, beta, jnp.where(is_g, na * sp, 0.0))

    return _pcall(body, name=name, grid=(S // ts,), in_specs=[_row(ts, LANES), _full(prm)],
                  out_specs=_row(ts, LANES), out_shape=SDS((S, LANES), F32))(ps, prm)


def _gates_bwd(ps, prm, dgates, H, name):
    S = ps.shape[0]
    ts = _tok(S)

    def body(p_ref, r_ref, dg_ref, dp_ref, dr_ref):
        is_b, is_g, beta, t, sp, na = _gate_parts(p_ref[...], r_ref[...], H)
        dg_v = dg_ref[...]
        dsp = jnp.where(is_g, dg_v * na * _sigmoid(t), 0.0)
        dp_ref[...] = jnp.where(is_b, dg_v * beta * (1.0 - beta), dsp)
        _acc_rows(dr_ref, pl.program_id(0),
                  [jnp.sum(jnp.where(is_g, dg_v * na * sp, 0.0), axis=0, keepdims=True),
                   jnp.sum(dsp, axis=0, keepdims=True)])

    return _pcall(body, name=name, grid=(S // ts,), in_specs=[_row(ts, LANES), _full(prm), _row(ts, LANES)],
                  out_specs=(_row(ts, LANES), pl.BlockSpec((2, LANES), lambda i: (0, 0))),
                  out_shape=(SDS((S, LANES), F32), SDS((2, LANES), F32)))(ps, prm, dgates)


_NN = (((2,), (1,)), ((0,), (0,)))
_NT = (((2,), (2,)), ((0,), (0,)))
_TN = (((1,), (1,)), ((0,), (0,)))


def _mdot(a, b, dims):
    return lax.dot_general(a.astype(_MXU), b.astype(_MXU), dims, preferred_element_type=F32)


def _mdot3(a, b, dims):
    ah, bh = a.astype(_MXU), b.astype(_MXU)
    al, bl = a - ah.astype(F32), b - bh.astype(F32)
    return _mdot(ah, bh, dims) + (_mdot(ah, bl, dims) + _mdot(al, bh, dims))


def _rounded_dot(dims, da_dims, db_dims, a_first, prod=_mdot):
    @jax.custom_vjp
    def f(a, b):
        return prod(a, b, dims)

    def fwd(a, b):
        return prod(a, b, dims), (a, b)

    def bwd(res, ct):
        a, b = res
        da = prod(ct, b, da_dims) if a_first[0] else prod(b, ct, da_dims)
        db = prod(ct, a, db_dims) if a_first[1] else prod(a, ct, db_dims)
        return da, db

    f.defvjp(fwd, bwd)
    return f


_mdot_nn = _rounded_dot(_NN, _NT, _TN, (True, False))
_mdot_nt = _rounded_dot(_NT, _NN, _TN, (True, True))
_mdot_tn = _rounded_dot(_TN, _NT, _NN, (False, False))
_dot = _rounded_dot(_NN, _NT, _TN, (True, False), _mdot3)


def _head_cols(gates, off, H):
    lane = lax.broadcasted_iota(jnp.int32, gates.shape, 1)
    cols = [jnp.sum(jnp.where(lane == off + h, gates, 0.0), axis=-1, keepdims=True) for h in range(H)]
    return jnp.concatenate([col[None] for col in cols], axis=0)


def _delta_chunk(qr, kr, v, z, gates, nw, s_in):
    H, C, dk = qr.shape
    beta, g = _head_cols(gates, 0, H), _head_cols(gates, H, H)
    q = qr * lax.rsqrt(jnp.sum(qr * qr, axis=-1, keepdims=True) + L2_EPS) * (dk ** -0.5)
    k = kr * lax.rsqrt(jnp.sum(kr * kr, axis=-1, keepdims=True) + L2_EPS)
    ri = lax.broadcasted_iota(jnp.int32, (1, C, C), 1)
    ci = lax.broadcasted_iota(jnp.int32, (1, C, C), 2)
    causal, strict, eye = ri >= ci, ri > ci, ri == ci
    gam_row = jnp.sum(jnp.where(ri <= ci, g, 0.0), axis=1, keepdims=True)
    gam_col = jnp.sum(jnp.where(eye, gam_row, 0.0), axis=-1, keepdims=True)
    g_last = jnp.sum(g, axis=1, keepdims=True)
    decay = jnp.where(causal, jnp.exp(jnp.where(causal, gam_col - gam_row, 0.0)), 0.0)
    kb = k * beta
    a = jnp.where(strict, _mdot_nt(kb, k) * decay, 0.0)
    t_inv = jnp.where(eye, 1.0, 0.0) - a
    p = a
    for _ in range(max(C.bit_length() - 2, 0)):
        p = _dot(p, p)
        t_inv = t_inv + _dot(t_inv, p)
    eg = jnp.exp(gam_col)
    u = _mdot_nn(t_inv, v * beta)
    w = _mdot_nn(t_inv, kb * eg)
    a_qk = _mdot_nt(q, k) * decay
    v_new = u - _mdot_nn(w, s_in)
    o = _mdot_nn(q * eg, s_in) + _mdot_nn(a_qk, v_new)
    s_out = s_in * jnp.exp(g_last) + _mdot_tn(k * jnp.exp(g_last - gam_col), v_new)
    og = o * lax.rsqrt(jnp.mean(o * o, axis=-1, keepdims=True) + RMS_EPS) * nw * (z * _sigmoid(z))
    return og, s_out


def _heads(ref, H, dk):
    return jnp.stack([ref[:, h * dk:(h + 1) * dk].astype(F32) for h in range(H)])


def _put_heads(ref, val, dk):
    for h in range(val.shape[0]):
        ref[:, h * dk:(h + 1) * dk] = val[h].astype(ref.dtype)


def _delta_fwd(qkv, pm, gates, nw, H, name):
    S = qkv.shape[0]
    hd = qkv.shape[1] // 3
    dk = hd // H
    N = S // CHUNK
    blk = lambda off: pl.BlockSpec((CHUNK, hd), lambda n: (n, off))

    def body(q_ref, k_ref, v_ref, z_ref, g_ref, nw_ref, og_ref, st_ref, s_scr):
        @pl.when(pl.program_id(0) == 0)
        def _():
            s_scr[...] = jnp.zeros_like(s_scr)

        s_in = s_scr[...]
        st_ref[...] = s_in
        og, s_out = _delta_chunk(_heads(q_ref, H, dk), _heads(k_ref, H, dk), _heads(v_ref, H, dk),
                                 _heads(z_ref, H, dk), g_ref[...], nw_ref[...], s_in)
        _put_heads(og_ref, og, dk)
        s_scr[...] = s_out

    return _pcall(
        body, name=name, grid=(N,),
        in_specs=[blk(0), blk(1), blk(2), blk(3), pl.BlockSpec((CHUNK, LANES), lambda n: (n, 0)), _full(nw)],
        out_specs=(blk(0), pl.BlockSpec((None, H, dk, dk), lambda n: (n, 0, 0, 0))),
        out_shape=(SDS((S, hd), _MXU), SDS((N, H, dk, dk), F32)),
        scratch_shapes=[pltpu.VMEM((H, dk, dk), F32)],
        compiler_params=_cparams(VMEM_BIG),
    )(qkv, qkv, qkv, pm, gates, nw)


def _delta_bwd(qkv, pm, gates, nw, states, dog, H, name):
    S = qkv.shape[0]
    hd = qkv.shape[1] // 3
    dk = hd // H
    N = S // CHUNK
    blk = lambda off: pl.BlockSpec((CHUNK, hd), lambda n: (N - 1 - n, off))
    gspec = pl.BlockSpec((CHUNK, LANES), lambda n: (N - 1 - n, 0))

    def body(q_ref, k_ref, v_ref, z_ref, g_ref, nw_ref, st_ref, do_ref,
             dq_ref, dk_ref, dv_ref, dz_ref, dg_ref, dnw_ref, ds_scr):
        n = pl.program_id(0)

        @pl.when(n == 0)
        def _():
            ds_scr[...] = jnp.zeros_like(ds_scr)

        _, vjp = jax.vjp(_delta_chunk, _heads(q_ref, H, dk), _heads(k_ref, H, dk), _heads(v_ref, H, dk),
                         _heads(z_ref, H, dk), g_ref[...], nw_ref[...], st_ref[...])
        dq, dkk, dv, dz, dg, dnw, ds_in = vjp((_heads(do_ref, H, dk), ds_scr[...]))
        _put_heads(dq_ref, dq, dk)
        _put_heads(dk_ref, dkk, dk)
        _put_heads(dv_ref, dv, dk)
        _put_heads(dz_ref, dz, dk)
        ds_scr[...] = ds_in
        dg_ref[...] = dg

        @pl.when(n == 0)
        def _():
            dnw_ref[...] = dnw

        @pl.when(n != 0)
        def _():
            dnw_ref[...] += dnw

    return _pcall(
        body, name=name, grid=(N,),
        in_specs=[blk(0), blk(1), blk(2), blk(3), gspec, _full(nw),
                  pl.BlockSpec((None, H, dk, dk), lambda n: (N - 1 - n, 0, 0, 0)), blk(0)],
        out_specs=(blk(0), blk(0), blk(0), blk(3), gspec, pl.BlockSpec((1, dk), lambda n: (0, 0))),
        out_shape=(SDS((S, hd), F32), SDS((S, hd), F32), SDS((S, hd), F32), SDS(pm.shape, F32),
                   SDS((S, LANES), F32), SDS((1, dk), F32)),
        scratch_shapes=[pltpu.VMEM((H, dk, dk), F32)],
        compiler_params=_cparams(VMEM_BIG),
    )(qkv, qkv, qkv, pm, gates, nw, states, dog)


def _rows_block(R, C):
    rb = R
    while rb * C * 4 > (1 << 20) and rb % 16 == 0:
        rb //= 2
    return rb


def _sum_slots(st, name, out_dtype=F32):
    n, R, C = st.shape
    rb = _rows_block(R, C)

    def body(s_ref, o_ref):
        acc = s_ref[0].astype(F32)
        for q in range(1, n):
            acc = acc + s_ref[q].astype(F32)
        o_ref[...] = acc.astype(o_ref.dtype)

    return _pcall(body, name=name, grid=(R // rb,), in_specs=[pl.BlockSpec((n, rb, C), lambda i: (0, i, 0))],
                  out_specs=pl.BlockSpec((rb, C), lambda i: (i, 0)), out_shape=SDS((R, C), out_dtype))(st)


def _sum_pair(own2, recv2, ic, name, out_dtype):
    _, R, C = own2.shape
    rb = _rows_block(R, C)

    def body(c_ref, a_ref, b_ref, o_ref):
        del c_ref
        o_ref[...] = (a_ref[...].astype(F32) + b_ref[...].astype(F32)).astype(o_ref.dtype)

    grid_spec = pltpu.PrefetchScalarGridSpec(
        num_scalar_prefetch=1, grid=(R // rb,),
        in_specs=[pl.BlockSpec((None, rb, C), lambda i, c: (c[0], i, 0)),
                  pl.BlockSpec((None, rb, C), lambda i, c: (1 - c[0], i, 0))],
        out_specs=pl.BlockSpec((rb, C), lambda i, c: (i, 0)))
    return _pcall(body, name=name, grid_spec=grid_spec, out_shape=SDS((R, C), out_dtype))(
        jnp.reshape(ic, (1,)).astype(jnp.int32), own2, recv2)


def _adamw_step(g, w_ref, m_ref, v_ref, g_ref, d_ref, mo_ref, vo_ref):
    m_new = ADAM_B1 * m_ref[...] + (1.0 - ADAM_B1) * g
    v_new = ADAM_B2 * v_ref[...] + (1.0 - ADAM_B2) * (g * g)
    m_hat = m_new / (1.0 - ADAM_B1 ** ADAM_STEP)
    v_hat = v_new / (1.0 - ADAM_B2 ** ADAM_STEP)
    g_ref[...] = g
    d_ref[...] = -ADAM_LR * (m_hat / (jnp.sqrt(v_hat) + ADAM_EPS) + ADAM_WD * w_ref[...])
    mo_ref[...] = m_new
    vo_ref[...] = v_new


def _adamw(w, m, v, st, name):
    R, C = w.shape
    n = st.shape[0]
    rb = _rows_block(R, C)
    spec = pl.BlockSpec((rb, C), lambda i: (i, 0))

    def body(w_ref, m_ref, v_ref, s_ref, *o_refs):
        g = s_ref[0]
        for q in range(1, n):
            g = g + s_ref[q]
        _adamw_step(g, w_ref, m_ref, v_ref, *o_refs)

    return _pcall(body, name=name, grid=(R // rb,),
                  in_specs=[spec, spec, spec, pl.BlockSpec((n, rb, C), lambda i: (0, i, 0))],
                  out_specs=(spec,) * 4, out_shape=(SDS((R, C), F32),) * 4)(w, m, v, st)


def _adamw_halves(w, m, v, own, recv2, ic, name):
    R, C = w.shape
    rb = _rows_block(R // 2, C)
    nb = R // 2 // rb
    spec = pl.BlockSpec((rb, C), lambda i, c: (i, 0))

    def body(c_ref, w_ref, m_ref, v_ref, own_ref, recv_ref, *o_refs):
        is_own = pl.program_id(0) // nb == c_ref[0]
        _adamw_step(jnp.where(is_own, own_ref[...], recv_ref[...]), w_ref, m_ref, v_ref, *o_refs)

    grid_spec = pltpu.PrefetchScalarGridSpec(
        num_scalar_prefetch=1, grid=(R // rb,),
        in_specs=[spec, spec, spec, pl.BlockSpec((rb, C), lambda i, c: (i % nb, 0)),
                  pl.BlockSpec((None, rb, C), lambda i, c: (1 - c[0], i % nb, 0))],
        out_specs=(spec,) * 4)
    return _pcall(body, name=name, grid_spec=grid_spec, out_shape=(SDS((R, C), F32),) * 4)(
        jnp.reshape(ic, (1,)).astype(jnp.int32), w, m, v, own, recv2)


def _pack(arrs, rows=1):
    flat = jnp.concatenate([a.reshape(-1).astype(F32) for a in arrs])
    quantum = rows * LANES
    pad = (-flat.shape[0]) % quantum
    flat = jnp.pad(flat, (0, pad))
    return flat.reshape(rows, -1)


def _unpack(flat, shapes):
    flat = flat.reshape(-1)
    out, off = [], 0
    for shp in shapes:
        size = 1
        for d in shp:
            size *= d
        out.append(flat[off:off + size].reshape(shp))
        off += size
    return out


def _mlp_fwd(x1, mod, lnp, w1, w2, tag):
    h2 = _modulate(x1, mod, 3, 4, f"{tag}_mod")
    a1, a2 = _mm(h2, w1, "nn", name=f"{tag}_up", out_dtypes=(_MXU, _MXU),
                 out_fn=lambda r: (r, jnp.square(jnp.maximum(r, 0.0))))
    y2 = _mm(a2, w2, "nn", name=f"{tag}_down")
    x2 = _combine(x1, y2, mod, 5, lnp, 2, f"{tag}_ln")
    return x2, (x1, h2, a1, a2, y2)


def _mlp_bwd(dx2, saved, mod, lnp, w1, w2, tag):
    x1, h2, a1, a2, y2 = saved
    dxa, dy2, dp = _combine_bwd(x1, y2, mod, 5, lnp, 2, dx2, f"{tag}_ln_b")
    da1 = _mm(dy2, w2, "nt", name=f"{tag}_down_bx", out_dtypes=(_MXU,), aux=[(a1, "mn")],
              out_fn=lambda r, a: (r * (2.0 * jnp.maximum(a.astype(F32), 0.0)),))
    dw2 = _mm(a2, dy2, "tn", name=f"{tag}_down_bw")
    dw1 = _mm(h2, da1, "tn", name=f"{tag}_up_bw")
    dh2 = _mm(da1, w1, "nt", name=f"{tag}_up_bx")
    dx1, dss = _modulate_bwd(x1, mod, 4, dh2, dxa, f"{tag}_mod_b")
    return dx1, dw1, dw2, (dss, dp)


def _dn_fwd(x, mod, lnp, wts, H, tag):
    w_main, w_small, conv_w, prm, nw, w_out = wts
    h = _modulate(x, mod, 0, 1, f"{tag}_mod")
    pm = _mm(h, w_main, "nn", name=f"{tag}_in")
    ps = _mm(h, w_small, "nn", name=f"{tag}_in_s")
    nqkv = conv_w.shape[1] // LANES
    qkv = _conv_silu(pm, conv_w, nqkv, f"{tag}_conv")
    gates = _gates(ps, prm, H, f"{tag}_gates")
    og, states = _delta_fwd(qkv, pm, gates, nw, H, f"{tag}_delta")
    y = _mm(og, w_out, "nn", name=f"{tag}_out")
    x1 = _combine(x, y, mod, 2, lnp, 0, f"{tag}_ln")
    return x1, (x, h, pm, ps, qkv, gates, states, og, y)


def _dn_bwd(dx1, saved, mod, lnp, wts, H, tag):
    w_main, w_small, conv_w, prm, nw, w_out = wts
    x, h, pm, ps, qkv, gates, states, og, y = saved
    dxa, dy, dp = _combine_bwd(x, y, mod, 2, lnp, 0, dx1, f"{tag}_ln_b")
    dog = _mm(dy, w_out, "nt", name=f"{tag}_out_bx")
    dw_out = _mm(og, dy, "tn", name=f"{tag}_out_bw")
    dq, dk, dv, dpm, dgates, dnw = _delta_bwd(qkv, pm, gates, nw, states, dog, H, f"{tag}_delta_b")
    dps, dprm = _gates_bwd(ps, prm, dgates, H, f"{tag}_gates_b")
    dcw = []
    nb = dq.shape[1] // LANES
    for part, dpart in enumerate((dq, dk, dv)):
        dpm, dcw_p = _conv_silu_bwd(pm, conv_w, dpart, dpm, part * nb, f"{tag}_conv_b{part}")
        dcw.append(dcw_p)
    dconv_w = jnp.concatenate(dcw, axis=1)
    dw_main = _mm(h, dpm, "tn", name=f"{tag}_in_bw")
    dw_small = _mm(h, dps, "tn", name=f"{tag}_in_s_bw")
    dh_s = _mm(dps, w_small, "nt", name=f"{tag}_in_s_bx")
    dh = _mm(dpm, w_main, "nt", name=f"{tag}_in_bx", aux=[(dh_s, "mn")], out_fn=lambda r, e: (r + e,))
    dx, dss = _modulate_bwd(x, mod, 1, dh, dxa, f"{tag}_mod_b")
    return dx, (dw_main, dw_small, dconv_w, dprm, dnw, dw_out), (dss, dp)


def _cf_fwd(x, mod, lnp, wts, tag):
    w_in, dw_w, dw_b, cln, w_out = wts
    h = _modulate(x, mod, 0, 1, f"{tag}_mod")
    p = _mm(h, w_in, "nn", name=f"{tag}_in")
    u2 = _glu_conv(p, dw_w, dw_b, f"{tag}_conv")
    u3 = _ln_silu(u2, cln, f"{tag}_cln")
    y = _mm(u3, w_out, "nn", name=f"{tag}_out")
    x1 = _combine(x, y, mod, 2, lnp, 0, f"{tag}_ln")
    return x1, (x, h, p, u2, u3, y)


def _cf_bwd(dx1, saved, mod, lnp, wts, tag):
    w_in, dw_w, dw_b, cln, w_out = wts
    x, h, p, u2, u3, y = saved
    dxa, dy, dp = _combine_bwd(x, y, mod, 2, lnp, 0, dx1, f"{tag}_ln_b")
    du3 = _mm(dy, w_out, "nt", name=f"{tag}_out_bx")
    dw_out = _mm(u3, dy, "tn", name=f"{tag}_out_bw")
    du2, dcln = _ln_silu_bwd(u2, cln, du3, f"{tag}_cln_b")
    dval, dgate, ddw_w, ddw_b = _glu_conv_bwd(p, dw_w, du2, f"{tag}_conv_b")
    dpp = jnp.concatenate([dval, dgate], axis=1)
    dw_in = _mm(h, dpp, "tn", name=f"{tag}_in_bw")
    dh = _mm(dpp, w_in, "nt", name=f"{tag}_in_bx")
    dx, dss = _modulate_bwd(x, mod, 1, dh, dxa, f"{tag}_mod_b")
    return dx, (dw_in, ddw_w, ddw_b, dcln, dw_out), (dss, dp)


def _two_d(a):
    return a.reshape(-1, a.shape[-1])


def kernel(x, c, ada_w, ada_b, ln_g, ln_b, dn_w_in, dn_conv_w, dn_a_log, dn_dt_bias, dn_norm_w, dn_w_out, cf_w_in, cf_dw_w, cf_dw_b, cf_ln_g, cf_ln_b, cf_w_out, ff_w1, ff_w2, loss_target, m_ada_w, m_ada_b, m_ln_g, m_ln_b, m_dn_w_in, m_dn_conv_w, m_dn_a_log, m_dn_dt_bias, m_dn_norm_w, m_dn_w_out, m_cf_w_in, m_cf_dw_w, m_cf_dw_b, m_cf_ln_g, m_cf_ln_b, m_cf_w_out, m_ff_w1, m_ff_w2, v_ada_w, v_ada_b, v_ln_g, v_ln_b, v_dn_w_in, v_dn_conv_w, v_dn_a_log, v_dn_dt_bias, v_dn_norm_w, v_dn_w_out, v_cf_w_in, v_cf_dw_w, v_cf_dw_b, v_cf_ln_g, v_cf_ln_b, v_cf_w_out, v_ff_w1, v_ff_w2):
    ix, iy, ic = lax.axis_index("x"), lax.axis_index("y"), lax.axis_index("c")
    chip = 2 * ix + iy
    dev = 4 * ix + 2 * iy + ic
    S, D = x.shape[1], x.shape[2]
    L = ada_w.shape[0]
    LA, LB = dn_w_in.shape[0], cf_w_in.shape[0]
    H = dn_a_log.shape[1]
    NMOD = ada_b.shape[1] // D
    dn_in = dn_w_in.shape[2] * 4
    n_main = dn_in - 2 * H
    assert L == N_LAYERS and 2 * H <= LANES
    x0, tgt = x[0], loss_target[0]

    small_sharded = [ln_g, ln_b, dn_conv_w, cf_dw_w, cf_dw_b, cf_ln_g, cf_ln_b]
    small_axes = [2, 2, 2, 2, 1, 1, 1]
    packed_small = _pack(small_sharded, rows=8)[None]
    def own_block(w, axis):
        shp = list(w.shape)
        shp[axis] *= 4
        return lax.dynamic_update_slice_in_dim(jnp.zeros(shp, _MXU), w.astype(_MXU), chip * w.shape[axis], axis)

    g_dn_in, g_dn_out, g_cf_in, g_cf_out, g_w1, g_w2 = _gather_two_level(
        [(own_block(dn_w_in[:, None], 1), 1), (own_block(dn_w_out, 1), 1), (own_block(cf_w_in, 2), 2),
         (own_block(cf_w_out, 1), 1), (own_block(ff_w1, 2), 2), (own_block(ff_w2, 1), 1)], "gather_weights")
    g_small = _exchange([(packed_small, 0)], "xy", "gather", "gather_small")[0]
    w_dn_in = jnp.transpose(g_dn_in, (0, 2, 1, 3)).reshape(LA, D, dn_in)
    w_dn_main = w_dn_in[:, :, :n_main]
    w_dn_small = jnp.pad(w_dn_in[:, :, n_main:], ((0, 0), (0, 0), (0, LANES - 2 * H)))
    shard_shapes = [a.shape for a in small_sharded]
    per_chip = [_unpack(g_small[q], shard_shapes) for q in range(4)]
    ln_g_f, ln_b_f, conv_w_f, dw_w_f, dw_b_f, cln_g_f, cln_b_f = [
        jnp.concatenate([per_chip[q][i] for q in range(4)], axis=small_axes[i]) for i in range(len(small_sharded))]

    c_all = _exchange([(c[None], 0)], "all", "gather", "gather_cond")[0].reshape(8, D)
    c_pad = jnp.pad(c_all, ((0, 8), (0, 0)))
    mod_sh = jnp.stack([_mm(c_pad, (ada_w, l), "nn", name=f"ada_{l}", a_fn=lambda t: t * _sigmoid(t))
                        for l in range(L)])
    mod_all = _exchange([(mod_sh, 2)], "xy", "gather", "gather_mod")[0]
    mod_mine = lax.dynamic_index_in_dim(mod_all, dev, axis=1, keepdims=False) + ada_b
    mods = mod_mine.reshape(L, NMOD, D)

    def lnp_of(l):
        return jnp.stack([ln_g_f[l, 0], ln_b_f[l, 0], ln_g_f[l, 1], ln_b_f[l, 1]])

    def dn_wts(j):
        prm = jnp.zeros((2, LANES), F32).at[0, H:2 * H].set(dn_a_log[j]).at[1, H:2 * H].set(dn_dt_bias[j])
        return ((w_dn_main, j), (w_dn_small, j), conv_w_f[j], prm, dn_norm_w[j][None], (g_dn_out, j))

    def cf_wts(j):
        return ((g_cf_in, j), dw_w_f[j], dw_b_f[j][None], jnp.stack([cln_g_f[j], cln_b_f[j]]), (g_cf_out, j))

    xs = x0
    saved = []
    for l in range(L):
        j = l // 2
        if l % 2 == 0:
            xs, sv_a = _dn_fwd(xs, mods[l], lnp_of(l), dn_wts(j), H, f"l{l}_dn")
        else:
            xs, sv_a = _cf_fwd(xs, mods[l], lnp_of(l), cf_wts(j), f"l{l}_cf")
        xs, sv_b = _mlp_fwd(xs, mods[l], lnp_of(l), (g_w1, l), (g_w2, l), f"l{l}_ff")
        saved.append((sv_a, sv_b))
    dx, loss_local = _loss_head(xs, tgt, "loss_head")
    loss = lax.psum(loss_local[0, 0], ("x", "y", "c"))

    gw1, gw2 = [None] * L, [None] * L
    g_dn = [None] * LA
    g_cf = [None] * LB
    dmods, dlns = [None] * L, [None] * L
    for l in reversed(range(L)):
        j = l // 2
        sv_a, sv_b = saved[l]
        dx, gw1[l], gw2[l], (dss2, dp2) = _mlp_bwd(dx, sv_b, mods[l], lnp_of(l), (g_w1, l), (g_w2, l), f"l{l}_ff")
        if l % 2 == 0:
            dx, g_dn[j], (dss1, dp1) = _dn_bwd(dx, sv_a, mods[l], lnp_of(l), dn_wts(j), H, f"l{l}_dn")
        else:
            dx, g_cf[j], (dss1, dp1) = _cf_bwd(dx, sv_a, mods[l], lnp_of(l), cf_wts(j), f"l{l}_cf")
        dmods[l] = jnp.concatenate([dss1, dp1[0:1], dss2, dp2[0:1]], axis=0)
        dlns[l] = (jnp.stack([dp1[1], dp2[1]]), jnp.stack([dp1[2], dp2[2]]))
    grad_x = dx[None]

    d_ln_g = jnp.stack([dlns[l][0] for l in range(L)])
    d_ln_b = jnp.stack([dlns[l][1] for l in range(L)])
    d_conv_w = jnp.stack([g_dn[j][2] for j in range(LA)])
    d_a_log = jnp.stack([g_dn[j][3][0, H:2 * H] for j in range(LA)])
    d_dt_bias = jnp.stack([g_dn[j][3][1, H:2 * H] for j in range(LA)])
    d_norm_w = jnp.stack([g_dn[j][4][0] for j in range(LA)])
    d_dw_w = jnp.stack([g_cf[j][1] for j in range(LB)])
    d_dw_b = jnp.stack([g_cf[j][2][0] for j in range(LB)])
    d_cln_g = jnp.stack([g_cf[j][3][0] for j in range(LB)])
    d_cln_b = jnp.stack([g_cf[j][3][1] for j in range(LB)])
    d_mod = jnp.stack(dmods).reshape(L, NMOD * D)
    small_full = [d_mod, d_ln_g, d_ln_b, d_conv_w, d_dw_w, d_dw_b, d_cln_g, d_cln_b, d_a_log, d_dt_bias, d_norm_w]
    small_all = _exchange([(_pack(small_full, rows=8)[None], 0)], "all", "gather", "gather_small_grads")[0]
    small_sum = _sum_slots(small_all, "sum_small_grads")
    (s_ada_b, s_ln_g, s_ln_b, s_conv_w, s_dw_w, s_dw_b, s_cln_g, s_cln_b, s_a_log, s_dt_bias, s_norm_w) = _unpack(
        small_sum, [a.shape for a in small_full])
    d_mod_all = small_all.reshape(8, -1)[:, :L * NMOD * D].reshape(8, L, NMOD * D)

    def shard(a, axis):
        size = a.shape[axis] // 4
        return lax.dynamic_slice_in_dim(a, chip * size, size, axis)

    ncol = ada_w.shape[2]
    d_mod_sh = jnp.pad(lax.dynamic_slice_in_dim(d_mod_all, chip * ncol, ncol, 2), ((0, 8), (0, 0), (0, 0)))
    g_ada_w = jnp.stack([_mm(c_pad, d_mod_sh[:, l], "tn", name=f"ada_bw_{l}", a_fn=lambda t: t * _sigmoid(t))
                         for l in range(L)])

    dn_in_full = jnp.stack([jnp.concatenate([g_dn[j][0], g_dn[j][1][:, :2 * H]], axis=1) for j in range(LA)])
    dn_in_st = jnp.transpose(dn_in_full.reshape(LA, D, 4, dn_in // 4), (0, 2, 1, 3))
    big = [(dn_in_st, 1), (jnp.stack([g_dn[j][5] for j in range(LA)]), 1),
           (jnp.stack([g_cf[j][0] for j in range(LB)]), 2), (jnp.stack([g_cf[j][4] for j in range(LB)]), 1),
           (jnp.stack(gw1), 2), (jnp.stack(gw2), 1)]
    pair = _exchange([(g, 0) for g, _ in big], "c", "scatter", "pair_grads", nsplit=8, keep_own=False)
    pair_sums = [_sum_pair(g.reshape(2, -1, g.shape[-1]), r.reshape(2, -1, r.shape[-1]), ic, f"sum_pair_{i}",
                           _MXU).reshape(r.shape[1:])
                 for i, ((g, _), r) in enumerate(zip(big, pair))]
    recv = _exchange([(s, ax) for s, (_, ax) in zip(pair_sums, big)], "xy", "scatter", "scatter_grads")
    sums = [_sum_slots(r.reshape(4, -1, r.shape[-1]), f"sum_grads_{i}") for i, r in enumerate(recv)]
    other = _exchange([(s[None], 0) for s in sums], "c", "gather", "swap_sums", nsplit=4, keep_own=False)

    big_w = [(dn_w_in, m_dn_w_in, v_dn_w_in), (dn_w_out, m_dn_w_out, v_dn_w_out), (cf_w_in, m_cf_w_in, v_cf_w_in),
             (cf_w_out, m_cf_w_out, v_cf_w_out), (ff_w1, m_ff_w1, v_ff_w1), (ff_w2, m_ff_w2, v_ff_w2)]
    big_out = []
    for i, ((w, m, v), own, oth) in enumerate(zip(big_w, sums, other)):
        res = _adamw_halves(_two_d(w), _two_d(m), _two_d(v), own, oth, ic, f"adamw_big_{i}")
        big_out.append([r.reshape(w.shape) for r in res])
    ada_out = [r.reshape(ada_w.shape) for r in _adamw(_two_d(ada_w), _two_d(m_ada_w), _two_d(v_ada_w),
                                                     _two_d(g_ada_w)[None], "adamw_ada_w")]

    small_w = [(ada_b, m_ada_b, v_ada_b, s_ada_b), (ln_g, m_ln_g, v_ln_g, shard(s_ln_g, 2)),
               (ln_b, m_ln_b, v_ln_b, shard(s_ln_b, 2)), (dn_conv_w, m_dn_conv_w, v_dn_conv_w, shard(s_conv_w, 2)),
               (dn_a_log, m_dn_a_log, v_dn_a_log, s_a_log), (dn_dt_bias, m_dn_dt_bias, v_dn_dt_bias, s_dt_bias),
               (dn_norm_w, m_dn_norm_w, v_dn_norm_w, s_norm_w), (cf_dw_w, m_cf_dw_w, v_cf_dw_w, shard(s_dw_w, 2)),
               (cf_dw_b, m_cf_dw_b, v_cf_dw_b, shard(s_dw_b, 1)), (cf_ln_g, m_cf_ln_g, v_cf_ln_g, shard(s_cln_g, 1)),
               (cf_ln_b, m_cf_ln_b, v_cf_ln_b, shard(s_cln_b, 1))]
    pk = [_pack([t[i] for t in small_w], rows=8) for i in range(4)]
    small_res = _adamw(pk[0], pk[1], pk[2], pk[3][None], "adamw_small")
    small_shapes = [t[0].shape for t in small_w]
    small_out = [_unpack(r, small_shapes) for r in small_res]

    def kind(k):
        sm = small_out[k]
        bg = [o[k] for o in big_out]
        return [ada_out[k], sm[0], sm[1], sm[2], bg[0], sm[3], sm[4], sm[5], sm[6], bg[1],
                bg[2], sm[7], sm[8], sm[9], sm[10], bg[3], bg[4], bg[5]]

    return (loss, grad_x, *kind(0), *kind(1), *kind(2), *kind(3))
```

```python
import functools

import jax
import jax.numpy as jnp
from jax import lax
from jax.experimental import pallas as pl
from jax.experimental.pallas import tpu as pltpu

F32 = jnp.float32
_MXU = jnp.bfloat16
_HI = lax.Precision.HIGHEST

N_LAYERS = 4
ALPHA = (2.0 * N_LAYERS) ** 0.25
LN_EPS = 1e-5
RMS_EPS = 1e-6
L2_EPS = 1e-6
CHUNK = 64
ADAM_LR, ADAM_B1, ADAM_B2, ADAM_EPS, ADAM_WD, ADAM_STEP = 0.001, 0.9, 0.999, 1e-08, 0.01, 10

LANES = 128
TOKEN_BLOCK = 256
VMEM_BIG = 48 * 1024 * 1024
MM_TILE = 1024

SDS = jax.ShapeDtypeStruct
MESH = pl.DeviceIdType.MESH


def _cparams(vmem=None):
    if vmem is None:
        return None
    return pltpu.CompilerParams(vmem_limit_bytes=vmem)


def _pcall(body, **kw):
    if kw.get("compiler_params", 1) is None:
        kw.pop("compiler_params")
    return pl.pallas_call(body, **kw)


def _full(arr):
    nd = arr.ndim
    return pl.BlockSpec(arr.shape, lambda *g: (0,) * nd)


def _bs(block, imap, lead=None):
    if lead is None:
        return pl.BlockSpec(block, imap)
    return pl.BlockSpec((None,) + tuple(block), lambda *g: (lead,) + tuple(imap(*g)))


def _split(a):
    return a if isinstance(a, tuple) else (a, None)


_GROUPS = {
    "xy": ([(1, 0, 0), (0, 1, 0), (1, 1, 0)], 4),
    "c": ([(0, 0, 1)], 2),
    "all": ([(1, 0, 0), (0, 1, 0), (1, 1, 0), (0, 0, 1), (1, 0, 1), (0, 1, 1), (1, 1, 1)], 8),
}


def _exchange(items, group, mode, name, nsplit=1, keep_own=True):
    masks, n = _GROUPS[group]
    npeer = len(masks)
    ni = len(items)
    arrs = [a for a, _ in items]
    out_shapes = []
    for a, ax in items:
        shp = list(a.shape)
        if mode == "gather":
            shp[ax] *= n
        else:
            shp[ax] //= n
            shp = [n] + shp
        out_shapes.append(SDS(tuple(shp), a.dtype))

    def body(*refs):
        ins, outs = refs[:ni], refs[ni:2 * ni]
        send_sems, recv_sems, local_sems = refs[2 * ni:]
        x, y, c = lax.axis_index("x"), lax.axis_index("y"), lax.axis_index("c")

        def slot(px, py, pc):
            if group == "xy":
                return 2 * px + py
            if group == "c":
                return pc
            return 4 * px + 2 * py + pc

        me = slot(x, y, c)

        def block(ref, ax, idx, size):
            ix = (slice(None),) * ax + (pl.ds(pl.multiple_of(idx * size, size), size),)
            return ref.at[ix]

        copies = []
        for it, (a, ax) in enumerate(items):
            in_ref, out_ref = ins[it], outs[it]
            if mode == "gather":
                size = a.shape[ax]
                src_own, dst_own = in_ref, block(out_ref, ax, me, size)
            else:
                size = a.shape[ax] // n
                src_own, dst_own = block(in_ref, ax, me, size), out_ref.at[me]
            sax, ns, cs = splits[it]
            pieces = [(slice(None),) * sax + (pl.ds(j * cs, cs),) for j in range(ns)]
            if keep_own:
                for j, piece in enumerate(pieces):
                    own = pltpu.make_async_copy(src_own.at[piece], dst_own.at[piece], local_sems.at[it * nsplit + j])
                    own.start()
                    copies.append(own)
            for k, m in enumerate(masks):
                peer = tuple((1 - v) if b else v for v, b in zip((x, y, c), m))
                if mode == "gather":
                    src, dst = in_ref, dst_own
                else:
                    src, dst = block(in_ref, ax, slot(*peer), size), out_ref.at[me]
                for j, piece in enumerate(pieces):
                    sem = (it * npeer + k) * nsplit + j
                    cp = pltpu.make_async_remote_copy(
                        src_ref=src.at[piece], dst_ref=dst.at[piece], send_sem=send_sems.at[sem],
                        recv_sem=recv_sems.at[sem], device_id=peer, device_id_type=MESH)
                    cp.start()
                    copies.append(cp)
        for cp in copies:
            cp.wait()

    splits = []
    for (a, ax), o in zip(items, out_shapes):
        bshape = a.shape if mode == "gather" else o.shape[1:]
        sax = max(range(len(bshape) - 1), key=lambda d: bshape[d])
        ns = nsplit if bshape[sax] % (nsplit * 16) == 0 else 1
        splits.append((sax, ns, bshape[sax] // ns))
    any_spec = pl.BlockSpec(memory_space=pl.ANY)
    nsem = ni * npeer * nsplit
    outs = _pcall(
        body, name=name, out_shape=tuple(out_shapes),
        in_specs=[any_spec] * ni, out_specs=tuple([any_spec] * ni),
        scratch_shapes=[pltpu.SemaphoreType.DMA((nsem,)), pltpu.SemaphoreType.DMA((nsem,)),
                        pltpu.SemaphoreType.DMA((ni * nsplit,))],
    )(*arrs)
    return list(outs)


def _gather_two_level(items, name):
    masks = _GROUPS["xy"][0]
    ni, npeer = len(items), len(masks)

    def body(*refs):
        outs = refs[ni:2 * ni]
        a_send, a_recv, b_send, b_recv = refs[2 * ni:]
        x, y, c = lax.axis_index("x"), lax.axis_index("y"), lax.axis_index("c")
        sibling = (x, y, 1 - c)
        peers = [tuple((1 - v) if b else v for v, b in zip((x, y), m)) for m in masks]

        def blk(it, layer_half, chip):
            full, ax = items[it]
            half, size = full.shape[0] // 2, full.shape[ax] // 4
            ix = [slice(None)] * (ax + 1)
            ix[0] = pl.ds(layer_half * half, half)
            ix[ax] = pl.ds(pl.multiple_of(chip * size, size), size)
            return outs[it].at[tuple(ix)]

        def copy(src_dst, sems, k, to):
            return pltpu.make_async_remote_copy(src_ref=src_dst, dst_ref=src_dst, send_sem=sems[0].at[k],
                                                recv_sem=sems[1].at[k], device_id=to, device_id_type=MESH)

        first, passed = [], []
        for it in range(ni):
            for k, (px, py) in enumerate(peers):
                cp = copy(blk(it, c, 2 * x + y), (a_send, a_recv), it * npeer + k, (px, py, c))
                cp.start()
                first.append(cp)
        for it in range(ni):
            for k, (px, py) in enumerate(peers):
                landed = blk(it, c, 2 * px + py)
                copy(landed, (a_send, a_recv), it * npeer + k, (px, py, c)).wait_recv()
                cp = copy(landed, (b_send, b_recv), it * npeer + k, sibling)
                cp.start()
                passed.append(cp)
        for it in range(ni):
            for k, (px, py) in enumerate(peers):
                copy(blk(it, 1 - c, 2 * px + py), (b_send, b_recv), it * npeer + k, sibling).wait_recv()
        for cp in first + passed:
            cp.wait_send()

    any_spec = pl.BlockSpec(memory_space=pl.ANY)
    arrs = [a for a, _ in items]
    outs = _pcall(
        body, name=name, out_shape=tuple(SDS(a.shape, a.dtype) for a in arrs),
        in_specs=[any_spec] * ni, out_specs=tuple([any_spec] * ni),
        input_output_aliases={i: i for i in range(ni)},
        scratch_shapes=[pltpu.SemaphoreType.DMA((ni * npeer,)) for _ in range(4)],
    )(*arrs)
    return list(outs)


def _tile(n, cap):
    if n <= cap:
        return n
    t = cap - cap % LANES
    while n % t:
        t -= LANES
    return t


def _mm(a, b, mode, *, name, out_dtypes=(F32,), tm=MM_TILE, tn=MM_TILE, tk=MM_TILE, a_fn=None, out_fn=None, aux=(),
        stack=None):
    (a, a_lead), (b, b_lead) = _split(a), _split(b)
    ash, bsh = a.shape[-2:], b.shape[-2:]
    if mode == "nn":
        (M, K), (_, N) = ash, bsh
    elif mode == "nt":
        (M, K), (N, _) = ash, bsh
    else:
        (K, M), (_, N) = ash, bsh
    tm, tn, tk = _tile(M, tm), _tile(N, tn), _tile(K, tk)
    nk = K // tk
    if mode == "tn":
        a_spec = _bs((tk, tm), lambda i, j, k: (k, i), a_lead)
    else:
        a_spec = _bs((tm, tk), lambda i, j, k: (i, k), a_lead)
    if mode == "nt":
        b_spec = _bs((tn, tk), lambda i, j, k: (j, k), b_lead)
    else:
        b_spec = _bs((tk, tn), lambda i, j, k: (k, j), b_lead)
    aux_arrs, aux_specs = [], []
    for arr, kind in aux:
        arr, lead = _split(arr)
        aux_arrs.append(arr)
        if kind == "mn":
            aux_specs.append(_bs((tm, tn), lambda i, j, k: (i, j), lead))
        else:
            aux_specs.append(_bs((1, tn), lambda i, j, k: (0, j), lead))
    na, no = len(aux_arrs), len(out_dtypes)
    dims = {"nn": (((1,), (0,)), ((), ())), "nt": (((1,), (1,)), ((), ())), "tn": (((0,), (0,)), ((), ()))}[mode]

    def finish(r, aux_refs, o_refs):
        outs = out_fn(r, *[x[...] for x in aux_refs]) if out_fn is not None else (r,)
        for o_ref, val in zip(o_refs, outs):
            o_ref[...] = val.astype(o_ref.dtype)

    def product(a_ref, b_ref):
        av = a_ref[...]
        if a_fn is not None:
            av = a_fn(av.astype(F32))
        return lax.dot_general(av.astype(_MXU), b_ref[...].astype(_MXU), dims, preferred_element_type=F32)

    nbuf = 0 if stack is None or stack[0] is None else 1

    def body_one(a_ref, b_ref, *rest):
        finish(product(a_ref, b_ref), rest[:na], rest[na + nbuf:na + nbuf + no])

    def body_acc(a_ref, b_ref, *rest):
        aux_refs, o_refs, acc = rest[:na], rest[na + nbuf:na + nbuf + no], rest[na + nbuf + no]
        k = pl.program_id(2)

        @pl.when(k == 0)
        def _():
            acc[...] = product(a_ref, b_ref)

        @pl.when(k != 0)
        def _():
            acc[...] += product(a_ref, b_ref)

        @pl.when(k == nk - 1)
        def _():
            finish(acc[...], aux_refs, o_refs)

    extra, aliases = {}, []
    if stack is None:
        out_shape = (M, N)
        o_spec = pl.BlockSpec((tm, tn), lambda i, j, k: (i, j))
    else:
        buf, layer, n_layers = stack
        assert no == 1
        out_shape = (n_layers, M, N)
        o_spec = pl.BlockSpec((None, tm, tn), lambda i, j, k: (layer, i, j))
        if buf is not None:
            aliases = [buf]
            extra = dict(input_output_aliases={2 + na: 0})
    outs = _pcall(
        body_one if nk == 1 else body_acc, name=name, grid=(M // tm, N // tn, nk),
        in_specs=[a_spec, b_spec] + aux_specs + [pl.BlockSpec(memory_space=pl.ANY)] * nbuf,
        out_specs=tuple([o_spec] * no),
        out_shape=tuple(SDS(out_shape, dt) for dt in out_dtypes),
        scratch_shapes=[] if nk == 1 else [pltpu.VMEM((tm, tn), F32)],
        compiler_params=pltpu.CompilerParams(dimension_semantics=("parallel", "parallel", "arbitrary"),
                                             vmem_limit_bytes=VMEM_BIG),
        **extra,
    )(a, b, *aux_arrs, *aliases)
    return outs[0] if no == 1 else outs


def _tok(S):
    ts = min(TOKEN_BLOCK, S)
    assert S % ts == 0
    return ts


def _row(ts, D):
    return pl.BlockSpec((ts, D), lambda i: (i, 0))


def _acc_rows(ref, i, rows):
    @pl.when(i == 0)
    def _():
        for r, v in enumerate(rows):
            ref[r:r + 1, :] = v

    @pl.when(i != 0)
    def _():
        for r, v in enumerate(rows):
            ref[r:r + 1, :] += v


def _modulate(x, mod, r_sh, r_sc, name):
    S, D = x.shape
    ts = _tok(S)

    def body(x_ref, m_ref, o_ref):
        o_ref[...] = (x_ref[...] * (1.0 + m_ref[r_sc:r_sc + 1, :]) + m_ref[r_sh:r_sh + 1, :]).astype(o_ref.dtype)

    return _pcall(body, name=name, grid=(S // ts,), in_specs=[_row(ts, D), _full(mod)],
                  out_specs=_row(ts, D), out_shape=SDS((S, D), _MXU))(x, mod)


def _modulate_bwd(x, mod, r_sc, dh, dxa, name):
    S, D = x.shape
    ts = _tok(S)

    def body(x_ref, m_ref, dh_ref, dxa_ref, dx_ref, dss_ref):
        dh_v = dh_ref[...]
        dx_ref[...] = dxa_ref[...] + dh_v * (1.0 + m_ref[r_sc:r_sc + 1, :])
        _acc_rows(dss_ref, pl.program_id(0),
                  [jnp.sum(dh_v, axis=0, keepdims=True), jnp.sum(dh_v * x_ref[...], axis=0, keepdims=True)])

    return _pcall(body, name=name, grid=(S // ts,),
                  in_specs=[_row(ts, D), _full(mod), _row(ts, D), _row(ts, D)],
                  out_specs=(_row(ts, D), pl.BlockSpec((2, D), lambda i: (0, 0))),
                  out_shape=(SDS((S, D), F32), SDS((2, D), F32)))(x, mod, dh, dxa)


def _norm_stats(z):
    mu = jnp.mean(z, axis=-1, keepdims=True)
    zc = z - mu
    var = jnp.mean(zc * zc, axis=-1, keepdims=True)
    rstd = lax.rsqrt(var + LN_EPS)
    return zc * rstd, rstd


def _norm_bwd(dxhat, xhat, rstd):
    return rstd * (dxhat - jnp.mean(dxhat, axis=-1, keepdims=True)
                   - xhat * jnp.mean(dxhat * xhat, axis=-1, keepdims=True))


def _combine(x, y, mod, r_gt, lnp, r_g, name):
    S, D = x.shape
    ts = _tok(S)

    def body(x_ref, y_ref, m_ref, l_ref, o_ref):
        z = ALPHA * x_ref[...] + (1.0 + m_ref[r_gt:r_gt + 1, :]) * y_ref[...]
        xhat, _ = _norm_stats(z)
        o_ref[...] = xhat * l_ref[r_g:r_g + 1, :] + l_ref[r_g + 1:r_g + 2, :]

    return _pcall(body, name=name, grid=(S // ts,), in_specs=[_row(ts, D), _row(ts, D), _full(mod), _full(lnp)],
                  out_specs=_row(ts, D), out_shape=SDS((S, D), F32))(x, y, mod, lnp)


def _combine_bwd(x, y, mod, r_gt, lnp, r_g, dout, name):
    S, D = x.shape
    ts = _tok(S)

    def body(x_ref, y_ref, m_ref, l_ref, do_ref, dxa_ref, dy_ref, dp_ref):
        gate = 1.0 + m_ref[r_gt:r_gt + 1, :]
        y_v, do_v = y_ref[...], do_ref[...]
        xhat, rstd = _norm_stats(ALPHA * x_ref[...] + gate * y_v)
        dz = _norm_bwd(do_v * l_ref[r_g:r_g + 1, :], xhat, rstd)
        dxa_ref[...] = ALPHA * dz
        dy_ref[...] = (gate * dz).astype(dy_ref.dtype)
        _acc_rows(dp_ref, pl.program_id(0),
                  [jnp.sum(dz * y_v, axis=0, keepdims=True), jnp.sum(do_v * xhat, axis=0, keepdims=True),
                   jnp.sum(do_v, axis=0, keepdims=True)])

    return _pcall(body, name=name, grid=(S // ts,),
                  in_specs=[_row(ts, D), _row(ts, D), _full(mod), _full(lnp), _row(ts, D)],
                  out_specs=(_row(ts, D), _row(ts, D), pl.BlockSpec((3, D), lambda i: (0, 0))),
                  out_shape=(SDS((S, D), F32), SDS((S, D), _MXU), SDS((3, D), F32)))(x, y, mod, lnp, dout)


def _sigmoid(t):
    return 1.0 / (1.0 + jnp.exp(-t))


def _ln_silu(u, lnp, name):
    S, D = u.shape
    ts = _tok(S)

    def body(u_ref, l_ref, o_ref):
        xhat, _ = _norm_stats(u_ref[...])
        t = xhat * l_ref[0:1, :] + l_ref[1:2, :]
        o_ref[...] = (t * _sigmoid(t)).astype(o_ref.dtype)

    return _pcall(body, name=name, grid=(S // ts,), in_specs=[_row(ts, D), _full(lnp)],
                  out_specs=_row(ts, D), out_shape=SDS((S, D), _MXU))(u, lnp)


def _ln_silu_bwd(u, lnp, dout, name):
    S, D = u.shape
    ts = _tok(S)

    def body(u_ref, l_ref, do_ref, du_ref, dp_ref):
        xhat, rstd = _norm_stats(u_ref[...])
        g = l_ref[0:1, :]
        t = xhat * g + l_ref[1:2, :]
        sg = _sigmoid(t)
        dt = do_ref[...] * (sg * (1.0 + t * (1.0 - sg)))
        du_ref[...] = _norm_bwd(dt * g, xhat, rstd)
        _acc_rows(dp_ref, pl.program_id(0),
                  [jnp.sum(dt * xhat, axis=0, keepdims=True), jnp.sum(dt, axis=0, keepdims=True)])

    return _pcall(body, name=name, grid=(S // ts,), in_specs=[_row(ts, D), _full(lnp), _row(ts, D)],
                  out_specs=(_row(ts, D), pl.BlockSpec((2, D), lambda i: (0, 0))),
                  out_shape=(SDS((S, D), F32), SDS((2, D), F32)))(u, lnp, dout)


def _loss_head(xf, tgt, name):
    S, D = xf.shape
    ts = _tok(S)

    def body(x_ref, t_ref, dx_ref, l_ref):
        err = x_ref[...] - t_ref[...]
        dx_ref[...] = err * (1.0 / D)
        part = jnp.sum(jnp.sum(err * err, axis=1, keepdims=True), axis=0, keepdims=True) * (0.5 / D)

        @pl.when(pl.program_id(0) == 0)
        def _():
            l_ref[...] = part

        @pl.when(pl.program_id(0) != 0)
        def _():
            l_ref[...] += part

    return _pcall(body, name=name, grid=(S // ts,), in_specs=[_row(ts, D), _row(ts, D)],
                  out_specs=(_row(ts, D), pl.BlockSpec((1, 1), lambda i: (0, 0))),
                  out_shape=(SDS((S, D), F32), SDS((1, 1), F32)))(xf, tgt)


def _shift_down(u, s, rows):
    if s == 0:
        return u
    return jnp.where(rows >= s, pltpu.roll(u, s, 0), 0.0)


def _shift_up(u, s, rows):
    if s == 0:
        return u
    n = u.shape[0]
    return jnp.where(rows < n - s, pltpu.roll(u, n - s, 0), 0.0)


def _dwconv(u, w_ref, taps, rows):
    acc = jnp.zeros_like(u)
    for j in range(taps):
        acc = acc + w_ref[j:j + 1, :] * _shift_down(u, taps - 1 - j, rows)
    return acc


def _dwconv_bwd(u, dy, w_ref, dw_ref, taps, rows):
    du = jnp.zeros_like(u)
    for j in range(taps):
        s = taps - 1 - j
        du = du + w_ref[j:j + 1, :] * _shift_up(dy, s, rows)
        dw_ref[j:j + 1, :] = jnp.sum(dy * _shift_down(u, s, rows), axis=0, keepdims=True)
    return du


def _col(S, j0=0):
    return pl.BlockSpec((S, LANES), lambda j: (0, j + j0))


def _conv_silu(pm, w, nblk, name):
    S = pm.shape[0]
    taps = w.shape[0]

    def body(u_ref, w_ref, o_ref):
        rows = lax.broadcasted_iota(jnp.int32, (S, LANES), 0)
        cv = _dwconv(u_ref[...], w_ref, taps, rows)
        o_ref[...] = cv * _sigmoid(cv)

    return _pcall(body, name=name, grid=(nblk,),
                  in_specs=[_col(S), pl.BlockSpec((taps, LANES), lambda j: (0, j))],
                  out_specs=_col(S), out_shape=SDS((S, nblk * LANES), F32),
                  compiler_params=_cparams(VMEM_BIG))(pm, w)


def _conv_silu_bwd(pm, w, dout, dpm, j0, name):
    S = pm.shape[0]
    taps = w.shape[0]
    nblk = dout.shape[1] // LANES

    def body(u_ref, w_ref, do_ref, dpm_in, du_ref, dw_ref):
        del dpm_in
        rows = lax.broadcasted_iota(jnp.int32, (S, LANES), 0)
        u = u_ref[...]
        cv = _dwconv(u, w_ref, taps, rows)
        sg = _sigmoid(cv)
        dc = do_ref[...] * (sg * (1.0 + cv * (1.0 - sg)))
        du_ref[...] = _dwconv_bwd(u, dc, w_ref, dw_ref, taps, rows)

    return _pcall(body, name=name, grid=(nblk,),
                  in_specs=[_col(S, j0), pl.BlockSpec((taps, LANES), lambda j: (0, j + j0)), _col(S),
                            pl.BlockSpec(memory_space=pl.ANY)],
                  out_specs=(_col(S, j0), pl.BlockSpec((taps, LANES), lambda j: (0, j))),
                  out_shape=(SDS(dpm.shape, F32), SDS((taps, nblk * LANES), F32)),
                  input_output_aliases={3: 0},
                  compiler_params=_cparams(VMEM_BIG))(pm, w, dout, dpm)


def _glu_conv(p, w, bias, name):
    S, C2 = p.shape
    nblk = C2 // 2 // LANES
    taps = w.shape[0]

    def body(v_ref, g_ref, w_ref, b_ref, o_ref):
        rows = lax.broadcasted_iota(jnp.int32, (S, LANES), 0)
        u = v_ref[...] * _sigmoid(g_ref[...])
        o_ref[...] = _dwconv(u, w_ref, taps, rows) + b_ref[...]

    return _pcall(body, name=name, grid=(nblk,),
                  in_specs=[_col(S), _col(S, nblk), pl.BlockSpec((taps, LANES), lambda j: (0, j)),
                            pl.BlockSpec((1, LANES), lambda j: (0, j))],
                  out_specs=_col(S), out_shape=SDS((S, nblk * LANES), F32),
                  compiler_params=_cparams(VMEM_BIG))(p, p, w, bias)


def _glu_conv_bwd(p, w, dout, name):
    S, C2 = p.shape
    nblk = C2 // 2 // LANES
    taps = w.shape[0]

    def body(v_ref, g_ref, w_ref, do_ref, dv_ref, dg_ref, dw_ref, db_ref):
        rows = lax.broadcasted_iota(jnp.int32, (S, LANES), 0)
        val, sg = v_ref[...], _sigmoid(g_ref[...])
        do_v = do_ref[...]
        du = _dwconv_bwd(val * sg, do_v, w_ref, dw_ref, taps, rows)
        dv_ref[...] = du * sg
        dg_ref[...] = du * val * sg * (1.0 - sg)
        db_ref[...] = jnp.sum(do_v, axis=0, keepdims=True)

    dval, dgate, dw, db = _pcall(
        body, name=name, grid=(nblk,),
        in_specs=[_col(S), _col(S, nblk), pl.BlockSpec((taps, LANES), lambda j: (0, j)), _col(S)],
        out_specs=(_col(S), _col(S), pl.BlockSpec((taps, LANES), lambda j: (0, j)),
                   pl.BlockSpec((1, LANES), lambda j: (0, j))),
        out_shape=(SDS((S, C2 // 2), F32), SDS((S, C2 // 2), F32), SDS((taps, C2 // 2), F32), SDS((1, C2 // 2), F32)),
        compiler_params=_cparams(VMEM_BIG))(p, p, w, dout)
    return dval, dgate, dw, db


def _log1p(e):
    u = 1.0 + e
    d = jnp.where(u == 1.0, 1.0, u - 1.0)
    return jnp.where(u == 1.0, e, jnp.log(u) * (e / d))


def _gate_parts(ps, prm, H):
    lane = lax.broadcasted_iota(jnp.int32, ps.shape, 1)
    is_b, is_g = lane < H, (lane >= H) & (lane < 2 * H)
    beta = _sigmoid(ps)
    t = ps + prm[1:2, :]
    sp = jnp.maximum(t, 0.0) + _log1p(jnp.exp(-jnp.abs(t)))
    na = -jnp.exp(prm[0:1, :])
    return is_b, is_g, beta, t, sp, na


def _gates(ps, prm, H, name):
    S = ps.shape[0]
    ts = _tok(S)

    def body(p_ref, r_ref, o_ref):
        is_b, is_g, beta, _, sp, na = _gate_parts(p_ref[...], r_ref[...], H)
        o_ref[...] = jnp.where(is_b, beta, jnp.where(is_g, na * sp, 0.0))

    return _pcall(body, name=name, grid=(S // ts,), in_specs=[_row(ts, LANES), _full(prm)],
                  out_specs=_row(ts, LANES), out_shape=SDS((S, LANES), F32))(ps, prm)


def _gates_bwd(ps, prm, dgates, H, name):
    S = ps.shape[0]
    ts = _tok(S)

    def body(p_ref, r_ref, dg_ref, dp_ref, dr_ref):
        is_b, is_g, beta, t, sp, na = _gate_parts(p_ref[...], r_ref[...], H)
        dg_v = dg_ref[...]
        dsp = jnp.where(is_g, dg_v * na * _sigmoid(t), 0.0)
        dp_ref[...] = jnp.where(is_b, dg_v * beta * (1.0 - beta), dsp)
        _acc_rows(dr_ref, pl.program_id(0),
                  [jnp.sum(jnp.where(is_g, dg_v * na * sp, 0.0), axis=0, keepdims=True),
                   jnp.sum(dsp, axis=0, keepdims=True)])

    return _pcall(body, name=name, grid=(S // ts,), in_specs=[_row(ts, LANES), _full(prm), _row(ts, LANES)],
                  out_specs=(_row(ts, LANES), pl.BlockSpec((2, LANES), lambda i: (0, 0))),
                  out_shape=(SDS((S, LANES), F32), SDS((2, LANES), F32)))(ps, prm, dgates)


_NN = (((2,), (1,)), ((0,), (0,)))
_NT = (((2,), (2,)), ((0,), (0,)))
_TN = (((1,), (1,)), ((0,), (0,)))


def _mdot(a, b, dims):
    return lax.dot_general(a.astype(_MXU), b.astype(_MXU), dims, preferred_element_type=F32)


def _mdot3(a, b, dims):
    ah, bh = a.astype(_MXU), b.astype(_MXU)
    al, bl = a - ah.astype(F32), b - bh.astype(F32)
    return _mdot(ah, bh, dims) + (_mdot(ah, bl, dims) + _mdot(al, bh, dims))


def _rounded_dot(dims, da_dims, db_dims, a_first, prod=_mdot):
    @jax.custom_vjp
    def f(a, b):
        return prod(a, b, dims)

    def fwd(a, b):
        return prod(a, b, dims), (a, b)

    def bwd(res, ct):
        a, b = res
        da = prod(ct, b, da_dims) if a_first[0] else prod(b, ct, da_dims)
        db = prod(ct, a, db_dims) if a_first[1] else prod(a, ct, db_dims)
        return da, db

    f.defvjp(fwd, bwd)
    return f


_mdot_nn = _rounded_dot(_NN, _NT, _TN, (True, False))
_mdot_nt = _rounded_dot(_NT, _NN, _TN, (True, True))
_mdot_tn = _rounded_dot(_TN, _NT, _NN, (False, False))
def _unit_lower_inverse(a):
    C = a.shape[-1]
    ri = lax.broadcasted_iota(jnp.int32, (1, C, C), 1)
    ci = lax.broadcasted_iota(jnp.int32, (1, C, C), 2)
    t_inv = jnp.where(ri == ci, 1.0, 0.0) - a
    p = a
    for _ in range(max(C.bit_length() - 2, 0)):
        p = _mdot3(p, p, _NN)
        t_inv = t_inv + _mdot3(t_inv, p, _NN)
    return t_inv


@jax.custom_vjp
def _known_inverse(a, t_inv):
    del a
    return t_inv


def _known_inverse_fwd(a, t_inv):
    del a
    return t_inv, t_inv


def _known_inverse_bwd(t_inv, ct):
    da = -_mdot3(_mdot3(t_inv, ct, _TN), t_inv, _NT)
    return da, jnp.zeros_like(t_inv)


_known_inverse.defvjp(_known_inverse_fwd, _known_inverse_bwd)


def _head_cols(gates, off, H):
    lane = lax.broadcasted_iota(jnp.int32, gates.shape, 1)
    cols = [jnp.sum(jnp.where(lane == off + h, gates, 0.0), axis=-1, keepdims=True) for h in range(H)]
    return jnp.concatenate([col[None] for col in cols], axis=0)


def _delta_chunk(qr, kr, v, z, gates, nw, s_in, t_known=None):
    H, C, dk = qr.shape
    beta, g = _head_cols(gates, 0, H), _head_cols(gates, H, H)
    q = qr * lax.rsqrt(jnp.sum(qr * qr, axis=-1, keepdims=True) + L2_EPS) * (dk ** -0.5)
    k = kr * lax.rsqrt(jnp.sum(kr * kr, axis=-1, keepdims=True) + L2_EPS)
    ri = lax.broadcasted_iota(jnp.int32, (1, C, C), 1)
    ci = lax.broadcasted_iota(jnp.int32, (1, C, C), 2)
    causal, strict, eye = ri >= ci, ri > ci, ri == ci
    gam_row = jnp.sum(jnp.where(ri <= ci, g, 0.0), axis=1, keepdims=True)
    gam_col = jnp.sum(jnp.where(eye, gam_row, 0.0), axis=-1, keepdims=True)
    g_last = jnp.sum(g, axis=1, keepdims=True)
    decay = jnp.where(causal, jnp.exp(jnp.where(causal, gam_col - gam_row, 0.0)), 0.0)
    kb = k * beta
    a = jnp.where(strict, _mdot_nt(kb, k) * decay, 0.0)
    t_inv = _unit_lower_inverse(a) if t_known is None else _known_inverse(a, t_known)
    eg = jnp.exp(gam_col)
    u = _mdot_nn(t_inv, v * beta)
    w = _mdot_nn(t_inv, kb * eg)
    a_qk = _mdot_nt(q, k) * decay
    v_new = u - _mdot_nn(w, s_in)
    o = _mdot_nn(q * eg, s_in) + _mdot_nn(a_qk, v_new)
    s_out = s_in * jnp.exp(g_last) + _mdot_tn(k * jnp.exp(g_last - gam_col), v_new)
    og = o * lax.rsqrt(jnp.mean(o * o, axis=-1, keepdims=True) + RMS_EPS) * nw * (z * _sigmoid(z))
    return og, s_out, t_inv


def _heads(ref, H, dk):
    return jnp.stack([ref[:, h * dk:(h + 1) * dk].astype(F32) for h in range(H)])


def _put_heads(ref, val, dk):
    for h in range(val.shape[0]):
        ref[:, h * dk:(h + 1) * dk] = val[h].astype(ref.dtype)


def _delta_fwd(qkv, pm, gates, nw, H, name):
    S = qkv.shape[0]
    hd = qkv.shape[1] // 3
    dk = hd // H
    N = S // CHUNK
    blk = lambda off: pl.BlockSpec((CHUNK, hd), lambda n: (n, off))

    def body(q_ref, k_ref, v_ref, z_ref, g_ref, nw_ref, og_ref, st_ref, ti_ref, s_scr):
        @pl.when(pl.program_id(0) == 0)
        def _():
            s_scr[...] = jnp.zeros_like(s_scr)

        s_in = s_scr[...]
        st_ref[...] = s_in
        og, s_out, t_inv = _delta_chunk(_heads(q_ref, H, dk), _heads(k_ref, H, dk), _heads(v_ref, H, dk),
                                        _heads(z_ref, H, dk), g_ref[...], nw_ref[...], s_in)
        _put_heads(og_ref, og, dk)
        ti_ref[...] = t_inv
        s_scr[...] = s_out

    return _pcall(
        body, name=name, grid=(N,),
        in_specs=[blk(0), blk(1), blk(2), blk(3), pl.BlockSpec((CHUNK, LANES), lambda n: (n, 0)), _full(nw)],
        out_specs=(blk(0), pl.BlockSpec((None, H, dk, dk), lambda n: (n, 0, 0, 0)),
                   pl.BlockSpec((None, H, CHUNK, CHUNK), lambda n: (n, 0, 0, 0))),
        out_shape=(SDS((S, hd), _MXU), SDS((N, H, dk, dk), F32), SDS((N, H, CHUNK, CHUNK), F32)),
        scratch_shapes=[pltpu.VMEM((H, dk, dk), F32)],
        compiler_params=_cparams(VMEM_BIG),
    )(qkv, qkv, qkv, pm, gates, nw)


def _delta_bwd(qkv, pm, gates, nw, states, t_invs, dog, H, name):
    S = qkv.shape[0]
    hd = qkv.shape[1] // 3
    dk = hd // H
    N = S // CHUNK
    blk = lambda off: pl.BlockSpec((CHUNK, hd), lambda n: (N - 1 - n, off))
    gspec = pl.BlockSpec((CHUNK, LANES), lambda n: (N - 1 - n, 0))

    def body(q_ref, k_ref, v_ref, z_ref, g_ref, nw_ref, st_ref, ti_ref, do_ref,
             dq_ref, dk_ref, dv_ref, dz_ref, dg_ref, dnw_ref, ds_scr):
        n = pl.program_id(0)

        @pl.when(n == 0)
        def _():
            ds_scr[...] = jnp.zeros_like(ds_scr)

        t_known = ti_ref[...]
        fn = functools.partial(_delta_chunk, t_known=t_known)
        _, vjp = jax.vjp(fn, _heads(q_ref, H, dk), _heads(k_ref, H, dk), _heads(v_ref, H, dk),
                         _heads(z_ref, H, dk), g_ref[...], nw_ref[...], st_ref[...])
        dq, dkk, dv, dz, dg, dnw, ds_in = vjp((_heads(do_ref, H, dk), ds_scr[...], jnp.zeros_like(t_known)))
        _put_heads(dq_ref, dq, dk)
        _put_heads(dk_ref, dkk, dk)
        _put_heads(dv_ref, dv, dk)
        _put_heads(dz_ref, dz, dk)
        ds_scr[...] = ds_in
        dg_ref[...] = dg

        @pl.when(n == 0)
        def _():
            dnw_ref[...] = dnw

        @pl.when(n != 0)
        def _():
            dnw_ref[...] += dnw

    return _pcall(
        body, name=name, grid=(N,),
        in_specs=[blk(0), blk(1), blk(2), blk(3), gspec, _full(nw),
                  pl.BlockSpec((None, H, dk, dk), lambda n: (N - 1 - n, 0, 0, 0)),
                  pl.BlockSpec((None, H, CHUNK, CHUNK), lambda n: (N - 1 - n, 0, 0, 0)), blk(0)],
        out_specs=(blk(0), blk(0), blk(0), blk(3), gspec, pl.BlockSpec((1, dk), lambda n: (0, 0))),
        out_shape=(SDS((S, hd), F32), SDS((S, hd), F32), SDS((S, hd), F32), SDS(pm.shape, F32),
                   SDS((S, LANES), F32), SDS((1, dk), F32)),
        scratch_shapes=[pltpu.VMEM((H, dk, dk), F32)],
        compiler_params=_cparams(VMEM_BIG),
    )(qkv, qkv, qkv, pm, gates, nw, states, t_invs, dog)


def _rows_block(R, C):
    rb = R
    while rb * C * 4 > (1 << 20) and rb % 16 == 0:
        rb //= 2
    return rb


def _sum_slots(st, name, out_dtype=F32):
    n, R, C = st.shape
    rb = _rows_block(R, C)

    def body(s_ref, o_ref):
        acc = s_ref[0].astype(F32)
        for q in range(1, n):
            acc = acc + s_ref[q].astype(F32)
        o_ref[...] = acc.astype(o_ref.dtype)

    return _pcall(body, name=name, grid=(R // rb,), in_specs=[pl.BlockSpec((n, rb, C), lambda i: (0, i, 0))],
                  out_specs=pl.BlockSpec((rb, C), lambda i: (i, 0)), out_shape=SDS((R, C), out_dtype))(st)


def _place_block(w, axis, chip, name):
    L, R, C = w.shape
    rb = _rows_block(R, C)
    nrb = R // rb
    shp = [L, R, C]
    shp[axis] *= 4
    if axis == 1:
        omap = lambda l, i, c: (l, c[0] * nrb + i, 0)
    else:
        omap = lambda l, i, c: (l, i, c[0])

    def body(c_ref, w_ref, o_ref):
        del c_ref
        o_ref[...] = w_ref[...].astype(o_ref.dtype)

    grid_spec = pltpu.PrefetchScalarGridSpec(
        num_scalar_prefetch=1, grid=(L, nrb),
        in_specs=[pl.BlockSpec((None, rb, C), lambda l, i, c: (l, i, 0))],
        out_specs=pl.BlockSpec((None, rb, C), omap))
    return _pcall(body, name=name, grid_spec=grid_spec, out_shape=SDS(tuple(shp), _MXU))(
        jnp.reshape(chip, (1,)).astype(jnp.int32), w)


def _sum_pair(own2, recv2, ic, name, out_dtype):
    _, R, C = own2.shape
    rb = _rows_block(R, C)

    def body(c_ref, a_ref, b_ref, o_ref):
        del c_ref
        o_ref[...] = (a_ref[...].astype(F32) + b_ref[...].astype(F32)).astype(o_ref.dtype)

    grid_spec = pltpu.PrefetchScalarGridSpec(
        num_scalar_prefetch=1, grid=(R // rb,),
        in_specs=[pl.BlockSpec((None, rb, C), lambda i, c: (c[0], i, 0)),
                  pl.BlockSpec((None, rb, C), lambda i, c: (1 - c[0], i, 0))],
        out_specs=pl.BlockSpec((rb, C), lambda i, c: (i, 0)))
    return _pcall(body, name=name, grid_spec=grid_spec, out_shape=SDS((R, C), out_dtype))(
        jnp.reshape(ic, (1,)).astype(jnp.int32), own2, recv2)


def _adamw_step(g, w_ref, m_ref, v_ref, g_ref, d_ref, mo_ref, vo_ref):
    m_new = ADAM_B1 * m_ref[...] + (1.0 - ADAM_B1) * g
    v_new = ADAM_B2 * v_ref[...] + (1.0 - ADAM_B2) * (g * g)
    m_hat = m_new / (1.0 - ADAM_B1 ** ADAM_STEP)
    v_hat = v_new / (1.0 - ADAM_B2 ** ADAM_STEP)
    g_ref[...] = g
    d_ref[...] = -ADAM_LR * (m_hat / (jnp.sqrt(v_hat) + ADAM_EPS) + ADAM_WD * w_ref[...])
    mo_ref[...] = m_new
    vo_ref[...] = v_new


def _adamw(w, m, v, st, name):
    R, C = w.shape
    n = st.shape[0]
    rb = _rows_block(R, C)
    spec = pl.BlockSpec((rb, C), lambda i: (i, 0))

    def body(w_ref, m_ref, v_ref, s_ref, *o_refs):
        g = s_ref[0]
        for q in range(1, n):
            g = g + s_ref[q]
        _adamw_step(g, w_ref, m_ref, v_ref, *o_refs)

    return _pcall(body, name=name, grid=(R // rb,),
                  in_specs=[spec, spec, spec, pl.BlockSpec((n, rb, C), lambda i: (0, i, 0))],
                  out_specs=(spec,) * 4, out_shape=(SDS((R, C), F32),) * 4)(w, m, v, st)


def _adamw_halves(w, m, v, own, recv2, ic, name):
    R, C = w.shape
    rb = _rows_block(R // 2, C)
    nb = R // 2 // rb
    spec = pl.BlockSpec((rb, C), lambda i, c: (i, 0))

    def body(c_ref, w_ref, m_ref, v_ref, own_ref, recv_ref, *o_refs):
        is_own = pl.program_id(0) // nb == c_ref[0]
        _adamw_step(jnp.where(is_own, own_ref[...], recv_ref[...]), w_ref, m_ref, v_ref, *o_refs)

    grid_spec = pltpu.PrefetchScalarGridSpec(
        num_scalar_prefetch=1, grid=(R // rb,),
        in_specs=[spec, spec, spec, pl.BlockSpec((rb, C), lambda i, c: (i % nb, 0)),
                  pl.BlockSpec((None, rb, C), lambda i, c: (1 - c[0], i % nb, 0))],
        out_specs=(spec,) * 4)
    return _pcall(body, name=name, grid_spec=grid_spec, out_shape=(SDS((R, C), F32),) * 4)(
        jnp.reshape(ic, (1,)).astype(jnp.int32), w, m, v, own, recv2)


def _pack(arrs, rows=1):
    flat = jnp.concatenate([a.reshape(-1).astype(F32) for a in arrs])
    quantum = rows * LANES
    pad = (-flat.shape[0]) % quantum
    flat = jnp.pad(flat, (0, pad))
    return flat.reshape(rows, -1)


def _unpack(flat, shapes):
    flat = flat.reshape(-1)
    out, off = [], 0
    for shp in shapes:
        size = 1
        for d in shp:
            size *= d
        out.append(flat[off:off + size].reshape(shp))
        off += size
    return out


def _mlp_fwd(x1, mod, lnp, w1, w2, tag):
    h2 = _modulate(x1, mod, 3, 4, f"{tag}_mod")
    a1, a2 = _mm(h2, w1, "nn", name=f"{tag}_up", out_dtypes=(_MXU, _MXU),
                 out_fn=lambda r: (r, jnp.square(jnp.maximum(r, 0.0))))
    y2 = _mm(a2, w2, "nn", name=f"{tag}_down")
    x2 = _combine(x1, y2, mod, 5, lnp, 2, f"{tag}_ln")
    return x2, (x1, h2, a1, a2, y2)


def _stacked(stacks, key, layer, n_layers, a, b, name):
    stacks[key] = _mm(a, b, "tn", name=name, stack=(stacks.get(key), layer, n_layers))


def _mlp_bwd(dx2, saved, mod, lnp, w1, w2, tag, stacks):
    x1, h2, a1, a2, y2 = saved
    dxa, dy2, dp = _combine_bwd(x1, y2, mod, 5, lnp, 2, dx2, f"{tag}_ln_b")
    da1 = _mm(dy2, w2, "nt", name=f"{tag}_down_bx", out_dtypes=(_MXU,), aux=[(a1, "mn")],
              out_fn=lambda r, a: (r * (2.0 * jnp.maximum(a.astype(F32), 0.0)),))
    layer, n_layers = w1[1], w1[0].shape[0]
    _stacked(stacks, "ff_w2", layer, n_layers, a2, dy2, f"{tag}_down_bw")
    _stacked(stacks, "ff_w1", layer, n_layers, h2, da1, f"{tag}_up_bw")
    dh2 = _mm(da1, w1, "nt", name=f"{tag}_up_bx")
    dx1, dss = _modulate_bwd(x1, mod, 4, dh2, dxa, f"{tag}_mod_b")
    return dx1, (dss, dp)


def _dn_fwd(x, mod, lnp, wts, H, tag):
    w_main, w_small, conv_w, prm, nw, w_out = wts
    h = _modulate(x, mod, 0, 1, f"{tag}_mod")
    pm = _mm(h, w_main, "nn", name=f"{tag}_in")
    ps = _mm(h, w_small, "nn", name=f"{tag}_in_s")
    nqkv = conv_w.shape[1] // LANES
    qkv = _conv_silu(pm, conv_w, nqkv, f"{tag}_conv")
    gates = _gates(ps, prm, H, f"{tag}_gates")
    og, *states = _delta_fwd(qkv, pm, gates, nw, H, f"{tag}_delta")
    y = _mm(og, w_out, "nn", name=f"{tag}_out")
    x1 = _combine(x, y, mod, 2, lnp, 0, f"{tag}_ln")
    return x1, (x, h, pm, ps, qkv, gates, states, og, y)


def _dn_bwd(dx1, saved, mod, lnp, wts, H, tag, stacks):
    w_main, w_small, conv_w, prm, nw, w_out = wts
    x, h, pm, ps, qkv, gates, states, og, y = saved
    dxa, dy, dp = _combine_bwd(x, y, mod, 2, lnp, 0, dx1, f"{tag}_ln_b")
    dog = _mm(dy, w_out, "nt", name=f"{tag}_out_bx")
    _stacked(stacks, "dn_w_out", w_out[1], w_out[0].shape[0], og, dy, f"{tag}_out_bw")
    dq, dk, dv, dpm, dgates, dnw = _delta_bwd(qkv, pm, gates, nw, *states, dog, H, f"{tag}_delta_b")
    dps, dprm = _gates_bwd(ps, prm, dgates, H, f"{tag}_gates_b")
    dcw = []
    nb = dq.shape[1] // LANES
    for part, dpart in enumerate((dq, dk, dv)):
        dpm, dcw_p = _conv_silu_bwd(pm, conv_w, dpart, dpm, part * nb, f"{tag}_conv_b{part}")
        dcw.append(dcw_p)
    dconv_w = jnp.concatenate(dcw, axis=1)
    dw_main = _mm(h, dpm, "tn", name=f"{tag}_in_bw")
    dw_small = _mm(h, dps, "tn", name=f"{tag}_in_s_bw")
    dh_s = _mm(dps, w_small, "nt", name=f"{tag}_in_s_bx")
    dh = _mm(dpm, w_main, "nt", name=f"{tag}_in_bx", aux=[(dh_s, "mn")], out_fn=lambda r, e: (r + e,))
    dx, dss = _modulate_bwd(x, mod, 1, dh, dxa, f"{tag}_mod_b")
    return dx, (dw_main, dw_small, dconv_w, dprm, dnw), (dss, dp)


def _cf_fwd(x, mod, lnp, wts, tag):
    w_in, dw_w, dw_b, cln, w_out = wts
    h = _modulate(x, mod, 0, 1, f"{tag}_mod")
    p = _mm(h, w_in, "nn", name=f"{tag}_in")
    u2 = _glu_conv(p, dw_w, dw_b, f"{tag}_conv")
    u3 = _ln_silu(u2, cln, f"{tag}_cln")
    y = _mm(u3, w_out, "nn", name=f"{tag}_out")
    x1 = _combine(x, y, mod, 2, lnp, 0, f"{tag}_ln")
    return x1, (x, h, p, u2, u3, y)


def _cf_bwd(dx1, saved, mod, lnp, wts, tag, stacks):
    w_in, dw_w, dw_b, cln, w_out = wts
    x, h, p, u2, u3, y = saved
    layer, n_layers = w_in[1], w_in[0].shape[0]
    dxa, dy, dp = _combine_bwd(x, y, mod, 2, lnp, 0, dx1, f"{tag}_ln_b")
    du3 = _mm(dy, w_out, "nt", name=f"{tag}_out_bx")
    _stacked(stacks, "cf_w_out", layer, n_layers, u3, dy, f"{tag}_out_bw")
    du2, dcln = _ln_silu_bwd(u2, cln, du3, f"{tag}_cln_b")
    dval, dgate, ddw_w, ddw_b = _glu_conv_bwd(p, dw_w, du2, f"{tag}_conv_b")
    dpp = jnp.concatenate([dval, dgate], axis=1)
    _stacked(stacks, "cf_w_in", layer, n_layers, h, dpp, f"{tag}_in_bw")
    dh = _mm(dpp, w_in, "nt", name=f"{tag}_in_bx")
    dx, dss = _modulate_bwd(x, mod, 1, dh, dxa, f"{tag}_mod_b")
    return dx, (ddw_w, ddw_b, dcln), (dss, dp)


def _two_d(a):
    return a.reshape(-1, a.shape[-1])


def kernel(x, c, ada_w, ada_b, ln_g, ln_b, dn_w_in, dn_conv_w, dn_a_log, dn_dt_bias, dn_norm_w, dn_w_out, cf_w_in, cf_dw_w, cf_dw_b, cf_ln_g, cf_ln_b, cf_w_out, ff_w1, ff_w2, loss_target, m_ada_w, m_ada_b, m_ln_g, m_ln_b, m_dn_w_in, m_dn_conv_w, m_dn_a_log, m_dn_dt_bias, m_dn_norm_w, m_dn_w_out, m_cf_w_in, m_cf_dw_w, m_cf_dw_b, m_cf_ln_g, m_cf_ln_b, m_cf_w_out, m_ff_w1, m_ff_w2, v_ada_w, v_ada_b, v_ln_g, v_ln_b, v_dn_w_in, v_dn_conv_w, v_dn_a_log, v_dn_dt_bias, v_dn_norm_w, v_dn_w_out, v_cf_w_in, v_cf_dw_w, v_cf_dw_b, v_cf_ln_g, v_cf_ln_b, v_cf_w_out, v_ff_w1, v_ff_w2):
    ix, iy, ic = lax.axis_index("x"), lax.axis_index("y"), lax.axis_index("c")
    chip = 2 * ix + iy
    dev = 4 * ix + 2 * iy + ic
    S, D = x.shape[1], x.shape[2]
    L = ada_w.shape[0]
    LA, LB = dn_w_in.shape[0], cf_w_in.shape[0]
    H = dn_a_log.shape[1]
    NMOD = ada_b.shape[1] // D
    dn_in = dn_w_in.shape[2] * 4
    n_main = dn_in - 2 * H
    assert L == N_LAYERS and 2 * H <= LANES
    x0, tgt = x[0], loss_target[0]

    small_sharded = [ln_g, ln_b, dn_conv_w, cf_dw_w, cf_dw_b, cf_ln_g, cf_ln_b]
    small_axes = [2, 2, 2, 2, 1, 1, 1]
    packed_small = _pack(small_sharded, rows=8)[None]
    big_names = ["dn_w_in", "dn_w_out", "cf_w_in", "cf_w_out", "ff_w1", "ff_w2"]
    big_axes = [1, 1, 2, 1, 2, 1]
    placed = [_place_block(w, ax, chip, f"place_{nm}")
              for w, ax, nm in zip([dn_w_in, dn_w_out, cf_w_in, cf_w_out, ff_w1, ff_w2], big_axes, big_names)]
    g_dn_in, g_dn_out, g_cf_in, g_cf_out, g_w1, g_w2 = _gather_two_level(list(zip(placed, big_axes)),
                                                                         "gather_weights")
    g_dn_in = g_dn_in.reshape(LA, 4, D, dn_in // 4)
    g_small = _exchange([(packed_small, 0)], "xy", "gather", "gather_small")[0]
    w_dn_in = jnp.transpose(g_dn_in, (0, 2, 1, 3)).reshape(LA, D, dn_in)
    w_dn_main = w_dn_in[:, :, :n_main]
    w_dn_small = jnp.pad(w_dn_in[:, :, n_main:], ((0, 0), (0, 0), (0, LANES - 2 * H)))
    shard_shapes = [a.shape for a in small_sharded]
    per_chip = [_unpack(g_small[q], shard_shapes) for q in range(4)]
    ln_g_f, ln_b_f, conv_w_f, dw_w_f, dw_b_f, cln_g_f, cln_b_f = [
        jnp.concatenate([per_chip[q][i] for q in range(4)], axis=small_axes[i]) for i in range(len(small_sharded))]

    c_all = _exchange([(c[None], 0)], "all", "gather", "gather_cond")[0].reshape(8, D)
    c_pad = jnp.pad(c_all, ((0, 8), (0, 0)))
    mod_sh = jnp.stack([_mm(c_pad, (ada_w, l), "nn", name=f"ada_{l}", a_fn=lambda t: t * _sigmoid(t))
                        for l in range(L)])
    mod_all = _exchange([(mod_sh, 2)], "xy", "gather", "gather_mod")[0]
    mod_mine = lax.dynamic_index_in_dim(mod_all, dev, axis=1, keepdims=False) + ada_b
    mods = mod_mine.reshape(L, NMOD, D)

    def lnp_of(l):
        return jnp.stack([ln_g_f[l, 0], ln_b_f[l, 0], ln_g_f[l, 1], ln_b_f[l, 1]])

    def dn_wts(j):
        prm = jnp.zeros((2, LANES), F32).at[0, H:2 * H].set(dn_a_log[j]).at[1, H:2 * H].set(dn_dt_bias[j])
        return ((w_dn_main, j), (w_dn_small, j), conv_w_f[j], prm, dn_norm_w[j][None], (g_dn_out, j))

    def cf_wts(j):
        return ((g_cf_in, j), dw_w_f[j], dw_b_f[j][None], jnp.stack([cln_g_f[j], cln_b_f[j]]), (g_cf_out, j))

    xs = x0
    saved = []
    for l in range(L):
        j = l // 2
        if l % 2 == 0:
            xs, sv_a = _dn_fwd(xs, mods[l], lnp_of(l), dn_wts(j), H, f"l{l}_dn")
        else:
            xs, sv_a = _cf_fwd(xs, mods[l], lnp_of(l), cf_wts(j), f"l{l}_cf")
        xs, sv_b = _mlp_fwd(xs, mods[l], lnp_of(l), (g_w1, l), (g_w2, l), f"l{l}_ff")
        saved.append((sv_a, sv_b))
    dx, loss_local = _loss_head(xs, tgt, "loss_head")
    loss = lax.psum(loss_local[0, 0], ("x", "y", "c"))

    stacks = {}
    g_dn = [None] * LA
    g_cf = [None] * LB
    dmods, dlns = [None] * L, [None] * L
    for l in reversed(range(L)):
        j = l // 2
        sv_a, sv_b = saved[l]
        dx, (dss2, dp2) = _mlp_bwd(dx, sv_b, mods[l], lnp_of(l), (g_w1, l), (g_w2, l), f"l{l}_ff", stacks)
        if l % 2 == 0:
            dx, g_dn[j], (dss1, dp1) = _dn_bwd(dx, sv_a, mods[l], lnp_of(l), dn_wts(j), H, f"l{l}_dn", stacks)
        else:
            dx, g_cf[j], (dss1, dp1) = _cf_bwd(dx, sv_a, mods[l], lnp_of(l), cf_wts(j), f"l{l}_cf", stacks)
        dmods[l] = jnp.concatenate([dss1, dp1[0:1], dss2, dp2[0:1]], axis=0)
        dlns[l] = (jnp.stack([dp1[1], dp2[1]]), jnp.stack([dp1[2], dp2[2]]))
    grad_x = dx[None]

    d_ln_g = jnp.stack([dlns[l][0] for l in range(L)])
    d_ln_b = jnp.stack([dlns[l][1] for l in range(L)])
    d_conv_w = jnp.stack([g_dn[j][2] for j in range(LA)])
    d_a_log = jnp.stack([g_dn[j][3][0, H:2 * H] for j in range(LA)])
    d_dt_bias = jnp.stack([g_dn[j][3][1, H:2 * H] for j in range(LA)])
    d_norm_w = jnp.stack([g_dn[j][4][0] for j in range(LA)])
    d_dw_w = jnp.stack([g_cf[j][0] for j in range(LB)])
    d_dw_b = jnp.stack([g_cf[j][1][0] for j in range(LB)])
    d_cln_g = jnp.stack([g_cf[j][2][0] for j in range(LB)])
    d_cln_b = jnp.stack([g_cf[j][2][1] for j in range(LB)])
    d_mod = jnp.stack(dmods).reshape(L, NMOD * D)
    small_full = [d_mod, d_ln_g, d_ln_b, d_conv_w, d_dw_w, d_dw_b, d_cln_g, d_cln_b, d_a_log, d_dt_bias, d_norm_w]
    small_all = _exchange([(_pack(small_full, rows=8)[None], 0)], "all", "gather", "gather_small_grads")[0]
    small_sum = _sum_slots(small_all, "sum_small_grads")
    (s_ada_b, s_ln_g, s_ln_b, s_conv_w, s_dw_w, s_dw_b, s_cln_g, s_cln_b, s_a_log, s_dt_bias, s_norm_w) = _unpack(
        small_sum, [a.shape for a in small_full])
    d_mod_all = small_all.reshape(8, -1)[:, :L * NMOD * D].reshape(8, L, NMOD * D)

    def shard(a, axis):
        size = a.shape[axis] // 4
        return lax.dynamic_slice_in_dim(a, chip * size, size, axis)

    ncol = ada_w.shape[2]
    d_mod_sh = jnp.pad(lax.dynamic_slice_in_dim(d_mod_all, chip * ncol, ncol, 2), ((0, 8), (0, 0), (0, 0)))
    g_ada_w = jnp.stack([_mm(c_pad, d_mod_sh[:, l], "tn", name=f"ada_bw_{l}", a_fn=lambda t: t * _sigmoid(t))
                         for l in range(L)])

    dn_in_full = jnp.stack([jnp.concatenate([g_dn[j][0], g_dn[j][1][:, :2 * H]], axis=1) for j in range(LA)])
    dn_in_st = jnp.transpose(dn_in_full.reshape(LA, D, 4, dn_in // 4), (0, 2, 1, 3))
    big = [(dn_in_st, 1), (stacks["dn_w_out"], 1), (stacks["cf_w_in"], 2), (stacks["cf_w_out"], 1),
           (stacks["ff_w1"], 2), (stacks["ff_w2"], 1)]
    pair = _exchange([(g, 0) for g, _ in big], "c", "scatter", "pair_grads", nsplit=8, keep_own=False)
    pair_sums = [_sum_pair(g.reshape(2, -1, g.shape[-1]), r.reshape(2, -1, r.shape[-1]), ic, f"sum_pair_{i}",
                           _MXU).reshape(r.shape[1:])
                 for i, ((g, _), r) in enumerate(zip(big, pair))]
    recv = _exchange([(s, ax) for s, (_, ax) in zip(pair_sums, big)], "xy", "scatter", "scatter_grads")
    sums = [_sum_slots(r.reshape(4, -1, r.shape[-1]), f"sum_grads_{i}") for i, r in enumerate(recv)]
    other = _exchange([(s[None], 0) for s in sums], "c", "gather", "swap_sums", nsplit=4, keep_own=False)

    big_w = [(dn_w_in, m_dn_w_in, v_dn_w_in), (dn_w_out, m_dn_w_out, v_dn_w_out), (cf_w_in, m_cf_w_in, v_cf_w_in),
             (cf_w_out, m_cf_w_out, v_cf_w_out), (ff_w1, m_ff_w1, v_ff_w1), (ff_w2, m_ff_w2, v_ff_w2)]
    big_out = []
    for i, ((w, m, v), own, oth) in enumerate(zip(big_w, sums, other)):
        res = _adamw_halves(_two_d(w), _two_d(m), _two_d(v), own, oth, ic, f"adamw_big_{i}")
        big_out.append([r.reshape(w.shape) for r in res])
    ada_out = [r.reshape(ada_w.shape) for r in _adamw(_two_d(ada_w), _two_d(m_ada_w), _two_d(v_ada_w),
                                                     _two_d(g_ada_w)[None], "adamw_ada_w")]

    small_w = [(ada_b, m_ada_b, v_ada_b, s_ada_b), (ln_g, m_ln_g, v_ln_g, shard(s_ln_g, 2)),
               (ln_b, m_ln_b, v_ln_b, shard(s_ln_b, 2)), (dn_conv_w, m_dn_conv_w, v_dn_conv_w, shard(s_conv_w, 2)),
               (dn_a_log, m_dn_a_log, v_dn_a_log, s_a_log), (dn_dt_bias, m_dn_dt_bias, v_dn_dt_bias, s_dt_bias),
               (dn_norm_w, m_dn_norm_w, v_dn_norm_w, s_norm_w), (cf_dw_w, m_cf_dw_w, v_cf_dw_w, shard(s_dw_w, 2)),
               (cf_dw_b, m_cf_dw_b, v_cf_dw_b, shard(s_dw_b, 1)), (cf_ln_g, m_cf_ln_g, v_cf_ln_g, shard(s_cln_g, 1)),
               (cf_ln_b, m_cf_ln_b, v_cf_ln_b, shard(s_cln_b, 1))]
    pk = [_pack([t[i] for t in small_w], rows=8) for i in range(4)]
    small_res = _adamw(pk[0], pk[1], pk[2], pk[3][None], "adamw_small")
    small_shapes = [t[0].shape for t in small_w]
    small_out = [_unpack(r, small_shapes) for r in small_res]

    def kind(k):
        sm = small_out[k]
        bg = [o[k] for o in big_out]
        return [ada_out[k], sm[0], sm[1], sm[2], bg[0], sm[3], sm[4], sm[5], sm[6], bg[1],
                bg[2], sm[7], sm[8], sm[9], sm[10], bg[3], bg[4], bg[5]]

    return (loss, grad_x, *kind(0), *kind(1), *kind(2), *kind(3))
```

```python
import functools

import jax
import jax.numpy as jnp
from jax import lax
from jax.experimental import pallas as pl
from jax.experimental.pallas import tpu as pltpu

F32 = jnp.float32
_MXU = jnp.bfloat16
_HI = lax.Precision.HIGHEST

N_LAYERS = 4
ALPHA = (2.0 * N_LAYERS) ** 0.25
LN_EPS = 1e-5
RMS_EPS = 1e-6
L2_EPS = 1e-6
CHUNK = 64
ADAM_LR, ADAM_B1, ADAM_B2, ADAM_EPS, ADAM_WD, ADAM_STEP = 0.001, 0.9, 0.999, 1e-08, 0.01, 10

LANES = 128
TOKEN_BLOCK = 256
VMEM_BIG = 48 * 1024 * 1024
MM_TILE = 1024

SDS = jax.ShapeDtypeStruct
MESH = pl.DeviceIdType.MESH


def _cparams(vmem=None):
    if vmem is None:
        return None
    return pltpu.CompilerParams(vmem_limit_bytes=vmem)


def _pcall(body, **kw):
    if kw.get("compiler_params", 1) is None:
        kw.pop("compiler_params")
    return pl.pallas_call(body, **kw)


def _full(arr):
    nd = arr.ndim
    return pl.BlockSpec(arr.shape, lambda *g: (0,) * nd)


def _bs(block, imap, lead=None):
    if lead is None:
        return pl.BlockSpec(block, imap)
    return pl.BlockSpec((None,) + tuple(block), lambda *g: (lead,) + tuple(imap(*g)))


def _split(a):
    return a if isinstance(a, tuple) else (a, None)


_GROUPS = {
    "xy": ([(1, 0, 0), (0, 1, 0), (1, 1, 0)], 4),
    "c": ([(0, 0, 1)], 2),
    "all": ([(1, 0, 0), (0, 1, 0), (1, 1, 0), (0, 0, 1), (1, 0, 1), (0, 1, 1), (1, 1, 1)], 8),
}


def _exchange(items, group, mode, name, nsplit=1, keep_own=True):
    masks, n = _GROUPS[group]
    npeer = len(masks)
    ni = len(items)
    arrs = [a for a, _ in items]
    out_shapes = []
    for a, ax in items:
        shp = list(a.shape)
        if mode == "gather":
            shp[ax] *= n
        else:
            shp[ax] //= n
            shp = [n] + shp
        out_shapes.append(SDS(tuple(shp), a.dtype))

    def body(*refs):
        ins, outs = refs[:ni], refs[ni:2 * ni]
        send_sems, recv_sems, local_sems = refs[2 * ni:]
        x, y, c = lax.axis_index("x"), lax.axis_index("y"), lax.axis_index("c")

        def slot(px, py, pc):
            if group == "xy":
                return 2 * px + py
            if group == "c":
                return pc
            return 4 * px + 2 * py + pc

        me = slot(x, y, c)

        def block(ref, ax, idx, size):
            ix = (slice(None),) * ax + (pl.ds(pl.multiple_of(idx * size, size), size),)
            return ref.at[ix]

        copies = []
        for it, (a, ax) in enumerate(items):
            in_ref, out_ref = ins[it], outs[it]
            if mode == "gather":
                size = a.shape[ax]
                src_own, dst_own = in_ref, block(out_ref, ax, me, size)
            else:
                size = a.shape[ax] // n
                src_own, dst_own = block(in_ref, ax, me, size), out_ref.at[me]
            sax, ns, cs = splits[it]
            pieces = [(slice(None),) * sax + (pl.ds(j * cs, cs),) for j in range(ns)]
            if keep_own:
                for j, piece in enumerate(pieces):
                    own = pltpu.make_async_copy(src_own.at[piece], dst_own.at[piece], local_sems.at[it * nsplit + j])
                    own.start()
                    copies.append(own)
            for k, m in enumerate(masks):
                peer = tuple((1 - v) if b else v for v, b in zip((x, y, c), m))
                if mode == "gather":
                    src, dst = in_ref, dst_own
                else:
                    src, dst = block(in_ref, ax, slot(*peer), size), out_ref.at[me]
                for j, piece in enumerate(pieces):
                    sem = (it * npeer + k) * nsplit + j
                    cp = pltpu.make_async_remote_copy(
                        src_ref=src.at[piece], dst_ref=dst.at[piece], send_sem=send_sems.at[sem],
                        recv_sem=recv_sems.at[sem], device_id=peer, device_id_type=MESH)
                    cp.start()
                    copies.append(cp)
        for cp in copies:
            cp.wait()

    splits = []
    for (a, ax), o in zip(items, out_shapes):
        bshape = a.shape if mode == "gather" else o.shape[1:]
        sax = max(range(len(bshape) - 1), key=lambda d: bshape[d])
        ns = nsplit if bshape[sax] % (nsplit * 16) == 0 else 1
        splits.append((sax, ns, bshape[sax] // ns))
    any_spec = pl.BlockSpec(memory_space=pl.ANY)
    nsem = ni * npeer * nsplit
    outs = _pcall(
        body, name=name, out_shape=tuple(out_shapes),
        in_specs=[any_spec] * ni, out_specs=tuple([any_spec] * ni),
        scratch_shapes=[pltpu.SemaphoreType.DMA((nsem,)), pltpu.SemaphoreType.DMA((nsem,)),
                        pltpu.SemaphoreType.DMA((ni * nsplit,))],
    )(*arrs)
    return list(outs)


_HBM = pl.BlockSpec(memory_space=pltpu.HBM)
_SEM = pl.BlockSpec(memory_space=pltpu.SEMAPHORE)
_ANY = pl.BlockSpec(memory_space=pl.ANY)
_SPLIT = pltpu.CompilerParams(has_side_effects=pltpu.SideEffectType.DATAFLOW_SIDE_EFFECTING)
_XY = _GROUPS["xy"][0]


def _in_hbm(a):
    return pltpu.with_memory_space_constraint(a, pltpu.HBM)


def _chip_peers():
    x, y, c = lax.axis_index("x"), lax.axis_index("y"), lax.axis_index("c")
    return x, y, c, [tuple((1 - v) if b else v for v, b in zip((x, y), m)) for m in _XY]


def _wblock(ref, ax, half, chip):
    R, C = ref.shape
    if ax == 0:
        rows = R // 8
        return ref.at[pl.ds(pl.multiple_of(chip * (2 * rows) + half * rows, rows), rows), :]
    rows, cols = R // 2, C // 4
    return ref.at[pl.ds(pl.multiple_of(half * rows, rows), rows), pl.ds(pl.multiple_of(chip * cols, cols), cols)]


def _gather_start(items, after, name):
    ni = len(items)
    arrs = [a for a, _ in items]

    def body(*refs):
        send_sems, recv_sems = refs[ni + 1], refs[ni + 2]
        outs, token = refs[ni + 3:2 * ni + 3], refs[2 * ni + 3]
        x, y, c, peers = _chip_peers()
        for it, (_, ax) in enumerate(items):
            mine = _wblock(outs[it], ax, c, 2 * x + y)
            for k, (px, py) in enumerate(peers):
                pltpu.make_async_remote_copy(
                    src_ref=mine, dst_ref=mine, send_sem=send_sems.at[it * 3 + k], recv_sem=recv_sems.at[it * 3 + k],
                    device_id=(px, py, c), device_id_type=MESH).start()
        token[...] = jnp.zeros_like(token)

    res = _pcall(
        body, name=name,
        out_shape=(pltpu.SemaphoreType.DMA((ni * 3,)), pltpu.SemaphoreType.DMA((ni * 3,)),
                   *[pltpu.HBM(a.shape, a.dtype) for a in arrs], SDS((8, LANES), F32)),
        in_specs=[_HBM] * ni + [_ANY],
        out_specs=(_SEM, _SEM, *[_HBM] * ni, pl.BlockSpec(memory_space=pltpu.VMEM)),
        input_output_aliases={i: 2 + i for i in range(ni)}, compiler_params=_SPLIT,
    )(*[_in_hbm(a) for a in arrs], after)
    return res[0], res[1], list(res[2:2 + ni]), res[2 + ni]


def _gather_wait(items, send_sems, recv_sems, after, name):
    ni = len(items)
    arrs = [a for a, _ in items]

    def body(*refs):
        s_sems, r_sems = refs[ni], refs[ni + 1]
        outs = refs[ni + 3:]
        x, y, c, peers = _chip_peers()
        for it, (_, ax) in enumerate(items):
            mine = _wblock(outs[it], ax, c, 2 * x + y)
            for k, (px, py) in enumerate(peers):
                cp = pltpu.make_async_remote_copy(
                    src_ref=mine, dst_ref=_wblock(outs[it], ax, c, 2 * px + py), send_sem=s_sems.at[it * 3 + k],
                    recv_sem=r_sems.at[it * 3 + k], device_id=(px, py, c), device_id_type=MESH)
                cp.wait_send()
                cp.wait_recv()

    res = _pcall(
        body, name=name, out_shape=tuple(pltpu.HBM(a.shape, a.dtype) for a in arrs),
        in_specs=[_HBM] * ni + [_SEM, _SEM, _ANY], out_specs=tuple([_HBM] * ni),
        input_output_aliases={i: i for i in range(ni)}, compiler_params=_SPLIT,
    )(*arrs, send_sems, recv_sems, after)
    return list(res)


def _gather_forward(items, name):
    ni = len(items)
    arrs = [a for a, _ in items]

    def body(*refs):
        outs = refs[ni:2 * ni]
        send_sems, recv_sems = refs[2 * ni:]
        x, y, c, peers = _chip_peers()
        def copy(it, ax, k, chip, dst_half):
            return pltpu.make_async_remote_copy(
                src_ref=_wblock(outs[it], ax, c, chip), dst_ref=_wblock(outs[it], ax, dst_half, chip),
                send_sem=send_sems.at[it * 3 + k], recv_sem=recv_sems.at[it * 3 + k],
                device_id=(x, y, 1 - c), device_id_type=MESH)

        for it, (_, ax) in enumerate(items):
            for k, (px, py) in enumerate(peers):
                copy(it, ax, k, 2 * px + py, c).start()
        for it, (_, ax) in enumerate(items):
            for k, (px, py) in enumerate(peers):
                copy(it, ax, k, 2 * px + py, c).wait_send()
                copy(it, ax, k, 2 * px + py, 1 - c).wait_recv()

    res = _pcall(
        body, name=name, out_shape=tuple(SDS(a.shape, a.dtype) for a in arrs),
        in_specs=[_ANY] * ni, out_specs=tuple([_ANY] * ni), input_output_aliases={i: i for i in range(ni)},
        scratch_shapes=[pltpu.SemaphoreType.DMA((ni * 3,)), pltpu.SemaphoreType.DMA((ni * 3,))],
    )(*arrs)
    return list(res)


def _gblock(ref, ax, chip):
    if ax == 0:
        return ref.at[chip]
    cols = ref.shape[1] // 4
    return ref.at[:, pl.ds(pl.multiple_of(chip * cols, cols), cols)]


def _scatter_start(items, after, name):
    ni = len(items)
    arrs = [a for a, _ in items]
    lands = []
    for a, ax in items:
        blk = a.shape[1:] if ax == 0 else (a.shape[0], a.shape[1] // 4)
        lands.append(lax.empty((4, *blk), a.dtype))

    def body(*refs):
        send_sems, recv_sems = refs[2 * ni + 1], refs[2 * ni + 2]
        srcs, dsts = refs[2 * ni + 3:3 * ni + 3], refs[3 * ni + 3:4 * ni + 3]
        token = refs[4 * ni + 3]
        x, y, c, peers = _chip_peers()
        for it, (_, ax) in enumerate(items):
            for k, (px, py) in enumerate(peers):
                pltpu.make_async_remote_copy(
                    src_ref=_gblock(srcs[it], ax, 2 * px + py), dst_ref=dsts[it].at[2 * x + y],
                    send_sem=send_sems.at[it * 3 + k], recv_sem=recv_sems.at[it * 3 + k],
                    device_id=(px, py, c), device_id_type=MESH).start()
        token[...] = jnp.zeros_like(token)

    res = _pcall(
        body, name=name,
        out_shape=(pltpu.SemaphoreType.DMA((ni * 3,)), pltpu.SemaphoreType.DMA((ni * 3,)),
                   *[pltpu.HBM(a.shape, a.dtype) for a in arrs], *[pltpu.HBM(a.shape, a.dtype) for a in lands],
                   SDS((8, LANES), F32)),
        in_specs=[_HBM] * (2 * ni) + [_ANY],
        out_specs=(_SEM, _SEM, *[_HBM] * (2 * ni), pl.BlockSpec(memory_space=pltpu.VMEM)),
        input_output_aliases={i: 2 + i for i in range(2 * ni)}, compiler_params=_SPLIT,
    )(*[_in_hbm(a) for a in arrs], *[_in_hbm(a) for a in lands], after)
    return res[0], res[1], list(res[2:2 + ni]), list(res[2 + ni:2 + 2 * ni]), res[2 + 2 * ni]


def _scatter_wait(items, lands, send_sems, recv_sems, after, name):
    ni = len(items)
    arrs = [a for a, _ in items]

    def body(*refs):
        s_sems, r_sems = refs[2 * ni], refs[2 * ni + 1]
        srcs, dsts = refs[2 * ni + 3:3 * ni + 3], refs[3 * ni + 3:]
        x, y, c, peers = _chip_peers()
        for it, (_, ax) in enumerate(items):
            for k, (px, py) in enumerate(peers):
                cp = pltpu.make_async_remote_copy(
                    src_ref=_gblock(srcs[it], ax, 2 * px + py), dst_ref=dsts[it].at[2 * px + py],
                    send_sem=s_sems.at[it * 3 + k], recv_sem=r_sems.at[it * 3 + k],
                    device_id=(px, py, c), device_id_type=MESH)
                cp.wait_send()
                cp.wait_recv()

    res = _pcall(
        body, name=name, out_shape=tuple(pltpu.HBM(a.shape, a.dtype) for a in arrs + list(lands)),
        in_specs=[_HBM] * (2 * ni) + [_SEM, _SEM, _ANY], out_specs=tuple([_HBM] * (2 * ni)),
        input_output_aliases={i: i for i in range(2 * ni)}, compiler_params=_SPLIT,
    )(*arrs, *lands, send_sems, recv_sems, after)
    return list(res[:ni]), list(res[ni:])


def _tile(n, cap):
    if n <= cap:
        return n
    t = cap - cap % LANES
    while n % t:
        t -= LANES
    return t


def _mm(a, b, mode, *, name, out_dtypes=(F32,), tm=MM_TILE, tn=MM_TILE, tk=MM_TILE, a_fn=None, out_fn=None, aux=(),
        stack=None):
    (a, a_lead), (b, b_lead) = _split(a), _split(b)
    ash, bsh = a.shape[-2:], b.shape[-2:]
    if mode == "nn":
        (M, K), (_, N) = ash, bsh
    elif mode == "nt":
        (M, K), (N, _) = ash, bsh
    else:
        (K, M), (_, N) = ash, bsh
    tm, tn, tk = _tile(M, tm), _tile(N, tn), _tile(K, tk)
    nk = K // tk
    if mode == "tn":
        a_spec = _bs((tk, tm), lambda i, j, k: (k, i), a_lead)
    else:
        a_spec = _bs((tm, tk), lambda i, j, k: (i, k), a_lead)
    if mode == "nt":
        b_spec = _bs((tn, tk), lambda i, j, k: (j, k), b_lead)
    else:
        b_spec = _bs((tk, tn), lambda i, j, k: (k, j), b_lead)
    aux_arrs, aux_specs = [], []
    for arr, kind in aux:
        arr, lead = _split(arr)
        aux_arrs.append(arr)
        if kind == "mn":
            aux_specs.append(_bs((tm, tn), lambda i, j, k: (i, j), lead))
        else:
            aux_specs.append(_bs((1, tn), lambda i, j, k: (0, j), lead))
    na, no = len(aux_arrs), len(out_dtypes)
    dims = {"nn": (((1,), (0,)), ((), ())), "nt": (((1,), (1,)), ((), ())), "tn": (((0,), (0,)), ((), ()))}[mode]

    def finish(r, aux_refs, o_refs):
        outs = out_fn(r, *[x[...] for x in aux_refs]) if out_fn is not None else (r,)
        for o_ref, val in zip(o_refs, outs):
            o_ref[...] = val.astype(o_ref.dtype)

    def product(a_ref, b_ref):
        av = a_ref[...]
        if a_fn is not None:
            av = a_fn(av.astype(F32))
        return lax.dot_general(av.astype(_MXU), b_ref[...].astype(_MXU), dims, preferred_element_type=F32)

    nbuf = 0 if stack is None or stack[0] is None else 1

    def body_one(a_ref, b_ref, *rest):
        finish(product(a_ref, b_ref), rest[:na], rest[na + nbuf:na + nbuf + no])

    def body_acc(a_ref, b_ref, *rest):
        aux_refs, o_refs, acc = rest[:na], rest[na + nbuf:na + nbuf + no], rest[na + nbuf + no]
        k = pl.program_id(2)

        @pl.when(k == 0)
        def _():
            acc[...] = product(a_ref, b_ref)

        @pl.when(k != 0)
        def _():
            acc[...] += product(a_ref, b_ref)

        @pl.when(k == nk - 1)
        def _():
            finish(acc[...], aux_refs, o_refs)

    extra, aliases = {}, []
    if stack is None:
        out_shape = (M, N)
        o_spec = pl.BlockSpec((tm, tn), lambda i, j, k: (i, j))
    else:
        buf, layer, n_layers = stack
        assert no == 1
        out_shape = (n_layers, M, N)
        o_spec = pl.BlockSpec((None, tm, tn), lambda i, j, k: (layer, i, j))
        if buf is not None:
            aliases = [buf]
            extra = dict(input_output_aliases={2 + na: 0})
    outs = _pcall(
        body_one if nk == 1 else body_acc, name=name, grid=(M // tm, N // tn, nk),
        in_specs=[a_spec, b_spec] + aux_specs + [pl.BlockSpec(memory_space=pl.ANY)] * nbuf,
        out_specs=tuple([o_spec] * no),
        out_shape=tuple(SDS(out_shape, dt) for dt in out_dtypes),
        scratch_shapes=[] if nk == 1 else [pltpu.VMEM((tm, tn), F32)],
        compiler_params=pltpu.CompilerParams(dimension_semantics=("parallel", "parallel", "arbitrary"),
                                             vmem_limit_bytes=VMEM_BIG),
        **extra,
    )(a, b, *aux_arrs, *aliases)
    return outs[0] if no == 1 else outs


def _tok(S):
    ts = min(TOKEN_BLOCK, S)
    assert S % ts == 0
    return ts


def _row(ts, D):
    return pl.BlockSpec((ts, D), lambda i: (i, 0))


def _acc_rows(ref, i, rows):
    @pl.when(i == 0)
    def _():
        for r, v in enumerate(rows):
            ref[r:r + 1, :] = v

    @pl.when(i != 0)
    def _():
        for r, v in enumerate(rows):
            ref[r:r + 1, :] += v


def _modulate(x, mod, r_sh, r_sc, name):
    S, D = x.shape
    ts = _tok(S)

    def body(x_ref, m_ref, o_ref):
        o_ref[...] = (x_ref[...] * (1.0 + m_ref[r_sc:r_sc + 1, :]) + m_ref[r_sh:r_sh + 1, :]).astype(o_ref.dtype)

    return _pcall(body, name=name, grid=(S // ts,), in_specs=[_row(ts, D), _full(mod)],
                  out_specs=_row(ts, D), out_shape=SDS((S, D), _MXU))(x, mod)


def _modulate_bwd(x, mod, r_sc, dh, dxa, name):
    S, D = x.shape
    ts = _tok(S)

    def body(x_ref, m_ref, dh_ref, dxa_ref, dx_ref, dss_ref):
        dh_v = dh_ref[...]
        dx_ref[...] = dxa_ref[...] + dh_v * (1.0 + m_ref[r_sc:r_sc + 1, :])
        _acc_rows(dss_ref, pl.program_id(0),
                  [jnp.sum(dh_v, axis=0, keepdims=True), jnp.sum(dh_v * x_ref[...], axis=0, keepdims=True)])

    return _pcall(body, name=name, grid=(S // ts,),
                  in_specs=[_row(ts, D), _full(mod), _row(ts, D), _row(ts, D)],
                  out_specs=(_row(ts, D), pl.BlockSpec((2, D), lambda i: (0, 0))),
                  out_shape=(SDS((S, D), F32), SDS((2, D), F32)))(x, mod, dh, dxa)


def _norm_stats(z):
    mu = jnp.mean(z, axis=-1, keepdims=True)
    zc = z - mu
    var = jnp.mean(zc * zc, axis=-1, keepdims=True)
    rstd = lax.rsqrt(var + LN_EPS)
    return zc * rstd, rstd


def _norm_bwd(dxhat, xhat, rstd):
    return rstd * (dxhat - jnp.mean(dxhat, axis=-1, keepdims=True)
                   - xhat * jnp.mean(dxhat * xhat, axis=-1, keepdims=True))


def _combine(x, y, mod, r_gt, lnp, r_g, name):
    S, D = x.shape
    ts = _tok(S)

    def body(x_ref, y_ref, m_ref, l_ref, o_ref):
        z = ALPHA * x_ref[...] + (1.0 + m_ref[r_gt:r_gt + 1, :]) * y_ref[...]
        xhat, _ = _norm_stats(z)
        o_ref[...] = xhat * l_ref[r_g:r_g + 1, :] + l_ref[r_g + 1:r_g + 2, :]

    return _pcall(body, name=name, grid=(S // ts,), in_specs=[_row(ts, D), _row(ts, D), _full(mod), _full(lnp)],
                  out_specs=_row(ts, D), out_shape=SDS((S, D), F32))(x, y, mod, lnp)


def _combine_bwd(x, y, mod, r_gt, lnp, r_g, dout, name):
    S, D = x.shape
    ts = _tok(S)

    def body(x_ref, y_ref, m_ref, l_ref, do_ref, dxa_ref, dy_ref, dp_ref):
        gate = 1.0 + m_ref[r_gt:r_gt + 1, :]
        y_v, do_v = y_ref[...], do_ref[...]
        xhat, rstd = _norm_stats(ALPHA * x_ref[...] + gate * y_v)
        dz = _norm_bwd(do_v * l_ref[r_g:r_g + 1, :], xhat, rstd)
        dxa_ref[...] = ALPHA * dz
        dy_ref[...] = (gate * dz).astype(dy_ref.dtype)
        _acc_rows(dp_ref, pl.program_id(0),
                  [jnp.sum(dz * y_v, axis=0, keepdims=True), jnp.sum(do_v * xhat, axis=0, keepdims=True),
                   jnp.sum(do_v, axis=0, keepdims=True)])

    return _pcall(body, name=name, grid=(S // ts,),
                  in_specs=[_row(ts, D), _row(ts, D), _full(mod), _full(lnp), _row(ts, D)],
                  out_specs=(_row(ts, D), _row(ts, D), pl.BlockSpec((3, D), lambda i: (0, 0))),
                  out_shape=(SDS((S, D), F32), SDS((S, D), _MXU), SDS((3, D), F32)))(x, y, mod, lnp, dout)


def _sigmoid(t):
    return 1.0 / (1.0 + jnp.exp(-t))


def _ln_silu(u, lnp, name):
    S, D = u.shape
    ts = _tok(S)

    def body(u_ref, l_ref, o_ref):
        xhat, _ = _norm_stats(u_ref[...])
        t = xhat * l_ref[0:1, :] + l_ref[1:2, :]
        o_ref[...] = (t * _sigmoid(t)).astype(o_ref.dtype)

    return _pcall(body, name=name, grid=(S // ts,), in_specs=[_row(ts, D), _full(lnp)],
                  out_specs=_row(ts, D), out_shape=SDS((S, D), _MXU))(u, lnp)


def _ln_silu_bwd(u, lnp, dout, name):
    S, D = u.shape
    ts = _tok(S)

    def body(u_ref, l_ref, do_ref, du_ref, dp_ref):
        xhat, rstd = _norm_stats(u_ref[...])
        g = l_ref[0:1, :]
        t = xhat * g + l_ref[1:2, :]
        sg = _sigmoid(t)
        dt = do_ref[...] * (sg * (1.0 + t * (1.0 - sg)))
        du_ref[...] = _norm_bwd(dt * g, xhat, rstd)
        _acc_rows(dp_ref, pl.program_id(0),
                  [jnp.sum(dt * xhat, axis=0, keepdims=True), jnp.sum(dt, axis=0, keepdims=True)])

    return _pcall(body, name=name, grid=(S // ts,), in_specs=[_row(ts, D), _full(lnp), _row(ts, D)],
                  out_specs=(_row(ts, D), pl.BlockSpec((2, D), lambda i: (0, 0))),
                  out_shape=(SDS((S, D), F32), SDS((2, D), F32)))(u, lnp, dout)


def _loss_head(xf, tgt, name):
    S, D = xf.shape
    ts = _tok(S)

    def body(x_ref, t_ref, dx_ref, l_ref):
        err = x_ref[...] - t_ref[...]
        dx_ref[...] = err * (1.0 / D)
        part = jnp.sum(jnp.sum(err * err, axis=1, keepdims=True), axis=0, keepdims=True) * (0.5 / D)

        @pl.when(pl.program_id(0) == 0)
        def _():
            l_ref[...] = part

        @pl.when(pl.program_id(0) != 0)
        def _():
            l_ref[...] += part

    return _pcall(body, name=name, grid=(S // ts,), in_specs=[_row(ts, D), _row(ts, D)],
                  out_specs=(_row(ts, D), pl.BlockSpec((1, 1), lambda i: (0, 0))),
                  out_shape=(SDS((S, D), F32), SDS((1, 1), F32)))(xf, tgt)


def _shift_down(u, s, rows):
    if s == 0:
        return u
    return jnp.where(rows >= s, pltpu.roll(u, s, 0), 0.0)


def _shift_up(u, s, rows):
    if s == 0:
        return u
    n = u.shape[0]
    return jnp.where(rows < n - s, pltpu.roll(u, n - s, 0), 0.0)


def _dwconv(u, w_ref, taps, rows):
    acc = jnp.zeros_like(u)
    for j in range(taps):
        acc = acc + w_ref[j:j + 1, :] * _shift_down(u, taps - 1 - j, rows)
    return acc


def _dwconv_bwd(u, dy, w_ref, dw_ref, taps, rows):
    du = jnp.zeros_like(u)
    for j in range(taps):
        s = taps - 1 - j
        du = du + w_ref[j:j + 1, :] * _shift_up(dy, s, rows)
        dw_ref[j:j + 1, :] = jnp.sum(dy * _shift_down(u, s, rows), axis=0, keepdims=True)
    return du


def _col(S, j0=0):
    return pl.BlockSpec((S, LANES), lambda j: (0, j + j0))


def _conv_silu(pm, w, nblk, name):
    S = pm.shape[0]
    taps = w.shape[0]

    def body(u_ref, w_ref, o_ref):
        rows = lax.broadcasted_iota(jnp.int32, (S, LANES), 0)
        cv = _dwconv(u_ref[...], w_ref, taps, rows)
        o_ref[...] = cv * _sigmoid(cv)

    return _pcall(body, name=name, grid=(nblk,),
                  in_specs=[_col(S), pl.BlockSpec((taps, LANES), lambda j: (0, j))],
                  out_specs=_col(S), out_shape=SDS((S, nblk * LANES), F32),
                  compiler_params=_cparams(VMEM_BIG))(pm, w)


def _conv_silu_bwd(pm, w, dout, dpm, j0, name):
    S = pm.shape[0]
    taps = w.shape[0]
    nblk = dout.shape[1] // LANES

    def body(u_ref, w_ref, do_ref, dpm_in, du_ref, dw_ref):
        del dpm_in
        rows = lax.broadcasted_iota(jnp.int32, (S, LANES), 0)
        u = u_ref[...]
        cv = _dwconv(u, w_ref, taps, rows)
        sg = _sigmoid(cv)
        dc = do_ref[...] * (sg * (1.0 + cv * (1.0 - sg)))
        du_ref[...] = _dwconv_bwd(u, dc, w_ref, dw_ref, taps, rows)

    return _pcall(body, name=name, grid=(nblk,),
                  in_specs=[_col(S, j0), pl.BlockSpec((taps, LANES), lambda j: (0, j + j0)), _col(S),
                            pl.BlockSpec(memory_space=pl.ANY)],
                  out_specs=(_col(S, j0), pl.BlockSpec((taps, LANES), lambda j: (0, j))),
                  out_shape=(SDS(dpm.shape, F32), SDS((taps, nblk * LANES), F32)),
                  input_output_aliases={3: 0},
                  compiler_params=_cparams(VMEM_BIG))(pm, w, dout, dpm)


def _glu_conv(p, w, bias, name):
    S, C2 = p.shape
    nblk = C2 // 2 // LANES
    taps = w.shape[0]

    def body(v_ref, g_ref, w_ref, b_ref, o_ref):
        rows = lax.broadcasted_iota(jnp.int32, (S, LANES), 0)
        u = v_ref[...] * _sigmoid(g_ref[...])
        o_ref[...] = _dwconv(u, w_ref, taps, rows) + b_ref[...]

    return _pcall(body, name=name, grid=(nblk,),
                  in_specs=[_col(S), _col(S, nblk), pl.BlockSpec((taps, LANES), lambda j: (0, j)),
                            pl.BlockSpec((1, LANES), lambda j: (0, j))],
                  out_specs=_col(S), out_shape=SDS((S, nblk * LANES), F32),
                  compiler_params=_cparams(VMEM_BIG))(p, p, w, bias)


def _glu_conv_bwd(p, w, dout, name):
    S, C2 = p.shape
    nblk = C2 // 2 // LANES
    taps = w.shape[0]

    def body(v_ref, g_ref, w_ref, do_ref, dv_ref, dg_ref, dw_ref, db_ref):
        rows = lax.broadcasted_iota(jnp.int32, (S, LANES), 0)
        val, sg = v_ref[...], _sigmoid(g_ref[...])
        do_v = do_ref[...]
        du = _dwconv_bwd(val * sg, do_v, w_ref, dw_ref, taps, rows)
        dv_ref[...] = du * sg
        dg_ref[...] = du * val * sg * (1.0 - sg)
        db_ref[...] = jnp.sum(do_v, axis=0, keepdims=True)

    dval, dgate, dw, db = _pcall(
        body, name=name, grid=(nblk,),
        in_specs=[_col(S), _col(S, nblk), pl.BlockSpec((taps, LANES), lambda j: (0, j)), _col(S)],
        out_specs=(_col(S), _col(S), pl.BlockSpec((taps, LANES), lambda j: (0, j)),
                   pl.BlockSpec((1, LANES), lambda j: (0, j))),
        out_shape=(SDS((S, C2 // 2), F32), SDS((S, C2 // 2), F32), SDS((taps, C2 // 2), F32), SDS((1, C2 // 2), F32)),
        compiler_params=_cparams(VMEM_BIG))(p, p, w, dout)
    return dval, dgate, dw, db


def _log1p(e):
    u = 1.0 + e
    d = jnp.where(u == 1.0, 1.0, u - 1.0)
    return jnp.where(u == 1.0, e, jnp.log(u) * (e / d))


def _gate_parts(ps, prm, H):
    lane = lax.broadcasted_iota(jnp.int32, ps.shape, 1)
    is_b, is_g = lane < H, (lane >= H) & (lane < 2 * H)
    beta = _sigmoid(ps)
    t = ps + prm[1:2, :]
    sp = jnp.maximum(t, 0.0) + _log1p(jnp.exp(-jnp.abs(t)))
    na = -jnp.exp(prm[0:1, :])
    return is_b, is_g, beta, t, sp, na


def _gates(ps, prm, H, name):
    S = ps.shape[0]
    ts = _tok(S)

    def body(p_ref, r_ref, o_ref):
        is_b, is_g, beta, _, sp, na = _gate_parts(p_ref[...], r_ref[...], H)
        o_ref[...] = jnp.where(is_b, beta, jnp.where(is_g, na * sp, 0.0))

    return _pcall(body, name=name, grid=(S // ts,), in_specs=[_row(ts, LANES), _full(prm)],
                  out_specs=_row(ts, LANES), out_shape=SDS((S, LANES), F32))(ps, prm)


def _gates_bwd(ps, prm, dgates, H, name):
    S = ps.shape[0]
    ts = _tok(S)

    def body(p_ref, r_ref, dg_ref, dp_ref, dr_ref):
        is_b, is_g, beta, t, sp, na = _gate_parts(p_ref[...], r_ref[...], H)
        dg_v = dg_ref[...]
        dsp = jnp.where(is_g, dg_v * na * _sigmoid(t), 0.0)
        dp_ref[...] = jnp.where(is_b, dg_v * beta * (1.0 - beta), dsp)
        _acc_rows(dr_ref, pl.program_id(0),
                  [jnp.sum(jnp.where(is_g, dg_v * na * sp, 0.0), axis=0, keepdims=True),
                   jnp.sum(dsp, axis=0, keepdims=True)])

    return _pcall(body, name=name, grid=(S // ts,), in_specs=[_row(ts, LANES), _full(prm), _row(ts, LANES)],
                  out_specs=(_row(ts, LANES), pl.BlockSpec((2, LANES), lambda i: (0, 0))),
                  out_shape=(SDS((S, LANES), F32), SDS((2, LANES), F32)))(ps, prm, dgates)


_NN = (((2,), (1,)), ((0,), (0,)))
_NT = (((2,), (2,)), ((0,), (0,)))
_TN = (((1,), (1,)), ((0,), (0,)))


def _mdot(a, b, dims):
    return lax.dot_general(a.astype(_MXU), b.astype(_MXU), dims, preferred_element_type=F32)


def _mdot3(a, b, dims):
    ah, bh = a.astype(_MXU), b.astype(_MXU)
    al, bl = a - ah.astype(F32), b - bh.astype(F32)
    return _mdot(ah, bh, dims) + (_mdot(ah, bl, dims) + _mdot(al, bh, dims))


def _rounded_dot(dims, da_dims, db_dims, a_first, prod=_mdot):
    @jax.custom_vjp
    def f(a, b):
        return prod(a, b, dims)

    def fwd(a, b):
        return prod(a, b, dims), (a, b)

    def bwd(res, ct):
        a, b = res
        da = prod(ct, b, da_dims) if a_first[0] else prod(b, ct, da_dims)
        db = prod(ct, a, db_dims) if a_first[1] else prod(a, ct, db_dims)
        return da, db

    f.defvjp(fwd, bwd)
    return f


_mdot_nn = _rounded_dot(_NN, _NT, _TN, (True, False))
_mdot_nt = _rounded_dot(_NT, _NN, _TN, (True, True))
_mdot_tn = _rounded_dot(_TN, _NT, _NN, (False, False))
def _unit_lower_inverse(a):
    C = a.shape[-1]
    ri = lax.broadcasted_iota(jnp.int32, (1, C, C), 1)
    ci = lax.broadcasted_iota(jnp.int32, (1, C, C), 2)
    t_inv = jnp.where(ri == ci, 1.0, 0.0) - a
    p = a
    for _ in range(max(C.bit_length() - 2, 0)):
        p = _mdot3(p, p, _NN)
        t_inv = t_inv + _mdot3(t_inv, p, _NN)
    return t_inv


@jax.custom_vjp
def _known_inverse(a, t_inv):
    del a
    return t_inv


def _known_inverse_fwd(a, t_inv):
    del a
    return t_inv, t_inv


def _known_inverse_bwd(t_inv, ct):
    da = -_mdot3(_mdot3(t_inv, ct, _TN), t_inv, _NT)
    return da, jnp.zeros_like(t_inv)


_known_inverse.defvjp(_known_inverse_fwd, _known_inverse_bwd)


def _head_cols(gates, off, H):
    lane = lax.broadcasted_iota(jnp.int32, gates.shape, 1)
    cols = [jnp.sum(jnp.where(lane == off + h, gates, 0.0), axis=-1, keepdims=True) for h in range(H)]
    return jnp.concatenate([col[None] for col in cols], axis=0)


def _delta_chunk(qr, kr, v, z, gates, nw, s_in, t_known=None):
    H, C, dk = qr.shape
    beta, g = _head_cols(gates, 0, H), _head_cols(gates, H, H)
    q = qr * lax.rsqrt(jnp.sum(qr * qr, axis=-1, keepdims=True) + L2_EPS) * (dk ** -0.5)
    k = kr * lax.rsqrt(jnp.sum(kr * kr, axis=-1, keepdims=True) + L2_EPS)
    ri = lax.broadcasted_iota(jnp.int32, (1, C, C), 1)
    ci = lax.broadcasted_iota(jnp.int32, (1, C, C), 2)
    causal, strict, eye = ri >= ci, ri > ci, ri == ci
    gam_row = jnp.sum(jnp.where(ri <= ci, g, 0.0), axis=1, keepdims=True)
    gam_col = jnp.sum(jnp.where(eye, gam_row, 0.0), axis=-1, keepdims=True)
    g_last = jnp.sum(g, axis=1, keepdims=True)
    decay = jnp.where(causal, jnp.exp(jnp.where(causal, gam_col - gam_row, 0.0)), 0.0)
    kb = k * beta
    a = jnp.where(strict, _mdot_nt(kb, k) * decay, 0.0)
    t_inv = _unit_lower_inverse(a) if t_known is None else _known_inverse(a, t_known)
    eg = jnp.exp(gam_col)
    u = _mdot_nn(t_inv, v * beta)
    w = _mdot_nn(t_inv, kb * eg)
    a_qk = _mdot_nt(q, k) * decay
    v_new = u - _mdot_nn(w, s_in)
    o = _mdot_nn(q * eg, s_in) + _mdot_nn(a_qk, v_new)
    s_out = s_in * jnp.exp(g_last) + _mdot_tn(k * jnp.exp(g_last - gam_col), v_new)
    og = o * lax.rsqrt(jnp.mean(o * o, axis=-1, keepdims=True) + RMS_EPS) * nw * (z * _sigmoid(z))
    return og, s_out, t_inv


def _heads(ref, H, dk):
    return jnp.stack([ref[:, h * dk:(h + 1) * dk].astype(F32) for h in range(H)])


def _put_heads(ref, val, dk):
    for h in range(val.shape[0]):
        ref[:, h * dk:(h + 1) * dk] = val[h].astype(ref.dtype)


def _delta_fwd(qkv, pm, gates, nw, H, name):
    S = qkv.shape[0]
    hd = qkv.shape[1] // 3
    dk = hd // H
    N = S // CHUNK
    blk = lambda off: pl.BlockSpec((CHUNK, hd), lambda n: (n, off))

    def body(q_ref, k_ref, v_ref, z_ref, g_ref, nw_ref, og_ref, st_ref, ti_ref, s_scr):
        @pl.when(pl.program_id(0) == 0)
        def _():
            s_scr[...] = jnp.zeros_like(s_scr)

        s_in = s_scr[...]
        st_ref[...] = s_in
        og, s_out, t_inv = _delta_chunk(_heads(q_ref, H, dk), _heads(k_ref, H, dk), _heads(v_ref, H, dk),
                                        _heads(z_ref, H, dk), g_ref[...], nw_ref[...], s_in)
        _put_heads(og_ref, og, dk)
        ti_ref[...] = t_inv
        s_scr[...] = s_out

    return _pcall(
        body, name=name, grid=(N,),
        in_specs=[blk(0), blk(1), blk(2), blk(3), pl.BlockSpec((CHUNK, LANES), lambda n: (n, 0)), _full(nw)],
        out_specs=(blk(0), pl.BlockSpec((None, H, dk, dk), lambda n: (n, 0, 0, 0)),
                   pl.BlockSpec((None, H, CHUNK, CHUNK), lambda n: (n, 0, 0, 0))),
        out_shape=(SDS((S, hd), _MXU), SDS((N, H, dk, dk), F32), SDS((N, H, CHUNK, CHUNK), F32)),
        scratch_shapes=[pltpu.VMEM((H, dk, dk), F32)],
        compiler_params=_cparams(VMEM_BIG),
    )(qkv, qkv, qkv, pm, gates, nw)


def _delta_bwd(qkv, pm, gates, nw, states, t_invs, dog, H, name):
    S = qkv.shape[0]
    hd = qkv.shape[1] // 3
    dk = hd // H
    N = S // CHUNK
    blk = lambda off: pl.BlockSpec((CHUNK, hd), lambda n: (N - 1 - n, off))
    gspec = pl.BlockSpec((CHUNK, LANES), lambda n: (N - 1 - n, 0))

    def body(q_ref, k_ref, v_ref, z_ref, g_ref, nw_ref, st_ref, ti_ref, do_ref,
             dq_ref, dk_ref, dv_ref, dz_ref, dg_ref, dnw_ref, ds_scr):
        n = pl.program_id(0)

        @pl.when(n == 0)
        def _():
            ds_scr[...] = jnp.zeros_like(ds_scr)

        t_known = ti_ref[...]
        fn = functools.partial(_delta_chunk, t_known=t_known)
        _, vjp = jax.vjp(fn, _heads(q_ref, H, dk), _heads(k_ref, H, dk), _heads(v_ref, H, dk),
                         _heads(z_ref, H, dk), g_ref[...], nw_ref[...], st_ref[...])
        dq, dkk, dv, dz, dg, dnw, ds_in = vjp((_heads(do_ref, H, dk), ds_scr[...], jnp.zeros_like(t_known)))
        _put_heads(dq_ref, dq, dk)
        _put_heads(dk_ref, dkk, dk)
        _put_heads(dv_ref, dv, dk)
        _put_heads(dz_ref, dz, dk)
        ds_scr[...] = ds_in
        dg_ref[...] = dg

        @pl.when(n == 0)
        def _():
            dnw_ref[...] = dnw

        @pl.when(n != 0)
        def _():
            dnw_ref[...] += dnw

    return _pcall(
        body, name=name, grid=(N,),
        in_specs=[blk(0), blk(1), blk(2), blk(3), gspec, _full(nw),
                  pl.BlockSpec((None, H, dk, dk), lambda n: (N - 1 - n, 0, 0, 0)),
                  pl.BlockSpec((None, H, CHUNK, CHUNK), lambda n: (N - 1 - n, 0, 0, 0)), blk(0)],
        out_specs=(blk(0), blk(0), blk(0), blk(3), gspec, pl.BlockSpec((1, dk), lambda n: (0, 0))),
        out_shape=(SDS((S, hd), F32), SDS((S, hd), F32), SDS((S, hd), F32), SDS(pm.shape, F32),
                   SDS((S, LANES), F32), SDS((1, dk), F32)),
        scratch_shapes=[pltpu.VMEM((H, dk, dk), F32)],
        compiler_params=_cparams(VMEM_BIG),
    )(qkv, qkv, qkv, pm, gates, nw, states, t_invs, dog)


def _rows_block(R, C):
    rb = R
    while rb * C * 4 > (1 << 20) and rb % 16 == 0:
        rb //= 2
    return rb


def _sum_slots(st, name, out_dtype=F32):
    n, R, C = st.shape
    rb = _rows_block(R, C)

    def body(s_ref, o_ref):
        acc = s_ref[0].astype(F32)
        for q in range(1, n):
            acc = acc + s_ref[q].astype(F32)
        o_ref[...] = acc.astype(o_ref.dtype)

    return _pcall(body, name=name, grid=(R // rb,), in_specs=[pl.BlockSpec((n, rb, C), lambda i: (0, i, 0))],
                  out_specs=pl.BlockSpec((rb, C), lambda i: (i, 0)), out_shape=SDS((R, C), out_dtype))(st)


def _scalar(v):
    return jnp.reshape(v, (1,)).astype(jnp.int32)


def _place_block(w, layer, ax, chip, name):
    _, R, C = w.shape
    rb = _rows_block(R, C)
    nrb = R // rb
    shp = [R, C]
    shp[ax] *= 4
    omap = (lambda i, c: (c[0] * nrb + i, 0)) if ax == 0 else (lambda i, c: (i, c[0]))

    def body(c_ref, w_ref, o_ref):
        del c_ref
        o_ref[...] = w_ref[...].astype(o_ref.dtype)

    grid_spec = pltpu.PrefetchScalarGridSpec(
        num_scalar_prefetch=1, grid=(nrb,),
        in_specs=[pl.BlockSpec((None, rb, C), lambda i, c: (layer, i, 0))],
        out_specs=pl.BlockSpec((rb, C), omap))
    return _pcall(body, name=name, grid_spec=grid_spec, out_shape=SDS(tuple(shp), _MXU))(_scalar(chip), w)


def _sum_pair(own, recv, ic, name, out_dtype):
    N, _, R, C = own.shape
    rb = _rows_block(R, C)

    def body(c_ref, a_ref, b_ref, o_ref):
        del c_ref
        o_ref[...] = (a_ref[...].astype(F32) + b_ref[...].astype(F32)).astype(o_ref.dtype)

    grid_spec = pltpu.PrefetchScalarGridSpec(
        num_scalar_prefetch=1, grid=(N, R // rb),
        in_specs=[pl.BlockSpec((None, None, rb, C), lambda p, i, c: (p, c[0], i, 0)),
                  pl.BlockSpec((None, None, rb, C), lambda p, i, c: (1 - c[0], p, i, 0))],
        out_specs=pl.BlockSpec((None, rb, C), lambda p, i, c: (p, i, 0)))
    return _pcall(body, name=name, grid_spec=grid_spec, out_shape=SDS((N, R, C), out_dtype))(
        _scalar(ic), own, recv)


def _sum_landed(pair_sum, ax, land, chip, name):
    _, R, Cb = land.shape
    rb = _rows_block(R, Cb)
    if ax == 0:
        own_spec = pl.BlockSpec((None, rb, Cb), lambda i, c: (c[0], i, 0))
    else:
        own_spec = pl.BlockSpec((rb, Cb), lambda i, c: (i, c[0]))
    slot = lambda d: pl.BlockSpec((None, rb, Cb), lambda i, c: ((c[0] + d) % 4, i, 0))

    def body(c_ref, own_ref, r1, r2, r3, o_ref):
        del c_ref
        o_ref[...] = ((own_ref[...].astype(F32) + r1[...].astype(F32)) + r2[...].astype(F32)) + r3[...].astype(F32)

    grid_spec = pltpu.PrefetchScalarGridSpec(
        num_scalar_prefetch=1, grid=(R // rb,), in_specs=[own_spec, slot(1), slot(2), slot(3)],
        out_specs=pl.BlockSpec((rb, Cb), lambda i, c: (i, 0)))
    return _pcall(body, name=name, grid_spec=grid_spec, out_shape=SDS((R, Cb), F32))(
        _scalar(chip), pair_sum, land, land, land)


def _adamw_step(g, w_ref, m_ref, v_ref, g_ref, d_ref, mo_ref, vo_ref):
    m_new = ADAM_B1 * m_ref[...] + (1.0 - ADAM_B1) * g
    v_new = ADAM_B2 * v_ref[...] + (1.0 - ADAM_B2) * (g * g)
    m_hat = m_new / (1.0 - ADAM_B1 ** ADAM_STEP)
    v_hat = v_new / (1.0 - ADAM_B2 ** ADAM_STEP)
    g_ref[...] = g
    d_ref[...] = -ADAM_LR * (m_hat / (jnp.sqrt(v_hat) + ADAM_EPS) + ADAM_WD * w_ref[...])
    mo_ref[...] = m_new
    vo_ref[...] = v_new


def _adamw(w, m, v, st, name):
    R, C = w.shape
    n = st.shape[0]
    rb = _rows_block(R, C)
    spec = pl.BlockSpec((rb, C), lambda i: (i, 0))

    def body(w_ref, m_ref, v_ref, s_ref, *o_refs):
        g = s_ref[0]
        for q in range(1, n):
            g = g + s_ref[q]
        _adamw_step(g, w_ref, m_ref, v_ref, *o_refs)

    return _pcall(body, name=name, grid=(R // rb,),
                  in_specs=[spec, spec, spec, pl.BlockSpec((n, rb, C), lambda i: (0, i, 0))],
                  out_specs=(spec,) * 4, out_shape=(SDS((R, C), F32),) * 4)(w, m, v, st)


def _adamw_halves(w, m, v, layer, own, recv2, ic, bufs, name):
    L, R, C = w.shape
    rb = _rows_block(R // 2, C)
    nb = R // 2 // rb
    spec = pl.BlockSpec((None, rb, C), lambda hb, i, c: (layer, hb * nb + i, 0))

    def body(c_ref, w_ref, m_ref, v_ref, own_ref, recv_ref, *rest):
        is_own = pl.program_id(0) == c_ref[0]
        _adamw_step(jnp.where(is_own, own_ref[...], recv_ref[...]), w_ref, m_ref, v_ref, *rest[-4:])

    nbuf = 0 if bufs is None else 4
    grid_spec = pltpu.PrefetchScalarGridSpec(
        num_scalar_prefetch=1, grid=(2, nb),
        in_specs=[spec, spec, spec, pl.BlockSpec((rb, C), lambda hb, i, c: (i, 0)),
                  pl.BlockSpec((None, rb, C), lambda hb, i, c: (1 - c[0], i, 0))] + [_ANY] * nbuf,
        out_specs=(spec,) * 4)
    extra = {} if bufs is None else dict(input_output_aliases={6 + q: q for q in range(4)})
    return _pcall(body, name=name, grid_spec=grid_spec, out_shape=(SDS((L, R, C), F32),) * 4, **extra)(
        _scalar(ic), w, m, v, own, recv2, *([] if bufs is None else bufs))


def _pack(arrs, rows=1):
    flat = jnp.concatenate([a.reshape(-1).astype(F32) for a in arrs])
    quantum = rows * LANES
    pad = (-flat.shape[0]) % quantum
    flat = jnp.pad(flat, (0, pad))
    return flat.reshape(rows, -1)


def _unpack(flat, shapes):
    flat = flat.reshape(-1)
    out, off = [], 0
    for shp in shapes:
        size = 1
        for d in shp:
            size *= d
        out.append(flat[off:off + size].reshape(shp))
        off += size
    return out


def _mlp_fwd(x1, mod, lnp, w1, w2, tag):
    h2 = _modulate(x1, mod, 3, 4, f"{tag}_mod")
    a1, a2 = _mm(h2, w1, "nn", name=f"{tag}_up", out_dtypes=(_MXU, _MXU),
                 out_fn=lambda r: (r, jnp.square(jnp.maximum(r, 0.0))))
    y2 = _mm(a2, w2, "nn", name=f"{tag}_down")
    x2 = _combine(x1, y2, mod, 5, lnp, 2, f"{tag}_ln")
    return x2, (x1, h2, a1, a2, y2)


def _weight_grad(grads, key, a, b, name):
    grads[key] = _mm(a, b, "tn", name=name)


def _mlp_bwd(dx2, saved, mod, lnp, w1, w2, tag, stacks):
    x1, h2, a1, a2, y2 = saved
    dxa, dy2, dp = _combine_bwd(x1, y2, mod, 5, lnp, 2, dx2, f"{tag}_ln_b")
    da1 = _mm(dy2, w2, "nt", name=f"{tag}_down_bx", out_dtypes=(_MXU,), aux=[(a1, "mn")],
              out_fn=lambda r, a: (r * (2.0 * jnp.maximum(a.astype(F32), 0.0)),))
    _weight_grad(stacks, "ff_w2", a2, dy2, f"{tag}_down_bw")
    _weight_grad(stacks, "ff_w1", h2, da1, f"{tag}_up_bw")
    dh2 = _mm(da1, w1, "nt", name=f"{tag}_up_bx")
    dx1, dss = _modulate_bwd(x1, mod, 4, dh2, dxa, f"{tag}_mod_b")
    return dx1, (dss, dp)


def _dn_fwd(x, mod, lnp, wts, H, tag):
    w_main, w_small, conv_w, prm, nw, w_out = wts
    h = _modulate(x, mod, 0, 1, f"{tag}_mod")
    pm = _mm(h, w_main, "nn", name=f"{tag}_in")
    ps = _mm(h, w_small, "nn", name=f"{tag}_in_s")
    nqkv = conv_w.shape[1] // LANES
    qkv = _conv_silu(pm, conv_w, nqkv, f"{tag}_conv")
    gates = _gates(ps, prm, H, f"{tag}_gates")
    og, *states = _delta_fwd(qkv, pm, gates, nw, H, f"{tag}_delta")
    y = _mm(og, w_out, "nn", name=f"{tag}_out")
    x1 = _combine(x, y, mod, 2, lnp, 0, f"{tag}_ln")
    return x1, (x, h, pm, ps, qkv, gates, states, og, y)


def _dn_bwd(dx1, saved, mod, lnp, wts, H, tag, stacks):
    w_main, w_small, conv_w, prm, nw, w_out = wts
    x, h, pm, ps, qkv, gates, states, og, y = saved
    dxa, dy, dp = _combine_bwd(x, y, mod, 2, lnp, 0, dx1, f"{tag}_ln_b")
    dog = _mm(dy, w_out, "nt", name=f"{tag}_out_bx")
    _weight_grad(stacks, "dn_w_out", og, dy, f"{tag}_out_bw")
    dq, dk, dv, dpm, dgates, dnw = _delta_bwd(qkv, pm, gates, nw, *states, dog, H, f"{tag}_delta_b")
    dps, dprm = _gates_bwd(ps, prm, dgates, H, f"{tag}_gates_b")
    dcw = []
    nb = dq.shape[1] // LANES
    for part, dpart in enumerate((dq, dk, dv)):
        dpm, dcw_p = _conv_silu_bwd(pm, conv_w, dpart, dpm, part * nb, f"{tag}_conv_b{part}")
        dcw.append(dcw_p)
    dconv_w = jnp.concatenate(dcw, axis=1)
    dw_main = _mm(h, dpm, "tn", name=f"{tag}_in_bw")
    dw_small = _mm(h, dps, "tn", name=f"{tag}_in_s_bw")
    dh_s = _mm(dps, w_small, "nt", name=f"{tag}_in_s_bx")
    dh = _mm(dpm, w_main, "nt", name=f"{tag}_in_bx", aux=[(dh_s, "mn")], out_fn=lambda r, e: (r + e,))
    dx, dss = _modulate_bwd(x, mod, 1, dh, dxa, f"{tag}_mod_b")
    return dx, (dw_main, dw_small, dconv_w, dprm, dnw), (dss, dp)


def _cf_fwd(x, mod, lnp, wts, tag):
    w_in, dw_w, dw_b, cln, w_out = wts
    h = _modulate(x, mod, 0, 1, f"{tag}_mod")
    p = _mm(h, w_in, "nn", name=f"{tag}_in")
    u2 = _glu_conv(p, dw_w, dw_b, f"{tag}_conv")
    u3 = _ln_silu(u2, cln, f"{tag}_cln")
    y = _mm(u3, w_out, "nn", name=f"{tag}_out")
    x1 = _combine(x, y, mod, 2, lnp, 0, f"{tag}_ln")
    return x1, (x, h, p, u2, u3, y)


def _cf_bwd(dx1, saved, mod, lnp, wts, tag, stacks):
    w_in, dw_w, dw_b, cln, w_out = wts
    x, h, p, u2, u3, y = saved
    dxa, dy, dp = _combine_bwd(x, y, mod, 2, lnp, 0, dx1, f"{tag}_ln_b")
    du3 = _mm(dy, w_out, "nt", name=f"{tag}_out_bx")
    _weight_grad(stacks, "cf_w_out", u3, dy, f"{tag}_out_bw")
    du2, dcln = _ln_silu_bwd(u2, cln, du3, f"{tag}_cln_b")
    dval, dgate, ddw_w, ddw_b = _glu_conv_bwd(p, dw_w, du2, f"{tag}_conv_b")
    dpp = jnp.concatenate([dval, dgate], axis=1)
    _weight_grad(stacks, "cf_w_in", h, dpp, f"{tag}_in_bw")
    dh = _mm(dpp, w_in, "nt", name=f"{tag}_in_bx")
    dx, dss = _modulate_bwd(x, mod, 1, dh, dxa, f"{tag}_mod_b")
    return dx, (ddw_w, ddw_b, dcln), (dss, dp)


def _two_d(a):
    return a.reshape(-1, a.shape[-1])


def kernel(x, c, ada_w, ada_b, ln_g, ln_b, dn_w_in, dn_conv_w, dn_a_log, dn_dt_bias, dn_norm_w, dn_w_out, cf_w_in, cf_dw_w, cf_dw_b, cf_ln_g, cf_ln_b, cf_w_out, ff_w1, ff_w2, loss_target, m_ada_w, m_ada_b, m_ln_g, m_ln_b, m_dn_w_in, m_dn_conv_w, m_dn_a_log, m_dn_dt_bias, m_dn_norm_w, m_dn_w_out, m_cf_w_in, m_cf_dw_w, m_cf_dw_b, m_cf_ln_g, m_cf_ln_b, m_cf_w_out, m_ff_w1, m_ff_w2, v_ada_w, v_ada_b, v_ln_g, v_ln_b, v_dn_w_in, v_dn_conv_w, v_dn_a_log, v_dn_dt_bias, v_dn_norm_w, v_dn_w_out, v_cf_w_in, v_cf_dw_w, v_cf_dw_b, v_cf_ln_g, v_cf_ln_b, v_cf_w_out, v_ff_w1, v_ff_w2):
    ix, iy, ic = lax.axis_index("x"), lax.axis_index("y"), lax.axis_index("c")
    chip = 2 * ix + iy
    dev = 4 * ix + 2 * iy + ic
    S, D = x.shape[1], x.shape[2]
    L = ada_w.shape[0]
    LA, LB = dn_w_in.shape[0], cf_w_in.shape[0]
    H = dn_a_log.shape[1]
    NMOD = ada_b.shape[1] // D
    dn_in = dn_w_in.shape[2] * 4
    n_main = dn_in - 2 * H
    assert L == N_LAYERS and 2 * H <= LANES
    x0, tgt = x[0], loss_target[0]

    small_sharded = [ln_g, ln_b, dn_conv_w, cf_dw_w, cf_dw_b, cf_ln_g, cf_ln_b]
    small_axes = [2, 2, 2, 2, 1, 1, 1]
    packed_small = _pack(small_sharded, rows=8)[None]
    big = {"dn_w_in": (dn_w_in, m_dn_w_in, v_dn_w_in, 0), "dn_w_out": (dn_w_out, m_dn_w_out, v_dn_w_out, 0),
           "cf_w_in": (cf_w_in, m_cf_w_in, v_cf_w_in, 1), "cf_w_out": (cf_w_out, m_cf_w_out, v_cf_w_out, 0),
           "ff_w1": (ff_w1, m_ff_w1, v_ff_w1, 1), "ff_w2": (ff_w2, m_ff_w2, v_ff_w2, 0)}

    def group(l):
        mixer = ("dn_w_in", "dn_w_out") if l % 2 == 0 else ("cf_w_in", "cf_w_out")
        return {mixer[0]: l // 2, mixer[1]: l // 2, "ff_w1": l, "ff_w2": l}

    def start_gather(l, after):
        names = list(group(l))
        axes = [big[nm][3] for nm in names]
        placed = [_place_block(big[nm][0], group(l)[nm], ax, chip, f"l{l}_place_{nm}") for nm, ax in zip(names, axes)]
        send, recv, arrs, token = _gather_start(list(zip(placed, axes)), after, f"l{l}_gather_start")
        return names, axes, arrs, send, recv, token

    def finish_gather(l, pending, after):
        names, axes, arrs, send, recv, _ = pending
        arrs = _gather_wait(list(zip(arrs, axes)), send, recv, after, f"l{l}_gather_wait")
        arrs = _gather_forward(list(zip(arrs, axes)), f"l{l}_gather_pass")
        return dict(zip(names, arrs))

    g_small = _exchange([(packed_small, 0)], "xy", "gather", "gather_small")[0]
    shard_shapes = [a.shape for a in small_sharded]
    per_chip = [_unpack(g_small[q], shard_shapes) for q in range(4)]
    ln_g_f, ln_b_f, conv_w_f, dw_w_f, dw_b_f, cln_g_f, cln_b_f = [
        jnp.concatenate([per_chip[q][i] for q in range(4)], axis=small_axes[i]) for i in range(len(small_sharded))]

    c_all = _exchange([(c[None], 0)], "all", "gather", "gather_cond")[0].reshape(8, D)
    c_pad = jnp.pad(c_all, ((0, 8), (0, 0)))
    mod_sh = jnp.stack([_mm(c_pad, (ada_w, l), "nn", name=f"ada_{l}", a_fn=lambda t: t * _sigmoid(t))
                        for l in range(L)])
    mod_all = _exchange([(mod_sh, 2)], "xy", "gather", "gather_mod")[0]
    mod_mine = lax.dynamic_index_in_dim(mod_all, dev, axis=1, keepdims=False) + ada_b
    mods = mod_mine.reshape(L, NMOD, D)

    def lnp_of(l):
        return jnp.stack([ln_g_f[l, 0], ln_b_f[l, 0], ln_g_f[l, 1], ln_b_f[l, 1]])

    def mixer_wts(l, wl):
        j = l // 2
        if l % 2 == 0:
            w_in = jnp.transpose(wl["dn_w_in"].reshape(4, D, dn_in // 4), (1, 0, 2)).reshape(D, dn_in)
            w_small = jnp.pad(w_in[:, n_main:], ((0, 0), (0, LANES - 2 * H)))
            prm = jnp.zeros((2, LANES), F32).at[0, H:2 * H].set(dn_a_log[j]).at[1, H:2 * H].set(dn_dt_bias[j])
            return (w_in[:, :n_main], w_small, conv_w_f[j], prm, dn_norm_w[j][None], wl["dn_w_out"])
        return (wl["cf_w_in"], dw_w_f[j], dw_b_f[j][None], jnp.stack([cln_g_f[j], cln_b_f[j]]), wl["cf_w_out"])

    xs = x0
    saved, wts, mod_of = [], [], []
    pending = start_gather(0, x0)
    for l in range(L):
        wl = finish_gather(l, pending, xs)
        mod_l = mods[l]
        if l + 1 < L:
            pending = start_gather(l + 1, wl["ff_w2"])
            mod_l = mod_l + pending[5][0, 0]
        wts.append((mixer_wts(l, wl), wl["ff_w1"], wl["ff_w2"]))
        mod_of.append(mod_l)
        if l % 2 == 0:
            xs, sv_a = _dn_fwd(xs, mod_l, lnp_of(l), wts[l][0], H, f"l{l}_dn")
        else:
            xs, sv_a = _cf_fwd(xs, mod_l, lnp_of(l), wts[l][0], f"l{l}_cf")
        xs, sv_b = _mlp_fwd(xs, mod_l, lnp_of(l), wts[l][1], wts[l][2], f"l{l}_ff")
        saved.append((sv_a, sv_b))
    dx, loss_local = _loss_head(xs, tgt, "loss_head")
    loss = lax.psum(loss_local[0, 0], ("x", "y", "c"))

    def start_scatter(l, grads):
        names = list(group(l))
        axes = [big[nm][3] for nm in names]
        own = [grads[nm].reshape((4, 2, -1, grads[nm].shape[1]) if ax == 0 else (1, 2, -1, grads[nm].shape[1]))
               for nm, ax in zip(names, axes)]
        pair = _exchange([(o, 1) for o in own], "c", "scatter", f"l{l}_pair_grads", keep_own=False)
        sums = []
        for nm, ax, o, r in zip(names, axes, own, pair):
            s = _sum_pair(o, r.reshape((2, o.shape[0]) + o.shape[2:]), ic, f"l{l}_sum_pair_{nm}", _MXU)
            sums.append(s if ax == 0 else s[0])
        send, recv, sums, lands, token = _scatter_start(list(zip(sums, axes)), mods, f"l{l}_scatter_start")
        return names, axes, sums, lands, send, recv, token

    def finish_scatter(l, pending, after):
        names, axes, sums, lands, send, recv, _ = pending
        sums, lands = _scatter_wait(list(zip(sums, axes)), lands, send, recv, after, f"l{l}_scatter_wait")
        mine = [_sum_landed(s, ax, ld, chip, f"l{l}_sum_grads_{nm}") for nm, ax, s, ld in zip(names, axes, sums, lands)]
        other = _exchange([(s[None], 0) for s in mine], "c", "gather", f"l{l}_swap_sums", keep_own=False)
        return {nm: (s, o) for nm, s, o in zip(names, mine, other)}

    g_dn = [None] * LA
    g_cf = [None] * LB
    dmods, dlns = [None] * L, [None] * L
    big_sums = {}
    pending, token = None, None
    for l in reversed(range(L)):
        j = l // 2
        sv_a, sv_b = saved[l]
        mod_l = mod_of[l] if token is None else mod_of[l] + token[0, 0]
        grads = {}
        dx, (dss2, dp2) = _mlp_bwd(dx, sv_b, mod_l, lnp_of(l), wts[l][1], wts[l][2], f"l{l}_ff", grads)
        if l % 2 == 0:
            dx, g_dn[j], (dss1, dp1) = _dn_bwd(dx, sv_a, mod_l, lnp_of(l), wts[l][0], H, f"l{l}_dn", grads)
            dn_in_g = jnp.concatenate([g_dn[j][0], g_dn[j][1][:, :2 * H]], axis=1)
            grads["dn_w_in"] = jnp.transpose(dn_in_g.reshape(D, 4, dn_in // 4), (1, 0, 2)).reshape(4 * D, dn_in // 4)
        else:
            dx, g_cf[j], (dss1, dp1) = _cf_bwd(dx, sv_a, mod_l, lnp_of(l), wts[l][0], f"l{l}_cf", grads)
        dmods[l] = jnp.concatenate([dss1, dp1[0:1], dss2, dp2[0:1]], axis=0)
        dlns[l] = (jnp.stack([dp1[1], dp2[1]]), jnp.stack([dp1[2], dp2[2]]))
        if pending is not None:
            for nm, pair_of in finish_scatter(l + 1, pending, dx).items():
                big_sums[nm, group(l + 1)[nm]] = pair_of
        pending = start_scatter(l, grads)
        token = pending[6]
    for nm, pair_of in finish_scatter(0, pending, dx).items():
        big_sums[nm, group(0)[nm]] = pair_of
    grad_x = dx[None]

    d_ln_g = jnp.stack([dlns[l][0] for l in range(L)])
    d_ln_b = jnp.stack([dlns[l][1] for l in range(L)])
    d_conv_w = jnp.stack([g_dn[j][2] for j in range(LA)])
    d_a_log = jnp.stack([g_dn[j][3][0, H:2 * H] for j in range(LA)])
    d_dt_bias = jnp.stack([g_dn[j][3][1, H:2 * H] for j in range(LA)])
    d_norm_w = jnp.stack([g_dn[j][4][0] for j in range(LA)])
    d_dw_w = jnp.stack([g_cf[j][0] for j in range(LB)])
    d_dw_b = jnp.stack([g_cf[j][1][0] for j in range(LB)])
    d_cln_g = jnp.stack([g_cf[j][2][0] for j in range(LB)])
    d_cln_b = jnp.stack([g_cf[j][2][1] for j in range(LB)])
    d_mod = jnp.stack(dmods).reshape(L, NMOD * D)
    small_full = [d_mod, d_ln_g, d_ln_b, d_conv_w, d_dw_w, d_dw_b, d_cln_g, d_cln_b, d_a_log, d_dt_bias, d_norm_w]
    small_all = _exchange([(_pack(small_full, rows=8)[None], 0)], "all", "gather", "gather_small_grads")[0]
    small_sum = _sum_slots(small_all, "sum_small_grads")
    (s_ada_b, s_ln_g, s_ln_b, s_conv_w, s_dw_w, s_dw_b, s_cln_g, s_cln_b, s_a_log, s_dt_bias, s_norm_w) = _unpack(
        small_sum, [a.shape for a in small_full])
    d_mod_all = small_all.reshape(8, -1)[:, :L * NMOD * D].reshape(8, L, NMOD * D)

    def shard(a, axis):
        size = a.shape[axis] // 4
        return lax.dynamic_slice_in_dim(a, chip * size, size, axis)

    ncol = ada_w.shape[2]
    d_mod_sh = jnp.pad(lax.dynamic_slice_in_dim(d_mod_all, chip * ncol, ncol, 2), ((0, 8), (0, 0), (0, 0)))
    g_ada_w = jnp.stack([_mm(c_pad, d_mod_sh[:, l], "tn", name=f"ada_bw_{l}", a_fn=lambda t: t * _sigmoid(t))
                         for l in range(L)])

    big_out = []
    for nm in ["dn_w_in", "dn_w_out", "cf_w_in", "cf_w_out", "ff_w1", "ff_w2"]:
        w, m, v, _ = big[nm]
        res = None
        for layer in range(w.shape[0]):
            own, oth = big_sums[nm, layer]
            res = _adamw_halves(w, m, v, layer, own, oth, ic, res, f"adamw_{nm}_{layer}")
        big_out.append(res)
    ada_out = [r.reshape(ada_w.shape) for r in _adamw(_two_d(ada_w), _two_d(m_ada_w), _two_d(v_ada_w),
                                                     _two_d(g_ada_w)[None], "adamw_ada_w")]

    small_w = [(ada_b, m_ada_b, v_ada_b, s_ada_b), (ln_g, m_ln_g, v_ln_g, shard(s_ln_g, 2)),
               (ln_b, m_ln_b, v_ln_b, shard(s_ln_b, 2)), (dn_conv_w, m_dn_conv_w, v_dn_conv_w, shard(s_conv_w, 2)),
               (dn_a_log, m_dn_a_log, v_dn_a_log, s_a_log), (dn_dt_bias, m_dn_dt_bias, v_dn_dt_bias, s_dt_bias),
               (dn_norm_w, m_dn_norm_w, v_dn_norm_w, s_norm_w), (cf_dw_w, m_cf_dw_w, v_cf_dw_w, shard(s_dw_w, 2)),
               (cf_dw_b, m_cf_dw_b, v_cf_dw_b, shard(s_dw_b, 1)), (cf_ln_g, m_cf_ln_g, v_cf_ln_g, shard(s_cln_g, 1)),
               (cf_ln_b, m_cf_ln_b, v_cf_ln_b, shard(s_cln_b, 1))]
    pk = [_pack([t[i] for t in small_w], rows=8) for i in range(4)]
    small_res = _adamw(pk[0], pk[1], pk[2], pk[3][None], "adamw_small")
    small_shapes = [t[0].shape for t in small_w]
    small_out = [_unpack(r, small_shapes) for r in small_res]

    def kind(k):
        sm = small_out[k]
        bg = [o[k] for o in big_out]
        return [ada_out[k], sm[0], sm[1], sm[2], bg[0], sm[3], sm[4], sm[5], sm[6], bg[1],
                bg[2], sm[7], sm[8], sm[9], sm[10], bg[3], bg[4], bg[5]]

    return (loss, grad_x, *kind(0), *kind(1), *kind(2), *kind(3))
```

```python
import functools

import jax
import jax.numpy as jnp
from jax import lax
from jax.experimental import pallas as pl
from jax.experimental.pallas import tpu as pltpu

F32 = jnp.float32
_MXU = jnp.bfloat16
_HI = lax.Precision.HIGHEST

N_LAYERS = 4
ALPHA = (2.0 * N_LAYERS) ** 0.25
LN_EPS = 1e-5
RMS_EPS = 1e-6
L2_EPS = 1e-6
CHUNK = 64
ADAM_LR, ADAM_B1, ADAM_B2, ADAM_EPS, ADAM_WD, ADAM_STEP = 0.001, 0.9, 0.999, 1e-08, 0.01, 10

LANES = 128
TOKEN_BLOCK = 256
VMEM_BIG = 48 * 1024 * 1024
MM_TILE = 1024

SDS = jax.ShapeDtypeStruct
MESH = pl.DeviceIdType.MESH


def _cparams(vmem=None):
    if vmem is None:
        return None
    return pltpu.CompilerParams(vmem_limit_bytes=vmem)


def _pcall(body, **kw):
    if kw.get("compiler_params", 1) is None:
        kw.pop("compiler_params")
    return pl.pallas_call(body, **kw)


def _full(arr):
    nd = arr.ndim
    return pl.BlockSpec(arr.shape, lambda *g: (0,) * nd)


def _bs(block, imap, lead=None):
    if lead is None:
        return pl.BlockSpec(block, imap)
    return pl.BlockSpec((None,) + tuple(block), lambda *g: (lead,) + tuple(imap(*g)))


def _split(a):
    return a if isinstance(a, tuple) else (a, None)


_GROUPS = {
    "xy": ([(1, 0, 0), (0, 1, 0), (1, 1, 0)], 4),
    "c": ([(0, 0, 1)], 2),
    "all": ([(1, 0, 0), (0, 1, 0), (1, 1, 0), (0, 0, 1), (1, 0, 1), (0, 1, 1), (1, 1, 1)], 8),
}


def _exchange(items, group, mode, name, nsplit=1, keep_own=True):
    masks, n = _GROUPS[group]
    npeer = len(masks)
    ni = len(items)
    arrs = [a for a, _ in items]
    out_shapes = []
    for a, ax in items:
        shp = list(a.shape)
        if mode == "gather":
            shp[ax] *= n
        else:
            shp[ax] //= n
            shp = [n] + shp
        out_shapes.append(SDS(tuple(shp), a.dtype))

    def body(*refs):
        ins, outs = refs[:ni], refs[ni:2 * ni]
        send_sems, recv_sems, local_sems = refs[2 * ni:]
        x, y, c = lax.axis_index("x"), lax.axis_index("y"), lax.axis_index("c")

        def slot(px, py, pc):
            if group == "xy":
                return 2 * px + py
            if group == "c":
                return pc
            return 4 * px + 2 * py + pc

        me = slot(x, y, c)

        def block(ref, ax, idx, size):
            ix = (slice(None),) * ax + (pl.ds(pl.multiple_of(idx * size, size), size),)
            return ref.at[ix]

        copies = []
        for it, (a, ax) in enumerate(items):
            in_ref, out_ref = ins[it], outs[it]
            if mode == "gather":
                size = a.shape[ax]
                src_own, dst_own = in_ref, block(out_ref, ax, me, size)
            else:
                size = a.shape[ax] // n
                src_own, dst_own = block(in_ref, ax, me, size), out_ref.at[me]
            sax, ns, cs = splits[it]
            pieces = [(slice(None),) * sax + (pl.ds(j * cs, cs),) for j in range(ns)]
            if keep_own:
                for j, piece in enumerate(pieces):
                    own = pltpu.make_async_copy(src_own.at[piece], dst_own.at[piece], local_sems.at[it * nsplit + j])
                    own.start()
                    copies.append(own)
            for k, m in enumerate(masks):
                peer = tuple((1 - v) if b else v for v, b in zip((x, y, c), m))
                if mode == "gather":
                    src, dst = in_ref, dst_own
                else:
                    src, dst = block(in_ref, ax, slot(*peer), size), out_ref.at[me]
                for j, piece in enumerate(pieces):
                    sem = (it * npeer + k) * nsplit + j
                    cp = pltpu.make_async_remote_copy(
                        src_ref=src.at[piece], dst_ref=dst.at[piece], send_sem=send_sems.at[sem],
                        recv_sem=recv_sems.at[sem], device_id=peer, device_id_type=MESH)
                    cp.start()
                    copies.append(cp)
        for cp in copies:
            cp.wait()

    splits = []
    for (a, ax), o in zip(items, out_shapes):
        bshape = a.shape if mode == "gather" else o.shape[1:]
        sax = max(range(len(bshape) - 1), key=lambda d: bshape[d])
        ns = nsplit if bshape[sax] % (nsplit * 16) == 0 else 1
        splits.append((sax, ns, bshape[sax] // ns))
    any_spec = pl.BlockSpec(memory_space=pl.ANY)
    nsem = ni * npeer * nsplit
    outs = _pcall(
        body, name=name, out_shape=tuple(out_shapes),
        in_specs=[any_spec] * ni, out_specs=tuple([any_spec] * ni),
        scratch_shapes=[pltpu.SemaphoreType.DMA((nsem,)), pltpu.SemaphoreType.DMA((nsem,)),
                        pltpu.SemaphoreType.DMA((ni * nsplit,))],
    )(*arrs)
    return list(outs)


_HBM = pl.BlockSpec(memory_space=pltpu.HBM)
_SEM = pl.BlockSpec(memory_space=pltpu.SEMAPHORE)
_ANY = pl.BlockSpec(memory_space=pl.ANY)
_SPLIT = pltpu.CompilerParams(has_side_effects=pltpu.SideEffectType.DATAFLOW_SIDE_EFFECTING)
_XY = _GROUPS["xy"][0]


def _in_hbm(a):
    return pltpu.with_memory_space_constraint(a, pltpu.HBM)


def _chip_peers():
    x, y, c = lax.axis_index("x"), lax.axis_index("y"), lax.axis_index("c")
    return x, y, c, [tuple((1 - v) if b else v for v, b in zip((x, y), m)) for m in _XY]


def _wblock(ref, ax, half, chip):
    R, C = ref.shape
    if ax == 0:
        rows = R // 8
        return ref.at[pl.ds(pl.multiple_of(chip * (2 * rows) + half * rows, rows), rows), :]
    rows, cols = R // 2, C // 4
    return ref.at[pl.ds(pl.multiple_of(half * rows, rows), rows), pl.ds(pl.multiple_of(chip * cols, cols), cols)]


def _gather_start(items, after, name):
    ni = len(items)
    arrs = [a for a, _ in items]

    def body(*refs):
        send_sems, recv_sems = refs[ni + 1], refs[ni + 2]
        outs, token = refs[ni + 3:2 * ni + 3], refs[2 * ni + 3]
        x, y, c, peers = _chip_peers()
        for it, (_, ax) in enumerate(items):
            mine = _wblock(outs[it], ax, c, 2 * x + y)
            for k, (px, py) in enumerate(peers):
                pltpu.make_async_remote_copy(
                    src_ref=mine, dst_ref=mine, send_sem=send_sems.at[it * 3 + k], recv_sem=recv_sems.at[it * 3 + k],
                    device_id=(px, py, c), device_id_type=MESH).start()
        token[...] = jnp.zeros_like(token)

    res = _pcall(
        body, name=name,
        out_shape=(pltpu.SemaphoreType.DMA((ni * 3,)), pltpu.SemaphoreType.DMA((ni * 3,)),
                   *[pltpu.HBM(a.shape, a.dtype) for a in arrs], SDS((8, LANES), F32)),
        in_specs=[_HBM] * ni + [_ANY],
        out_specs=(_SEM, _SEM, *[_HBM] * ni, pl.BlockSpec(memory_space=pltpu.VMEM)),
        input_output_aliases={i: 2 + i for i in range(ni)}, compiler_params=_SPLIT,
    )(*[_in_hbm(a) for a in arrs], after)
    return res[0], res[1], list(res[2:2 + ni]), res[2 + ni]


def _gather_wait(items, send_sems, recv_sems, after, name):
    ni = len(items)
    arrs = [a for a, _ in items]

    def body(*refs):
        s_sems, r_sems = refs[ni], refs[ni + 1]
        outs = refs[ni + 3:]
        x, y, c, peers = _chip_peers()
        for it, (_, ax) in enumerate(items):
            mine = _wblock(outs[it], ax, c, 2 * x + y)
            for k, (px, py) in enumerate(peers):
                cp = pltpu.make_async_remote_copy(
                    src_ref=mine, dst_ref=_wblock(outs[it], ax, c, 2 * px + py), send_sem=s_sems.at[it * 3 + k],
                    recv_sem=r_sems.at[it * 3 + k], device_id=(px, py, c), device_id_type=MESH)
                cp.wait_send()
                cp.wait_recv()

    res = _pcall(
        body, name=name, out_shape=tuple(pltpu.HBM(a.shape, a.dtype) for a in arrs),
        in_specs=[_HBM] * ni + [_SEM, _SEM, _ANY], out_specs=tuple([_HBM] * ni),
        input_output_aliases={i: i for i in range(ni)}, compiler_params=_SPLIT,
    )(*arrs, send_sems, recv_sems, after)
    return list(res)


def _gather_forward(items, name):
    ni = len(items)
    arrs = [a for a, _ in items]

    def body(*refs):
        outs = refs[ni:2 * ni]
        send_sems, recv_sems = refs[2 * ni:]
        x, y, c, peers = _chip_peers()
        def copy(it, ax, k, chip, dst_half):
            return pltpu.make_async_remote_copy(
                src_ref=_wblock(outs[it], ax, c, chip), dst_ref=_wblock(outs[it], ax, dst_half, chip),
                send_sem=send_sems.at[it * 3 + k], recv_sem=recv_sems.at[it * 3 + k],
                device_id=(x, y, 1 - c), device_id_type=MESH)

        for it, (_, ax) in enumerate(items):
            for k, (px, py) in enumerate(peers):
                copy(it, ax, k, 2 * px + py, c).start()
        for it, (_, ax) in enumerate(items):
            for k, (px, py) in enumerate(peers):
                copy(it, ax, k, 2 * px + py, c).wait_send()
                copy(it, ax, k, 2 * px + py, 1 - c).wait_recv()

    res = _pcall(
        body, name=name, out_shape=tuple(SDS(a.shape, a.dtype) for a in arrs),
        in_specs=[_ANY] * ni, out_specs=tuple([_ANY] * ni), input_output_aliases={i: i for i in range(ni)},
        scratch_shapes=[pltpu.SemaphoreType.DMA((ni * 3,)), pltpu.SemaphoreType.DMA((ni * 3,))],
    )(*arrs)
    return list(res)


def _gblock(ref, ax, chip):
    if ax == 0:
        return ref.at[chip]
    cols = ref.shape[1] // 4
    return ref.at[:, pl.ds(pl.multiple_of(chip * cols, cols), cols)]


def _scatter_start(items, after, name):
    ni = len(items)
    arrs = [a for a, _ in items]
    lands = []
    for a, ax in items:
        blk = a.shape[1:] if ax == 0 else (a.shape[0], a.shape[1] // 4)
        lands.append(lax.empty((4, *blk), a.dtype))

    def body(*refs):
        send_sems, recv_sems = refs[2 * ni + 1], refs[2 * ni + 2]
        srcs, dsts = refs[2 * ni + 3:3 * ni + 3], refs[3 * ni + 3:4 * ni + 3]
        token = refs[4 * ni + 3]
        x, y, c, peers = _chip_peers()
        for it, (_, ax) in enumerate(items):
            for k, (px, py) in enumerate(peers):
                pltpu.make_async_remote_copy(
                    src_ref=_gblock(srcs[it], ax, 2 * px + py), dst_ref=dsts[it].at[2 * x + y],
                    send_sem=send_sems.at[it * 3 + k], recv_sem=recv_sems.at[it * 3 + k],
                    device_id=(px, py, c), device_id_type=MESH).start()
        token[...] = jnp.zeros_like(token)

    res = _pcall(
        body, name=name,
        out_shape=(pltpu.SemaphoreType.DMA((ni * 3,)), pltpu.SemaphoreType.DMA((ni * 3,)),
                   *[pltpu.HBM(a.shape, a.dtype) for a in arrs], *[pltpu.HBM(a.shape, a.dtype) for a in lands],
                   SDS((8, LANES), F32)),
        in_specs=[_HBM] * (2 * ni) + [_ANY],
        out_specs=(_SEM, _SEM, *[_HBM] * (2 * ni), pl.BlockSpec(memory_space=pltpu.VMEM)),
        input_output_aliases={i: 2 + i for i in range(2 * ni)}, compiler_params=_SPLIT,
    )(*[_in_hbm(a) for a in arrs], *[_in_hbm(a) for a in lands], after)
    return res[0], res[1], list(res[2:2 + ni]), list(res[2 + ni:2 + 2 * ni]), res[2 + 2 * ni]


def _scatter_wait(items, lands, send_sems, recv_sems, after, name):
    ni = len(items)
    arrs = [a for a, _ in items]

    def body(*refs):
        s_sems, r_sems = refs[2 * ni], refs[2 * ni + 1]
        srcs, dsts = refs[2 * ni + 3:3 * ni + 3], refs[3 * ni + 3:]
        x, y, c, peers = _chip_peers()
        for it, (_, ax) in enumerate(items):
            for k, (px, py) in enumerate(peers):
                cp = pltpu.make_async_remote_copy(
                    src_ref=_gblock(srcs[it], ax, 2 * px + py), dst_ref=dsts[it].at[2 * px + py],
                    send_sem=s_sems.at[it * 3 + k], recv_sem=r_sems.at[it * 3 + k],
                    device_id=(px, py, c), device_id_type=MESH)
                cp.wait_send()
                cp.wait_recv()

    res = _pcall(
        body, name=name, out_shape=tuple(pltpu.HBM(a.shape, a.dtype) for a in arrs + list(lands)),
        in_specs=[_HBM] * (2 * ni) + [_SEM, _SEM, _ANY], out_specs=tuple([_HBM] * (2 * ni)),
        input_output_aliases={i: i for i in range(2 * ni)}, compiler_params=_SPLIT,
    )(*arrs, *lands, send_sems, recv_sems, after)
    return list(res[:ni]), list(res[ni:])


def _tile(n, cap):
    if n <= cap:
        return n
    t = cap - cap % LANES
    while n % t:
        t -= LANES
    return t


def _mm(a, b, mode, *, name, out_dtypes=(F32,), tm=MM_TILE, tn=MM_TILE, tk=MM_TILE, a_fn=None, out_fn=None, aux=(),
        stack=None):
    (a, a_lead), (b, b_lead) = _split(a), _split(b)
    ash, bsh = a.shape[-2:], b.shape[-2:]
    if mode == "nn":
        (M, K), (_, N) = ash, bsh
    elif mode == "nt":
        (M, K), (N, _) = ash, bsh
    else:
        (K, M), (_, N) = ash, bsh
    tm, tn, tk = _tile(M, tm), _tile(N, tn), _tile(K, tk)
    nk = K // tk
    if mode == "tn":
        a_spec = _bs((tk, tm), lambda i, j, k: (k, i), a_lead)
    else:
        a_spec = _bs((tm, tk), lambda i, j, k: (i, k), a_lead)
    if mode == "nt":
        b_spec = _bs((tn, tk), lambda i, j, k: (j, k), b_lead)
    else:
        b_spec = _bs((tk, tn), lambda i, j, k: (k, j), b_lead)
    aux_arrs, aux_specs = [], []
    for arr, kind in aux:
        arr, lead = _split(arr)
        aux_arrs.append(arr)
        if kind == "mn":
            aux_specs.append(_bs((tm, tn), lambda i, j, k: (i, j), lead))
        else:
            aux_specs.append(_bs((1, tn), lambda i, j, k: (0, j), lead))
    na, no = len(aux_arrs), len(out_dtypes)
    dims = {"nn": (((1,), (0,)), ((), ())), "nt": (((1,), (1,)), ((), ())), "tn": (((0,), (0,)), ((), ()))}[mode]

    def finish(r, aux_refs, o_refs):
        outs = out_fn(r, *[x[...] for x in aux_refs]) if out_fn is not None else (r,)
        for o_ref, val in zip(o_refs, outs):
            o_ref[...] = val.astype(o_ref.dtype)

    def product(a_ref, b_ref):
        av = a_ref[...]
        if a_fn is not None:
            av = a_fn(av.astype(F32))
        return lax.dot_general(av.astype(_MXU), b_ref[...].astype(_MXU), dims, preferred_element_type=F32)

    nbuf = 0 if stack is None or stack[0] is None else 1

    def body_one(a_ref, b_ref, *rest):
        finish(product(a_ref, b_ref), rest[:na], rest[na + nbuf:na + nbuf + no])

    def body_acc(a_ref, b_ref, *rest):
        aux_refs, o_refs, acc = rest[:na], rest[na + nbuf:na + nbuf + no], rest[na + nbuf + no]
        k = pl.program_id(2)

        @pl.when(k == 0)
        def _():
            acc[...] = product(a_ref, b_ref)

        @pl.when(k != 0)
        def _():
            acc[...] += product(a_ref, b_ref)

        @pl.when(k == nk - 1)
        def _():
            finish(acc[...], aux_refs, o_refs)

    extra, aliases = {}, []
    if stack is None:
        out_shape = (M, N)
        o_spec = pl.BlockSpec((tm, tn), lambda i, j, k: (i, j))
    else:
        buf, layer, n_layers = stack
        assert no == 1
        out_shape = (n_layers, M, N)
        o_spec = pl.BlockSpec((None, tm, tn), lambda i, j, k: (layer, i, j))
        if buf is not None:
            aliases = [buf]
            extra = dict(input_output_aliases={2 + na: 0})
    outs = _pcall(
        body_one if nk == 1 else body_acc, name=name, grid=(M // tm, N // tn, nk),
        in_specs=[a_spec, b_spec] + aux_specs + [pl.BlockSpec(memory_space=pl.ANY)] * nbuf,
        out_specs=tuple([o_spec] * no),
        out_shape=tuple(SDS(out_shape, dt) for dt in out_dtypes),
        scratch_shapes=[] if nk == 1 else [pltpu.VMEM((tm, tn), F32)],
        compiler_params=pltpu.CompilerParams(dimension_semantics=("parallel", "parallel", "arbitrary"),
                                             vmem_limit_bytes=VMEM_BIG),
        **extra,
    )(a, b, *aux_arrs, *aliases)
    return outs[0] if no == 1 else outs


def _tok(S):
    ts = min(TOKEN_BLOCK, S)
    assert S % ts == 0
    return ts


def _row(ts, D):
    return pl.BlockSpec((ts, D), lambda i: (i, 0))


def _acc_rows(ref, i, rows):
    @pl.when(i == 0)
    def _():
        for r, v in enumerate(rows):
            ref[r:r + 1, :] = v

    @pl.when(i != 0)
    def _():
        for r, v in enumerate(rows):
            ref[r:r + 1, :] += v


def _modulate(x, mod, r_sh, r_sc, name):
    S, D = x.shape
    ts = _tok(S)

    def body(x_ref, m_ref, o_ref):
        o_ref[...] = (x_ref[...] * (1.0 + m_ref[r_sc:r_sc + 1, :]) + m_ref[r_sh:r_sh + 1, :]).astype(o_ref.dtype)

    return _pcall(body, name=name, grid=(S // ts,), in_specs=[_row(ts, D), _full(mod)],
                  out_specs=_row(ts, D), out_shape=SDS((S, D), _MXU))(x, mod)


def _modulate_bwd(x, mod, r_sc, dh, dxa, name):
    S, D = x.shape
    ts = _tok(S)

    def body(x_ref, m_ref, dh_ref, dxa_ref, dx_ref, dss_ref):
        dh_v = dh_ref[...]
        dx_ref[...] = dxa_ref[...] + dh_v * (1.0 + m_ref[r_sc:r_sc + 1, :])
        _acc_rows(dss_ref, pl.program_id(0),
                  [jnp.sum(dh_v, axis=0, keepdims=True), jnp.sum(dh_v * x_ref[...], axis=0, keepdims=True)])

    return _pcall(body, name=name, grid=(S // ts,),
                  in_specs=[_row(ts, D), _full(mod), _row(ts, D), _row(ts, D)],
                  out_specs=(_row(ts, D), pl.BlockSpec((2, D), lambda i: (0, 0))),
                  out_shape=(SDS((S, D), F32), SDS((2, D), F32)))(x, mod, dh, dxa)


def _norm_stats(z):
    mu = jnp.mean(z, axis=-1, keepdims=True)
    zc = z - mu
    var = jnp.mean(zc * zc, axis=-1, keepdims=True)
    rstd = lax.rsqrt(var + LN_EPS)
    return zc * rstd, rstd


def _norm_bwd(dxhat, xhat, rstd):
    return rstd * (dxhat - jnp.mean(dxhat, axis=-1, keepdims=True)
                   - xhat * jnp.mean(dxhat * xhat, axis=-1, keepdims=True))


def _combine(x, y, mod, r_gt, lnp, r_g, name):
    S, D = x.shape
    ts = _tok(S)

    def body(x_ref, y_ref, m_ref, l_ref, o_ref):
        z = ALPHA * x_ref[...] + (1.0 + m_ref[r_gt:r_gt + 1, :]) * y_ref[...]
        xhat, _ = _norm_stats(z)
        o_ref[...] = xhat * l_ref[r_g:r_g + 1, :] + l_ref[r_g + 1:r_g + 2, :]

    return _pcall(body, name=name, grid=(S // ts,), in_specs=[_row(ts, D), _row(ts, D), _full(mod), _full(lnp)],
                  out_specs=_row(ts, D), out_shape=SDS((S, D), F32))(x, y, mod, lnp)


def _combine_bwd(x, y, mod, r_gt, lnp, r_g, dout, name):
    S, D = x.shape
    ts = _tok(S)

    def body(x_ref, y_ref, m_ref, l_ref, do_ref, dxa_ref, dy_ref, dp_ref):
        gate = 1.0 + m_ref[r_gt:r_gt + 1, :]
        y_v, do_v = y_ref[...], do_ref[...]
        xhat, rstd = _norm_stats(ALPHA * x_ref[...] + gate * y_v)
        dz = _norm_bwd(do_v * l_ref[r_g:r_g + 1, :], xhat, rstd)
        dxa_ref[...] = ALPHA * dz
        dy_ref[...] = (gate * dz).astype(dy_ref.dtype)
        _acc_rows(dp_ref, pl.program_id(0),
                  [jnp.sum(dz * y_v, axis=0, keepdims=True), jnp.sum(do_v * xhat, axis=0, keepdims=True),
                   jnp.sum(do_v, axis=0, keepdims=True)])

    return _pcall(body, name=name, grid=(S // ts,),
                  in_specs=[_row(ts, D), _row(ts, D), _full(mod), _full(lnp), _row(ts, D)],
                  out_specs=(_row(ts, D), _row(ts, D), pl.BlockSpec((3, D), lambda i: (0, 0))),
                  out_shape=(SDS((S, D), F32), SDS((S, D), _MXU), SDS((3, D), F32)))(x, y, mod, lnp, dout)


def _sigmoid(t):
    return 1.0 / (1.0 + jnp.exp(-t))


def _ln_silu(u, lnp, name):
    S, D = u.shape
    ts = _tok(S)

    def body(u_ref, l_ref, o_ref):
        xhat, _ = _norm_stats(u_ref[...])
        t = xhat * l_ref[0:1, :] + l_ref[1:2, :]
        o_ref[...] = (t * _sigmoid(t)).astype(o_ref.dtype)

    return _pcall(body, name=name, grid=(S // ts,), in_specs=[_row(ts, D), _full(lnp)],
                  out_specs=_row(ts, D), out_shape=SDS((S, D), _MXU))(u, lnp)


def _ln_silu_bwd(u, lnp, dout, name):
    S, D = u.shape
    ts = _tok(S)

    def body(u_ref, l_ref, do_ref, du_ref, dp_ref):
        xhat, rstd = _norm_stats(u_ref[...])
        g = l_ref[0:1, :]
        t = xhat * g + l_ref[1:2, :]
        sg = _sigmoid(t)
        dt = do_ref[...] * (sg * (1.0 + t * (1.0 - sg)))
        du_ref[...] = _norm_bwd(dt * g, xhat, rstd)
        _acc_rows(dp_ref, pl.program_id(0),
                  [jnp.sum(dt * xhat, axis=0, keepdims=True), jnp.sum(dt, axis=0, keepdims=True)])

    return _pcall(body, name=name, grid=(S // ts,), in_specs=[_row(ts, D), _full(lnp), _row(ts, D)],
                  out_specs=(_row(ts, D), pl.BlockSpec((2, D), lambda i: (0, 0))),
                  out_shape=(SDS((S, D), F32), SDS((2, D), F32)))(u, lnp, dout)


def _loss_head(xf, tgt, name):
    S, D = xf.shape
    ts = _tok(S)

    def body(x_ref, t_ref, dx_ref, l_ref):
        err = x_ref[...] - t_ref[...]
        dx_ref[...] = err * (1.0 / D)
        part = jnp.sum(jnp.sum(err * err, axis=1, keepdims=True), axis=0, keepdims=True) * (0.5 / D)

        @pl.when(pl.program_id(0) == 0)
        def _():
            l_ref[...] = part

        @pl.when(pl.program_id(0) != 0)
        def _():
            l_ref[...] += part

    return _pcall(body, name=name, grid=(S // ts,), in_specs=[_row(ts, D), _row(ts, D)],
                  out_specs=(_row(ts, D), pl.BlockSpec((1, 1), lambda i: (0, 0))),
                  out_shape=(SDS((S, D), F32), SDS((1, 1), F32)))(xf, tgt)


def _shift_down(u, s, rows):
    if s == 0:
        return u
    return jnp.where(rows >= s, pltpu.roll(u, s, 0), 0.0)


def _shift_up(u, s, rows):
    if s == 0:
        return u
    n = u.shape[0]
    return jnp.where(rows < n - s, pltpu.roll(u, n - s, 0), 0.0)


def _dwconv(u, w_ref, taps, rows):
    acc = jnp.zeros_like(u)
    for j in range(taps):
        acc = acc + w_ref[j:j + 1, :] * _shift_down(u, taps - 1 - j, rows)
    return acc


def _dwconv_bwd(u, dy, w_ref, dw_ref, taps, rows):
    du = jnp.zeros_like(u)
    for j in range(taps):
        s = taps - 1 - j
        du = du + w_ref[j:j + 1, :] * _shift_up(dy, s, rows)
        dw_ref[j:j + 1, :] = jnp.sum(dy * _shift_down(u, s, rows), axis=0, keepdims=True)
    return du


def _col(S, j0=0):
    return pl.BlockSpec((S, LANES), lambda j: (0, j + j0))


def _conv_silu(pm, w, nblk, name):
    S = pm.shape[0]
    taps = w.shape[0]

    def body(u_ref, w_ref, o_ref):
        rows = lax.broadcasted_iota(jnp.int32, (S, LANES), 0)
        cv = _dwconv(u_ref[...], w_ref, taps, rows)
        o_ref[...] = cv * _sigmoid(cv)

    return _pcall(body, name=name, grid=(nblk,),
                  in_specs=[_col(S), pl.BlockSpec((taps, LANES), lambda j: (0, j))],
                  out_specs=_col(S), out_shape=SDS((S, nblk * LANES), F32),
                  compiler_params=_cparams(VMEM_BIG))(pm, w)


def _conv_silu_bwd(pm, w, dout, dpm, j0, name):
    S = pm.shape[0]
    taps = w.shape[0]
    nblk = dout.shape[1] // LANES

    def body(u_ref, w_ref, do_ref, dpm_in, du_ref, dw_ref):
        del dpm_in
        rows = lax.broadcasted_iota(jnp.int32, (S, LANES), 0)
        u = u_ref[...]
        cv = _dwconv(u, w_ref, taps, rows)
        sg = _sigmoid(cv)
        dc = do_ref[...] * (sg * (1.0 + cv * (1.0 - sg)))
        du_ref[...] = _dwconv_bwd(u, dc, w_ref, dw_ref, taps, rows)

    return _pcall(body, name=name, grid=(nblk,),
                  in_specs=[_col(S, j0), pl.BlockSpec((taps, LANES), lambda j: (0, j + j0)), _col(S),
                            pl.BlockSpec(memory_space=pl.ANY)],
                  out_specs=(_col(S, j0), pl.BlockSpec((taps, LANES), lambda j: (0, j))),
                  out_shape=(SDS(dpm.shape, F32), SDS((taps, nblk * LANES), F32)),
                  input_output_aliases={3: 0},
                  compiler_params=_cparams(VMEM_BIG))(pm, w, dout, dpm)


def _glu_conv(p, w, bias, name):
    S, C2 = p.shape
    nblk = C2 // 2 // LANES
    taps = w.shape[0]

    def body(v_ref, g_ref, w_ref, b_ref, o_ref):
        rows = lax.broadcasted_iota(jnp.int32, (S, LANES), 0)
        u = v_ref[...] * _sigmoid(g_ref[...])
        o_ref[...] = _dwconv(u, w_ref, taps, rows) + b_ref[...]

    return _pcall(body, name=name, grid=(nblk,),
                  in_specs=[_col(S), _col(S, nblk), pl.BlockSpec((taps, LANES), lambda j: (0, j)),
                            pl.BlockSpec((1, LANES), lambda j: (0, j))],
                  out_specs=_col(S), out_shape=SDS((S, nblk * LANES), F32),
                  compiler_params=_cparams(VMEM_BIG))(p, p, w, bias)


def _glu_conv_bwd(p, w, dout, name):
    S, C2 = p.shape
    nblk = C2 // 2 // LANES
    taps = w.shape[0]

    def body(v_ref, g_ref, w_ref, do_ref, dv_ref, dg_ref, dw_ref, db_ref):
        rows = lax.broadcasted_iota(jnp.int32, (S, LANES), 0)
        val, sg = v_ref[...], _sigmoid(g_ref[...])
        do_v = do_ref[...]
        du = _dwconv_bwd(val * sg, do_v, w_ref, dw_ref, taps, rows)
        dv_ref[...] = du * sg
        dg_ref[...] = du * val * sg * (1.0 - sg)
        db_ref[...] = jnp.sum(do_v, axis=0, keepdims=True)

    dval, dgate, dw, db = _pcall(
        body, name=name, grid=(nblk,),
        in_specs=[_col(S), _col(S, nblk), pl.BlockSpec((taps, LANES), lambda j: (0, j)), _col(S)],
        out_specs=(_col(S), _col(S), pl.BlockSpec((taps, LANES), lambda j: (0, j)),
                   pl.BlockSpec((1, LANES), lambda j: (0, j))),
        out_shape=(SDS((S, C2 // 2), F32), SDS((S, C2 // 2), F32), SDS((taps, C2 // 2), F32), SDS((1, C2 // 2), F32)),
        compiler_params=_cparams(VMEM_BIG))(p, p, w, dout)
    return dval, dgate, dw, db


def _log1p(e):
    u = 1.0 + e
    d = jnp.where(u == 1.0, 1.0, u - 1.0)
    return jnp.where(u == 1.0, e, jnp.log(u) * (e / d))


def _gate_parts(ps, prm, H):
    lane = lax.broadcasted_iota(jnp.int32, ps.shape, 1)
    is_b, is_g = lane < H, (lane >= H) & (lane < 2 * H)
    beta = _sigmoid(ps)
    t = ps + prm[1:2, :]
    sp = jnp.maximum(t, 0.0) + _log1p(jnp.exp(-jnp.abs(t)))
    na = -jnp.exp(prm[0:1, :])
    return is_b, is_g, beta, t, sp, na


def _gates(ps, prm, H, name):
    S = ps.shape[0]
    ts = _tok(S)

    def body(p_ref, r_ref, o_ref):
        is_b, is_g, beta, _, sp, na = _gate_parts(p_ref[...], r_ref[...], H)
        o_ref[...] = jnp.where(is_b, beta, jnp.where(is_g, na * sp, 0.0))

    return _pcall(body, name=name, grid=(S // ts,), in_specs=[_row(ts, LANES), _full(prm)],
                  out_specs=_row(ts, LANES), out_shape=SDS((S, LANES), F32))(ps, prm)


def _gates_bwd(ps, prm, dgates, H, name):
    S = ps.shape[0]
    ts = _tok(S)

    def body(p_ref, r_ref, dg_ref, dp_ref, dr_ref):
        is_b, is_g, beta, t, sp, na = _gate_parts(p_ref[...], r_ref[...], H)
        dg_v = dg_ref[...]
        dsp = jnp.where(is_g, dg_v * na * _sigmoid(t), 0.0)
        dp_ref[...] = jnp.where(is_b, dg_v * beta * (1.0 - beta), dsp)
        _acc_rows(dr_ref, pl.program_id(0),
                  [jnp.sum(jnp.where(is_g, dg_v * na * sp, 0.0), axis=0, keepdims=True),
                   jnp.sum(dsp, axis=0, keepdims=True)])

    return _pcall(body, name=name, grid=(S // ts,), in_specs=[_row(ts, LANES), _full(prm), _row(ts, LANES)],
                  out_specs=(_row(ts, LANES), pl.BlockSpec((2, LANES), lambda i: (0, 0))),
                  out_shape=(SDS((S, LANES), F32), SDS((2, LANES), F32)))(ps, prm, dgates)


_NN = (((2,), (1,)), ((0,), (0,)))
_NT = (((2,), (2,)), ((0,), (0,)))
_TN = (((1,), (1,)), ((0,), (0,)))


def _mdot(a, b, dims):
    return lax.dot_general(a.astype(_MXU), b.astype(_MXU), dims, preferred_element_type=F32)


def _mdot3(a, b, dims):
    ah, bh = a.astype(_MXU), b.astype(_MXU)
    al, bl = a - ah.astype(F32), b - bh.astype(F32)
    return _mdot(ah, bh, dims) + (_mdot(ah, bl, dims) + _mdot(al, bh, dims))


def _rounded_dot(dims, da_dims, db_dims, a_first, prod=_mdot):
    @jax.custom_vjp
    def f(a, b):
        return prod(a, b, dims)

    def fwd(a, b):
        return prod(a, b, dims), (a, b)

    def bwd(res, ct):
        a, b = res
        da = prod(ct, b, da_dims) if a_first[0] else prod(b, ct, da_dims)
        db = prod(ct, a, db_dims) if a_first[1] else prod(a, ct, db_dims)
        return da, db

    f.defvjp(fwd, bwd)
    return f


_mdot_nn = _rounded_dot(_NN, _NT, _TN, (True, False))
_mdot_nt = _rounded_dot(_NT, _NN, _TN, (True, True))
_mdot_tn = _rounded_dot(_TN, _NT, _NN, (False, False))
def _unit_lower_inverse(a):
    C = a.shape[-1]
    ri = lax.broadcasted_iota(jnp.int32, (1, C, C), 1)
    ci = lax.broadcasted_iota(jnp.int32, (1, C, C), 2)
    t_inv = jnp.where(ri == ci, 1.0, 0.0) - a
    p = a
    for _ in range(max(C.bit_length() - 2, 0)):
        p = _mdot3(p, p, _NN)
        t_inv = t_inv + _mdot3(t_inv, p, _NN)
    return t_inv


@jax.custom_vjp
def _known_inverse(a, t_inv):
    del a
    return t_inv


def _known_inverse_fwd(a, t_inv):
    del a
    return t_inv, t_inv


def _known_inverse_bwd(t_inv, ct):
    da = -_mdot3(_mdot3(t_inv, ct, _TN), t_inv, _NT)
    return da, jnp.zeros_like(t_inv)


_known_inverse.defvjp(_known_inverse_fwd, _known_inverse_bwd)


def _head_cols(gates, off, H):
    lane = lax.broadcasted_iota(jnp.int32, gates.shape, 1)
    cols = [jnp.sum(jnp.where(lane == off + h, gates, 0.0), axis=-1, keepdims=True) for h in range(H)]
    return jnp.concatenate([col[None] for col in cols], axis=0)


def _delta_chunk(qr, kr, v, z, gates, nw, s_in, t_known=None):
    H, C, dk = qr.shape
    beta, g = _head_cols(gates, 0, H), _head_cols(gates, H, H)
    q = qr * lax.rsqrt(jnp.sum(qr * qr, axis=-1, keepdims=True) + L2_EPS) * (dk ** -0.5)
    k = kr * lax.rsqrt(jnp.sum(kr * kr, axis=-1, keepdims=True) + L2_EPS)
    ri = lax.broadcasted_iota(jnp.int32, (1, C, C), 1)
    ci = lax.broadcasted_iota(jnp.int32, (1, C, C), 2)
    causal, strict, eye = ri >= ci, ri > ci, ri == ci
    gam_row = jnp.sum(jnp.where(ri <= ci, g, 0.0), axis=1, keepdims=True)
    gam_col = jnp.sum(jnp.where(eye, gam_row, 0.0), axis=-1, keepdims=True)
    g_last = jnp.sum(g, axis=1, keepdims=True)
    decay = jnp.where(causal, jnp.exp(jnp.where(causal, gam_col - gam_row, 0.0)), 0.0)
    kb = k * beta
    a = jnp.where(strict, _mdot_nt(kb, k) * decay, 0.0)
    t_inv = _unit_lower_inverse(a) if t_known is None else _known_inverse(a, t_known)
    eg = jnp.exp(gam_col)
    u = _mdot_nn(t_inv, v * beta)
    w = _mdot_nn(t_inv, kb * eg)
    a_qk = _mdot_nt(q, k) * decay
    v_new = u - _mdot_nn(w, s_in)
    o = _mdot_nn(q * eg, s_in) + _mdot_nn(a_qk, v_new)
    s_out = s_in * jnp.exp(g_last) + _mdot_tn(k * jnp.exp(g_last - gam_col), v_new)
    og = o * lax.rsqrt(jnp.mean(o * o, axis=-1, keepdims=True) + RMS_EPS) * nw * (z * _sigmoid(z))
    return og, s_out, t_inv


def _heads(ref, H, dk):
    return jnp.stack([ref[:, h * dk:(h + 1) * dk].astype(F32) for h in range(H)])


def _put_heads(ref, val, dk):
    for h in range(val.shape[0]):
        ref[:, h * dk:(h + 1) * dk] = val[h].astype(ref.dtype)


def _delta_fwd(qkv, pm, gates, nw, H, name):
    S = qkv.shape[0]
    hd = qkv.shape[1] // 3
    dk = hd // H
    N = S // CHUNK
    blk = lambda off: pl.BlockSpec((CHUNK, hd), lambda n: (n, off))

    def body(q_ref, k_ref, v_ref, z_ref, g_ref, nw_ref, og_ref, st_ref, ti_ref, s_scr):
        @pl.when(pl.program_id(0) == 0)
        def _():
            s_scr[...] = jnp.zeros_like(s_scr)

        s_in = s_scr[...]
        st_ref[...] = s_in
        og, s_out, t_inv = _delta_chunk(_heads(q_ref, H, dk), _heads(k_ref, H, dk), _heads(v_ref, H, dk),
                                        _heads(z_ref, H, dk), g_ref[...], nw_ref[...], s_in)
        _put_heads(og_ref, og, dk)
        ti_ref[...] = t_inv
        s_scr[...] = s_out

    return _pcall(
        body, name=name, grid=(N,),
        in_specs=[blk(0), blk(1), blk(2), blk(3), pl.BlockSpec((CHUNK, LANES), lambda n: (n, 0)), _full(nw)],
        out_specs=(blk(0), pl.BlockSpec((None, H, dk, dk), lambda n: (n, 0, 0, 0)),
                   pl.BlockSpec((None, H, CHUNK, CHUNK), lambda n: (n, 0, 0, 0))),
        out_shape=(SDS((S, hd), _MXU), SDS((N, H, dk, dk), F32), SDS((N, H, CHUNK, CHUNK), F32)),
        scratch_shapes=[pltpu.VMEM((H, dk, dk), F32)],
        compiler_params=_cparams(VMEM_BIG),
    )(qkv, qkv, qkv, pm, gates, nw)


def _delta_bwd(qkv, pm, gates, nw, states, t_invs, dog, H, name):
    S = qkv.shape[0]
    hd = qkv.shape[1] // 3
    dk = hd // H
    N = S // CHUNK
    blk = lambda off: pl.BlockSpec((CHUNK, hd), lambda n: (N - 1 - n, off))
    gspec = pl.BlockSpec((CHUNK, LANES), lambda n: (N - 1 - n, 0))

    def body(q_ref, k_ref, v_ref, z_ref, g_ref, nw_ref, st_ref, ti_ref, do_ref,
             dq_ref, dk_ref, dv_ref, dz_ref, dg_ref, dnw_ref, ds_scr):
        n = pl.program_id(0)

        @pl.when(n == 0)
        def _():
            ds_scr[...] = jnp.zeros_like(ds_scr)

        t_known = ti_ref[...]
        fn = functools.partial(_delta_chunk, t_known=t_known)
        _, vjp = jax.vjp(fn, _heads(q_ref, H, dk), _heads(k_ref, H, dk), _heads(v_ref, H, dk),
                         _heads(z_ref, H, dk), g_ref[...], nw_ref[...], st_ref[...])
        dq, dkk, dv, dz, dg, dnw, ds_in = vjp((_heads(do_ref, H, dk), ds_scr[...], jnp.zeros_like(t_known)))
        _put_heads(dq_ref, dq, dk)
        _put_heads(dk_ref, dkk, dk)
        _put_heads(dv_ref, dv, dk)
        _put_heads(dz_ref, dz, dk)
        ds_scr[...] = ds_in
        dg_ref[...] = dg

        @pl.when(n == 0)
        def _():
            dnw_ref[...] = dnw

        @pl.when(n != 0)
        def _():
            dnw_ref[...] += dnw

    return _pcall(
        body, name=name, grid=(N,),
        in_specs=[blk(0), blk(1), blk(2), blk(3), gspec, _full(nw),
                  pl.BlockSpec((None, H, dk, dk), lambda n: (N - 1 - n, 0, 0, 0)),
                  pl.BlockSpec((None, H, CHUNK, CHUNK), lambda n: (N - 1 - n, 0, 0, 0)), blk(0)],
        out_specs=(blk(0), blk(0), blk(0), blk(3), gspec, pl.BlockSpec((1, dk), lambda n: (0, 0))),
        out_shape=(SDS((S, hd), F32), SDS((S, hd), F32), SDS((S, hd), F32), SDS(pm.shape, F32),
                   SDS((S, LANES), F32), SDS((1, dk), F32)),
        scratch_shapes=[pltpu.VMEM((H, dk, dk), F32)],
        compiler_params=_cparams(VMEM_BIG),
    )(qkv, qkv, qkv, pm, gates, nw, states, t_invs, dog)


def _rows_block(R, C):
    rb = R
    while rb * C * 4 > (1 << 20) and rb % 16 == 0:
        rb //= 2
    return rb


def _sum_slots(st, name, out_dtype=F32):
    n, R, C = st.shape
    rb = _rows_block(R, C)

    def body(s_ref, o_ref):
        acc = s_ref[0].astype(F32)
        for q in range(1, n):
            acc = acc + s_ref[q].astype(F32)
        o_ref[...] = acc.astype(o_ref.dtype)

    return _pcall(body, name=name, grid=(R // rb,), in_specs=[pl.BlockSpec((n, rb, C), lambda i: (0, i, 0))],
                  out_specs=pl.BlockSpec((rb, C), lambda i: (i, 0)), out_shape=SDS((R, C), out_dtype))(st)


def _scalar(v):
    return jnp.reshape(v, (1,)).astype(jnp.int32)


def _place_block(w, layer, ax, chip, name):
    _, R, C = w.shape
    rb = _rows_block(R, C)
    nrb = R // rb
    shp = [R, C]
    shp[ax] *= 4
    omap = (lambda i, c: (c[0] * nrb + i, 0)) if ax == 0 else (lambda i, c: (i, c[0]))

    def body(c_ref, w_ref, o_ref):
        del c_ref
        o_ref[...] = w_ref[...].astype(o_ref.dtype)

    grid_spec = pltpu.PrefetchScalarGridSpec(
        num_scalar_prefetch=1, grid=(nrb,),
        in_specs=[pl.BlockSpec((None, rb, C), lambda i, c: (layer, i, 0))],
        out_specs=pl.BlockSpec((rb, C), omap))
    return _pcall(body, name=name, grid_spec=grid_spec, out_shape=SDS(tuple(shp), _MXU))(_scalar(chip), w)


def _sum_pair(own, recv, ic, name, out_dtype):
    N, _, R, C = own.shape
    rb = _rows_block(R, C)

    def body(c_ref, a_ref, b_ref, o_ref):
        del c_ref
        o_ref[...] = (a_ref[...].astype(F32) + b_ref[...].astype(F32)).astype(o_ref.dtype)

    grid_spec = pltpu.PrefetchScalarGridSpec(
        num_scalar_prefetch=1, grid=(N, R // rb),
        in_specs=[pl.BlockSpec((None, None, rb, C), lambda p, i, c: (p, c[0], i, 0)),
                  pl.BlockSpec((None, None, rb, C), lambda p, i, c: (1 - c[0], p, i, 0))],
        out_specs=pl.BlockSpec((None, rb, C), lambda p, i, c: (p, i, 0)))
    return _pcall(body, name=name, grid_spec=grid_spec, out_shape=SDS((N, R, C), out_dtype))(
        _scalar(ic), own, recv)


def _sum_landed(pair_sum, ax, land, chip, name):
    _, R, Cb = land.shape
    rb = _rows_block(R, Cb)
    if ax == 0:
        own_spec = pl.BlockSpec((None, rb, Cb), lambda i, c: (c[0], i, 0))
    else:
        own_spec = pl.BlockSpec((rb, Cb), lambda i, c: (i, c[0]))
    slot = lambda d: pl.BlockSpec((None, rb, Cb), lambda i, c: ((c[0] + d) % 4, i, 0))

    def body(c_ref, own_ref, r1, r2, r3, o_ref):
        del c_ref
        o_ref[...] = ((own_ref[...].astype(F32) + r1[...].astype(F32)) + r2[...].astype(F32)) + r3[...].astype(F32)

    grid_spec = pltpu.PrefetchScalarGridSpec(
        num_scalar_prefetch=1, grid=(R // rb,), in_specs=[own_spec, slot(1), slot(2), slot(3)],
        out_specs=pl.BlockSpec((rb, Cb), lambda i, c: (i, 0)))
    return _pcall(body, name=name, grid_spec=grid_spec, out_shape=SDS((R, Cb), F32))(
        _scalar(chip), pair_sum, land, land, land)


def _adamw_step(g, w_ref, m_ref, v_ref, g_ref, d_ref, mo_ref, vo_ref):
    m_new = ADAM_B1 * m_ref[...] + (1.0 - ADAM_B1) * g
    v_new = ADAM_B2 * v_ref[...] + (1.0 - ADAM_B2) * (g * g)
    m_hat = m_new / (1.0 - ADAM_B1 ** ADAM_STEP)
    v_hat = v_new / (1.0 - ADAM_B2 ** ADAM_STEP)
    g_ref[...] = g
    d_ref[...] = -ADAM_LR * (m_hat / (jnp.sqrt(v_hat) + ADAM_EPS) + ADAM_WD * w_ref[...])
    mo_ref[...] = m_new
    vo_ref[...] = v_new


def _adamw(w, m, v, st, name):
    R, C = w.shape
    n = st.shape[0]
    rb = _rows_block(R, C)
    spec = pl.BlockSpec((rb, C), lambda i: (i, 0))

    def body(w_ref, m_ref, v_ref, s_ref, *o_refs):
        g = s_ref[0]
        for q in range(1, n):
            g = g + s_ref[q]
        _adamw_step(g, w_ref, m_ref, v_ref, *o_refs)

    return _pcall(body, name=name, grid=(R // rb,),
                  in_specs=[spec, spec, spec, pl.BlockSpec((n, rb, C), lambda i: (0, i, 0))],
                  out_specs=(spec,) * 4, out_shape=(SDS((R, C), F32),) * 4)(w, m, v, st)


def _adamw_halves(w, m, v, layer, own, recv2, ic, bufs, name):
    L, R, C = w.shape
    rb = _rows_block(R // 2, C)
    nb = R // 2 // rb
    spec = pl.BlockSpec((None, rb, C), lambda hb, i, c: (layer, hb * nb + i, 0))

    def body(c_ref, w_ref, m_ref, v_ref, own_ref, recv_ref, *rest):
        is_own = pl.program_id(0) == c_ref[0]
        _adamw_step(jnp.where(is_own, own_ref[...], recv_ref[...]), w_ref, m_ref, v_ref, *rest[-4:])

    nbuf = 0 if bufs is None else 4
    grid_spec = pltpu.PrefetchScalarGridSpec(
        num_scalar_prefetch=1, grid=(2, nb),
        in_specs=[spec, spec, spec, pl.BlockSpec((rb, C), lambda hb, i, c: (i, 0)),
                  pl.BlockSpec((None, rb, C), lambda hb, i, c: (1 - c[0], i, 0))] + [_ANY] * nbuf,
        out_specs=(spec,) * 4)
    extra = {} if bufs is None else dict(input_output_aliases={6 + q: q for q in range(4)})
    return _pcall(body, name=name, grid_spec=grid_spec, out_shape=(SDS((L, R, C), F32),) * 4, **extra)(
        _scalar(ic), w, m, v, own, recv2, *([] if bufs is None else bufs))


def _pack(arrs, rows=1):
    flat = jnp.concatenate([a.reshape(-1).astype(F32) for a in arrs])
    quantum = rows * LANES
    pad = (-flat.shape[0]) % quantum
    flat = jnp.pad(flat, (0, pad))
    return flat.reshape(rows, -1)


def _unpack(flat, shapes):
    flat = flat.reshape(-1)
    out, off = [], 0
    for shp in shapes:
        size = 1
        for d in shp:
            size *= d
        out.append(flat[off:off + size].reshape(shp))
        off += size
    return out


def _mlp_fwd(x1, mod, lnp, w1, w2, tag):
    h2 = _modulate(x1, mod, 3, 4, f"{tag}_mod")
    a1, a2 = _mm(h2, w1, "nn", name=f"{tag}_up", out_dtypes=(_MXU, _MXU),
                 out_fn=lambda r: (r, jnp.square(jnp.maximum(r, 0.0))))
    y2 = _mm(a2, w2, "nn", name=f"{tag}_down")
    x2 = _combine(x1, y2, mod, 5, lnp, 2, f"{tag}_ln")
    return x2, (x1, h2, a1, a2, y2)


def _weight_grad(grads, key, a, b, name):
    grads[key] = _mm(a, b, "tn", name=name)


def _mlp_bwd(dx2, saved, mod, lnp, w1, w2, tag, stacks):
    x1, h2, a1, a2, y2 = saved
    dxa, dy2, dp = _combine_bwd(x1, y2, mod, 5, lnp, 2, dx2, f"{tag}_ln_b")
    da1 = _mm(dy2, w2, "nt", name=f"{tag}_down_bx", out_dtypes=(_MXU,), aux=[(a1, "mn")],
              out_fn=lambda r, a: (r * (2.0 * jnp.maximum(a.astype(F32), 0.0)),))
    _weight_grad(stacks, "ff_w2", a2, dy2, f"{tag}_down_bw")
    _weight_grad(stacks, "ff_w1", h2, da1, f"{tag}_up_bw")
    dh2 = _mm(da1, w1, "nt", name=f"{tag}_up_bx")
    dx1, dss = _modulate_bwd(x1, mod, 4, dh2, dxa, f"{tag}_mod_b")
    return dx1, (dss, dp)


def _dn_fwd(x, mod, lnp, wts, H, tag):
    w_main, w_small, conv_w, prm, nw, w_out = wts
    h = _modulate(x, mod, 0, 1, f"{tag}_mod")
    pm = _mm(h, w_main, "nn", name=f"{tag}_in")
    ps = _mm(h, w_small, "nn", name=f"{tag}_in_s")
    nqkv = conv_w.shape[1] // LANES
    qkv = _conv_silu(pm, conv_w, nqkv, f"{tag}_conv")
    gates = _gates(ps, prm, H, f"{tag}_gates")
    og, *states = _delta_fwd(qkv, pm, gates, nw, H, f"{tag}_delta")
    y = _mm(og, w_out, "nn", name=f"{tag}_out")
    x1 = _combine(x, y, mod, 2, lnp, 0, f"{tag}_ln")
    return x1, (x, h, pm, ps, qkv, gates, states, og, y)


def _dn_bwd(dx1, saved, mod, lnp, wts, H, tag, stacks):
    w_main, w_small, conv_w, prm, nw, w_out = wts
    x, h, pm, ps, qkv, gates, states, og, y = saved
    dxa, dy, dp = _combine_bwd(x, y, mod, 2, lnp, 0, dx1, f"{tag}_ln_b")
    dog = _mm(dy, w_out, "nt", name=f"{tag}_out_bx")
    _weight_grad(stacks, "dn_w_out", og, dy, f"{tag}_out_bw")
    dq, dk, dv, dpm, dgates, dnw = _delta_bwd(qkv, pm, gates, nw, *states, dog, H, f"{tag}_delta_b")
    dps, dprm = _gates_bwd(ps, prm, dgates, H, f"{tag}_gates_b")
    dcw = []
    nb = dq.shape[1] // LANES
    for part, dpart in enumerate((dq, dk, dv)):
        dpm, dcw_p = _conv_silu_bwd(pm, conv_w, dpart, dpm, part * nb, f"{tag}_conv_b{part}")
        dcw.append(dcw_p)
    dconv_w = jnp.concatenate(dcw, axis=1)
    dw_main = _mm(h, dpm, "tn", name=f"{tag}_in_bw")
    dw_small = _mm(h, dps, "tn", name=f"{tag}_in_s_bw")
    dh_s = _mm(dps, w_small, "nt", name=f"{tag}_in_s_bx")
    dh = _mm(dpm, w_main, "nt", name=f"{tag}_in_bx", aux=[(dh_s, "mn")], out_fn=lambda r, e: (r + e,))
    dx, dss = _modulate_bwd(x, mod, 1, dh, dxa, f"{tag}_mod_b")
    return dx, (dw_main, dw_small, dconv_w, dprm, dnw), (dss, dp)


def _cf_fwd(x, mod, lnp, wts, tag):
    w_in, dw_w, dw_b, cln, w_out = wts
    h = _modulate(x, mod, 0, 1, f"{tag}_mod")
    p = _mm(h, w_in, "nn", name=f"{tag}_in")
    u2 = _glu_conv(p, dw_w, dw_b, f"{tag}_conv")
    u3 = _ln_silu(u2, cln, f"{tag}_cln")
    y = _mm(u3, w_out, "nn", name=f"{tag}_out")
    x1 = _combine(x, y, mod, 2, lnp, 0, f"{tag}_ln")
    return x1, (x, h, p, u2, u3, y)


def _cf_bwd(dx1, saved, mod, lnp, wts, tag, stacks):
    w_in, dw_w, dw_b, cln, w_out = wts
    x, h, p, u2, u3, y = saved
    dxa, dy, dp = _combine_bwd(x, y, mod, 2, lnp, 0, dx1, f"{tag}_ln_b")
    du3 = _mm(dy, w_out, "nt", name=f"{tag}_out_bx")
    _weight_grad(stacks, "cf_w_out", u3, dy, f"{tag}_out_bw")
    du2, dcln = _ln_silu_bwd(u2, cln, du3, f"{tag}_cln_b")
    dval, dgate, ddw_w, ddw_b = _glu_conv_bwd(p, dw_w, du2, f"{tag}_conv_b")
    dpp = jnp.concatenate([dval, dgate], axis=1)
    _weight_grad(stacks, "cf_w_in", h, dpp, f"{tag}_in_bw")
    dh = _mm(dpp, w_in, "nt", name=f"{tag}_in_bx")
    dx, dss = _modulate_bwd(x, mod, 1, dh, dxa, f"{tag}_mod_b")
    return dx, (ddw_w, ddw_b, dcln), (dss, dp)


def _two_d(a):
    return a.reshape(-1, a.shape[-1])


def kernel(x, c, ada_w, ada_b, ln_g, ln_b, dn_w_in, dn_conv_w, dn_a_log, dn_dt_bias, dn_norm_w, dn_w_out, cf_w_in, cf_dw_w, cf_dw_b, cf_ln_g, cf_ln_b, cf_w_out, ff_w1, ff_w2, loss_target, m_ada_w, m_ada_b, m_ln_g, m_ln_b, m_dn_w_in, m_dn_conv_w, m_dn_a_log, m_dn_dt_bias, m_dn_norm_w, m_dn_w_out, m_cf_w_in, m_cf_dw_w, m_cf_dw_b, m_cf_ln_g, m_cf_ln_b, m_cf_w_out, m_ff_w1, m_ff_w2, v_ada_w, v_ada_b, v_ln_g, v_ln_b, v_dn_w_in, v_dn_conv_w, v_dn_a_log, v_dn_dt_bias, v_dn_norm_w, v_dn_w_out, v_cf_w_in, v_cf_dw_w, v_cf_dw_b, v_cf_ln_g, v_cf_ln_b, v_cf_w_out, v_ff_w1, v_ff_w2):
    ix, iy, ic = lax.axis_index("x"), lax.axis_index("y"), lax.axis_index("c")
    chip = 2 * ix + iy
    dev = 4 * ix + 2 * iy + ic
    S, D = x.shape[1], x.shape[2]
    L = ada_w.shape[0]
    LA, LB = dn_w_in.shape[0], cf_w_in.shape[0]
    H = dn_a_log.shape[1]
    NMOD = ada_b.shape[1] // D
    dn_in = dn_w_in.shape[2] * 4
    n_main = dn_in - 2 * H
    assert L == N_LAYERS and 2 * H <= LANES
    x0, tgt = x[0], loss_target[0]

    small_sharded = [ln_g, ln_b, dn_conv_w, cf_dw_w, cf_dw_b, cf_ln_g, cf_ln_b]
    small_axes = [2, 2, 2, 2, 1, 1, 1]
    packed_small = _pack(small_sharded, rows=8)[None]
    big = {"dn_w_in": (dn_w_in, m_dn_w_in, v_dn_w_in, 0), "dn_w_out": (dn_w_out, m_dn_w_out, v_dn_w_out, 0),
           "cf_w_in": (cf_w_in, m_cf_w_in, v_cf_w_in, 1), "cf_w_out": (cf_w_out, m_cf_w_out, v_cf_w_out, 0),
           "ff_w1": (ff_w1, m_ff_w1, v_ff_w1, 1), "ff_w2": (ff_w2, m_ff_w2, v_ff_w2, 0)}

    def group(g):
        l = g // 2
        if g % 2:
            return {"ff_w1": l, "ff_w2": l}
        mixer = ("dn_w_in", "dn_w_out") if l % 2 == 0 else ("cf_w_in", "cf_w_out")
        return {mixer[0]: l // 2, mixer[1]: l // 2}

    def start_gather(l, after):
        names = list(group(l))
        axes = [big[nm][3] for nm in names]
        placed = [_place_block(big[nm][0], group(l)[nm], ax, chip, f"l{l}_place_{nm}") for nm, ax in zip(names, axes)]
        send, recv, arrs, token = _gather_start(list(zip(placed, axes)), after, f"l{l}_gather_start")
        return names, axes, arrs, send, recv, token

    def finish_gather(l, pending, after):
        names, axes, arrs, send, recv, _ = pending
        arrs = _gather_wait(list(zip(arrs, axes)), send, recv, after, f"l{l}_gather_wait")
        arrs = _gather_forward(list(zip(arrs, axes)), f"l{l}_gather_pass")
        return dict(zip(names, arrs))

    g_small = _exchange([(packed_small, 0)], "xy", "gather", "gather_small")[0]
    shard_shapes = [a.shape for a in small_sharded]
    per_chip = [_unpack(g_small[q], shard_shapes) for q in range(4)]
    ln_g_f, ln_b_f, conv_w_f, dw_w_f, dw_b_f, cln_g_f, cln_b_f = [
        jnp.concatenate([per_chip[q][i] for q in range(4)], axis=small_axes[i]) for i in range(len(small_sharded))]

    c_all = _exchange([(c[None], 0)], "all", "gather", "gather_cond")[0].reshape(8, D)
    c_pad = jnp.pad(c_all, ((0, 8), (0, 0)))
    mod_sh = jnp.stack([_mm(c_pad, (ada_w, l), "nn", name=f"ada_{l}", a_fn=lambda t: t * _sigmoid(t))
                        for l in range(L)])
    mod_all = _exchange([(mod_sh, 2)], "xy", "gather", "gather_mod")[0]
    mod_mine = lax.dynamic_index_in_dim(mod_all, dev, axis=1, keepdims=False) + ada_b
    mods = mod_mine.reshape(L, NMOD, D)

    def lnp_of(l):
        return jnp.stack([ln_g_f[l, 0], ln_b_f[l, 0], ln_g_f[l, 1], ln_b_f[l, 1]])

    def mixer_wts(l, wl):
        j = l // 2
        if l % 2 == 0:
            w_in = jnp.transpose(wl["dn_w_in"].reshape(4, D, dn_in // 4), (1, 0, 2)).reshape(D, dn_in)
            w_small = jnp.pad(w_in[:, n_main:], ((0, 0), (0, LANES - 2 * H)))
            prm = jnp.zeros((2, LANES), F32).at[0, H:2 * H].set(dn_a_log[j]).at[1, H:2 * H].set(dn_dt_bias[j])
            return (w_in[:, :n_main], w_small, conv_w_f[j], prm, dn_norm_w[j][None], wl["dn_w_out"])
        return (wl["cf_w_in"], dw_w_f[j], dw_b_f[j][None], jnp.stack([cln_g_f[j], cln_b_f[j]]), wl["cf_w_out"])

    xs = x0
    saved, wts, mod_of = [], [], []
    pending = start_gather(0, x0)
    for g in range(2 * L):
        l = g // 2
        wl = finish_gather(g, pending, xs)
        mod_g = mods[l]
        if g + 1 < 2 * L:
            pending = start_gather(g + 1, next(iter(wl.values())))
            mod_g = mod_g + pending[5][0, 0]
        mod_of.append(mod_g)
        if g % 2:
            wts.append((wl["ff_w1"], wl["ff_w2"]))
            xs, sv = _mlp_fwd(xs, mod_g, lnp_of(l), *wts[g], f"l{l}_ff")
        elif l % 2 == 0:
            wts.append(mixer_wts(l, wl))
            xs, sv = _dn_fwd(xs, mod_g, lnp_of(l), wts[g], H, f"l{l}_dn")
        else:
            wts.append(mixer_wts(l, wl))
            xs, sv = _cf_fwd(xs, mod_g, lnp_of(l), wts[g], f"l{l}_cf")
        saved.append(sv)
    dx, loss_local = _loss_head(xs, tgt, "loss_head")
    loss = lax.psum(loss_local[0, 0], ("x", "y", "c"))

    def start_scatter(l, grads):
        names = list(group(l))
        axes = [big[nm][3] for nm in names]
        own = [grads[nm].reshape((4, 2, -1, grads[nm].shape[1]) if ax == 0 else (1, 2, -1, grads[nm].shape[1]))
               for nm, ax in zip(names, axes)]
        pair = _exchange([(o, 1) for o in own], "c", "scatter", f"l{l}_pair_grads", keep_own=False)
        sums = []
        for nm, ax, o, r in zip(names, axes, own, pair):
            s = _sum_pair(o, r.reshape((2, o.shape[0]) + o.shape[2:]), ic, f"l{l}_sum_pair_{nm}", _MXU)
            sums.append(s if ax == 0 else s[0])
        send, recv, sums, lands, token = _scatter_start(list(zip(sums, axes)), mods, f"l{l}_scatter_start")
        return names, axes, sums, lands, send, recv, token

    def finish_scatter(l, pending, after):
        names, axes, sums, lands, send, recv, _ = pending
        sums, lands = _scatter_wait(list(zip(sums, axes)), lands, send, recv, after, f"l{l}_scatter_wait")
        mine = [_sum_landed(s, ax, ld, chip, f"l{l}_sum_grads_{nm}") for nm, ax, s, ld in zip(names, axes, sums, lands)]
        other = _exchange([(s[None], 0) for s in mine], "c", "gather", f"l{l}_swap_sums", keep_own=False)
        return {nm: (s, o) for nm, s, o in zip(names, mine, other)}

    g_dn = [None] * LA
    g_cf = [None] * LB
    dmods, dlns = [None] * L, [None] * L
    big_sums = {}
    pending, token = None, None
    small_of = {}
    for g in reversed(range(2 * L)):
        l, j = g // 2, g // 4
        mod_g = mod_of[g] if token is None else mod_of[g] + token[0, 0]
        grads = {}
        if g % 2:
            dx, small_of[g] = _mlp_bwd(dx, saved[g], mod_g, lnp_of(l), *wts[g], f"l{l}_ff", grads)
        elif l % 2 == 0:
            dx, g_dn[j], small_of[g] = _dn_bwd(dx, saved[g], mod_g, lnp_of(l), wts[g], H, f"l{l}_dn", grads)
            dn_in_g = jnp.concatenate([g_dn[j][0], g_dn[j][1][:, :2 * H]], axis=1)
            grads["dn_w_in"] = jnp.transpose(dn_in_g.reshape(D, 4, dn_in // 4), (1, 0, 2)).reshape(4 * D, dn_in // 4)
        else:
            dx, g_cf[j], small_of[g] = _cf_bwd(dx, saved[g], mod_g, lnp_of(l), wts[g], f"l{l}_cf", grads)
        if pending is not None:
            for nm, pair_of in finish_scatter(g + 1, pending, dx).items():
                big_sums[nm, group(g + 1)[nm]] = pair_of
        pending = start_scatter(g, grads)
        token = pending[6]
    for nm, pair_of in finish_scatter(0, pending, dx).items():
        big_sums[nm, group(0)[nm]] = pair_of
    for l in range(L):
        (dss1, dp1), (dss2, dp2) = small_of[2 * l], small_of[2 * l + 1]
        dmods[l] = jnp.concatenate([dss1, dp1[0:1], dss2, dp2[0:1]], axis=0)
        dlns[l] = (jnp.stack([dp1[1], dp2[1]]), jnp.stack([dp1[2], dp2[2]]))
    grad_x = dx[None]

    d_ln_g = jnp.stack([dlns[l][0] for l in range(L)])
    d_ln_b = jnp.stack([dlns[l][1] for l in range(L)])
    d_conv_w = jnp.stack([g_dn[j][2] for j in range(LA)])
    d_a_log = jnp.stack([g_dn[j][3][0, H:2 * H] for j in range(LA)])
    d_dt_bias = jnp.stack([g_dn[j][3][1, H:2 * H] for j in range(LA)])
    d_norm_w = jnp.stack([g_dn[j][4][0] for j in range(LA)])
    d_dw_w = jnp.stack([g_cf[j][0] for j in range(LB)])
    d_dw_b = jnp.stack([g_cf[j][1][0] for j in range(LB)])
    d_cln_g = jnp.stack([g_cf[j][2][0] for j in range(LB)])
    d_cln_b = jnp.stack([g_cf[j][2][1] for j in range(LB)])
    d_mod = jnp.stack(dmods).reshape(L, NMOD * D)
    small_full = [d_mod, d_ln_g, d_ln_b, d_conv_w, d_dw_w, d_dw_b, d_cln_g, d_cln_b, d_a_log, d_dt_bias, d_norm_w]
    small_all = _exchange([(_pack(small_full, rows=8)[None], 0)], "all", "gather", "gather_small_grads")[0]
    small_sum = _sum_slots(small_all, "sum_small_grads")
    (s_ada_b, s_ln_g, s_ln_b, s_conv_w, s_dw_w, s_dw_b, s_cln_g, s_cln_b, s_a_log, s_dt_bias, s_norm_w) = _unpack(
        small_sum, [a.shape for a in small_full])
    d_mod_all = small_all.reshape(8, -1)[:, :L * NMOD * D].reshape(8, L, NMOD * D)

    def shard(a, axis):
        size = a.shape[axis] // 4
        return lax.dynamic_slice_in_dim(a, chip * size, size, axis)

    ncol = ada_w.shape[2]
    d_mod_sh = jnp.pad(lax.dynamic_slice_in_dim(d_mod_all, chip * ncol, ncol, 2), ((0, 8), (0, 0), (0, 0)))
    g_ada_w = None
    for l in range(L):
        g_ada_w = _mm(c_pad, d_mod_sh[:, l], "tn", name=f"ada_bw_{l}", a_fn=lambda t: t * _sigmoid(t),
                      stack=(g_ada_w, l, L))

    big_out = []
    for nm in ["dn_w_in", "dn_w_out", "cf_w_in", "cf_w_out", "ff_w1", "ff_w2"]:
        w, m, v, _ = big[nm]
        res = None
        for layer in range(w.shape[0]):
            own, oth = big_sums[nm, layer]
            res = _adamw_halves(w, m, v, layer, own, oth, ic, res, f"adamw_{nm}_{layer}")
        big_out.append(res)
    ada_out = [r.reshape(ada_w.shape) for r in _adamw(_two_d(ada_w), _two_d(m_ada_w), _two_d(v_ada_w),
                                                     _two_d(g_ada_w)[None], "adamw_ada_w")]

    small_w = [(ada_b, m_ada_b, v_ada_b, s_ada_b), (ln_g, m_ln_g, v_ln_g, shard(s_ln_g, 2)),
               (ln_b, m_ln_b, v_ln_b, shard(s_ln_b, 2)), (dn_conv_w, m_dn_conv_w, v_dn_conv_w, shard(s_conv_w, 2)),
               (dn_a_log, m_dn_a_log, v_dn_a_log, s_a_log), (dn_dt_bias, m_dn_dt_bias, v_dn_dt_bias, s_dt_bias),
               (dn_norm_w, m_dn_norm_w, v_dn_norm_w, s_norm_w), (cf_dw_w, m_cf_dw_w, v_cf_dw_w, shard(s_dw_w, 2)),
               (cf_dw_b, m_cf_dw_b, v_cf_dw_b, shard(s_dw_b, 1)), (cf_ln_g, m_cf_ln_g, v_cf_ln_g, shard(s_cln_g, 1)),
               (cf_ln_b, m_cf_ln_b, v_cf_ln_b, shard(s_cln_b, 1))]
    pk = [_pack([t[i] for t in small_w], rows=8) for i in range(4)]
    small_res = _adamw(pk[0], pk[1], pk[2], pk[3][None], "adamw_small")
    small_shapes = [t[0].shape for t in small_w]
    small_out = [_unpack(r, small_shapes) for r in small_res]

    def kind(k):
        sm = small_out[k]
        bg = [o[k] for o in big_out]
        return [ada_out[k], sm[0], sm[1], sm[2], bg[0], sm[3], sm[4], sm[5], sm[6], bg[1],
                bg[2], sm[7], sm[8], sm[9], sm[10], bg[3], bg[4], bg[5]]

    return (loss, grad_x, *kind(0), *kind(1), *kind(2), *kind(3))
```

```python
import functools

import jax
import jax.numpy as jnp
from jax import lax
from jax.experimental import pallas as pl
from jax.experimental.pallas import tpu as pltpu

F32 = jnp.float32
_MXU = jnp.bfloat16
_HI = lax.Precision.HIGHEST

N_LAYERS = 4
ALPHA = (2.0 * N_LAYERS) ** 0.25
LN_EPS = 1e-5
RMS_EPS = 1e-6
L2_EPS = 1e-6
CHUNK = 64
ADAM_LR, ADAM_B1, ADAM_B2, ADAM_EPS, ADAM_WD, ADAM_STEP = 0.001, 0.9, 0.999, 1e-08, 0.01, 10

LANES = 128
TOKEN_BLOCK = 256
VMEM_BIG = 48 * 1024 * 1024
MM_TILE = 1024

SDS = jax.ShapeDtypeStruct
MESH = pl.DeviceIdType.MESH


def _cparams(vmem=None):
    if vmem is None:
        return None
    return pltpu.CompilerParams(vmem_limit_bytes=vmem)


def _pcall(body, **kw):
    if kw.get("compiler_params", 1) is None:
        kw.pop("compiler_params")
    return pl.pallas_call(body, **kw)


def _full(arr):
    nd = arr.ndim
    return pl.BlockSpec(arr.shape, lambda *g: (0,) * nd)


def _bs(block, imap, lead=None):
    if lead is None:
        return pl.BlockSpec(block, imap)
    return pl.BlockSpec((None,) + tuple(block), lambda *g: (lead,) + tuple(imap(*g)))


def _split(a):
    return a if isinstance(a, tuple) else (a, None)


_GROUPS = {
    "xy": ([(1, 0, 0), (0, 1, 0), (1, 1, 0)], 4),
    "c": ([(0, 0, 1)], 2),
    "all": ([(1, 0, 0), (0, 1, 0), (1, 1, 0), (0, 0, 1), (1, 0, 1), (0, 1, 1), (1, 1, 1)], 8),
}


def _exchange(items, group, mode, name, nsplit=1, keep_own=True):
    masks, n = _GROUPS[group]
    npeer = len(masks)
    ni = len(items)
    arrs = [a for a, _ in items]
    out_shapes = []
    for a, ax in items:
        shp = list(a.shape)
        if mode == "gather":
            shp[ax] *= n
        else:
            shp[ax] //= n
            shp = [n] + shp
        out_shapes.append(SDS(tuple(shp), a.dtype))

    def body(*refs):
        ins, outs = refs[:ni], refs[ni:2 * ni]
        send_sems, recv_sems, local_sems = refs[2 * ni:]
        x, y, c = lax.axis_index("x"), lax.axis_index("y"), lax.axis_index("c")

        def slot(px, py, pc):
            if group == "xy":
                return 2 * px + py
            if group == "c":
                return pc
            return 4 * px + 2 * py + pc

        me = slot(x, y, c)

        def block(ref, ax, idx, size):
            ix = (slice(None),) * ax + (pl.ds(pl.multiple_of(idx * size, size), size),)
            return ref.at[ix]

        copies = []
        for it, (a, ax) in enumerate(items):
            in_ref, out_ref = ins[it], outs[it]
            if mode == "gather":
                size = a.shape[ax]
                src_own, dst_own = in_ref, block(out_ref, ax, me, size)
            else:
                size = a.shape[ax] // n
                src_own, dst_own = block(in_ref, ax, me, size), out_ref.at[me]
            sax, ns, cs = splits[it]
            pieces = [(slice(None),) * sax + (pl.ds(j * cs, cs),) for j in range(ns)]
            if keep_own:
                for j, piece in enumerate(pieces):
                    own = pltpu.make_async_copy(src_own.at[piece], dst_own.at[piece], local_sems.at[it * nsplit + j])
                    own.start()
                    copies.append(own)
            for k, m in enumerate(masks):
                peer = tuple((1 - v) if b else v for v, b in zip((x, y, c), m))
                if mode == "gather":
                    src, dst = in_ref, dst_own
                else:
                    src, dst = block(in_ref, ax, slot(*peer), size), out_ref.at[me]
                for j, piece in enumerate(pieces):
                    sem = (it * npeer + k) * nsplit + j
                    cp = pltpu.make_async_remote_copy(
                        src_ref=src.at[piece], dst_ref=dst.at[piece], send_sem=send_sems.at[sem],
                        recv_sem=recv_sems.at[sem], device_id=peer, device_id_type=MESH)
                    cp.start()
                    copies.append(cp)
        for cp in copies:
            cp.wait()

    splits = []
    for (a, ax), o in zip(items, out_shapes):
        bshape = a.shape if mode == "gather" else o.shape[1:]
        sax = max(range(len(bshape) - 1), key=lambda d: bshape[d])
        ns = nsplit if bshape[sax] % (nsplit * 16) == 0 else 1
        splits.append((sax, ns, bshape[sax] // ns))
    any_spec = pl.BlockSpec(memory_space=pl.ANY)
    nsem = ni * npeer * nsplit
    outs = _pcall(
        body, name=name, out_shape=tuple(out_shapes),
        in_specs=[any_spec] * ni, out_specs=tuple([any_spec] * ni),
        scratch_shapes=[pltpu.SemaphoreType.DMA((nsem,)), pltpu.SemaphoreType.DMA((nsem,)),
                        pltpu.SemaphoreType.DMA((ni * nsplit,))],
    )(*arrs)
    return list(outs)


_HBM = pl.BlockSpec(memory_space=pltpu.HBM)
_SEM = pl.BlockSpec(memory_space=pltpu.SEMAPHORE)
_ANY = pl.BlockSpec(memory_space=pl.ANY)
_SPLIT = pltpu.CompilerParams(has_side_effects=pltpu.SideEffectType.DATAFLOW_SIDE_EFFECTING)
_XY = _GROUPS["xy"][0]


def _in_hbm(a):
    return pltpu.with_memory_space_constraint(a, pltpu.HBM)


def _chip_peers():
    x, y, c = lax.axis_index("x"), lax.axis_index("y"), lax.axis_index("c")
    return x, y, c, [tuple((1 - v) if b else v for v, b in zip((x, y), m)) for m in _XY]


def _wblock(ref, ax, half, chip):
    R, C = ref.shape
    if ax == 0:
        rows = R // 8
        return ref.at[pl.ds(pl.multiple_of(chip * (2 * rows) + half * rows, rows), rows), :]
    rows, cols = R // 2, C // 4
    return ref.at[pl.ds(pl.multiple_of(half * rows, rows), rows), pl.ds(pl.multiple_of(chip * cols, cols), cols)]


def _gather_start(items, after, name):
    ni = len(items)
    arrs = [a for a, _ in items]

    def body(*refs):
        send_sems, recv_sems = refs[ni + 1], refs[ni + 2]
        outs, token = refs[ni + 3:2 * ni + 3], refs[2 * ni + 3]
        x, y, c, peers = _chip_peers()
        for it, (_, ax) in enumerate(items):
            mine = _wblock(outs[it], ax, c, 2 * x + y)
            for k, (px, py) in enumerate(peers):
                pltpu.make_async_remote_copy(
                    src_ref=mine, dst_ref=mine, send_sem=send_sems.at[it * 3 + k], recv_sem=recv_sems.at[it * 3 + k],
                    device_id=(px, py, c), device_id_type=MESH).start()
        token[...] = jnp.zeros_like(token)

    res = _pcall(
        body, name=name,
        out_shape=(pltpu.SemaphoreType.DMA((ni * 3,)), pltpu.SemaphoreType.DMA((ni * 3,)),
                   *[pltpu.HBM(a.shape, a.dtype) for a in arrs], SDS((8, LANES), F32)),
        in_specs=[_HBM] * ni + [_ANY],
        out_specs=(_SEM, _SEM, *[_HBM] * ni, pl.BlockSpec(memory_space=pltpu.VMEM)),
        input_output_aliases={i: 2 + i for i in range(ni)}, compiler_params=_SPLIT,
    )(*[_in_hbm(a) for a in arrs], after)
    return res[0], res[1], list(res[2:2 + ni]), res[2 + ni]


def _gather_wait(items, send_sems, recv_sems, after, name):
    ni = len(items)
    arrs = [a for a, _ in items]

    def body(*refs):
        s_sems, r_sems = refs[ni], refs[ni + 1]
        outs = refs[ni + 3:]
        x, y, c, peers = _chip_peers()
        for it, (_, ax) in enumerate(items):
            mine = _wblock(outs[it], ax, c, 2 * x + y)
            for k, (px, py) in enumerate(peers):
                cp = pltpu.make_async_remote_copy(
                    src_ref=mine, dst_ref=_wblock(outs[it], ax, c, 2 * px + py), send_sem=s_sems.at[it * 3 + k],
                    recv_sem=r_sems.at[it * 3 + k], device_id=(px, py, c), device_id_type=MESH)
                cp.wait_send()
                cp.wait_recv()

    res = _pcall(
        body, name=name, out_shape=tuple(pltpu.HBM(a.shape, a.dtype) for a in arrs),
        in_specs=[_HBM] * ni + [_SEM, _SEM, _ANY], out_specs=tuple([_HBM] * ni),
        input_output_aliases={i: i for i in range(ni)}, compiler_params=_SPLIT,
    )(*arrs, send_sems, recv_sems, after)
    return list(res)


def _gather_forward(items, name):
    ni = len(items)
    arrs = [a for a, _ in items]

    def body(*refs):
        outs = refs[ni:2 * ni]
        send_sems, recv_sems = refs[2 * ni:]
        x, y, c, peers = _chip_peers()
        def copy(it, ax, k, chip, dst_half):
            return pltpu.make_async_remote_copy(
                src_ref=_wblock(outs[it], ax, c, chip), dst_ref=_wblock(outs[it], ax, dst_half, chip),
                send_sem=send_sems.at[it * 3 + k], recv_sem=recv_sems.at[it * 3 + k],
                device_id=(x, y, 1 - c), device_id_type=MESH)

        for it, (_, ax) in enumerate(items):
            for k, (px, py) in enumerate(peers):
                copy(it, ax, k, 2 * px + py, c).start()
        for it, (_, ax) in enumerate(items):
            for k, (px, py) in enumerate(peers):
                copy(it, ax, k, 2 * px + py, c).wait_send()
                copy(it, ax, k, 2 * px + py, 1 - c).wait_recv()

    res = _pcall(
        body, name=name, out_shape=tuple(SDS(a.shape, a.dtype) for a in arrs),
        in_specs=[_ANY] * ni, out_specs=tuple([_ANY] * ni), input_output_aliases={i: i for i in range(ni)},
        scratch_shapes=[pltpu.SemaphoreType.DMA((ni * 3,)), pltpu.SemaphoreType.DMA((ni * 3,))],
    )(*arrs)
    return list(res)


def _gblock(ref, ax, chip):
    size = ref.shape[ax] // 4
    piece = pl.ds(pl.multiple_of(chip * size, size), size)
    return ref.at[piece, :] if ax == 0 else ref.at[:, piece]


def _scatter_start(items, after, name):
    ni = len(items)
    arrs = [a for a, _ in items]
    lands = []
    for a, ax in items:
        blk = (a.shape[0] // 4, a.shape[1]) if ax == 0 else (a.shape[0], a.shape[1] // 4)
        lands.append(lax.empty((4, *blk), a.dtype))

    def body(*refs):
        send_sems, recv_sems = refs[2 * ni + 1], refs[2 * ni + 2]
        srcs, dsts = refs[2 * ni + 3:3 * ni + 3], refs[3 * ni + 3:4 * ni + 3]
        token = refs[4 * ni + 3]
        x, y, c, peers = _chip_peers()
        for it, (_, ax) in enumerate(items):
            for k, (px, py) in enumerate(peers):
                pltpu.make_async_remote_copy(
                    src_ref=_gblock(srcs[it], ax, 2 * px + py), dst_ref=dsts[it].at[2 * x + y],
                    send_sem=send_sems.at[it * 3 + k], recv_sem=recv_sems.at[it * 3 + k],
                    device_id=(px, py, c), device_id_type=MESH).start()
        token[...] = jnp.zeros_like(token)

    res = _pcall(
        body, name=name,
        out_shape=(pltpu.SemaphoreType.DMA((ni * 3,)), pltpu.SemaphoreType.DMA((ni * 3,)),
                   *[pltpu.HBM(a.shape, a.dtype) for a in arrs], *[pltpu.HBM(a.shape, a.dtype) for a in lands],
                   SDS((8, LANES), F32)),
        in_specs=[_HBM] * (2 * ni) + [_ANY],
        out_specs=(_SEM, _SEM, *[_HBM] * (2 * ni), pl.BlockSpec(memory_space=pltpu.VMEM)),
        input_output_aliases={i: 2 + i for i in range(2 * ni)}, compiler_params=_SPLIT,
    )(*[_in_hbm(a) for a in arrs], *[_in_hbm(a) for a in lands], after)
    return res[0], res[1], list(res[2:2 + ni]), list(res[2 + ni:2 + 2 * ni]), res[2 + 2 * ni]


def _scatter_wait(items, lands, send_sems, recv_sems, after, name):
    ni = len(items)
    arrs = [a for a, _ in items]

    def body(*refs):
        s_sems, r_sems = refs[2 * ni], refs[2 * ni + 1]
        srcs, dsts = refs[2 * ni + 3:3 * ni + 3], refs[3 * ni + 3:]
        x, y, c, peers = _chip_peers()
        for it, (_, ax) in enumerate(items):
            for k, (px, py) in enumerate(peers):
                cp = pltpu.make_async_remote_copy(
                    src_ref=_gblock(srcs[it], ax, 2 * px + py), dst_ref=dsts[it].at[2 * px + py],
                    send_sem=s_sems.at[it * 3 + k], recv_sem=r_sems.at[it * 3 + k],
                    device_id=(px, py, c), device_id_type=MESH)
                cp.wait_send()
                cp.wait_recv()

    res = _pcall(
        body, name=name, out_shape=tuple(pltpu.HBM(a.shape, a.dtype) for a in arrs + list(lands)),
        in_specs=[_HBM] * (2 * ni) + [_SEM, _SEM, _ANY], out_specs=tuple([_HBM] * (2 * ni)),
        input_output_aliases={i: i for i in range(2 * ni)}, compiler_params=_SPLIT,
    )(*arrs, *lands, send_sems, recv_sems, after)
    return list(res[:ni]), list(res[ni:])


def _tile(n, cap):
    if n <= cap:
        return n
    t = cap - cap % LANES
    while n % t:
        t -= LANES
    return t


def _mm(a, b, mode, *, name, out_dtypes=(F32,), tm=MM_TILE, tn=MM_TILE, tk=MM_TILE, a_fn=None, out_fn=None, aux=(),
        stack=None):
    (a, a_lead), (b, b_lead) = _split(a), _split(b)
    ash, bsh = a.shape[-2:], b.shape[-2:]
    if mode == "nn":
        (M, K), (_, N) = ash, bsh
    elif mode == "nt":
        (M, K), (N, _) = ash, bsh
    else:
        (K, M), (_, N) = ash, bsh
    tm, tn, tk = _tile(M, tm), _tile(N, tn), _tile(K, tk)
    nk = K // tk
    if mode == "tn":
        a_spec = _bs((tk, tm), lambda i, j, k: (k, i), a_lead)
    else:
        a_spec = _bs((tm, tk), lambda i, j, k: (i, k), a_lead)
    if mode == "nt":
        b_spec = _bs((tn, tk), lambda i, j, k: (j, k), b_lead)
    else:
        b_spec = _bs((tk, tn), lambda i, j, k: (k, j), b_lead)
    aux_arrs, aux_specs = [], []
    for arr, kind in aux:
        arr, lead = _split(arr)
        aux_arrs.append(arr)
        if kind == "mn":
            aux_specs.append(_bs((tm, tn), lambda i, j, k: (i, j), lead))
        else:
            aux_specs.append(_bs((1, tn), lambda i, j, k: (0, j), lead))
    na, no = len(aux_arrs), len(out_dtypes)
    dims = {"nn": (((1,), (0,)), ((), ())), "nt": (((1,), (1,)), ((), ())), "tn": (((0,), (0,)), ((), ()))}[mode]

    def finish(r, aux_refs, o_refs):
        outs = out_fn(r, *[x[...] for x in aux_refs]) if out_fn is not None else (r,)
        for o_ref, val in zip(o_refs, outs):
            o_ref[...] = val.astype(o_ref.dtype)

    def product(a_ref, b_ref):
        av = a_ref[...]
        if a_fn is not None:
            av = a_fn(av.astype(F32))
        return lax.dot_general(av.astype(_MXU), b_ref[...].astype(_MXU), dims, preferred_element_type=F32)

    nbuf = 0 if stack is None or stack[0] is None else 1

    def body_one(a_ref, b_ref, *rest):
        finish(product(a_ref, b_ref), rest[:na], rest[na + nbuf:na + nbuf + no])

    def body_acc(a_ref, b_ref, *rest):
        aux_refs, o_refs, acc = rest[:na], rest[na + nbuf:na + nbuf + no], rest[na + nbuf + no]
        k = pl.program_id(2)

        @pl.when(k == 0)
        def _():
            acc[...] = product(a_ref, b_ref)

        @pl.when(k != 0)
        def _():
            acc[...] += product(a_ref, b_ref)

        @pl.when(k == nk - 1)
        def _():
            finish(acc[...], aux_refs, o_refs)

    extra, aliases = {}, []
    if stack is None:
        out_shape = (M, N)
        o_spec = pl.BlockSpec((tm, tn), lambda i, j, k: (i, j))
    else:
        buf, layer, n_layers = stack
        assert no == 1
        out_shape = (n_layers, M, N)
        o_spec = pl.BlockSpec((None, tm, tn), lambda i, j, k: (layer, i, j))
        if buf is not None:
            aliases = [buf]
            extra = dict(input_output_aliases={2 + na: 0})
    outs = _pcall(
        body_one if nk == 1 else body_acc, name=name, grid=(M // tm, N // tn, nk),
        in_specs=[a_spec, b_spec] + aux_specs + [pl.BlockSpec(memory_space=pl.ANY)] * nbuf,
        out_specs=tuple([o_spec] * no),
        out_shape=tuple(SDS(out_shape, dt) for dt in out_dtypes),
        scratch_shapes=[] if nk == 1 else [pltpu.VMEM((tm, tn), F32)],
        compiler_params=pltpu.CompilerParams(dimension_semantics=("parallel", "parallel", "arbitrary"),
                                             vmem_limit_bytes=VMEM_BIG),
        **extra,
    )(a, b, *aux_arrs, *aliases)
    return outs[0] if no == 1 else outs


def _tok(S):
    ts = min(TOKEN_BLOCK, S)
    assert S % ts == 0
    return ts


def _row(ts, D):
    return pl.BlockSpec((ts, D), lambda i: (i, 0))


def _acc_rows(ref, i, rows):
    @pl.when(i == 0)
    def _():
        for r, v in enumerate(rows):
            ref[r:r + 1, :] = v

    @pl.when(i != 0)
    def _():
        for r, v in enumerate(rows):
            ref[r:r + 1, :] += v


def _modulate(x, mod, r_sh, r_sc, name):
    S, D = x.shape
    ts = _tok(S)

    def body(x_ref, m_ref, o_ref):
        o_ref[...] = (x_ref[...] * (1.0 + m_ref[r_sc:r_sc + 1, :]) + m_ref[r_sh:r_sh + 1, :]).astype(o_ref.dtype)

    return _pcall(body, name=name, grid=(S // ts,), in_specs=[_row(ts, D), _full(mod)],
                  out_specs=_row(ts, D), out_shape=SDS((S, D), _MXU))(x, mod)


def _modulate_bwd(x, mod, r_sc, dh, dxa, name):
    S, D = x.shape
    ts = _tok(S)

    def body(x_ref, m_ref, dh_ref, dxa_ref, dx_ref, dss_ref):
        dh_v = dh_ref[...]
        dx_ref[...] = dxa_ref[...] + dh_v * (1.0 + m_ref[r_sc:r_sc + 1, :])
        _acc_rows(dss_ref, pl.program_id(0),
                  [jnp.sum(dh_v, axis=0, keepdims=True), jnp.sum(dh_v * x_ref[...], axis=0, keepdims=True)])

    return _pcall(body, name=name, grid=(S // ts,),
                  in_specs=[_row(ts, D), _full(mod), _row(ts, D), _row(ts, D)],
                  out_specs=(_row(ts, D), pl.BlockSpec((2, D), lambda i: (0, 0))),
                  out_shape=(SDS((S, D), F32), SDS((2, D), F32)))(x, mod, dh, dxa)


def _norm_stats(z):
    mu = jnp.mean(z, axis=-1, keepdims=True)
    zc = z - mu
    var = jnp.mean(zc * zc, axis=-1, keepdims=True)
    rstd = lax.rsqrt(var + LN_EPS)
    return zc * rstd, rstd


def _norm_bwd(dxhat, xhat, rstd):
    return rstd * (dxhat - jnp.mean(dxhat, axis=-1, keepdims=True)
                   - xhat * jnp.mean(dxhat * xhat, axis=-1, keepdims=True))


def _combine(x, y, mod, r_gt, lnp, r_g, name):
    S, D = x.shape
    ts = _tok(S)

    def body(x_ref, y_ref, m_ref, l_ref, o_ref):
        z = ALPHA * x_ref[...] + (1.0 + m_ref[r_gt:r_gt + 1, :]) * y_ref[...]
        xhat, _ = _norm_stats(z)
        o_ref[...] = xhat * l_ref[r_g:r_g + 1, :] + l_ref[r_g + 1:r_g + 2, :]

    return _pcall(body, name=name, grid=(S // ts,), in_specs=[_row(ts, D), _row(ts, D), _full(mod), _full(lnp)],
                  out_specs=_row(ts, D), out_shape=SDS((S, D), F32))(x, y, mod, lnp)


def _combine_bwd(x, y, mod, r_gt, lnp, r_g, dout, name):
    S, D = x.shape
    ts = _tok(S)

    def body(x_ref, y_ref, m_ref, l_ref, do_ref, dxa_ref, dy_ref, dp_ref):
        gate = 1.0 + m_ref[r_gt:r_gt + 1, :]
        y_v, do_v = y_ref[...], do_ref[...]
        xhat, rstd = _norm_stats(ALPHA * x_ref[...] + gate * y_v)
        dz = _norm_bwd(do_v * l_ref[r_g:r_g + 1, :], xhat, rstd)
        dxa_ref[...] = ALPHA * dz
        dy_ref[...] = (gate * dz).astype(dy_ref.dtype)
        _acc_rows(dp_ref, pl.program_id(0),
                  [jnp.sum(dz * y_v, axis=0, keepdims=True), jnp.sum(do_v * xhat, axis=0, keepdims=True),
                   jnp.sum(do_v, axis=0, keepdims=True)])

    return _pcall(body, name=name, grid=(S // ts,),
                  in_specs=[_row(ts, D), _row(ts, D), _full(mod), _full(lnp), _row(ts, D)],
                  out_specs=(_row(ts, D), _row(ts, D), pl.BlockSpec((3, D), lambda i: (0, 0))),
                  out_shape=(SDS((S, D), F32), SDS((S, D), _MXU), SDS((3, D), F32)))(x, y, mod, lnp, dout)


def _sigmoid(t):
    return 1.0 / (1.0 + jnp.exp(-t))


def _ln_silu(u, lnp, name):
    S, D = u.shape
    ts = _tok(S)

    def body(u_ref, l_ref, o_ref):
        xhat, _ = _norm_stats(u_ref[...])
        t = xhat * l_ref[0:1, :] + l_ref[1:2, :]
        o_ref[...] = (t * _sigmoid(t)).astype(o_ref.dtype)

    return _pcall(body, name=name, grid=(S // ts,), in_specs=[_row(ts, D), _full(lnp)],
                  out_specs=_row(ts, D), out_shape=SDS((S, D), _MXU))(u, lnp)


def _ln_silu_bwd(u, lnp, dout, name):
    S, D = u.shape
    ts = _tok(S)

    def body(u_ref, l_ref, do_ref, du_ref, dp_ref):
        xhat, rstd = _norm_stats(u_ref[...])
        g = l_ref[0:1, :]
        t = xhat * g + l_ref[1:2, :]
        sg = _sigmoid(t)
        dt = do_ref[...] * (sg * (1.0 + t * (1.0 - sg)))
        du_ref[...] = _norm_bwd(dt * g, xhat, rstd)
        _acc_rows(dp_ref, pl.program_id(0),
                  [jnp.sum(dt * xhat, axis=0, keepdims=True), jnp.sum(dt, axis=0, keepdims=True)])

    return _pcall(body, name=name, grid=(S // ts,), in_specs=[_row(ts, D), _full(lnp), _row(ts, D)],
                  out_specs=(_row(ts, D), pl.BlockSpec((2, D), lambda i: (0, 0))),
                  out_shape=(SDS((S, D), F32), SDS((2, D), F32)))(u, lnp, dout)


def _loss_head(xf, tgt, name):
    S, D = xf.shape
    ts = _tok(S)

    def body(x_ref, t_ref, dx_ref, l_ref):
        err = x_ref[...] - t_ref[...]
        dx_ref[...] = err * (1.0 / D)
        part = jnp.sum(jnp.sum(err * err, axis=1, keepdims=True), axis=0, keepdims=True) * (0.5 / D)

        @pl.when(pl.program_id(0) == 0)
        def _():
            l_ref[...] = part

        @pl.when(pl.program_id(0) != 0)
        def _():
            l_ref[...] += part

    return _pcall(body, name=name, grid=(S // ts,), in_specs=[_row(ts, D), _row(ts, D)],
                  out_specs=(_row(ts, D), pl.BlockSpec((1, 1), lambda i: (0, 0))),
                  out_shape=(SDS((S, D), F32), SDS((1, 1), F32)))(xf, tgt)


def _shift_down(u, s, rows):
    if s == 0:
        return u
    return jnp.where(rows >= s, pltpu.roll(u, s, 0), 0.0)


def _shift_up(u, s, rows):
    if s == 0:
        return u
    n = u.shape[0]
    return jnp.where(rows < n - s, pltpu.roll(u, n - s, 0), 0.0)


def _dwconv(u, w_ref, taps, rows):
    acc = jnp.zeros_like(u)
    for j in range(taps):
        acc = acc + w_ref[j:j + 1, :] * _shift_down(u, taps - 1 - j, rows)
    return acc


def _dwconv_bwd(u, dy, w_ref, dw_ref, taps, rows):
    du = jnp.zeros_like(u)
    for j in range(taps):
        s = taps - 1 - j
        du = du + w_ref[j:j + 1, :] * _shift_up(dy, s, rows)
        dw_ref[j:j + 1, :] = jnp.sum(dy * _shift_down(u, s, rows), axis=0, keepdims=True)
    return du


def _col(S, j0=0):
    return pl.BlockSpec((S, LANES), lambda j: (0, j + j0))


def _conv_silu(pm, w, nblk, name):
    S = pm.shape[0]
    taps = w.shape[0]

    def body(u_ref, w_ref, o_ref):
        rows = lax.broadcasted_iota(jnp.int32, (S, LANES), 0)
        cv = _dwconv(u_ref[...], w_ref, taps, rows)
        o_ref[...] = cv * _sigmoid(cv)

    return _pcall(body, name=name, grid=(nblk,),
                  in_specs=[_col(S), pl.BlockSpec((taps, LANES), lambda j: (0, j))],
                  out_specs=_col(S), out_shape=SDS((S, nblk * LANES), F32),
                  compiler_params=_cparams(VMEM_BIG))(pm, w)


def _conv_silu_bwd(pm, w, dout, dpm, j0, name):
    S = pm.shape[0]
    taps = w.shape[0]
    nblk = dout.shape[1] // LANES

    def body(u_ref, w_ref, do_ref, dpm_in, du_ref, dw_ref):
        del dpm_in
        rows = lax.broadcasted_iota(jnp.int32, (S, LANES), 0)
        u = u_ref[...]
        cv = _dwconv(u, w_ref, taps, rows)
        sg = _sigmoid(cv)
        dc = do_ref[...] * (sg * (1.0 + cv * (1.0 - sg)))
        du_ref[...] = _dwconv_bwd(u, dc, w_ref, dw_ref, taps, rows)

    return _pcall(body, name=name, grid=(nblk,),
                  in_specs=[_col(S, j0), pl.BlockSpec((taps, LANES), lambda j: (0, j + j0)), _col(S),
                            pl.BlockSpec(memory_space=pl.ANY)],
                  out_specs=(_col(S, j0), pl.BlockSpec((taps, LANES), lambda j: (0, j))),
                  out_shape=(SDS(dpm.shape, F32), SDS((taps, nblk * LANES), F32)),
                  input_output_aliases={3: 0},
                  compiler_params=_cparams(VMEM_BIG))(pm, w, dout, dpm)


def _glu_conv(p, w, bias, name):
    S, C2 = p.shape
    nblk = C2 // 2 // LANES
    taps = w.shape[0]

    def body(v_ref, g_ref, w_ref, b_ref, o_ref):
        rows = lax.broadcasted_iota(jnp.int32, (S, LANES), 0)
        u = v_ref[...] * _sigmoid(g_ref[...])
        o_ref[...] = _dwconv(u, w_ref, taps, rows) + b_ref[...]

    return _pcall(body, name=name, grid=(nblk,),
                  in_specs=[_col(S), _col(S, nblk), pl.BlockSpec((taps, LANES), lambda j: (0, j)),
                            pl.BlockSpec((1, LANES), lambda j: (0, j))],
                  out_specs=_col(S), out_shape=SDS((S, nblk * LANES), F32),
                  compiler_params=_cparams(VMEM_BIG))(p, p, w, bias)


def _glu_conv_bwd(p, w, dout, name):
    S, C2 = p.shape
    nblk = C2 // 2 // LANES
    taps = w.shape[0]

    def body(v_ref, g_ref, w_ref, do_ref, dv_ref, dg_ref, dw_ref, db_ref):
        rows = lax.broadcasted_iota(jnp.int32, (S, LANES), 0)
        val, sg = v_ref[...], _sigmoid(g_ref[...])
        do_v = do_ref[...]
        du = _dwconv_bwd(val * sg, do_v, w_ref, dw_ref, taps, rows)
        dv_ref[...] = du * sg
        dg_ref[...] = du * val * sg * (1.0 - sg)
        db_ref[...] = jnp.sum(do_v, axis=0, keepdims=True)

    dval, dgate, dw, db = _pcall(
        body, name=name, grid=(nblk,),
        in_specs=[_col(S), _col(S, nblk), pl.BlockSpec((taps, LANES), lambda j: (0, j)), _col(S)],
        out_specs=(_col(S), _col(S), pl.BlockSpec((taps, LANES), lambda j: (0, j)),
                   pl.BlockSpec((1, LANES), lambda j: (0, j))),
        out_shape=(SDS((S, C2 // 2), F32), SDS((S, C2 // 2), F32), SDS((taps, C2 // 2), F32), SDS((1, C2 // 2), F32)),
        compiler_params=_cparams(VMEM_BIG))(p, p, w, dout)
    return dval, dgate, dw, db


def _log1p(e):
    u = 1.0 + e
    d = jnp.where(u == 1.0, 1.0, u - 1.0)
    return jnp.where(u == 1.0, e, jnp.log(u) * (e / d))


def _gate_parts(ps, prm, H):
    lane = lax.broadcasted_iota(jnp.int32, ps.shape, 1)
    is_b, is_g = lane < H, (lane >= H) & (lane < 2 * H)
    beta = _sigmoid(ps)
    t = ps + prm[1:2, :]
    sp = jnp.maximum(t, 0.0) + _log1p(jnp.exp(-jnp.abs(t)))
    na = -jnp.exp(prm[0:1, :])
    return is_b, is_g, beta, t, sp, na


def _gates(ps, prm, H, name):
    S = ps.shape[0]
    ts = _tok(S)

    def body(p_ref, r_ref, o_ref):
        is_b, is_g, beta, _, sp, na = _gate_parts(p_ref[...], r_ref[...], H)
        o_ref[...] = jnp.where(is_b, beta, jnp.where(is_g, na * sp, 0.0))

    return _pcall(body, name=name, grid=(S // ts,), in_specs=[_row(ts, LANES), _full(prm)],
                  out_specs=_row(ts, LANES), out_shape=SDS((S, LANES), F32))(ps, prm)


def _gates_bwd(ps, prm, dgates, H, name):
    S = ps.shape[0]
    ts = _tok(S)

    def body(p_ref, r_ref, dg_ref, dp_ref, dr_ref):
        is_b, is_g, beta, t, sp, na = _gate_parts(p_ref[...], r_ref[...], H)
        dg_v = dg_ref[...]
        dsp = jnp.where(is_g, dg_v * na * _sigmoid(t), 0.0)
        dp_ref[...] = jnp.where(is_b, dg_v * beta * (1.0 - beta), dsp)
        _acc_rows(dr_ref, pl.program_id(0),
                  [jnp.sum(jnp.where(is_g, dg_v * na * sp, 0.0), axis=0, keepdims=True),
                   jnp.sum(dsp, axis=0, keepdims=True)])

    return _pcall(body, name=name, grid=(S // ts,), in_specs=[_row(ts, LANES), _full(prm), _row(ts, LANES)],
                  out_specs=(_row(ts, LANES), pl.BlockSpec((2, LANES), lambda i: (0, 0))),
                  out_shape=(SDS((S, LANES), F32), SDS((2, LANES), F32)))(ps, prm, dgates)


_NN = (((2,), (1,)), ((0,), (0,)))
_NT = (((2,), (2,)), ((0,), (0,)))
_TN = (((1,), (1,)), ((0,), (0,)))


def _mdot(a, b, dims):
    return lax.dot_general(a.astype(_MXU), b.astype(_MXU), dims, preferred_element_type=F32)


def _mdot3(a, b, dims):
    ah, bh = a.astype(_MXU), b.astype(_MXU)
    al, bl = a - ah.astype(F32), b - bh.astype(F32)
    return _mdot(ah, bh, dims) + (_mdot(ah, bl, dims) + _mdot(al, bh, dims))


def _rounded_dot(dims, da_dims, db_dims, a_first, prod=_mdot):
    @jax.custom_vjp
    def f(a, b):
        return prod(a, b, dims)

    def fwd(a, b):
        return prod(a, b, dims), (a, b)

    def bwd(res, ct):
        a, b = res
        da = prod(ct, b, da_dims) if a_first[0] else prod(b, ct, da_dims)
        db = prod(ct, a, db_dims) if a_first[1] else prod(a, ct, db_dims)
        return da, db

    f.defvjp(fwd, bwd)
    return f


_mdot_nn = _rounded_dot(_NN, _NT, _TN, (True, False))
_mdot_nt = _rounded_dot(_NT, _NN, _TN, (True, True))
_mdot_tn = _rounded_dot(_TN, _NT, _NN, (False, False))
def _unit_lower_inverse(a):
    C = a.shape[-1]
    ri = lax.broadcasted_iota(jnp.int32, (1, C, C), 1)
    ci = lax.broadcasted_iota(jnp.int32, (1, C, C), 2)
    t_inv = jnp.where(ri == ci, 1.0, 0.0) - a
    p = a
    for _ in range(max(C.bit_length() - 2, 0)):
        p = _mdot3(p, p, _NN)
        t_inv = t_inv + _mdot3(t_inv, p, _NN)
    return t_inv


@jax.custom_vjp
def _known_inverse(a, t_inv):
    del a
    return t_inv


def _known_inverse_fwd(a, t_inv):
    del a
    return t_inv, t_inv


def _known_inverse_bwd(t_inv, ct):
    da = -_mdot3(_mdot3(t_inv, ct, _TN), t_inv, _NT)
    return da, jnp.zeros_like(t_inv)


_known_inverse.defvjp(_known_inverse_fwd, _known_inverse_bwd)


def _head_cols(gates, off, H):
    lane = lax.broadcasted_iota(jnp.int32, gates.shape, 1)
    cols = [jnp.sum(jnp.where(lane == off + h, gates, 0.0), axis=-1, keepdims=True) for h in range(H)]
    return jnp.concatenate([col[None] for col in cols], axis=0)


def _delta_chunk(qr, kr, v, z, gates, nw, s_in, t_known=None):
    H, C, dk = qr.shape
    beta, g = _head_cols(gates, 0, H), _head_cols(gates, H, H)
    q = qr * lax.rsqrt(jnp.sum(qr * qr, axis=-1, keepdims=True) + L2_EPS) * (dk ** -0.5)
    k = kr * lax.rsqrt(jnp.sum(kr * kr, axis=-1, keepdims=True) + L2_EPS)
    ri = lax.broadcasted_iota(jnp.int32, (1, C, C), 1)
    ci = lax.broadcasted_iota(jnp.int32, (1, C, C), 2)
    causal, strict, eye = ri >= ci, ri > ci, ri == ci
    gam_row = jnp.sum(jnp.where(ri <= ci, g, 0.0), axis=1, keepdims=True)
    gam_col = jnp.sum(jnp.where(eye, gam_row, 0.0), axis=-1, keepdims=True)
    g_last = jnp.sum(g, axis=1, keepdims=True)
    decay = jnp.where(causal, jnp.exp(jnp.where(causal, gam_col - gam_row, 0.0)), 0.0)
    kb = k * beta
    a = jnp.where(strict, _mdot_nt(kb, k) * decay, 0.0)
    t_inv = _unit_lower_inverse(a) if t_known is None else _known_inverse(a, t_known)
    eg = jnp.exp(gam_col)
    u = _mdot_nn(t_inv, v * beta)
    w = _mdot_nn(t_inv, kb * eg)
    a_qk = _mdot_nt(q, k) * decay
    v_new = u - _mdot_nn(w, s_in)
    o = _mdot_nn(q * eg, s_in) + _mdot_nn(a_qk, v_new)
    s_out = s_in * jnp.exp(g_last) + _mdot_tn(k * jnp.exp(g_last - gam_col), v_new)
    og = o * lax.rsqrt(jnp.mean(o * o, axis=-1, keepdims=True) + RMS_EPS) * nw * (z * _sigmoid(z))
    return og, s_out, t_inv


def _heads(ref, H, dk):
    return jnp.stack([ref[:, h * dk:(h + 1) * dk].astype(F32) for h in range(H)])


def _put_heads(ref, val, dk):
    for h in range(val.shape[0]):
        ref[:, h * dk:(h + 1) * dk] = val[h].astype(ref.dtype)


def _delta_fwd(qkv, pm, gates, nw, H, name):
    S = qkv.shape[0]
    hd = qkv.shape[1] // 3
    dk = hd // H
    N = S // CHUNK
    blk = lambda off: pl.BlockSpec((CHUNK, hd), lambda n: (n, off))

    def body(q_ref, k_ref, v_ref, z_ref, g_ref, nw_ref, og_ref, st_ref, ti_ref, s_scr):
        @pl.when(pl.program_id(0) == 0)
        def _():
            s_scr[...] = jnp.zeros_like(s_scr)

        s_in = s_scr[...]
        st_ref[...] = s_in
        og, s_out, t_inv = _delta_chunk(_heads(q_ref, H, dk), _heads(k_ref, H, dk), _heads(v_ref, H, dk),
                                        _heads(z_ref, H, dk), g_ref[...], nw_ref[...], s_in)
        _put_heads(og_ref, og, dk)
        ti_ref[...] = t_inv
        s_scr[...] = s_out

    return _pcall(
        body, name=name, grid=(N,),
        in_specs=[blk(0), blk(1), blk(2), blk(3), pl.BlockSpec((CHUNK, LANES), lambda n: (n, 0)), _full(nw)],
        out_specs=(blk(0), pl.BlockSpec((None, H, dk, dk), lambda n: (n, 0, 0, 0)),
                   pl.BlockSpec((None, H, CHUNK, CHUNK), lambda n: (n, 0, 0, 0))),
        out_shape=(SDS((S, hd), _MXU), SDS((N, H, dk, dk), F32), SDS((N, H, CHUNK, CHUNK), F32)),
        scratch_shapes=[pltpu.VMEM((H, dk, dk), F32)],
        compiler_params=_cparams(VMEM_BIG),
    )(qkv, qkv, qkv, pm, gates, nw)


def _delta_bwd(qkv, pm, gates, nw, states, t_invs, dog, H, name):
    S = qkv.shape[0]
    hd = qkv.shape[1] // 3
    dk = hd // H
    N = S // CHUNK
    blk = lambda off: pl.BlockSpec((CHUNK, hd), lambda n: (N - 1 - n, off))
    gspec = pl.BlockSpec((CHUNK, LANES), lambda n: (N - 1 - n, 0))

    def body(q_ref, k_ref, v_ref, z_ref, g_ref, nw_ref, st_ref, ti_ref, do_ref,
             dq_ref, dk_ref, dv_ref, dz_ref, dg_ref, dnw_ref, ds_scr):
        n = pl.program_id(0)

        @pl.when(n == 0)
        def _():
            ds_scr[...] = jnp.zeros_like(ds_scr)

        t_known = ti_ref[...]
        fn = functools.partial(_delta_chunk, t_known=t_known)
        _, vjp = jax.vjp(fn, _heads(q_ref, H, dk), _heads(k_ref, H, dk), _heads(v_ref, H, dk),
                         _heads(z_ref, H, dk), g_ref[...], nw_ref[...], st_ref[...])
        dq, dkk, dv, dz, dg, dnw, ds_in = vjp((_heads(do_ref, H, dk), ds_scr[...], jnp.zeros_like(t_known)))
        _put_heads(dq_ref, dq, dk)
        _put_heads(dk_ref, dkk, dk)
        _put_heads(dv_ref, dv, dk)
        _put_heads(dz_ref, dz, dk)
        ds_scr[...] = ds_in
        dg_ref[...] = dg

        @pl.when(n == 0)
        def _():
            dnw_ref[...] = dnw

        @pl.when(n != 0)
        def _():
            dnw_ref[...] += dnw

    return _pcall(
        body, name=name, grid=(N,),
        in_specs=[blk(0), blk(1), blk(2), blk(3), gspec, _full(nw),
                  pl.BlockSpec((None, H, dk, dk), lambda n: (N - 1 - n, 0, 0, 0)),
                  pl.BlockSpec((None, H, CHUNK, CHUNK), lambda n: (N - 1 - n, 0, 0, 0)), blk(0)],
        out_specs=(blk(0), blk(0), blk(0), blk(3), gspec, pl.BlockSpec((1, dk), lambda n: (0, 0))),
        out_shape=(SDS((S, hd), F32), SDS((S, hd), F32), SDS((S, hd), F32), SDS(pm.shape, F32),
                   SDS((S, LANES), F32), SDS((1, dk), F32)),
        scratch_shapes=[pltpu.VMEM((H, dk, dk), F32)],
        compiler_params=_cparams(VMEM_BIG),
    )(qkv, qkv, qkv, pm, gates, nw, states, t_invs, dog)


def _rows_block(R, C):
    rb = R
    while rb * C * 4 > (1 << 20) and rb % 16 == 0:
        rb //= 2
    return rb


def _sum_slots(st, name, out_dtype=F32):
    n, R, C = st.shape
    rb = _rows_block(R, C)

    def body(s_ref, o_ref):
        acc = s_ref[0].astype(F32)
        for q in range(1, n):
            acc = acc + s_ref[q].astype(F32)
        o_ref[...] = acc.astype(o_ref.dtype)

    return _pcall(body, name=name, grid=(R // rb,), in_specs=[pl.BlockSpec((n, rb, C), lambda i: (0, i, 0))],
                  out_specs=pl.BlockSpec((rb, C), lambda i: (i, 0)), out_shape=SDS((R, C), out_dtype))(st)


def _scalar(v):
    return jnp.reshape(v, (1,)).astype(jnp.int32)


def _place_block(w, layer, ax, chip, name):
    _, R, C = w.shape
    rb = _rows_block(R, C)
    nrb = R // rb
    shp = [R, C]
    shp[ax] *= 4
    omap = (lambda i, c: (c[0] * nrb + i, 0)) if ax == 0 else (lambda i, c: (i, c[0]))

    def body(c_ref, w_ref, o_ref):
        del c_ref
        o_ref[...] = w_ref[...].astype(o_ref.dtype)

    grid_spec = pltpu.PrefetchScalarGridSpec(
        num_scalar_prefetch=1, grid=(nrb,),
        in_specs=[pl.BlockSpec((None, rb, C), lambda i, c: (layer, i, 0))],
        out_specs=pl.BlockSpec((rb, C), omap))
    return _pcall(body, name=name, grid_spec=grid_spec, out_shape=SDS(tuple(shp), _MXU))(_scalar(chip), w)


def _sum_pair(own, recv, ic, name, out_dtype):
    N, _, R, C = own.shape
    rb = _rows_block(R, C)

    def body(c_ref, a_ref, b_ref, o_ref):
        del c_ref
        o_ref[...] = (a_ref[...].astype(F32) + b_ref[...].astype(F32)).astype(o_ref.dtype)

    grid_spec = pltpu.PrefetchScalarGridSpec(
        num_scalar_prefetch=1, grid=(N, R // rb),
        in_specs=[pl.BlockSpec((None, None, rb, C), lambda p, i, c: (p, c[0], i, 0)),
                  pl.BlockSpec((None, None, rb, C), lambda p, i, c: (1 - c[0], p, i, 0))],
        out_specs=pl.BlockSpec((None, rb, C), lambda p, i, c: (p, i, 0)))
    return _pcall(body, name=name, grid_spec=grid_spec, out_shape=SDS((N, R, C), out_dtype))(
        _scalar(ic), own, recv)


def _sum_landed(grad, ax, land, chip, name):
    _, R, Cb = land.shape
    rb = _rows_block(R, Cb)
    nrb = R // rb
    if ax == 0:
        own_spec = pl.BlockSpec((rb, Cb), lambda i, c: (c[0] * nrb + i, 0))
    else:
        own_spec = pl.BlockSpec((rb, Cb), lambda i, c: (i, c[0]))
    slot = lambda d: pl.BlockSpec((None, rb, Cb), lambda i, c: ((c[0] + d) % 4, i, 0))

    def body(c_ref, own_ref, r1, r2, r3, o_ref):
        del c_ref
        o_ref[...] = ((own_ref[...].astype(F32) + r1[...].astype(F32)) + r2[...].astype(F32)) + r3[...].astype(F32)

    grid_spec = pltpu.PrefetchScalarGridSpec(
        num_scalar_prefetch=1, grid=(R // rb,), in_specs=[own_spec, slot(1), slot(2), slot(3)],
        out_specs=pl.BlockSpec((rb, Cb), lambda i, c: (i, 0)))
    return _pcall(body, name=name, grid_spec=grid_spec, out_shape=SDS((R, Cb), F32))(
        _scalar(chip), grad, land, land, land)


def _adamw_step(g, w_ref, m_ref, v_ref, g_ref, d_ref, mo_ref, vo_ref):
    m_new = ADAM_B1 * m_ref[...] + (1.0 - ADAM_B1) * g
    v_new = ADAM_B2 * v_ref[...] + (1.0 - ADAM_B2) * (g * g)
    m_hat = m_new / (1.0 - ADAM_B1 ** ADAM_STEP)
    v_hat = v_new / (1.0 - ADAM_B2 ** ADAM_STEP)
    g_ref[...] = g
    d_ref[...] = -ADAM_LR * (m_hat / (jnp.sqrt(v_hat) + ADAM_EPS) + ADAM_WD * w_ref[...])
    mo_ref[...] = m_new
    vo_ref[...] = v_new


def _adamw(w, m, v, st, name):
    R, C = w.shape
    n = st.shape[0]
    rb = _rows_block(R, C)
    spec = pl.BlockSpec((rb, C), lambda i: (i, 0))

    def body(w_ref, m_ref, v_ref, s_ref, *o_refs):
        g = s_ref[0]
        for q in range(1, n):
            g = g + s_ref[q]
        _adamw_step(g, w_ref, m_ref, v_ref, *o_refs)

    return _pcall(body, name=name, grid=(R // rb,),
                  in_specs=[spec, spec, spec, pl.BlockSpec((n, rb, C), lambda i: (0, i, 0))],
                  out_specs=(spec,) * 4, out_shape=(SDS((R, C), F32),) * 4)(w, m, v, st)


def _adamw_pair(w, m, v, layer, own, recv2, ic, bufs, name):
    L, R, C = w.shape
    rb = _rows_block(R, C)
    spec = pl.BlockSpec((None, rb, C), lambda i, c: (layer, i, 0))

    def body(c_ref, w_ref, m_ref, v_ref, own_ref, recv_ref, *rest):
        del c_ref
        _adamw_step(own_ref[...] + recv_ref[...], w_ref, m_ref, v_ref, *rest[-4:])

    nbuf = 0 if bufs is None else 4
    grid_spec = pltpu.PrefetchScalarGridSpec(
        num_scalar_prefetch=1, grid=(R // rb,),
        in_specs=[spec, spec, spec, pl.BlockSpec((rb, C), lambda i, c: (i, 0)),
                  pl.BlockSpec((None, rb, C), lambda i, c: (1 - c[0], i, 0))] + [_ANY] * nbuf,
        out_specs=(spec,) * 4)
    extra = {} if bufs is None else dict(input_output_aliases={6 + q: q for q in range(4)})
    return _pcall(body, name=name, grid_spec=grid_spec, out_shape=(SDS((L, R, C), F32),) * 4, **extra)(
        _scalar(ic), w, m, v, own, recv2, *([] if bufs is None else bufs))


def _pack(arrs, rows=1):
    flat = jnp.concatenate([a.reshape(-1).astype(F32) for a in arrs])
    quantum = rows * LANES
    pad = (-flat.shape[0]) % quantum
    flat = jnp.pad(flat, (0, pad))
    return flat.reshape(rows, -1)


def _unpack(flat, shapes):
    flat = flat.reshape(-1)
    out, off = [], 0
    for shp in shapes:
        size = 1
        for d in shp:
            size *= d
        out.append(flat[off:off + size].reshape(shp))
        off += size
    return out


def _mlp_fwd(x1, mod, lnp, w1, w2, tag):
    h2 = _modulate(x1, mod, 3, 4, f"{tag}_mod")
    a1, a2 = _mm(h2, w1, "nn", name=f"{tag}_up", out_dtypes=(_MXU, _MXU),
                 out_fn=lambda r: (r, jnp.square(jnp.maximum(r, 0.0))))
    y2 = _mm(a2, w2, "nn", name=f"{tag}_down")
    x2 = _combine(x1, y2, mod, 5, lnp, 2, f"{tag}_ln")
    return x2, (x1, h2, a1, a2, y2)


def _weight_grad(grads, key, a, b, name):
    grads[key] = _mm(a, b, "tn", name=name, out_dtypes=(_MXU,))


def _mlp_bwd(dx2, saved, mod, lnp, w1, w2, tag, stacks):
    x1, h2, a1, a2, y2 = saved
    dxa, dy2, dp = _combine_bwd(x1, y2, mod, 5, lnp, 2, dx2, f"{tag}_ln_b")
    da1 = _mm(dy2, w2, "nt", name=f"{tag}_down_bx", out_dtypes=(_MXU,), aux=[(a1, "mn")],
              out_fn=lambda r, a: (r * (2.0 * jnp.maximum(a.astype(F32), 0.0)),))
    _weight_grad(stacks, "ff_w2", a2, dy2, f"{tag}_down_bw")
    _weight_grad(stacks, "ff_w1", h2, da1, f"{tag}_up_bw")
    dh2 = _mm(da1, w1, "nt", name=f"{tag}_up_bx")
    dx1, dss = _modulate_bwd(x1, mod, 4, dh2, dxa, f"{tag}_mod_b")
    return dx1, (dss, dp)


def _dn_fwd(x, mod, lnp, wts, H, tag):
    w_main, w_small, conv_w, prm, nw, w_out = wts
    h = _modulate(x, mod, 0, 1, f"{tag}_mod")
    pm = _mm(h, w_main, "nn", name=f"{tag}_in")
    ps = _mm(h, w_small, "nn", name=f"{tag}_in_s")
    nqkv = conv_w.shape[1] // LANES
    qkv = _conv_silu(pm, conv_w, nqkv, f"{tag}_conv")
    gates = _gates(ps, prm, H, f"{tag}_gates")
    og, *states = _delta_fwd(qkv, pm, gates, nw, H, f"{tag}_delta")
    y = _mm(og, w_out, "nn", name=f"{tag}_out")
    x1 = _combine(x, y, mod, 2, lnp, 0, f"{tag}_ln")
    return x1, (x, h, pm, ps, qkv, gates, states, og, y)


def _dn_bwd(dx1, saved, mod, lnp, wts, H, tag, stacks):
    w_main, w_small, conv_w, prm, nw, w_out = wts
    x, h, pm, ps, qkv, gates, states, og, y = saved
    dxa, dy, dp = _combine_bwd(x, y, mod, 2, lnp, 0, dx1, f"{tag}_ln_b")
    dog = _mm(dy, w_out, "nt", name=f"{tag}_out_bx")
    _weight_grad(stacks, "dn_w_out", og, dy, f"{tag}_out_bw")
    dq, dk, dv, dpm, dgates, dnw = _delta_bwd(qkv, pm, gates, nw, *states, dog, H, f"{tag}_delta_b")
    dps, dprm = _gates_bwd(ps, prm, dgates, H, f"{tag}_gates_b")
    dcw = []
    nb = dq.shape[1] // LANES
    for part, dpart in enumerate((dq, dk, dv)):
        dpm, dcw_p = _conv_silu_bwd(pm, conv_w, dpart, dpm, part * nb, f"{tag}_conv_b{part}")
        dcw.append(dcw_p)
    dconv_w = jnp.concatenate(dcw, axis=1)
    dw_main = _mm(h, dpm, "tn", name=f"{tag}_in_bw", out_dtypes=(_MXU,))
    dw_small = _mm(h, dps, "tn", name=f"{tag}_in_s_bw", out_dtypes=(_MXU,))
    dh_s = _mm(dps, w_small, "nt", name=f"{tag}_in_s_bx")
    dh = _mm(dpm, w_main, "nt", name=f"{tag}_in_bx", aux=[(dh_s, "mn")], out_fn=lambda r, e: (r + e,))
    dx, dss = _modulate_bwd(x, mod, 1, dh, dxa, f"{tag}_mod_b")
    return dx, (dw_main, dw_small, dconv_w, dprm, dnw), (dss, dp)


def _cf_fwd(x, mod, lnp, wts, tag):
    w_in, dw_w, dw_b, cln, w_out = wts
    h = _modulate(x, mod, 0, 1, f"{tag}_mod")
    p = _mm(h, w_in, "nn", name=f"{tag}_in")
    u2 = _glu_conv(p, dw_w, dw_b, f"{tag}_conv")
    u3 = _ln_silu(u2, cln, f"{tag}_cln")
    y = _mm(u3, w_out, "nn", name=f"{tag}_out")
    x1 = _combine(x, y, mod, 2, lnp, 0, f"{tag}_ln")
    return x1, (x, h, p, u2, u3, y)


def _cf_bwd(dx1, saved, mod, lnp, wts, tag, stacks):
    w_in, dw_w, dw_b, cln, w_out = wts
    x, h, p, u2, u3, y = saved
    dxa, dy, dp = _combine_bwd(x, y, mod, 2, lnp, 0, dx1, f"{tag}_ln_b")
    du3 = _mm(dy, w_out, "nt", name=f"{tag}_out_bx")
    _weight_grad(stacks, "cf_w_out", u3, dy, f"{tag}_out_bw")
    du2, dcln = _ln_silu_bwd(u2, cln, du3, f"{tag}_cln_b")
    dval, dgate, ddw_w, ddw_b = _glu_conv_bwd(p, dw_w, du2, f"{tag}_conv_b")
    dpp = jnp.concatenate([dval, dgate], axis=1)
    _weight_grad(stacks, "cf_w_in", h, dpp, f"{tag}_in_bw")
    dh = _mm(dpp, w_in, "nt", name=f"{tag}_in_bx")
    dx, dss = _modulate_bwd(x, mod, 1, dh, dxa, f"{tag}_mod_b")
    return dx, (ddw_w, ddw_b, dcln), (dss, dp)


def _two_d(a):
    return a.reshape(-1, a.shape[-1])


def kernel(x, c, ada_w, ada_b, ln_g, ln_b, dn_w_in, dn_conv_w, dn_a_log, dn_dt_bias, dn_norm_w, dn_w_out, cf_w_in, cf_dw_w, cf_dw_b, cf_ln_g, cf_ln_b, cf_w_out, ff_w1, ff_w2, loss_target, m_ada_w, m_ada_b, m_ln_g, m_ln_b, m_dn_w_in, m_dn_conv_w, m_dn_a_log, m_dn_dt_bias, m_dn_norm_w, m_dn_w_out, m_cf_w_in, m_cf_dw_w, m_cf_dw_b, m_cf_ln_g, m_cf_ln_b, m_cf_w_out, m_ff_w1, m_ff_w2, v_ada_w, v_ada_b, v_ln_g, v_ln_b, v_dn_w_in, v_dn_conv_w, v_dn_a_log, v_dn_dt_bias, v_dn_norm_w, v_dn_w_out, v_cf_w_in, v_cf_dw_w, v_cf_dw_b, v_cf_ln_g, v_cf_ln_b, v_cf_w_out, v_ff_w1, v_ff_w2):
    ix, iy, ic = lax.axis_index("x"), lax.axis_index("y"), lax.axis_index("c")
    chip = 2 * ix + iy
    dev = 4 * ix + 2 * iy + ic
    S, D = x.shape[1], x.shape[2]
    L = ada_w.shape[0]
    LA, LB = dn_w_in.shape[0], cf_w_in.shape[0]
    H = dn_a_log.shape[1]
    NMOD = ada_b.shape[1] // D
    dn_in = dn_w_in.shape[2] * 4
    n_main = dn_in - 2 * H
    assert L == N_LAYERS and 2 * H <= LANES
    x0, tgt = x[0], loss_target[0]

    small_sharded = [ln_g, ln_b, dn_conv_w, cf_dw_w, cf_dw_b, cf_ln_g, cf_ln_b]
    small_axes = [2, 2, 2, 2, 1, 1, 1]
    packed_small = _pack(small_sharded, rows=8)[None]
    big = {"dn_w_in": (dn_w_in, m_dn_w_in, v_dn_w_in, 0), "dn_w_out": (dn_w_out, m_dn_w_out, v_dn_w_out, 0),
           "cf_w_in": (cf_w_in, m_cf_w_in, v_cf_w_in, 1), "cf_w_out": (cf_w_out, m_cf_w_out, v_cf_w_out, 0),
           "ff_w1": (ff_w1, m_ff_w1, v_ff_w1, 1), "ff_w2": (ff_w2, m_ff_w2, v_ff_w2, 0)}

    def group(g):
        l = g // 2
        if g % 2:
            return {"ff_w1": l, "ff_w2": l}
        mixer = ("dn_w_in", "dn_w_out") if l % 2 == 0 else ("cf_w_in", "cf_w_out")
        return {mixer[0]: l // 2, mixer[1]: l // 2}

    def start_gather(l, after):
        names = list(group(l))
        axes = [big[nm][3] for nm in names]
        placed = [_place_block(big[nm][0], group(l)[nm], ax, chip, f"l{l}_place_{nm}") for nm, ax in zip(names, axes)]
        send, recv, arrs, token = _gather_start(list(zip(placed, axes)), after, f"l{l}_gather_start")
        return names, axes, arrs, send, recv, token

    def finish_gather(l, pending, after):
        names, axes, arrs, send, recv, _ = pending
        arrs = _gather_wait(list(zip(arrs, axes)), send, recv, after, f"l{l}_gather_wait")
        arrs = _gather_forward(list(zip(arrs, axes)), f"l{l}_gather_pass")
        return dict(zip(names, arrs))

    g_small = _exchange([(packed_small, 0)], "xy", "gather", "gather_small")[0]
    shard_shapes = [a.shape for a in small_sharded]
    per_chip = [_unpack(g_small[q], shard_shapes) for q in range(4)]
    ln_g_f, ln_b_f, conv_w_f, dw_w_f, dw_b_f, cln_g_f, cln_b_f = [
        jnp.concatenate([per_chip[q][i] for q in range(4)], axis=small_axes[i]) for i in range(len(small_sharded))]

    c_all = _exchange([(c[None], 0)], "all", "gather", "gather_cond")[0].reshape(8, D)
    c_pad = jnp.pad(c_all, ((0, 8), (0, 0)))
    mod_sh = jnp.stack([_mm(c_pad, (ada_w, l), "nn", name=f"ada_{l}", a_fn=lambda t: t * _sigmoid(t))
                        for l in range(L)])
    mod_all = _exchange([(mod_sh, 2)], "xy", "gather", "gather_mod")[0]
    mod_mine = lax.dynamic_index_in_dim(mod_all, dev, axis=1, keepdims=False) + ada_b
    mods = mod_mine.reshape(L, NMOD, D)

    def lnp_of(l):
        return jnp.stack([ln_g_f[l, 0], ln_b_f[l, 0], ln_g_f[l, 1], ln_b_f[l, 1]])

    def mixer_wts(l, wl):
        j = l // 2
        if l % 2 == 0:
            w_in = jnp.transpose(wl["dn_w_in"].reshape(4, D, dn_in // 4), (1, 0, 2)).reshape(D, dn_in)
            w_small = jnp.pad(w_in[:, n_main:], ((0, 0), (0, LANES - 2 * H)))
            prm = jnp.zeros((2, LANES), F32).at[0, H:2 * H].set(dn_a_log[j]).at[1, H:2 * H].set(dn_dt_bias[j])
            return (w_in[:, :n_main], w_small, conv_w_f[j], prm, dn_norm_w[j][None], wl["dn_w_out"])
        return (wl["cf_w_in"], dw_w_f[j], dw_b_f[j][None], jnp.stack([cln_g_f[j], cln_b_f[j]]), wl["cf_w_out"])

    xs = x0
    saved, wts, mod_of = [], [], []
    pending = start_gather(0, mods + jnp.minimum(jnp.abs(g_small[0, 0, 0]), 0.0))
    for g in range(2 * L):
        l = g // 2
        wl = finish_gather(g, pending, xs)
        mod_g = mods[l]
        if g + 1 < 2 * L:
            pending = start_gather(g + 1, next(iter(wl.values())))
            mod_g = mod_g + pending[5][0, 0]
        mod_of.append(mod_g)
        if g % 2:
            wts.append((wl["ff_w1"], wl["ff_w2"]))
            xs, sv = _mlp_fwd(xs, mod_g, lnp_of(l), *wts[g], f"l{l}_ff")
        elif l % 2 == 0:
            wts.append(mixer_wts(l, wl))
            xs, sv = _dn_fwd(xs, mod_g, lnp_of(l), wts[g], H, f"l{l}_dn")
        else:
            wts.append(mixer_wts(l, wl))
            xs, sv = _cf_fwd(xs, mod_g, lnp_of(l), wts[g], f"l{l}_cf")
        saved.append(sv)
    dx, loss_local = _loss_head(xs, tgt, "loss_head")
    loss = lax.psum(loss_local[0, 0], ("x", "y", "c"))

    def start_scatter(l, grads):
        names = list(group(l))
        axes = [big[nm][3] for nm in names]
        send, recv, sums, lands, token = _scatter_start([(grads[nm], ax) for nm, ax in zip(names, axes)], mods,
                                                        f"l{l}_scatter_start")
        return names, axes, sums, lands, send, recv, token

    def finish_scatter(l, pending, after):
        names, axes, sums, lands, send, recv, _ = pending
        sums, lands = _scatter_wait(list(zip(sums, axes)), lands, send, recv, after, f"l{l}_scatter_wait")
        mine = [_sum_landed(s, ax, ld, chip, f"l{l}_sum_grads_{nm}") for nm, ax, s, ld in zip(names, axes, sums, lands)]
        other = _exchange([(s[None], 0) for s in mine], "c", "gather", f"l{l}_swap_sums", keep_own=False)
        return {nm: (s, o) for nm, s, o in zip(names, mine, other)}

    g_dn = [None] * LA
    g_cf = [None] * LB
    dmods, dlns = [None] * L, [None] * L
    big_sums = {}
    pending, token = None, None
    small_of = {}
    for g in reversed(range(2 * L)):
        l, j = g // 2, g // 4
        mod_g = mod_of[g] if token is None else mod_of[g] + token[0, 0]
        grads = {}
        if g % 2:
            dx, small_of[g] = _mlp_bwd(dx, saved[g], mod_g, lnp_of(l), *wts[g], f"l{l}_ff", grads)
        elif l % 2 == 0:
            dx, g_dn[j], small_of[g] = _dn_bwd(dx, saved[g], mod_g, lnp_of(l), wts[g], H, f"l{l}_dn", grads)
            dn_in_g = jnp.concatenate([g_dn[j][0], g_dn[j][1][:, :2 * H]], axis=1)
            grads["dn_w_in"] = jnp.transpose(dn_in_g.reshape(D, 4, dn_in // 4), (1, 0, 2)).reshape(4 * D, dn_in // 4)
        else:
            dx, g_cf[j], small_of[g] = _cf_bwd(dx, saved[g], mod_g, lnp_of(l), wts[g], f"l{l}_cf", grads)
        if pending is not None:
            for nm, pair_of in finish_scatter(g + 1, pending, dx).items():
                big_sums[nm, group(g + 1)[nm]] = pair_of
        pending = start_scatter(g, grads)
        token = pending[6]
    for nm, pair_of in finish_scatter(0, pending, dx).items():
        big_sums[nm, group(0)[nm]] = pair_of
    for l in range(L):
        (dss1, dp1), (dss2, dp2) = small_of[2 * l], small_of[2 * l + 1]
        dmods[l] = jnp.concatenate([dss1, dp1[0:1], dss2, dp2[0:1]], axis=0)
        dlns[l] = (jnp.stack([dp1[1], dp2[1]]), jnp.stack([dp1[2], dp2[2]]))
    grad_x = dx[None]

    d_ln_g = jnp.stack([dlns[l][0] for l in range(L)])
    d_ln_b = jnp.stack([dlns[l][1] for l in range(L)])
    d_conv_w = jnp.stack([g_dn[j][2] for j in range(LA)])
    d_a_log = jnp.stack([g_dn[j][3][0, H:2 * H] for j in range(LA)])
    d_dt_bias = jnp.stack([g_dn[j][3][1, H:2 * H] for j in range(LA)])
    d_norm_w = jnp.stack([g_dn[j][4][0] for j in range(LA)])
    d_dw_w = jnp.stack([g_cf[j][0] for j in range(LB)])
    d_dw_b = jnp.stack([g_cf[j][1][0] for j in range(LB)])
    d_cln_g = jnp.stack([g_cf[j][2][0] for j in range(LB)])
    d_cln_b = jnp.stack([g_cf[j][2][1] for j in range(LB)])
    d_mod = jnp.stack(dmods).reshape(L, NMOD * D)
    small_full = [d_mod, d_ln_g, d_ln_b, d_conv_w, d_dw_w, d_dw_b, d_cln_g, d_cln_b, d_a_log, d_dt_bias, d_norm_w]
    small_all = _exchange([(_pack(small_full, rows=8)[None], 0)], "all", "gather", "gather_small_grads")[0]
    small_sum = _sum_slots(small_all, "sum_small_grads")
    (s_ada_b, s_ln_g, s_ln_b, s_conv_w, s_dw_w, s_dw_b, s_cln_g, s_cln_b, s_a_log, s_dt_bias, s_norm_w) = _unpack(
        small_sum, [a.shape for a in small_full])
    d_mod_all = small_all.reshape(8, -1)[:, :L * NMOD * D].reshape(8, L, NMOD * D)

    def shard(a, axis):
        size = a.shape[axis] // 4
        return lax.dynamic_slice_in_dim(a, chip * size, size, axis)

    ncol = ada_w.shape[2]
    d_mod_sh = jnp.pad(lax.dynamic_slice_in_dim(d_mod_all, chip * ncol, ncol, 2), ((0, 8), (0, 0), (0, 0)))
    g_ada_w = None
    for l in range(L):
        g_ada_w = _mm(c_pad, d_mod_sh[:, l], "tn", name=f"ada_bw_{l}", a_fn=lambda t: t * _sigmoid(t),
                      stack=(g_ada_w, l, L))

    big_out = []
    for nm in ["dn_w_in", "dn_w_out", "cf_w_in", "cf_w_out", "ff_w1", "ff_w2"]:
        w, m, v, _ = big[nm]
        res = None
        for layer in range(w.shape[0]):
            own, oth = big_sums[nm, layer]
            res = _adamw_pair(w, m, v, layer, own, oth, ic, res, f"adamw_{nm}_{layer}")
        big_out.append(res)
    ada_out = [r.reshape(ada_w.shape) for r in _adamw(_two_d(ada_w), _two_d(m_ada_w), _two_d(v_ada_w),
                                                     _two_d(g_ada_w)[None], "adamw_ada_w")]

    small_w = [(ada_b, m_ada_b, v_ada_b, s_ada_b), (ln_g, m_ln_g, v_ln_g, shard(s_ln_g, 2)),
               (ln_b, m_ln_b, v_ln_b, shard(s_ln_b, 2)), (dn_conv_w, m_dn_conv_w, v_dn_conv_w, shard(s_conv_w, 2)),
               (dn_a_log, m_dn_a_log, v_dn_a_log, s_a_log), (dn_dt_bias, m_dn_dt_bias, v_dn_dt_bias, s_dt_bias),
               (dn_norm_w, m_dn_norm_w, v_dn_norm_w, s_norm_w), (cf_dw_w, m_cf_dw_w, v_cf_dw_w, shard(s_dw_w, 2)),
               (cf_dw_b, m_cf_dw_b, v_cf_dw_b, shard(s_dw_b, 1)), (cf_ln_g, m_cf_ln_g, v_cf_ln_g, shard(s_cln_g, 1)),
               (cf_ln_b, m_cf_ln_b, v_cf_ln_b, shard(s_cln_b, 1))]
    pk = [_pack([t[i] for t in small_w], rows=8) for i in range(4)]
    small_res = _adamw(pk[0], pk[1], pk[2], pk[3][None], "adamw_small")
    small_shapes = [t[0].shape for t in small_w]
    small_out = [_unpack(r, small_shapes) for r in small_res]

    def kind(k):
        sm = small_out[k]
        bg = [o[k] for o in big_out]
        return [ada_out[k], sm[0], sm[1], sm[2], bg[0], sm[3], sm[4], sm[5], sm[6], bg[1],
                bg[2], sm[7], sm[8], sm[9], sm[10], bg[3], bg[4], bg[5]]

    return (loss, grad_x, *kind(0), *kind(1), *kind(2), *kind(3))
```

```python
import functools

import jax
import jax.numpy as jnp
from jax import lax
from jax.experimental import pallas as pl
from jax.experimental.pallas import tpu as pltpu

F32 = jnp.float32
_MXU = jnp.bfloat16
_HI = lax.Precision.HIGHEST

N_LAYERS = 4
ALPHA = (2.0 * N_LAYERS) ** 0.25
LN_EPS = 1e-5
RMS_EPS = 1e-6
L2_EPS = 1e-6
CHUNK = 64
ADAM_LR, ADAM_B1, ADAM_B2, ADAM_EPS, ADAM_WD, ADAM_STEP = 0.001, 0.9, 0.999, 1e-08, 0.01, 10

LANES = 128
TOKEN_BLOCK = 256
VMEM_BIG = 48 * 1024 * 1024
MM_TILE = 1024

SDS = jax.ShapeDtypeStruct
MESH = pl.DeviceIdType.MESH


def _cparams(vmem=None):
    if vmem is None:
        return None
    return pltpu.CompilerParams(vmem_limit_bytes=vmem)


def _pcall(body, **kw):
    if kw.get("compiler_params", 1) is None:
        kw.pop("compiler_params")
    return pl.pallas_call(body, **kw)


def _full(arr):
    nd = arr.ndim
    return pl.BlockSpec(arr.shape, lambda *g: (0,) * nd)


def _bs(block, imap, lead=None):
    if lead is None:
        return pl.BlockSpec(block, imap)
    return pl.BlockSpec((None,) + tuple(block), lambda *g: (lead,) + tuple(imap(*g)))


def _split(a):
    return a if isinstance(a, tuple) else (a, None)


_GROUPS = {
    "xy": ([(1, 0, 0), (0, 1, 0), (1, 1, 0)], 4),
    "c": ([(0, 0, 1)], 2),
    "all": ([(1, 0, 0), (0, 1, 0), (1, 1, 0), (0, 0, 1), (1, 0, 1), (0, 1, 1), (1, 1, 1)], 8),
}


def _exchange(items, group, mode, name, nsplit=1, keep_own=True):
    masks, n = _GROUPS[group]
    npeer = len(masks)
    ni = len(items)
    arrs = [a for a, _ in items]
    out_shapes = []
    for a, ax in items:
        shp = list(a.shape)
        if mode == "gather":
            shp[ax] *= n
        else:
            shp[ax] //= n
            shp = [n] + shp
        out_shapes.append(SDS(tuple(shp), a.dtype))

    def body(*refs):
        ins, outs = refs[:ni], refs[ni:2 * ni]
        send_sems, recv_sems, local_sems = refs[2 * ni:]
        x, y, c = lax.axis_index("x"), lax.axis_index("y"), lax.axis_index("c")

        def slot(px, py, pc):
            if group == "xy":
                return 2 * px + py
            if group == "c":
                return pc
            return 4 * px + 2 * py + pc

        me = slot(x, y, c)

        def block(ref, ax, idx, size):
            ix = (slice(None),) * ax + (pl.ds(pl.multiple_of(idx * size, size), size),)
            return ref.at[ix]

        copies = []
        for it, (a, ax) in enumerate(items):
            in_ref, out_ref = ins[it], outs[it]
            if mode == "gather":
                size = a.shape[ax]
                src_own, dst_own = in_ref, block(out_ref, ax, me, size)
            else:
                size = a.shape[ax] // n
                src_own, dst_own = block(in_ref, ax, me, size), out_ref.at[me]
            sax, ns, cs = splits[it]
            pieces = [(slice(None),) * sax + (pl.ds(j * cs, cs),) for j in range(ns)]
            if keep_own:
                for j, piece in enumerate(pieces):
                    own = pltpu.make_async_copy(src_own.at[piece], dst_own.at[piece], local_sems.at[it * nsplit + j])
                    own.start()
                    copies.append(own)
            for k, m in enumerate(masks):
                peer = tuple((1 - v) if b else v for v, b in zip((x, y, c), m))
                if mode == "gather":
                    src, dst = in_ref, dst_own
                else:
                    src, dst = block(in_ref, ax, slot(*peer), size), out_ref.at[me]
                for j, piece in enumerate(pieces):
                    sem = (it * npeer + k) * nsplit + j
                    cp = pltpu.make_async_remote_copy(
                        src_ref=src.at[piece], dst_ref=dst.at[piece], send_sem=send_sems.at[sem],
                        recv_sem=recv_sems.at[sem], device_id=peer, device_id_type=MESH)
                    cp.start()
                    copies.append(cp)
        for cp in copies:
            cp.wait()

    splits = []
    for (a, ax), o in zip(items, out_shapes):
        bshape = a.shape if mode == "gather" else o.shape[1:]
        sax = max(range(len(bshape) - 1), key=lambda d: bshape[d])
        ns = nsplit if bshape[sax] % (nsplit * 16) == 0 else 1
        splits.append((sax, ns, bshape[sax] // ns))
    any_spec = pl.BlockSpec(memory_space=pl.ANY)
    nsem = ni * npeer * nsplit
    outs = _pcall(
        body, name=name, out_shape=tuple(out_shapes),
        in_specs=[any_spec] * ni, out_specs=tuple([any_spec] * ni),
        scratch_shapes=[pltpu.SemaphoreType.DMA((nsem,)), pltpu.SemaphoreType.DMA((nsem,)),
                        pltpu.SemaphoreType.DMA((ni * nsplit,))],
    )(*arrs)
    return list(outs)


_HBM = pl.BlockSpec(memory_space=pltpu.HBM)
_SEM = pl.BlockSpec(memory_space=pltpu.SEMAPHORE)
_ANY = pl.BlockSpec(memory_space=pl.ANY)
_SPLIT = pltpu.CompilerParams(has_side_effects=pltpu.SideEffectType.DATAFLOW_SIDE_EFFECTING)
_XY = _GROUPS["xy"][0]


def _in_hbm(a):
    return pltpu.with_memory_space_constraint(a, pltpu.HBM)


def _chip_peers():
    x, y, c = lax.axis_index("x"), lax.axis_index("y"), lax.axis_index("c")
    return x, y, c, [tuple((1 - v) if b else v for v, b in zip((x, y), m)) for m in _XY]


def _wblock(ref, ax, half, chip):
    R, C = ref.shape
    if ax == 0:
        rows = R // 8
        return ref.at[pl.ds(pl.multiple_of(chip * (2 * rows) + half * rows, rows), rows), :]
    rows, cols = R // 2, C // 4
    return ref.at[pl.ds(pl.multiple_of(half * rows, rows), rows), pl.ds(pl.multiple_of(chip * cols, cols), cols)]


def _gather_start(items, after, name):
    ni = len(items)
    arrs = [a for a, _ in items]

    def body(*refs):
        send_sems, recv_sems = refs[ni + 1], refs[ni + 2]
        outs, token = refs[ni + 3:2 * ni + 3], refs[2 * ni + 3]
        x, y, c, peers = _chip_peers()
        for it, (_, ax) in enumerate(items):
            mine = _wblock(outs[it], ax, c, 2 * x + y)
            for k, (px, py) in enumerate(peers):
                pltpu.make_async_remote_copy(
                    src_ref=mine, dst_ref=mine, send_sem=send_sems.at[it * 3 + k], recv_sem=recv_sems.at[it * 3 + k],
                    device_id=(px, py, c), device_id_type=MESH).start()
        token[...] = jnp.zeros_like(token)

    res = _pcall(
        body, name=name,
        out_shape=(pltpu.SemaphoreType.DMA((ni * 3,)), pltpu.SemaphoreType.DMA((ni * 3,)),
                   *[pltpu.HBM(a.shape, a.dtype) for a in arrs], SDS((8, LANES), F32)),
        in_specs=[_HBM] * ni + [_ANY],
        out_specs=(_SEM, _SEM, *[_HBM] * ni, pl.BlockSpec(memory_space=pltpu.VMEM)),
        input_output_aliases={i: 2 + i for i in range(ni)}, compiler_params=_SPLIT,
    )(*[_in_hbm(a) for a in arrs], after)
    return res[0], res[1], list(res[2:2 + ni]), res[2 + ni]


def _gather_wait(items, send_sems, recv_sems, after, name):
    ni = len(items)
    arrs = [a for a, _ in items]

    def body(*refs):
        s_sems, r_sems = refs[ni], refs[ni + 1]
        outs = refs[ni + 3:]
        x, y, c, peers = _chip_peers()
        for it, (_, ax) in enumerate(items):
            mine = _wblock(outs[it], ax, c, 2 * x + y)
            for k, (px, py) in enumerate(peers):
                cp = pltpu.make_async_remote_copy(
                    src_ref=mine, dst_ref=_wblock(outs[it], ax, c, 2 * px + py), send_sem=s_sems.at[it * 3 + k],
                    recv_sem=r_sems.at[it * 3 + k], device_id=(px, py, c), device_id_type=MESH)
                cp.wait_send()
                cp.wait_recv()

    res = _pcall(
        body, name=name, out_shape=tuple(pltpu.HBM(a.shape, a.dtype) for a in arrs),
        in_specs=[_HBM] * ni + [_SEM, _SEM, _ANY], out_specs=tuple([_HBM] * ni),
        input_output_aliases={i: i for i in range(ni)}, compiler_params=_SPLIT,
    )(*arrs, send_sems, recv_sems, after)
    return list(res)


def _gather_forward(items, name):
    ni = len(items)
    arrs = [a for a, _ in items]

    def body(*refs):
        outs = refs[ni:2 * ni]
        send_sems, recv_sems = refs[2 * ni:]
        x, y, c, peers = _chip_peers()
        def copy(it, ax, k, chip, dst_half):
            return pltpu.make_async_remote_copy(
                src_ref=_wblock(outs[it], ax, c, chip), dst_ref=_wblock(outs[it], ax, dst_half, chip),
                send_sem=send_sems.at[it * 3 + k], recv_sem=recv_sems.at[it * 3 + k],
                device_id=(x, y, 1 - c), device_id_type=MESH)

        for it, (_, ax) in enumerate(items):
            for k, (px, py) in enumerate(peers):
                copy(it, ax, k, 2 * px + py, c).start()
        for it, (_, ax) in enumerate(items):
            for k, (px, py) in enumerate(peers):
                copy(it, ax, k, 2 * px + py, c).wait_send()
                copy(it, ax, k, 2 * px + py, 1 - c).wait_recv()

    res = _pcall(
        body, name=name, out_shape=tuple(SDS(a.shape, a.dtype) for a in arrs),
        in_specs=[_ANY] * ni, out_specs=tuple([_ANY] * ni), input_output_aliases={i: i for i in range(ni)},
        scratch_shapes=[pltpu.SemaphoreType.DMA((ni * 3,)), pltpu.SemaphoreType.DMA((ni * 3,))],
    )(*arrs)
    return list(res)


def _gblock(ref, ax, chip):
    size = ref.shape[ax] // 4
    piece = pl.ds(pl.multiple_of(chip * size, size), size)
    return ref.at[piece, :] if ax == 0 else ref.at[:, piece]


def _scatter_start(items, after, name):
    ni = len(items)
    arrs = [a for a, _ in items]
    lands = []
    for a, ax in items:
        blk = (a.shape[0] // 4, a.shape[1]) if ax == 0 else (a.shape[0], a.shape[1] // 4)
        lands.append(lax.empty((4, *blk), a.dtype))

    def body(*refs):
        send_sems, recv_sems = refs[2 * ni + 1], refs[2 * ni + 2]
        srcs, dsts = refs[2 * ni + 3:3 * ni + 3], refs[3 * ni + 3:4 * ni + 3]
        token = refs[4 * ni + 3]
        x, y, c, peers = _chip_peers()
        for it, (_, ax) in enumerate(items):
            for k, (px, py) in enumerate(peers):
                pltpu.make_async_remote_copy(
                    src_ref=_gblock(srcs[it], ax, 2 * px + py), dst_ref=dsts[it].at[2 * x + y],
                    send_sem=send_sems.at[it * 3 + k], recv_sem=recv_sems.at[it * 3 + k],
                    device_id=(px, py, c), device_id_type=MESH).start()
        token[...] = jnp.zeros_like(token)

    res = _pcall(
        body, name=name,
        out_shape=(pltpu.SemaphoreType.DMA((ni * 3,)), pltpu.SemaphoreType.DMA((ni * 3,)),
                   *[pltpu.HBM(a.shape, a.dtype) for a in arrs], *[pltpu.HBM(a.shape, a.dtype) for a in lands],
                   SDS((8, LANES), F32)),
        in_specs=[_HBM] * (2 * ni) + [_ANY],
        out_specs=(_SEM, _SEM, *[_HBM] * (2 * ni), pl.BlockSpec(memory_space=pltpu.VMEM)),
        input_output_aliases={i: 2 + i for i in range(2 * ni)}, compiler_params=_SPLIT,
    )(*[_in_hbm(a) for a in arrs], *[_in_hbm(a) for a in lands], after)
    return res[0], res[1], list(res[2:2 + ni]), list(res[2 + ni:2 + 2 * ni]), res[2 + 2 * ni]


def _scatter_wait(items, lands, send_sems, recv_sems, after, name):
    ni = len(items)
    arrs = [a for a, _ in items]

    def body(*refs):
        s_sems, r_sems = refs[2 * ni], refs[2 * ni + 1]
        srcs, dsts = refs[2 * ni + 3:3 * ni + 3], refs[3 * ni + 3:]
        x, y, c, peers = _chip_peers()
        for it, (_, ax) in enumerate(items):
            for k, (px, py) in enumerate(peers):
                cp = pltpu.make_async_remote_copy(
                    src_ref=_gblock(srcs[it], ax, 2 * px + py), dst_ref=dsts[it].at[2 * px + py],
                    send_sem=s_sems.at[it * 3 + k], recv_sem=r_sems.at[it * 3 + k],
                    device_id=(px, py, c), device_id_type=MESH)
                cp.wait_send()
                cp.wait_recv()

    res = _pcall(
        body, name=name, out_shape=tuple(pltpu.HBM(a.shape, a.dtype) for a in arrs + list(lands)),
        in_specs=[_HBM] * (2 * ni) + [_SEM, _SEM, _ANY], out_specs=tuple([_HBM] * (2 * ni)),
        input_output_aliases={i: i for i in range(2 * ni)}, compiler_params=_SPLIT,
    )(*arrs, *lands, send_sems, recv_sems, after)
    return list(res[:ni]), list(res[ni:])


def _tile(n, cap):
    if n <= cap:
        return n
    t = cap - cap % LANES
    while n % t:
        t -= LANES
    return t


def _mm(a, b, mode, *, name, out_dtypes=(F32,), tm=MM_TILE, tn=MM_TILE, tk=MM_TILE, a_fn=None, out_fn=None, aux=(),
        stack=None):
    (a, a_lead), (b, b_lead) = _split(a), _split(b)
    ash, bsh = a.shape[-2:], b.shape[-2:]
    if mode == "nn":
        (M, K), (_, N) = ash, bsh
    elif mode == "nt":
        (M, K), (N, _) = ash, bsh
    else:
        (K, M), (_, N) = ash, bsh
    tm, tn, tk = _tile(M, tm), _tile(N, tn), _tile(K, tk)
    nk = K // tk
    if mode == "tn":
        a_spec = _bs((tk, tm), lambda i, j, k: (k, i), a_lead)
    else:
        a_spec = _bs((tm, tk), lambda i, j, k: (i, k), a_lead)
    if mode == "nt":
        b_spec = _bs((tn, tk), lambda i, j, k: (j, k), b_lead)
    else:
        b_spec = _bs((tk, tn), lambda i, j, k: (k, j), b_lead)
    aux_arrs, aux_specs = [], []
    for arr, kind in aux:
        arr, lead = _split(arr)
        aux_arrs.append(arr)
        if kind == "mn":
            aux_specs.append(_bs((tm, tn), lambda i, j, k: (i, j), lead))
        else:
            aux_specs.append(_bs((1, tn), lambda i, j, k: (0, j), lead))
    na, no = len(aux_arrs), len(out_dtypes)
    dims = {"nn": (((1,), (0,)), ((), ())), "nt": (((1,), (1,)), ((), ())), "tn": (((0,), (0,)), ((), ()))}[mode]

    def finish(r, aux_refs, o_refs):
        outs = out_fn(r, *[x[...] for x in aux_refs]) if out_fn is not None else (r,)
        for o_ref, val in zip(o_refs, outs):
            o_ref[...] = val.astype(o_ref.dtype)

    def product(a_ref, b_ref):
        av = a_ref[...]
        if a_fn is not None:
            av = a_fn(av.astype(F32))
        return lax.dot_general(av.astype(_MXU), b_ref[...].astype(_MXU), dims, preferred_element_type=F32)

    nbuf = 0 if stack is None or stack[0] is None else 1

    def body_one(a_ref, b_ref, *rest):
        finish(product(a_ref, b_ref), rest[:na], rest[na + nbuf:na + nbuf + no])

    def body_acc(a_ref, b_ref, *rest):
        aux_refs, o_refs, acc = rest[:na], rest[na + nbuf:na + nbuf + no], rest[na + nbuf + no]
        k = pl.program_id(2)

        @pl.when(k == 0)
        def _():
            acc[...] = product(a_ref, b_ref)

        @pl.when(k != 0)
        def _():
            acc[...] += product(a_ref, b_ref)

        @pl.when(k == nk - 1)
        def _():
            finish(acc[...], aux_refs, o_refs)

    extra, aliases = {}, []
    if stack is None:
        out_shape = (M, N)
        o_spec = pl.BlockSpec((tm, tn), lambda i, j, k: (i, j))
    else:
        buf, layer, n_layers = stack
        assert no == 1
        out_shape = (n_layers, M, N)
        o_spec = pl.BlockSpec((None, tm, tn), lambda i, j, k: (layer, i, j))
        if buf is not None:
            aliases = [buf]
            extra = dict(input_output_aliases={2 + na: 0})
    outs = _pcall(
        body_one if nk == 1 else body_acc, name=name, grid=(M // tm, N // tn, nk),
        in_specs=[a_spec, b_spec] + aux_specs + [pl.BlockSpec(memory_space=pl.ANY)] * nbuf,
        out_specs=tuple([o_spec] * no),
        out_shape=tuple(SDS(out_shape, dt) for dt in out_dtypes),
        scratch_shapes=[] if nk == 1 else [pltpu.VMEM((tm, tn), F32)],
        compiler_params=pltpu.CompilerParams(dimension_semantics=("parallel", "parallel", "arbitrary"),
                                             vmem_limit_bytes=VMEM_BIG),
        **extra,
    )(a, b, *aux_arrs, *aliases)
    return outs[0] if no == 1 else outs


def _tok(S):
    ts = min(TOKEN_BLOCK, S)
    assert S % ts == 0
    return ts


def _row(ts, D):
    return pl.BlockSpec((ts, D), lambda i: (i, 0))


def _acc_rows(ref, i, rows):
    @pl.when(i == 0)
    def _():
        for r, v in enumerate(rows):
            ref[r:r + 1, :] = v

    @pl.when(i != 0)
    def _():
        for r, v in enumerate(rows):
            ref[r:r + 1, :] += v


def _modulate(x, mod, r_sh, r_sc, name):
    S, D = x.shape
    ts = _tok(S)

    def body(x_ref, m_ref, o_ref):
        o_ref[...] = (x_ref[...] * (1.0 + m_ref[r_sc:r_sc + 1, :]) + m_ref[r_sh:r_sh + 1, :]).astype(o_ref.dtype)

    return _pcall(body, name=name, grid=(S // ts,), in_specs=[_row(ts, D), _full(mod)],
                  out_specs=_row(ts, D), out_shape=SDS((S, D), _MXU))(x, mod)


def _modulate_bwd(x, mod, r_sc, dh, dxa, name):
    S, D = x.shape
    ts = _tok(S)

    def body(x_ref, m_ref, dh_ref, dxa_ref, dx_ref, dss_ref):
        dh_v = dh_ref[...]
        dx_ref[...] = dxa_ref[...] + dh_v * (1.0 + m_ref[r_sc:r_sc + 1, :])
        _acc_rows(dss_ref, pl.program_id(0),
                  [jnp.sum(dh_v, axis=0, keepdims=True), jnp.sum(dh_v * x_ref[...], axis=0, keepdims=True)])

    return _pcall(body, name=name, grid=(S // ts,),
                  in_specs=[_row(ts, D), _full(mod), _row(ts, D), _row(ts, D)],
                  out_specs=(_row(ts, D), pl.BlockSpec((2, D), lambda i: (0, 0))),
                  out_shape=(SDS((S, D), F32), SDS((2, D), F32)))(x, mod, dh, dxa)


def _norm_stats(z):
    mu = jnp.mean(z, axis=-1, keepdims=True)
    zc = z - mu
    var = jnp.mean(zc * zc, axis=-1, keepdims=True)
    rstd = lax.rsqrt(var + LN_EPS)
    return zc * rstd, rstd


def _norm_bwd(dxhat, xhat, rstd):
    return rstd * (dxhat - jnp.mean(dxhat, axis=-1, keepdims=True)
                   - xhat * jnp.mean(dxhat * xhat, axis=-1, keepdims=True))


def _combine(x, y, mod, r_gt, lnp, r_g, name):
    S, D = x.shape
    ts = _tok(S)

    def body(x_ref, y_ref, m_ref, l_ref, o_ref):
        z = ALPHA * x_ref[...] + (1.0 + m_ref[r_gt:r_gt + 1, :]) * y_ref[...]
        xhat, _ = _norm_stats(z)
        o_ref[...] = xhat * l_ref[r_g:r_g + 1, :] + l_ref[r_g + 1:r_g + 2, :]

    return _pcall(body, name=name, grid=(S // ts,), in_specs=[_row(ts, D), _row(ts, D), _full(mod), _full(lnp)],
                  out_specs=_row(ts, D), out_shape=SDS((S, D), F32))(x, y, mod, lnp)


def _combine_bwd(x, y, mod, r_gt, lnp, r_g, dout, name):
    S, D = x.shape
    ts = _tok(S)

    def body(x_ref, y_ref, m_ref, l_ref, do_ref, dxa_ref, dy_ref, dp_ref):
        gate = 1.0 + m_ref[r_gt:r_gt + 1, :]
        y_v, do_v = y_ref[...], do_ref[...]
        xhat, rstd = _norm_stats(ALPHA * x_ref[...] + gate * y_v)
        dz = _norm_bwd(do_v * l_ref[r_g:r_g + 1, :], xhat, rstd)
        dxa_ref[...] = ALPHA * dz
        dy_ref[...] = (gate * dz).astype(dy_ref.dtype)
        _acc_rows(dp_ref, pl.program_id(0),
                  [jnp.sum(dz * y_v, axis=0, keepdims=True), jnp.sum(do_v * xhat, axis=0, keepdims=True),
                   jnp.sum(do_v, axis=0, keepdims=True)])

    return _pcall(body, name=name, grid=(S // ts,),
                  in_specs=[_row(ts, D), _row(ts, D), _full(mod), _full(lnp), _row(ts, D)],
                  out_specs=(_row(ts, D), _row(ts, D), pl.BlockSpec((3, D), lambda i: (0, 0))),
                  out_shape=(SDS((S, D), F32), SDS((S, D), _MXU), SDS((3, D), F32)))(x, y, mod, lnp, dout)


def _sigmoid(t):
    return 1.0 / (1.0 + jnp.exp(-t))


def _ln_silu(u, lnp, name):
    S, D = u.shape
    ts = _tok(S)

    def body(u_ref, l_ref, o_ref):
        xhat, _ = _norm_stats(u_ref[...])
        t = xhat * l_ref[0:1, :] + l_ref[1:2, :]
        o_ref[...] = (t * _sigmoid(t)).astype(o_ref.dtype)

    return _pcall(body, name=name, grid=(S // ts,), in_specs=[_row(ts, D), _full(lnp)],
                  out_specs=_row(ts, D), out_shape=SDS((S, D), _MXU))(u, lnp)


def _ln_silu_bwd(u, lnp, dout, name):
    S, D = u.shape
    ts = _tok(S)

    def body(u_ref, l_ref, do_ref, du_ref, dp_ref):
        xhat, rstd = _norm_stats(u_ref[...])
        g = l_ref[0:1, :]
        t = xhat * g + l_ref[1:2, :]
        sg = _sigmoid(t)
        dt = do_ref[...] * (sg * (1.0 + t * (1.0 - sg)))
        du_ref[...] = _norm_bwd(dt * g, xhat, rstd)
        _acc_rows(dp_ref, pl.program_id(0),
                  [jnp.sum(dt * xhat, axis=0, keepdims=True), jnp.sum(dt, axis=0, keepdims=True)])

    return _pcall(body, name=name, grid=(S // ts,), in_specs=[_row(ts, D), _full(lnp), _row(ts, D)],
                  out_specs=(_row(ts, D), pl.BlockSpec((2, D), lambda i: (0, 0))),
                  out_shape=(SDS((S, D), F32), SDS((2, D), F32)))(u, lnp, dout)


def _loss_head(xf, tgt, name):
    S, D = xf.shape
    ts = _tok(S)

    def body(x_ref, t_ref, dx_ref, l_ref):
        err = x_ref[...] - t_ref[...]
        dx_ref[...] = err * (1.0 / D)
        part = jnp.sum(jnp.sum(err * err, axis=1, keepdims=True), axis=0, keepdims=True) * (0.5 / D)

        @pl.when(pl.program_id(0) == 0)
        def _():
            l_ref[...] = part

        @pl.when(pl.program_id(0) != 0)
        def _():
            l_ref[...] += part

    return _pcall(body, name=name, grid=(S // ts,), in_specs=[_row(ts, D), _row(ts, D)],
                  out_specs=(_row(ts, D), pl.BlockSpec((1, 1), lambda i: (0, 0))),
                  out_shape=(SDS((S, D), F32), SDS((1, 1), F32)))(xf, tgt)


def _shift_down(u, s, rows):
    if s == 0:
        return u
    return jnp.where(rows >= s, pltpu.roll(u, s, 0), 0.0)


def _shift_up(u, s, rows):
    if s == 0:
        return u
    n = u.shape[0]
    return jnp.where(rows < n - s, pltpu.roll(u, n - s, 0), 0.0)


def _dwconv(u, w_ref, taps, rows):
    acc = jnp.zeros_like(u)
    for j in range(taps):
        acc = acc + w_ref[j:j + 1, :] * _shift_down(u, taps - 1 - j, rows)
    return acc


def _dwconv_bwd(u, dy, w_ref, dw_ref, taps, rows):
    du = jnp.zeros_like(u)
    for j in range(taps):
        s = taps - 1 - j
        du = du + w_ref[j:j + 1, :] * _shift_up(dy, s, rows)
        dw_ref[j:j + 1, :] = jnp.sum(dy * _shift_down(u, s, rows), axis=0, keepdims=True)
    return du


def _col(S, j0=0):
    return pl.BlockSpec((S, LANES), lambda j: (0, j + j0))


def _conv_silu(pm, w, nblk, name):
    S = pm.shape[0]
    taps = w.shape[0]

    def body(u_ref, w_ref, o_ref):
        rows = lax.broadcasted_iota(jnp.int32, (S, LANES), 0)
        cv = _dwconv(u_ref[...], w_ref, taps, rows)
        o_ref[...] = cv * _sigmoid(cv)

    return _pcall(body, name=name, grid=(nblk,),
                  in_specs=[_col(S), pl.BlockSpec((taps, LANES), lambda j: (0, j))],
                  out_specs=_col(S), out_shape=SDS((S, nblk * LANES), F32),
                  compiler_params=_cparams(VMEM_BIG))(pm, w)


def _conv_silu_bwd(pm, w, dout, dpm, j0, name):
    S = pm.shape[0]
    taps = w.shape[0]
    nblk = dout.shape[1] // LANES

    def body(u_ref, w_ref, do_ref, dpm_in, du_ref, dw_ref):
        del dpm_in
        rows = lax.broadcasted_iota(jnp.int32, (S, LANES), 0)
        u = u_ref[...]
        cv = _dwconv(u, w_ref, taps, rows)
        sg = _sigmoid(cv)
        dc = do_ref[...] * (sg * (1.0 + cv * (1.0 - sg)))
        du_ref[...] = _dwconv_bwd(u, dc, w_ref, dw_ref, taps, rows)

    return _pcall(body, name=name, grid=(nblk,),
                  in_specs=[_col(S, j0), pl.BlockSpec((taps, LANES), lambda j: (0, j + j0)), _col(S),
                            pl.BlockSpec(memory_space=pl.ANY)],
                  out_specs=(_col(S, j0), pl.BlockSpec((taps, LANES), lambda j: (0, j))),
                  out_shape=(SDS(dpm.shape, F32), SDS((taps, nblk * LANES), F32)),
                  input_output_aliases={3: 0},
                  compiler_params=_cparams(VMEM_BIG))(pm, w, dout, dpm)


def _glu_conv(p, w, bias, name):
    S, C2 = p.shape
    nblk = C2 // 2 // LANES
    taps = w.shape[0]

    def body(v_ref, g_ref, w_ref, b_ref, o_ref):
        rows = lax.broadcasted_iota(jnp.int32, (S, LANES), 0)
        u = v_ref[...] * _sigmoid(g_ref[...])
        o_ref[...] = _dwconv(u, w_ref, taps, rows) + b_ref[...]

    return _pcall(body, name=name, grid=(nblk,),
                  in_specs=[_col(S), _col(S, nblk), pl.BlockSpec((taps, LANES), lambda j: (0, j)),
                            pl.BlockSpec((1, LANES), lambda j: (0, j))],
                  out_specs=_col(S), out_shape=SDS((S, nblk * LANES), F32),
                  compiler_params=_cparams(VMEM_BIG))(p, p, w, bias)


def _glu_conv_bwd(p, w, dout, name):
    S, C2 = p.shape
    nblk = C2 // 2 // LANES
    taps = w.shape[0]

    def body(v_ref, g_ref, w_ref, do_ref, dv_ref, dg_ref, dw_ref, db_ref):
        rows = lax.broadcasted_iota(jnp.int32, (S, LANES), 0)
        val, sg = v_ref[...], _sigmoid(g_ref[...])
        do_v = do_ref[...]
        du = _dwconv_bwd(val * sg, do_v, w_ref, dw_ref, taps, rows)
        dv_ref[...] = du * sg
        dg_ref[...] = du * val * sg * (1.0 - sg)
        db_ref[...] = jnp.sum(do_v, axis=0, keepdims=True)

    dval, dgate, dw, db = _pcall(
        body, name=name, grid=(nblk,),
        in_specs=[_col(S), _col(S, nblk), pl.BlockSpec((taps, LANES), lambda j: (0, j)), _col(S)],
        out_specs=(_col(S), _col(S), pl.BlockSpec((taps, LANES), lambda j: (0, j)),
                   pl.BlockSpec((1, LANES), lambda j: (0, j))),
        out_shape=(SDS((S, C2 // 2), F32), SDS((S, C2 // 2), F32), SDS((taps, C2 // 2), F32), SDS((1, C2 // 2), F32)),
        compiler_params=_cparams(VMEM_BIG))(p, p, w, dout)
    return dval, dgate, dw, db


def _log1p(e):
    u = 1.0 + e
    d = jnp.where(u == 1.0, 1.0, u - 1.0)
    return jnp.where(u == 1.0, e, jnp.log(u) * (e / d))


def _gate_parts(ps, prm, H):
    lane = lax.broadcasted_iota(jnp.int32, ps.shape, 1)
    is_b, is_g = lane < H, (lane >= H) & (lane < 2 * H)
    beta = _sigmoid(ps)
    t = ps + prm[1:2, :]
    sp = jnp.maximum(t, 0.0) + _log1p(jnp.exp(-jnp.abs(t)))
    na = -jnp.exp(prm[0:1, :])
    return is_b, is_g, beta, t, sp, na


def _gates(ps, prm, H, name):
    S = ps.shape[0]
    ts = _tok(S)

    def body(p_ref, r_ref, o_ref):
        is_b, is_g, beta, _, sp, na = _gate_parts(p_ref[...], r_ref[...], H)
        o_ref[...] = jnp.where(is_b, beta, jnp.where(is_g, na * sp, 0.0))

    return _pcall(body, name=name, grid=(S // ts,), in_specs=[_row(ts, LANES), _full(prm)],
                  out_specs=_row(ts, LANES), out_shape=SDS((S, LANES), F32))(ps, prm)


def _gates_bwd(ps, prm, dgates, H, name):
    S = ps.shape[0]
    ts = _tok(S)

    def body(p_ref, r_ref, dg_ref, dp_ref, dr_ref):
        is_b, is_g, beta, t, sp, na = _gate_parts(p_ref[...], r_ref[...], H)
        dg_v = dg_ref[...]
        dsp = jnp.where(is_g, dg_v * na * _sigmoid(t), 0.0)
        dp_ref[...] = jnp.where(is_b, dg_v * beta * (1.0 - beta), dsp)
        _acc_rows(dr_ref, pl.program_id(0),
                  [jnp.sum(jnp.where(is_g, dg_v * na * sp, 0.0), axis=0, keepdims=True),
                   jnp.sum(dsp, axis=0, keepdims=True)])

    return _pcall(body, name=name, grid=(S // ts,), in_specs=[_row(ts, LANES), _full(prm), _row(ts, LANES)],
                  out_specs=(_row(ts, LANES), pl.BlockSpec((2, LANES), lambda i: (0, 0))),
                  out_shape=(SDS((S, LANES), F32), SDS((2, LANES), F32)))(ps, prm, dgates)


_NN = (((2,), (1,)), ((0,), (0,)))
_NT = (((2,), (2,)), ((0,), (0,)))
_TN = (((1,), (1,)), ((0,), (0,)))


def _mdot(a, b, dims):
    return lax.dot_general(a.astype(_MXU), b.astype(_MXU), dims, preferred_element_type=F32)


def _mdot3(a, b, dims):
    ah, bh = a.astype(_MXU), b.astype(_MXU)
    al, bl = a - ah.astype(F32), b - bh.astype(F32)
    return _mdot(ah, bh, dims) + (_mdot(ah, bl, dims) + _mdot(al, bh, dims))


def _rounded_dot(dims, da_dims, db_dims, a_first, prod=_mdot):
    @jax.custom_vjp
    def f(a, b):
        return prod(a, b, dims)

    def fwd(a, b):
        return prod(a, b, dims), (a, b)

    def bwd(res, ct):
        a, b = res
        da = prod(ct, b, da_dims) if a_first[0] else prod(b, ct, da_dims)
        db = prod(ct, a, db_dims) if a_first[1] else prod(a, ct, db_dims)
        return da, db

    f.defvjp(fwd, bwd)
    return f


_mdot_nn = _rounded_dot(_NN, _NT, _TN, (True, False))
_mdot_nt = _rounded_dot(_NT, _NN, _TN, (True, True))
_mdot_tn = _rounded_dot(_TN, _NT, _NN, (False, False))
def _unit_lower_inverse(a):
    C = a.shape[-1]
    ri = lax.broadcasted_iota(jnp.int32, (1, C, C), 1)
    ci = lax.broadcasted_iota(jnp.int32, (1, C, C), 2)
    t_inv = jnp.where(ri == ci, 1.0, 0.0) - a
    p = a
    for _ in range(max(C.bit_length() - 2, 0)):
        p = _mdot3(p, p, _NN)
        t_inv = t_inv + _mdot3(t_inv, p, _NN)
    return t_inv


@jax.custom_vjp
def _known_inverse(a, t_inv):
    del a
    return t_inv


def _known_inverse_fwd(a, t_inv):
    del a
    return t_inv, t_inv


def _known_inverse_bwd(t_inv, ct):
    da = -_mdot3(_mdot3(t_inv, ct, _TN), t_inv, _NT)
    return da, jnp.zeros_like(t_inv)


_known_inverse.defvjp(_known_inverse_fwd, _known_inverse_bwd)


def _head_cols(gates, off, H):
    lane = lax.broadcasted_iota(jnp.int32, gates.shape, 1)
    cols = [jnp.sum(jnp.where(lane == off + h, gates, 0.0), axis=-1, keepdims=True) for h in range(H)]
    return jnp.concatenate([col[None] for col in cols], axis=0)


def _delta_chunk(qr, kr, v, z, gates, nw, s_in, t_known=None):
    H, C, dk = qr.shape
    beta, g = _head_cols(gates, 0, H), _head_cols(gates, H, H)
    q = qr * lax.rsqrt(jnp.sum(qr * qr, axis=-1, keepdims=True) + L2_EPS) * (dk ** -0.5)
    k = kr * lax.rsqrt(jnp.sum(kr * kr, axis=-1, keepdims=True) + L2_EPS)
    ri = lax.broadcasted_iota(jnp.int32, (1, C, C), 1)
    ci = lax.broadcasted_iota(jnp.int32, (1, C, C), 2)
    causal, strict, eye = ri >= ci, ri > ci, ri == ci
    gam_row = jnp.sum(jnp.where(ri <= ci, g, 0.0), axis=1, keepdims=True)
    gam_col = jnp.sum(jnp.where(eye, gam_row, 0.0), axis=-1, keepdims=True)
    g_last = jnp.sum(g, axis=1, keepdims=True)
    decay = jnp.where(causal, jnp.exp(jnp.where(causal, gam_col - gam_row, 0.0)), 0.0)
    kb = k * beta
    a = jnp.where(strict, _mdot_nt(kb, k) * decay, 0.0)
    t_inv = _unit_lower_inverse(a) if t_known is None else _known_inverse(a, t_known)
    eg = jnp.exp(gam_col)
    u = _mdot_nn(t_inv, v * beta)
    w = _mdot_nn(t_inv, kb * eg)
    a_qk = _mdot_nt(q, k) * decay
    v_new = u - _mdot_nn(w, s_in)
    o = _mdot_nn(q * eg, s_in) + _mdot_nn(a_qk, v_new)
    s_out = s_in * jnp.exp(g_last) + _mdot_tn(k * jnp.exp(g_last - gam_col), v_new)
    og = o * lax.rsqrt(jnp.mean(o * o, axis=-1, keepdims=True) + RMS_EPS) * nw * (z * _sigmoid(z))
    return og, s_out, t_inv


def _heads(ref, H, dk):
    return jnp.stack([ref[:, h * dk:(h + 1) * dk].astype(F32) for h in range(H)])


def _put_heads(ref, val, dk):
    for h in range(val.shape[0]):
        ref[:, h * dk:(h + 1) * dk] = val[h].astype(ref.dtype)


def _delta_fwd(qkv, pm, gates, nw, H, name):
    S = qkv.shape[0]
    hd = qkv.shape[1] // 3
    dk = hd // H
    N = S // CHUNK
    blk = lambda off: pl.BlockSpec((CHUNK, hd), lambda n: (n, off))

    def body(q_ref, k_ref, v_ref, z_ref, g_ref, nw_ref, og_ref, st_ref, ti_ref, s_scr):
        @pl.when(pl.program_id(0) == 0)
        def _():
            s_scr[...] = jnp.zeros_like(s_scr)

        s_in = s_scr[...]
        st_ref[...] = s_in
        og, s_out, t_inv = _delta_chunk(_heads(q_ref, H, dk), _heads(k_ref, H, dk), _heads(v_ref, H, dk),
                                        _heads(z_ref, H, dk), g_ref[...], nw_ref[...], s_in)
        _put_heads(og_ref, og, dk)
        ti_ref[...] = t_inv
        s_scr[...] = s_out

    return _pcall(
        body, name=name, grid=(N,),
        in_specs=[blk(0), blk(1), blk(2), blk(3), pl.BlockSpec((CHUNK, LANES), lambda n: (n, 0)), _full(nw)],
        out_specs=(blk(0), pl.BlockSpec((None, H, dk, dk), lambda n: (n, 0, 0, 0)),
                   pl.BlockSpec((None, H, CHUNK, CHUNK), lambda n: (n, 0, 0, 0))),
        out_shape=(SDS((S, hd), _MXU), SDS((N, H, dk, dk), F32), SDS((N, H, CHUNK, CHUNK), F32)),
        scratch_shapes=[pltpu.VMEM((H, dk, dk), F32)],
        compiler_params=_cparams(VMEM_BIG),
    )(qkv, qkv, qkv, pm, gates, nw)


def _delta_bwd(qkv, pm, gates, nw, states, t_invs, dog, H, name):
    S = qkv.shape[0]
    hd = qkv.shape[1] // 3
    dk = hd // H
    N = S // CHUNK
    blk = lambda off: pl.BlockSpec((CHUNK, hd), lambda n: (N - 1 - n, off))
    gspec = pl.BlockSpec((CHUNK, LANES), lambda n: (N - 1 - n, 0))

    def body(q_ref, k_ref, v_ref, z_ref, g_ref, nw_ref, st_ref, ti_ref, do_ref,
             dq_ref, dk_ref, dv_ref, dz_ref, dg_ref, dnw_ref, ds_scr):
        n = pl.program_id(0)

        @pl.when(n == 0)
        def _():
            ds_scr[...] = jnp.zeros_like(ds_scr)

        t_known = ti_ref[...]
        fn = functools.partial(_delta_chunk, t_known=t_known)
        _, vjp = jax.vjp(fn, _heads(q_ref, H, dk), _heads(k_ref, H, dk), _heads(v_ref, H, dk),
                         _heads(z_ref, H, dk), g_ref[...], nw_ref[...], st_ref[...])
        dq, dkk, dv, dz, dg, dnw, ds_in = vjp((_heads(do_ref, H, dk), ds_scr[...], jnp.zeros_like(t_known)))
        _put_heads(dq_ref, dq, dk)
        _put_heads(dk_ref, dkk, dk)
        _put_heads(dv_ref, dv, dk)
        _put_heads(dz_ref, dz, dk)
        ds_scr[...] = ds_in
        dg_ref[...] = dg

        @pl.when(n == 0)
        def _():
            dnw_ref[...] = dnw

        @pl.when(n != 0)
        def _():
            dnw_ref[...] += dnw

    return _pcall(
        body, name=name, grid=(N,),
        in_specs=[blk(0), blk(1), blk(2), blk(3), gspec, _full(nw),
                  pl.BlockSpec((None, H, dk, dk), lambda n: (N - 1 - n, 0, 0, 0)),
                  pl.BlockSpec((None, H, CHUNK, CHUNK), lambda n: (N - 1 - n, 0, 0, 0)), blk(0)],
        out_specs=(blk(0), blk(0), blk(0), blk(3), gspec, pl.BlockSpec((1, dk), lambda n: (0, 0))),
        out_shape=(SDS((S, hd), F32), SDS((S, hd), F32), SDS((S, hd), F32), SDS(pm.shape, F32),
                   SDS((S, LANES), F32), SDS((1, dk), F32)),
        scratch_shapes=[pltpu.VMEM((H, dk, dk), F32)],
        compiler_params=_cparams(VMEM_BIG),
    )(qkv, qkv, qkv, pm, gates, nw, states, t_invs, dog)


def _rows_block(R, C):
    rb = R
    while rb * C * 4 > (1 << 20) and rb % 16 == 0:
        rb //= 2
    return rb


def _sum_slots(st, name, out_dtype=F32):
    n, R, C = st.shape
    rb = _rows_block(R, C)

    def body(s_ref, o_ref):
        acc = s_ref[0].astype(F32)
        for q in range(1, n):
            acc = acc + s_ref[q].astype(F32)
        o_ref[...] = acc.astype(o_ref.dtype)

    return _pcall(body, name=name, grid=(R // rb,), in_specs=[pl.BlockSpec((n, rb, C), lambda i: (0, i, 0))],
                  out_specs=pl.BlockSpec((rb, C), lambda i: (i, 0)), out_shape=SDS((R, C), out_dtype))(st)


def _scalar(v):
    return jnp.reshape(v, (1,)).astype(jnp.int32)


def _place_block(w, layer, ax, chip, name):
    _, R, C = w.shape
    rb = _rows_block(R, C)
    nrb = R // rb
    shp = [R, C]
    shp[ax] *= 4
    omap = (lambda i, c: (c[0] * nrb + i, 0)) if ax == 0 else (lambda i, c: (i, c[0]))

    def body(c_ref, w_ref, o_ref):
        del c_ref
        o_ref[...] = w_ref[...].astype(o_ref.dtype)

    grid_spec = pltpu.PrefetchScalarGridSpec(
        num_scalar_prefetch=1, grid=(nrb,),
        in_specs=[pl.BlockSpec((None, rb, C), lambda i, c: (layer, i, 0))],
        out_specs=pl.BlockSpec((rb, C), omap))
    return _pcall(body, name=name, grid_spec=grid_spec, out_shape=SDS(tuple(shp), _MXU))(_scalar(chip), w)


def _sum_pair(own, recv, ic, name, out_dtype):
    N, _, R, C = own.shape
    rb = _rows_block(R, C)

    def body(c_ref, a_ref, b_ref, o_ref):
        del c_ref
        o_ref[...] = (a_ref[...].astype(F32) + b_ref[...].astype(F32)).astype(o_ref.dtype)

    grid_spec = pltpu.PrefetchScalarGridSpec(
        num_scalar_prefetch=1, grid=(N, R // rb),
        in_specs=[pl.BlockSpec((None, None, rb, C), lambda p, i, c: (p, c[0], i, 0)),
                  pl.BlockSpec((None, None, rb, C), lambda p, i, c: (1 - c[0], p, i, 0))],
        out_specs=pl.BlockSpec((None, rb, C), lambda p, i, c: (p, i, 0)))
    return _pcall(body, name=name, grid_spec=grid_spec, out_shape=SDS((N, R, C), out_dtype))(
        _scalar(ic), own, recv)


def _sum_landed(grad, ax, land, chip, name):
    _, R, Cb = land.shape
    rb = _rows_block(R, Cb)
    nrb = R // rb
    if ax == 0:
        own_spec = pl.BlockSpec((rb, Cb), lambda i, c: (c[0] * nrb + i, 0))
    else:
        own_spec = pl.BlockSpec((rb, Cb), lambda i, c: (i, c[0]))
    slot = lambda d: pl.BlockSpec((None, rb, Cb), lambda i, c: ((c[0] + d) % 4, i, 0))

    def body(c_ref, own_ref, r1, r2, r3, o_ref):
        del c_ref
        o_ref[...] = ((own_ref[...].astype(F32) + r1[...].astype(F32)) + r2[...].astype(F32)) + r3[...].astype(F32)

    grid_spec = pltpu.PrefetchScalarGridSpec(
        num_scalar_prefetch=1, grid=(R // rb,), in_specs=[own_spec, slot(1), slot(2), slot(3)],
        out_specs=pl.BlockSpec((rb, Cb), lambda i, c: (i, 0)))
    return _pcall(body, name=name, grid_spec=grid_spec, out_shape=SDS((R, Cb), F32))(
        _scalar(chip), grad, land, land, land)


def _adamw_step(g, w_ref, m_ref, v_ref, g_ref, d_ref, mo_ref, vo_ref):
    m_new = ADAM_B1 * m_ref[...] + (1.0 - ADAM_B1) * g
    v_new = ADAM_B2 * v_ref[...] + (1.0 - ADAM_B2) * (g * g)
    m_hat = m_new / (1.0 - ADAM_B1 ** ADAM_STEP)
    v_hat = v_new / (1.0 - ADAM_B2 ** ADAM_STEP)
    g_ref[...] = g
    d_ref[...] = -ADAM_LR * (m_hat / (jnp.sqrt(v_hat) + ADAM_EPS) + ADAM_WD * w_ref[...])
    mo_ref[...] = m_new
    vo_ref[...] = v_new


def _adamw(w, m, v, st, name):
    R, C = w.shape
    n = st.shape[0]
    rb = _rows_block(R, C)
    spec = pl.BlockSpec((rb, C), lambda i: (i, 0))

    def body(w_ref, m_ref, v_ref, s_ref, *o_refs):
        g = s_ref[0]
        for q in range(1, n):
            g = g + s_ref[q]
        _adamw_step(g, w_ref, m_ref, v_ref, *o_refs)

    return _pcall(body, name=name, grid=(R // rb,),
                  in_specs=[spec, spec, spec, pl.BlockSpec((n, rb, C), lambda i: (0, i, 0))],
                  out_specs=(spec,) * 4, out_shape=(SDS((R, C), F32),) * 4)(w, m, v, st)


def _adamw_pair(w, m, v, layer, own, recv2, ic, bufs, name):
    L, R, C = w.shape
    rb = _rows_block(R, C)
    spec = pl.BlockSpec((None, rb, C), lambda i, c: (layer, i, 0))

    def body(c_ref, w_ref, m_ref, v_ref, own_ref, recv_ref, *rest):
        del c_ref
        _adamw_step(own_ref[...] + recv_ref[...], w_ref, m_ref, v_ref, *rest[-4:])

    nbuf = 0 if bufs is None else 4
    grid_spec = pltpu.PrefetchScalarGridSpec(
        num_scalar_prefetch=1, grid=(R // rb,),
        in_specs=[spec, spec, spec, pl.BlockSpec((rb, C), lambda i, c: (i, 0)),
                  pl.BlockSpec((None, rb, C), lambda i, c: (1 - c[0], i, 0))] + [_ANY] * nbuf,
        out_specs=(spec,) * 4)
    extra = {} if bufs is None else dict(input_output_aliases={6 + q: q for q in range(4)})
    return _pcall(body, name=name, grid_spec=grid_spec, out_shape=(SDS((L, R, C), F32),) * 4, **extra)(
        _scalar(ic), w, m, v, own, recv2, *([] if bufs is None else bufs))


def _pack(arrs, rows=1):
    flat = jnp.concatenate([a.reshape(-1).astype(F32) for a in arrs])
    quantum = rows * LANES
    pad = (-flat.shape[0]) % quantum
    flat = jnp.pad(flat, (0, pad))
    return flat.reshape(rows, -1)


def _unpack(flat, shapes):
    flat = flat.reshape(-1)
    out, off = [], 0
    for shp in shapes:
        size = 1
        for d in shp:
            size *= d
        out.append(flat[off:off + size].reshape(shp))
        off += size
    return out


def _mlp_fwd(x1, mod, lnp, w1, w2, tag):
    h2 = _modulate(x1, mod, 3, 4, f"{tag}_mod")
    a1, a2 = _mm(h2, w1, "nn", name=f"{tag}_up", out_dtypes=(_MXU, _MXU),
                 out_fn=lambda r: (r, jnp.square(jnp.maximum(r, 0.0))))
    y2 = _mm(a2, w2, "nn", name=f"{tag}_down")
    x2 = _combine(x1, y2, mod, 5, lnp, 2, f"{tag}_ln")
    return x2, (x1, h2, a1, a2, y2)


def _weight_grad(grads, key, a, b, name):
    grads[key] = _mm(a, b, "tn", name=name, out_dtypes=(_MXU,))


def _mlp_bwd(dx2, saved, mod, lnp, w1, w2, tag, stacks):
    x1, h2, a1, a2, y2 = saved
    dxa, dy2, dp = _combine_bwd(x1, y2, mod, 5, lnp, 2, dx2, f"{tag}_ln_b")
    da1 = _mm(dy2, w2, "nt", name=f"{tag}_down_bx", out_dtypes=(_MXU,), aux=[(a1, "mn")],
              out_fn=lambda r, a: (r * (2.0 * jnp.maximum(a.astype(F32), 0.0)),))
    _weight_grad(stacks, "ff_w2", a2, dy2, f"{tag}_down_bw")
    _weight_grad(stacks, "ff_w1", h2, da1, f"{tag}_up_bw")
    dh2 = _mm(da1, w1, "nt", name=f"{tag}_up_bx")
    dx1, dss = _modulate_bwd(x1, mod, 4, dh2, dxa, f"{tag}_mod_b")
    return dx1, (dss, dp)


def _dn_fwd(x, mod, lnp, wts, H, tag):
    w_main, w_small, conv_w, prm, nw, w_out = wts
    h = _modulate(x, mod, 0, 1, f"{tag}_mod")
    pm = _mm(h, w_main, "nn", name=f"{tag}_in")
    ps = _mm(h, w_small, "nn", name=f"{tag}_in_s")
    nqkv = conv_w.shape[1] // LANES
    qkv = _conv_silu(pm, conv_w, nqkv, f"{tag}_conv")
    gates = _gates(ps, prm, H, f"{tag}_gates")
    og, *states = _delta_fwd(qkv, pm, gates, nw, H, f"{tag}_delta")
    y = _mm(og, w_out, "nn", name=f"{tag}_out")
    x1 = _combine(x, y, mod, 2, lnp, 0, f"{tag}_ln")
    return x1, (x, h, pm, ps, qkv, gates, states, og, y)


def _dn_bwd(dx1, saved, mod, lnp, wts, H, tag, stacks):
    w_main, w_small, conv_w, prm, nw, w_out = wts
    x, h, pm, ps, qkv, gates, states, og, y = saved
    dxa, dy, dp = _combine_bwd(x, y, mod, 2, lnp, 0, dx1, f"{tag}_ln_b")
    dog = _mm(dy, w_out, "nt", name=f"{tag}_out_bx")
    _weight_grad(stacks, "dn_w_out", og, dy, f"{tag}_out_bw")
    dq, dk, dv, dpm, dgates, dnw = _delta_bwd(qkv, pm, gates, nw, *states, dog, H, f"{tag}_delta_b")
    dps, dprm = _gates_bwd(ps, prm, dgates, H, f"{tag}_gates_b")
    dcw = []
    nb = dq.shape[1] // LANES
    for part, dpart in enumerate((dq, dk, dv)):
        dpm, dcw_p = _conv_silu_bwd(pm, conv_w, dpart, dpm, part * nb, f"{tag}_conv_b{part}")
        dcw.append(dcw_p)
    dconv_w = jnp.concatenate(dcw, axis=1)
    dw_main = _mm(h, dpm, "tn", name=f"{tag}_in_bw", out_dtypes=(_MXU,))
    dw_small = _mm(h, dps, "tn", name=f"{tag}_in_s_bw", out_dtypes=(_MXU,))
    dh_s = _mm(dps, w_small, "nt", name=f"{tag}_in_s_bx")
    dh = _mm(dpm, w_main, "nt", name=f"{tag}_in_bx", aux=[(dh_s, "mn")], out_fn=lambda r, e: (r + e,))
    dx, dss = _modulate_bwd(x, mod, 1, dh, dxa, f"{tag}_mod_b")
    return dx, (dw_main, dw_small, dconv_w, dprm, dnw), (dss, dp)


def _cf_fwd(x, mod, lnp, wts, tag):
    w_in, dw_w, dw_b, cln, w_out = wts
    h = _modulate(x, mod, 0, 1, f"{tag}_mod")
    p = _mm(h, w_in, "nn", name=f"{tag}_in")
    u2 = _glu_conv(p, dw_w, dw_b, f"{tag}_conv")
    u3 = _ln_silu(u2, cln, f"{tag}_cln")
    y = _mm(u3, w_out, "nn", name=f"{tag}_out")
    x1 = _combine(x, y, mod, 2, lnp, 0, f"{tag}_ln")
    return x1, (x, h, p, u2, u3, y)


def _cf_bwd(dx1, saved, mod, lnp, wts, tag, stacks):
    w_in, dw_w, dw_b, cln, w_out = wts
    x, h, p, u2, u3, y = saved
    dxa, dy, dp = _combine_bwd(x, y, mod, 2, lnp, 0, dx1, f"{tag}_ln_b")
    du3 = _mm(dy, w_out, "nt", name=f"{tag}_out_bx")
    _weight_grad(stacks, "cf_w_out", u3, dy, f"{tag}_out_bw")
    du2, dcln = _ln_silu_bwd(u2, cln, du3, f"{tag}_cln_b")
    dval, dgate, ddw_w, ddw_b = _glu_conv_bwd(p, dw_w, du2, f"{tag}_conv_b")
    dpp = jnp.concatenate([dval, dgate], axis=1)
    _weight_grad(stacks, "cf_w_in", h, dpp, f"{tag}_in_bw")
    dh = _mm(dpp, w_in, "nt", name=f"{tag}_in_bx")
    dx, dss = _modulate_bwd(x, mod, 1, dh, dxa, f"{tag}_mod_b")
    return dx, (ddw_w, ddw_b, dcln), (dss, dp)


def _two_d(a):
    return a.reshape(-1, a.shape[-1])


def kernel(x, c, ada_w, ada_b, ln_g, ln_b, dn_w_in, dn_conv_w, dn_a_log, dn_dt_bias, dn_norm_w, dn_w_out, cf_w_in, cf_dw_w, cf_dw_b, cf_ln_g, cf_ln_b, cf_w_out, ff_w1, ff_w2, loss_target, m_ada_w, m_ada_b, m_ln_g, m_ln_b, m_dn_w_in, m_dn_conv_w, m_dn_a_log, m_dn_dt_bias, m_dn_norm_w, m_dn_w_out, m_cf_w_in, m_cf_dw_w, m_cf_dw_b, m_cf_ln_g, m_cf_ln_b, m_cf_w_out, m_ff_w1, m_ff_w2, v_ada_w, v_ada_b, v_ln_g, v_ln_b, v_dn_w_in, v_dn_conv_w, v_dn_a_log, v_dn_dt_bias, v_dn_norm_w, v_dn_w_out, v_cf_w_in, v_cf_dw_w, v_cf_dw_b, v_cf_ln_g, v_cf_ln_b, v_cf_w_out, v_ff_w1, v_ff_w2):
    ix, iy, ic = lax.axis_index("x"), lax.axis_index("y"), lax.axis_index("c")
    chip = 2 * ix + iy
    dev = 4 * ix + 2 * iy + ic
    S, D = x.shape[1], x.shape[2]
    L = ada_w.shape[0]
    LA, LB = dn_w_in.shape[0], cf_w_in.shape[0]
    H = dn_a_log.shape[1]
    NMOD = ada_b.shape[1] // D
    dn_in = dn_w_in.shape[2] * 4
    n_main = dn_in - 2 * H
    assert L == N_LAYERS and 2 * H <= LANES
    x0, tgt = x[0], loss_target[0]

    small_sharded = [ln_g, ln_b, dn_conv_w, cf_dw_w, cf_dw_b, cf_ln_g, cf_ln_b]
    small_axes = [2, 2, 2, 2, 1, 1, 1]
    packed_small = _pack(small_sharded, rows=8)[None]
    big = {"dn_w_in": (dn_w_in, m_dn_w_in, v_dn_w_in, 0), "dn_w_out": (dn_w_out, m_dn_w_out, v_dn_w_out, 0),
           "cf_w_in": (cf_w_in, m_cf_w_in, v_cf_w_in, 1), "cf_w_out": (cf_w_out, m_cf_w_out, v_cf_w_out, 0),
           "ff_w1": (ff_w1, m_ff_w1, v_ff_w1, 1), "ff_w2": (ff_w2, m_ff_w2, v_ff_w2, 0)}

    def group(g):
        l = g // 2
        if g % 2:
            return {"ff_w1": l, "ff_w2": l}
        mixer = ("dn_w_in", "dn_w_out") if l % 2 == 0 else ("cf_w_in", "cf_w_out")
        return {mixer[0]: l // 2, mixer[1]: l // 2}

    def place(l):
        return [_place_block(big[nm][0], lw, big[nm][3], chip, f"l{l}_place_{nm}") for nm, lw in group(l).items()]

    def start_gather(l, placed, after):
        names = list(group(l))
        axes = [big[nm][3] for nm in names]
        send, recv, arrs, token = _gather_start(list(zip(placed, axes)), after, f"l{l}_gather_start")
        return names, axes, arrs, send, recv, token

    def finish_gather(l, pending, after):
        names, axes, arrs, send, recv, _ = pending
        arrs = _gather_wait(list(zip(arrs, axes)), send, recv, after, f"l{l}_gather_wait")
        arrs = _gather_forward(list(zip(arrs, axes)), f"l{l}_gather_pass")
        return dict(zip(names, arrs))

    g_small = _exchange([(packed_small, 0)], "xy", "gather", "gather_small")[0]
    shard_shapes = [a.shape for a in small_sharded]
    per_chip = [_unpack(g_small[q], shard_shapes) for q in range(4)]
    ln_g_f, ln_b_f, conv_w_f, dw_w_f, dw_b_f, cln_g_f, cln_b_f = [
        jnp.concatenate([per_chip[q][i] for q in range(4)], axis=small_axes[i]) for i in range(len(small_sharded))]

    c_all = _exchange([(c[None], 0)], "all", "gather", "gather_cond")[0].reshape(8, D)
    c_pad = jnp.pad(c_all, ((0, 8), (0, 0)))
    mod_sh = jnp.stack([_mm(c_pad, (ada_w, l), "nn", name=f"ada_{l}", a_fn=lambda t: t * _sigmoid(t))
                        for l in range(L)])
    mod_all = _exchange([(mod_sh, 2)], "xy", "gather", "gather_mod")[0]
    mod_mine = lax.dynamic_index_in_dim(mod_all, dev, axis=1, keepdims=False) + ada_b
    mods = mod_mine.reshape(L, NMOD, D)

    def lnp_of(l):
        return jnp.stack([ln_g_f[l, 0], ln_b_f[l, 0], ln_g_f[l, 1], ln_b_f[l, 1]])

    def mixer_wts(l, wl):
        j = l // 2
        if l % 2 == 0:
            w_in = jnp.transpose(wl["dn_w_in"].reshape(4, D, dn_in // 4), (1, 0, 2)).reshape(D, dn_in)
            w_small = jnp.pad(w_in[:, n_main:], ((0, 0), (0, LANES - 2 * H)))
            prm = jnp.zeros((2, LANES), F32).at[0, H:2 * H].set(dn_a_log[j]).at[1, H:2 * H].set(dn_dt_bias[j])
            return (w_in[:, :n_main], w_small, conv_w_f[j], prm, dn_norm_w[j][None], wl["dn_w_out"])
        return (wl["cf_w_in"], dw_w_f[j], dw_b_f[j][None], jnp.stack([cln_g_f[j], cln_b_f[j]]), wl["cf_w_out"])

    xs = x0
    saved, wts, mod_of = [], [], []
    pending = start_gather(0, place(0), mods + jnp.minimum(jnp.abs(g_small[0, 0, 0]), 0.0))
    placed = [None] + [place(g) for g in range(1, 2 * L)]
    for g in range(2 * L):
        l = g // 2
        wl = finish_gather(g, pending, xs)
        mod_g = mods[l]
        if g + 1 < 2 * L:
            pending = start_gather(g + 1, placed[g + 1], next(iter(wl.values())))
            mod_g = mod_g + pending[5][0, 0]
        mod_of.append(mod_g)
        if g % 2:
            wts.append((wl["ff_w1"], wl["ff_w2"]))
            xs, sv = _mlp_fwd(xs, mod_g, lnp_of(l), *wts[g], f"l{l}_ff")
        elif l % 2 == 0:
            wts.append(mixer_wts(l, wl))
            xs, sv = _dn_fwd(xs, mod_g, lnp_of(l), wts[g], H, f"l{l}_dn")
        else:
            wts.append(mixer_wts(l, wl))
            xs, sv = _cf_fwd(xs, mod_g, lnp_of(l), wts[g], f"l{l}_cf")
        saved.append(sv)
    dx, loss_local = _loss_head(xs, tgt, "loss_head")
    loss = lax.psum(loss_local[0, 0], ("x", "y", "c"))

    def start_scatter(l, grads):
        names = list(group(l))
        axes = [big[nm][3] for nm in names]
        send, recv, sums, lands, token = _scatter_start([(grads[nm], ax) for nm, ax in zip(names, axes)], mods,
                                                        f"l{l}_scatter_start")
        return names, axes, sums, lands, send, recv, token

    def finish_scatter(l, pending, after):
        names, axes, sums, lands, send, recv, _ = pending
        sums, lands = _scatter_wait(list(zip(sums, axes)), lands, send, recv, after, f"l{l}_scatter_wait")
        mine = [_sum_landed(s, ax, ld, chip, f"l{l}_sum_grads_{nm}") for nm, ax, s, ld in zip(names, axes, sums, lands)]
        other = _exchange([(s[None], 0) for s in mine], "c", "gather", f"l{l}_swap_sums", keep_own=False)
        return {nm: (s, o) for nm, s, o in zip(names, mine, other)}

    g_dn = [None] * LA
    g_cf = [None] * LB
    dmods, dlns = [None] * L, [None] * L
    big_sums = {}
    pending, token = None, None
    small_of = {}
    for g in reversed(range(2 * L)):
        l, j = g // 2, g // 4
        mod_g = mod_of[g] if token is None else mod_of[g] + token[0, 0]
        grads = {}
        if g % 2:
            dx, small_of[g] = _mlp_bwd(dx, saved[g], mod_g, lnp_of(l), *wts[g], f"l{l}_ff", grads)
        elif l % 2 == 0:
            dx, g_dn[j], small_of[g] = _dn_bwd(dx, saved[g], mod_g, lnp_of(l), wts[g], H, f"l{l}_dn", grads)
            dn_in_g = jnp.concatenate([g_dn[j][0], g_dn[j][1][:, :2 * H]], axis=1)
            grads["dn_w_in"] = jnp.transpose(dn_in_g.reshape(D, 4, dn_in // 4), (1, 0, 2)).reshape(4 * D, dn_in // 4)
        else:
            dx, g_cf[j], small_of[g] = _cf_bwd(dx, saved[g], mod_g, lnp_of(l), wts[g], f"l{l}_cf", grads)
        if pending is not None:
            for nm, pair_of in finish_scatter(g + 1, pending, dx).items():
                big_sums[nm, group(g + 1)[nm]] = pair_of
        pending = start_scatter(g, grads)
        token = pending[6]
    big_names = ["dn_w_in", "dn_w_out", "cf_w_in", "cf_w_out", "ff_w1", "ff_w2"]
    big_res = {nm: None for nm in big_names}

    def update(nm, layer):
        w, m, v, _ = big[nm]
        own, oth = big_sums[nm, layer]
        big_res[nm] = _adamw_pair(w, m, v, layer, own, oth, ic, big_res[nm], f"adamw_{nm}_{layer}")

    for nm in big_names:
        for layer in range(big[nm][0].shape[0]):
            if group(0).get(nm) != layer:
                update(nm, layer)
    for nm, pair_of in finish_scatter(0, pending, big_res["ff_w2"][0]).items():
        big_sums[nm, group(0)[nm]] = pair_of
    for nm, layer in group(0).items():
        update(nm, layer)
    big_out = [big_res[nm] for nm in big_names]
    for l in range(L):
        (dss1, dp1), (dss2, dp2) = small_of[2 * l], small_of[2 * l + 1]
        dmods[l] = jnp.concatenate([dss1, dp1[0:1], dss2, dp2[0:1]], axis=0)
        dlns[l] = (jnp.stack([dp1[1], dp2[1]]), jnp.stack([dp1[2], dp2[2]]))
    grad_x = dx[None]

    d_ln_g = jnp.stack([dlns[l][0] for l in range(L)])
    d_ln_b = jnp.stack([dlns[l][1] for l in range(L)])
    d_conv_w = jnp.stack([g_dn[j][2] for j in range(LA)])
    d_a_log = jnp.stack([g_dn[j][3][0, H:2 * H] for j in range(LA)])
    d_dt_bias = jnp.stack([g_dn[j][3][1, H:2 * H] for j in range(LA)])
    d_norm_w = jnp.stack([g_dn[j][4][0] for j in range(LA)])
    d_dw_w = jnp.stack([g_cf[j][0] for j in range(LB)])
    d_dw_b = jnp.stack([g_cf[j][1][0] for j in range(LB)])
    d_cln_g = jnp.stack([g_cf[j][2][0] for j in range(LB)])
    d_cln_b = jnp.stack([g_cf[j][2][1] for j in range(LB)])
    d_mod = jnp.stack(dmods).reshape(L, NMOD * D)
    small_full = [d_mod, d_ln_g, d_ln_b, d_conv_w, d_dw_w, d_dw_b, d_cln_g, d_cln_b, d_a_log, d_dt_bias, d_norm_w]
    small_all = _exchange([(_pack(small_full, rows=8)[None], 0)], "all", "gather", "gather_small_grads")[0]
    small_sum = _sum_slots(small_all, "sum_small_grads")
    (s_ada_b, s_ln_g, s_ln_b, s_conv_w, s_dw_w, s_dw_b, s_cln_g, s_cln_b, s_a_log, s_dt_bias, s_norm_w) = _unpack(
        small_sum, [a.shape for a in small_full])
    d_mod_all = small_all.reshape(8, -1)[:, :L * NMOD * D].reshape(8, L, NMOD * D)

    def shard(a, axis):
        size = a.shape[axis] // 4
        return lax.dynamic_slice_in_dim(a, chip * size, size, axis)

    ncol = ada_w.shape[2]
    d_mod_sh = jnp.pad(lax.dynamic_slice_in_dim(d_mod_all, chip * ncol, ncol, 2), ((0, 8), (0, 0), (0, 0)))
    g_ada_w = None
    for l in range(L):
        g_ada_w = _mm(c_pad, d_mod_sh[:, l], "tn", name=f"ada_bw_{l}", a_fn=lambda t: t * _sigmoid(t),
                      stack=(g_ada_w, l, L))

    ada_out =[r.reshape(ada_w.shape) for r in _adamw(_two_d(ada_w), _two_d(m_ada_w), _two_d(v_ada_w),
                                                     _two_d(g_ada_w)[None], "adamw_ada_w")]

    small_w = [(ada_b, m_ada_b, v_ada_b, s_ada_b), (ln_g, m_ln_g, v_ln_g, shard(s_ln_g, 2)),
               (ln_b, m_ln_b, v_ln_b, shard(s_ln_b, 2)), (dn_conv_w, m_dn_conv_w, v_dn_conv_w, shard(s_conv_w, 2)),
               (dn_a_log, m_dn_a_log, v_dn_a_log, s_a_log), (dn_dt_bias, m_dn_dt_bias, v_dn_dt_bias, s_dt_bias),
               (dn_norm_w, m_dn_norm_w, v_dn_norm_w, s_norm_w), (cf_dw_w, m_cf_dw_w, v_cf_dw_w, shard(s_dw_w, 2)),
               (cf_dw_b, m_cf_dw_b, v_cf_dw_b, shard(s_dw_b, 1)), (cf_ln_g, m_cf_ln_g, v_cf_ln_g, shard(s_cln_g, 1)),
               (cf_ln_b, m_cf_ln_b, v_cf_ln_b, shard(s_cln_b, 1))]
    pk = [_pack([t[i] for t in small_w], rows=8) for i in range(4)]
    small_res = _adamw(pk[0], pk[1], pk[2], pk[3][None], "adamw_small")
    small_shapes = [t[0].shape for t in small_w]
    small_out = [_unpack(r, small_shapes) for r in small_res]

    def kind(k):
        sm = small_out[k]
        bg = [o[k] for o in big_out]
        return [ada_out[k], sm[0], sm[1], sm[2], bg[0], sm[3], sm[4], sm[5], sm[6], bg[1],
                bg[2], sm[7], sm[8], sm[9], sm[10], bg[3], bg[4], bg[5]]

    return (loss, grad_x, *kind(0), *kind(1), *kind(2), *kind(3))
```

```python
import functools

import jax
import jax.numpy as jnp
from jax import lax
from jax.experimental import pallas as pl
from jax.experimental.pallas import tpu as pltpu

F32 = jnp.float32
_MXU = jnp.bfloat16
_HI = lax.Precision.HIGHEST

N_LAYERS = 4
ALPHA = (2.0 * N_LAYERS) ** 0.25
LN_EPS = 1e-5
RMS_EPS = 1e-6
L2_EPS = 1e-6
CHUNK = 64
ADAM_LR, ADAM_B1, ADAM_B2, ADAM_EPS, ADAM_WD, ADAM_STEP = 0.001, 0.9, 0.999, 1e-08, 0.01, 10

LANES = 128
TOKEN_BLOCK = 256
VMEM_BIG = 48 * 1024 * 1024
MM_TILE = 1024

SDS = jax.ShapeDtypeStruct
MESH = pl.DeviceIdType.MESH


def _cparams(vmem=None):
    if vmem is None:
        return None
    return pltpu.CompilerParams(vmem_limit_bytes=vmem)


def _pcall(body, **kw):
    if kw.get("compiler_params", 1) is None:
        kw.pop("compiler_params")
    return pl.pallas_call(body, **kw)


def _full(arr):
    nd = arr.ndim
    return pl.BlockSpec(arr.shape, lambda *g: (0,) * nd)


def _bs(block, imap, lead=None):
    if lead is None:
        return pl.BlockSpec(block, imap)
    return pl.BlockSpec((None,) + tuple(block), lambda *g: (lead,) + tuple(imap(*g)))


def _split(a):
    return a if isinstance(a, tuple) else (a, None)


_GROUPS = {
    "xy": ([(1, 0, 0), (0, 1, 0), (1, 1, 0)], 4),
    "c": ([(0, 0, 1)], 2),
    "all": ([(1, 0, 0), (0, 1, 0), (1, 1, 0), (0, 0, 1), (1, 0, 1), (0, 1, 1), (1, 1, 1)], 8),
}


def _exchange(items, group, mode, name, nsplit=1, keep_own=True):
    masks, n = _GROUPS[group]
    npeer = len(masks)
    ni = len(items)
    arrs = [a for a, _ in items]
    out_shapes = []
    for a, ax in items:
        shp = list(a.shape)
        if mode == "gather":
            shp[ax] *= n
        else:
            shp[ax] //= n
            shp = [n] + shp
        out_shapes.append(SDS(tuple(shp), a.dtype))

    def body(*refs):
        ins, outs = refs[:ni], refs[ni:2 * ni]
        send_sems, recv_sems, local_sems = refs[2 * ni:]
        x, y, c = lax.axis_index("x"), lax.axis_index("y"), lax.axis_index("c")

        def slot(px, py, pc):
            if group == "xy":
                return 2 * px + py
            if group == "c":
                return pc
            return 4 * px + 2 * py + pc

        me = slot(x, y, c)

        def block(ref, ax, idx, size):
            ix = (slice(None),) * ax + (pl.ds(pl.multiple_of(idx * size, size), size),)
            return ref.at[ix]

        copies = []
        for it, (a, ax) in enumerate(items):
            in_ref, out_ref = ins[it], outs[it]
            if mode == "gather":
                size = a.shape[ax]
                src_own, dst_own = in_ref, block(out_ref, ax, me, size)
            else:
                size = a.shape[ax] // n
                src_own, dst_own = block(in_ref, ax, me, size), out_ref.at[me]
            sax, ns, cs = splits[it]
            pieces = [(slice(None),) * sax + (pl.ds(j * cs, cs),) for j in range(ns)]
            if keep_own:
                for j, piece in enumerate(pieces):
                    own = pltpu.make_async_copy(src_own.at[piece], dst_own.at[piece], local_sems.at[it * nsplit + j])
                    own.start()
                    copies.append(own)
            for k, m in enumerate(masks):
                peer = tuple((1 - v) if b else v for v, b in zip((x, y, c), m))
                if mode == "gather":
                    src, dst = in_ref, dst_own
                else:
                    src, dst = block(in_ref, ax, slot(*peer), size), out_ref.at[me]
                for j, piece in enumerate(pieces):
                    sem = (it * npeer + k) * nsplit + j
                    cp = pltpu.make_async_remote_copy(
                        src_ref=src.at[piece], dst_ref=dst.at[piece], send_sem=send_sems.at[sem],
                        recv_sem=recv_sems.at[sem], device_id=peer, device_id_type=MESH)
                    cp.start()
                    copies.append(cp)
        for cp in copies:
            cp.wait()

    splits = []
    for (a, ax), o in zip(items, out_shapes):
        bshape = a.shape if mode == "gather" else o.shape[1:]
        sax = max(range(len(bshape) - 1), key=lambda d: bshape[d])
        ns = nsplit if bshape[sax] % (nsplit * 16) == 0 else 1
        splits.append((sax, ns, bshape[sax] // ns))
    any_spec = pl.BlockSpec(memory_space=pl.ANY)
    nsem = ni * npeer * nsplit
    outs = _pcall(
        body, name=name, out_shape=tuple(out_shapes),
        in_specs=[any_spec] * ni, out_specs=tuple([any_spec] * ni),
        scratch_shapes=[pltpu.SemaphoreType.DMA((nsem,)), pltpu.SemaphoreType.DMA((nsem,)),
                        pltpu.SemaphoreType.DMA((ni * nsplit,))],
    )(*arrs)
    return list(outs)


_HBM = pl.BlockSpec(memory_space=pltpu.HBM)
_SEM = pl.BlockSpec(memory_space=pltpu.SEMAPHORE)
_ANY = pl.BlockSpec(memory_space=pl.ANY)
_SPLIT = pltpu.CompilerParams(has_side_effects=pltpu.SideEffectType.DATAFLOW_SIDE_EFFECTING)
_XY = _GROUPS["xy"][0]


def _in_hbm(a):
    return pltpu.with_memory_space_constraint(a, pltpu.HBM)


def _chip_peers():
    x, y, c = lax.axis_index("x"), lax.axis_index("y"), lax.axis_index("c")
    return x, y, c, [tuple((1 - v) if b else v for v, b in zip((x, y), m)) for m in _XY]


def _wblock(ref, ax, half, chip):
    R, C = ref.shape
    if ax == 0:
        rows = R // 8
        return ref.at[pl.ds(pl.multiple_of(chip * (2 * rows) + half * rows, rows), rows), :]
    rows, cols = R // 2, C // 4
    return ref.at[pl.ds(pl.multiple_of(half * rows, rows), rows), pl.ds(pl.multiple_of(chip * cols, cols), cols)]


def _gather_start(items, after, name):
    ni = len(items)
    arrs = [a for a, _ in items]

    def body(*refs):
        send_sems, recv_sems = refs[ni + 1], refs[ni + 2]
        outs, token = refs[ni + 3:2 * ni + 3], refs[2 * ni + 3]
        x, y, c, peers = _chip_peers()
        for it, (_, ax) in enumerate(items):
            mine = _wblock(outs[it], ax, c, 2 * x + y)
            for k, (px, py) in enumerate(peers):
                pltpu.make_async_remote_copy(
                    src_ref=mine, dst_ref=mine, send_sem=send_sems.at[it * 3 + k], recv_sem=recv_sems.at[it * 3 + k],
                    device_id=(px, py, c), device_id_type=MESH).start()
        token[...] = jnp.zeros_like(token)

    res = _pcall(
        body, name=name,
        out_shape=(pltpu.SemaphoreType.DMA((ni * 3,)), pltpu.SemaphoreType.DMA((ni * 3,)),
                   *[pltpu.HBM(a.shape, a.dtype) for a in arrs], SDS((8, LANES), F32)),
        in_specs=[_HBM] * ni + [_ANY],
        out_specs=(_SEM, _SEM, *[_HBM] * ni, pl.BlockSpec(memory_space=pltpu.VMEM)),
        input_output_aliases={i: 2 + i for i in range(ni)}, compiler_params=_SPLIT,
    )(*[_in_hbm(a) for a in arrs], after)
    return res[0], res[1], list(res[2:2 + ni]), res[2 + ni]


def _gather_wait(items, send_sems, recv_sems, after, name):
    ni = len(items)
    arrs = [a for a, _ in items]

    def body(*refs):
        s_sems, r_sems = refs[ni], refs[ni + 1]
        outs = refs[ni + 3:]
        x, y, c, peers = _chip_peers()
        for it, (_, ax) in enumerate(items):
            mine = _wblock(outs[it], ax, c, 2 * x + y)
            for k, (px, py) in enumerate(peers):
                cp = pltpu.make_async_remote_copy(
                    src_ref=mine, dst_ref=_wblock(outs[it], ax, c, 2 * px + py), send_sem=s_sems.at[it * 3 + k],
                    recv_sem=r_sems.at[it * 3 + k], device_id=(px, py, c), device_id_type=MESH)
                cp.wait_send()
                cp.wait_recv()

    res = _pcall(
        body, name=name, out_shape=tuple(pltpu.HBM(a.shape, a.dtype) for a in arrs),
        in_specs=[_HBM] * ni + [_SEM, _SEM, _ANY], out_specs=tuple([_HBM] * ni),
        input_output_aliases={i: i for i in range(ni)}, compiler_params=_SPLIT,
    )(*arrs, send_sems, recv_sems, after)
    return list(res)


def _gather_forward(items, name):
    ni = len(items)
    arrs = [a for a, _ in items]

    def body(*refs):
        outs = refs[ni:2 * ni]
        send_sems, recv_sems = refs[2 * ni:]
        x, y, c, peers = _chip_peers()
        def copy(it, ax, k, chip, dst_half):
            return pltpu.make_async_remote_copy(
                src_ref=_wblock(outs[it], ax, c, chip), dst_ref=_wblock(outs[it], ax, dst_half, chip),
                send_sem=send_sems.at[it * 3 + k], recv_sem=recv_sems.at[it * 3 + k],
                device_id=(x, y, 1 - c), device_id_type=MESH)

        for it, (_, ax) in enumerate(items):
            for k, (px, py) in enumerate(peers):
                copy(it, ax, k, 2 * px + py, c).start()
        for it, (_, ax) in enumerate(items):
            for k, (px, py) in enumerate(peers):
                copy(it, ax, k, 2 * px + py, c).wait_send()
                copy(it, ax, k, 2 * px + py, 1 - c).wait_recv()

    res = _pcall(
        body, name=name, out_shape=tuple(SDS(a.shape, a.dtype) for a in arrs),
        in_specs=[_ANY] * ni, out_specs=tuple([_ANY] * ni), input_output_aliases={i: i for i in range(ni)},
        scratch_shapes=[pltpu.SemaphoreType.DMA((ni * 3,)), pltpu.SemaphoreType.DMA((ni * 3,))],
    )(*arrs)
    return list(res)


def _gblock(ref, ax, chip):
    size = ref.shape[ax] // 4
    piece = pl.ds(pl.multiple_of(chip * size, size), size)
    return ref.at[piece, :] if ax == 0 else ref.at[:, piece]


def _scatter_start(items, after, name):
    ni = len(items)
    arrs = [a for a, _ in items]
    lands = []
    for a, ax in items:
        blk = (a.shape[0] // 4, a.shape[1]) if ax == 0 else (a.shape[0], a.shape[1] // 4)
        lands.append(lax.empty((4, *blk), a.dtype))

    def body(*refs):
        send_sems, recv_sems = refs[2 * ni + 1], refs[2 * ni + 2]
        srcs, dsts = refs[2 * ni + 3:3 * ni + 3], refs[3 * ni + 3:4 * ni + 3]
        token = refs[4 * ni + 3]
        x, y, c, peers = _chip_peers()
        for it, (_, ax) in enumerate(items):
            for k, (px, py) in enumerate(peers):
                pltpu.make_async_remote_copy(
                    src_ref=_gblock(srcs[it], ax, 2 * px + py), dst_ref=dsts[it].at[2 * x + y],
                    send_sem=send_sems.at[it * 3 + k], recv_sem=recv_sems.at[it * 3 + k],
                    device_id=(px, py, c), device_id_type=MESH).start()
        token[...] = jnp.zeros_like(token)

    res = _pcall(
        body, name=name,
        out_shape=(pltpu.SemaphoreType.DMA((ni * 3,)), pltpu.SemaphoreType.DMA((ni * 3,)),
                   *[pltpu.HBM(a.shape, a.dtype) for a in arrs], *[pltpu.HBM(a.shape, a.dtype) for a in lands],
                   SDS((8, LANES), F32)),
        in_specs=[_HBM] * (2 * ni) + [_ANY],
        out_specs=(_SEM, _SEM, *[_HBM] * (2 * ni), pl.BlockSpec(memory_space=pltpu.VMEM)),
        input_output_aliases={i: 2 + i for i in range(2 * ni)}, compiler_params=_SPLIT,
    )(*[_in_hbm(a) for a in arrs], *[_in_hbm(a) for a in lands], after)
    return res[0], res[1], list(res[2:2 + ni]), list(res[2 + ni:2 + 2 * ni]), res[2 + 2 * ni]


def _scatter_wait(items, lands, send_sems, recv_sems, after, name):
    ni = len(items)
    arrs = [a for a, _ in items]

    def body(*refs):
        s_sems, r_sems = refs[2 * ni], refs[2 * ni + 1]
        srcs, dsts = refs[2 * ni + 3:3 * ni + 3], refs[3 * ni + 3:]
        x, y, c, peers = _chip_peers()
        for it, (_, ax) in enumerate(items):
            for k, (px, py) in enumerate(peers):
                cp = pltpu.make_async_remote_copy(
                    src_ref=_gblock(srcs[it], ax, 2 * px + py), dst_ref=dsts[it].at[2 * px + py],
                    send_sem=s_sems.at[it * 3 + k], recv_sem=r_sems.at[it * 3 + k],
                    device_id=(px, py, c), device_id_type=MESH)
                cp.wait_send()
                cp.wait_recv()

    res = _pcall(
        body, name=name, out_shape=tuple(pltpu.HBM(a.shape, a.dtype) for a in arrs + list(lands)),
        in_specs=[_HBM] * (2 * ni) + [_SEM, _SEM, _ANY], out_specs=tuple([_HBM] * (2 * ni)),
        input_output_aliases={i: i for i in range(2 * ni)}, compiler_params=_SPLIT,
    )(*arrs, *lands, send_sems, recv_sems, after)
    return list(res[:ni]), list(res[ni:])


def _tile(n, cap):
    if n <= cap:
        return n
    t = cap - cap % LANES
    while n % t:
        t -= LANES
    return t


def _mm(a, b, mode, *, name, out_dtypes=(F32,), tm=MM_TILE, tn=MM_TILE, tk=MM_TILE, a_fn=None, out_fn=None, aux=(),
        stack=None):
    (a, a_lead), (b, b_lead) = _split(a), _split(b)
    ash, bsh = a.shape[-2:], b.shape[-2:]
    if mode == "nn":
        (M, K), (_, N) = ash, bsh
    elif mode == "nt":
        (M, K), (N, _) = ash, bsh
    else:
        (K, M), (_, N) = ash, bsh
    tm, tn, tk = _tile(M, tm), _tile(N, tn), _tile(K, tk)
    nk = K // tk
    if mode == "tn":
        a_spec = _bs((tk, tm), lambda i, j, k: (k, i), a_lead)
    else:
        a_spec = _bs((tm, tk), lambda i, j, k: (i, k), a_lead)
    if mode == "nt":
        b_spec = _bs((tn, tk), lambda i, j, k: (j, k), b_lead)
    else:
        b_spec = _bs((tk, tn), lambda i, j, k: (k, j), b_lead)
    aux_arrs, aux_specs = [], []
    for arr, kind in aux:
        arr, lead = _split(arr)
        aux_arrs.append(arr)
        if kind == "mn":
            aux_specs.append(_bs((tm, tn), lambda i, j, k: (i, j), lead))
        else:
            aux_specs.append(_bs((1, tn), lambda i, j, k: (0, j), lead))
    na, no = len(aux_arrs), len(out_dtypes)
    dims = {"nn": (((1,), (0,)), ((), ())), "nt": (((1,), (1,)), ((), ())), "tn": (((0,), (0,)), ((), ()))}[mode]

    def finish(r, aux_refs, o_refs):
        outs = out_fn(r, *[x[...] for x in aux_refs]) if out_fn is not None else (r,)
        for o_ref, val in zip(o_refs, outs):
            o_ref[...] = val.astype(o_ref.dtype)

    def product(a_ref, b_ref):
        av = a_ref[...]
        if a_fn is not None:
            av = a_fn(av.astype(F32))
        return lax.dot_general(av.astype(_MXU), b_ref[...].astype(_MXU), dims, preferred_element_type=F32)

    nbuf = 0 if stack is None or stack[0] is None else 1

    def body_one(a_ref, b_ref, *rest):
        finish(product(a_ref, b_ref), rest[:na], rest[na + nbuf:na + nbuf + no])

    def body_acc(a_ref, b_ref, *rest):
        aux_refs, o_refs, acc = rest[:na], rest[na + nbuf:na + nbuf + no], rest[na + nbuf + no]
        k = pl.program_id(2)

        @pl.when(k == 0)
        def _():
            acc[...] = product(a_ref, b_ref)

        @pl.when(k != 0)
        def _():
            acc[...] += product(a_ref, b_ref)

        @pl.when(k == nk - 1)
        def _():
            finish(acc[...], aux_refs, o_refs)

    extra, aliases = {}, []
    if stack is None:
        out_shape = (M, N)
        o_spec = pl.BlockSpec((tm, tn), lambda i, j, k: (i, j))
    else:
        buf, layer, n_layers = stack
        assert no == 1
        out_shape = (n_layers, M, N)
        o_spec = pl.BlockSpec((None, tm, tn), lambda i, j, k: (layer, i, j))
        if buf is not None:
            aliases = [buf]
            extra = dict(input_output_aliases={2 + na: 0})
    outs = _pcall(
        body_one if nk == 1 else body_acc, name=name, grid=(M // tm, N // tn, nk),
        in_specs=[a_spec, b_spec] + aux_specs + [pl.BlockSpec(memory_space=pl.ANY)] * nbuf,
        out_specs=tuple([o_spec] * no),
        out_shape=tuple(SDS(out_shape, dt) for dt in out_dtypes),
        scratch_shapes=[] if nk == 1 else [pltpu.VMEM((tm, tn), F32)],
        compiler_params=pltpu.CompilerParams(dimension_semantics=("parallel", "parallel", "arbitrary"),
                                             vmem_limit_bytes=VMEM_BIG),
        **extra,
    )(a, b, *aux_arrs, *aliases)
    return outs[0] if no == 1 else outs


def _tok(S):
    ts = min(TOKEN_BLOCK, S)
    assert S % ts == 0
    return ts


def _row(ts, D):
    return pl.BlockSpec((ts, D), lambda i: (i, 0))


def _acc_rows(ref, i, rows):
    @pl.when(i == 0)
    def _():
        for r, v in enumerate(rows):
            ref[r:r + 1, :] = v

    @pl.when(i != 0)
    def _():
        for r, v in enumerate(rows):
            ref[r:r + 1, :] += v


def _modulate(x, mod, r_sh, r_sc, name):
    S, D = x.shape
    ts = _tok(S)

    def body(x_ref, m_ref, o_ref):
        o_ref[...] = (x_ref[...] * (1.0 + m_ref[r_sc:r_sc + 1, :]) + m_ref[r_sh:r_sh + 1, :]).astype(o_ref.dtype)

    return _pcall(body, name=name, grid=(S // ts,), in_specs=[_row(ts, D), _full(mod)],
                  out_specs=_row(ts, D), out_shape=SDS((S, D), _MXU))(x, mod)


def _modulate_bwd(x, mod, r_sc, dh, dxa, name):
    S, D = x.shape
    ts = _tok(S)

    def body(x_ref, m_ref, dh_ref, dxa_ref, dx_ref, dss_ref):
        dh_v = dh_ref[...]
        dx_ref[...] = dxa_ref[...] + dh_v * (1.0 + m_ref[r_sc:r_sc + 1, :])
        _acc_rows(dss_ref, pl.program_id(0),
                  [jnp.sum(dh_v, axis=0, keepdims=True), jnp.sum(dh_v * x_ref[...], axis=0, keepdims=True)])

    return _pcall(body, name=name, grid=(S // ts,),
                  in_specs=[_row(ts, D), _full(mod), _row(ts, D), _row(ts, D)],
                  out_specs=(_row(ts, D), pl.BlockSpec((2, D), lambda i: (0, 0))),
                  out_shape=(SDS((S, D), F32), SDS((2, D), F32)))(x, mod, dh, dxa)


def _norm_stats(z):
    mu = jnp.mean(z, axis=-1, keepdims=True)
    zc = z - mu
    var = jnp.mean(zc * zc, axis=-1, keepdims=True)
    rstd = lax.rsqrt(var + LN_EPS)
    return zc * rstd, rstd


def _norm_bwd(dxhat, xhat, rstd):
    return rstd * (dxhat - jnp.mean(dxhat, axis=-1, keepdims=True)
                   - xhat * jnp.mean(dxhat * xhat, axis=-1, keepdims=True))


def _combine(x, y, mod, r_gt, lnp, r_g, name):
    S, D = x.shape
    ts = _tok(S)

    def body(x_ref, y_ref, m_ref, l_ref, o_ref):
        z = ALPHA * x_ref[...] + (1.0 + m_ref[r_gt:r_gt + 1, :]) * y_ref[...]
        xhat, _ = _norm_stats(z)
        o_ref[...] = xhat * l_ref[r_g:r_g + 1, :] + l_ref[r_g + 1:r_g + 2, :]

    return _pcall(body, name=name, grid=(S // ts,), in_specs=[_row(ts, D), _row(ts, D), _full(mod), _full(lnp)],
                  out_specs=_row(ts, D), out_shape=SDS((S, D), F32))(x, y, mod, lnp)


def _combine_bwd(x, y, mod, r_gt, lnp, r_g, dout, name):
    S, D = x.shape
    ts = _tok(S)

    def body(x_ref, y_ref, m_ref, l_ref, do_ref, dxa_ref, dy_ref, dp_ref):
        gate = 1.0 + m_ref[r_gt:r_gt + 1, :]
        y_v, do_v = y_ref[...], do_ref[...]
        xhat, rstd = _norm_stats(ALPHA * x_ref[...] + gate * y_v)
        dz = _norm_bwd(do_v * l_ref[r_g:r_g + 1, :], xhat, rstd)
        dxa_ref[...] = ALPHA * dz
        dy_ref[...] = (gate * dz).astype(dy_ref.dtype)
        _acc_rows(dp_ref, pl.program_id(0),
                  [jnp.sum(dz * y_v, axis=0, keepdims=True), jnp.sum(do_v * xhat, axis=0, keepdims=True),
                   jnp.sum(do_v, axis=0, keepdims=True)])

    return _pcall(body, name=name, grid=(S // ts,),
                  in_specs=[_row(ts, D), _row(ts, D), _full(mod), _full(lnp), _row(ts, D)],
                  out_specs=(_row(ts, D), _row(ts, D), pl.BlockSpec((3, D), lambda i: (0, 0))),
                  out_shape=(SDS((S, D), F32), SDS((S, D), _MXU), SDS((3, D), F32)))(x, y, mod, lnp, dout)


def _sigmoid(t):
    return 1.0 / (1.0 + jnp.exp(-t))


def _ln_silu(u, lnp, name):
    S, D = u.shape
    ts = _tok(S)

    def body(u_ref, l_ref, o_ref):
        xhat, _ = _norm_stats(u_ref[...])
        t = xhat * l_ref[0:1, :] + l_ref[1:2, :]
        o_ref[...] = (t * _sigmoid(t)).astype(o_ref.dtype)

    return _pcall(body, name=name, grid=(S // ts,), in_specs=[_row(ts, D), _full(lnp)],
                  out_specs=_row(ts, D), out_shape=SDS((S, D), _MXU))(u, lnp)


def _ln_silu_bwd(u, lnp, dout, name):
    S, D = u.shape
    ts = _tok(S)

    def body(u_ref, l_ref, do_ref, du_ref, dp_ref):
        xhat, rstd = _norm_stats(u_ref[...])
        g = l_ref[0:1, :]
        t = xhat * g + l_ref[1:2, :]
        sg = _sigmoid(t)
        dt = do_ref[...] * (sg * (1.0 + t * (1.0 - sg)))
        du_ref[...] = _norm_bwd(dt * g, xhat, rstd)
        _acc_rows(dp_ref, pl.program_id(0),
                  [jnp.sum(dt * xhat, axis=0, keepdims=True), jnp.sum(dt, axis=0, keepdims=True)])

    return _pcall(body, name=name, grid=(S // ts,), in_specs=[_row(ts, D), _full(lnp), _row(ts, D)],
                  out_specs=(_row(ts, D), pl.BlockSpec((2, D), lambda i: (0, 0))),
                  out_shape=(SDS((S, D), F32), SDS((2, D), F32)))(u, lnp, dout)


def _loss_head(xf, tgt, name):
    S, D = xf.shape
    ts = _tok(S)

    def body(x_ref, t_ref, dx_ref, l_ref):
        err = x_ref[...] - t_ref[...]
        dx_ref[...] = err * (1.0 / D)
        part = jnp.sum(jnp.sum(err * err, axis=1, keepdims=True), axis=0, keepdims=True) * (0.5 / D)

        @pl.when(pl.program_id(0) == 0)
        def _():
            l_ref[...] = part

        @pl.when(pl.program_id(0) != 0)
        def _():
            l_ref[...] += part

    return _pcall(body, name=name, grid=(S // ts,), in_specs=[_row(ts, D), _row(ts, D)],
                  out_specs=(_row(ts, D), pl.BlockSpec((1, 1), lambda i: (0, 0))),
                  out_shape=(SDS((S, D), F32), SDS((1, 1), F32)))(xf, tgt)


CONV_PAD = 32


def _padded(u, before):
    zeros = jnp.zeros((CONV_PAD, u.shape[1]), u.dtype)
    return jnp.concatenate([zeros, u] if before else [u, zeros], axis=0)


SUBLANES = 8


def _shift_down(u_pad, s, rolled):
    q, r = divmod(s, SUBLANES)
    if r not in rolled:
        rolled[r] = u_pad if r == 0 else pltpu.roll(u_pad, r, 0)
    start = CONV_PAD - SUBLANES * q
    return rolled[r][start:start + u_pad.shape[0] - CONV_PAD]


def _shift_up(u_pad, s, rolled):
    n = u_pad.shape[0]
    q, r = divmod(s, SUBLANES)
    if r not in rolled:
        rolled[r] = u_pad if r == 0 else pltpu.roll(u_pad, n - r, 0)
    return rolled[r][SUBLANES * q:SUBLANES * q + n - CONV_PAD]


def _dwconv(u, w_ref, taps, rows):
    del rows
    assert taps - 1 <= CONV_PAD
    u_pad, rolled = _padded(u, True), {}
    acc = jnp.zeros_like(u)
    for j in range(taps):
        acc = acc + w_ref[j:j + 1, :] * _shift_down(u_pad, taps - 1 - j, rolled)
    return acc


def _dwconv_bwd(u, dy, w_ref, dw_ref, taps, rows):
    del rows
    assert taps - 1 <= CONV_PAD
    u_pad, dy_pad, u_rolled, dy_rolled = _padded(u, True), _padded(dy, False), {}, {}
    du = jnp.zeros_like(u)
    for j in range(taps):
        s = taps - 1 - j
        du = du + w_ref[j:j + 1, :] * _shift_up(dy_pad, s, dy_rolled)
        dw_ref[j:j + 1, :] = jnp.sum(dy * _shift_down(u_pad, s, u_rolled), axis=0, keepdims=True)
    return du


def _col(S, j0=0):
    return pl.BlockSpec((S, LANES), lambda j: (0, j + j0))


def _conv_silu(pm, w, nblk, name):
    S = pm.shape[0]
    taps = w.shape[0]

    def body(u_ref, w_ref, o_ref):
        rows = lax.broadcasted_iota(jnp.int32, (S, LANES), 0)
        cv = _dwconv(u_ref[...], w_ref, taps, rows)
        o_ref[...] = cv * _sigmoid(cv)

    return _pcall(body, name=name, grid=(nblk,),
                  in_specs=[_col(S), pl.BlockSpec((taps, LANES), lambda j: (0, j))],
                  out_specs=_col(S), out_shape=SDS((S, nblk * LANES), F32),
                  compiler_params=_cparams(VMEM_BIG))(pm, w)


def _conv_silu_bwd(pm, w, dout, dpm, j0, name):
    S = pm.shape[0]
    taps = w.shape[0]
    nblk = dout.shape[1] // LANES

    def body(u_ref, w_ref, do_ref, dpm_in, du_ref, dw_ref):
        del dpm_in
        rows = lax.broadcasted_iota(jnp.int32, (S, LANES), 0)
        u = u_ref[...]
        cv = _dwconv(u, w_ref, taps, rows)
        sg = _sigmoid(cv)
        dc = do_ref[...] * (sg * (1.0 + cv * (1.0 - sg)))
        du_ref[...] = _dwconv_bwd(u, dc, w_ref, dw_ref, taps, rows)

    return _pcall(body, name=name, grid=(nblk,),
                  in_specs=[_col(S, j0), pl.BlockSpec((taps, LANES), lambda j: (0, j + j0)), _col(S),
                            pl.BlockSpec(memory_space=pl.ANY)],
                  out_specs=(_col(S, j0), pl.BlockSpec((taps, LANES), lambda j: (0, j))),
                  out_shape=(SDS(dpm.shape, F32), SDS((taps, nblk * LANES), F32)),
                  input_output_aliases={3: 0},
                  compiler_params=_cparams(VMEM_BIG))(pm, w, dout, dpm)


def _glu_conv(p, w, bias, name):
    S, C2 = p.shape
    nblk = C2 // 2 // LANES
    taps = w.shape[0]

    def body(v_ref, g_ref, w_ref, b_ref, o_ref):
        rows = lax.broadcasted_iota(jnp.int32, (S, LANES), 0)
        u = v_ref[...] * _sigmoid(g_ref[...])
        o_ref[...] = _dwconv(u, w_ref, taps, rows) + b_ref[...]

    return _pcall(body, name=name, grid=(nblk,),
                  in_specs=[_col(S), _col(S, nblk), pl.BlockSpec((taps, LANES), lambda j: (0, j)),
                            pl.BlockSpec((1, LANES), lambda j: (0, j))],
                  out_specs=_col(S), out_shape=SDS((S, nblk * LANES), F32),
                  compiler_params=_cparams(VMEM_BIG))(p, p, w, bias)


def _glu_conv_bwd(p, w, dout, name):
    S, C2 = p.shape
    nblk = C2 // 2 // LANES
    taps = w.shape[0]

    def body(v_ref, g_ref, w_ref, do_ref, dv_ref, dg_ref, dw_ref, db_ref):
        rows = lax.broadcasted_iota(jnp.int32, (S, LANES), 0)
        val, sg = v_ref[...], _sigmoid(g_ref[...])
        do_v = do_ref[...]
        du = _dwconv_bwd(val * sg, do_v, w_ref, dw_ref, taps, rows)
        dv_ref[...] = du * sg
        dg_ref[...] = du * val * sg * (1.0 - sg)
        db_ref[...] = jnp.sum(do_v, axis=0, keepdims=True)

    dval, dgate, dw, db = _pcall(
        body, name=name, grid=(nblk,),
        in_specs=[_col(S), _col(S, nblk), pl.BlockSpec((taps, LANES), lambda j: (0, j)), _col(S)],
        out_specs=(_col(S), _col(S), pl.BlockSpec((taps, LANES), lambda j: (0, j)),
                   pl.BlockSpec((1, LANES), lambda j: (0, j))),
        out_shape=(SDS((S, C2 // 2), F32), SDS((S, C2 // 2), F32), SDS((taps, C2 // 2), F32), SDS((1, C2 // 2), F32)),
        compiler_params=_cparams(VMEM_BIG))(p, p, w, dout)
    return dval, dgate, dw, db


def _log1p(e):
    u = 1.0 + e
    d = jnp.where(u == 1.0, 1.0, u - 1.0)
    return jnp.where(u == 1.0, e, jnp.log(u) * (e / d))


def _gate_parts(ps, prm, H):
    lane = lax.broadcasted_iota(jnp.int32, ps.shape, 1)
    is_b, is_g = lane < H, (lane >= H) & (lane < 2 * H)
    beta = _sigmoid(ps)
    t = ps + prm[1:2, :]
    sp = jnp.maximum(t, 0.0) + _log1p(jnp.exp(-jnp.abs(t)))
    na = -jnp.exp(prm[0:1, :])
    return is_b, is_g, beta, t, sp, na


def _gates(ps, prm, H, name):
    S = ps.shape[0]
    ts = _tok(S)

    def body(p_ref, r_ref, o_ref):
        is_b, is_g, beta, _, sp, na = _gate_parts(p_ref[...], r_ref[...], H)
        o_ref[...] = jnp.where(is_b, beta, jnp.where(is_g, na * sp, 0.0))

    return _pcall(body, name=name, grid=(S // ts,), in_specs=[_row(ts, LANES), _full(prm)],
                  out_specs=_row(ts, LANES), out_shape=SDS((S, LANES), F32))(ps, prm)


def _gates_bwd(ps, prm, dgates, H, name):
    S = ps.shape[0]
    ts = _tok(S)

    def body(p_ref, r_ref, dg_ref, dp_ref, dr_ref):
        is_b, is_g, beta, t, sp, na = _gate_parts(p_ref[...], r_ref[...], H)
        dg_v = dg_ref[...]
        dsp = jnp.where(is_g, dg_v * na * _sigmoid(t), 0.0)
        dp_ref[...] = jnp.where(is_b, dg_v * beta * (1.0 - beta), dsp)
        _acc_rows(dr_ref, pl.program_id(0),
                  [jnp.sum(jnp.where(is_g, dg_v * na * sp, 0.0), axis=0, keepdims=True),
                   jnp.sum(dsp, axis=0, keepdims=True)])

    return _pcall(body, name=name, grid=(S // ts,), in_specs=[_row(ts, LANES), _full(prm), _row(ts, LANES)],
                  out_specs=(_row(ts, LANES), pl.BlockSpec((2, LANES), lambda i: (0, 0))),
                  out_shape=(SDS((S, LANES), F32), SDS((2, LANES), F32)))(ps, prm, dgates)


_NN = (((2,), (1,)), ((0,), (0,)))
_NT = (((2,), (2,)), ((0,), (0,)))
_TN = (((1,), (1,)), ((0,), (0,)))


def _mdot(a, b, dims):
    return lax.dot_general(a.astype(_MXU), b.astype(_MXU), dims, preferred_element_type=F32)


def _mdot3(a, b, dims):
    ah, bh = a.astype(_MXU), b.astype(_MXU)
    al, bl = a - ah.astype(F32), b - bh.astype(F32)
    return _mdot(ah, bh, dims) + (_mdot(ah, bl, dims) + _mdot(al, bh, dims))


def _rounded_dot(dims, da_dims, db_dims, a_first, prod=_mdot):
    @jax.custom_vjp
    def f(a, b):
        return prod(a, b, dims)

    def fwd(a, b):
        return prod(a, b, dims), (a, b)

    def bwd(res, ct):
        a, b = res
        da = prod(ct, b, da_dims) if a_first[0] else prod(b, ct, da_dims)
        db = prod(ct, a, db_dims) if a_first[1] else prod(a, ct, db_dims)
        return da, db

    f.defvjp(fwd, bwd)
    return f


_mdot_nn = _rounded_dot(_NN, _NT, _TN, (True, False))
_mdot_nt = _rounded_dot(_NT, _NN, _TN, (True, True))
_mdot_tn = _rounded_dot(_TN, _NT, _NN, (False, False))
def _unit_lower_inverse(a):
    C = a.shape[-1]
    ri = lax.broadcasted_iota(jnp.int32, (1, C, C), 1)
    ci = lax.broadcasted_iota(jnp.int32, (1, C, C), 2)
    t_inv = jnp.where(ri == ci, 1.0, 0.0) - a
    p = a
    for _ in range(max(C.bit_length() - 2, 0)):
        p = _mdot3(p, p, _NN)
        t_inv = t_inv + _mdot3(t_inv, p, _NN)
    return t_inv


@jax.custom_vjp
def _known_inverse(a, t_inv):
    del a
    return t_inv


def _known_inverse_fwd(a, t_inv):
    del a
    return t_inv, t_inv


def _known_inverse_bwd(t_inv, ct):
    da = -_mdot3(_mdot3(t_inv, ct, _TN), t_inv, _NT)
    return da, jnp.zeros_like(t_inv)


_known_inverse.defvjp(_known_inverse_fwd, _known_inverse_bwd)


def _head_cols(gates, off, H):
    lane = lax.broadcasted_iota(jnp.int32, gates.shape, 1)
    cols = [jnp.sum(jnp.where(lane == off + h, gates, 0.0), axis=-1, keepdims=True) for h in range(H)]
    return jnp.concatenate([col[None] for col in cols], axis=0)


def _delta_chunk(qr, kr, v, z, gates, nw, s_in, t_known=None):
    H, C, dk = qr.shape
    beta, g = _head_cols(gates, 0, H), _head_cols(gates, H, H)
    q = qr * lax.rsqrt(jnp.sum(qr * qr, axis=-1, keepdims=True) + L2_EPS) * (dk ** -0.5)
    k = kr * lax.rsqrt(jnp.sum(kr * kr, axis=-1, keepdims=True) + L2_EPS)
    ri = lax.broadcasted_iota(jnp.int32, (1, C, C), 1)
    ci = lax.broadcasted_iota(jnp.int32, (1, C, C), 2)
    causal, strict, eye = ri >= ci, ri > ci, ri == ci
    gam_row = jnp.sum(jnp.where(ri <= ci, g, 0.0), axis=1, keepdims=True)
    gam_col = jnp.sum(jnp.where(eye, gam_row, 0.0), axis=-1, keepdims=True)
    g_last = jnp.sum(g, axis=1, keepdims=True)
    decay = jnp.where(causal, jnp.exp(jnp.where(causal, gam_col - gam_row, 0.0)), 0.0)
    kb = k * beta
    a = jnp.where(strict, _mdot_nt(kb, k) * decay, 0.0)
    t_inv = _unit_lower_inverse(a) if t_known is None else _known_inverse(a, t_known)
    eg = jnp.exp(gam_col)
    u = _mdot_nn(t_inv, v * beta)
    w = _mdot_nn(t_inv, kb * eg)
    a_qk = _mdot_nt(q, k) * decay
    v_new = u - _mdot_nn(w, s_in)
    o = _mdot_nn(q * eg, s_in) + _mdot_nn(a_qk, v_new)
    s_out = s_in * jnp.exp(g_last) + _mdot_tn(k * jnp.exp(g_last - gam_col), v_new)
    og = o * lax.rsqrt(jnp.mean(o * o, axis=-1, keepdims=True) + RMS_EPS) * nw * (z * _sigmoid(z))
    return og, s_out, t_inv


def _heads(ref, H, dk):
    return jnp.stack([ref[:, h * dk:(h + 1) * dk].astype(F32) for h in range(H)])


def _put_heads(ref, val, dk):
    for h in range(val.shape[0]):
        ref[:, h * dk:(h + 1) * dk] = val[h].astype(ref.dtype)


def _delta_fwd(qkv, pm, gates, nw, H, name):
    S = qkv.shape[0]
    hd = qkv.shape[1] // 3
    dk = hd // H
    N = S // CHUNK
    blk = lambda off: pl.BlockSpec((CHUNK, hd), lambda n: (n, off))

    def body(q_ref, k_ref, v_ref, z_ref, g_ref, nw_ref, og_ref, st_ref, ti_ref, s_scr):
        @pl.when(pl.program_id(0) == 0)
        def _():
            s_scr[...] = jnp.zeros_like(s_scr)

        s_in = s_scr[...]
        st_ref[...] = s_in
        og, s_out, t_inv = _delta_chunk(_heads(q_ref, H, dk), _heads(k_ref, H, dk), _heads(v_ref, H, dk),
                                        _heads(z_ref, H, dk), g_ref[...], nw_ref[...], s_in)
        _put_heads(og_ref, og, dk)
        ti_ref[...] = t_inv
        s_scr[...] = s_out

    return _pcall(
        body, name=name, grid=(N,),
        in_specs=[blk(0), blk(1), blk(2), blk(3), pl.BlockSpec((CHUNK, LANES), lambda n: (n, 0)), _full(nw)],
        out_specs=(blk(0), pl.BlockSpec((None, H, dk, dk), lambda n: (n, 0, 0, 0)),
                   pl.BlockSpec((None, H, CHUNK, CHUNK), lambda n: (n, 0, 0, 0))),
        out_shape=(SDS((S, hd), _MXU), SDS((N, H, dk, dk), F32), SDS((N, H, CHUNK, CHUNK), F32)),
        scratch_shapes=[pltpu.VMEM((H, dk, dk), F32)],
        compiler_params=_cparams(VMEM_BIG),
    )(qkv, qkv, qkv, pm, gates, nw)


def _delta_bwd(qkv, pm, gates, nw, states, t_invs, dog, H, name):
    S = qkv.shape[0]
    hd = qkv.shape[1] // 3
    dk = hd // H
    N = S // CHUNK
    blk = lambda off: pl.BlockSpec((CHUNK, hd), lambda n: (N - 1 - n, off))
    gspec = pl.BlockSpec((CHUNK, LANES), lambda n: (N - 1 - n, 0))

    def body(q_ref, k_ref, v_ref, z_ref, g_ref, nw_ref, st_ref, ti_ref, do_ref,
             dq_ref, dk_ref, dv_ref, dz_ref, dg_ref, dnw_ref, ds_scr):
        n = pl.program_id(0)

        @pl.when(n == 0)
        def _():
            ds_scr[...] = jnp.zeros_like(ds_scr)

        t_known = ti_ref[...]
        fn = functools.partial(_delta_chunk, t_known=t_known)
        _, vjp = jax.vjp(fn, _heads(q_ref, H, dk), _heads(k_ref, H, dk), _heads(v_ref, H, dk),
                         _heads(z_ref, H, dk), g_ref[...], nw_ref[...], st_ref[...])
        dq, dkk, dv, dz, dg, dnw, ds_in = vjp((_heads(do_ref, H, dk), ds_scr[...], jnp.zeros_like(t_known)))
        _put_heads(dq_ref, dq, dk)
        _put_heads(dk_ref, dkk, dk)
        _put_heads(dv_ref, dv, dk)
        _put_heads(dz_ref, dz, dk)
        ds_scr[...] = ds_in
        dg_ref[...] = dg

        @pl.when(n == 0)
        def _():
            dnw_ref[...] = dnw

        @pl.when(n != 0)
        def _():
            dnw_ref[...] += dnw

    return _pcall(
        body, name=name, grid=(N,),
        in_specs=[blk(0), blk(1), blk(2), blk(3), gspec, _full(nw),
                  pl.BlockSpec((None, H, dk, dk), lambda n: (N - 1 - n, 0, 0, 0)),
                  pl.BlockSpec((None, H, CHUNK, CHUNK), lambda n: (N - 1 - n, 0, 0, 0)), blk(0)],
        out_specs=(blk(0), blk(0), blk(0), blk(3), gspec, pl.BlockSpec((1, dk), lambda n: (0, 0))),
        out_shape=(SDS((S, hd), F32), SDS((S, hd), F32), SDS((S, hd), F32), SDS(pm.shape, F32),
                   SDS((S, LANES), F32), SDS((1, dk), F32)),
        scratch_shapes=[pltpu.VMEM((H, dk, dk), F32)],
        compiler_params=_cparams(VMEM_BIG),
    )(qkv, qkv, qkv, pm, gates, nw, states, t_invs, dog)


def _rows_block(R, C):
    rb = R
    while rb * C * 4 > (1 << 20) and rb % 16 == 0:
        rb //= 2
    return rb


def _sum_slots(st, name, out_dtype=F32):
    n, R, C = st.shape
    rb = _rows_block(R, C)

    def body(s_ref, o_ref):
        acc = s_ref[0].astype(F32)
        for q in range(1, n):
            acc = acc + s_ref[q].astype(F32)
        o_ref[...] = acc.astype(o_ref.dtype)

    return _pcall(body, name=name, grid=(R // rb,), in_specs=[pl.BlockSpec((n, rb, C), lambda i: (0, i, 0))],
                  out_specs=pl.BlockSpec((rb, C), lambda i: (i, 0)), out_shape=SDS((R, C), out_dtype))(st)


def _scalar(v):
    return jnp.reshape(v, (1,)).astype(jnp.int32)


def _place_block(w, layer, ax, chip, dep, name):
    _, R, C = w.shape
    rb = _rows_block(R, C)
    nrb = R // rb
    shp = [R, C]
    shp[ax] *= 4
    omap = (lambda i, c: (c[0] * nrb + i, 0)) if ax == 0 else (lambda i, c: (i, c[0]))

    def body(c_ref, w_ref, dep_ref, o_ref):
        del c_ref, dep_ref
        o_ref[...] = w_ref[...].astype(o_ref.dtype)

    grid_spec = pltpu.PrefetchScalarGridSpec(
        num_scalar_prefetch=1, grid=(nrb,),
        in_specs=[pl.BlockSpec((None, rb, C), lambda i, c: (layer, i, 0)), _ANY],
        out_specs=pl.BlockSpec((rb, C), omap))
    return _pcall(body, name=name, grid_spec=grid_spec, out_shape=SDS(tuple(shp), _MXU))(_scalar(chip), w, dep)


def _sum_pair(own, recv, ic, name, out_dtype):
    N, _, R, C = own.shape
    rb = _rows_block(R, C)

    def body(c_ref, a_ref, b_ref, o_ref):
        del c_ref
        o_ref[...] = (a_ref[...].astype(F32) + b_ref[...].astype(F32)).astype(o_ref.dtype)

    grid_spec = pltpu.PrefetchScalarGridSpec(
        num_scalar_prefetch=1, grid=(N, R // rb),
        in_specs=[pl.BlockSpec((None, None, rb, C), lambda p, i, c: (p, c[0], i, 0)),
                  pl.BlockSpec((None, None, rb, C), lambda p, i, c: (1 - c[0], p, i, 0))],
        out_specs=pl.BlockSpec((None, rb, C), lambda p, i, c: (p, i, 0)))
    return _pcall(body, name=name, grid_spec=grid_spec, out_shape=SDS((N, R, C), out_dtype))(
        _scalar(ic), own, recv)


def _sum_landed(grad, ax, land, chip, name):
    _, R, Cb = land.shape
    rb = _rows_block(R, Cb)
    nrb = R // rb
    if ax == 0:
        own_spec = pl.BlockSpec((rb, Cb), lambda i, c: (c[0] * nrb + i, 0))
    else:
        own_spec = pl.BlockSpec((rb, Cb), lambda i, c: (i, c[0]))
    slot = lambda d: pl.BlockSpec((None, rb, Cb), lambda i, c: ((c[0] + d) % 4, i, 0))

    def body(c_ref, own_ref, r1, r2, r3, o_ref):
        del c_ref
        o_ref[...] = ((own_ref[...].astype(F32) + r1[...].astype(F32)) + r2[...].astype(F32)) + r3[...].astype(F32)

    grid_spec = pltpu.PrefetchScalarGridSpec(
        num_scalar_prefetch=1, grid=(R // rb,), in_specs=[own_spec, slot(1), slot(2), slot(3)],
        out_specs=pl.BlockSpec((rb, Cb), lambda i, c: (i, 0)))
    return _pcall(body, name=name, grid_spec=grid_spec, out_shape=SDS((R, Cb), F32))(
        _scalar(chip), grad, land, land, land)


def _adamw_step(g, w_ref, m_ref, v_ref, g_ref, d_ref, mo_ref, vo_ref):
    m_new = ADAM_B1 * m_ref[...] + (1.0 - ADAM_B1) * g
    v_new = ADAM_B2 * v_ref[...] + (1.0 - ADAM_B2) * (g * g)
    m_hat = m_new / (1.0 - ADAM_B1 ** ADAM_STEP)
    v_hat = v_new / (1.0 - ADAM_B2 ** ADAM_STEP)
    g_ref[...] = g
    d_ref[...] = -ADAM_LR * (m_hat / (jnp.sqrt(v_hat) + ADAM_EPS) + ADAM_WD * w_ref[...])
    mo_ref[...] = m_new
    vo_ref[...] = v_new


def _adamw(w, m, v, st, name):
    R, C = w.shape
    n = st.shape[0]
    rb = _rows_block(R, C)
    spec = pl.BlockSpec((rb, C), lambda i: (i, 0))

    def body(w_ref, m_ref, v_ref, s_ref, *o_refs):
        g = s_ref[0]
        for q in range(1, n):
            g = g + s_ref[q]
        _adamw_step(g, w_ref, m_ref, v_ref, *o_refs)

    return _pcall(body, name=name, grid=(R // rb,),
                  in_specs=[spec, spec, spec, pl.BlockSpec((n, rb, C), lambda i: (0, i, 0))],
                  out_specs=(spec,) * 4, out_shape=(SDS((R, C), F32),) * 4)(w, m, v, st)


def _adamw_pair(w, m, v, layer, own, recv2, ic, bufs, dep, name):
    L, R, C = w.shape
    rb = _rows_block(R, C)
    spec = pl.BlockSpec((None, rb, C), lambda i, c: (layer, i, 0))

    def body(c_ref, w_ref, m_ref, v_ref, own_ref, recv_ref, *rest):
        del c_ref
        _adamw_step(own_ref[...] + recv_ref[...], w_ref, m_ref, v_ref, *rest[-4:])

    nbuf = 0 if bufs is None else 4
    grid_spec = pltpu.PrefetchScalarGridSpec(
        num_scalar_prefetch=1, grid=(R // rb,),
        in_specs=[spec, spec, spec, pl.BlockSpec((rb, C), lambda i, c: (i, 0)),
                  pl.BlockSpec((None, rb, C), lambda i, c: (1 - c[0], i, 0))] + [_ANY] * (nbuf + 1),
        out_specs=(spec,) * 4)
    extra = {} if bufs is None else dict(input_output_aliases={6 + q: q for q in range(4)})
    return _pcall(body, name=name, grid_spec=grid_spec, out_shape=(SDS((L, R, C), F32),) * 4, **extra)(
        _scalar(ic), w, m, v, own, recv2, *([] if bufs is None else bufs), dep)


def _pack(arrs, rows=1):
    flat = jnp.concatenate([a.reshape(-1).astype(F32) for a in arrs])
    quantum = rows * LANES
    pad = (-flat.shape[0]) % quantum
    flat = jnp.pad(flat, (0, pad))
    return flat.reshape(rows, -1)


def _unpack(flat, shapes):
    flat = flat.reshape(-1)
    out, off = [], 0
    for shp in shapes:
        size = 1
        for d in shp:
            size *= d
        out.append(flat[off:off + size].reshape(shp))
        off += size
    return out


def _mlp_fwd(x1, mod, lnp, w1, w2, tag):
    h2 = _modulate(x1, mod, 3, 4, f"{tag}_mod")
    a1, a2 = _mm(h2, w1, "nn", name=f"{tag}_up", out_dtypes=(_MXU, _MXU),
                 out_fn=lambda r: (r, jnp.square(jnp.maximum(r, 0.0))))
    y2 = _mm(a2, w2, "nn", name=f"{tag}_down")
    x2 = _combine(x1, y2, mod, 5, lnp, 2, f"{tag}_ln")
    return x2, (x1, h2, a1, a2, y2)


def _weight_grad(grads, key, a, b, name):
    grads[key] = _mm(a, b, "tn", name=name, out_dtypes=(_MXU,))


def _mlp_bwd(dx2, saved, mod, lnp, w1, w2, tag, stacks):
    x1, h2, a1, a2, y2 = saved
    dxa, dy2, dp = _combine_bwd(x1, y2, mod, 5, lnp, 2, dx2, f"{tag}_ln_b")
    da1 = _mm(dy2, w2, "nt", name=f"{tag}_down_bx", out_dtypes=(_MXU,), aux=[(a1, "mn")],
              out_fn=lambda r, a: (r * (2.0 * jnp.maximum(a.astype(F32), 0.0)),))
    _weight_grad(stacks, "ff_w2", a2, dy2, f"{tag}_down_bw")
    _weight_grad(stacks, "ff_w1", h2, da1, f"{tag}_up_bw")
    dh2 = _mm(da1, w1, "nt", name=f"{tag}_up_bx")
    dx1, dss = _modulate_bwd(x1, mod, 4, dh2, dxa, f"{tag}_mod_b")
    return dx1, (dss, dp)


def _dn_fwd(x, mod, lnp, wts, H, tag):
    w_main, w_small, conv_w, prm, nw, w_out = wts
    h = _modulate(x, mod, 0, 1, f"{tag}_mod")
    pm = _mm(h, w_main, "nn", name=f"{tag}_in")
    ps = _mm(h, w_small, "nn", name=f"{tag}_in_s")
    nqkv = conv_w.shape[1] // LANES
    qkv = _conv_silu(pm, conv_w, nqkv, f"{tag}_conv")
    gates = _gates(ps, prm, H, f"{tag}_gates")
    og, *states = _delta_fwd(qkv, pm, gates, nw, H, f"{tag}_delta")
    y = _mm(og, w_out, "nn", name=f"{tag}_out")
    x1 = _combine(x, y, mod, 2, lnp, 0, f"{tag}_ln")
    return x1, (x, h, pm, ps, qkv, gates, states, og, y)


def _dn_bwd(dx1, saved, mod, lnp, wts, H, tag, stacks):
    w_main, w_small, conv_w, prm, nw, w_out = wts
    x, h, pm, ps, qkv, gates, states, og, y = saved
    dxa, dy, dp = _combine_bwd(x, y, mod, 2, lnp, 0, dx1, f"{tag}_ln_b")
    dog = _mm(dy, w_out, "nt", name=f"{tag}_out_bx")
    _weight_grad(stacks, "dn_w_out", og, dy, f"{tag}_out_bw")
    dq, dk, dv, dpm, dgates, dnw = _delta_bwd(qkv, pm, gates, nw, *states, dog, H, f"{tag}_delta_b")
    dps, dprm = _gates_bwd(ps, prm, dgates, H, f"{tag}_gates_b")
    dcw = []
    nb = dq.shape[1] // LANES
    for part, dpart in enumerate((dq, dk, dv)):
        dpm, dcw_p = _conv_silu_bwd(pm, conv_w, dpart, dpm, part * nb, f"{tag}_conv_b{part}")
        dcw.append(dcw_p)
    dconv_w = jnp.concatenate(dcw, axis=1)
    dw_main = _mm(h, dpm, "tn", name=f"{tag}_in_bw", out_dtypes=(_MXU,))
    dw_small = _mm(h, dps, "tn", name=f"{tag}_in_s_bw", out_dtypes=(_MXU,))
    dh_s = _mm(dps, w_small, "nt", name=f"{tag}_in_s_bx")
    dh = _mm(dpm, w_main, "nt", name=f"{tag}_in_bx", aux=[(dh_s, "mn")], out_fn=lambda r, e: (r + e,))
    dx, dss = _modulate_bwd(x, mod, 1, dh, dxa, f"{tag}_mod_b")
    return dx, (dw_main, dw_small, dconv_w, dprm, dnw), (dss, dp)


def _cf_fwd(x, mod, lnp, wts, tag):
    w_in, dw_w, dw_b, cln, w_out = wts
    h = _modulate(x, mod, 0, 1, f"{tag}_mod")
    p = _mm(h, w_in, "nn", name=f"{tag}_in")
    u2 = _glu_conv(p, dw_w, dw_b, f"{tag}_conv")
    u3 = _ln_silu(u2, cln, f"{tag}_cln")
    y = _mm(u3, w_out, "nn", name=f"{tag}_out")
    x1 = _combine(x, y, mod, 2, lnp, 0, f"{tag}_ln")
    return x1, (x, h, p, u2, u3, y)


def _cf_bwd(dx1, saved, mod, lnp, wts, tag, stacks):
    w_in, dw_w, dw_b, cln, w_out = wts
    x, h, p, u2, u3, y = saved
    dxa, dy, dp = _combine_bwd(x, y, mod, 2, lnp, 0, dx1, f"{tag}_ln_b")
    du3 = _mm(dy, w_out, "nt", name=f"{tag}_out_bx")
    _weight_grad(stacks, "cf_w_out", u3, dy, f"{tag}_out_bw")
    du2, dcln = _ln_silu_bwd(u2, cln, du3, f"{tag}_cln_b")
    dval, dgate, ddw_w, ddw_b = _glu_conv_bwd(p, dw_w, du2, f"{tag}_conv_b")
    dpp = jnp.concatenate([dval, dgate], axis=1)
    _weight_grad(stacks, "cf_w_in", h, dpp, f"{tag}_in_bw")
    dh = _mm(dpp, w_in, "nt", name=f"{tag}_in_bx")
    dx, dss = _modulate_bwd(x, mod, 1, dh, dxa, f"{tag}_mod_b")
    return dx, (ddw_w, ddw_b, dcln), (dss, dp)


def _two_d(a):
    return a.reshape(-1, a.shape[-1])


def kernel(x, c, ada_w, ada_b, ln_g, ln_b, dn_w_in, dn_conv_w, dn_a_log, dn_dt_bias, dn_norm_w, dn_w_out, cf_w_in, cf_dw_w, cf_dw_b, cf_ln_g, cf_ln_b, cf_w_out, ff_w1, ff_w2, loss_target, m_ada_w, m_ada_b, m_ln_g, m_ln_b, m_dn_w_in, m_dn_conv_w, m_dn_a_log, m_dn_dt_bias, m_dn_norm_w, m_dn_w_out, m_cf_w_in, m_cf_dw_w, m_cf_dw_b, m_cf_ln_g, m_cf_ln_b, m_cf_w_out, m_ff_w1, m_ff_w2, v_ada_w, v_ada_b, v_ln_g, v_ln_b, v_dn_w_in, v_dn_conv_w, v_dn_a_log, v_dn_dt_bias, v_dn_norm_w, v_dn_w_out, v_cf_w_in, v_cf_dw_w, v_cf_dw_b, v_cf_ln_g, v_cf_ln_b, v_cf_w_out, v_ff_w1, v_ff_w2):
    ix, iy, ic = lax.axis_index("x"), lax.axis_index("y"), lax.axis_index("c")
    chip = 2 * ix + iy
    dev = 4 * ix + 2 * iy + ic
    S, D = x.shape[1], x.shape[2]
    L = ada_w.shape[0]
    LA, LB = dn_w_in.shape[0], cf_w_in.shape[0]
    H = dn_a_log.shape[1]
    NMOD = ada_b.shape[1] // D
    dn_in = dn_w_in.shape[2] * 4
    n_main = dn_in - 2 * H
    assert L == N_LAYERS and 2 * H <= LANES
    x0, tgt = x[0], loss_target[0]

    small_sharded = [ln_g, ln_b, dn_conv_w, cf_dw_w, cf_dw_b, cf_ln_g, cf_ln_b]
    small_axes = [2, 2, 2, 2, 1, 1, 1]
    packed_small = _pack(small_sharded, rows=8)[None]
    big = {"dn_w_in": (dn_w_in, m_dn_w_in, v_dn_w_in, 0), "dn_w_out": (dn_w_out, m_dn_w_out, v_dn_w_out, 0),
           "cf_w_in": (cf_w_in, m_cf_w_in, v_cf_w_in, 1), "cf_w_out": (cf_w_out, m_cf_w_out, v_cf_w_out, 0),
           "ff_w1": (ff_w1, m_ff_w1, v_ff_w1, 1), "ff_w2": (ff_w2, m_ff_w2, v_ff_w2, 0)}

    def group(g):
        l = g // 2
        if g % 2:
            return {"ff_w1": l, "ff_w2": l}
        mixer = ("dn_w_in", "dn_w_out") if l % 2 == 0 else ("cf_w_in", "cf_w_out")
        return {mixer[0]: l // 2, mixer[1]: l // 2}

    def place(l, dep):
        return [_place_block(big[nm][0], lw, big[nm][3], chip, dep, f"l{l}_place_{nm}")
                for nm, lw in group(l).items()]

    def start_gather(l, placed, after):
        names = list(group(l))
        axes = [big[nm][3] for nm in names]
        send, recv, arrs, token = _gather_start(list(zip(placed, axes)), after, f"l{l}_gather_start")
        return names, axes, arrs, send, recv, token

    def finish_gather(l, pending, after):
        names, axes, arrs, send, recv, _ = pending
        arrs = _gather_wait(list(zip(arrs, axes)), send, recv, after, f"l{l}_gather_wait")
        arrs = _gather_forward(list(zip(arrs, axes)), f"l{l}_gather_pass")
        return dict(zip(names, arrs))

    g_small = _exchange([(packed_small, 0)], "xy", "gather", "gather_small")[0]
    shard_shapes = [a.shape for a in small_sharded]
    per_chip = [_unpack(g_small[q], shard_shapes) for q in range(4)]
    ln_g_f, ln_b_f, conv_w_f, dw_w_f, dw_b_f, cln_g_f, cln_b_f = [
        jnp.concatenate([per_chip[q][i] for q in range(4)], axis=small_axes[i]) for i in range(len(small_sharded))]

    c_all = _exchange([(c[None], 0)], "all", "gather", "gather_cond")[0].reshape(8, D)
    c_pad = jnp.pad(c_all, ((0, 8), (0, 0)))
    mod_sh = jnp.stack([_mm(c_pad, (ada_w, l), "nn", name=f"ada_{l}", a_fn=lambda t: t * _sigmoid(t))
                        for l in range(L)])
    mod_all = _exchange([(mod_sh, 2)], "xy", "gather", "gather_mod")[0]
    mod_mine = lax.dynamic_index_in_dim(mod_all, dev, axis=1, keepdims=False) + ada_b
    mods = mod_mine.reshape(L, NMOD, D)

    def lnp_of(l):
        return jnp.stack([ln_g_f[l, 0], ln_b_f[l, 0], ln_g_f[l, 1], ln_b_f[l, 1]])

    def mixer_wts(l, wl):
        j = l // 2
        if l % 2 == 0:
            w_in = jnp.transpose(wl["dn_w_in"].reshape(4, D, dn_in // 4), (1, 0, 2)).reshape(D, dn_in)
            w_small = jnp.pad(w_in[:, n_main:], ((0, 0), (0, LANES - 2 * H)))
            prm = jnp.zeros((2, LANES), F32).at[0, H:2 * H].set(dn_a_log[j]).at[1, H:2 * H].set(dn_dt_bias[j])
            return (w_in[:, :n_main], w_small, conv_w_f[j], prm, dn_norm_w[j][None], wl["dn_w_out"])
        return (wl["cf_w_in"], dw_w_f[j], dw_b_f[j][None], jnp.stack([cln_g_f[j], cln_b_f[j]]), wl["cf_w_out"])

    xs = x0
    saved, wts, mod_of = [], [], []
    pending = start_gather(0, place(0, mods), mods + jnp.minimum(jnp.abs(g_small[0, 0, 0]), 0.0))
    placed = [None] + [place(g, pending[5]) for g in range(1, 2 * L)]
    for g in range(2 * L):
        l = g // 2
        wl = finish_gather(g, pending, xs if g else placed[-1][-1])
        mod_g = mods[l]
        if g + 1 < 2 * L:
            pending = start_gather(g + 1, placed[g + 1], next(iter(wl.values())))
            mod_g = mod_g + pending[5][0, 0]
        mod_of.append(mod_g)
        if g % 2:
            wts.append((wl["ff_w1"], wl["ff_w2"]))
            xs, sv = _mlp_fwd(xs, mod_g, lnp_of(l), *wts[g], f"l{l}_ff")
        elif l % 2 == 0:
            wts.append(mixer_wts(l, wl))
            xs, sv = _dn_fwd(xs, mod_g, lnp_of(l), wts[g], H, f"l{l}_dn")
        else:
            wts.append(mixer_wts(l, wl))
            xs, sv = _cf_fwd(xs, mod_g, lnp_of(l), wts[g], f"l{l}_cf")
        saved.append(sv)
    dx, loss_local = _loss_head(xs, tgt, "loss_head")
    loss = lax.psum(loss_local[0, 0], ("x", "y", "c"))

    def start_scatter(l, grads):
        names = list(group(l))
        axes = [big[nm][3] for nm in names]
        send, recv, sums, lands, token = _scatter_start([(grads[nm], ax) for nm, ax in zip(names, axes)], mods,
                                                        f"l{l}_scatter_start")
        return names, axes, sums, lands, send, recv, token

    def finish_scatter(l, pending, after):
        names, axes, sums, lands, send, recv, _ = pending
        sums, lands = _scatter_wait(list(zip(sums, axes)), lands, send, recv, after, f"l{l}_scatter_wait")
        mine = [_sum_landed(s, ax, ld, chip, f"l{l}_sum_grads_{nm}") for nm, ax, s, ld in zip(names, axes, sums, lands)]
        other = _exchange([(s[None], 0) for s in mine], "c", "gather", f"l{l}_swap_sums", keep_own=False)
        return {nm: (s, o) for nm, s, o in zip(names, mine, other)}

    g_dn = [None] * LA
    g_cf = [None] * LB
    dmods, dlns = [None] * L, [None] * L
    big_sums = {}
    pending, token = None, None
    small_of = {}
    for g in reversed(range(2 * L)):
        l, j = g // 2, g // 4
        mod_g = mod_of[g] if token is None else mod_of[g] + token[0, 0]
        grads = {}
        if g % 2:
            dx, small_of[g] = _mlp_bwd(dx, saved[g], mod_g, lnp_of(l), *wts[g], f"l{l}_ff", grads)
        elif l % 2 == 0:
            dx, g_dn[j], small_of[g] = _dn_bwd(dx, saved[g], mod_g, lnp_of(l), wts[g], H, f"l{l}_dn", grads)
            dn_in_g = jnp.concatenate([g_dn[j][0], g_dn[j][1][:, :2 * H]], axis=1)
            grads["dn_w_in"] = jnp.transpose(dn_in_g.reshape(D, 4, dn_in // 4), (1, 0, 2)).reshape(4 * D, dn_in // 4)
        else:
            dx, g_cf[j], small_of[g] = _cf_bwd(dx, saved[g], mod_g, lnp_of(l), wts[g], f"l{l}_cf", grads)
        if pending is not None:
            for nm, pair_of in finish_scatter(g + 1, pending, dx).items():
                big_sums[nm, group(g + 1)[nm]] = pair_of
        pending = start_scatter(g, grads)
        token = pending[6]
    big_names = ["dn_w_in", "dn_w_out", "cf_w_in", "cf_w_out", "ff_w1", "ff_w2"]
    big_res = {nm: None for nm in big_names}

    def update(nm, layer):
        w, m, v, _ = big[nm]
        own, oth = big_sums[nm, layer]
        big_res[nm] = _adamw_pair(w, m, v, layer, own, oth, ic, big_res[nm], token, f"adamw_{nm}_{layer}")

    for nm in big_names:
        for layer in range(big[nm][0].shape[0]):
            if group(0).get(nm) != layer:
                update(nm, layer)
    for nm, pair_of in finish_scatter(0, pending, big_res["ff_w2"][0]).items():
        big_sums[nm, group(0)[nm]] = pair_of
    for nm, layer in group(0).items():
        update(nm, layer)
    big_out = [big_res[nm] for nm in big_names]
    for l in range(L):
        (dss1, dp1), (dss2, dp2) = small_of[2 * l], small_of[2 * l + 1]
        dmods[l] = jnp.concatenate([dss1, dp1[0:1], dss2, dp2[0:1]], axis=0)
        dlns[l] = (jnp.stack([dp1[1], dp2[1]]), jnp.stack([dp1[2], dp2[2]]))
    grad_x = dx[None]

    d_ln_g = jnp.stack([dlns[l][0] for l in range(L)])
    d_ln_b = jnp.stack([dlns[l][1] for l in range(L)])
    d_conv_w = jnp.stack([g_dn[j][2] for j in range(LA)])
    d_a_log = jnp.stack([g_dn[j][3][0, H:2 * H] for j in range(LA)])
    d_dt_bias = jnp.stack([g_dn[j][3][1, H:2 * H] for j in range(LA)])
    d_norm_w = jnp.stack([g_dn[j][4][0] for j in range(LA)])
    d_dw_w = jnp.stack([g_cf[j][0] for j in range(LB)])
    d_dw_b = jnp.stack([g_cf[j][1][0] for j in range(LB)])
    d_cln_g = jnp.stack([g_cf[j][2][0] for j in range(LB)])
    d_cln_b = jnp.stack([g_cf[j][2][1] for j in range(LB)])
    d_mod = jnp.stack(dmods).reshape(L, NMOD * D)
    small_full = [d_mod, d_ln_g, d_ln_b, d_conv_w, d_dw_w, d_dw_b, d_cln_g, d_cln_b, d_a_log, d_dt_bias, d_norm_w]
    small_all = _exchange([(_pack(small_full, rows=8)[None], 0)], "all", "gather", "gather_small_grads")[0]
    small_sum = _sum_slots(small_all, "sum_small_grads")
    (s_ada_b, s_ln_g, s_ln_b, s_conv_w, s_dw_w, s_dw_b, s_cln_g, s_cln_b, s_a_log, s_dt_bias, s_norm_w) = _unpack(
        small_sum, [a.shape for a in small_full])
    d_mod_all = small_all.reshape(8, -1)[:, :L * NMOD * D].reshape(8, L, NMOD * D)

    def shard(a, axis):
        size = a.shape[axis] // 4
        return lax.dynamic_slice_in_dim(a, chip * size, size, axis)

    ncol = ada_w.shape[2]
    d_mod_sh = jnp.pad(lax.dynamic_slice_in_dim(d_mod_all, chip * ncol, ncol, 2), ((0, 8), (0, 0), (0, 0)))
    g_ada_w = None
    for l in range(L):
        g_ada_w = _mm(c_pad, d_mod_sh[:, l], "tn", name=f"ada_bw_{l}", a_fn=lambda t: t * _sigmoid(t),
                      stack=(g_ada_w, l, L))

    ada_out =[r.reshape(ada_w.shape) for r in _adamw(_two_d(ada_w), _two_d(m_ada_w), _two_d(v_ada_w),
                                                     _two_d(g_ada_w)[None], "adamw_ada_w")]

    small_w = [(ada_b, m_ada_b, v_ada_b, s_ada_b), (ln_g, m_ln_g, v_ln_g, shard(s_ln_g, 2)),
               (ln_b, m_ln_b, v_ln_b, shard(s_ln_b, 2)), (dn_conv_w, m_dn_conv_w, v_dn_conv_w, shard(s_conv_w, 2)),
               (dn_a_log, m_dn_a_log, v_dn_a_log, s_a_log), (dn_dt_bias, m_dn_dt_bias, v_dn_dt_bias, s_dt_bias),
               (dn_norm_w, m_dn_norm_w, v_dn_norm_w, s_norm_w), (cf_dw_w, m_cf_dw_w, v_cf_dw_w, shard(s_dw_w, 2)),
               (cf_dw_b, m_cf_dw_b, v_cf_dw_b, shard(s_dw_b, 1)), (cf_ln_g, m_cf_ln_g, v_cf_ln_g, shard(s_cln_g, 1)),
               (cf_ln_b, m_cf_ln_b, v_cf_ln_b, shard(s_cln_b, 1))]
    pk = [_pack([t[i] for t in small_w], rows=8) for i in range(4)]
    small_res = _adamw(pk[0], pk[1], pk[2], pk[3][None], "adamw_small")
    small_shapes = [t[0].shape for t in small_w]
    small_out = [_unpack(r, small_shapes) for r in small_res]

    def kind(k):
        sm = small_out[k]
        bg = [o[k] for o in big_out]
        return [ada_out[k], sm[0], sm[1], sm[2], bg[0], sm[3], sm[4], sm[5], sm[6], bg[1],
                bg[2], sm[7], sm[8], sm[9], sm[10], bg[3], bg[4], bg[5]]

    return (loss, grad_x, *kind(0), *kind(1), *kind(2), *kind(3))
```

```python
import functools

import jax
import jax.numpy as jnp
from jax import lax
from jax.experimental import pallas as pl
from jax.experimental.pallas import tpu as pltpu

F32 = jnp.float32
_MXU = jnp.bfloat16
_HI = lax.Precision.HIGHEST

N_LAYERS = 4
ALPHA = (2.0 * N_LAYERS) ** 0.25
LN_EPS = 1e-5
RMS_EPS = 1e-6
L2_EPS = 1e-6
CHUNK = 64
ADAM_LR, ADAM_B1, ADAM_B2, ADAM_EPS, ADAM_WD, ADAM_STEP = 0.001, 0.9, 0.999, 1e-08, 0.01, 10

LANES = 128
TOKEN_BLOCK = 256
VMEM_BIG = 48 * 1024 * 1024
MM_TILE = 1024

SDS = jax.ShapeDtypeStruct
MESH = pl.DeviceIdType.MESH


def _cparams(vmem=None):
    if vmem is None:
        return None
    return pltpu.CompilerParams(vmem_limit_bytes=vmem)


def _pcall(body, **kw):
    if kw.get("compiler_params", 1) is None:
        kw.pop("compiler_params")
    return pl.pallas_call(body, **kw)


def _full(arr):
    nd = arr.ndim
    return pl.BlockSpec(arr.shape, lambda *g: (0,) * nd)


def _bs(block, imap, lead=None):
    if lead is None:
        return pl.BlockSpec(block, imap)
    return pl.BlockSpec((None,) + tuple(block), lambda *g: (lead,) + tuple(imap(*g)))


def _split(a):
    return a if isinstance(a, tuple) else (a, None)


_GROUPS = {
    "xy": ([(1, 0, 0), (0, 1, 0), (1, 1, 0)], 4),
    "c": ([(0, 0, 1)], 2),
    "all": ([(1, 0, 0), (0, 1, 0), (1, 1, 0), (0, 0, 1), (1, 0, 1), (0, 1, 1), (1, 1, 1)], 8),
}


def _exchange(items, group, mode, name, nsplit=1, keep_own=True):
    masks, n = _GROUPS[group]
    npeer = len(masks)
    ni = len(items)
    arrs = [a for a, _ in items]
    out_shapes = []
    for a, ax in items:
        shp = list(a.shape)
        if mode == "gather":
            shp[ax] *= n
        else:
            shp[ax] //= n
            shp = [n] + shp
        out_shapes.append(SDS(tuple(shp), a.dtype))

    def body(*refs):
        ins, outs = refs[:ni], refs[ni:2 * ni]
        send_sems, recv_sems, local_sems = refs[2 * ni:]
        x, y, c = lax.axis_index("x"), lax.axis_index("y"), lax.axis_index("c")

        def slot(px, py, pc):
            if group == "xy":
                return 2 * px + py
            if group == "c":
                return pc
            return 4 * px + 2 * py + pc

        me = slot(x, y, c)

        def block(ref, ax, idx, size):
            ix = (slice(None),) * ax + (pl.ds(pl.multiple_of(idx * size, size), size),)
            return ref.at[ix]

        copies = []
        for it, (a, ax) in enumerate(items):
            in_ref, out_ref = ins[it], outs[it]
            if mode == "gather":
                size = a.shape[ax]
                src_own, dst_own = in_ref, block(out_ref, ax, me, size)
            else:
                size = a.shape[ax] // n
                src_own, dst_own = block(in_ref, ax, me, size), out_ref.at[me]
            sax, ns, cs = splits[it]
            pieces = [(slice(None),) * sax + (pl.ds(j * cs, cs),) for j in range(ns)]
            if keep_own:
                for j, piece in enumerate(pieces):
                    own = pltpu.make_async_copy(src_own.at[piece], dst_own.at[piece], local_sems.at[it * nsplit + j])
                    own.start()
                    copies.append(own)
            for k, m in enumerate(masks):
                peer = tuple((1 - v) if b else v for v, b in zip((x, y, c), m))
                if mode == "gather":
                    src, dst = in_ref, dst_own
                else:
                    src, dst = block(in_ref, ax, slot(*peer), size), out_ref.at[me]
                for j, piece in enumerate(pieces):
                    sem = (it * npeer + k) * nsplit + j
                    cp = pltpu.make_async_remote_copy(
                        src_ref=src.at[piece], dst_ref=dst.at[piece], send_sem=send_sems.at[sem],
                        recv_sem=recv_sems.at[sem], device_id=peer, device_id_type=MESH)
                    cp.start()
                    copies.append(cp)
        for cp in copies:
            cp.wait()

    splits = []
    for (a, ax), o in zip(items, out_shapes):
        bshape = a.shape if mode == "gather" else o.shape[1:]
        sax = max(range(len(bshape) - 1), key=lambda d: bshape[d])
        ns = nsplit if bshape[sax] % (nsplit * 16) == 0 else 1
        splits.append((sax, ns, bshape[sax] // ns))
    any_spec = pl.BlockSpec(memory_space=pl.ANY)
    nsem = ni * npeer * nsplit
    outs = _pcall(
        body, name=name, out_shape=tuple(out_shapes),
        in_specs=[any_spec] * ni, out_specs=tuple([any_spec] * ni),
        scratch_shapes=[pltpu.SemaphoreType.DMA((nsem,)), pltpu.SemaphoreType.DMA((nsem,)),
                        pltpu.SemaphoreType.DMA((ni * nsplit,))],
    )(*arrs)
    return list(outs)


_HBM = pl.BlockSpec(memory_space=pltpu.HBM)
_SEM = pl.BlockSpec(memory_space=pltpu.SEMAPHORE)
_ANY = pl.BlockSpec(memory_space=pl.ANY)
_SPLIT = pltpu.CompilerParams(has_side_effects=pltpu.SideEffectType.DATAFLOW_SIDE_EFFECTING)
_XY = _GROUPS["xy"][0]


def _in_hbm(a):
    return pltpu.with_memory_space_constraint(a, pltpu.HBM)


def _chip_peers():
    x, y, c = lax.axis_index("x"), lax.axis_index("y"), lax.axis_index("c")
    return x, y, c, [tuple((1 - v) if b else v for v, b in zip((x, y), m)) for m in _XY]


def _wblock(ref, ax, half, chip):
    R, C = ref.shape
    if ax == 0:
        rows = R // 8
        return ref.at[pl.ds(pl.multiple_of(chip * (2 * rows) + half * rows, rows), rows), :]
    rows, cols = R // 2, C // 4
    return ref.at[pl.ds(pl.multiple_of(half * rows, rows), rows), pl.ds(pl.multiple_of(chip * cols, cols), cols)]


def _gather_start(items, after, name):
    ni = len(items)
    arrs = [a for a, _ in items]

    def body(*refs):
        send_sems, recv_sems = refs[ni + 1], refs[ni + 2]
        outs, token = refs[ni + 3:2 * ni + 3], refs[2 * ni + 3]
        x, y, c, peers = _chip_peers()
        for it, (_, ax) in enumerate(items):
            mine = _wblock(outs[it], ax, c, 2 * x + y)
            for k, (px, py) in enumerate(peers):
                pltpu.make_async_remote_copy(
                    src_ref=mine, dst_ref=mine, send_sem=send_sems.at[it * 3 + k], recv_sem=recv_sems.at[it * 3 + k],
                    device_id=(px, py, c), device_id_type=MESH).start()
        token[...] = jnp.zeros_like(token)

    res = _pcall(
        body, name=name,
        out_shape=(pltpu.SemaphoreType.DMA((ni * 3,)), pltpu.SemaphoreType.DMA((ni * 3,)),
                   *[pltpu.HBM(a.shape, a.dtype) for a in arrs], SDS((8, LANES), F32)),
        in_specs=[_HBM] * ni + [_ANY],
        out_specs=(_SEM, _SEM, *[_HBM] * ni, pl.BlockSpec(memory_space=pltpu.VMEM)),
        input_output_aliases={i: 2 + i for i in range(ni)}, compiler_params=_SPLIT,
    )(*[_in_hbm(a) for a in arrs], after)
    return res[0], res[1], list(res[2:2 + ni]), res[2 + ni]


def _gather_wait(items, send_sems, recv_sems, after, name):
    ni = len(items)
    arrs = [a for a, _ in items]

    def body(*refs):
        s_sems, r_sems = refs[ni], refs[ni + 1]
        outs = refs[ni + 3:]
        x, y, c, peers = _chip_peers()
        for it, (_, ax) in enumerate(items):
            mine = _wblock(outs[it], ax, c, 2 * x + y)
            for k, (px, py) in enumerate(peers):
                cp = pltpu.make_async_remote_copy(
                    src_ref=mine, dst_ref=_wblock(outs[it], ax, c, 2 * px + py), send_sem=s_sems.at[it * 3 + k],
                    recv_sem=r_sems.at[it * 3 + k], device_id=(px, py, c), device_id_type=MESH)
                cp.wait_send()
                cp.wait_recv()

    res = _pcall(
        body, name=name, out_shape=tuple(pltpu.HBM(a.shape, a.dtype) for a in arrs),
        in_specs=[_HBM] * ni + [_SEM, _SEM, _ANY], out_specs=tuple([_HBM] * ni),
        input_output_aliases={i: i for i in range(ni)}, compiler_params=_SPLIT,
    )(*arrs, send_sems, recv_sems, after)
    return list(res)


def _gather_forward(items, name):
    ni = len(items)
    arrs = [a for a, _ in items]

    def body(*refs):
        outs = refs[ni:2 * ni]
        send_sems, recv_sems = refs[2 * ni:]
        x, y, c, peers = _chip_peers()
        def copy(it, ax, k, chip, dst_half):
            return pltpu.make_async_remote_copy(
                src_ref=_wblock(outs[it], ax, c, chip), dst_ref=_wblock(outs[it], ax, dst_half, chip),
                send_sem=send_sems.at[it * 3 + k], recv_sem=recv_sems.at[it * 3 + k],
                device_id=(x, y, 1 - c), device_id_type=MESH)

        for it, (_, ax) in enumerate(items):
            for k, (px, py) in enumerate(peers):
                copy(it, ax, k, 2 * px + py, c).start()
        for it, (_, ax) in enumerate(items):
            for k, (px, py) in enumerate(peers):
                copy(it, ax, k, 2 * px + py, c).wait_send()
                copy(it, ax, k, 2 * px + py, 1 - c).wait_recv()

    res = _pcall(
        body, name=name, out_shape=tuple(SDS(a.shape, a.dtype) for a in arrs),
        in_specs=[_ANY] * ni, out_specs=tuple([_ANY] * ni), input_output_aliases={i: i for i in range(ni)},
        scratch_shapes=[pltpu.SemaphoreType.DMA((ni * 3,)), pltpu.SemaphoreType.DMA((ni * 3,))],
    )(*arrs)
    return list(res)


def _gblock(ref, ax, chip):
    size = ref.shape[ax] // 4
    piece = pl.ds(pl.multiple_of(chip * size, size), size)
    return ref.at[piece, :] if ax == 0 else ref.at[:, piece]


def _scatter_start(items, after, name):
    ni = len(items)
    arrs = [a for a, _ in items]
    lands = []
    for a, ax in items:
        blk = (a.shape[0] // 4, a.shape[1]) if ax == 0 else (a.shape[0], a.shape[1] // 4)
        lands.append(lax.empty((4, *blk), a.dtype))

    def body(*refs):
        send_sems, recv_sems = refs[2 * ni + 1], refs[2 * ni + 2]
        srcs, dsts = refs[2 * ni + 3:3 * ni + 3], refs[3 * ni + 3:4 * ni + 3]
        token = refs[4 * ni + 3]
        x, y, c, peers = _chip_peers()
        for it, (_, ax) in enumerate(items):
            for k, (px, py) in enumerate(peers):
                pltpu.make_async_remote_copy(
                    src_ref=_gblock(srcs[it], ax, 2 * px + py), dst_ref=dsts[it].at[2 * x + y],
                    send_sem=send_sems.at[it * 3 + k], recv_sem=recv_sems.at[it * 3 + k],
                    device_id=(px, py, c), device_id_type=MESH).start()
        token[...] = jnp.zeros_like(token)

    res = _pcall(
        body, name=name,
        out_shape=(pltpu.SemaphoreType.DMA((ni * 3,)), pltpu.SemaphoreType.DMA((ni * 3,)),
                   *[pltpu.HBM(a.shape, a.dtype) for a in arrs], *[pltpu.HBM(a.shape, a.dtype) for a in lands],
                   SDS((8, LANES), F32)),
        in_specs=[_HBM] * (2 * ni) + [_ANY],
        out_specs=(_SEM, _SEM, *[_HBM] * (2 * ni), pl.BlockSpec(memory_space=pltpu.VMEM)),
        input_output_aliases={i: 2 + i for i in range(2 * ni)}, compiler_params=_SPLIT,
    )(*[_in_hbm(a) for a in arrs], *[_in_hbm(a) for a in lands], after)
    return res[0], res[1], list(res[2:2 + ni]), list(res[2 + ni:2 + 2 * ni]), res[2 + 2 * ni]


def _scatter_wait(items, lands, send_sems, recv_sems, after, name):
    ni = len(items)
    arrs = [a for a, _ in items]

    def body(*refs):
        s_sems, r_sems = refs[2 * ni], refs[2 * ni + 1]
        srcs, dsts = refs[2 * ni + 3:3 * ni + 3], refs[3 * ni + 3:]
        x, y, c, peers = _chip_peers()
        for it, (_, ax) in enumerate(items):
            for k, (px, py) in enumerate(peers):
                cp = pltpu.make_async_remote_copy(
                    src_ref=_gblock(srcs[it], ax, 2 * px + py), dst_ref=dsts[it].at[2 * px + py],
                    send_sem=s_sems.at[it * 3 + k], recv_sem=r_sems.at[it * 3 + k],
                    device_id=(px, py, c), device_id_type=MESH)
                cp.wait_send()
                cp.wait_recv()

    res = _pcall(
        body, name=name, out_shape=tuple(pltpu.HBM(a.shape, a.dtype) for a in arrs + list(lands)),
        in_specs=[_HBM] * (2 * ni) + [_SEM, _SEM, _ANY], out_specs=tuple([_HBM] * (2 * ni)),
        input_output_aliases={i: i for i in range(2 * ni)}, compiler_params=_SPLIT,
    )(*arrs, *lands, send_sems, recv_sems, after)
    return list(res[:ni]), list(res[ni:])


def _tile(n, cap):
    if n <= cap:
        return n
    t = cap - cap % LANES
    while n % t:
        t -= LANES
    return t


def _mm(a, b, mode, *, name, out_dtypes=(F32,), tm=MM_TILE, tn=MM_TILE, tk=MM_TILE, a_fn=None, out_fn=None, aux=(),
        stack=None):
    (a, a_lead), (b, b_lead) = _split(a), _split(b)
    ash, bsh = a.shape[-2:], b.shape[-2:]
    if mode == "nn":
        (M, K), (_, N) = ash, bsh
    elif mode == "nt":
        (M, K), (N, _) = ash, bsh
    else:
        (K, M), (_, N) = ash, bsh
    tm, tn, tk = _tile(M, tm), _tile(N, tn), _tile(K, tk)
    nk = K // tk
    if mode == "tn":
        a_spec = _bs((tk, tm), lambda i, j, k: (k, i), a_lead)
    else:
        a_spec = _bs((tm, tk), lambda i, j, k: (i, k), a_lead)
    if mode == "nt":
        b_spec = _bs((tn, tk), lambda i, j, k: (j, k), b_lead)
    else:
        b_spec = _bs((tk, tn), lambda i, j, k: (k, j), b_lead)
    aux_arrs, aux_specs = [], []
    for arr, kind in aux:
        arr, lead = _split(arr)
        aux_arrs.append(arr)
        if kind == "mn":
            aux_specs.append(_bs((tm, tn), lambda i, j, k: (i, j), lead))
        else:
            aux_specs.append(_bs((1, tn), lambda i, j, k: (0, j), lead))
    na, no = len(aux_arrs), len(out_dtypes)
    dims = {"nn": (((1,), (0,)), ((), ())), "nt": (((1,), (1,)), ((), ())), "tn": (((0,), (0,)), ((), ()))}[mode]

    def finish(r, aux_refs, o_refs):
        outs = out_fn(r, *[x[...] for x in aux_refs]) if out_fn is not None else (r,)
        for o_ref, val in zip(o_refs, outs):
            o_ref[...] = val.astype(o_ref.dtype)

    def product(a_ref, b_ref):
        av = a_ref[...]
        if a_fn is not None:
            av = a_fn(av.astype(F32))
        return lax.dot_general(av.astype(_MXU), b_ref[...].astype(_MXU), dims, preferred_element_type=F32)

    nbuf = 0 if stack is None or stack[0] is None else 1

    def body_one(a_ref, b_ref, *rest):
        finish(product(a_ref, b_ref), rest[:na], rest[na + nbuf:na + nbuf + no])

    def body_acc(a_ref, b_ref, *rest):
        aux_refs, o_refs, acc = rest[:na], rest[na + nbuf:na + nbuf + no], rest[na + nbuf + no]
        k = pl.program_id(2)

        @pl.when(k == 0)
        def _():
            acc[...] = product(a_ref, b_ref)

        @pl.when(k != 0)
        def _():
            acc[...] += product(a_ref, b_ref)

        @pl.when(k == nk - 1)
        def _():
            finish(acc[...], aux_refs, o_refs)

    extra, aliases = {}, []
    if stack is None:
        out_shape = (M, N)
        o_spec = pl.BlockSpec((tm, tn), lambda i, j, k: (i, j))
    else:
        buf, layer, n_layers = stack
        assert no == 1
        out_shape = (n_layers, M, N)
        o_spec = pl.BlockSpec((None, tm, tn), lambda i, j, k: (layer, i, j))
        if buf is not None:
            aliases = [buf]
            extra = dict(input_output_aliases={2 + na: 0})
    outs = _pcall(
        body_one if nk == 1 else body_acc, name=name, grid=(M // tm, N // tn, nk),
        in_specs=[a_spec, b_spec] + aux_specs + [pl.BlockSpec(memory_space=pl.ANY)] * nbuf,
        out_specs=tuple([o_spec] * no),
        out_shape=tuple(SDS(out_shape, dt) for dt in out_dtypes),
        scratch_shapes=[] if nk == 1 else [pltpu.VMEM((tm, tn), F32)],
        compiler_params=pltpu.CompilerParams(dimension_semantics=("parallel", "parallel", "arbitrary"),
                                             vmem_limit_bytes=VMEM_BIG),
        **extra,
    )(a, b, *aux_arrs, *aliases)
    return outs[0] if no == 1 else outs


def _tok(S):
    ts = min(TOKEN_BLOCK, S)
    assert S % ts == 0
    return ts


def _row(ts, D):
    return pl.BlockSpec((ts, D), lambda i: (i, 0))


def _acc_rows(ref, i, rows):
    @pl.when(i == 0)
    def _():
        for r, v in enumerate(rows):
            ref[r:r + 1, :] = v

    @pl.when(i != 0)
    def _():
        for r, v in enumerate(rows):
            ref[r:r + 1, :] += v


def _modulate(x, mod, r_sh, r_sc, name):
    S, D = x.shape
    ts = _tok(S)

    def body(x_ref, m_ref, o_ref):
        o_ref[...] = (x_ref[...] * (1.0 + m_ref[r_sc:r_sc + 1, :]) + m_ref[r_sh:r_sh + 1, :]).astype(o_ref.dtype)

    return _pcall(body, name=name, grid=(S // ts,), in_specs=[_row(ts, D), _full(mod)],
                  out_specs=_row(ts, D), out_shape=SDS((S, D), _MXU))(x, mod)


def _modulate_bwd(x, mod, r_sc, dh, dxa, name):
    S, D = x.shape
    ts = _tok(S)

    def body(x_ref, m_ref, dh_ref, dxa_ref, dx_ref, dss_ref):
        dh_v = dh_ref[...]
        dx_ref[...] = dxa_ref[...] + dh_v * (1.0 + m_ref[r_sc:r_sc + 1, :])
        _acc_rows(dss_ref, pl.program_id(0),
                  [jnp.sum(dh_v, axis=0, keepdims=True), jnp.sum(dh_v * x_ref[...], axis=0, keepdims=True)])

    return _pcall(body, name=name, grid=(S // ts,),
                  in_specs=[_row(ts, D), _full(mod), _row(ts, D), _row(ts, D)],
                  out_specs=(_row(ts, D), pl.BlockSpec((2, D), lambda i: (0, 0))),
                  out_shape=(SDS((S, D), F32), SDS((2, D), F32)))(x, mod, dh, dxa)


def _norm_stats(z):
    mu = jnp.mean(z, axis=-1, keepdims=True)
    zc = z - mu
    var = jnp.mean(zc * zc, axis=-1, keepdims=True)
    rstd = lax.rsqrt(var + LN_EPS)
    return zc * rstd, rstd


def _norm_bwd(dxhat, xhat, rstd):
    return rstd * (dxhat - jnp.mean(dxhat, axis=-1, keepdims=True)
                   - xhat * jnp.mean(dxhat * xhat, axis=-1, keepdims=True))


def _combine(x, y, mod, r_gt, lnp, r_g, name):
    S, D = x.shape
    ts = _tok(S)

    def body(x_ref, y_ref, m_ref, l_ref, o_ref):
        z = ALPHA * x_ref[...] + (1.0 + m_ref[r_gt:r_gt + 1, :]) * y_ref[...]
        xhat, _ = _norm_stats(z)
        o_ref[...] = xhat * l_ref[r_g:r_g + 1, :] + l_ref[r_g + 1:r_g + 2, :]

    return _pcall(body, name=name, grid=(S // ts,), in_specs=[_row(ts, D), _row(ts, D), _full(mod), _full(lnp)],
                  out_specs=_row(ts, D), out_shape=SDS((S, D), F32))(x, y, mod, lnp)


def _combine_bwd(x, y, mod, r_gt, lnp, r_g, dout, name):
    S, D = x.shape
    ts = _tok(S)

    def body(x_ref, y_ref, m_ref, l_ref, do_ref, dxa_ref, dy_ref, dp_ref):
        gate = 1.0 + m_ref[r_gt:r_gt + 1, :]
        y_v, do_v = y_ref[...], do_ref[...]
        xhat, rstd = _norm_stats(ALPHA * x_ref[...] + gate * y_v)
        dz = _norm_bwd(do_v * l_ref[r_g:r_g + 1, :], xhat, rstd)
        dxa_ref[...] = ALPHA * dz
        dy_ref[...] = (gate * dz).astype(dy_ref.dtype)
        _acc_rows(dp_ref, pl.program_id(0),
                  [jnp.sum(dz * y_v, axis=0, keepdims=True), jnp.sum(do_v * xhat, axis=0, keepdims=True),
                   jnp.sum(do_v, axis=0, keepdims=True)])

    return _pcall(body, name=name, grid=(S // ts,),
                  in_specs=[_row(ts, D), _row(ts, D), _full(mod), _full(lnp), _row(ts, D)],
                  out_specs=(_row(ts, D), _row(ts, D), pl.BlockSpec((3, D), lambda i: (0, 0))),
                  out_shape=(SDS((S, D), F32), SDS((S, D), _MXU), SDS((3, D), F32)))(x, y, mod, lnp, dout)


def _modcomb_bwd(later, x, y, mod, r_gt, lnp, r_g, name):
    xn, mod_n, r_sc_n, dh, dxa_n = later
    S, D = x.shape
    ts = _tok(S)

    def body(xn_ref, mn_ref, dh_ref, dxan_ref, x_ref, y_ref, m_ref, l_ref, dxa_ref, dy_ref, dp_ref, dss_ref):
        i = pl.program_id(0)
        dh_v = dh_ref[...]
        do_v = dxan_ref[...] + dh_v * (1.0 + mn_ref[r_sc_n:r_sc_n + 1, :])
        _acc_rows(dss_ref, i, [jnp.sum(dh_v, axis=0, keepdims=True),
                               jnp.sum(dh_v * xn_ref[...], axis=0, keepdims=True)])
        gate = 1.0 + m_ref[r_gt:r_gt + 1, :]
        y_v = y_ref[...]
        xhat, rstd = _norm_stats(ALPHA * x_ref[...] + gate * y_v)
        dz = _norm_bwd(do_v * l_ref[r_g:r_g + 1, :], xhat, rstd)
        dxa_ref[...] = ALPHA * dz
        dy_ref[...] = (gate * dz).astype(dy_ref.dtype)
        _acc_rows(dp_ref, i, [jnp.sum(dz * y_v, axis=0, keepdims=True), jnp.sum(do_v * xhat, axis=0, keepdims=True),
                              jnp.sum(do_v, axis=0, keepdims=True)])

    row = _row(ts, D)
    return _pcall(body, name=name, grid=(S // ts,),
                  in_specs=[row, _full(mod_n), row, row, row, row, _full(mod), _full(lnp)],
                  out_specs=(row, row, pl.BlockSpec((3, D), lambda i: (0, 0)), pl.BlockSpec((2, D), lambda i: (0, 0))),
                  out_shape=(SDS((S, D), F32), SDS((S, D), _MXU), SDS((3, D), F32), SDS((2, D), F32)))(
        xn, mod_n, dh, dxa_n, x, y, mod, lnp)


def _sublayer_head_bwd(dout, x, y, mod, r_gt, lnp, r_g, tag):
    if isinstance(dout, tuple):
        return _modcomb_bwd(dout, x, y, mod, r_gt, lnp, r_g, f"{tag}_ln_b")
    return (*_combine_bwd(x, y, mod, r_gt, lnp, r_g, dout, f"{tag}_ln_b"), None)


def _sigmoid(t):
    return 1.0 / (1.0 + jnp.exp(-t))


def _ln_silu(u, lnp, name):
    S, D = u.shape
    ts = _tok(S)

    def body(u_ref, l_ref, o_ref):
        xhat, _ = _norm_stats(u_ref[...])
        t = xhat * l_ref[0:1, :] + l_ref[1:2, :]
        o_ref[...] = (t * _sigmoid(t)).astype(o_ref.dtype)

    return _pcall(body, name=name, grid=(S // ts,), in_specs=[_row(ts, D), _full(lnp)],
                  out_specs=_row(ts, D), out_shape=SDS((S, D), _MXU))(u, lnp)


def _ln_silu_bwd(u, lnp, dout, name):
    S, D = u.shape
    ts = _tok(S)

    def body(u_ref, l_ref, do_ref, du_ref, dp_ref):
        xhat, rstd = _norm_stats(u_ref[...])
        g = l_ref[0:1, :]
        t = xhat * g + l_ref[1:2, :]
        sg = _sigmoid(t)
        dt = do_ref[...] * (sg * (1.0 + t * (1.0 - sg)))
        du_ref[...] = _norm_bwd(dt * g, xhat, rstd)
        _acc_rows(dp_ref, pl.program_id(0),
                  [jnp.sum(dt * xhat, axis=0, keepdims=True), jnp.sum(dt, axis=0, keepdims=True)])

    return _pcall(body, name=name, grid=(S // ts,), in_specs=[_row(ts, D), _full(lnp), _row(ts, D)],
                  out_specs=(_row(ts, D), pl.BlockSpec((2, D), lambda i: (0, 0))),
                  out_shape=(SDS((S, D), F32), SDS((2, D), F32)))(u, lnp, dout)


def _loss_head(xf, tgt, name):
    S, D = xf.shape
    ts = _tok(S)

    def body(x_ref, t_ref, dx_ref, l_ref):
        err = x_ref[...] - t_ref[...]
        dx_ref[...] = err * (1.0 / D)
        part = jnp.sum(jnp.sum(err * err, axis=1, keepdims=True), axis=0, keepdims=True) * (0.5 / D)

        @pl.when(pl.program_id(0) == 0)
        def _():
            l_ref[...] = part

        @pl.when(pl.program_id(0) != 0)
        def _():
            l_ref[...] += part

    return _pcall(body, name=name, grid=(S // ts,), in_specs=[_row(ts, D), _row(ts, D)],
                  out_specs=(_row(ts, D), pl.BlockSpec((1, 1), lambda i: (0, 0))),
                  out_shape=(SDS((S, D), F32), SDS((1, 1), F32)))(xf, tgt)


CONV_PAD = 32


def _padded(u, before):
    zeros = jnp.zeros((CONV_PAD, u.shape[1]), u.dtype)
    return jnp.concatenate([zeros, u] if before else [u, zeros], axis=0)


SUBLANES = 8


def _shift_down(u_pad, s, rolled):
    q, r = divmod(s, SUBLANES)
    if r not in rolled:
        rolled[r] = u_pad if r == 0 else pltpu.roll(u_pad, r, 0)
    start = CONV_PAD - SUBLANES * q
    return rolled[r][start:start + u_pad.shape[0] - CONV_PAD]


def _shift_up(u_pad, s, rolled):
    n = u_pad.shape[0]
    q, r = divmod(s, SUBLANES)
    if r not in rolled:
        rolled[r] = u_pad if r == 0 else pltpu.roll(u_pad, n - r, 0)
    return rolled[r][SUBLANES * q:SUBLANES * q + n - CONV_PAD]


def _dwconv(u, w_ref, taps, rows):
    del rows
    assert taps - 1 <= CONV_PAD
    u_pad, rolled = _padded(u, True), {}
    acc = jnp.zeros_like(u)
    for j in range(taps):
        acc = acc + w_ref[j:j + 1, :] * _shift_down(u_pad, taps - 1 - j, rolled)
    return acc


def _dwconv_bwd(u, dy, w_ref, dw_ref, taps, rows):
    del rows
    assert taps - 1 <= CONV_PAD
    u_pad, dy_pad, u_rolled, dy_rolled = _padded(u, True), _padded(dy, False), {}, {}
    du = jnp.zeros_like(u)
    for j in range(taps):
        s = taps - 1 - j
        du = du + w_ref[j:j + 1, :] * _shift_up(dy_pad, s, dy_rolled)
        dw_ref[j:j + 1, :] = jnp.sum(dy * _shift_down(u_pad, s, u_rolled), axis=0, keepdims=True)
    return du


def _col(S, j0=0):
    return pl.BlockSpec((S, LANES), lambda j: (0, j + j0))


def _conv_silu(pm, w, nblk, name):
    S = pm.shape[0]
    taps = w.shape[0]

    def body(u_ref, w_ref, o_ref):
        rows = lax.broadcasted_iota(jnp.int32, (S, LANES), 0)
        cv = _dwconv(u_ref[...], w_ref, taps, rows)
        o_ref[...] = cv * _sigmoid(cv)

    return _pcall(body, name=name, grid=(nblk,),
                  in_specs=[_col(S), pl.BlockSpec((taps, LANES), lambda j: (0, j))],
                  out_specs=_col(S), out_shape=SDS((S, nblk * LANES), F32),
                  compiler_params=_cparams(VMEM_BIG))(pm, w)


def _conv_silu_bwd(pm, w, dout, dpm, j0, name):
    S = pm.shape[0]
    taps = w.shape[0]
    nblk = dout.shape[1] // LANES

    def body(u_ref, w_ref, do_ref, dpm_in, du_ref, dw_ref):
        del dpm_in
        rows = lax.broadcasted_iota(jnp.int32, (S, LANES), 0)
        u = u_ref[...]
        cv = _dwconv(u, w_ref, taps, rows)
        sg = _sigmoid(cv)
        dc = do_ref[...] * (sg * (1.0 + cv * (1.0 - sg)))
        du_ref[...] = _dwconv_bwd(u, dc, w_ref, dw_ref, taps, rows)

    return _pcall(body, name=name, grid=(nblk,),
                  in_specs=[_col(S, j0), pl.BlockSpec((taps, LANES), lambda j: (0, j + j0)), _col(S),
                            pl.BlockSpec(memory_space=pl.ANY)],
                  out_specs=(_col(S, j0), pl.BlockSpec((taps, LANES), lambda j: (0, j))),
                  out_shape=(SDS(dpm.shape, F32), SDS((taps, nblk * LANES), F32)),
                  input_output_aliases={3: 0},
                  compiler_params=_cparams(VMEM_BIG))(pm, w, dout, dpm)


def _glu_conv(p, w, bias, name):
    S, C2 = p.shape
    nblk = C2 // 2 // LANES
    taps = w.shape[0]

    def body(v_ref, g_ref, w_ref, b_ref, o_ref):
        rows = lax.broadcasted_iota(jnp.int32, (S, LANES), 0)
        u = v_ref[...] * _sigmoid(g_ref[...])
        o_ref[...] = _dwconv(u, w_ref, taps, rows) + b_ref[...]

    return _pcall(body, name=name, grid=(nblk,),
                  in_specs=[_col(S), _col(S, nblk), pl.BlockSpec((taps, LANES), lambda j: (0, j)),
                            pl.BlockSpec((1, LANES), lambda j: (0, j))],
                  out_specs=_col(S), out_shape=SDS((S, nblk * LANES), F32),
                  compiler_params=_cparams(VMEM_BIG))(p, p, w, bias)


def _glu_conv_bwd(p, w, dout, name):
    S, C2 = p.shape
    nblk = C2 // 2 // LANES
    taps = w.shape[0]

    def body(v_ref, g_ref, w_ref, do_ref, dv_ref, dg_ref, dw_ref, db_ref):
        rows = lax.broadcasted_iota(jnp.int32, (S, LANES), 0)
        val, sg = v_ref[...], _sigmoid(g_ref[...])
        do_v = do_ref[...]
        du = _dwconv_bwd(val * sg, do_v, w_ref, dw_ref, taps, rows)
        dv_ref[...] = du * sg
        dg_ref[...] = du * val * sg * (1.0 - sg)
        db_ref[...] = jnp.sum(do_v, axis=0, keepdims=True)

    dval, dgate, dw, db = _pcall(
        body, name=name, grid=(nblk,),
        in_specs=[_col(S), _col(S, nblk), pl.BlockSpec((taps, LANES), lambda j: (0, j)), _col(S)],
        out_specs=(_col(S), _col(S), pl.BlockSpec((taps, LANES), lambda j: (0, j)),
                   pl.BlockSpec((1, LANES), lambda j: (0, j))),
        out_shape=(SDS((S, C2 // 2), F32), SDS((S, C2 // 2), F32), SDS((taps, C2 // 2), F32), SDS((1, C2 // 2), F32)),
        compiler_params=_cparams(VMEM_BIG))(p, p, w, dout)
    return dval, dgate, dw, db


def _log1p(e):
    u = 1.0 + e
    d = jnp.where(u == 1.0, 1.0, u - 1.0)
    return jnp.where(u == 1.0, e, jnp.log(u) * (e / d))


def _gate_parts(ps, prm, H):
    lane = lax.broadcasted_iota(jnp.int32, ps.shape, 1)
    is_b, is_g = lane < H, (lane >= H) & (lane < 2 * H)
    beta = _sigmoid(ps)
    t = ps + prm[1:2, :]
    sp = jnp.maximum(t, 0.0) + _log1p(jnp.exp(-jnp.abs(t)))
    na = -jnp.exp(prm[0:1, :])
    return is_b, is_g, beta, t, sp, na


def _gates(ps, prm, H, name):
    S = ps.shape[0]
    ts = _tok(S)

    def body(p_ref, r_ref, o_ref):
        is_b, is_g, beta, _, sp, na = _gate_parts(p_ref[...], r_ref[...], H)
        o_ref[...] = jnp.where(is_b, beta, jnp.where(is_g, na * sp, 0.0))

    return _pcall(body, name=name, grid=(S // ts,), in_specs=[_row(ts, LANES), _full(prm)],
                  out_specs=_row(ts, LANES), out_shape=SDS((S, LANES), F32))(ps, prm)


def _gates_bwd(ps, prm, dgates, H, name):
    S = ps.shape[0]
    ts = _tok(S)

    def body(p_ref, r_ref, dg_ref, dp_ref, dr_ref):
        is_b, is_g, beta, t, sp, na = _gate_parts(p_ref[...], r_ref[...], H)
        dg_v = dg_ref[...]
        dsp = jnp.where(is_g, dg_v * na * _sigmoid(t), 0.0)
        dp_ref[...] = jnp.where(is_b, dg_v * beta * (1.0 - beta), dsp)
        _acc_rows(dr_ref, pl.program_id(0),
                  [jnp.sum(jnp.where(is_g, dg_v * na * sp, 0.0), axis=0, keepdims=True),
                   jnp.sum(dsp, axis=0, keepdims=True)])

    return _pcall(body, name=name, grid=(S // ts,), in_specs=[_row(ts, LANES), _full(prm), _row(ts, LANES)],
                  out_specs=(_row(ts, LANES), pl.BlockSpec((2, LANES), lambda i: (0, 0))),
                  out_shape=(SDS((S, LANES), F32), SDS((2, LANES), F32)))(ps, prm, dgates)


_NN = (((2,), (1,)), ((0,), (0,)))
_NT = (((2,), (2,)), ((0,), (0,)))
_TN = (((1,), (1,)), ((0,), (0,)))


def _mdot(a, b, dims):
    return lax.dot_general(a.astype(_MXU), b.astype(_MXU), dims, preferred_element_type=F32)


def _mdot3(a, b, dims):
    ah, bh = a.astype(_MXU), b.astype(_MXU)
    al, bl = a - ah.astype(F32), b - bh.astype(F32)
    return _mdot(ah, bh, dims) + (_mdot(ah, bl, dims) + _mdot(al, bh, dims))


def _rounded_dot(dims, da_dims, db_dims, a_first, prod=_mdot):
    @jax.custom_vjp
    def f(a, b):
        return prod(a, b, dims)

    def fwd(a, b):
        return prod(a, b, dims), (a, b)

    def bwd(res, ct):
        a, b = res
        da = prod(ct, b, da_dims) if a_first[0] else prod(b, ct, da_dims)
        db = prod(ct, a, db_dims) if a_first[1] else prod(a, ct, db_dims)
        return da, db

    f.defvjp(fwd, bwd)
    return f


_mdot_nn = _rounded_dot(_NN, _NT, _TN, (True, False))
_mdot_nt = _rounded_dot(_NT, _NN, _TN, (True, True))
_mdot_tn = _rounded_dot(_TN, _NT, _NN, (False, False))
def _unit_lower_inverse(a):
    C = a.shape[-1]
    ri = lax.broadcasted_iota(jnp.int32, (1, C, C), 1)
    ci = lax.broadcasted_iota(jnp.int32, (1, C, C), 2)
    t_inv = jnp.where(ri == ci, 1.0, 0.0) - a
    p = a
    for _ in range(max(C.bit_length() - 2, 0)):
        p = _mdot3(p, p, _NN)
        t_inv = t_inv + _mdot3(t_inv, p, _NN)
    return t_inv


@jax.custom_vjp
def _known_inverse(a, t_inv):
    del a
    return t_inv


def _known_inverse_fwd(a, t_inv):
    del a
    return t_inv, t_inv


def _known_inverse_bwd(t_inv, ct):
    da = -_mdot3(_mdot3(t_inv, ct, _TN), t_inv, _NT)
    return da, jnp.zeros_like(t_inv)


_known_inverse.defvjp(_known_inverse_fwd, _known_inverse_bwd)


def _head_cols(gates, off, H):
    lane = lax.broadcasted_iota(jnp.int32, gates.shape, 1)
    cols = [jnp.sum(jnp.where(lane == off + h, gates, 0.0), axis=-1, keepdims=True) for h in range(H)]
    return jnp.concatenate([col[None] for col in cols], axis=0)


def _delta_chunk(qr, kr, v, z, gates, nw, s_in, t_known=None):
    H, C, dk = qr.shape
    beta, g = _head_cols(gates, 0, H), _head_cols(gates, H, H)
    q = qr * lax.rsqrt(jnp.sum(qr * qr, axis=-1, keepdims=True) + L2_EPS) * (dk ** -0.5)
    k = kr * lax.rsqrt(jnp.sum(kr * kr, axis=-1, keepdims=True) + L2_EPS)
    ri = lax.broadcasted_iota(jnp.int32, (1, C, C), 1)
    ci = lax.broadcasted_iota(jnp.int32, (1, C, C), 2)
    causal, strict, eye = ri >= ci, ri > ci, ri == ci
    gam_row = jnp.sum(jnp.where(ri <= ci, g, 0.0), axis=1, keepdims=True)
    gam_col = jnp.sum(jnp.where(eye, gam_row, 0.0), axis=-1, keepdims=True)
    g_last = jnp.sum(g, axis=1, keepdims=True)
    decay = jnp.where(causal, jnp.exp(jnp.where(causal, gam_col - gam_row, 0.0)), 0.0)
    kb = k * beta
    a = jnp.where(strict, _mdot_nt(kb, k) * decay, 0.0)
    t_inv = _unit_lower_inverse(a) if t_known is None else _known_inverse(a, t_known)
    eg = jnp.exp(gam_col)
    u = _mdot_nn(t_inv, v * beta)
    w = _mdot_nn(t_inv, kb * eg)
    a_qk = _mdot_nt(q, k) * decay
    v_new = u - _mdot_nn(w, s_in)
    o = _mdot_nn(q * eg, s_in) + _mdot_nn(a_qk, v_new)
    s_out = s_in * jnp.exp(g_last) + _mdot_tn(k * jnp.exp(g_last - gam_col), v_new)
    og = o * lax.rsqrt(jnp.mean(o * o, axis=-1, keepdims=True) + RMS_EPS) * nw * (z * _sigmoid(z))
    return og, s_out, t_inv


def _heads(ref, H, dk):
    return jnp.stack([ref[:, h * dk:(h + 1) * dk].astype(F32) for h in range(H)])


def _put_heads(ref, val, dk):
    for h in range(val.shape[0]):
        ref[:, h * dk:(h + 1) * dk] = val[h].astype(ref.dtype)


def _delta_fwd(qkv, pm, gates, nw, H, name):
    S = qkv.shape[0]
    hd = qkv.shape[1] // 3
    dk = hd // H
    N = S // CHUNK
    blk = lambda off: pl.BlockSpec((CHUNK, hd), lambda n: (n, off))

    def body(q_ref, k_ref, v_ref, z_ref, g_ref, nw_ref, og_ref, st_ref, ti_ref, s_scr):
        @pl.when(pl.program_id(0) == 0)
        def _():
            s_scr[...] = jnp.zeros_like(s_scr)

        s_in = s_scr[...]
        st_ref[...] = s_in
        og, s_out, t_inv = _delta_chunk(_heads(q_ref, H, dk), _heads(k_ref, H, dk), _heads(v_ref, H, dk),
                                        _heads(z_ref, H, dk), g_ref[...], nw_ref[...], s_in)
        _put_heads(og_ref, og, dk)
        ti_ref[...] = t_inv
        s_scr[...] = s_out

    return _pcall(
        body, name=name, grid=(N,),
        in_specs=[blk(0), blk(1), blk(2), blk(3), pl.BlockSpec((CHUNK, LANES), lambda n: (n, 0)), _full(nw)],
        out_specs=(blk(0), pl.BlockSpec((None, H, dk, dk), lambda n: (n, 0, 0, 0)),
                   pl.BlockSpec((None, H, CHUNK, CHUNK), lambda n: (n, 0, 0, 0))),
        out_shape=(SDS((S, hd), _MXU), SDS((N, H, dk, dk), F32), SDS((N, H, CHUNK, CHUNK), F32)),
        scratch_shapes=[pltpu.VMEM((H, dk, dk), F32)],
        compiler_params=_cparams(VMEM_BIG),
    )(qkv, qkv, qkv, pm, gates, nw)


def _delta_bwd(qkv, pm, gates, nw, states, t_invs, dog, H, name):
    S = qkv.shape[0]
    hd = qkv.shape[1] // 3
    dk = hd // H
    N = S // CHUNK
    blk = lambda off: pl.BlockSpec((CHUNK, hd), lambda n: (N - 1 - n, off))
    gspec = pl.BlockSpec((CHUNK, LANES), lambda n: (N - 1 - n, 0))

    def body(q_ref, k_ref, v_ref, z_ref, g_ref, nw_ref, st_ref, ti_ref, do_ref,
             dq_ref, dk_ref, dv_ref, dz_ref, dg_ref, dnw_ref, ds_scr):
        n = pl.program_id(0)

        @pl.when(n == 0)
        def _():
            ds_scr[...] = jnp.zeros_like(ds_scr)

        t_known = ti_ref[...]
        fn = functools.partial(_delta_chunk, t_known=t_known)
        _, vjp = jax.vjp(fn, _heads(q_ref, H, dk), _heads(k_ref, H, dk), _heads(v_ref, H, dk),
                         _heads(z_ref, H, dk), g_ref[...], nw_ref[...], st_ref[...])
        dq, dkk, dv, dz, dg, dnw, ds_in = vjp((_heads(do_ref, H, dk), ds_scr[...], jnp.zeros_like(t_known)))
        _put_heads(dq_ref, dq, dk)
        _put_heads(dk_ref, dkk, dk)
        _put_heads(dv_ref, dv, dk)
        _put_heads(dz_ref, dz, dk)
        ds_scr[...] = ds_in
        dg_ref[...] = dg

        @pl.when(n == 0)
        def _():
            dnw_ref[...] = dnw

        @pl.when(n != 0)
        def _():
            dnw_ref[...] += dnw

    return _pcall(
        body, name=name, grid=(N,),
        in_specs=[blk(0), blk(1), blk(2), blk(3), gspec, _full(nw),
                  pl.BlockSpec((None, H, dk, dk), lambda n: (N - 1 - n, 0, 0, 0)),
                  pl.BlockSpec((None, H, CHUNK, CHUNK), lambda n: (N - 1 - n, 0, 0, 0)), blk(0)],
        out_specs=(blk(0), blk(0), blk(0), blk(3), gspec, pl.BlockSpec((1, dk), lambda n: (0, 0))),
        out_shape=(SDS((S, hd), F32), SDS((S, hd), F32), SDS((S, hd), F32), SDS(pm.shape, F32),
                   SDS((S, LANES), F32), SDS((1, dk), F32)),
        scratch_shapes=[pltpu.VMEM((H, dk, dk), F32)],
        compiler_params=_cparams(VMEM_BIG),
    )(qkv, qkv, qkv, pm, gates, nw, states, t_invs, dog)


def _rows_block(R, C):
    rb = R
    while rb * C * 4 > (1 << 20) and rb % 16 == 0:
        rb //= 2
    return rb


def _sum_slots(st, name, out_dtype=F32):
    n, R, C = st.shape
    rb = _rows_block(R, C)

    def body(s_ref, o_ref):
        acc = s_ref[0].astype(F32)
        for q in range(1, n):
            acc = acc + s_ref[q].astype(F32)
        o_ref[...] = acc.astype(o_ref.dtype)

    return _pcall(body, name=name, grid=(R // rb,), in_specs=[pl.BlockSpec((n, rb, C), lambda i: (0, i, 0))],
                  out_specs=pl.BlockSpec((rb, C), lambda i: (i, 0)), out_shape=SDS((R, C), out_dtype))(st)


def _scalar(v):
    return jnp.reshape(v, (1,)).astype(jnp.int32)


def _place_block(w, layer, ax, chip, dep, name):
    _, R, C = w.shape
    rb = _rows_block(R, C)
    nrb = R // rb
    shp = [R, C]
    shp[ax] *= 4
    omap = (lambda i, c: (c[0] * nrb + i, 0)) if ax == 0 else (lambda i, c: (i, c[0]))

    def body(c_ref, w_ref, dep_ref, o_ref):
        del c_ref, dep_ref
        o_ref[...] = w_ref[...].astype(o_ref.dtype)

    grid_spec = pltpu.PrefetchScalarGridSpec(
        num_scalar_prefetch=1, grid=(nrb,),
        in_specs=[pl.BlockSpec((None, rb, C), lambda i, c: (layer, i, 0)), _ANY],
        out_specs=pl.BlockSpec((rb, C), omap))
    return _pcall(body, name=name, grid_spec=grid_spec, out_shape=SDS(tuple(shp), _MXU))(_scalar(chip), w, dep)


def _sum_pair(own, recv, ic, name, out_dtype):
    N, _, R, C = own.shape
    rb = _rows_block(R, C)

    def body(c_ref, a_ref, b_ref, o_ref):
        del c_ref
        o_ref[...] = (a_ref[...].astype(F32) + b_ref[...].astype(F32)).astype(o_ref.dtype)

    grid_spec = pltpu.PrefetchScalarGridSpec(
        num_scalar_prefetch=1, grid=(N, R // rb),
        in_specs=[pl.BlockSpec((None, None, rb, C), lambda p, i, c: (p, c[0], i, 0)),
                  pl.BlockSpec((None, None, rb, C), lambda p, i, c: (1 - c[0], p, i, 0))],
        out_specs=pl.BlockSpec((None, rb, C), lambda p, i, c: (p, i, 0)))
    return _pcall(body, name=name, grid_spec=grid_spec, out_shape=SDS((N, R, C), out_dtype))(
        _scalar(ic), own, recv)


def _sum_landed(grad, ax, land, chip, name):
    _, R, Cb = land.shape
    rb = _rows_block(R, Cb)
    nrb = R // rb
    if ax == 0:
        own_spec = pl.BlockSpec((rb, Cb), lambda i, c: (c[0] * nrb + i, 0))
    else:
        own_spec = pl.BlockSpec((rb, Cb), lambda i, c: (i, c[0]))
    slot = lambda d: pl.BlockSpec((None, rb, Cb), lambda i, c: ((c[0] + d) % 4, i, 0))

    def body(c_ref, own_ref, r1, r2, r3, o_ref):
        del c_ref
        o_ref[...] = ((own_ref[...].astype(F32) + r1[...].astype(F32)) + r2[...].astype(F32)) + r3[...].astype(F32)

    grid_spec = pltpu.PrefetchScalarGridSpec(
        num_scalar_prefetch=1, grid=(R // rb,), in_specs=[own_spec, slot(1), slot(2), slot(3)],
        out_specs=pl.BlockSpec((rb, Cb), lambda i, c: (i, 0)))
    return _pcall(body, name=name, grid_spec=grid_spec, out_shape=SDS((R, Cb), F32))(
        _scalar(chip), grad, land, land, land)


def _adamw_step(g, w_ref, m_ref, v_ref, g_ref, d_ref, mo_ref, vo_ref):
    m_new = ADAM_B1 * m_ref[...] + (1.0 - ADAM_B1) * g
    v_new = ADAM_B2 * v_ref[...] + (1.0 - ADAM_B2) * (g * g)
    m_hat = m_new / (1.0 - ADAM_B1 ** ADAM_STEP)
    v_hat = v_new / (1.0 - ADAM_B2 ** ADAM_STEP)
    g_ref[...] = g
    d_ref[...] = -ADAM_LR * (m_hat / (jnp.sqrt(v_hat) + ADAM_EPS) + ADAM_WD * w_ref[...])
    mo_ref[...] = m_new
    vo_ref[...] = v_new


def _adamw(w, m, v, st, name):
    R, C = w.shape
    n = st.shape[0]
    rb = _rows_block(R, C)
    spec = pl.BlockSpec((rb, C), lambda i: (i, 0))

    def body(w_ref, m_ref, v_ref, s_ref, *o_refs):
        g = s_ref[0]
        for q in range(1, n):
            g = g + s_ref[q]
        _adamw_step(g, w_ref, m_ref, v_ref, *o_refs)

    return _pcall(body, name=name, grid=(R // rb,),
                  in_specs=[spec, spec, spec, pl.BlockSpec((n, rb, C), lambda i: (0, i, 0))],
                  out_specs=(spec,) * 4, out_shape=(SDS((R, C), F32),) * 4)(w, m, v, st)


def _adamw_pair(w, m, v, layer, own, recv2, ic, bufs, dep, name):
    L, R, C = w.shape
    rb = _rows_block(R, C)
    spec = pl.BlockSpec((None, rb, C), lambda i, c: (layer, i, 0))

    def body(c_ref, w_ref, m_ref, v_ref, own_ref, recv_ref, *rest):
        del c_ref
        _adamw_step(own_ref[...] + recv_ref[...], w_ref, m_ref, v_ref, *rest[-4:])

    nbuf = 0 if bufs is None else 4
    grid_spec = pltpu.PrefetchScalarGridSpec(
        num_scalar_prefetch=1, grid=(R // rb,),
        in_specs=[spec, spec, spec, pl.BlockSpec((rb, C), lambda i, c: (i, 0)),
                  pl.BlockSpec((None, rb, C), lambda i, c: (1 - c[0], i, 0))] + [_ANY] * (nbuf + 1),
        out_specs=(spec,) * 4)
    extra = {} if bufs is None else dict(input_output_aliases={6 + q: q for q in range(4)})
    return _pcall(body, name=name, grid_spec=grid_spec, out_shape=(SDS((L, R, C), F32),) * 4, **extra)(
        _scalar(ic), w, m, v, own, recv2, *([] if bufs is None else bufs), dep)


def _pack(arrs, rows=1):
    flat = jnp.concatenate([a.reshape(-1).astype(F32) for a in arrs])
    quantum = rows * LANES
    pad = (-flat.shape[0]) % quantum
    flat = jnp.pad(flat, (0, pad))
    return flat.reshape(rows, -1)


def _unpack(flat, shapes):
    flat = flat.reshape(-1)
    out, off = [], 0
    for shp in shapes:
        size = 1
        for d in shp:
            size *= d
        out.append(flat[off:off + size].reshape(shp))
        off += size
    return out


def _mlp_fwd(x1, mod, lnp, w1, w2, tag):
    h2 = _modulate(x1, mod, 3, 4, f"{tag}_mod")
    a1, a2 = _mm(h2, w1, "nn", name=f"{tag}_up", out_dtypes=(_MXU, _MXU),
                 out_fn=lambda r: (r, jnp.square(jnp.maximum(r, 0.0))))
    y2 = _mm(a2, w2, "nn", name=f"{tag}_down")
    x2 = _combine(x1, y2, mod, 5, lnp, 2, f"{tag}_ln")
    return x2, (x1, h2, a1, a2, y2)


def _weight_grad(grads, key, a, b, name):
    grads[key] = _mm(a, b, "tn", name=name, out_dtypes=(_MXU,))


def _mlp_bwd(dx2, saved, mod, lnp, w1, w2, tag, stacks):
    x1, h2, a1, a2, y2 = saved
    dxa, dy2, dp, dss_later = _sublayer_head_bwd(dx2, x1, y2, mod, 5, lnp, 2, tag)
    da1 = _mm(dy2, w2, "nt", name=f"{tag}_down_bx", out_dtypes=(_MXU,), aux=[(a1, "mn")],
              out_fn=lambda r, a: (r * (2.0 * jnp.maximum(a.astype(F32), 0.0)),))
    _weight_grad(stacks, "ff_w2", a2, dy2, f"{tag}_down_bw")
    _weight_grad(stacks, "ff_w1", h2, da1, f"{tag}_up_bw")
    dh2 = _mm(da1, w1, "nt", name=f"{tag}_up_bx")
    return (x1, mod, 4, dh2, dxa), (dss_later, dp)


def _dn_fwd(x, mod, lnp, wts, H, tag):
    w_main, w_small, conv_w, prm, nw, w_out = wts
    h = _modulate(x, mod, 0, 1, f"{tag}_mod")
    pm = _mm(h, w_main, "nn", name=f"{tag}_in")
    ps = _mm(h, w_small, "nn", name=f"{tag}_in_s")
    nqkv = conv_w.shape[1] // LANES
    qkv = _conv_silu(pm, conv_w, nqkv, f"{tag}_conv")
    gates = _gates(ps, prm, H, f"{tag}_gates")
    og, *states = _delta_fwd(qkv, pm, gates, nw, H, f"{tag}_delta")
    y = _mm(og, w_out, "nn", name=f"{tag}_out")
    x1 = _combine(x, y, mod, 2, lnp, 0, f"{tag}_ln")
    return x1, (x, h, pm, ps, qkv, gates, states, og, y)


def _dn_bwd(dx1, saved, mod, lnp, wts, H, tag, stacks):
    w_main, w_small, conv_w, prm, nw, w_out = wts
    x, h, pm, ps, qkv, gates, states, og, y = saved
    dxa, dy, dp, dss_later = _sublayer_head_bwd(dx1, x, y, mod, 2, lnp, 0, tag)
    dog = _mm(dy, w_out, "nt", name=f"{tag}_out_bx")
    _weight_grad(stacks, "dn_w_out", og, dy, f"{tag}_out_bw")
    dq, dk, dv, dpm, dgates, dnw = _delta_bwd(qkv, pm, gates, nw, *states, dog, H, f"{tag}_delta_b")
    dps, dprm = _gates_bwd(ps, prm, dgates, H, f"{tag}_gates_b")
    dcw = []
    nb = dq.shape[1] // LANES
    for part, dpart in enumerate((dq, dk, dv)):
        dpm, dcw_p = _conv_silu_bwd(pm, conv_w, dpart, dpm, part * nb, f"{tag}_conv_b{part}")
        dcw.append(dcw_p)
    dconv_w = jnp.concatenate(dcw, axis=1)
    dw_main = _mm(h, dpm, "tn", name=f"{tag}_in_bw", out_dtypes=(_MXU,))
    dw_small = _mm(h, dps, "tn", name=f"{tag}_in_s_bw", out_dtypes=(_MXU,))
    dh_s = _mm(dps, w_small, "nt", name=f"{tag}_in_s_bx")
    dh = _mm(dpm, w_main, "nt", name=f"{tag}_in_bx", aux=[(dh_s, "mn")], out_fn=lambda r, e: (r + e,))
    return (x, mod, 1, dh, dxa), (dw_main, dw_small, dconv_w, dprm, dnw), (dss_later, dp)


def _cf_fwd(x, mod, lnp, wts, tag):
    w_in, dw_w, dw_b, cln, w_out = wts
    h = _modulate(x, mod, 0, 1, f"{tag}_mod")
    p = _mm(h, w_in, "nn", name=f"{tag}_in")
    u2 = _glu_conv(p, dw_w, dw_b, f"{tag}_conv")
    u3 = _ln_silu(u2, cln, f"{tag}_cln")
    y = _mm(u3, w_out, "nn", name=f"{tag}_out")
    x1 = _combine(x, y, mod, 2, lnp, 0, f"{tag}_ln")
    return x1, (x, h, p, u2, u3, y)


def _cf_bwd(dx1, saved, mod, lnp, wts, tag, stacks):
    w_in, dw_w, dw_b, cln, w_out = wts
    x, h, p, u2, u3, y = saved
    dxa, dy, dp, dss_later = _sublayer_head_bwd(dx1, x, y, mod, 2, lnp, 0, tag)
    du3 = _mm(dy, w_out, "nt", name=f"{tag}_out_bx")
    _weight_grad(stacks, "cf_w_out", u3, dy, f"{tag}_out_bw")
    du2, dcln = _ln_silu_bwd(u2, cln, du3, f"{tag}_cln_b")
    dval, dgate, ddw_w, ddw_b = _glu_conv_bwd(p, dw_w, du2, f"{tag}_conv_b")
    dpp = jnp.concatenate([dval, dgate], axis=1)
    _weight_grad(stacks, "cf_w_in", h, dpp, f"{tag}_in_bw")
    dh = _mm(dpp, w_in, "nt", name=f"{tag}_in_bx")
    return (x, mod, 1, dh, dxa), (ddw_w, ddw_b, dcln), (dss_later, dp)


def _two_d(a):
    return a.reshape(-1, a.shape[-1])


def kernel(x, c, ada_w, ada_b, ln_g, ln_b, dn_w_in, dn_conv_w, dn_a_log, dn_dt_bias, dn_norm_w, dn_w_out, cf_w_in, cf_dw_w, cf_dw_b, cf_ln_g, cf_ln_b, cf_w_out, ff_w1, ff_w2, loss_target, m_ada_w, m_ada_b, m_ln_g, m_ln_b, m_dn_w_in, m_dn_conv_w, m_dn_a_log, m_dn_dt_bias, m_dn_norm_w, m_dn_w_out, m_cf_w_in, m_cf_dw_w, m_cf_dw_b, m_cf_ln_g, m_cf_ln_b, m_cf_w_out, m_ff_w1, m_ff_w2, v_ada_w, v_ada_b, v_ln_g, v_ln_b, v_dn_w_in, v_dn_conv_w, v_dn_a_log, v_dn_dt_bias, v_dn_norm_w, v_dn_w_out, v_cf_w_in, v_cf_dw_w, v_cf_dw_b, v_cf_ln_g, v_cf_ln_b, v_cf_w_out, v_ff_w1, v_ff_w2):
    ix, iy, ic = lax.axis_index("x"), lax.axis_index("y"), lax.axis_index("c")
    chip = 2 * ix + iy
    dev = 4 * ix + 2 * iy + ic
    S, D = x.shape[1], x.shape[2]
    L = ada_w.shape[0]
    LA, LB = dn_w_in.shape[0], cf_w_in.shape[0]
    H = dn_a_log.shape[1]
    NMOD = ada_b.shape[1] // D
    dn_in = dn_w_in.shape[2] * 4
    n_main = dn_in - 2 * H
    assert L == N_LAYERS and 2 * H <= LANES
    x0, tgt = x[0], loss_target[0]

    small_sharded = [ln_g, ln_b, dn_conv_w, cf_dw_w, cf_dw_b, cf_ln_g, cf_ln_b]
    small_axes = [2, 2, 2, 2, 1, 1, 1]
    packed_small = _pack(small_sharded, rows=8)[None]
    big = {"dn_w_in": (dn_w_in, m_dn_w_in, v_dn_w_in, 0), "dn_w_out": (dn_w_out, m_dn_w_out, v_dn_w_out, 0),
           "cf_w_in": (cf_w_in, m_cf_w_in, v_cf_w_in, 1), "cf_w_out": (cf_w_out, m_cf_w_out, v_cf_w_out, 0),
           "ff_w1": (ff_w1, m_ff_w1, v_ff_w1, 1), "ff_w2": (ff_w2, m_ff_w2, v_ff_w2, 0)}

    def group(g):
        l = g // 2
        if g % 2:
            return {"ff_w1": l, "ff_w2": l}
        mixer = ("dn_w_in", "dn_w_out") if l % 2 == 0 else ("cf_w_in", "cf_w_out")
        return {mixer[0]: l // 2, mixer[1]: l // 2}

    def place(l, dep):
        out = []
        for nm, lw in group(l).items():
            out.append(_place_block(big[nm][0], lw, big[nm][3], chip, dep, f"l{l}_place_{nm}"))
            dep = out[-1]
        return out

    def start_gather(l, placed, after):
        names = list(group(l))
        axes = [big[nm][3] for nm in names]
        send, recv, arrs, token = _gather_start(list(zip(placed, axes)), after, f"l{l}_gather_start")
        return names, axes, arrs, send, recv, token

    def finish_gather(l, pending, after):
        names, axes, arrs, send, recv, _ = pending
        arrs = _gather_wait(list(zip(arrs, axes)), send, recv, after, f"l{l}_gather_wait")
        arrs = _gather_forward(list(zip(arrs, axes)), f"l{l}_gather_pass")
        return dict(zip(names, arrs))

    g_small = _exchange([(packed_small, 0)], "xy", "gather", "gather_small")[0]
    shard_shapes = [a.shape for a in small_sharded]
    per_chip = [_unpack(g_small[q], shard_shapes) for q in range(4)]
    ln_g_f, ln_b_f, conv_w_f, dw_w_f, dw_b_f, cln_g_f, cln_b_f = [
        jnp.concatenate([per_chip[q][i] for q in range(4)], axis=small_axes[i]) for i in range(len(small_sharded))]

    c_all = _exchange([(c[None], 0)], "all", "gather", "gather_cond")[0].reshape(8, D)
    c_pad = jnp.pad(c_all, ((0, 8), (0, 0)))
    mod_sh = jnp.stack([_mm(c_pad, (ada_w, l), "nn", name=f"ada_{l}", a_fn=lambda t: t * _sigmoid(t))
                        for l in range(L)])
    mod_all = _exchange([(mod_sh, 2)], "xy", "gather", "gather_mod")[0]
    mod_mine = lax.dynamic_index_in_dim(mod_all, dev, axis=1, keepdims=False) + ada_b
    mods = mod_mine.reshape(L, NMOD, D)

    def lnp_of(l):
        return jnp.stack([ln_g_f[l, 0], ln_b_f[l, 0], ln_g_f[l, 1], ln_b_f[l, 1]])

    def mixer_wts(l, wl):
        j = l // 2
        if l % 2 == 0:
            w_in = jnp.transpose(wl["dn_w_in"].reshape(4, D, dn_in // 4), (1, 0, 2)).reshape(D, dn_in)
            w_small = jnp.pad(w_in[:, n_main:], ((0, 0), (0, LANES - 2 * H)))
            prm = jnp.zeros((2, LANES), F32).at[0, H:2 * H].set(dn_a_log[j]).at[1, H:2 * H].set(dn_dt_bias[j])
            return (w_in[:, :n_main], w_small, conv_w_f[j], prm, dn_norm_w[j][None], wl["dn_w_out"])
        return (wl["cf_w_in"], dw_w_f[j], dw_b_f[j][None], jnp.stack([cln_g_f[j], cln_b_f[j]]), wl["cf_w_out"])

    xs = x0
    saved, wts, mod_of = [], [], []
    pending = start_gather(0, place(0, mods), mods + jnp.minimum(jnp.abs(g_small[0, 0, 0]), 0.0))
    placed = [None]
    for g in range(1, 2 * L):
        placed.append(place(g, pending[5] if g == 1 else placed[-1][-1]))
    for g in range(2 * L):
        l = g // 2
        wl = finish_gather(g, pending, xs if g else placed[-1][-1])
        mod_g = mods[l]
        if g + 1 < 2 * L:
            pending = start_gather(g + 1, placed[g + 1], next(iter(wl.values())))
            mod_g = mod_g + pending[5][0, 0]
        mod_of.append(mod_g)
        if g % 2:
            wts.append((wl["ff_w1"], wl["ff_w2"]))
            xs, sv = _mlp_fwd(xs, mod_g, lnp_of(l), *wts[g], f"l{l}_ff")
        elif l % 2 == 0:
            wts.append(mixer_wts(l, wl))
            xs, sv = _dn_fwd(xs, mod_g, lnp_of(l), wts[g], H, f"l{l}_dn")
        else:
            wts.append(mixer_wts(l, wl))
            xs, sv = _cf_fwd(xs, mod_g, lnp_of(l), wts[g], f"l{l}_cf")
        saved.append(sv)
    dx, loss_local = _loss_head(xs, tgt, "loss_head")
    loss = lax.psum(loss_local[0, 0], ("x", "y", "c"))

    def start_scatter(l, grads):
        names = list(group(l))
        axes = [big[nm][3] for nm in names]
        send, recv, sums, lands, token = _scatter_start([(grads[nm], ax) for nm, ax in zip(names, axes)], mods,
                                                        f"l{l}_scatter_start")
        return names, axes, sums, lands, send, recv, token

    def finish_scatter(l, pending, after):
        names, axes, sums, lands, send, recv, _ = pending
        sums, lands = _scatter_wait(list(zip(sums, axes)), lands, send, recv, after, f"l{l}_scatter_wait")
        mine = [_sum_landed(s, ax, ld, chip, f"l{l}_sum_grads_{nm}") for nm, ax, s, ld in zip(names, axes, sums, lands)]
        other = _exchange([(s[None], 0) for s in mine], "c", "gather", f"l{l}_swap_sums", keep_own=False)
        return {nm: (s, o) for nm, s, o in zip(names, mine, other)}

    g_dn = [None] * LA
    g_cf = [None] * LB
    dmods, dlns = [None] * L, [None] * L
    big_sums = {}
    pending, token = None, None
    dss_of, dp_of = {}, {}
    for g in reversed(range(2 * L)):
        l, j = g // 2, g // 4
        mod_g = mod_of[g] if token is None else mod_of[g] + token[0, 0]
        grads = {}
        if g % 2:
            dx, (dss_of[g + 1], dp_of[g]) = _mlp_bwd(dx, saved[g], mod_g, lnp_of(l), *wts[g], f"l{l}_ff", grads)
        elif l % 2 == 0:
            dx, g_dn[j], (dss_of[g + 1], dp_of[g]) = _dn_bwd(dx, saved[g], mod_g, lnp_of(l), wts[g], H,
                                                              f"l{l}_dn", grads)
            dn_in_g = jnp.concatenate([g_dn[j][0], g_dn[j][1][:, :2 * H]], axis=1)
            grads["dn_w_in"] = jnp.transpose(dn_in_g.reshape(D, 4, dn_in // 4), (1, 0, 2)).reshape(4 * D, dn_in // 4)
        else:
            dx, g_cf[j], (dss_of[g + 1], dp_of[g]) = _cf_bwd(dx, saved[g], mod_g, lnp_of(l), wts[g],
                                                              f"l{l}_cf", grads)
        if pending is not None:
            for nm, pair_of in finish_scatter(g + 1, pending, dx[3]).items():
                big_sums[nm, group(g + 1)[nm]] = pair_of
        pending = start_scatter(g, grads)
        token = pending[6]
    x_in, mod_in, r_sc_in, dh_in, dxa_in = dx
    dx, dss_of[0] = _modulate_bwd(x_in, mod_in, r_sc_in, dh_in, dxa_in, "l0_dn_mod_b")
    big_names = ["dn_w_in", "dn_w_out", "cf_w_in", "cf_w_out", "ff_w1", "ff_w2"]
    big_res = {nm: None for nm in big_names}

    def update(nm, layer):
        w, m, v, _ = big[nm]
        own, oth = big_sums[nm, layer]
        big_res[nm] = _adamw_pair(w, m, v, layer, own, oth, ic, big_res[nm], token, f"adamw_{nm}_{layer}")

    for nm in big_names:
        for layer in range(big[nm][0].shape[0]):
            if group(0).get(nm) != layer:
                update(nm, layer)
    for nm, pair_of in finish_scatter(0, pending, big_res["ff_w2"][0]).items():
        big_sums[nm, group(0)[nm]] = pair_of
    for nm, layer in group(0).items():
        update(nm, layer)
    big_out = [big_res[nm] for nm in big_names]
    for l in range(L):
        (dss1, dp1), (dss2, dp2) = (dss_of[2 * l], dp_of[2 * l]), (dss_of[2 * l + 1], dp_of[2 * l + 1])
        dmods[l] = jnp.concatenate([dss1, dp1[0:1], dss2, dp2[0:1]], axis=0)
        dlns[l] = (jnp.stack([dp1[1], dp2[1]]), jnp.stack([dp1[2], dp2[2]]))
    grad_x = dx[None]

    d_ln_g = jnp.stack([dlns[l][0] for l in range(L)])
    d_ln_b = jnp.stack([dlns[l][1] for l in range(L)])
    d_conv_w = jnp.stack([g_dn[j][2] for j in range(LA)])
    d_a_log = jnp.stack([g_dn[j][3][0, H:2 * H] for j in range(LA)])
    d_dt_bias = jnp.stack([g_dn[j][3][1, H:2 * H] for j in range(LA)])
    d_norm_w = jnp.stack([g_dn[j][4][0] for j in range(LA)])
    d_dw_w = jnp.stack([g_cf[j][0] for j in range(LB)])
    d_dw_b = jnp.stack([g_cf[j][1][0] for j in range(LB)])
    d_cln_g = jnp.stack([g_cf[j][2][0] for j in range(LB)])
    d_cln_b = jnp.stack([g_cf[j][2][1] for j in range(LB)])
    d_mod = jnp.stack(dmods).reshape(L, NMOD * D)
    small_full = [d_mod, d_ln_g, d_ln_b, d_conv_w, d_dw_w, d_dw_b, d_cln_g, d_cln_b, d_a_log, d_dt_bias, d_norm_w]
    landed = jnp.minimum(jnp.abs(big_sums["dn_w_out", 0][0][0, 0]), 0.0)
    small_all = _exchange([(_pack(small_full, rows=8)[None] + landed, 0)], "all", "gather", "gather_small_grads")[0]
    small_sum = _sum_slots(small_all, "sum_small_grads")
    (s_ada_b, s_ln_g, s_ln_b, s_conv_w, s_dw_w, s_dw_b, s_cln_g, s_cln_b, s_a_log, s_dt_bias, s_norm_w) = _unpack(
        small_sum, [a.shape for a in small_full])
    d_mod_all = small_all.reshape(8, -1)[:, :L * NMOD * D].reshape(8, L, NMOD * D)

    def shard(a, axis):
        size = a.shape[axis] // 4
        return lax.dynamic_slice_in_dim(a, chip * size, size, axis)

    ncol = ada_w.shape[2]
    d_mod_sh = jnp.pad(lax.dynamic_slice_in_dim(d_mod_all, chip * ncol, ncol, 2), ((0, 8), (0, 0), (0, 0)))
    g_ada_w = None
    for l in range(L):
        g_ada_w = _mm(c_pad, d_mod_sh[:, l], "tn", name=f"ada_bw_{l}", a_fn=lambda t: t * _sigmoid(t),
                      stack=(g_ada_w, l, L))

    ada_out =[r.reshape(ada_w.shape) for r in _adamw(_two_d(ada_w), _two_d(m_ada_w), _two_d(v_ada_w),
                                                     _two_d(g_ada_w)[None], "adamw_ada_w")]

    small_w = [(ada_b, m_ada_b, v_ada_b, s_ada_b), (ln_g, m_ln_g, v_ln_g, shard(s_ln_g, 2)),
               (ln_b, m_ln_b, v_ln_b, shard(s_ln_b, 2)), (dn_conv_w, m_dn_conv_w, v_dn_conv_w, shard(s_conv_w, 2)),
               (dn_a_log, m_dn_a_log, v_dn_a_log, s_a_log), (dn_dt_bias, m_dn_dt_bias, v_dn_dt_bias, s_dt_bias),
               (dn_norm_w, m_dn_norm_w, v_dn_norm_w, s_norm_w), (cf_dw_w, m_cf_dw_w, v_cf_dw_w, shard(s_dw_w, 2)),
               (cf_dw_b, m_cf_dw_b, v_cf_dw_b, shard(s_dw_b, 1)), (cf_ln_g, m_cf_ln_g, v_cf_ln_g, shard(s_cln_g, 1)),
               (cf_ln_b, m_cf_ln_b, v_cf_ln_b, shard(s_cln_b, 1))]
    pk = [_pack([t[i] for t in small_w], rows=8) for i in range(4)]
    small_res = _adamw(pk[0], pk[1], pk[2], pk[3][None], "adamw_small")
    small_shapes = [t[0].shape for t in small_w]
    small_out = [_unpack(r, small_shapes) for r in small_res]

    def kind(k):
        sm = small_out[k]
        bg = [o[k] for o in big_out]
        return [ada_out[k], sm[0], sm[1], sm[2], bg[0], sm[3], sm[4], sm[5], sm[6], bg[1],
                bg[2], sm[7], sm[8], sm[9], sm[10], bg[3], bg[4], bg[5]]

    return (loss, grad_x, *kind(0), *kind(1), *kind(2), *kind(3))
```

```python
import functools

import jax
import jax.numpy as jnp
from jax import lax
from jax.experimental import pallas as pl
from jax.experimental.pallas import tpu as pltpu

F32 = jnp.float32
_MXU = jnp.bfloat16

N_LAYERS = 4
ALPHA = (2.0 * N_LAYERS) ** 0.25
LN_EPS = 1e-5
RMS_EPS = 1e-6
L2_EPS = 1e-6
CHUNK = 64
ADAM_LR, ADAM_B1, ADAM_B2, ADAM_EPS, ADAM_WD, ADAM_STEP = 0.001, 0.9, 0.999, 1e-08, 0.01, 10

LANES = 128
TOKEN_BLOCK = 256
VMEM_BIG = 48 * 1024 * 1024
MM_TILE = 1024

SDS = jax.ShapeDtypeStruct
MESH = pl.DeviceIdType.MESH


def _cparams(vmem=None):
    if vmem is None:
        return None
    return pltpu.CompilerParams(vmem_limit_bytes=vmem)


def _pcall(body, **kw):
    if kw.get("compiler_params", 1) is None:
        kw.pop("compiler_params")
    return pl.pallas_call(body, **kw)


def _full(arr):
    nd = arr.ndim
    return pl.BlockSpec(arr.shape, lambda *g: (0,) * nd)


def _bs(block, imap, lead=None):
    if lead is None:
        return pl.BlockSpec(block, imap)
    return pl.BlockSpec((None,) + tuple(block), lambda *g: (lead,) + tuple(imap(*g)))


def _split(a):
    return a if isinstance(a, tuple) else (a, None)


_GROUPS = {
    "xy": ([(1, 0, 0), (0, 1, 0), (1, 1, 0)], 4),
    "c": ([(0, 0, 1)], 2),
    "all": ([(1, 0, 0), (0, 1, 0), (1, 1, 0), (0, 0, 1), (1, 0, 1), (0, 1, 1), (1, 1, 1)], 8),
}


def _exchange(items, group, mode, name, keep_own=True):
    masks, n = _GROUPS[group]
    npeer = len(masks)
    ni = len(items)
    arrs = [a for a, _ in items]
    out_shapes = []
    for a, ax in items:
        shp = list(a.shape)
        if mode == "gather":
            shp[ax] *= n
        else:
            shp[ax] //= n
            shp = [n] + shp
        out_shapes.append(SDS(tuple(shp), a.dtype))

    def body(*refs):
        ins, outs = refs[:ni], refs[ni:2 * ni]
        send_sems, recv_sems, local_sems = refs[2 * ni:]
        x, y, c = lax.axis_index("x"), lax.axis_index("y"), lax.axis_index("c")

        def slot(px, py, pc):
            if group == "xy":
                return 2 * px + py
            if group == "c":
                return pc
            return 4 * px + 2 * py + pc

        me = slot(x, y, c)

        def block(ref, ax, idx, size):
            ix = (slice(None),) * ax + (pl.ds(pl.multiple_of(idx * size, size), size),)
            return ref.at[ix]

        copies = []
        for it, (a, ax) in enumerate(items):
            in_ref, out_ref = ins[it], outs[it]
            if mode == "gather":
                size = a.shape[ax]
                src_own, dst_own = in_ref, block(out_ref, ax, me, size)
            else:
                size = a.shape[ax] // n
                src_own, dst_own = block(in_ref, ax, me, size), out_ref.at[me]
            if keep_own:
                own = pltpu.make_async_copy(src_own, dst_own, local_sems.at[it])
                own.start()
                copies.append(own)
            for k, m in enumerate(masks):
                peer = tuple((1 - v) if b else v for v, b in zip((x, y, c), m))
                if mode == "gather":
                    src, dst = in_ref, dst_own
                else:
                    src, dst = block(in_ref, ax, slot(*peer), size), out_ref.at[me]
                cp = pltpu.make_async_remote_copy(
                    src_ref=src, dst_ref=dst, send_sem=send_sems.at[it * npeer + k],
                    recv_sem=recv_sems.at[it * npeer + k], device_id=peer, device_id_type=MESH)
                cp.start()
                copies.append(cp)
        for cp in copies:
            cp.wait()

    any_spec = pl.BlockSpec(memory_space=pl.ANY)
    outs = _pcall(
        body, name=name, out_shape=tuple(out_shapes),
        in_specs=[any_spec] * ni, out_specs=tuple([any_spec] * ni),
        scratch_shapes=[pltpu.SemaphoreType.DMA((ni * npeer,)), pltpu.SemaphoreType.DMA((ni * npeer,)),
                        pltpu.SemaphoreType.DMA((ni,))],
    )(*arrs)
    return list(outs)


_HBM = pl.BlockSpec(memory_space=pltpu.HBM)
_SEM = pl.BlockSpec(memory_space=pltpu.SEMAPHORE)
_ANY = pl.BlockSpec(memory_space=pl.ANY)
_SPLIT = pltpu.CompilerParams(has_side_effects=pltpu.SideEffectType.DATAFLOW_SIDE_EFFECTING)
_XY = _GROUPS["xy"][0]


def _in_hbm(a):
    return pltpu.with_memory_space_constraint(a, pltpu.HBM)


def _chip_peers():
    x, y, c = lax.axis_index("x"), lax.axis_index("y"), lax.axis_index("c")
    return x, y, c, [tuple((1 - v) if b else v for v, b in zip((x, y), m)) for m in _XY]


def _wblock(ref, ax, half, chip):
    R, C = ref.shape
    if ax == 0:
        rows = R // 8
        return ref.at[pl.ds(pl.multiple_of(chip * (2 * rows) + half * rows, rows), rows), :]
    rows, cols = R // 2, C // 4
    return ref.at[pl.ds(pl.multiple_of(half * rows, rows), rows), pl.ds(pl.multiple_of(chip * cols, cols), cols)]


def _gather_start(items, after, name):
    ni = len(items)
    arrs = [a for a, _ in items]

    def body(*refs):
        send_sems, recv_sems = refs[ni + 1], refs[ni + 2]
        outs, token = refs[ni + 3:2 * ni + 3], refs[2 * ni + 3]
        x, y, c, peers = _chip_peers()
        for it, (_, ax) in enumerate(items):
            mine = _wblock(outs[it], ax, c, 2 * x + y)
            for k, (px, py) in enumerate(peers):
                pltpu.make_async_remote_copy(
                    src_ref=mine, dst_ref=mine, send_sem=send_sems.at[it * 3 + k], recv_sem=recv_sems.at[it * 3 + k],
                    device_id=(px, py, c), device_id_type=MESH).start()
        token[...] = jnp.zeros_like(token)

    res = _pcall(
        body, name=name,
        out_shape=(pltpu.SemaphoreType.DMA((ni * 3,)), pltpu.SemaphoreType.DMA((ni * 3,)),
                   *[pltpu.HBM(a.shape, a.dtype) for a in arrs], SDS((8, LANES), F32)),
        in_specs=[_HBM] * ni + [_ANY],
        out_specs=(_SEM, _SEM, *[_HBM] * ni, pl.BlockSpec(memory_space=pltpu.VMEM)),
        input_output_aliases={i: 2 + i for i in range(ni)}, compiler_params=_SPLIT,
    )(*[_in_hbm(a) for a in arrs], after)
    return res[0], res[1], list(res[2:2 + ni]), res[2 + ni]


def _gather_wait(items, send_sems, recv_sems, after, name):
    ni = len(items)
    arrs = [a for a, _ in items]

    def body(*refs):
        s_sems, r_sems = refs[ni], refs[ni + 1]
        outs = refs[ni + 3:]
        x, y, c, peers = _chip_peers()
        for it, (_, ax) in enumerate(items):
            mine = _wblock(outs[it], ax, c, 2 * x + y)
            for k, (px, py) in enumerate(peers):
                cp = pltpu.make_async_remote_copy(
                    src_ref=mine, dst_ref=_wblock(outs[it], ax, c, 2 * px + py), send_sem=s_sems.at[it * 3 + k],
                    recv_sem=r_sems.at[it * 3 + k], device_id=(px, py, c), device_id_type=MESH)
                cp.wait_send()
                cp.wait_recv()

    res = _pcall(
        body, name=name, out_shape=tuple(pltpu.HBM(a.shape, a.dtype) for a in arrs),
        in_specs=[_HBM] * ni + [_SEM, _SEM, _ANY], out_specs=tuple([_HBM] * ni),
        input_output_aliases={i: i for i in range(ni)}, compiler_params=_SPLIT,
    )(*arrs, send_sems, recv_sems, after)
    return list(res)


def _gather_forward(items, name):
    ni = len(items)
    arrs = [a for a, _ in items]

    def body(*refs):
        outs = refs[ni:2 * ni]
        send_sems, recv_sems = refs[2 * ni:]
        x, y, c, peers = _chip_peers()
        def copy(it, ax, k, chip, dst_half):
            return pltpu.make_async_remote_copy(
                src_ref=_wblock(outs[it], ax, c, chip), dst_ref=_wblock(outs[it], ax, dst_half, chip),
                send_sem=send_sems.at[it * 3 + k], recv_sem=recv_sems.at[it * 3 + k],
                device_id=(x, y, 1 - c), device_id_type=MESH)

        for it, (_, ax) in enumerate(items):
            for k, (px, py) in enumerate(peers):
                copy(it, ax, k, 2 * px + py, c).start()
        for it, (_, ax) in enumerate(items):
            for k, (px, py) in enumerate(peers):
                copy(it, ax, k, 2 * px + py, c).wait_send()
                copy(it, ax, k, 2 * px + py, 1 - c).wait_recv()

    res = _pcall(
        body, name=name, out_shape=tuple(SDS(a.shape, a.dtype) for a in arrs),
        in_specs=[_ANY] * ni, out_specs=tuple([_ANY] * ni), input_output_aliases={i: i for i in range(ni)},
        scratch_shapes=[pltpu.SemaphoreType.DMA((ni * 3,)), pltpu.SemaphoreType.DMA((ni * 3,))],
    )(*arrs)
    return list(res)


def _gblock(ref, ax, chip):
    size = ref.shape[ax] // 4
    piece = pl.ds(pl.multiple_of(chip * size, size), size)
    return ref.at[piece, :] if ax == 0 else ref.at[:, piece]


def _scatter_start(items, after, name):
    ni = len(items)
    arrs = [a for a, _ in items]
    lands = []
    for a, ax in items:
        blk = (a.shape[0] // 4, a.shape[1]) if ax == 0 else (a.shape[0], a.shape[1] // 4)
        lands.append(lax.empty((4, *blk), a.dtype))

    def body(*refs):
        send_sems, recv_sems = refs[2 * ni + 1], refs[2 * ni + 2]
        srcs, dsts = refs[2 * ni + 3:3 * ni + 3], refs[3 * ni + 3:4 * ni + 3]
        token = refs[4 * ni + 3]
        x, y, c, peers = _chip_peers()
        for it, (_, ax) in enumerate(items):
            for k, (px, py) in enumerate(peers):
                pltpu.make_async_remote_copy(
                    src_ref=_gblock(srcs[it], ax, 2 * px + py), dst_ref=dsts[it].at[2 * x + y],
                    send_sem=send_sems.at[it * 3 + k], recv_sem=recv_sems.at[it * 3 + k],
                    device_id=(px, py, c), device_id_type=MESH).start()
        token[...] = jnp.zeros_like(token)

    res = _pcall(
        body, name=name,
        out_shape=(pltpu.SemaphoreType.DMA((ni * 3,)), pltpu.SemaphoreType.DMA((ni * 3,)),
                   *[pltpu.HBM(a.shape, a.dtype) for a in arrs], *[pltpu.HBM(a.shape, a.dtype) for a in lands],
                   SDS((8, LANES), F32)),
        in_specs=[_HBM] * (2 * ni) + [_ANY],
        out_specs=(_SEM, _SEM, *[_HBM] * (2 * ni), pl.BlockSpec(memory_space=pltpu.VMEM)),
        input_output_aliases={i: 2 + i for i in range(2 * ni)}, compiler_params=_SPLIT,
    )(*[_in_hbm(a) for a in arrs], *[_in_hbm(a) for a in lands], after)
    return res[0], res[1], list(res[2:2 + ni]), list(res[2 + ni:2 + 2 * ni]), res[2 + 2 * ni]


def _scatter_wait(items, lands, send_sems, recv_sems, after, name):
    ni = len(items)
    arrs = [a for a, _ in items]

    def body(*refs):
        s_sems, r_sems = refs[2 * ni], refs[2 * ni + 1]
        srcs, dsts = refs[2 * ni + 3:3 * ni + 3], refs[3 * ni + 3:]
        x, y, c, peers = _chip_peers()
        for it, (_, ax) in enumerate(items):
            for k, (px, py) in enumerate(peers):
                cp = pltpu.make_async_remote_copy(
                    src_ref=_gblock(srcs[it], ax, 2 * px + py), dst_ref=dsts[it].at[2 * px + py],
                    send_sem=s_sems.at[it * 3 + k], recv_sem=r_sems.at[it * 3 + k],
                    device_id=(px, py, c), device_id_type=MESH)
                cp.wait_send()
                cp.wait_recv()

    res = _pcall(
        body, name=name, out_shape=tuple(pltpu.HBM(a.shape, a.dtype) for a in arrs + list(lands)),
        in_specs=[_HBM] * (2 * ni) + [_SEM, _SEM, _ANY], out_specs=tuple([_HBM] * (2 * ni)),
        input_output_aliases={i: i for i in range(2 * ni)}, compiler_params=_SPLIT,
    )(*arrs, *lands, send_sems, recv_sems, after)
    return list(res[:ni]), list(res[ni:])


def _tile(n, cap):
    if n <= cap:
        return n
    t = cap - cap % LANES
    while n % t:
        t -= LANES
    return t


def _mm(a, b, mode, *, name, out_dtypes=(F32,), tm=MM_TILE, tn=MM_TILE, tk=MM_TILE, a_fn=None, out_fn=None, aux=(),
        stack=None):
    (a, a_lead), (b, b_lead) = _split(a), _split(b)
    ash, bsh = a.shape[-2:], b.shape[-2:]
    if mode == "nn":
        (M, K), (_, N) = ash, bsh
    elif mode == "nt":
        (M, K), (N, _) = ash, bsh
    else:
        (K, M), (_, N) = ash, bsh
    tm, tn, tk = _tile(M, tm), _tile(N, tn), _tile(K, tk)
    nk = K // tk
    if mode == "tn":
        a_spec = _bs((tk, tm), lambda i, j, k: (k, i), a_lead)
    else:
        a_spec = _bs((tm, tk), lambda i, j, k: (i, k), a_lead)
    if mode == "nt":
        b_spec = _bs((tn, tk), lambda i, j, k: (j, k), b_lead)
    else:
        b_spec = _bs((tk, tn), lambda i, j, k: (k, j), b_lead)
    aux_arrs, aux_specs = [], []
    for arr, kind in aux:
        arr, lead = _split(arr)
        aux_arrs.append(arr)
        if kind == "mn":
            aux_specs.append(_bs((tm, tn), lambda i, j, k: (i, j), lead))
        else:
            aux_specs.append(_bs((1, tn), lambda i, j, k: (0, j), lead))
    na, no = len(aux_arrs), len(out_dtypes)
    dims = {"nn": (((1,), (0,)), ((), ())), "nt": (((1,), (1,)), ((), ())), "tn": (((0,), (0,)), ((), ()))}[mode]

    def finish(r, aux_refs, o_refs):
        outs = out_fn(r, *[x[...] for x in aux_refs]) if out_fn is not None else (r,)
        for o_ref, val in zip(o_refs, outs):
            o_ref[...] = val.astype(o_ref.dtype)

    def product(a_ref, b_ref):
        av = a_ref[...]
        if a_fn is not None:
            av = a_fn(av.astype(F32))
        return lax.dot_general(av.astype(_MXU), b_ref[...].astype(_MXU), dims, preferred_element_type=F32)

    nbuf = 0 if stack is None or stack[0] is None else 1

    def body_one(a_ref, b_ref, *rest):
        finish(product(a_ref, b_ref), rest[:na], rest[na + nbuf:na + nbuf + no])

    def body_acc(a_ref, b_ref, *rest):
        aux_refs, o_refs, acc = rest[:na], rest[na + nbuf:na + nbuf + no], rest[na + nbuf + no]
        k = pl.program_id(2)

        @pl.when(k == 0)
        def _():
            acc[...] = product(a_ref, b_ref)

        @pl.when(k != 0)
        def _():
            acc[...] += product(a_ref, b_ref)

        @pl.when(k == nk - 1)
        def _():
            finish(acc[...], aux_refs, o_refs)

    extra, aliases = {}, []
    if stack is None:
        out_shape = (M, N)
        o_spec = pl.BlockSpec((tm, tn), lambda i, j, k: (i, j))
    else:
        buf, layer, n_layers = stack
        assert no == 1
        out_shape = (n_layers, M, N)
        o_spec = pl.BlockSpec((None, tm, tn), lambda i, j, k: (layer, i, j))
        if buf is not None:
            aliases = [buf]
            extra = dict(input_output_aliases={2 + na: 0})
    outs = _pcall(
        body_one if nk == 1 else body_acc, name=name, grid=(M // tm, N // tn, nk),
        in_specs=[a_spec, b_spec] + aux_specs + [pl.BlockSpec(memory_space=pl.ANY)] * nbuf,
        out_specs=tuple([o_spec] * no),
        out_shape=tuple(SDS(out_shape, dt) for dt in out_dtypes),
        scratch_shapes=[] if nk == 1 else [pltpu.VMEM((tm, tn), F32)],
        compiler_params=pltpu.CompilerParams(dimension_semantics=("parallel", "parallel", "arbitrary"),
                                             vmem_limit_bytes=VMEM_BIG),
        **extra,
    )(a, b, *aux_arrs, *aliases)
    return outs[0] if no == 1 else outs


def _tok(S):
    ts = min(TOKEN_BLOCK, S)
    assert S % ts == 0
    return ts


def _row(ts, D):
    return pl.BlockSpec((ts, D), lambda i: (i, 0))


def _acc_rows(ref, i, rows):
    @pl.when(i == 0)
    def _():
        for r, v in enumerate(rows):
            ref[r:r + 1, :] = v

    @pl.when(i != 0)
    def _():
        for r, v in enumerate(rows):
            ref[r:r + 1, :] += v


def _modulate(x, mod, r_sh, r_sc, name):
    S, D = x.shape
    ts = _tok(S)

    def body(x_ref, m_ref, o_ref):
        o_ref[...] = (x_ref[...] * (1.0 + m_ref[r_sc:r_sc + 1, :]) + m_ref[r_sh:r_sh + 1, :]).astype(o_ref.dtype)

    return _pcall(body, name=name, grid=(S // ts,), in_specs=[_row(ts, D), _full(mod)],
                  out_specs=_row(ts, D), out_shape=SDS((S, D), _MXU))(x, mod)


def _modulate_bwd(x, mod, r_sc, dh, dxa, name):
    S, D = x.shape
    ts = _tok(S)

    def body(x_ref, m_ref, dh_ref, dxa_ref, dx_ref, dss_ref):
        dh_v = dh_ref[...]
        dx_ref[...] = dxa_ref[...] + dh_v * (1.0 + m_ref[r_sc:r_sc + 1, :])
        _acc_rows(dss_ref, pl.program_id(0),
                  [jnp.sum(dh_v, axis=0, keepdims=True), jnp.sum(dh_v * x_ref[...], axis=0, keepdims=True)])

    return _pcall(body, name=name, grid=(S // ts,),
                  in_specs=[_row(ts, D), _full(mod), _row(ts, D), _row(ts, D)],
                  out_specs=(_row(ts, D), pl.BlockSpec((2, D), lambda i: (0, 0))),
                  out_shape=(SDS((S, D), F32), SDS((2, D), F32)))(x, mod, dh, dxa)


def _norm_stats(z):
    mu = jnp.mean(z, axis=-1, keepdims=True)
    zc = z - mu
    var = jnp.mean(zc * zc, axis=-1, keepdims=True)
    rstd = lax.rsqrt(var + LN_EPS)
    return zc * rstd, rstd


def _norm_bwd(dxhat, xhat, rstd):
    return rstd * (dxhat - jnp.mean(dxhat, axis=-1, keepdims=True)
                   - xhat * jnp.mean(dxhat * xhat, axis=-1, keepdims=True))


def _combine(x, y, mod, r_gt, lnp, r_g, name):
    S, D = x.shape
    ts = _tok(S)

    def body(x_ref, y_ref, m_ref, l_ref, o_ref):
        z = ALPHA * x_ref[...] + (1.0 + m_ref[r_gt:r_gt + 1, :]) * y_ref[...]
        xhat, _ = _norm_stats(z)
        o_ref[...] = xhat * l_ref[r_g:r_g + 1, :] + l_ref[r_g + 1:r_g + 2, :]

    return _pcall(body, name=name, grid=(S // ts,), in_specs=[_row(ts, D), _row(ts, D), _full(mod), _full(lnp)],
                  out_specs=_row(ts, D), out_shape=SDS((S, D), F32))(x, y, mod, lnp)


def _combine_bwd(x, y, mod, r_gt, lnp, r_g, dout, name):
    S, D = x.shape
    ts = _tok(S)

    def body(x_ref, y_ref, m_ref, l_ref, do_ref, dxa_ref, dy_ref, dp_ref):
        gate = 1.0 + m_ref[r_gt:r_gt + 1, :]
        y_v, do_v = y_ref[...], do_ref[...]
        xhat, rstd = _norm_stats(ALPHA * x_ref[...] + gate * y_v)
        dz = _norm_bwd(do_v * l_ref[r_g:r_g + 1, :], xhat, rstd)
        dxa_ref[...] = ALPHA * dz
        dy_ref[...] = (gate * dz).astype(dy_ref.dtype)
        _acc_rows(dp_ref, pl.program_id(0),
                  [jnp.sum(dz * y_v, axis=0, keepdims=True), jnp.sum(do_v * xhat, axis=0, keepdims=True),
                   jnp.sum(do_v, axis=0, keepdims=True)])

    return _pcall(body, name=name, grid=(S // ts,),
                  in_specs=[_row(ts, D), _row(ts, D), _full(mod), _full(lnp), _row(ts, D)],
                  out_specs=(_row(ts, D), _row(ts, D), pl.BlockSpec((3, D), lambda i: (0, 0))),
                  out_shape=(SDS((S, D), F32), SDS((S, D), _MXU), SDS((3, D), F32)))(x, y, mod, lnp, dout)


def _modcomb_bwd(later, x, y, mod, r_gt, lnp, r_g, name):
    xn, mod_n, r_sc_n, dh, dxa_n = later
    S, D = x.shape
    ts = _tok(S)

    def body(xn_ref, mn_ref, dh_ref, dxan_ref, x_ref, y_ref, m_ref, l_ref, dxa_ref, dy_ref, dp_ref, dss_ref):
        i = pl.program_id(0)
        dh_v = dh_ref[...]
        do_v = dxan_ref[...] + dh_v * (1.0 + mn_ref[r_sc_n:r_sc_n + 1, :])
        _acc_rows(dss_ref, i, [jnp.sum(dh_v, axis=0, keepdims=True),
                               jnp.sum(dh_v * xn_ref[...], axis=0, keepdims=True)])
        gate = 1.0 + m_ref[r_gt:r_gt + 1, :]
        y_v = y_ref[...]
        xhat, rstd = _norm_stats(ALPHA * x_ref[...] + gate * y_v)
        dz = _norm_bwd(do_v * l_ref[r_g:r_g + 1, :], xhat, rstd)
        dxa_ref[...] = ALPHA * dz
        dy_ref[...] = (gate * dz).astype(dy_ref.dtype)
        _acc_rows(dp_ref, i, [jnp.sum(dz * y_v, axis=0, keepdims=True), jnp.sum(do_v * xhat, axis=0, keepdims=True),
                              jnp.sum(do_v, axis=0, keepdims=True)])

    row = _row(ts, D)
    return _pcall(body, name=name, grid=(S // ts,),
                  in_specs=[row, _full(mod_n), row, row, row, row, _full(mod), _full(lnp)],
                  out_specs=(row, row, pl.BlockSpec((3, D), lambda i: (0, 0)), pl.BlockSpec((2, D), lambda i: (0, 0))),
                  out_shape=(SDS((S, D), F32), SDS((S, D), _MXU), SDS((3, D), F32), SDS((2, D), F32)))(
        xn, mod_n, dh, dxa_n, x, y, mod, lnp)


def _sublayer_head_bwd(dout, x, y, mod, r_gt, lnp, r_g, tag):
    if isinstance(dout, tuple):
        return _modcomb_bwd(dout, x, y, mod, r_gt, lnp, r_g, f"{tag}_ln_b")
    return (*_combine_bwd(x, y, mod, r_gt, lnp, r_g, dout, f"{tag}_ln_b"), None)


def _sigmoid(t):
    return 1.0 / (1.0 + jnp.exp(-t))


def _ln_silu(u, lnp, name):
    S, D = u.shape
    ts = _tok(S)

    def body(u_ref, l_ref, o_ref):
        xhat, _ = _norm_stats(u_ref[...])
        t = xhat * l_ref[0:1, :] + l_ref[1:2, :]
        o_ref[...] = (t * _sigmoid(t)).astype(o_ref.dtype)

    return _pcall(body, name=name, grid=(S // ts,), in_specs=[_row(ts, D), _full(lnp)],
                  out_specs=_row(ts, D), out_shape=SDS((S, D), _MXU))(u, lnp)


def _ln_silu_bwd(u, lnp, dout, name):
    S, D = u.shape
    ts = _tok(S)

    def body(u_ref, l_ref, do_ref, du_ref, dp_ref):
        xhat, rstd = _norm_stats(u_ref[...])
        g = l_ref[0:1, :]
        t = xhat * g + l_ref[1:2, :]
        sg = _sigmoid(t)
        dt = do_ref[...] * (sg * (1.0 + t * (1.0 - sg)))
        du_ref[...] = _norm_bwd(dt * g, xhat, rstd)
        _acc_rows(dp_ref, pl.program_id(0),
                  [jnp.sum(dt * xhat, axis=0, keepdims=True), jnp.sum(dt, axis=0, keepdims=True)])

    return _pcall(body, name=name, grid=(S // ts,), in_specs=[_row(ts, D), _full(lnp), _row(ts, D)],
                  out_specs=(_row(ts, D), pl.BlockSpec((2, D), lambda i: (0, 0))),
                  out_shape=(SDS((S, D), F32), SDS((2, D), F32)))(u, lnp, dout)


def _loss_head(xf, tgt, name):
    S, D = xf.shape
    ts = _tok(S)

    def body(x_ref, t_ref, dx_ref, l_ref):
        err = x_ref[...] - t_ref[...]
        dx_ref[...] = err * (1.0 / D)
        part = jnp.sum(jnp.sum(err * err, axis=1, keepdims=True), axis=0, keepdims=True) * (0.5 / D)

        @pl.when(pl.program_id(0) == 0)
        def _():
            l_ref[...] = part

        @pl.when(pl.program_id(0) != 0)
        def _():
            l_ref[...] += part

    return _pcall(body, name=name, grid=(S // ts,), in_specs=[_row(ts, D), _row(ts, D)],
                  out_specs=(_row(ts, D), pl.BlockSpec((1, 1), lambda i: (0, 0))),
                  out_shape=(SDS((S, D), F32), SDS((1, 1), F32)))(xf, tgt)


CONV_PAD = 32


def _padded(u, before):
    zeros = jnp.zeros((CONV_PAD, u.shape[1]), u.dtype)
    return jnp.concatenate([zeros, u] if before else [u, zeros], axis=0)


SUBLANES = 8


def _shift_down(u_pad, s, rolled):
    q, r = divmod(s, SUBLANES)
    if r not in rolled:
        rolled[r] = u_pad if r == 0 else pltpu.roll(u_pad, r, 0)
    start = CONV_PAD - SUBLANES * q
    return rolled[r][start:start + u_pad.shape[0] - CONV_PAD]


def _shift_up(u_pad, s, rolled):
    n = u_pad.shape[0]
    q, r = divmod(s, SUBLANES)
    if r not in rolled:
        rolled[r] = u_pad if r == 0 else pltpu.roll(u_pad, n - r, 0)
    return rolled[r][SUBLANES * q:SUBLANES * q + n - CONV_PAD]


def _dwconv(u, w_ref, taps, rows):
    del rows
    assert taps - 1 <= CONV_PAD
    u_pad, rolled = _padded(u, True), {}
    acc = jnp.zeros_like(u)
    for j in range(taps):
        acc = acc + w_ref[j:j + 1, :] * _shift_down(u_pad, taps - 1 - j, rolled)
    return acc


def _dwconv_bwd(u, dy, w_ref, dw_ref, taps, rows):
    del rows
    assert taps - 1 <= CONV_PAD
    u_pad, dy_pad, u_rolled, dy_rolled = _padded(u, True), _padded(dy, False), {}, {}
    du = jnp.zeros_like(u)
    for j in range(taps):
        s = taps - 1 - j
        du = du + w_ref[j:j + 1, :] * _shift_up(dy_pad, s, dy_rolled)
        dw_ref[j:j + 1, :] = jnp.sum(dy * _shift_down(u_pad, s, u_rolled), axis=0, keepdims=True)
    return du


def _col(S, j0=0):
    return pl.BlockSpec((S, LANES), lambda j: (0, j + j0))


def _conv_silu(pm, w, nblk, name):
    S = pm.shape[0]
    taps = w.shape[0]

    def body(u_ref, w_ref, o_ref):
        rows = lax.broadcasted_iota(jnp.int32, (S, LANES), 0)
        cv = _dwconv(u_ref[...], w_ref, taps, rows)
        o_ref[...] = cv * _sigmoid(cv)

    return _pcall(body, name=name, grid=(nblk,),
                  in_specs=[_col(S), pl.BlockSpec((taps, LANES), lambda j: (0, j))],
                  out_specs=_col(S), out_shape=SDS((S, nblk * LANES), F32),
                  compiler_params=_cparams(VMEM_BIG))(pm, w)


def _conv_silu_bwd(pm, w, dout, dpm, j0, name):
    S = pm.shape[0]
    taps = w.shape[0]
    nblk = dout.shape[1] // LANES

    def body(u_ref, w_ref, do_ref, dpm_in, du_ref, dw_ref):
        del dpm_in
        rows = lax.broadcasted_iota(jnp.int32, (S, LANES), 0)
        u = u_ref[...]
        cv = _dwconv(u, w_ref, taps, rows)
        sg = _sigmoid(cv)
        dc = do_ref[...] * (sg * (1.0 + cv * (1.0 - sg)))
        du_ref[...] = _dwconv_bwd(u, dc, w_ref, dw_ref, taps, rows)

    return _pcall(body, name=name, grid=(nblk,),
                  in_specs=[_col(S, j0), pl.BlockSpec((taps, LANES), lambda j: (0, j + j0)), _col(S),
                            pl.BlockSpec(memory_space=pl.ANY)],
                  out_specs=(_col(S, j0), pl.BlockSpec((taps, LANES), lambda j: (0, j))),
                  out_shape=(SDS(dpm.shape, F32), SDS((taps, nblk * LANES), F32)),
                  input_output_aliases={3: 0},
                  compiler_params=_cparams(VMEM_BIG))(pm, w, dout, dpm)


def _glu_conv(p, w, bias, name):
    S, C2 = p.shape
    nblk = C2 // 2 // LANES
    taps = w.shape[0]

    def body(v_ref, g_ref, w_ref, b_ref, o_ref):
        rows = lax.broadcasted_iota(jnp.int32, (S, LANES), 0)
        u = v_ref[...] * _sigmoid(g_ref[...])
        o_ref[...] = _dwconv(u, w_ref, taps, rows) + b_ref[...]

    return _pcall(body, name=name, grid=(nblk,),
                  in_specs=[_col(S), _col(S, nblk), pl.BlockSpec((taps, LANES), lambda j: (0, j)),
                            pl.BlockSpec((1, LANES), lambda j: (0, j))],
                  out_specs=_col(S), out_shape=SDS((S, nblk * LANES), F32),
                  compiler_params=_cparams(VMEM_BIG))(p, p, w, bias)


def _glu_conv_bwd(p, w, dout, name):
    S, C2 = p.shape
    nblk = C2 // 2 // LANES
    taps = w.shape[0]

    def body(v_ref, g_ref, w_ref, do_ref, dv_ref, dg_ref, dw_ref, db_ref):
        rows = lax.broadcasted_iota(jnp.int32, (S, LANES), 0)
        val, sg = v_ref[...], _sigmoid(g_ref[...])
        do_v = do_ref[...]
        du = _dwconv_bwd(val * sg, do_v, w_ref, dw_ref, taps, rows)
        dv_ref[...] = du * sg
        dg_ref[...] = du * val * sg * (1.0 - sg)
        db_ref[...] = jnp.sum(do_v, axis=0, keepdims=True)

    dval, dgate, dw, db = _pcall(
        body, name=name, grid=(nblk,),
        in_specs=[_col(S), _col(S, nblk), pl.BlockSpec((taps, LANES), lambda j: (0, j)), _col(S)],
        out_specs=(_col(S), _col(S), pl.BlockSpec((taps, LANES), lambda j: (0, j)),
                   pl.BlockSpec((1, LANES), lambda j: (0, j))),
        out_shape=(SDS((S, C2 // 2), F32), SDS((S, C2 // 2), F32), SDS((taps, C2 // 2), F32), SDS((1, C2 // 2), F32)),
        compiler_params=_cparams(VMEM_BIG))(p, p, w, dout)
    return dval, dgate, dw, db


def _log1p(e):
    u = 1.0 + e
    d = jnp.where(u == 1.0, 1.0, u - 1.0)
    return jnp.where(u == 1.0, e, jnp.log(u) * (e / d))


def _gate_parts(ps, prm, H):
    lane = lax.broadcasted_iota(jnp.int32, ps.shape, 1)
    is_b, is_g = lane < H, (lane >= H) & (lane < 2 * H)
    beta = _sigmoid(ps)
    t = ps + prm[1:2, :]
    sp = jnp.maximum(t, 0.0) + _log1p(jnp.exp(-jnp.abs(t)))
    na = -jnp.exp(prm[0:1, :])
    return is_b, is_g, beta, t, sp, na


def _gates(ps, prm, H, name):
    S = ps.shape[0]
    ts = _tok(S)

    def body(p_ref, r_ref, o_ref):
        is_b, is_g, beta, _, sp, na = _gate_parts(p_ref[...], r_ref[...], H)
        o_ref[...] = jnp.where(is_b, beta, jnp.where(is_g, na * sp, 0.0))

    return _pcall(body, name=name, grid=(S // ts,), in_specs=[_row(ts, LANES), _full(prm)],
                  out_specs=_row(ts, LANES), out_shape=SDS((S, LANES), F32))(ps, prm)


def _gates_bwd(ps, prm, dgates, H, name):
    S = ps.shape[0]
    ts = _tok(S)

    def body(p_ref, r_ref, dg_ref, dp_ref, dr_ref):
        is_b, is_g, beta, t, sp, na = _gate_parts(p_ref[...], r_ref[...], H)
        dg_v = dg_ref[...]
        dsp = jnp.where(is_g, dg_v * na * _sigmoid(t), 0.0)
        dp_ref[...] = jnp.where(is_b, dg_v * beta * (1.0 - beta), dsp)
        _acc_rows(dr_ref, pl.program_id(0),
                  [jnp.sum(jnp.where(is_g, dg_v * na * sp, 0.0), axis=0, keepdims=True),
                   jnp.sum(dsp, axis=0, keepdims=True)])

    return _pcall(body, name=name, grid=(S // ts,), in_specs=[_row(ts, LANES), _full(prm), _row(ts, LANES)],
                  out_specs=(_row(ts, LANES), pl.BlockSpec((2, LANES), lambda i: (0, 0))),
                  out_shape=(SDS((S, LANES), F32), SDS((2, LANES), F32)))(ps, prm, dgates)


_NN = (((2,), (1,)), ((0,), (0,)))
_NT = (((2,), (2,)), ((0,), (0,)))
_TN = (((1,), (1,)), ((0,), (0,)))


def _mdot(a, b, dims):
    return lax.dot_general(a.astype(_MXU), b.astype(_MXU), dims, preferred_element_type=F32)


def _mdot3(a, b, dims):
    ah, bh = a.astype(_MXU), b.astype(_MXU)
    al, bl = a - ah.astype(F32), b - bh.astype(F32)
    return _mdot(ah, bh, dims) + (_mdot(ah, bl, dims) + _mdot(al, bh, dims))


def _rounded_dot(dims, da_dims, db_dims, a_first, prod=_mdot):
    @jax.custom_vjp
    def f(a, b):
        return prod(a, b, dims)

    def fwd(a, b):
        return prod(a, b, dims), (a, b)

    def bwd(res, ct):
        a, b = res
        da = prod(ct, b, da_dims) if a_first[0] else prod(b, ct, da_dims)
        db = prod(ct, a, db_dims) if a_first[1] else prod(a, ct, db_dims)
        return da, db

    f.defvjp(fwd, bwd)
    return f


_mdot_nn = _rounded_dot(_NN, _NT, _TN, (True, False))
_mdot_nt = _rounded_dot(_NT, _NN, _TN, (True, True))
_mdot_tn = _rounded_dot(_TN, _NT, _NN, (False, False))
def _unit_lower_inverse(a):
    C = a.shape[-1]
    ri = lax.broadcasted_iota(jnp.int32, (1, C, C), 1)
    ci = lax.broadcasted_iota(jnp.int32, (1, C, C), 2)
    t_inv = jnp.where(ri == ci, 1.0, 0.0) - a
    p = a
    for _ in range(max(C.bit_length() - 2, 0)):
        p = _mdot3(p, p, _NN)
        t_inv = t_inv + _mdot3(t_inv, p, _NN)
    return t_inv


@jax.custom_vjp
def _known_inverse(a, t_inv):
    del a
    return t_inv


def _known_inverse_fwd(a, t_inv):
    del a
    return t_inv, t_inv


def _known_inverse_bwd(t_inv, ct):
    da = -_mdot3(_mdot3(t_inv, ct, _TN), t_inv, _NT)
    return da, jnp.zeros_like(t_inv)


_known_inverse.defvjp(_known_inverse_fwd, _known_inverse_bwd)


def _head_cols(gates, off, H):
    lane = lax.broadcasted_iota(jnp.int32, gates.shape, 1)
    cols = [jnp.sum(jnp.where(lane == off + h, gates, 0.0), axis=-1, keepdims=True) for h in range(H)]
    return jnp.concatenate([col[None] for col in cols], axis=0)


def _delta_chunk(qr, kr, v, z, gates, nw, s_in, t_known=None):
    H, C, dk = qr.shape
    beta, g = _head_cols(gates, 0, H), _head_cols(gates, H, H)
    q = qr * lax.rsqrt(jnp.sum(qr * qr, axis=-1, keepdims=True) + L2_EPS) * (dk ** -0.5)
    k = kr * lax.rsqrt(jnp.sum(kr * kr, axis=-1, keepdims=True) + L2_EPS)
    ri = lax.broadcasted_iota(jnp.int32, (1, C, C), 1)
    ci = lax.broadcasted_iota(jnp.int32, (1, C, C), 2)
    causal, strict, eye = ri >= ci, ri > ci, ri == ci
    gam_row = jnp.sum(jnp.where(ri <= ci, g, 0.0), axis=1, keepdims=True)
    gam_col = jnp.sum(jnp.where(eye, gam_row, 0.0), axis=-1, keepdims=True)
    g_last = jnp.sum(g, axis=1, keepdims=True)
    decay = jnp.where(causal, jnp.exp(jnp.where(causal, gam_col - gam_row, 0.0)), 0.0)
    kb = k * beta
    a = jnp.where(strict, _mdot_nt(kb, k) * decay, 0.0)
    t_inv = _unit_lower_inverse(a) if t_known is None else _known_inverse(a, t_known)
    eg = jnp.exp(gam_col)
    u = _mdot_nn(t_inv, v * beta)
    w = _mdot_nn(t_inv, kb * eg)
    a_qk = _mdot_nt(q, k) * decay
    v_new = u - _mdot_nn(w, s_in)
    o = _mdot_nn(q * eg, s_in) + _mdot_nn(a_qk, v_new)
    s_out = s_in * jnp.exp(g_last) + _mdot_tn(k * jnp.exp(g_last - gam_col), v_new)
    og = o * lax.rsqrt(jnp.mean(o * o, axis=-1, keepdims=True) + RMS_EPS) * nw * (z * _sigmoid(z))
    return og, s_out, t_inv


def _heads(ref, H, dk):
    return jnp.stack([ref[:, h * dk:(h + 1) * dk].astype(F32) for h in range(H)])


def _put_heads(ref, val, dk):
    for h in range(val.shape[0]):
        ref[:, h * dk:(h + 1) * dk] = val[h].astype(ref.dtype)


def _delta_fwd(qkv, pm, gates, nw, H, name):
    S = qkv.shape[0]
    hd = qkv.shape[1] // 3
    dk = hd // H
    N = S // CHUNK
    blk = lambda off: pl.BlockSpec((CHUNK, hd), lambda n: (n, off))

    def body(q_ref, k_ref, v_ref, z_ref, g_ref, nw_ref, og_ref, st_ref, ti_ref, s_scr):
        @pl.when(pl.program_id(0) == 0)
        def _():
            s_scr[...] = jnp.zeros_like(s_scr)

        s_in = s_scr[...]
        st_ref[...] = s_in
        og, s_out, t_inv = _delta_chunk(_heads(q_ref, H, dk), _heads(k_ref, H, dk), _heads(v_ref, H, dk),
                                        _heads(z_ref, H, dk), g_ref[...], nw_ref[...], s_in)
        _put_heads(og_ref, og, dk)
        ti_ref[...] = t_inv
        s_scr[...] = s_out

    return _pcall(
        body, name=name, grid=(N,),
        in_specs=[blk(0), blk(1), blk(2), blk(3), pl.BlockSpec((CHUNK, LANES), lambda n: (n, 0)), _full(nw)],
        out_specs=(blk(0), pl.BlockSpec((None, H, dk, dk), lambda n: (n, 0, 0, 0)),
                   pl.BlockSpec((None, H, CHUNK, CHUNK), lambda n: (n, 0, 0, 0))),
        out_shape=(SDS((S, hd), _MXU), SDS((N, H, dk, dk), F32), SDS((N, H, CHUNK, CHUNK), F32)),
        scratch_shapes=[pltpu.VMEM((H, dk, dk), F32)],
        compiler_params=_cparams(VMEM_BIG),
    )(qkv, qkv, qkv, pm, gates, nw)


def _delta_bwd(qkv, pm, gates, nw, states, t_invs, dog, H, name):
    S = qkv.shape[0]
    hd = qkv.shape[1] // 3
    dk = hd // H
    N = S // CHUNK
    blk = lambda off: pl.BlockSpec((CHUNK, hd), lambda n: (N - 1 - n, off))
    gspec = pl.BlockSpec((CHUNK, LANES), lambda n: (N - 1 - n, 0))

    def body(q_ref, k_ref, v_ref, z_ref, g_ref, nw_ref, st_ref, ti_ref, do_ref,
             dq_ref, dk_ref, dv_ref, dz_ref, dg_ref, dnw_ref, ds_scr):
        n = pl.program_id(0)

        @pl.when(n == 0)
        def _():
            ds_scr[...] = jnp.zeros_like(ds_scr)

        t_known = ti_ref[...]
        fn = functools.partial(_delta_chunk, t_known=t_known)
        _, vjp = jax.vjp(fn, _heads(q_ref, H, dk), _heads(k_ref, H, dk), _heads(v_ref, H, dk),
                         _heads(z_ref, H, dk), g_ref[...], nw_ref[...], st_ref[...])
        dq, dkk, dv, dz, dg, dnw, ds_in = vjp((_heads(do_ref, H, dk), ds_scr[...], jnp.zeros_like(t_known)))
        _put_heads(dq_ref, dq, dk)
        _put_heads(dk_ref, dkk, dk)
        _put_heads(dv_ref, dv, dk)
        _put_heads(dz_ref, dz, dk)
        ds_scr[...] = ds_in
        dg_ref[...] = dg

        @pl.when(n == 0)
        def _():
            dnw_ref[...] = dnw

        @pl.when(n != 0)
        def _():
            dnw_ref[...] += dnw

    return _pcall(
        body, name=name, grid=(N,),
        in_specs=[blk(0), blk(1), blk(2), blk(3), gspec, _full(nw),
                  pl.BlockSpec((None, H, dk, dk), lambda n: (N - 1 - n, 0, 0, 0)),
                  pl.BlockSpec((None, H, CHUNK, CHUNK), lambda n: (N - 1 - n, 0, 0, 0)), blk(0)],
        out_specs=(blk(0), blk(0), blk(0), blk(3), gspec, pl.BlockSpec((1, dk), lambda n: (0, 0))),
        out_shape=(SDS((S, hd), F32), SDS((S, hd), F32), SDS((S, hd), F32), SDS(pm.shape, F32),
                   SDS((S, LANES), F32), SDS((1, dk), F32)),
        scratch_shapes=[pltpu.VMEM((H, dk, dk), F32)],
        compiler_params=_cparams(VMEM_BIG),
    )(qkv, qkv, qkv, pm, gates, nw, states, t_invs, dog)


def _rows_block(R, C):
    rb = R
    while rb * C * 4 > (1 << 20) and rb % 16 == 0:
        rb //= 2
    return rb


def _sum_slots(st, name, out_dtype=F32):
    n, R, C = st.shape
    rb = _rows_block(R, C)

    def body(s_ref, o_ref):
        acc = s_ref[0].astype(F32)
        for q in range(1, n):
            acc = acc + s_ref[q].astype(F32)
        o_ref[...] = acc.astype(o_ref.dtype)

    return _pcall(body, name=name, grid=(R // rb,), in_specs=[pl.BlockSpec((n, rb, C), lambda i: (0, i, 0))],
                  out_specs=pl.BlockSpec((rb, C), lambda i: (i, 0)), out_shape=SDS((R, C), out_dtype))(st)


def _scalar(v):
    return jnp.reshape(v, (1,)).astype(jnp.int32)


def _place_block(w, layer, ax, chip, dep, name):
    _, R, C = w.shape
    rb = _rows_block(R, C)
    nrb = R // rb
    shp = [R, C]
    shp[ax] *= 4
    omap = (lambda i, c: (c[0] * nrb + i, 0)) if ax == 0 else (lambda i, c: (i, c[0]))

    def body(c_ref, w_ref, dep_ref, o_ref):
        del c_ref, dep_ref
        o_ref[...] = w_ref[...].astype(o_ref.dtype)

    grid_spec = pltpu.PrefetchScalarGridSpec(
        num_scalar_prefetch=1, grid=(nrb,),
        in_specs=[pl.BlockSpec((None, rb, C), lambda i, c: (layer, i, 0)), _ANY],
        out_specs=pl.BlockSpec((rb, C), omap))
    return _pcall(body, name=name, grid_spec=grid_spec, out_shape=SDS(tuple(shp), _MXU))(_scalar(chip), w, dep)


def _sum_landed(grad, ax, land, chip, name):
    _, R, Cb = land.shape
    rb = _rows_block(R, Cb)
    nrb = R // rb
    if ax == 0:
        own_spec = pl.BlockSpec((rb, Cb), lambda i, c: (c[0] * nrb + i, 0))
    else:
        own_spec = pl.BlockSpec((rb, Cb), lambda i, c: (i, c[0]))
    slot = lambda d: pl.BlockSpec((None, rb, Cb), lambda i, c: ((c[0] + d) % 4, i, 0))

    def body(c_ref, own_ref, r1, r2, r3, o_ref):
        del c_ref
        o_ref[...] = ((own_ref[...].astype(F32) + r1[...].astype(F32)) + r2[...].astype(F32)) + r3[...].astype(F32)

    grid_spec = pltpu.PrefetchScalarGridSpec(
        num_scalar_prefetch=1, grid=(R // rb,), in_specs=[own_spec, slot(1), slot(2), slot(3)],
        out_specs=pl.BlockSpec((rb, Cb), lambda i, c: (i, 0)))
    return _pcall(body, name=name, grid_spec=grid_spec, out_shape=SDS((R, Cb), F32))(
        _scalar(chip), grad, land, land, land)


def _adamw_step(g, w_ref, m_ref, v_ref, g_ref, d_ref, mo_ref, vo_ref):
    m_new = ADAM_B1 * m_ref[...] + (1.0 - ADAM_B1) * g
    v_new = ADAM_B2 * v_ref[...] + (1.0 - ADAM_B2) * (g * g)
    m_hat = m_new / (1.0 - ADAM_B1 ** ADAM_STEP)
    v_hat = v_new / (1.0 - ADAM_B2 ** ADAM_STEP)
    g_ref[...] = g
    d_ref[...] = -ADAM_LR * (m_hat / (jnp.sqrt(v_hat) + ADAM_EPS) + ADAM_WD * w_ref[...])
    mo_ref[...] = m_new
    vo_ref[...] = v_new


def _adamw(w, m, v, st, name):
    R, C = w.shape
    n = st.shape[0]
    rb = _rows_block(R, C)
    spec = pl.BlockSpec((rb, C), lambda i: (i, 0))

    def body(w_ref, m_ref, v_ref, s_ref, *o_refs):
        g = s_ref[0]
        for q in range(1, n):
            g = g + s_ref[q]
        _adamw_step(g, w_ref, m_ref, v_ref, *o_refs)

    return _pcall(body, name=name, grid=(R // rb,),
                  in_specs=[spec, spec, spec, pl.BlockSpec((n, rb, C), lambda i: (0, i, 0))],
                  out_specs=(spec,) * 4, out_shape=(SDS((R, C), F32),) * 4)(w, m, v, st)


def _adamw_pair(w, m, v, layer, own, recv2, ic, bufs, dep, name):
    L, R, C = w.shape
    rb = _rows_block(R, C)
    spec = pl.BlockSpec((None, rb, C), lambda i, c: (layer, i, 0))

    def body(c_ref, w_ref, m_ref, v_ref, own_ref, recv_ref, *rest):
        del c_ref
        _adamw_step(own_ref[...] + recv_ref[...], w_ref, m_ref, v_ref, *rest[-4:])

    nbuf = 0 if bufs is None else 4
    grid_spec = pltpu.PrefetchScalarGridSpec(
        num_scalar_prefetch=1, grid=(R // rb,),
        in_specs=[spec, spec, spec, pl.BlockSpec((rb, C), lambda i, c: (i, 0)),
                  pl.BlockSpec((None, rb, C), lambda i, c: (1 - c[0], i, 0))] + [_ANY] * (nbuf + 1),
        out_specs=(spec,) * 4)
    extra = {} if bufs is None else dict(input_output_aliases={6 + q: q for q in range(4)})
    return _pcall(body, name=name, grid_spec=grid_spec, out_shape=(SDS((L, R, C), F32),) * 4, **extra)(
        _scalar(ic), w, m, v, own, recv2, *([] if bufs is None else bufs), dep)


def _pack(arrs, rows=1):
    flat = jnp.concatenate([a.reshape(-1).astype(F32) for a in arrs])
    quantum = rows * LANES
    pad = (-flat.shape[0]) % quantum
    flat = jnp.pad(flat, (0, pad))
    return flat.reshape(rows, -1)


def _unpack(flat, shapes):
    flat = flat.reshape(-1)
    out, off = [], 0
    for shp in shapes:
        size = 1
        for d in shp:
            size *= d
        out.append(flat[off:off + size].reshape(shp))
        off += size
    return out


def _mlp_fwd(x1, mod, lnp, w1, w2, tag):
    h2 = _modulate(x1, mod, 3, 4, f"{tag}_mod")
    a1, a2 = _mm(h2, w1, "nn", name=f"{tag}_up", out_dtypes=(_MXU, _MXU),
                 out_fn=lambda r: (r, jnp.square(jnp.maximum(r, 0.0))))
    y2 = _mm(a2, w2, "nn", name=f"{tag}_down")
    x2 = _combine(x1, y2, mod, 5, lnp, 2, f"{tag}_ln")
    return x2, (x1, h2, a1, a2, y2)


def _weight_grad(grads, key, a, b, name):
    grads[key] = _mm(a, b, "tn", name=name, out_dtypes=(_MXU,))


def _mlp_bwd(dx2, saved, mod, lnp, w1, w2, tag, stacks):
    x1, h2, a1, a2, y2 = saved
    dxa, dy2, dp, dss_later = _sublayer_head_bwd(dx2, x1, y2, mod, 5, lnp, 2, tag)
    da1 = _mm(dy2, w2, "nt", name=f"{tag}_down_bx", out_dtypes=(_MXU,), aux=[(a1, "mn")],
              out_fn=lambda r, a: (r * (2.0 * jnp.maximum(a.astype(F32), 0.0)),))
    _weight_grad(stacks, "ff_w2", a2, dy2, f"{tag}_down_bw")
    _weight_grad(stacks, "ff_w1", h2, da1, f"{tag}_up_bw")
    dh2 = _mm(da1, w1, "nt", name=f"{tag}_up_bx")
    return (x1, mod, 4, dh2, dxa), (dss_later, dp)


def _dn_fwd(x, mod, lnp, wts, H, tag):
    w_main, w_small, conv_w, prm, nw, w_out = wts
    h = _modulate(x, mod, 0, 1, f"{tag}_mod")
    pm = _mm(h, w_main, "nn", name=f"{tag}_in")
    ps = _mm(h, w_small, "nn", name=f"{tag}_in_s")
    nqkv = conv_w.shape[1] // LANES
    qkv = _conv_silu(pm, conv_w, nqkv, f"{tag}_conv")
    gates = _gates(ps, prm, H, f"{tag}_gates")
    og, *states = _delta_fwd(qkv, pm, gates, nw, H, f"{tag}_delta")
    y = _mm(og, w_out, "nn", name=f"{tag}_out")
    x1 = _combine(x, y, mod, 2, lnp, 0, f"{tag}_ln")
    return x1, (x, h, pm, ps, qkv, gates, states, og, y)


def _dn_bwd(dx1, saved, mod, lnp, wts, H, tag, stacks):
    w_main, w_small, conv_w, prm, nw, w_out = wts
    x, h, pm, ps, qkv, gates, states, og, y = saved
    dxa, dy, dp, dss_later = _sublayer_head_bwd(dx1, x, y, mod, 2, lnp, 0, tag)
    dog = _mm(dy, w_out, "nt", name=f"{tag}_out_bx")
    _weight_grad(stacks, "dn_w_out", og, dy, f"{tag}_out_bw")
    dq, dk, dv, dpm, dgates, dnw = _delta_bwd(qkv, pm, gates, nw, *states, dog, H, f"{tag}_delta_b")
    dps, dprm = _gates_bwd(ps, prm, dgates, H, f"{tag}_gates_b")
    dcw = []
    nb = dq.shape[1] // LANES
    for part, dpart in enumerate((dq, dk, dv)):
        dpm, dcw_p = _conv_silu_bwd(pm, conv_w, dpart, dpm, part * nb, f"{tag}_conv_b{part}")
        dcw.append(dcw_p)
    dconv_w = jnp.concatenate(dcw, axis=1)
    dw_main = _mm(h, dpm, "tn", name=f"{tag}_in_bw", out_dtypes=(_MXU,))
    dw_small = _mm(h, dps, "tn", name=f"{tag}_in_s_bw", out_dtypes=(_MXU,))
    dh_s = _mm(dps, w_small, "nt", name=f"{tag}_in_s_bx")
    dh = _mm(dpm, w_main, "nt", name=f"{tag}_in_bx", aux=[(dh_s, "mn")], out_fn=lambda r, e: (r + e,))
    return (x, mod, 1, dh, dxa), (dw_main, dw_small, dconv_w, dprm, dnw), (dss_later, dp)


def _cf_fwd(x, mod, lnp, wts, tag):
    w_in, dw_w, dw_b, cln, w_out = wts
    h = _modulate(x, mod, 0, 1, f"{tag}_mod")
    p = _mm(h, w_in, "nn", name=f"{tag}_in")
    u2 = _glu_conv(p, dw_w, dw_b, f"{tag}_conv")
    u3 = _ln_silu(u2, cln, f"{tag}_cln")
    y = _mm(u3, w_out, "nn", name=f"{tag}_out")
    x1 = _combine(x, y, mod, 2, lnp, 0, f"{tag}_ln")
    return x1, (x, h, p, u2, u3, y)


def _cf_bwd(dx1, saved, mod, lnp, wts, tag, stacks):
    w_in, dw_w, dw_b, cln, w_out = wts
    x, h, p, u2, u3, y = saved
    dxa, dy, dp, dss_later = _sublayer_head_bwd(dx1, x, y, mod, 2, lnp, 0, tag)
    du3 = _mm(dy, w_out, "nt", name=f"{tag}_out_bx")
    _weight_grad(stacks, "cf_w_out", u3, dy, f"{tag}_out_bw")
    du2, dcln = _ln_silu_bwd(u2, cln, du3, f"{tag}_cln_b")
    dval, dgate, ddw_w, ddw_b = _glu_conv_bwd(p, dw_w, du2, f"{tag}_conv_b")
    dpp = jnp.concatenate([dval, dgate], axis=1)
    _weight_grad(stacks, "cf_w_in", h, dpp, f"{tag}_in_bw")
    dh = _mm(dpp, w_in, "nt", name=f"{tag}_in_bx")
    return (x, mod, 1, dh, dxa), (ddw_w, ddw_b, dcln), (dss_later, dp)


def _two_d(a):
    return a.reshape(-1, a.shape[-1])


def kernel(x, c, ada_w, ada_b, ln_g, ln_b, dn_w_in, dn_conv_w, dn_a_log, dn_dt_bias, dn_norm_w, dn_w_out, cf_w_in, cf_dw_w, cf_dw_b, cf_ln_g, cf_ln_b, cf_w_out, ff_w1, ff_w2, loss_target, m_ada_w, m_ada_b, m_ln_g, m_ln_b, m_dn_w_in, m_dn_conv_w, m_dn_a_log, m_dn_dt_bias, m_dn_norm_w, m_dn_w_out, m_cf_w_in, m_cf_dw_w, m_cf_dw_b, m_cf_ln_g, m_cf_ln_b, m_cf_w_out, m_ff_w1, m_ff_w2, v_ada_w, v_ada_b, v_ln_g, v_ln_b, v_dn_w_in, v_dn_conv_w, v_dn_a_log, v_dn_dt_bias, v_dn_norm_w, v_dn_w_out, v_cf_w_in, v_cf_dw_w, v_cf_dw_b, v_cf_ln_g, v_cf_ln_b, v_cf_w_out, v_ff_w1, v_ff_w2):
    ix, iy, ic = lax.axis_index("x"), lax.axis_index("y"), lax.axis_index("c")
    chip = 2 * ix + iy
    dev = 4 * ix + 2 * iy + ic
    S, D = x.shape[1], x.shape[2]
    L = ada_w.shape[0]
    LA, LB = dn_w_in.shape[0], cf_w_in.shape[0]
    H = dn_a_log.shape[1]
    NMOD = ada_b.shape[1] // D
    dn_in = dn_w_in.shape[2] * 4
    n_main = dn_in - 2 * H
    assert L == N_LAYERS and 2 * H <= LANES
    x0, tgt = x[0], loss_target[0]

    small_sharded = [ln_g, ln_b, dn_conv_w, cf_dw_w, cf_dw_b, cf_ln_g, cf_ln_b]
    small_axes = [2, 2, 2, 2, 1, 1, 1]
    packed_small = _pack(small_sharded, rows=8)[None]
    big = {"dn_w_in": (dn_w_in, m_dn_w_in, v_dn_w_in, 0), "dn_w_out": (dn_w_out, m_dn_w_out, v_dn_w_out, 0),
           "cf_w_in": (cf_w_in, m_cf_w_in, v_cf_w_in, 1), "cf_w_out": (cf_w_out, m_cf_w_out, v_cf_w_out, 0),
           "ff_w1": (ff_w1, m_ff_w1, v_ff_w1, 1), "ff_w2": (ff_w2, m_ff_w2, v_ff_w2, 0)}

    def group(g):
        l = g // 2
        if g % 2:
            return {"ff_w1": l, "ff_w2": l}
        mixer = ("dn_w_in", "dn_w_out") if l % 2 == 0 else ("cf_w_in", "cf_w_out")
        return {mixer[0]: l // 2, mixer[1]: l // 2}

    def place(l, dep):
        out = []
        for nm, lw in group(l).items():
            out.append(_place_block(big[nm][0], lw, big[nm][3], chip, dep, f"l{l}_place_{nm}"))
            dep = out[-1]
        return out

    def start_gather(l, placed, after):
        names = list(group(l))
        axes = [big[nm][3] for nm in names]
        send, recv, arrs, token = _gather_start(list(zip(placed, axes)), after, f"l{l}_gather_start")
        return names, axes, arrs, send, recv, token

    def finish_gather(l, pending, after):
        names, axes, arrs, send, recv, _ = pending
        arrs = _gather_wait(list(zip(arrs, axes)), send, recv, after, f"l{l}_gather_wait")
        arrs = _gather_forward(list(zip(arrs, axes)), f"l{l}_gather_pass")
        return dict(zip(names, arrs))

    pending = start_gather(0, place(0, x0), x0)
    placed = [None]
    for g in range(1, 2 * L):
        placed.append(place(g, pending[5] if g == 1 else placed[-1][-1]))
    first_wl = finish_gather(0, pending, placed[-1][-1])
    landed = jnp.minimum(jnp.abs(next(iter(first_wl.values()))[0, 0].astype(F32)), 0.0)

    g_small = _exchange([(packed_small + landed, 0)], "xy", "gather", "gather_small")[0]
    shard_shapes = [a.shape for a in small_sharded]
    per_chip = [_unpack(g_small[q], shard_shapes) for q in range(4)]
    ln_g_f, ln_b_f, conv_w_f, dw_w_f, dw_b_f, cln_g_f, cln_b_f = [
        jnp.concatenate([per_chip[q][i] for q in range(4)], axis=small_axes[i]) for i in range(len(small_sharded))]

    c_all = _exchange([(c[None] + landed, 0)], "all", "gather", "gather_cond")[0].reshape(8, D)
    c_pad = jnp.pad(c_all, ((0, 8), (0, 0)))
    mod_sh = jnp.stack([_mm(c_pad, (ada_w, l), "nn", name=f"ada_{l}", a_fn=lambda t: t * _sigmoid(t))
                        for l in range(L)])
    mod_all = _exchange([(mod_sh, 2)], "xy", "gather", "gather_mod")[0]
    mod_mine = lax.dynamic_index_in_dim(mod_all, dev, axis=1, keepdims=False) + ada_b
    mods = mod_mine.reshape(L, NMOD, D)

    def lnp_of(l):
        return jnp.stack([ln_g_f[l, 0], ln_b_f[l, 0], ln_g_f[l, 1], ln_b_f[l, 1]])

    def mixer_wts(l, wl):
        j = l // 2
        if l % 2 == 0:
            w_in = jnp.transpose(wl["dn_w_in"].reshape(4, D, dn_in // 4), (1, 0, 2)).reshape(D, dn_in)
            w_small = jnp.pad(w_in[:, n_main:], ((0, 0), (0, LANES - 2 * H)))
            prm = jnp.zeros((2, LANES), F32).at[0, H:2 * H].set(dn_a_log[j]).at[1, H:2 * H].set(dn_dt_bias[j])
            return (w_in[:, :n_main], w_small, conv_w_f[j], prm, dn_norm_w[j][None], wl["dn_w_out"])
        return (wl["cf_w_in"], dw_w_f[j], dw_b_f[j][None], jnp.stack([cln_g_f[j], cln_b_f[j]]), wl["cf_w_out"])

    xs = x0
    saved, wts, mod_of = [], [], []
    for g in range(2 * L):
        l = g // 2
        wl = finish_gather(g, pending, xs) if g else first_wl
        mod_g = mods[l]
        if g + 1 < 2 * L:
            pending = start_gather(g + 1, placed[g + 1], next(iter(wl.values())))
            mod_g = mod_g + pending[5][0, 0]
        mod_of.append(mod_g)
        if g % 2:
            wts.append((wl["ff_w1"], wl["ff_w2"]))
            xs, sv = _mlp_fwd(xs, mod_g, lnp_of(l), *wts[g], f"l{l}_ff")
        elif l % 2 == 0:
            wts.append(mixer_wts(l, wl))
            xs, sv = _dn_fwd(xs, mod_g, lnp_of(l), wts[g], H, f"l{l}_dn")
        else:
            wts.append(mixer_wts(l, wl))
            xs, sv = _cf_fwd(xs, mod_g, lnp_of(l), wts[g], f"l{l}_cf")
        saved.append(sv)
    dx, loss_local = _loss_head(xs, tgt, "loss_head")
    loss = lax.psum(loss_local[0, 0], ("x", "y", "c"))

    def start_scatter(l, grads):
        names = list(group(l))
        axes = [big[nm][3] for nm in names]
        send, recv, sums, lands, token = _scatter_start([(grads[nm], ax) for nm, ax in zip(names, axes)], mods,
                                                        f"l{l}_scatter_start")
        return names, axes, sums, lands, send, recv, token

    def finish_scatter(l, pending, after):
        names, axes, sums, lands, send, recv, _ = pending
        sums, lands = _scatter_wait(list(zip(sums, axes)), lands, send, recv, after, f"l{l}_scatter_wait")
        mine = [_sum_landed(s, ax, ld, chip, f"l{l}_sum_grads_{nm}") for nm, ax, s, ld in zip(names, axes, sums, lands)]
        other = _exchange([(s[None], 0) for s in mine], "c", "gather", f"l{l}_swap_sums", keep_own=False)
        return {nm: (s, o) for nm, s, o in zip(names, mine, other)}

    g_dn = [None] * LA
    g_cf = [None] * LB
    dmods, dlns = [None] * L, [None] * L
    big_sums = {}
    pending, token = None, None
    dss_of, dp_of = {}, {}
    for g in reversed(range(2 * L)):
        l, j = g // 2, g // 4
        mod_g = mod_of[g] if token is None else mod_of[g] + token[0, 0]
        grads = {}
        if g % 2:
            dx, (dss_of[g + 1], dp_of[g]) = _mlp_bwd(dx, saved[g], mod_g, lnp_of(l), *wts[g], f"l{l}_ff", grads)
        elif l % 2 == 0:
            dx, g_dn[j], (dss_of[g + 1], dp_of[g]) = _dn_bwd(dx, saved[g], mod_g, lnp_of(l), wts[g], H,
                                                              f"l{l}_dn", grads)
            dn_in_g = jnp.concatenate([g_dn[j][0], g_dn[j][1][:, :2 * H]], axis=1)
            grads["dn_w_in"] = jnp.transpose(dn_in_g.reshape(D, 4, dn_in // 4), (1, 0, 2)).reshape(4 * D, dn_in // 4)
        else:
            dx, g_cf[j], (dss_of[g + 1], dp_of[g]) = _cf_bwd(dx, saved[g], mod_g, lnp_of(l), wts[g],
                                                              f"l{l}_cf", grads)
        if pending is not None:
            for nm, pair_of in finish_scatter(g + 1, pending, dx[3]).items():
                big_sums[nm, group(g + 1)[nm]] = pair_of
        pending = start_scatter(g, grads)
        token = pending[6]
    x_in, mod_in, r_sc_in, dh_in, dxa_in = dx
    dx, dss_of[0] = _modulate_bwd(x_in, mod_in, r_sc_in, dh_in, dxa_in, "l0_dn_mod_b")
    big_names = ["dn_w_in", "dn_w_out", "cf_w_in", "cf_w_out", "ff_w1", "ff_w2"]
    big_res = {nm: None for nm in big_names}

    def update(nm, layer):
        w, m, v, _ = big[nm]
        own, oth = big_sums[nm, layer]
        big_res[nm] = _adamw_pair(w, m, v, layer, own, oth, ic, big_res[nm], token, f"adamw_{nm}_{layer}")

    for nm in big_names:
        for layer in range(big[nm][0].shape[0]):
            if group(0).get(nm) != layer:
                update(nm, layer)
    for nm, pair_of in finish_scatter(0, pending, big_res["ff_w2"][0]).items():
        big_sums[nm, group(0)[nm]] = pair_of
    for nm, layer in group(0).items():
        update(nm, layer)
    big_out = [big_res[nm] for nm in big_names]
    for l in range(L):
        (dss1, dp1), (dss2, dp2) = (dss_of[2 * l], dp_of[2 * l]), (dss_of[2 * l + 1], dp_of[2 * l + 1])
        dmods[l] = jnp.concatenate([dss1, dp1[0:1], dss2, dp2[0:1]], axis=0)
        dlns[l] = (jnp.stack([dp1[1], dp2[1]]), jnp.stack([dp1[2], dp2[2]]))
    grad_x = dx[None]

    d_ln_g = jnp.stack([dlns[l][0] for l in range(L)])
    d_ln_b = jnp.stack([dlns[l][1] for l in range(L)])
    d_conv_w = jnp.stack([g_dn[j][2] for j in range(LA)])
    d_a_log = jnp.stack([g_dn[j][3][0, H:2 * H] for j in range(LA)])
    d_dt_bias = jnp.stack([g_dn[j][3][1, H:2 * H] for j in range(LA)])
    d_norm_w = jnp.stack([g_dn[j][4][0] for j in range(LA)])
    d_dw_w = jnp.stack([g_cf[j][0] for j in range(LB)])
    d_dw_b = jnp.stack([g_cf[j][1][0] for j in range(LB)])
    d_cln_g = jnp.stack([g_cf[j][2][0] for j in range(LB)])
    d_cln_b = jnp.stack([g_cf[j][2][1] for j in range(LB)])
    d_mod = jnp.stack(dmods).reshape(L, NMOD * D)
    small_full = [d_mod, d_ln_g, d_ln_b, d_conv_w, d_dw_w, d_dw_b, d_cln_g, d_cln_b, d_a_log, d_dt_bias, d_norm_w]
    landed = jnp.minimum(jnp.abs(big_sums["dn_w_out", 0][0][0, 0]), 0.0)
    small_all = _exchange([(_pack(small_full, rows=8)[None] + landed, 0)], "all", "gather", "gather_small_grads")[0]
    small_sum = _sum_slots(small_all, "sum_small_grads")
    (s_ada_b, s_ln_g, s_ln_b, s_conv_w, s_dw_w, s_dw_b, s_cln_g, s_cln_b, s_a_log, s_dt_bias, s_norm_w) = _unpack(
        small_sum, [a.shape for a in small_full])
    d_mod_all = small_all.reshape(8, -1)[:, :L * NMOD * D].reshape(8, L, NMOD * D)

    def shard(a, axis):
        size = a.shape[axis] // 4
        return lax.dynamic_slice_in_dim(a, chip * size, size, axis)

    ncol = ada_w.shape[2]
    d_mod_sh = jnp.pad(lax.dynamic_slice_in_dim(d_mod_all, chip * ncol, ncol, 2), ((0, 8), (0, 0), (0, 0)))
    g_ada_w = None
    for l in range(L):
        g_ada_w = _mm(c_pad, d_mod_sh[:, l], "tn", name=f"ada_bw_{l}", a_fn=lambda t: t * _sigmoid(t),
                      stack=(g_ada_w, l, L))

    ada_out =[r.reshape(ada_w.shape) for r in _adamw(_two_d(ada_w), _two_d(m_ada_w), _two_d(v_ada_w),
                                                     _two_d(g_ada_w)[None], "adamw_ada_w")]

    small_w = [(ada_b, m_ada_b, v_ada_b, s_ada_b), (ln_g, m_ln_g, v_ln_g, shard(s_ln_g, 2)),
               (ln_b, m_ln_b, v_ln_b, shard(s_ln_b, 2)), (dn_conv_w, m_dn_conv_w, v_dn_conv_w, shard(s_conv_w, 2)),
               (dn_a_log, m_dn_a_log, v_dn_a_log, s_a_log), (dn_dt_bias, m_dn_dt_bias, v_dn_dt_bias, s_dt_bias),
               (dn_norm_w, m_dn_norm_w, v_dn_norm_w, s_norm_w), (cf_dw_w, m_cf_dw_w, v_cf_dw_w, shard(s_dw_w, 2)),
               (cf_dw_b, m_cf_dw_b, v_cf_dw_b, shard(s_dw_b, 1)), (cf_ln_g, m_cf_ln_g, v_cf_ln_g, shard(s_cln_g, 1)),
               (cf_ln_b, m_cf_ln_b, v_cf_ln_b, shard(s_cln_b, 1))]
    pk = [_pack([t[i] for t in small_w], rows=8) for i in range(4)]
    small_res = _adamw(pk[0], pk[1], pk[2], pk[3][None], "adamw_small")
    small_shapes = [t[0].shape for t in small_w]
    small_out = [_unpack(r, small_shapes) for r in small_res]

    def kind(k):
        sm = small_out[k]
        bg = [o[k] for o in big_out]
        return [ada_out[k], sm[0], sm[1], sm[2], bg[0], sm[3], sm[4], sm[5], sm[6], bg[1],
                bg[2], sm[7], sm[8], sm[9], sm[10], bg[3], bg[4], bg[5]]

    return (loss, grad_x, *kind(0), *kind(1), *kind(2), *kind(3))
```

```python
import functools

import jax
import jax.numpy as jnp
from jax import lax
from jax.experimental import pallas as pl
from jax.experimental.pallas import tpu as pltpu

F32 = jnp.float32
_MXU = jnp.bfloat16
_HI = lax.Precision.HIGHEST

N_LAYERS = 4
ALPHA = (2.0 * N_LAYERS) ** 0.25
LN_EPS = 1e-5
RMS_EPS = 1e-6
L2_EPS = 1e-6
CHUNK = 64
ADAM_LR, ADAM_B1, ADAM_B2, ADAM_EPS, ADAM_WD, ADAM_STEP = 0.001, 0.9, 0.999, 1e-08, 0.01, 10

LANES = 128
TOKEN_BLOCK = 256
VMEM_BIG = 48 * 1024 * 1024
MM_TILE = 1024

SDS = jax.ShapeDtypeStruct
MESH = pl.DeviceIdType.MESH


def _cparams(vmem=None):
    if vmem is None:
        return None
    return pltpu.CompilerParams(vmem_limit_bytes=vmem)


def _pcall(body, **kw):
    if kw.get("compiler_params", 1) is None:
        kw.pop("compiler_params")
    return pl.pallas_call(body, **kw)


def _full(arr):
    nd = arr.ndim
    return pl.BlockSpec(arr.shape, lambda *g: (0,) * nd)


def _bs(block, imap, lead=None):
    if lead is None:
        return pl.BlockSpec(block, imap)
    return pl.BlockSpec((None,) + tuple(block), lambda *g: (lead,) + tuple(imap(*g)))


def _split(a):
    return a if isinstance(a, tuple) else (a, None)


_GROUPS = {
    "xy": ([(1, 0, 0), (0, 1, 0), (1, 1, 0)], 4),
    "c": ([(0, 0, 1)], 2),
    "all": ([(1, 0, 0), (0, 1, 0), (1, 1, 0), (0, 0, 1), (1, 0, 1), (0, 1, 1), (1, 1, 1)], 8),
}


def _exchange(items, group, mode, name, nsplit=1, keep_own=True):
    masks, n = _GROUPS[group]
    npeer = len(masks)
    ni = len(items)
    arrs = [a for a, _ in items]
    out_shapes = []
    for a, ax in items:
        shp = list(a.shape)
        if mode == "gather":
            shp[ax] *= n
        else:
            shp[ax] //= n
            shp = [n] + shp
        out_shapes.append(SDS(tuple(shp), a.dtype))

    def body(*refs):
        ins, outs = refs[:ni], refs[ni:2 * ni]
        send_sems, recv_sems, local_sems = refs[2 * ni:]
        x, y, c = lax.axis_index("x"), lax.axis_index("y"), lax.axis_index("c")

        def slot(px, py, pc):
            if group == "xy":
                return 2 * px + py
            if group == "c":
                return pc
            return 4 * px + 2 * py + pc

        me = slot(x, y, c)

        def block(ref, ax, idx, size):
            ix = (slice(None),) * ax + (pl.ds(pl.multiple_of(idx * size, size), size),)
            return ref.at[ix]

        copies = []
        for it, (a, ax) in enumerate(items):
            in_ref, out_ref = ins[it], outs[it]
            if mode == "gather":
                size = a.shape[ax]
                src_own, dst_own = in_ref, block(out_ref, ax, me, size)
            else:
                size = a.shape[ax] // n
                src_own, dst_own = block(in_ref, ax, me, size), out_ref.at[me]
            sax, ns, cs = splits[it]
            pieces = [(slice(None),) * sax + (pl.ds(j * cs, cs),) for j in range(ns)]
            if keep_own:
                for j, piece in enumerate(pieces):
                    own = pltpu.make_async_copy(src_own.at[piece], dst_own.at[piece], local_sems.at[it * nsplit + j])
                    own.start()
                    copies.append(own)
            for k, m in enumerate(masks):
                peer = tuple((1 - v) if b else v for v, b in zip((x, y, c), m))
                if mode == "gather":
                    src, dst = in_ref, dst_own
                else:
                    src, dst = block(in_ref, ax, slot(*peer), size), out_ref.at[me]
                for j, piece in enumerate(pieces):
                    sem = (it * npeer + k) * nsplit + j
                    cp = pltpu.make_async_remote_copy(
                        src_ref=src.at[piece], dst_ref=dst.at[piece], send_sem=send_sems.at[sem],
                        recv_sem=recv_sems.at[sem], device_id=peer, device_id_type=MESH)
                    cp.start()
                    copies.append(cp)
        for cp in copies:
            cp.wait()

    splits = []
    for (a, ax), o in zip(items, out_shapes):
        bshape = a.shape if mode == "gather" else o.shape[1:]
        sax = max(range(len(bshape) - 1), key=lambda d: bshape[d])
        ns = nsplit if bshape[sax] % (nsplit * 16) == 0 else 1
        splits.append((sax, ns, bshape[sax] // ns))
    any_spec = pl.BlockSpec(memory_space=pl.ANY)
    nsem = ni * npeer * nsplit
    outs = _pcall(
        body, name=name, out_shape=tuple(out_shapes),
        in_specs=[any_spec] * ni, out_specs=tuple([any_spec] * ni),
        scratch_shapes=[pltpu.SemaphoreType.DMA((nsem,)), pltpu.SemaphoreType.DMA((nsem,)),
                        pltpu.SemaphoreType.DMA((ni * nsplit,))],
    )(*arrs)
    return list(outs)


_HBM = pl.BlockSpec(memory_space=pltpu.HBM)
_SEM = pl.BlockSpec(memory_space=pltpu.SEMAPHORE)
_ANY = pl.BlockSpec(memory_space=pl.ANY)
_SPLIT = pltpu.CompilerParams(has_side_effects=pltpu.SideEffectType.DATAFLOW_SIDE_EFFECTING)
_XY = _GROUPS["xy"][0]


def _in_hbm(a):
    return pltpu.with_memory_space_constraint(a, pltpu.HBM)


def _chip_peers():
    x, y, c = lax.axis_index("x"), lax.axis_index("y"), lax.axis_index("c")
    return x, y, c, [tuple((1 - v) if b else v for v, b in zip((x, y), m)) for m in _XY]


def _wblock(ref, ax, half, chip):
    R, C = ref.shape
    if ax == 0:
        rows = R // 8
        return ref.at[pl.ds(pl.multiple_of(chip * (2 * rows) + half * rows, rows), rows), :]
    rows, cols = R // 2, C // 4
    return ref.at[pl.ds(pl.multiple_of(half * rows, rows), rows), pl.ds(pl.multiple_of(chip * cols, cols), cols)]


def _gather_start(items, after, name):
    ni = len(items)
    arrs = [a for a, _ in items]

    def body(*refs):
        send_sems, recv_sems = refs[ni + 1], refs[ni + 2]
        outs, token = refs[ni + 3:2 * ni + 3], refs[2 * ni + 3]
        x, y, c, peers = _chip_peers()
        for it, (_, ax) in enumerate(items):
            mine = _wblock(outs[it], ax, c, 2 * x + y)
            for k, (px, py) in enumerate(peers):
                pltpu.make_async_remote_copy(
                    src_ref=mine, dst_ref=mine, send_sem=send_sems.at[it * 3 + k], recv_sem=recv_sems.at[it * 3 + k],
                    device_id=(px, py, c), device_id_type=MESH).start()
        token[...] = jnp.zeros_like(token)

    res = _pcall(
        body, name=name,
        out_shape=(pltpu.SemaphoreType.DMA((ni * 3,)), pltpu.SemaphoreType.DMA((ni * 3,)),
                   *[pltpu.HBM(a.shape, a.dtype) for a in arrs], SDS((8, LANES), F32)),
        in_specs=[_HBM] * ni + [_ANY],
        out_specs=(_SEM, _SEM, *[_HBM] * ni, pl.BlockSpec(memory_space=pltpu.VMEM)),
        input_output_aliases={i: 2 + i for i in range(ni)}, compiler_params=_SPLIT,
    )(*[_in_hbm(a) for a in arrs], after)
    return res[0], res[1], list(res[2:2 + ni]), res[2 + ni]


def _gather_wait(items, send_sems, recv_sems, after, name):
    ni = len(items)
    arrs = [a for a, _ in items]

    def body(*refs):
        s_sems, r_sems = refs[ni], refs[ni + 1]
        outs = refs[ni + 3:]
        x, y, c, peers = _chip_peers()
        for it, (_, ax) in enumerate(items):
            mine = _wblock(outs[it], ax, c, 2 * x + y)
            for k, (px, py) in enumerate(peers):
                cp = pltpu.make_async_remote_copy(
                    src_ref=mine, dst_ref=_wblock(outs[it], ax, c, 2 * px + py), send_sem=s_sems.at[it * 3 + k],
                    recv_sem=r_sems.at[it * 3 + k], device_id=(px, py, c), device_id_type=MESH)
                cp.wait_send()
                cp.wait_recv()

    res = _pcall(
        body, name=name, out_shape=tuple(pltpu.HBM(a.shape, a.dtype) for a in arrs),
        in_specs=[_HBM] * ni + [_SEM, _SEM, _ANY], out_specs=tuple([_HBM] * ni),
        input_output_aliases={i: i for i in range(ni)}, compiler_params=_SPLIT,
    )(*arrs, send_sems, recv_sems, after)
    return list(res)


def _gather_forward(items, name):
    ni = len(items)
    arrs = [a for a, _ in items]

    def body(*refs):
        outs = refs[ni:2 * ni]
        send_sems, recv_sems = refs[2 * ni:]
        x, y, c, peers = _chip_peers()
        def copy(it, ax, k, chip, dst_half):
            return pltpu.make_async_remote_copy(
                src_ref=_wblock(outs[it], ax, c, chip), dst_ref=_wblock(outs[it], ax, dst_half, chip),
                send_sem=send_sems.at[it * 3 + k], recv_sem=recv_sems.at[it * 3 + k],
                device_id=(x, y, 1 - c), device_id_type=MESH)

        for it, (_, ax) in enumerate(items):
            for k, (px, py) in enumerate(peers):
                copy(it, ax, k, 2 * px + py, c).start()
        for it, (_, ax) in enumerate(items):
            for k, (px, py) in enumerate(peers):
                copy(it, ax, k, 2 * px + py, c).wait_send()
                copy(it, ax, k, 2 * px + py, 1 - c).wait_recv()

    res = _pcall(
        body, name=name, out_shape=tuple(SDS(a.shape, a.dtype) for a in arrs),
        in_specs=[_ANY] * ni, out_specs=tuple([_ANY] * ni), input_output_aliases={i: i for i in range(ni)},
        scratch_shapes=[pltpu.SemaphoreType.DMA((ni * 3,)), pltpu.SemaphoreType.DMA((ni * 3,))],
    )(*arrs)
    return list(res)


def _gblock(ref, ax, chip):
    size = ref.shape[ax] // 4
    piece = pl.ds(pl.multiple_of(chip * size, size), size)
    return ref.at[piece, :] if ax == 0 else ref.at[:, piece]


def _scatter_start(items, after, name):
    ni = len(items)
    arrs = [a for a, _ in items]
    lands = []
    for a, ax in items:
        blk = (a.shape[0] // 4, a.shape[1]) if ax == 0 else (a.shape[0], a.shape[1] // 4)
        lands.append(lax.empty((4, *blk), a.dtype))

    def body(*refs):
        send_sems, recv_sems = refs[2 * ni + 1], refs[2 * ni + 2]
        srcs, dsts = refs[2 * ni + 3:3 * ni + 3], refs[3 * ni + 3:4 * ni + 3]
        token = refs[4 * ni + 3]
        x, y, c, peers = _chip_peers()
        for it, (_, ax) in enumerate(items):
            for k, (px, py) in enumerate(peers):
                pltpu.make_async_remote_copy(
                    src_ref=_gblock(srcs[it], ax, 2 * px + py), dst_ref=dsts[it].at[2 * x + y],
                    send_sem=send_sems.at[it * 3 + k], recv_sem=recv_sems.at[it * 3 + k],
                    device_id=(px, py, c), device_id_type=MESH).start()
        token[...] = jnp.zeros_like(token)

    res = _pcall(
        body, name=name,
        out_shape=(pltpu.SemaphoreType.DMA((ni * 3,)), pltpu.SemaphoreType.DMA((ni * 3,)),
                   *[pltpu.HBM(a.shape, a.dtype) for a in arrs], *[pltpu.HBM(a.shape, a.dtype) for a in lands],
                   SDS((8, LANES), F32)),
        in_specs=[_HBM] * (2 * ni) + [_ANY],
        out_specs=(_SEM, _SEM, *[_HBM] * (2 * ni), pl.BlockSpec(memory_space=pltpu.VMEM)),
        input_output_aliases={i: 2 + i for i in range(2 * ni)}, compiler_params=_SPLIT,
    )(*[_in_hbm(a) for a in arrs], *[_in_hbm(a) for a in lands], after)
    return res[0], res[1], list(res[2:2 + ni]), list(res[2 + ni:2 + 2 * ni]), res[2 + 2 * ni]


def _scatter_wait(items, lands, send_sems, recv_sems, after, name):
    ni = len(items)
    arrs = [a for a, _ in items]

    def body(*refs):
        s_sems, r_sems = refs[2 * ni], refs[2 * ni + 1]
        srcs, dsts = refs[2 * ni + 3:3 * ni + 3], refs[3 * ni + 3:]
        x, y, c, peers = _chip_peers()
        for it, (_, ax) in enumerate(items):
            for k, (px, py) in enumerate(peers):
                cp = pltpu.make_async_remote_copy(
                    src_ref=_gblock(srcs[it], ax, 2 * px + py), dst_ref=dsts[it].at[2 * px + py],
                    send_sem=s_sems.at[it * 3 + k], recv_sem=r_sems.at[it * 3 + k],
                    device_id=(px, py, c), device_id_type=MESH)
                cp.wait_send()
                cp.wait_recv()

    res = _pcall(
        body, name=name, out_shape=tuple(pltpu.HBM(a.shape, a.dtype) for a in arrs + list(lands)),
        in_specs=[_HBM] * (2 * ni) + [_SEM, _SEM, _ANY], out_specs=tuple([_HBM] * (2 * ni)),
        input_output_aliases={i: i for i in range(2 * ni)}, compiler_params=_SPLIT,
    )(*arrs, *lands, send_sems, recv_sems, after)
    return list(res[:ni]), list(res[ni:])


def _tile(n, cap):
    if n <= cap:
        return n
    t = cap - cap % LANES
    while n % t:
        t -= LANES
    return t


def _mm(a, b, mode, *, name, out_dtypes=(F32,), tm=MM_TILE, tn=MM_TILE, tk=MM_TILE, a_fn=None, out_fn=None, aux=(),
        stack=None):
    (a, a_lead), (b, b_lead) = _split(a), _split(b)
    ash, bsh = a.shape[-2:], b.shape[-2:]
    if mode == "nn":
        (M, K), (_, N) = ash, bsh
    elif mode == "nt":
        (M, K), (N, _) = ash, bsh
    else:
        (K, M), (_, N) = ash, bsh
    tm, tn, tk = _tile(M, tm), _tile(N, tn), _tile(K, tk)
    nk = K // tk
    if mode == "tn":
        a_spec = _bs((tk, tm), lambda i, j, k: (k, i), a_lead)
    else:
        a_spec = _bs((tm, tk), lambda i, j, k: (i, k), a_lead)
    if mode == "nt":
        b_spec = _bs((tn, tk), lambda i, j, k: (j, k), b_lead)
    else:
        b_spec = _bs((tk, tn), lambda i, j, k: (k, j), b_lead)
    aux_arrs, aux_specs = [], []
    for arr, kind in aux:
        arr, lead = _split(arr)
        aux_arrs.append(arr)
        if kind == "mn":
            aux_specs.append(_bs((tm, tn), lambda i, j, k: (i, j), lead))
        else:
            aux_specs.append(_bs((1, tn), lambda i, j, k: (0, j), lead))
    na, no = len(aux_arrs), len(out_dtypes)
    dims = {"nn": (((1,), (0,)), ((), ())), "nt": (((1,), (1,)), ((), ())), "tn": (((0,), (0,)), ((), ()))}[mode]

    def finish(r, aux_refs, o_refs):
        outs = out_fn(r, *[x[...] for x in aux_refs]) if out_fn is not None else (r,)
        for o_ref, val in zip(o_refs, outs):
            o_ref[...] = val.astype(o_ref.dtype)

    def product(a_ref, b_ref):
        av = a_ref[...]
        if a_fn is not None:
            av = a_fn(av.astype(F32))
        return lax.dot_general(av.astype(_MXU), b_ref[...].astype(_MXU), dims, preferred_element_type=F32)

    nbuf = 0 if stack is None or stack[0] is None else 1

    def body_one(a_ref, b_ref, *rest):
        finish(product(a_ref, b_ref), rest[:na], rest[na + nbuf:na + nbuf + no])

    def body_acc(a_ref, b_ref, *rest):
        aux_refs, o_refs, acc = rest[:na], rest[na + nbuf:na + nbuf + no], rest[na + nbuf + no]
        k = pl.program_id(2)

        @pl.when(k == 0)
        def _():
            acc[...] = product(a_ref, b_ref)

        @pl.when(k != 0)
        def _():
            acc[...] += product(a_ref, b_ref)

        @pl.when(k == nk - 1)
        def _():
            finish(acc[...], aux_refs, o_refs)

    extra, aliases = {}, []
    if stack is None:
        out_shape = (M, N)
        o_spec = pl.BlockSpec((tm, tn), lambda i, j, k: (i, j))
    else:
        buf, layer, n_layers = stack
        assert no == 1
        out_shape = (n_layers, M, N)
        o_spec = pl.BlockSpec((None, tm, tn), lambda i, j, k: (layer, i, j))
        if buf is not None:
            aliases = [buf]
            extra = dict(input_output_aliases={2 + na: 0})
    outs = _pcall(
        body_one if nk == 1 else body_acc, name=name, grid=(M // tm, N // tn, nk),
        in_specs=[a_spec, b_spec] + aux_specs + [pl.BlockSpec(memory_space=pl.ANY)] * nbuf,
        out_specs=tuple([o_spec] * no),
        out_shape=tuple(SDS(out_shape, dt) for dt in out_dtypes),
        scratch_shapes=[] if nk == 1 else [pltpu.VMEM((tm, tn), F32)],
        compiler_params=pltpu.CompilerParams(dimension_semantics=("parallel", "parallel", "arbitrary"),
                                             vmem_limit_bytes=VMEM_BIG),
        **extra,
    )(a, b, *aux_arrs, *aliases)
    return outs[0] if no == 1 else outs


def _tok(S):
    ts = min(TOKEN_BLOCK, S)
    assert S % ts == 0
    return ts


def _row(ts, D):
    return pl.BlockSpec((ts, D), lambda i: (i, 0))


def _acc_rows(ref, i, rows):
    @pl.when(i == 0)
    def _():
        for r, v in enumerate(rows):
            ref[r:r + 1, :] = v

    @pl.when(i != 0)
    def _():
        for r, v in enumerate(rows):
            ref[r:r + 1, :] += v


def _modulate(x, mod, r_sh, r_sc, name):
    S, D = x.shape
    ts = _tok(S)

    def body(x_ref, m_ref, o_ref):
        o_ref[...] = (x_ref[...] * (1.0 + m_ref[r_sc:r_sc + 1, :]) + m_ref[r_sh:r_sh + 1, :]).astype(o_ref.dtype)

    return _pcall(body, name=name, grid=(S // ts,), in_specs=[_row(ts, D), _full(mod)],
                  out_specs=_row(ts, D), out_shape=SDS((S, D), _MXU))(x, mod)


def _modulate_bwd(x, mod, r_sc, dh, dxa, name):
    S, D = x.shape
    ts = _tok(S)

    def body(x_ref, m_ref, dh_ref, dxa_ref, dx_ref, dss_ref):
        dh_v = dh_ref[...]
        dx_ref[...] = dxa_ref[...] + dh_v * (1.0 + m_ref[r_sc:r_sc + 1, :])
        _acc_rows(dss_ref, pl.program_id(0),
                  [jnp.sum(dh_v, axis=0, keepdims=True), jnp.sum(dh_v * x_ref[...], axis=0, keepdims=True)])

    return _pcall(body, name=name, grid=(S // ts,),
                  in_specs=[_row(ts, D), _full(mod), _row(ts, D), _row(ts, D)],
                  out_specs=(_row(ts, D), pl.BlockSpec((2, D), lambda i: (0, 0))),
                  out_shape=(SDS((S, D), F32), SDS((2, D), F32)))(x, mod, dh, dxa)


def _norm_stats(z):
    mu = jnp.mean(z, axis=-1, keepdims=True)
    zc = z - mu
    var = jnp.mean(zc * zc, axis=-1, keepdims=True)
    rstd = lax.rsqrt(var + LN_EPS)
    return zc * rstd, rstd


def _norm_bwd(dxhat, xhat, rstd):
    return rstd * (dxhat - jnp.mean(dxhat, axis=-1, keepdims=True)
                   - xhat * jnp.mean(dxhat * xhat, axis=-1, keepdims=True))


def _combine(x, y, mod, r_gt, lnp, r_g, name):
    S, D = x.shape
    ts = _tok(S)

    def body(x_ref, y_ref, m_ref, l_ref, o_ref):
        z = ALPHA * x_ref[...] + (1.0 + m_ref[r_gt:r_gt + 1, :]) * y_ref[...]
        xhat, _ = _norm_stats(z)
        o_ref[...] = xhat * l_ref[r_g:r_g + 1, :] + l_ref[r_g + 1:r_g + 2, :]

    return _pcall(body, name=name, grid=(S // ts,), in_specs=[_row(ts, D), _row(ts, D), _full(mod), _full(lnp)],
                  out_specs=_row(ts, D), out_shape=SDS((S, D), F32))(x, y, mod, lnp)


def _combine_bwd(x, y, mod, r_gt, lnp, r_g, dout, name):
    S, D = x.shape
    ts = _tok(S)

    def body(x_ref, y_ref, m_ref, l_ref, do_ref, dxa_ref, dy_ref, dp_ref):
        gate = 1.0 + m_ref[r_gt:r_gt + 1, :]
        y_v, do_v = y_ref[...], do_ref[...]
        xhat, rstd = _norm_stats(ALPHA * x_ref[...] + gate * y_v)
        dz = _norm_bwd(do_v * l_ref[r_g:r_g + 1, :], xhat, rstd)
        dxa_ref[...] = ALPHA * dz
        dy_ref[...] = (gate * dz).astype(dy_ref.dtype)
        _acc_rows(dp_ref, pl.program_id(0),
                  [jnp.sum(dz * y_v, axis=0, keepdims=True), jnp.sum(do_v * xhat, axis=0, keepdims=True),
                   jnp.sum(do_v, axis=0, keepdims=True)])

    return _pcall(body, name=name, grid=(S // ts,),
                  in_specs=[_row(ts, D), _row(ts, D), _full(mod), _full(lnp), _row(ts, D)],
                  out_specs=(_row(ts, D), _row(ts, D), pl.BlockSpec((3, D), lambda i: (0, 0))),
                  out_shape=(SDS((S, D), F32), SDS((S, D), _MXU), SDS((3, D), F32)))(x, y, mod, lnp, dout)


def _modcomb_bwd(later, x, y, mod, r_gt, lnp, r_g, name):
    xn, mod_n, r_sc_n, dh, dxa_n = later
    S, D = x.shape
    ts = _tok(S)

    def body(xn_ref, mn_ref, dh_ref, dxan_ref, x_ref, y_ref, m_ref, l_ref, dxa_ref, dy_ref, dp_ref, dss_ref):
        i = pl.program_id(0)
        dh_v = dh_ref[...]
        do_v = dxan_ref[...] + dh_v * (1.0 + mn_ref[r_sc_n:r_sc_n + 1, :])
        _acc_rows(dss_ref, i, [jnp.sum(dh_v, axis=0, keepdims=True),
                               jnp.sum(dh_v * xn_ref[...], axis=0, keepdims=True)])
        gate = 1.0 + m_ref[r_gt:r_gt + 1, :]
        y_v = y_ref[...]
        xhat, rstd = _norm_stats(ALPHA * x_ref[...] + gate * y_v)
        dz = _norm_bwd(do_v * l_ref[r_g:r_g + 1, :], xhat, rstd)
        dxa_ref[...] = ALPHA * dz
        dy_ref[...] = (gate * dz).astype(dy_ref.dtype)
        _acc_rows(dp_ref, i, [jnp.sum(dz * y_v, axis=0, keepdims=True), jnp.sum(do_v * xhat, axis=0, keepdims=True),
                              jnp.sum(do_v, axis=0, keepdims=True)])

    row = _row(ts, D)
    return _pcall(body, name=name, grid=(S // ts,),
                  in_specs=[row, _full(mod_n), row, row, row, row, _full(mod), _full(lnp)],
                  out_specs=(row, row, pl.BlockSpec((3, D), lambda i: (0, 0)), pl.BlockSpec((2, D), lambda i: (0, 0))),
                  out_shape=(SDS((S, D), F32), SDS((S, D), _MXU), SDS((3, D), F32), SDS((2, D), F32)))(
        xn, mod_n, dh, dxa_n, x, y, mod, lnp)


def _sublayer_head_bwd(dout, x, y, mod, r_gt, lnp, r_g, tag):
    if isinstance(dout, tuple):
        return _modcomb_bwd(dout, x, y, mod, r_gt, lnp, r_g, f"{tag}_ln_b")
    return (*_combine_bwd(x, y, mod, r_gt, lnp, r_g, dout, f"{tag}_ln_b"), None)


def _sigmoid(t):
    return 1.0 / (1.0 + jnp.exp(-t))


def _ln_silu(u, lnp, name):
    S, D = u.shape
    ts = _tok(S)

    def body(u_ref, l_ref, o_ref):
        xhat, _ = _norm_stats(u_ref[...])
        t = xhat * l_ref[0:1, :] + l_ref[1:2, :]
        o_ref[...] = (t * _sigmoid(t)).astype(o_ref.dtype)

    return _pcall(body, name=name, grid=(S // ts,), in_specs=[_row(ts, D), _full(lnp)],
                  out_specs=_row(ts, D), out_shape=SDS((S, D), _MXU))(u, lnp)


def _ln_silu_bwd(u, lnp, dout, name):
    S, D = u.shape
    ts = _tok(S)

    def body(u_ref, l_ref, do_ref, du_ref, dp_ref):
        xhat, rstd = _norm_stats(u_ref[...])
        g = l_ref[0:1, :]
        t = xhat * g + l_ref[1:2, :]
        sg = _sigmoid(t)
        dt = do_ref[...] * (sg * (1.0 + t * (1.0 - sg)))
        du_ref[...] = _norm_bwd(dt * g, xhat, rstd)
        _acc_rows(dp_ref, pl.program_id(0),
                  [jnp.sum(dt * xhat, axis=0, keepdims=True), jnp.sum(dt, axis=0, keepdims=True)])

    return _pcall(body, name=name, grid=(S // ts,), in_specs=[_row(ts, D), _full(lnp), _row(ts, D)],
                  out_specs=(_row(ts, D), pl.BlockSpec((2, D), lambda i: (0, 0))),
                  out_shape=(SDS((S, D), F32), SDS((2, D), F32)))(u, lnp, dout)


def _loss_head(xf, tgt, name):
    S, D = xf.shape
    ts = _tok(S)

    def body(x_ref, t_ref, dx_ref, l_ref):
        err = x_ref[...] - t_ref[...]
        dx_ref[...] = err * (1.0 / D)
        part = jnp.sum(jnp.sum(err * err, axis=1, keepdims=True), axis=0, keepdims=True) * (0.5 / D)

        @pl.when(pl.program_id(0) == 0)
        def _():
            l_ref[...] = part

        @pl.when(pl.program_id(0) != 0)
        def _():
            l_ref[...] += part

    return _pcall(body, name=name, grid=(S // ts,), in_specs=[_row(ts, D), _row(ts, D)],
                  out_specs=(_row(ts, D), pl.BlockSpec((1, 1), lambda i: (0, 0))),
                  out_shape=(SDS((S, D), F32), SDS((1, 1), F32)))(xf, tgt)


CONV_PAD = 32


def _padded(u, before):
    zeros = jnp.zeros((CONV_PAD, u.shape[1]), u.dtype)
    return jnp.concatenate([zeros, u] if before else [u, zeros], axis=0)


SUBLANES = 8


def _shift_down(u_pad, s, rolled):
    q, r = divmod(s, SUBLANES)
    if r not in rolled:
        rolled[r] = u_pad if r == 0 else pltpu.roll(u_pad, r, 0)
    start = CONV_PAD - SUBLANES * q
    return rolled[r][start:start + u_pad.shape[0] - CONV_PAD]


def _shift_up(u_pad, s, rolled):
    n = u_pad.shape[0]
    q, r = divmod(s, SUBLANES)
    if r not in rolled:
        rolled[r] = u_pad if r == 0 else pltpu.roll(u_pad, n - r, 0)
    return rolled[r][SUBLANES * q:SUBLANES * q + n - CONV_PAD]


def _dwconv(u, w_ref, taps, rows):
    del rows
    assert taps - 1 <= CONV_PAD
    u_pad, rolled = _padded(u, True), {}
    acc = jnp.zeros_like(u)
    for j in range(taps):
        acc = acc + w_ref[j:j + 1, :] * _shift_down(u_pad, taps - 1 - j, rolled)
    return acc


def _dwconv_bwd(u, dy, w_ref, dw_ref, taps, rows):
    del rows
    assert taps - 1 <= CONV_PAD
    u_pad, dy_pad, u_rolled, dy_rolled = _padded(u, True), _padded(dy, False), {}, {}
    du = jnp.zeros_like(u)
    for j in range(taps):
        s = taps - 1 - j
        du = du + w_ref[j:j + 1, :] * _shift_up(dy_pad, s, dy_rolled)
        dw_ref[j:j + 1, :] = jnp.sum(dy * _shift_down(u_pad, s, u_rolled), axis=0, keepdims=True)
    return du


def _col(S, j0=0):
    return pl.BlockSpec((S, LANES), lambda j: (0, j + j0))


def _conv_silu(pm, w, nblk, name):
    S = pm.shape[0]
    taps = w.shape[0]

    def body(u_ref, w_ref, o_ref):
        rows = lax.broadcasted_iota(jnp.int32, (S, LANES), 0)
        cv = _dwconv(u_ref[...], w_ref, taps, rows)
        o_ref[...] = cv * _sigmoid(cv)

    return _pcall(body, name=name, grid=(nblk,),
                  in_specs=[_col(S), pl.BlockSpec((taps, LANES), lambda j: (0, j))],
                  out_specs=_col(S), out_shape=SDS((S, nblk * LANES), F32),
                  compiler_params=_cparams(VMEM_BIG))(pm, w)


def _conv_silu_bwd(pm, w, dout, dpm, j0, name):
    S = pm.shape[0]
    taps = w.shape[0]
    nblk = dout.shape[1] // LANES

    def body(u_ref, w_ref, do_ref, dpm_in, du_ref, dw_ref):
        del dpm_in
        rows = lax.broadcasted_iota(jnp.int32, (S, LANES), 0)
        u = u_ref[...]
        cv = _dwconv(u, w_ref, taps, rows)
        sg = _sigmoid(cv)
        dc = do_ref[...] * (sg * (1.0 + cv * (1.0 - sg)))
        du_ref[...] = _dwconv_bwd(u, dc, w_ref, dw_ref, taps, rows)

    return _pcall(body, name=name, grid=(nblk,),
                  in_specs=[_col(S, j0), pl.BlockSpec((taps, LANES), lambda j: (0, j + j0)), _col(S),
                            pl.BlockSpec(memory_space=pl.ANY)],
                  out_specs=(_col(S, j0), pl.BlockSpec((taps, LANES), lambda j: (0, j))),
                  out_shape=(SDS(dpm.shape, F32), SDS((taps, nblk * LANES), F32)),
                  input_output_aliases={3: 0},
                  compiler_params=_cparams(VMEM_BIG))(pm, w, dout, dpm)


def _glu_conv(p, w, bias, name):
    S, C2 = p.shape
    nblk = C2 // 2 // LANES
    taps = w.shape[0]

    def body(v_ref, g_ref, w_ref, b_ref, o_ref):
        rows = lax.broadcasted_iota(jnp.int32, (S, LANES), 0)
        u = v_ref[...] * _sigmoid(g_ref[...])
        o_ref[...] = _dwconv(u, w_ref, taps, rows) + b_ref[...]

    return _pcall(body, name=name, grid=(nblk,),
                  in_specs=[_col(S), _col(S, nblk), pl.BlockSpec((taps, LANES), lambda j: (0, j)),
                            pl.BlockSpec((1, LANES), lambda j: (0, j))],
                  out_specs=_col(S), out_shape=SDS((S, nblk * LANES), F32),
                  compiler_params=_cparams(VMEM_BIG))(p, p, w, bias)


def _glu_conv_bwd(p, w, dout, name):
    S, C2 = p.shape
    nblk = C2 // 2 // LANES
    taps = w.shape[0]

    def body(v_ref, g_ref, w_ref, do_ref, dv_ref, dg_ref, dw_ref, db_ref):
        rows = lax.broadcasted_iota(jnp.int32, (S, LANES), 0)
        val, sg = v_ref[...], _sigmoid(g_ref[...])
        do_v = do_ref[...]
        du = _dwconv_bwd(val * sg, do_v, w_ref, dw_ref, taps, rows)
        dv_ref[...] = du * sg
        dg_ref[...] = du * val * sg * (1.0 - sg)
        db_ref[...] = jnp.sum(do_v, axis=0, keepdims=True)

    dval, dgate, dw, db = _pcall(
        body, name=name, grid=(nblk,),
        in_specs=[_col(S), _col(S, nblk), pl.BlockSpec((taps, LANES), lambda j: (0, j)), _col(S)],
        out_specs=(_col(S), _col(S), pl.BlockSpec((taps, LANES), lambda j: (0, j)),
                   pl.BlockSpec((1, LANES), lambda j: (0, j))),
        out_shape=(SDS((S, C2 // 2), F32), SDS((S, C2 // 2), F32), SDS((taps, C2 // 2), F32), SDS((1, C2 // 2), F32)),
        compiler_params=_cparams(VMEM_BIG))(p, p, w, dout)
    return dval, dgate, dw, db


def _log1p(e):
    u = 1.0 + e
    d = jnp.where(u == 1.0, 1.0, u - 1.0)
    return jnp.where(u == 1.0, e, jnp.log(u) * (e / d))


def _gate_parts(ps, prm, H):
    lane = lax.broadcasted_iota(jnp.int32, ps.shape, 1)
    is_b, is_g = lane < H, (lane >= H) & (lane < 2 * H)
    beta = _sigmoid(ps)
    t = ps + prm[1:2, :]
    sp = jnp.maximum(t, 0.0) + _log1p(jnp.exp(-jnp.abs(t)))
    na = -jnp.exp(prm[0:1, :])
    return is_b, is_g, beta, t, sp, na


def _gates(ps, prm, H, name):
    S = ps.shape[0]
    ts = _tok(S)

    def body(p_ref, r_ref, o_ref):
        is_b, is_g, beta, _, sp, na = _gate_parts(p_ref[...], r_ref[...], H)
        o_ref[...] = jnp.where(is_b, beta, jnp.where(is_g, na * sp, 0.0))

    return _pcall(body, name=name, grid=(S // ts,), in_specs=[_row(ts, LANES), _full(prm)],
                  out_specs=_row(ts, LANES), out_shape=SDS((S, LANES), F32))(ps, prm)


def _gates_bwd(ps, prm, dgates, H, name):
    S = ps.shape[0]
    ts = _tok(S)

    def body(p_ref, r_ref, dg_ref, dp_ref, dr_ref):
        is_b, is_g, beta, t, sp, na = _gate_parts(p_ref[...], r_ref[...], H)
        dg_v = dg_ref[...]
        dsp = jnp.where(is_g, dg_v * na * _sigmoid(t), 0.0)
        dp_ref[...] = jnp.where(is_b, dg_v * beta * (1.0 - beta), dsp)
        _acc_rows(dr_ref, pl.program_id(0),
                  [jnp.sum(jnp.where(is_g, dg_v * na * sp, 0.0), axis=0, keepdims=True),
                   jnp.sum(dsp, axis=0, keepdims=True)])

    return _pcall(body, name=name, grid=(S // ts,), in_specs=[_row(ts, LANES), _full(prm), _row(ts, LANES)],
                  out_specs=(_row(ts, LANES), pl.BlockSpec((2, LANES), lambda i: (0, 0))),
                  out_shape=(SDS((S, LANES), F32), SDS((2, LANES), F32)))(ps, prm, dgates)


_NN = (((2,), (1,)), ((0,), (0,)))
_NT = (((2,), (2,)), ((0,), (0,)))
_TN = (((1,), (1,)), ((0,), (0,)))


def _mdot(a, b, dims):
    return lax.dot_general(a.astype(_MXU), b.astype(_MXU), dims, preferred_element_type=F32)


def _mdot3(a, b, dims):
    ah, bh = a.astype(_MXU), b.astype(_MXU)
    al, bl = a - ah.astype(F32), b - bh.astype(F32)
    return _mdot(ah, bh, dims) + (_mdot(ah, bl, dims) + _mdot(al, bh, dims))


def _rounded_dot(dims, da_dims, db_dims, a_first, prod=_mdot):
    @jax.custom_vjp
    def f(a, b):
        return prod(a, b, dims)

    def fwd(a, b):
        return prod(a, b, dims), (a, b)

    def bwd(res, ct):
        a, b = res
        da = prod(ct, b, da_dims) if a_first[0] else prod(b, ct, da_dims)
        db = prod(ct, a, db_dims) if a_first[1] else prod(a, ct, db_dims)
        return da, db

    f.defvjp(fwd, bwd)
    return f


_mdot_nn = _rounded_dot(_NN, _NT, _TN, (True, False))
_mdot_nt = _rounded_dot(_NT, _NN, _TN, (True, True))
_mdot_tn = _rounded_dot(_TN, _NT, _NN, (False, False))
def _unit_lower_inverse(a):
    C = a.shape[-1]
    ri = lax.broadcasted_iota(jnp.int32, (1, C, C), 1)
    ci = lax.broadcasted_iota(jnp.int32, (1, C, C), 2)
    t_inv = jnp.where(ri == ci, 1.0, 0.0) - a
    p = a
    for _ in range(max(C.bit_length() - 2, 0)):
        p = _mdot3(p, p, _NN)
        t_inv = t_inv + _mdot3(t_inv, p, _NN)
    return t_inv


@jax.custom_vjp
def _known_inverse(a, t_inv):
    del a
    return t_inv


def _known_inverse_fwd(a, t_inv):
    del a
    return t_inv, t_inv


def _known_inverse_bwd(t_inv, ct):
    da = -_mdot3(_mdot3(t_inv, ct, _TN), t_inv, _NT)
    return da, jnp.zeros_like(t_inv)


_known_inverse.defvjp(_known_inverse_fwd, _known_inverse_bwd)


def _head_cols(gates, off, H):
    lane = lax.broadcasted_iota(jnp.int32, gates.shape, 1)
    cols = [jnp.sum(jnp.where(lane == off + h, gates, 0.0), axis=-1, keepdims=True) for h in range(H)]
    return jnp.concatenate([col[None] for col in cols], axis=0)


def _delta_chunk(qr, kr, v, z, gates, nw, s_in, t_known=None):
    H, C, dk = qr.shape
    beta, g = _head_cols(gates, 0, H), _head_cols(gates, H, H)
    q = qr * lax.rsqrt(jnp.sum(qr * qr, axis=-1, keepdims=True) + L2_EPS) * (dk ** -0.5)
    k = kr * lax.rsqrt(jnp.sum(kr * kr, axis=-1, keepdims=True) + L2_EPS)
    ri = lax.broadcasted_iota(jnp.int32, (1, C, C), 1)
    ci = lax.broadcasted_iota(jnp.int32, (1, C, C), 2)
    causal, strict, eye = ri >= ci, ri > ci, ri == ci
    gam_row = jnp.sum(jnp.where(ri <= ci, g, 0.0), axis=1, keepdims=True)
    gam_col = jnp.sum(jnp.where(eye, gam_row, 0.0), axis=-1, keepdims=True)
    g_last = jnp.sum(g, axis=1, keepdims=True)
    decay = jnp.where(causal, jnp.exp(jnp.where(causal, gam_col - gam_row, 0.0)), 0.0)
    kb = k * beta
    a = jnp.where(strict, _mdot_nt(kb, k) * decay, 0.0)
    t_inv = _unit_lower_inverse(a) if t_known is None else _known_inverse(a, t_known)
    eg = jnp.exp(gam_col)
    u = _mdot_nn(t_inv, v * beta)
    w = _mdot_nn(t_inv, kb * eg)
    a_qk = _mdot_nt(q, k) * decay
    v_new = u - _mdot_nn(w, s_in)
    o = _mdot_nn(q * eg, s_in) + _mdot_nn(a_qk, v_new)
    s_out = s_in * jnp.exp(g_last) + _mdot_tn(k * jnp.exp(g_last - gam_col), v_new)
    og = o * lax.rsqrt(jnp.mean(o * o, axis=-1, keepdims=True) + RMS_EPS) * nw * (z * _sigmoid(z))
    return og, s_out, t_inv


def _heads(ref, H, dk):
    return jnp.stack([ref[:, h * dk:(h + 1) * dk].astype(F32) for h in range(H)])


def _put_heads(ref, val, dk):
    for h in range(val.shape[0]):
        ref[:, h * dk:(h + 1) * dk] = val[h].astype(ref.dtype)


def _delta_fwd(qkv, pm, gates, nw, H, name):
    S = qkv.shape[0]
    hd = qkv.shape[1] // 3
    dk = hd // H
    N = S // CHUNK
    blk = lambda off: pl.BlockSpec((CHUNK, hd), lambda n: (n, off))

    def body(q_ref, k_ref, v_ref, z_ref, g_ref, nw_ref, og_ref, st_ref, ti_ref, s_scr):
        @pl.when(pl.program_id(0) == 0)
        def _():
            s_scr[...] = jnp.zeros_like(s_scr)

        s_in = s_scr[...]
        st_ref[...] = s_in
        og, s_out, t_inv = _delta_chunk(_heads(q_ref, H, dk), _heads(k_ref, H, dk), _heads(v_ref, H, dk),
                                        _heads(z_ref, H, dk), g_ref[...], nw_ref[...], s_in)
        _put_heads(og_ref, og, dk)
        ti_ref[...] = t_inv
        s_scr[...] = s_out

    return _pcall(
        body, name=name, grid=(N,),
        in_specs=[blk(0), blk(1), blk(2), blk(3), pl.BlockSpec((CHUNK, LANES), lambda n: (n, 0)), _full(nw)],
        out_specs=(blk(0), pl.BlockSpec((None, H, dk, dk), lambda n: (n, 0, 0, 0)),
                   pl.BlockSpec((None, H, CHUNK, CHUNK), lambda n: (n, 0, 0, 0))),
        out_shape=(SDS((S, hd), _MXU), SDS((N, H, dk, dk), F32), SDS((N, H, CHUNK, CHUNK), F32)),
        scratch_shapes=[pltpu.VMEM((H, dk, dk), F32)],
        compiler_params=_cparams(VMEM_BIG),
    )(qkv, qkv, qkv, pm, gates, nw)


def _delta_bwd(qkv, pm, gates, nw, states, t_invs, dog, H, name):
    S = qkv.shape[0]
    hd = qkv.shape[1] // 3
    dk = hd // H
    N = S // CHUNK
    blk = lambda off: pl.BlockSpec((CHUNK, hd), lambda n: (N - 1 - n, off))
    gspec = pl.BlockSpec((CHUNK, LANES), lambda n: (N - 1 - n, 0))

    def body(q_ref, k_ref, v_ref, z_ref, g_ref, nw_ref, st_ref, ti_ref, do_ref,
             dq_ref, dk_ref, dv_ref, dz_ref, dg_ref, dnw_ref, ds_scr):
        n = pl.program_id(0)

        @pl.when(n == 0)
        def _():
            ds_scr[...] = jnp.zeros_like(ds_scr)

        t_known = ti_ref[...]
        fn = functools.partial(_delta_chunk, t_known=t_known)
        _, vjp = jax.vjp(fn, _heads(q_ref, H, dk), _heads(k_ref, H, dk), _heads(v_ref, H, dk),
                         _heads(z_ref, H, dk), g_ref[...], nw_ref[...], st_ref[...])
        dq, dkk, dv, dz, dg, dnw, ds_in = vjp((_heads(do_ref, H, dk), ds_scr[...], jnp.zeros_like(t_known)))
        _put_heads(dq_ref, dq, dk)
        _put_heads(dk_ref, dkk, dk)
        _put_heads(dv_ref, dv, dk)
        _put_heads(dz_ref, dz, dk)
        ds_scr[...] = ds_in
        dg_ref[...] = dg

        @pl.when(n == 0)
        def _():
            dnw_ref[...] = dnw

        @pl.when(n != 0)
        def _():
            dnw_ref[...] += dnw

    return _pcall(
        body, name=name, grid=(N,),
        in_specs=[blk(0), blk(1), blk(2), blk(3), gspec, _full(nw),
                  pl.BlockSpec((None, H, dk, dk), lambda n: (N - 1 - n, 0, 0, 0)),
                  pl.BlockSpec((None, H, CHUNK, CHUNK), lambda n: (N - 1 - n, 0, 0, 0)), blk(0)],
        out_specs=(blk(0), blk(0), blk(0), blk(3), gspec, pl.BlockSpec((1, dk), lambda n: (0, 0))),
        out_shape=(SDS((S, hd), F32), SDS((S, hd), F32), SDS((S, hd), F32), SDS(pm.shape, F32),
                   SDS((S, LANES), F32), SDS((1, dk), F32)),
        scratch_shapes=[pltpu.VMEM((H, dk, dk), F32)],
        compiler_params=_cparams(VMEM_BIG),
    )(qkv, qkv, qkv, pm, gates, nw, states, t_invs, dog)


def _rows_block(R, C):
    rb = R
    while rb * C * 4 > (1 << 20) and rb % 16 == 0:
        rb //= 2
    return rb


def _sum_slots(st, name, out_dtype=F32):
    n, R, C = st.shape
    rb = _rows_block(R, C)

    def body(s_ref, o_ref):
        acc = s_ref[0].astype(F32)
        for q in range(1, n):
            acc = acc + s_ref[q].astype(F32)
        o_ref[...] = acc.astype(o_ref.dtype)

    return _pcall(body, name=name, grid=(R // rb,), in_specs=[pl.BlockSpec((n, rb, C), lambda i: (0, i, 0))],
                  out_specs=pl.BlockSpec((rb, C), lambda i: (i, 0)), out_shape=SDS((R, C), out_dtype))(st)


def _scalar(v):
    return jnp.reshape(v, (1,)).astype(jnp.int32)


def _place_block(w, layer, ax, chip, dep, name):
    _, R, C = w.shape
    rb = _rows_block(R, C)
    nrb = R // rb
    shp = [R, C]
    shp[ax] *= 4
    omap = (lambda i, c: (c[0] * nrb + i, 0)) if ax == 0 else (lambda i, c: (i, c[0]))

    def body(c_ref, w_ref, dep_ref, o_ref):
        del c_ref, dep_ref
        o_ref[...] = w_ref[...].astype(o_ref.dtype)

    grid_spec = pltpu.PrefetchScalarGridSpec(
        num_scalar_prefetch=1, grid=(nrb,),
        in_specs=[pl.BlockSpec((None, rb, C), lambda i, c: (layer, i, 0)), _ANY],
        out_specs=pl.BlockSpec((rb, C), omap))
    return _pcall(body, name=name, grid_spec=grid_spec, out_shape=SDS(tuple(shp), _MXU))(_scalar(chip), w, dep)


def _sum_pair(own, recv, ic, name, out_dtype):
    N, _, R, C = own.shape
    rb = _rows_block(R, C)

    def body(c_ref, a_ref, b_ref, o_ref):
        del c_ref
        o_ref[...] = (a_ref[...].astype(F32) + b_ref[...].astype(F32)).astype(o_ref.dtype)

    grid_spec = pltpu.PrefetchScalarGridSpec(
        num_scalar_prefetch=1, grid=(N, R // rb),
        in_specs=[pl.BlockSpec((None, None, rb, C), lambda p, i, c: (p, c[0], i, 0)),
                  pl.BlockSpec((None, None, rb, C), lambda p, i, c: (1 - c[0], p, i, 0))],
        out_specs=pl.BlockSpec((None, rb, C), lambda p, i, c: (p, i, 0)))
    return _pcall(body, name=name, grid_spec=grid_spec, out_shape=SDS((N, R, C), out_dtype))(
        _scalar(ic), own, recv)


def _sum_landed(grad, ax, land, chip, name):
    _, R, Cb = land.shape
    rb = _rows_block(R, Cb)
    nrb = R // rb
    if ax == 0:
        own_spec = pl.BlockSpec((rb, Cb), lambda i, c: (c[0] * nrb + i, 0))
    else:
        own_spec = pl.BlockSpec((rb, Cb), lambda i, c: (i, c[0]))
    slot = lambda d: pl.BlockSpec((None, rb, Cb), lambda i, c: ((c[0] + d) % 4, i, 0))

    def body(c_ref, own_ref, r1, r2, r3, o_ref):
        del c_ref
        o_ref[...] = ((own_ref[...].astype(F32) + r1[...].astype(F32)) + r2[...].astype(F32)) + r3[...].astype(F32)

    grid_spec = pltpu.PrefetchScalarGridSpec(
        num_scalar_prefetch=1, grid=(R // rb,), in_specs=[own_spec, slot(1), slot(2), slot(3)],
        out_specs=pl.BlockSpec((rb, Cb), lambda i, c: (i, 0)))
    return _pcall(body, name=name, grid_spec=grid_spec, out_shape=SDS((R, Cb), F32))(
        _scalar(chip), grad, land, land, land)


def _adamw_step(g, w_ref, m_ref, v_ref, g_ref, d_ref, mo_ref, vo_ref):
    m_new = ADAM_B1 * m_ref[...] + (1.0 - ADAM_B1) * g
    v_new = ADAM_B2 * v_ref[...] + (1.0 - ADAM_B2) * (g * g)
    m_hat = m_new / (1.0 - ADAM_B1 ** ADAM_STEP)
    v_hat = v_new / (1.0 - ADAM_B2 ** ADAM_STEP)
    g_ref[...] = g
    d_ref[...] = -ADAM_LR * (m_hat / (jnp.sqrt(v_hat) + ADAM_EPS) + ADAM_WD * w_ref[...])
    mo_ref[...] = m_new
    vo_ref[...] = v_new


def _adamw(w, m, v, st, name):
    R, C = w.shape
    n = st.shape[0]
    rb = _rows_block(R, C)
    spec = pl.BlockSpec((rb, C), lambda i: (i, 0))

    def body(w_ref, m_ref, v_ref, s_ref, *o_refs):
        g = s_ref[0]
        for q in range(1, n):
            g = g + s_ref[q]
        _adamw_step(g, w_ref, m_ref, v_ref, *o_refs)

    return _pcall(body, name=name, grid=(R // rb,),
                  in_specs=[spec, spec, spec, pl.BlockSpec((n, rb, C), lambda i: (0, i, 0))],
                  out_specs=(spec,) * 4, out_shape=(SDS((R, C), F32),) * 4)(w, m, v, st)


def _adamw_pair(w, m, v, layer, own, recv2, ic, bufs, dep, name):
    L, R, C = w.shape
    rb = _rows_block(R, C)
    spec = pl.BlockSpec((None, rb, C), lambda i, c: (layer, i, 0))

    def body(c_ref, w_ref, m_ref, v_ref, own_ref, recv_ref, *rest):
        del c_ref
        _adamw_step(own_ref[...] + recv_ref[...], w_ref, m_ref, v_ref, *rest[-4:])

    nbuf = 0 if bufs is None else 4
    grid_spec = pltpu.PrefetchScalarGridSpec(
        num_scalar_prefetch=1, grid=(R // rb,),
        in_specs=[spec, spec, spec, pl.BlockSpec((rb, C), lambda i, c: (i, 0)),
                  pl.BlockSpec((None, rb, C), lambda i, c: (1 - c[0], i, 0))] + [_ANY] * (nbuf + 1),
        out_specs=(spec,) * 4)
    extra = {} if bufs is None else dict(input_output_aliases={6 + q: q for q in range(4)})
    return _pcall(body, name=name, grid_spec=grid_spec, out_shape=(SDS((L, R, C), F32),) * 4, **extra)(
        _scalar(ic), w, m, v, own, recv2, *([] if bufs is None else bufs), dep)


def _pack(arrs, rows=1):
    flat = jnp.concatenate([a.reshape(-1).astype(F32) for a in arrs])
    quantum = rows * LANES
    pad = (-flat.shape[0]) % quantum
    flat = jnp.pad(flat, (0, pad))
    return flat.reshape(rows, -1)


def _unpack(flat, shapes):
    flat = flat.reshape(-1)
    out, off = [], 0
    for shp in shapes:
        size = 1
        for d in shp:
            size *= d
        out.append(flat[off:off + size].reshape(shp))
        off += size
    return out


def _mlp_fwd(x1, mod, lnp, w1, w2, tag):
    h2 = _modulate(x1, mod, 3, 4, f"{tag}_mod")
    a1, a2 = _mm(h2, w1, "nn", name=f"{tag}_up", out_dtypes=(_MXU, _MXU),
                 out_fn=lambda r: (r, jnp.square(jnp.maximum(r, 0.0))))
    y2 = _mm(a2, w2, "nn", name=f"{tag}_down")
    x2 = _combine(x1, y2, mod, 5, lnp, 2, f"{tag}_ln")
    return x2, (x1, h2, a1, a2, y2)


def _weight_grad(grads, key, a, b, name):
    grads[key] = _mm(a, b, "tn", name=name, out_dtypes=(_MXU,), tk=2 * MM_TILE)


def _mlp_bwd(dx2, saved, mod, lnp, w1, w2, tag, stacks):
    x1, h2, a1, a2, y2 = saved
    dxa, dy2, dp, dss_later = _sublayer_head_bwd(dx2, x1, y2, mod, 5, lnp, 2, tag)
    da1 = _mm(dy2, w2, "nt", name=f"{tag}_down_bx", out_dtypes=(_MXU,), aux=[(a1, "mn")],
              out_fn=lambda r, a: (r * (2.0 * jnp.maximum(a.astype(F32), 0.0)),))
    _weight_grad(stacks, "ff_w2", a2, dy2, f"{tag}_down_bw")
    _weight_grad(stacks, "ff_w1", h2, da1, f"{tag}_up_bw")
    dh2 = _mm(da1, w1, "nt", name=f"{tag}_up_bx")
    return (x1, mod, 4, dh2, dxa), (dss_later, dp)


def _dn_fwd(x, mod, lnp, wts, H, tag):
    w_main, w_small, conv_w, prm, nw, w_out = wts
    h = _modulate(x, mod, 0, 1, f"{tag}_mod")
    pm = _mm(h, w_main, "nn", name=f"{tag}_in")
    ps = _mm(h, w_small, "nn", name=f"{tag}_in_s")
    nqkv = conv_w.shape[1] // LANES
    qkv = _conv_silu(pm, conv_w, nqkv, f"{tag}_conv")
    gates = _gates(ps, prm, H, f"{tag}_gates")
    og, *states = _delta_fwd(qkv, pm, gates, nw, H, f"{tag}_delta")
    y = _mm(og, w_out, "nn", name=f"{tag}_out")
    x1 = _combine(x, y, mod, 2, lnp, 0, f"{tag}_ln")
    return x1, (x, h, pm, ps, qkv, gates, states, og, y)


def _dn_bwd(dx1, saved, mod, lnp, wts, H, tag, stacks):
    w_main, w_small, conv_w, prm, nw, w_out = wts
    x, h, pm, ps, qkv, gates, states, og, y = saved
    dxa, dy, dp, dss_later = _sublayer_head_bwd(dx1, x, y, mod, 2, lnp, 0, tag)
    dog = _mm(dy, w_out, "nt", name=f"{tag}_out_bx")
    _weight_grad(stacks, "dn_w_out", og, dy, f"{tag}_out_bw")
    dq, dk, dv, dpm, dgates, dnw = _delta_bwd(qkv, pm, gates, nw, *states, dog, H, f"{tag}_delta_b")
    dps, dprm = _gates_bwd(ps, prm, dgates, H, f"{tag}_gates_b")
    dcw = []
    nb = dq.shape[1] // LANES
    for part, dpart in enumerate((dq, dk, dv)):
        dpm, dcw_p = _conv_silu_bwd(pm, conv_w, dpart, dpm, part * nb, f"{tag}_conv_b{part}")
        dcw.append(dcw_p)
    dconv_w = jnp.concatenate(dcw, axis=1)
    dw_main = _mm(h, dpm, "tn", name=f"{tag}_in_bw", out_dtypes=(_MXU,))
    dw_small = _mm(h, dps, "tn", name=f"{tag}_in_s_bw", out_dtypes=(_MXU,))
    dh_s = _mm(dps, w_small, "nt", name=f"{tag}_in_s_bx")
    dh = _mm(dpm, w_main, "nt", name=f"{tag}_in_bx", aux=[(dh_s, "mn")], out_fn=lambda r, e: (r + e,))
    return (x, mod, 1, dh, dxa), (dw_main, dw_small, dconv_w, dprm, dnw), (dss_later, dp)


def _cf_fwd(x, mod, lnp, wts, tag):
    w_in, dw_w, dw_b, cln, w_out = wts
    h = _modulate(x, mod, 0, 1, f"{tag}_mod")
    p = _mm(h, w_in, "nn", name=f"{tag}_in")
    u2 = _glu_conv(p, dw_w, dw_b, f"{tag}_conv")
    u3 = _ln_silu(u2, cln, f"{tag}_cln")
    y = _mm(u3, w_out, "nn", name=f"{tag}_out")
    x1 = _combine(x, y, mod, 2, lnp, 0, f"{tag}_ln")
    return x1, (x, h, p, u2, u3, y)


def _cf_bwd(dx1, saved, mod, lnp, wts, tag, stacks):
    w_in, dw_w, dw_b, cln, w_out = wts
    x, h, p, u2, u3, y = saved
    dxa, dy, dp, dss_later = _sublayer_head_bwd(dx1, x, y, mod, 2, lnp, 0, tag)
    du3 = _mm(dy, w_out, "nt", name=f"{tag}_out_bx")
    _weight_grad(stacks, "cf_w_out", u3, dy, f"{tag}_out_bw")
    du2, dcln = _ln_silu_bwd(u2, cln, du3, f"{tag}_cln_b")
    dval, dgate, ddw_w, ddw_b = _glu_conv_bwd(p, dw_w, du2, f"{tag}_conv_b")
    dpp = jnp.concatenate([dval, dgate], axis=1)
    _weight_grad(stacks, "cf_w_in", h, dpp, f"{tag}_in_bw")
    dh = _mm(dpp, w_in, "nt", name=f"{tag}_in_bx")
    return (x, mod, 1, dh, dxa), (ddw_w, ddw_b, dcln), (dss_later, dp)


def _two_d(a):
    return a.reshape(-1, a.shape[-1])


def kernel(x, c, ada_w, ada_b, ln_g, ln_b, dn_w_in, dn_conv_w, dn_a_log, dn_dt_bias, dn_norm_w, dn_w_out, cf_w_in, cf_dw_w, cf_dw_b, cf_ln_g, cf_ln_b, cf_w_out, ff_w1, ff_w2, loss_target, m_ada_w, m_ada_b, m_ln_g, m_ln_b, m_dn_w_in, m_dn_conv_w, m_dn_a_log, m_dn_dt_bias, m_dn_norm_w, m_dn_w_out, m_cf_w_in, m_cf_dw_w, m_cf_dw_b, m_cf_ln_g, m_cf_ln_b, m_cf_w_out, m_ff_w1, m_ff_w2, v_ada_w, v_ada_b, v_ln_g, v_ln_b, v_dn_w_in, v_dn_conv_w, v_dn_a_log, v_dn_dt_bias, v_dn_norm_w, v_dn_w_out, v_cf_w_in, v_cf_dw_w, v_cf_dw_b, v_cf_ln_g, v_cf_ln_b, v_cf_w_out, v_ff_w1, v_ff_w2):
    ix, iy, ic = lax.axis_index("x"), lax.axis_index("y"), lax.axis_index("c")
    chip = 2 * ix + iy
    dev = 4 * ix + 2 * iy + ic
    S, D = x.shape[1], x.shape[2]
    L = ada_w.shape[0]
    LA, LB = dn_w_in.shape[0], cf_w_in.shape[0]
    H = dn_a_log.shape[1]
    NMOD = ada_b.shape[1] // D
    dn_in = dn_w_in.shape[2] * 4
    n_main = dn_in - 2 * H
    assert L == N_LAYERS and 2 * H <= LANES
    x0, tgt = x[0], loss_target[0]

    small_sharded = [ln_g, ln_b, dn_conv_w, cf_dw_w, cf_dw_b, cf_ln_g, cf_ln_b]
    small_axes = [2, 2, 2, 2, 1, 1, 1]
    packed_small = _pack(small_sharded, rows=8)[None]
    big = {"dn_w_in": (dn_w_in, m_dn_w_in, v_dn_w_in, 0), "dn_w_out": (dn_w_out, m_dn_w_out, v_dn_w_out, 0),
           "cf_w_in": (cf_w_in, m_cf_w_in, v_cf_w_in, 1), "cf_w_out": (cf_w_out, m_cf_w_out, v_cf_w_out, 0),
           "ff_w1": (ff_w1, m_ff_w1, v_ff_w1, 1), "ff_w2": (ff_w2, m_ff_w2, v_ff_w2, 0)}

    def group(g):
        l = g // 2
        if g % 2:
            return {"ff_w1": l, "ff_w2": l}
        mixer = ("dn_w_in", "dn_w_out") if l % 2 == 0 else ("cf_w_in", "cf_w_out")
        return {mixer[0]: l // 2, mixer[1]: l // 2}

    def place(l, dep):
        out = []
        for nm, lw in group(l).items():
            out.append(_place_block(big[nm][0], lw, big[nm][3], chip, dep, f"l{l}_place_{nm}"))
            dep = out[-1]
        return out

    def start_gather(l, placed, after):
        names = list(group(l))
        axes = [big[nm][3] for nm in names]
        send, recv, arrs, token = _gather_start(list(zip(placed, axes)), after, f"l{l}_gather_start")
        return names, axes, arrs, send, recv, token

    def finish_gather(l, pending, after):
        names, axes, arrs, send, recv, _ = pending
        arrs = _gather_wait(list(zip(arrs, axes)), send, recv, after, f"l{l}_gather_wait")
        arrs = _gather_forward(list(zip(arrs, axes)), f"l{l}_gather_pass")
        return dict(zip(names, arrs))

    g_small = _exchange([(packed_small, 0)], "xy", "gather", "gather_small")[0]
    shard_shapes = [a.shape for a in small_sharded]
    per_chip = [_unpack(g_small[q], shard_shapes) for q in range(4)]
    ln_g_f, ln_b_f, conv_w_f, dw_w_f, dw_b_f, cln_g_f, cln_b_f = [
        jnp.concatenate([per_chip[q][i] for q in range(4)], axis=small_axes[i]) for i in range(len(small_sharded))]

    c_all = _exchange([(c[None], 0)], "all", "gather", "gather_cond")[0].reshape(8, D)
    c_pad = jnp.pad(c_all, ((0, 8), (0, 0)))
    mod_sh = jnp.stack([_mm(c_pad, (ada_w, l), "nn", name=f"ada_{l}", a_fn=lambda t: t * _sigmoid(t))
                        for l in range(L)])
    mod_all = _exchange([(mod_sh, 2)], "xy", "gather", "gather_mod")[0]
    mod_mine = lax.dynamic_index_in_dim(mod_all, dev, axis=1, keepdims=False) + ada_b
    mods = mod_mine.reshape(L, NMOD, D)

    def lnp_of(l):
        return jnp.stack([ln_g_f[l, 0], ln_b_f[l, 0], ln_g_f[l, 1], ln_b_f[l, 1]])

    def mixer_wts(l, wl):
        j = l // 2
        if l % 2 == 0:
            w_in = jnp.transpose(wl["dn_w_in"].reshape(4, D, dn_in // 4), (1, 0, 2)).reshape(D, dn_in)
            w_small = jnp.pad(w_in[:, n_main:], ((0, 0), (0, LANES - 2 * H)))
            prm = jnp.zeros((2, LANES), F32).at[0, H:2 * H].set(dn_a_log[j]).at[1, H:2 * H].set(dn_dt_bias[j])
            return (w_in[:, :n_main], w_small, conv_w_f[j], prm, dn_norm_w[j][None], wl["dn_w_out"])
        return (wl["cf_w_in"], dw_w_f[j], dw_b_f[j][None], jnp.stack([cln_g_f[j], cln_b_f[j]]), wl["cf_w_out"])

    xs = x0
    saved, wts, mod_of = [], [], []
    pending = start_gather(0, place(0, mods), mods + jnp.minimum(jnp.abs(g_small[0, 0, 0]), 0.0))
    placed = [None]
    for g in range(1, 2 * L):
        placed.append(place(g, pending[5] if g == 1 else placed[-1][-1]))
    for g in range(2 * L):
        l = g // 2
        wl = finish_gather(g, pending, xs if g else placed[-1][-1])
        mod_g = mods[l]
        if g + 1 < 2 * L:
            pending = start_gather(g + 1, placed[g + 1], next(iter(wl.values())))
            mod_g = mod_g + pending[5][0, 0]
        mod_of.append(mod_g)
        if g % 2:
            wts.append((wl["ff_w1"], wl["ff_w2"]))
            xs, sv = _mlp_fwd(xs, mod_g, lnp_of(l), *wts[g], f"l{l}_ff")
        elif l % 2 == 0:
            wts.append(mixer_wts(l, wl))
            xs, sv = _dn_fwd(xs, mod_g, lnp_of(l), wts[g], H, f"l{l}_dn")
        else:
            wts.append(mixer_wts(l, wl))
            xs, sv = _cf_fwd(xs, mod_g, lnp_of(l), wts[g], f"l{l}_cf")
        saved.append(sv)
    dx, loss_local = _loss_head(xs, tgt, "loss_head")
    loss = lax.psum(loss_local[0, 0], ("x", "y", "c"))

    def start_scatter(l, grads):
        names = list(group(l))
        axes = [big[nm][3] for nm in names]
        send, recv, sums, lands, token = _scatter_start([(grads[nm], ax) for nm, ax in zip(names, axes)], mods,
                                                        f"l{l}_scatter_start")
        return names, axes, sums, lands, send, recv, token

    def finish_scatter(l, pending, after):
        names, axes, sums, lands, send, recv, _ = pending
        sums, lands = _scatter_wait(list(zip(sums, axes)), lands, send, recv, after, f"l{l}_scatter_wait")
        mine = [_sum_landed(s, ax, ld, chip, f"l{l}_sum_grads_{nm}") for nm, ax, s, ld in zip(names, axes, sums, lands)]
        other = _exchange([(s[None], 0) for s in mine], "c", "gather", f"l{l}_swap_sums", keep_own=False)
        return {nm: (s, o) for nm, s, o in zip(names, mine, other)}

    g_dn = [None] * LA
    g_cf = [None] * LB
    dmods, dlns = [None] * L, [None] * L
    big_sums = {}
    pending, token = None, None
    dss_of, dp_of = {}, {}
    for g in reversed(range(2 * L)):
        l, j = g // 2, g // 4
        mod_g = mod_of[g] if token is None else mod_of[g] + token[0, 0]
        grads = {}
        if g % 2:
            dx, (dss_of[g + 1], dp_of[g]) = _mlp_bwd(dx, saved[g], mod_g, lnp_of(l), *wts[g], f"l{l}_ff", grads)
        elif l % 2 == 0:
            dx, g_dn[j], (dss_of[g + 1], dp_of[g]) = _dn_bwd(dx, saved[g], mod_g, lnp_of(l), wts[g], H,
                                                              f"l{l}_dn", grads)
            dn_in_g = jnp.concatenate([g_dn[j][0], g_dn[j][1][:, :2 * H]], axis=1)
            grads["dn_w_in"] = jnp.transpose(dn_in_g.reshape(D, 4, dn_in // 4), (1, 0, 2)).reshape(4 * D, dn_in // 4)
        else:
            dx, g_cf[j], (dss_of[g + 1], dp_of[g]) = _cf_bwd(dx, saved[g], mod_g, lnp_of(l), wts[g],
                                                              f"l{l}_cf", grads)
        if pending is not None:
            for nm, pair_of in finish_scatter(g + 1, pending, dx[3]).items():
                big_sums[nm, group(g + 1)[nm]] = pair_of
        pending = start_scatter(g, grads)
        token = pending[6]
    x_in, mod_in, r_sc_in, dh_in, dxa_in = dx
    dx, dss_of[0] = _modulate_bwd(x_in, mod_in, r_sc_in, dh_in, dxa_in, "l0_dn_mod_b")
    big_names = ["dn_w_in", "dn_w_out", "cf_w_in", "cf_w_out", "ff_w1", "ff_w2"]
    big_res = {nm: None for nm in big_names}

    def update(nm, layer):
        w, m, v, _ = big[nm]
        own, oth = big_sums[nm, layer]
        big_res[nm] = _adamw_pair(w, m, v, layer, own, oth, ic, big_res[nm], token, f"adamw_{nm}_{layer}")

    for nm in big_names:
        for layer in range(big[nm][0].shape[0]):
            if group(0).get(nm) != layer:
                update(nm, layer)
    for nm, pair_of in finish_scatter(0, pending, big_res["ff_w2"][0]).items():
        big_sums[nm, group(0)[nm]] = pair_of
    for nm, layer in group(0).items():
        update(nm, layer)
    big_out = [big_res[nm] for nm in big_names]
    for l in range(L):
        (dss1, dp1), (dss2, dp2) = (dss_of[2 * l], dp_of[2 * l]), (dss_of[2 * l + 1], dp_of[2 * l + 1])
        dmods[l] = jnp.concatenate([dss1, dp1[0:1], dss2, dp2[0:1]], axis=0)
        dlns[l] = (jnp.stack([dp1[1], dp2[1]]), jnp.stack([dp1[2], dp2[2]]))
    grad_x = dx[None]

    d_ln_g = jnp.stack([dlns[l][0] for l in range(L)])
    d_ln_b = jnp.stack([dlns[l][1] for l in range(L)])
    d_conv_w = jnp.stack([g_dn[j][2] for j in range(LA)])
    d_a_log = jnp.stack([g_dn[j][3][0, H:2 * H] for j in range(LA)])
    d_dt_bias = jnp.stack([g_dn[j][3][1, H:2 * H] for j in range(LA)])
    d_norm_w = jnp.stack([g_dn[j][4][0] for j in range(LA)])
    d_dw_w = jnp.stack([g_cf[j][0] for j in range(LB)])
    d_dw_b = jnp.stack([g_cf[j][1][0] for j in range(LB)])
    d_cln_g = jnp.stack([g_cf[j][2][0] for j in range(LB)])
    d_cln_b = jnp.stack([g_cf[j][2][1] for j in range(LB)])
    d_mod = jnp.stack(dmods).reshape(L, NMOD * D)
    small_full = [d_mod, d_ln_g, d_ln_b, d_conv_w, d_dw_w, d_dw_b, d_cln_g, d_cln_b, d_a_log, d_dt_bias, d_norm_w]
    landed = jnp.minimum(jnp.abs(big_sums["dn_w_out", 0][0][0, 0]), 0.0)
    small_all = _exchange([(_pack(small_full, rows=8)[None] + landed, 0)], "all", "gather", "gather_small_grads")[0]
    small_sum = _sum_slots(small_all, "sum_small_grads")
    (s_ada_b, s_ln_g, s_ln_b, s_conv_w, s_dw_w, s_dw_b, s_cln_g, s_cln_b, s_a_log, s_dt_bias, s_norm_w) = _unpack(
        small_sum, [a.shape for a in small_full])
    d_mod_all = small_all.reshape(8, -1)[:, :L * NMOD * D].reshape(8, L, NMOD * D)

    def shard(a, axis):
        size = a.shape[axis] // 4
        return lax.dynamic_slice_in_dim(a, chip * size, size, axis)

    ncol = ada_w.shape[2]
    d_mod_sh = jnp.pad(lax.dynamic_slice_in_dim(d_mod_all, chip * ncol, ncol, 2), ((0, 8), (0, 0), (0, 0)))
    g_ada_w = None
    for l in range(L):
        g_ada_w = _mm(c_pad, d_mod_sh[:, l], "tn", name=f"ada_bw_{l}", a_fn=lambda t: t * _sigmoid(t),
                      stack=(g_ada_w, l, L))

    ada_out =[r.reshape(ada_w.shape) for r in _adamw(_two_d(ada_w), _two_d(m_ada_w), _two_d(v_ada_w),
                                                     _two_d(g_ada_w)[None], "adamw_ada_w")]

    small_w = [(ada_b, m_ada_b, v_ada_b, s_ada_b), (ln_g, m_ln_g, v_ln_g, shard(s_ln_g, 2)),
               (ln_b, m_ln_b, v_ln_b, shard(s_ln_b, 2)), (dn_conv_w, m_dn_conv_w, v_dn_conv_w, shard(s_conv_w, 2)),
               (dn_a_log, m_dn_a_log, v_dn_a_log, s_a_log), (dn_dt_bias, m_dn_dt_bias, v_dn_dt_bias, s_dt_bias),
               (dn_norm_w, m_dn_norm_w, v_dn_norm_w, s_norm_w), (cf_dw_w, m_cf_dw_w, v_cf_dw_w, shard(s_dw_w, 2)),
               (cf_dw_b, m_cf_dw_b, v_cf_dw_b, shard(s_dw_b, 1)), (cf_ln_g, m_cf_ln_g, v_cf_ln_g, shard(s_cln_g, 1)),
               (cf_ln_b, m_cf_ln_b, v_cf_ln_b, shard(s_cln_b, 1))]
    pk = [_pack([t[i] for t in small_w], rows=8) for i in range(4)]
    small_res = _adamw(pk[0], pk[1], pk[2], pk[3][None], "adamw_small")
    small_shapes = [t[0].shape for t in small_w]
    small_out = [_unpack(r, small_shapes) for r in small_res]

    def kind(k):
        sm = small_out[k]
        bg = [o[k] for o in big_out]
        return [ada_out[k], sm[0], sm[1], sm[2], bg[0], sm[3], sm[4], sm[5], sm[6], bg[1],
                bg[2], sm[7], sm[8], sm[9], sm[10], bg[3], bg[4], bg[5]]

    return (loss, grad_x, *kind(0), *kind(1), *kind(2), *kind(3))
```

```python
import functools

import jax
import jax.numpy as jnp
from jax import lax
from jax.experimental import pallas as pl
from jax.experimental.pallas import tpu as pltpu

F32 = jnp.float32
_MXU = jnp.bfloat16
_HI = lax.Precision.HIGHEST

N_LAYERS = 4
ALPHA = (2.0 * N_LAYERS) ** 0.25
LN_EPS = 1e-5
RMS_EPS = 1e-6
L2_EPS = 1e-6
CHUNK = 64
ADAM_LR, ADAM_B1, ADAM_B2, ADAM_EPS, ADAM_WD, ADAM_STEP = 0.001, 0.9, 0.999, 1e-08, 0.01, 10

LANES = 128
TOKEN_BLOCK = 256
VMEM_BIG = 48 * 1024 * 1024
MM_TILE = 1024

SDS = jax.ShapeDtypeStruct
MESH = pl.DeviceIdType.MESH


def _cparams(vmem=None):
    if vmem is None:
        return None
    return pltpu.CompilerParams(vmem_limit_bytes=vmem)


def _pcall(body, **kw):
    if kw.get("compiler_params", 1) is None:
        kw.pop("compiler_params")
    return pl.pallas_call(body, **kw)


def _full(arr):
    nd = arr.ndim
    return pl.BlockSpec(arr.shape, lambda *g: (0,) * nd)


def _bs(block, imap, lead=None):
    if lead is None:
        return pl.BlockSpec(block, imap)
    return pl.BlockSpec((None,) + tuple(block), lambda *g: (lead,) + tuple(imap(*g)))


def _split(a):
    return a if isinstance(a, tuple) else (a, None)


_GROUPS = {
    "xy": ([(1, 0, 0), (0, 1, 0), (1, 1, 0)], 4),
    "c": ([(0, 0, 1)], 2),
    "all": ([(1, 0, 0), (0, 1, 0), (1, 1, 0), (0, 0, 1), (1, 0, 1), (0, 1, 1), (1, 1, 1)], 8),
}


def _exchange(items, group, mode, name, nsplit=1, keep_own=True):
    masks, n = _GROUPS[group]
    npeer = len(masks)
    ni = len(items)
    arrs = [a for a, _ in items]
    out_shapes = []
    for a, ax in items:
        shp = list(a.shape)
        if mode == "gather":
            shp[ax] *= n
        else:
            shp[ax] //= n
            shp = [n] + shp
        out_shapes.append(SDS(tuple(shp), a.dtype))

    def body(*refs):
        ins, outs = refs[:ni], refs[ni:2 * ni]
        send_sems, recv_sems, local_sems = refs[2 * ni:]
        x, y, c = lax.axis_index("x"), lax.axis_index("y"), lax.axis_index("c")

        def slot(px, py, pc):
            if group == "xy":
                return 2 * px + py
            if group == "c":
                return pc
            return 4 * px + 2 * py + pc

        me = slot(x, y, c)

        def block(ref, ax, idx, size):
            ix = (slice(None),) * ax + (pl.ds(pl.multiple_of(idx * size, size), size),)
            return ref.at[ix]

        copies = []
        for it, (a, ax) in enumerate(items):
            in_ref, out_ref = ins[it], outs[it]
            if mode == "gather":
                size = a.shape[ax]
                src_own, dst_own = in_ref, block(out_ref, ax, me, size)
            else:
                size = a.shape[ax] // n
                src_own, dst_own = block(in_ref, ax, me, size), out_ref.at[me]
            sax, ns, cs = splits[it]
            pieces = [(slice(None),) * sax + (pl.ds(j * cs, cs),) for j in range(ns)]
            if keep_own:
                for j, piece in enumerate(pieces):
                    own = pltpu.make_async_copy(src_own.at[piece], dst_own.at[piece], local_sems.at[it * nsplit + j])
                    own.start()
                    copies.append(own)
            for k, m in enumerate(masks):
                peer = tuple((1 - v) if b else v for v, b in zip((x, y, c), m))
                if mode == "gather":
                    src, dst = in_ref, dst_own
                else:
                    src, dst = block(in_ref, ax, slot(*peer), size), out_ref.at[me]
                for j, piece in enumerate(pieces):
                    sem = (it * npeer + k) * nsplit + j
                    cp = pltpu.make_async_remote_copy(
                        src_ref=src.at[piece], dst_ref=dst.at[piece], send_sem=send_sems.at[sem],
                        recv_sem=recv_sems.at[sem], device_id=peer, device_id_type=MESH)
                    cp.start()
                    copies.append(cp)
        for cp in copies:
            cp.wait()

    splits = []
    for (a, ax), o in zip(items, out_shapes):
        bshape = a.shape if mode == "gather" else o.shape[1:]
        sax = max(range(len(bshape) - 1), key=lambda d: bshape[d])
        ns = nsplit if bshape[sax] % (nsplit * 16) == 0 else 1
        splits.append((sax, ns, bshape[sax] // ns))
    any_spec = pl.BlockSpec(memory_space=pl.ANY)
    nsem = ni * npeer * nsplit
    outs = _pcall(
        body, name=name, out_shape=tuple(out_shapes),
        in_specs=[any_spec] * ni, out_specs=tuple([any_spec] * ni),
        scratch_shapes=[pltpu.SemaphoreType.DMA((nsem,)), pltpu.SemaphoreType.DMA((nsem,)),
                        pltpu.SemaphoreType.DMA((ni * nsplit,))],
    )(*arrs)
    return list(outs)


_HBM = pl.BlockSpec(memory_space=pltpu.HBM)
_SEM = pl.BlockSpec(memory_space=pltpu.SEMAPHORE)
_ANY = pl.BlockSpec(memory_space=pl.ANY)
_SPLIT = pltpu.CompilerParams(has_side_effects=pltpu.SideEffectType.DATAFLOW_SIDE_EFFECTING)
_XY = _GROUPS["xy"][0]


def _in_hbm(a):
    return pltpu.with_memory_space_constraint(a, pltpu.HBM)


def _chip_peers():
    x, y, c = lax.axis_index("x"), lax.axis_index("y"), lax.axis_index("c")
    return x, y, c, [tuple((1 - v) if b else v for v, b in zip((x, y), m)) for m in _XY]


def _wblock(ref, ax, half, chip):
    R, C = ref.shape
    if ax == 0:
        rows = R // 8
        return ref.at[pl.ds(pl.multiple_of(chip * (2 * rows) + half * rows, rows), rows), :]
    rows, cols = R // 2, C // 4
    return ref.at[pl.ds(pl.multiple_of(half * rows, rows), rows), pl.ds(pl.multiple_of(chip * cols, cols), cols)]


def _gather_start(items, after, name):
    ni = len(items)
    arrs = [a for a, _ in items]

    def body(*refs):
        send_sems, recv_sems = refs[ni + 1], refs[ni + 2]
        outs, token = refs[ni + 3:2 * ni + 3], refs[2 * ni + 3]
        x, y, c, peers = _chip_peers()
        for it, (_, ax) in enumerate(items):
            mine = _wblock(outs[it], ax, c, 2 * x + y)
            for k, (px, py) in enumerate(peers):
                pltpu.make_async_remote_copy(
                    src_ref=mine, dst_ref=mine, send_sem=send_sems.at[it * 3 + k], recv_sem=recv_sems.at[it * 3 + k],
                    device_id=(px, py, c), device_id_type=MESH).start()
        token[...] = jnp.zeros_like(token)

    res = _pcall(
        body, name=name,
        out_shape=(pltpu.SemaphoreType.DMA((ni * 3,)), pltpu.SemaphoreType.DMA((ni * 3,)),
                   *[pltpu.HBM(a.shape, a.dtype) for a in arrs], SDS((8, LANES), F32)),
        in_specs=[_HBM] * ni + [_ANY],
        out_specs=(_SEM, _SEM, *[_HBM] * ni, pl.BlockSpec(memory_space=pltpu.VMEM)),
        input_output_aliases={i: 2 + i for i in range(ni)}, compiler_params=_SPLIT,
    )(*[_in_hbm(a) for a in arrs], after)
    return res[0], res[1], list(res[2:2 + ni]), res[2 + ni]


def _gather_wait(items, send_sems, recv_sems, after, name):
    ni = len(items)
    arrs = [a for a, _ in items]

    def body(*refs):
        s_sems, r_sems = refs[ni], refs[ni + 1]
        outs = refs[ni + 3:]
        x, y, c, peers = _chip_peers()
        for it, (_, ax) in enumerate(items):
            mine = _wblock(outs[it], ax, c, 2 * x + y)
            for k, (px, py) in enumerate(peers):
                cp = pltpu.make_async_remote_copy(
                    src_ref=mine, dst_ref=_wblock(outs[it], ax, c, 2 * px + py), send_sem=s_sems.at[it * 3 + k],
                    recv_sem=r_sems.at[it * 3 + k], device_id=(px, py, c), device_id_type=MESH)
                cp.wait_send()
                cp.wait_recv()

    res = _pcall(
        body, name=name, out_shape=tuple(pltpu.HBM(a.shape, a.dtype) for a in arrs),
        in_specs=[_HBM] * ni + [_SEM, _SEM, _ANY], out_specs=tuple([_HBM] * ni),
        input_output_aliases={i: i for i in range(ni)}, compiler_params=_SPLIT,
    )(*arrs, send_sems, recv_sems, after)
    return list(res)


def _gather_forward(items, name):
    ni = len(items)
    arrs = [a for a, _ in items]

    def body(*refs):
        outs = refs[ni:2 * ni]
        send_sems, recv_sems = refs[2 * ni:]
        x, y, c, peers = _chip_peers()
        def copy(it, ax, k, chip, dst_half):
            return pltpu.make_async_remote_copy(
                src_ref=_wblock(outs[it], ax, c, chip), dst_ref=_wblock(outs[it], ax, dst_half, chip),
                send_sem=send_sems.at[it * 3 + k], recv_sem=recv_sems.at[it * 3 + k],
                device_id=(x, y, 1 - c), device_id_type=MESH)

        for it, (_, ax) in enumerate(items):
            for k, (px, py) in enumerate(peers):
                copy(it, ax, k, 2 * px + py, c).start()
        for it, (_, ax) in enumerate(items):
            for k, (px, py) in enumerate(peers):
                copy(it, ax, k, 2 * px + py, c).wait_send()
                copy(it, ax, k, 2 * px + py, 1 - c).wait_recv()

    res = _pcall(
        body, name=name, out_shape=tuple(SDS(a.shape, a.dtype) for a in arrs),
        in_specs=[_ANY] * ni, out_specs=tuple([_ANY] * ni), input_output_aliases={i: i for i in range(ni)},
        scratch_shapes=[pltpu.SemaphoreType.DMA((ni * 3,)), pltpu.SemaphoreType.DMA((ni * 3,))],
    )(*arrs)
    return list(res)


def _gblock(ref, ax, chip):
    size = ref.shape[ax] // 4
    piece = pl.ds(pl.multiple_of(chip * size, size), size)
    return ref.at[piece, :] if ax == 0 else ref.at[:, piece]


def _scatter_start(items, after, name):
    ni = len(items)
    arrs = [a for a, _ in items]
    lands = []
    for a, ax in items:
        blk = (a.shape[0] // 4, a.shape[1]) if ax == 0 else (a.shape[0], a.shape[1] // 4)
        lands.append(lax.empty((4, *blk), a.dtype))

    def body(*refs):
        send_sems, recv_sems = refs[2 * ni + 1], refs[2 * ni + 2]
        srcs, dsts = refs[2 * ni + 3:3 * ni + 3], refs[3 * ni + 3:4 * ni + 3]
        token = refs[4 * ni + 3]
        x, y, c, peers = _chip_peers()
        for it, (_, ax) in enumerate(items):
            for k, (px, py) in enumerate(peers):
                pltpu.make_async_remote_copy(
                    src_ref=_gblock(srcs[it], ax, 2 * px + py), dst_ref=dsts[it].at[2 * x + y],
                    send_sem=send_sems.at[it * 3 + k], recv_sem=recv_sems.at[it * 3 + k],
                    device_id=(px, py, c), device_id_type=MESH).start()
        token[...] = jnp.zeros_like(token)

    res = _pcall(
        body, name=name,
        out_shape=(pltpu.SemaphoreType.DMA((ni * 3,)), pltpu.SemaphoreType.DMA((ni * 3,)),
                   *[pltpu.HBM(a.shape, a.dtype) for a in arrs], *[pltpu.HBM(a.shape, a.dtype) for a in lands],
                   SDS((8, LANES), F32)),
        in_specs=[_HBM] * (2 * ni) + [_ANY],
        out_specs=(_SEM, _SEM, *[_HBM] * (2 * ni), pl.BlockSpec(memory_space=pltpu.VMEM)),
        input_output_aliases={i: 2 + i for i in range(2 * ni)}, compiler_params=_SPLIT,
    )(*[_in_hbm(a) for a in arrs], *[_in_hbm(a) for a in lands], after)
    return res[0], res[1], list(res[2:2 + ni]), list(res[2 + ni:2 + 2 * ni]), res[2 + 2 * ni]


def _scatter_wait(items, lands, send_sems, recv_sems, after, name):
    ni = len(items)
    arrs = [a for a, _ in items]

    def body(*refs):
        s_sems, r_sems = refs[2 * ni], refs[2 * ni + 1]
        srcs, dsts = refs[2 * ni + 3:3 * ni + 3], refs[3 * ni + 3:]
        x, y, c, peers = _chip_peers()
        for it, (_, ax) in enumerate(items):
            for k, (px, py) in enumerate(peers):
                cp = pltpu.make_async_remote_copy(
                    src_ref=_gblock(srcs[it], ax, 2 * px + py), dst_ref=dsts[it].at[2 * px + py],
                    send_sem=s_sems.at[it * 3 + k], recv_sem=r_sems.at[it * 3 + k],
                    device_id=(px, py, c), device_id_type=MESH)
                cp.wait_send()
                cp.wait_recv()

    res = _pcall(
        body, name=name, out_shape=tuple(pltpu.HBM(a.shape, a.dtype) for a in arrs + list(lands)),
        in_specs=[_HBM] * (2 * ni) + [_SEM, _SEM, _ANY], out_specs=tuple([_HBM] * (2 * ni)),
        input_output_aliases={i: i for i in range(2 * ni)}, compiler_params=_SPLIT,
    )(*arrs, *lands, send_sems, recv_sems, after)
    return list(res[:ni]), list(res[ni:])


def _tile(n, cap):
    if n <= cap:
        return n
    t = cap - cap % LANES
    while n % t:
        t -= LANES
    return t


def _mm(a, b, mode, *, name, out_dtypes=(F32,), tm=MM_TILE, tn=MM_TILE, tk=MM_TILE, a_fn=None, out_fn=None, aux=(),
        stack=None):
    (a, a_lead), (b, b_lead) = _split(a), _split(b)
    ash, bsh = a.shape[-2:], b.shape[-2:]
    if mode == "nn":
        (M, K), (_, N) = ash, bsh
    elif mode == "nt":
        (M, K), (N, _) = ash, bsh
    else:
        (K, M), (_, N) = ash, bsh
    tm, tn, tk = _tile(M, tm), _tile(N, tn), _tile(K, tk)
    nk = K // tk
    if mode == "tn":
        a_spec = _bs((tk, tm), lambda i, j, k: (k, i), a_lead)
    else:
        a_spec = _bs((tm, tk), lambda i, j, k: (i, k), a_lead)
    if mode == "nt":
        b_spec = _bs((tn, tk), lambda i, j, k: (j, k), b_lead)
    else:
        b_spec = _bs((tk, tn), lambda i, j, k: (k, j), b_lead)
    aux_arrs, aux_specs = [], []
    for arr, kind in aux:
        arr, lead = _split(arr)
        aux_arrs.append(arr)
        if kind == "mn":
            aux_specs.append(_bs((tm, tn), lambda i, j, k: (i, j), lead))
        else:
            aux_specs.append(_bs((1, tn), lambda i, j, k: (0, j), lead))
    na, no = len(aux_arrs), len(out_dtypes)
    dims = {"nn": (((1,), (0,)), ((), ())), "nt": (((1,), (1,)), ((), ())), "tn": (((0,), (0,)), ((), ()))}[mode]

    def finish(r, aux_refs, o_refs):
        outs = out_fn(r, *[x[...] for x in aux_refs]) if out_fn is not None else (r,)
        for o_ref, val in zip(o_refs, outs):
            o_ref[...] = val.astype(o_ref.dtype)

    def product(a_ref, b_ref):
        av = a_ref[...]
        if a_fn is not None:
            av = a_fn(av.astype(F32))
        return lax.dot_general(av.astype(_MXU), b_ref[...].astype(_MXU), dims, preferred_element_type=F32)

    nbuf = 0 if stack is None or stack[0] is None else 1

    def body_one(a_ref, b_ref, *rest):
        finish(product(a_ref, b_ref), rest[:na], rest[na + nbuf:na + nbuf + no])

    def body_acc(a_ref, b_ref, *rest):
        aux_refs, o_refs, acc = rest[:na], rest[na + nbuf:na + nbuf + no], rest[na + nbuf + no]
        k = pl.program_id(2)

        @pl.when(k == 0)
        def _():
            acc[...] = product(a_ref, b_ref)

        @pl.when(k != 0)
        def _():
            acc[...] += product(a_ref, b_ref)

        @pl.when(k == nk - 1)
        def _():
            finish(acc[...], aux_refs, o_refs)

    extra, aliases = {}, []
    if stack is None:
        out_shape = (M, N)
        o_spec = pl.BlockSpec((tm, tn), lambda i, j, k: (i, j))
    else:
        buf, layer, n_layers = stack
        assert no == 1
        out_shape = (n_layers, M, N)
        o_spec = pl.BlockSpec((None, tm, tn), lambda i, j, k: (layer, i, j))
        if buf is not None:
            aliases = [buf]
            extra = dict(input_output_aliases={2 + na: 0})
    outs = _pcall(
        body_one if nk == 1 else body_acc, name=name, grid=(M // tm, N // tn, nk),
        in_specs=[a_spec, b_spec] + aux_specs + [pl.BlockSpec(memory_space=pl.ANY)] * nbuf,
        out_specs=tuple([o_spec] * no),
        out_shape=tuple(SDS(out_shape, dt) for dt in out_dtypes),
        scratch_shapes=[] if nk == 1 else [pltpu.VMEM((tm, tn), F32)],
        compiler_params=pltpu.CompilerParams(dimension_semantics=("parallel", "parallel", "arbitrary"),
                                             vmem_limit_bytes=VMEM_BIG),
        **extra,
    )(a, b, *aux_arrs, *aliases)
    return outs[0] if no == 1 else outs


def _tok(S):
    ts = min(TOKEN_BLOCK, S)
    assert S % ts == 0
    return ts


def _row(ts, D):
    return pl.BlockSpec((ts, D), lambda i: (i, 0))


def _acc_rows(ref, i, rows):
    @pl.when(i == 0)
    def _():
        for r, v in enumerate(rows):
            ref[r:r + 1, :] = v

    @pl.when(i != 0)
    def _():
        for r, v in enumerate(rows):
            ref[r:r + 1, :] += v


def _modulate(x, mod, r_sh, r_sc, name):
    S, D = x.shape
    ts = _tok(S)

    def body(x_ref, m_ref, o_ref):
        o_ref[...] = (x_ref[...] * (1.0 + m_ref[r_sc:r_sc + 1, :]) + m_ref[r_sh:r_sh + 1, :]).astype(o_ref.dtype)

    return _pcall(body, name=name, grid=(S // ts,), in_specs=[_row(ts, D), _full(mod)],
                  out_specs=_row(ts, D), out_shape=SDS((S, D), _MXU))(x, mod)


def _modulate_bwd(x, mod, r_sc, dh, dxa, name):
    S, D = x.shape
    ts = _tok(S)

    def body(x_ref, m_ref, dh_ref, dxa_ref, dx_ref, dss_ref):
        dh_v = dh_ref[...]
        dx_ref[...] = dxa_ref[...] + dh_v * (1.0 + m_ref[r_sc:r_sc + 1, :])
        _acc_rows(dss_ref, pl.program_id(0),
                  [jnp.sum(dh_v, axis=0, keepdims=True), jnp.sum(dh_v * x_ref[...], axis=0, keepdims=True)])

    return _pcall(body, name=name, grid=(S // ts,),
                  in_specs=[_row(ts, D), _full(mod), _row(ts, D), _row(ts, D)],
                  out_specs=(_row(ts, D), pl.BlockSpec((2, D), lambda i: (0, 0))),
                  out_shape=(SDS((S, D), F32), SDS((2, D), F32)))(x, mod, dh, dxa)


def _norm_stats(z):
    mu = jnp.mean(z, axis=-1, keepdims=True)
    zc = z - mu
    var = jnp.mean(zc * zc, axis=-1, keepdims=True)
    rstd = lax.rsqrt(var + LN_EPS)
    return zc * rstd, rstd


def _norm_bwd(dxhat, xhat, rstd):
    return rstd * (dxhat - jnp.mean(dxhat, axis=-1, keepdims=True)
                   - xhat * jnp.mean(dxhat * xhat, axis=-1, keepdims=True))


def _combine(x, y, mod, r_gt, lnp, r_g, name, nxt=None):
    S, D = x.shape
    ts = _tok(S)

    def body(x_ref, y_ref, m_ref, l_ref, *rest):
        z = ALPHA * x_ref[...] + (1.0 + m_ref[r_gt:r_gt + 1, :]) * y_ref[...]
        xhat, _ = _norm_stats(z)
        out = xhat * l_ref[r_g:r_g + 1, :] + l_ref[r_g + 1:r_g + 2, :]
        if nxt is None:
            rest[0][...] = out
        else:
            n_ref, o_ref, h_ref = rest
            o_ref[...] = out
            h_ref[...] = (out * (1.0 + n_ref[nxt[2]:nxt[2] + 1, :]) + n_ref[nxt[1]:nxt[1] + 1, :]).astype(h_ref.dtype)

    if nxt is None:
        return _pcall(body, name=name, grid=(S // ts,), in_specs=[_row(ts, D), _row(ts, D), _full(mod), _full(lnp)],
                      out_specs=_row(ts, D), out_shape=SDS((S, D), F32))(x, y, mod, lnp), None
    return _pcall(body, name=name, grid=(S // ts,),
                  in_specs=[_row(ts, D), _row(ts, D), _full(mod), _full(lnp), _full(nxt[0])],
                  out_specs=(_row(ts, D), _row(ts, D)),
                  out_shape=(SDS((S, D), F32), SDS((S, D), _MXU)))(x, y, mod, lnp, nxt[0])


def _combine_bwd(x, y, mod, r_gt, lnp, r_g, dout, name):
    S, D = x.shape
    ts = _tok(S)

    def body(x_ref, y_ref, m_ref, l_ref, do_ref, dxa_ref, dy_ref, dp_ref):
        gate = 1.0 + m_ref[r_gt:r_gt + 1, :]
        y_v, do_v = y_ref[...], do_ref[...]
        xhat, rstd = _norm_stats(ALPHA * x_ref[...] + gate * y_v)
        dz = _norm_bwd(do_v * l_ref[r_g:r_g + 1, :], xhat, rstd)
        dxa_ref[...] = ALPHA * dz
        dy_ref[...] = (gate * dz).astype(dy_ref.dtype)
        _acc_rows(dp_ref, pl.program_id(0),
                  [jnp.sum(dz * y_v, axis=0, keepdims=True), jnp.sum(do_v * xhat, axis=0, keepdims=True),
                   jnp.sum(do_v, axis=0, keepdims=True)])

    return _pcall(body, name=name, grid=(S // ts,),
                  in_specs=[_row(ts, D), _row(ts, D), _full(mod), _full(lnp), _row(ts, D)],
                  out_specs=(_row(ts, D), _row(ts, D), pl.BlockSpec((3, D), lambda i: (0, 0))),
                  out_shape=(SDS((S, D), F32), SDS((S, D), _MXU), SDS((3, D), F32)))(x, y, mod, lnp, dout)


def _modcomb_bwd(later, x, y, mod, r_gt, lnp, r_g, name):
    xn, mod_n, r_sc_n, dh, dxa_n = later
    S, D = x.shape
    ts = _tok(S)

    def body(xn_ref, mn_ref, dh_ref, dxan_ref, x_ref, y_ref, m_ref, l_ref, dxa_ref, dy_ref, dp_ref, dss_ref):
        i = pl.program_id(0)
        dh_v = dh_ref[...]
        do_v = dxan_ref[...] + dh_v * (1.0 + mn_ref[r_sc_n:r_sc_n + 1, :])
        _acc_rows(dss_ref, i, [jnp.sum(dh_v, axis=0, keepdims=True),
                               jnp.sum(dh_v * xn_ref[...], axis=0, keepdims=True)])
        gate = 1.0 + m_ref[r_gt:r_gt + 1, :]
        y_v = y_ref[...]
        xhat, rstd = _norm_stats(ALPHA * x_ref[...] + gate * y_v)
        dz = _norm_bwd(do_v * l_ref[r_g:r_g + 1, :], xhat, rstd)
        dxa_ref[...] = ALPHA * dz
        dy_ref[...] = (gate * dz).astype(dy_ref.dtype)
        _acc_rows(dp_ref, i, [jnp.sum(dz * y_v, axis=0, keepdims=True), jnp.sum(do_v * xhat, axis=0, keepdims=True),
                              jnp.sum(do_v, axis=0, keepdims=True)])

    row = _row(ts, D)
    return _pcall(body, name=name, grid=(S // ts,),
                  in_specs=[row, _full(mod_n), row, row, row, row, _full(mod), _full(lnp)],
                  out_specs=(row, row, pl.BlockSpec((3, D), lambda i: (0, 0)), pl.BlockSpec((2, D), lambda i: (0, 0))),
                  out_shape=(SDS((S, D), F32), SDS((S, D), _MXU), SDS((3, D), F32), SDS((2, D), F32)))(
        xn, mod_n, dh, dxa_n, x, y, mod, lnp)


def _sublayer_head_bwd(dout, x, y, mod, r_gt, lnp, r_g, tag):
    if isinstance(dout, tuple):
        return _modcomb_bwd(dout, x, y, mod, r_gt, lnp, r_g, f"{tag}_ln_b")
    return (*_combine_bwd(x, y, mod, r_gt, lnp, r_g, dout, f"{tag}_ln_b"), None)


def _sigmoid(t):
    return 1.0 / (1.0 + jnp.exp(-t))


def _ln_silu(u, lnp, name):
    S, D = u.shape
    ts = _tok(S)

    def body(u_ref, l_ref, o_ref):
        xhat, _ = _norm_stats(u_ref[...])
        t = xhat * l_ref[0:1, :] + l_ref[1:2, :]
        o_ref[...] = (t * _sigmoid(t)).astype(o_ref.dtype)

    return _pcall(body, name=name, grid=(S // ts,), in_specs=[_row(ts, D), _full(lnp)],
                  out_specs=_row(ts, D), out_shape=SDS((S, D), _MXU))(u, lnp)


def _ln_silu_bwd(u, lnp, dout, name):
    S, D = u.shape
    ts = _tok(S)

    def body(u_ref, l_ref, do_ref, du_ref, dp_ref):
        xhat, rstd = _norm_stats(u_ref[...])
        g = l_ref[0:1, :]
        t = xhat * g + l_ref[1:2, :]
        sg = _sigmoid(t)
        dt = do_ref[...] * (sg * (1.0 + t * (1.0 - sg)))
        du_ref[...] = _norm_bwd(dt * g, xhat, rstd)
        _acc_rows(dp_ref, pl.program_id(0),
                  [jnp.sum(dt * xhat, axis=0, keepdims=True), jnp.sum(dt, axis=0, keepdims=True)])

    return _pcall(body, name=name, grid=(S // ts,), in_specs=[_row(ts, D), _full(lnp), _row(ts, D)],
                  out_specs=(_row(ts, D), pl.BlockSpec((2, D), lambda i: (0, 0))),
                  out_shape=(SDS((S, D), F32), SDS((2, D), F32)))(u, lnp, dout)


def _loss_head(xf, tgt, name):
    S, D = xf.shape
    ts = _tok(S)

    def body(x_ref, t_ref, dx_ref, l_ref):
        err = x_ref[...] - t_ref[...]
        dx_ref[...] = err * (1.0 / D)
        part = jnp.sum(jnp.sum(err * err, axis=1, keepdims=True), axis=0, keepdims=True) * (0.5 / D)

        @pl.when(pl.program_id(0) == 0)
        def _():
            l_ref[...] = part

        @pl.when(pl.program_id(0) != 0)
        def _():
            l_ref[...] += part

    return _pcall(body, name=name, grid=(S // ts,), in_specs=[_row(ts, D), _row(ts, D)],
                  out_specs=(_row(ts, D), pl.BlockSpec((1, 1), lambda i: (0, 0))),
                  out_shape=(SDS((S, D), F32), SDS((1, 1), F32)))(xf, tgt)


CONV_PAD = 32


def _padded(u, before):
    zeros = jnp.zeros((CONV_PAD, u.shape[1]), u.dtype)
    return jnp.concatenate([zeros, u] if before else [u, zeros], axis=0)


SUBLANES = 8


def _shift_down(u_pad, s, rolled):
    q, r = divmod(s, SUBLANES)
    if r not in rolled:
        rolled[r] = u_pad if r == 0 else pltpu.roll(u_pad, r, 0)
    start = CONV_PAD - SUBLANES * q
    return rolled[r][start:start + u_pad.shape[0] - CONV_PAD]


def _shift_up(u_pad, s, rolled):
    n = u_pad.shape[0]
    q, r = divmod(s, SUBLANES)
    if r not in rolled:
        rolled[r] = u_pad if r == 0 else pltpu.roll(u_pad, n - r, 0)
    return rolled[r][SUBLANES * q:SUBLANES * q + n - CONV_PAD]


def _dwconv(u, w_ref, taps, rows):
    del rows
    assert taps - 1 <= CONV_PAD
    u_pad, rolled = _padded(u, True), {}
    acc = jnp.zeros_like(u)
    for j in range(taps):
        acc = acc + w_ref[j:j + 1, :] * _shift_down(u_pad, taps - 1 - j, rolled)
    return acc


def _dwconv_bwd(u, dy, w_ref, dw_ref, taps, rows):
    del rows
    assert taps - 1 <= CONV_PAD
    u_pad, dy_pad, u_rolled, dy_rolled = _padded(u, True), _padded(dy, False), {}, {}
    du = jnp.zeros_like(u)
    for j in range(taps):
        s = taps - 1 - j
        du = du + w_ref[j:j + 1, :] * _shift_up(dy_pad, s, dy_rolled)
        dw_ref[j:j + 1, :] = jnp.sum(dy * _shift_down(u_pad, s, u_rolled), axis=0, keepdims=True)
    return du


def _col(S, j0=0):
    return pl.BlockSpec((S, LANES), lambda j: (0, j + j0))


def _conv_silu(pm, w, nblk, name):
    S = pm.shape[0]
    taps = w.shape[0]

    def body(u_ref, w_ref, o_ref):
        rows = lax.broadcasted_iota(jnp.int32, (S, LANES), 0)
        cv = _dwconv(u_ref[...], w_ref, taps, rows)
        o_ref[...] = cv * _sigmoid(cv)

    return _pcall(body, name=name, grid=(nblk,),
                  in_specs=[_col(S), pl.BlockSpec((taps, LANES), lambda j: (0, j))],
                  out_specs=_col(S), out_shape=SDS((S, nblk * LANES), F32),
                  compiler_params=_cparams(VMEM_BIG))(pm, w)


def _conv_silu_bwd(pm, w, dout, dpm, j0, name):
    S = pm.shape[0]
    taps = w.shape[0]
    nblk = dout.shape[1] // LANES

    def body(u_ref, w_ref, do_ref, dpm_in, du_ref, dw_ref):
        del dpm_in
        rows = lax.broadcasted_iota(jnp.int32, (S, LANES), 0)
        u = u_ref[...]
        cv = _dwconv(u, w_ref, taps, rows)
        sg = _sigmoid(cv)
        dc = do_ref[...] * (sg * (1.0 + cv * (1.0 - sg)))
        du_ref[...] = _dwconv_bwd(u, dc, w_ref, dw_ref, taps, rows)

    return _pcall(body, name=name, grid=(nblk,),
                  in_specs=[_col(S, j0), pl.BlockSpec((taps, LANES), lambda j: (0, j + j0)), _col(S),
                            pl.BlockSpec(memory_space=pl.ANY)],
                  out_specs=(_col(S, j0), pl.BlockSpec((taps, LANES), lambda j: (0, j))),
                  out_shape=(SDS(dpm.shape, F32), SDS((taps, nblk * LANES), F32)),
                  input_output_aliases={3: 0},
                  compiler_params=_cparams(VMEM_BIG))(pm, w, dout, dpm)


def _glu_conv(p, w, bias, name):
    S, C2 = p.shape
    nblk = C2 // 2 // LANES
    taps = w.shape[0]

    def body(v_ref, g_ref, w_ref, b_ref, o_ref):
        rows = lax.broadcasted_iota(jnp.int32, (S, LANES), 0)
        u = v_ref[...] * _sigmoid(g_ref[...])
        o_ref[...] = _dwconv(u, w_ref, taps, rows) + b_ref[...]

    return _pcall(body, name=name, grid=(nblk,),
                  in_specs=[_col(S), _col(S, nblk), pl.BlockSpec((taps, LANES), lambda j: (0, j)),
                            pl.BlockSpec((1, LANES), lambda j: (0, j))],
                  out_specs=_col(S), out_shape=SDS((S, nblk * LANES), F32),
                  compiler_params=_cparams(VMEM_BIG))(p, p, w, bias)


def _glu_conv_bwd(p, w, dout, name):
    S, C2 = p.shape
    nblk = C2 // 2 // LANES
    taps = w.shape[0]

    def body(v_ref, g_ref, w_ref, do_ref, dv_ref, dg_ref, dw_ref, db_ref):
        rows = lax.broadcasted_iota(jnp.int32, (S, LANES), 0)
        val, sg = v_ref[...], _sigmoid(g_ref[...])
        do_v = do_ref[...]
        du = _dwconv_bwd(val * sg, do_v, w_ref, dw_ref, taps, rows)
        dv_ref[...] = du * sg
        dg_ref[...] = du * val * sg * (1.0 - sg)
        db_ref[...] = jnp.sum(do_v, axis=0, keepdims=True)

    dval, dgate, dw, db = _pcall(
        body, name=name, grid=(nblk,),
        in_specs=[_col(S), _col(S, nblk), pl.BlockSpec((taps, LANES), lambda j: (0, j)), _col(S)],
        out_specs=(_col(S), _col(S), pl.BlockSpec((taps, LANES), lambda j: (0, j)),
                   pl.BlockSpec((1, LANES), lambda j: (0, j))),
        out_shape=(SDS((S, C2 // 2), F32), SDS((S, C2 // 2), F32), SDS((taps, C2 // 2), F32), SDS((1, C2 // 2), F32)),
        compiler_params=_cparams(VMEM_BIG))(p, p, w, dout)
    return dval, dgate, dw, db


def _log1p(e):
    u = 1.0 + e
    d = jnp.where(u == 1.0, 1.0, u - 1.0)
    return jnp.where(u == 1.0, e, jnp.log(u) * (e / d))


def _gate_parts(ps, prm, H):
    lane = lax.broadcasted_iota(jnp.int32, ps.shape, 1)
    is_b, is_g = lane < H, (lane >= H) & (lane < 2 * H)
    beta = _sigmoid(ps)
    t = ps + prm[1:2, :]
    sp = jnp.maximum(t, 0.0) + _log1p(jnp.exp(-jnp.abs(t)))
    na = -jnp.exp(prm[0:1, :])
    return is_b, is_g, beta, t, sp, na


def _gates(ps, prm, H, name):
    S = ps.shape[0]
    ts = _tok(S)

    def body(p_ref, r_ref, o_ref):
        is_b, is_g, beta, _, sp, na = _gate_parts(p_ref[...], r_ref[...], H)
        o_ref[...] = jnp.where(is_b, beta, jnp.where(is_g, na * sp, 0.0))

    return _pcall(body, name=name, grid=(S // ts,), in_specs=[_row(ts, LANES), _full(prm)],
                  out_specs=_row(ts, LANES), out_shape=SDS((S, LANES), F32))(ps, prm)


def _gates_bwd(ps, prm, dgates, H, name):
    S = ps.shape[0]
    ts = _tok(S)

    def body(p_ref, r_ref, dg_ref, dp_ref, dr_ref):
        is_b, is_g, beta, t, sp, na = _gate_parts(p_ref[...], r_ref[...], H)
        dg_v = dg_ref[...]
        dsp = jnp.where(is_g, dg_v * na * _sigmoid(t), 0.0)
        dp_ref[...] = jnp.where(is_b, dg_v * beta * (1.0 - beta), dsp)
        _acc_rows(dr_ref, pl.program_id(0),
                  [jnp.sum(jnp.where(is_g, dg_v * na * sp, 0.0), axis=0, keepdims=True),
                   jnp.sum(dsp, axis=0, keepdims=True)])

    return _pcall(body, name=name, grid=(S // ts,), in_specs=[_row(ts, LANES), _full(prm), _row(ts, LANES)],
                  out_specs=(_row(ts, LANES), pl.BlockSpec((2, LANES), lambda i: (0, 0))),
                  out_shape=(SDS((S, LANES), F32), SDS((2, LANES), F32)))(ps, prm, dgates)


_NN = (((2,), (1,)), ((0,), (0,)))
_NT = (((2,), (2,)), ((0,), (0,)))
_TN = (((1,), (1,)), ((0,), (0,)))


def _mdot(a, b, dims):
    return lax.dot_general(a.astype(_MXU), b.astype(_MXU), dims, preferred_element_type=F32)


def _mdot3(a, b, dims):
    ah, bh = a.astype(_MXU), b.astype(_MXU)
    al, bl = a - ah.astype(F32), b - bh.astype(F32)
    return _mdot(ah, bh, dims) + (_mdot(ah, bl, dims) + _mdot(al, bh, dims))


def _rounded_dot(dims, da_dims, db_dims, a_first, prod=_mdot):
    @jax.custom_vjp
    def f(a, b):
        return prod(a, b, dims)

    def fwd(a, b):
        return prod(a, b, dims), (a, b)

    def bwd(res, ct):
        a, b = res
        da = prod(ct, b, da_dims) if a_first[0] else prod(b, ct, da_dims)
        db = prod(ct, a, db_dims) if a_first[1] else prod(a, ct, db_dims)
        return da, db

    f.defvjp(fwd, bwd)
    return f


_mdot_nn = _rounded_dot(_NN, _NT, _TN, (True, False))
_mdot_nt = _rounded_dot(_NT, _NN, _TN, (True, True))
_mdot_tn = _rounded_dot(_TN, _NT, _NN, (False, False))
def _unit_lower_inverse(a):
    C = a.shape[-1]
    ri = lax.broadcasted_iota(jnp.int32, (1, C, C), 1)
    ci = lax.broadcasted_iota(jnp.int32, (1, C, C), 2)
    t_inv = jnp.where(ri == ci, 1.0, 0.0) - a
    p = a
    for _ in range(max(C.bit_length() - 2, 0)):
        p = _mdot3(p, p, _NN)
        t_inv = t_inv + _mdot3(t_inv, p, _NN)
    return t_inv


@jax.custom_vjp
def _known_inverse(a, t_inv):
    del a
    return t_inv


def _known_inverse_fwd(a, t_inv):
    del a
    return t_inv, t_inv


def _known_inverse_bwd(t_inv, ct):
    da = -_mdot3(_mdot3(t_inv, ct, _TN), t_inv, _NT)
    return da, jnp.zeros_like(t_inv)


_known_inverse.defvjp(_known_inverse_fwd, _known_inverse_bwd)


def _head_cols(gates, off, H):
    lane = lax.broadcasted_iota(jnp.int32, gates.shape, 1)
    cols = [jnp.sum(jnp.where(lane == off + h, gates, 0.0), axis=-1, keepdims=True) for h in range(H)]
    return jnp.concatenate([col[None] for col in cols], axis=0)


def _delta_chunk(qr, kr, v, z, gates, nw, s_in, t_known=None):
    H, C, dk = qr.shape
    beta, g = _head_cols(gates, 0, H), _head_cols(gates, H, H)
    q = qr * lax.rsqrt(jnp.sum(qr * qr, axis=-1, keepdims=True) + L2_EPS) * (dk ** -0.5)
    k = kr * lax.rsqrt(jnp.sum(kr * kr, axis=-1, keepdims=True) + L2_EPS)
    ri = lax.broadcasted_iota(jnp.int32, (1, C, C), 1)
    ci = lax.broadcasted_iota(jnp.int32, (1, C, C), 2)
    causal, strict, eye = ri >= ci, ri > ci, ri == ci
    gam_row = jnp.sum(jnp.where(ri <= ci, g, 0.0), axis=1, keepdims=True)
    gam_col = jnp.sum(jnp.where(eye, gam_row, 0.0), axis=-1, keepdims=True)
    g_last = jnp.sum(g, axis=1, keepdims=True)
    decay = jnp.where(causal, jnp.exp(jnp.where(causal, gam_col - gam_row, 0.0)), 0.0)
    kb = k * beta
    a = jnp.where(strict, _mdot_nt(kb, k) * decay, 0.0)
    t_inv = _unit_lower_inverse(a) if t_known is None else _known_inverse(a, t_known)
    eg = jnp.exp(gam_col)
    u = _mdot_nn(t_inv, v * beta)
    w = _mdot_nn(t_inv, kb * eg)
    a_qk = _mdot_nt(q, k) * decay
    v_new = u - _mdot_nn(w, s_in)
    o = _mdot_nn(q * eg, s_in) + _mdot_nn(a_qk, v_new)
    s_out = s_in * jnp.exp(g_last) + _mdot_tn(k * jnp.exp(g_last - gam_col), v_new)
    og = o * lax.rsqrt(jnp.mean(o * o, axis=-1, keepdims=True) + RMS_EPS) * nw * (z * _sigmoid(z))
    return og, s_out, t_inv


def _heads(ref, H, dk):
    return jnp.stack([ref[:, h * dk:(h + 1) * dk].astype(F32) for h in range(H)])


def _put_heads(ref, val, dk):
    for h in range(val.shape[0]):
        ref[:, h * dk:(h + 1) * dk] = val[h].astype(ref.dtype)


def _delta_fwd(qkv, pm, gates, nw, H, name):
    S = qkv.shape[0]
    hd = qkv.shape[1] // 3
    dk = hd // H
    N = S // CHUNK
    blk = lambda off: pl.BlockSpec((CHUNK, hd), lambda n: (n, off))

    def body(q_ref, k_ref, v_ref, z_ref, g_ref, nw_ref, og_ref, st_ref, ti_ref, s_scr):
        @pl.when(pl.program_id(0) == 0)
        def _():
            s_scr[...] = jnp.zeros_like(s_scr)

        s_in = s_scr[...]
        st_ref[...] = s_in
        og, s_out, t_inv = _delta_chunk(_heads(q_ref, H, dk), _heads(k_ref, H, dk), _heads(v_ref, H, dk),
                                        _heads(z_ref, H, dk), g_ref[...], nw_ref[...], s_in)
        _put_heads(og_ref, og, dk)
        ti_ref[...] = t_inv
        s_scr[...] = s_out

    return _pcall(
        body, name=name, grid=(N,),
        in_specs=[blk(0), blk(1), blk(2), blk(3), pl.BlockSpec((CHUNK, LANES), lambda n: (n, 0)), _full(nw)],
        out_specs=(blk(0), pl.BlockSpec((None, H, dk, dk), lambda n: (n, 0, 0, 0)),
                   pl.BlockSpec((None, H, CHUNK, CHUNK), lambda n: (n, 0, 0, 0))),
        out_shape=(SDS((S, hd), _MXU), SDS((N, H, dk, dk), F32), SDS((N, H, CHUNK, CHUNK), F32)),
        scratch_shapes=[pltpu.VMEM((H, dk, dk), F32)],
        compiler_params=_cparams(VMEM_BIG),
    )(qkv, qkv, qkv, pm, gates, nw)


def _delta_bwd(qkv, pm, gates, nw, states, t_invs, dog, H, name):
    S = qkv.shape[0]
    hd = qkv.shape[1] // 3
    dk = hd // H
    N = S // CHUNK
    blk = lambda off: pl.BlockSpec((CHUNK, hd), lambda n: (N - 1 - n, off))
    gspec = pl.BlockSpec((CHUNK, LANES), lambda n: (N - 1 - n, 0))

    def body(q_ref, k_ref, v_ref, z_ref, g_ref, nw_ref, st_ref, ti_ref, do_ref,
             dq_ref, dk_ref, dv_ref, dz_ref, dg_ref, dnw_ref, ds_scr):
        n = pl.program_id(0)

        @pl.when(n == 0)
        def _():
            ds_scr[...] = jnp.zeros_like(ds_scr)

        t_known = ti_ref[...]
        fn = functools.partial(_delta_chunk, t_known=t_known)
        _, vjp = jax.vjp(fn, _heads(q_ref, H, dk), _heads(k_ref, H, dk), _heads(v_ref, H, dk),
                         _heads(z_ref, H, dk), g_ref[...], nw_ref[...], st_ref[...])
        dq, dkk, dv, dz, dg, dnw, ds_in = vjp((_heads(do_ref, H, dk), ds_scr[...], jnp.zeros_like(t_known)))
        _put_heads(dq_ref, dq, dk)
        _put_heads(dk_ref, dkk, dk)
        _put_heads(dv_ref, dv, dk)
        _put_heads(dz_ref, dz, dk)
        ds_scr[...] = ds_in
        dg_ref[...] = dg

        @pl.when(n == 0)
        def _():
            dnw_ref[...] = dnw

        @pl.when(n != 0)
        def _():
            dnw_ref[...] += dnw

    return _pcall(
        body, name=name, grid=(N,),
        in_specs=[blk(0), blk(1), blk(2), blk(3), gspec, _full(nw),
                  pl.BlockSpec((None, H, dk, dk), lambda n: (N - 1 - n, 0, 0, 0)),
                  pl.BlockSpec((None, H, CHUNK, CHUNK), lambda n: (N - 1 - n, 0, 0, 0)), blk(0)],
        out_specs=(blk(0), blk(0), blk(0), blk(3), gspec, pl.BlockSpec((1, dk), lambda n: (0, 0))),
        out_shape=(SDS((S, hd), F32), SDS((S, hd), F32), SDS((S, hd), F32), SDS(pm.shape, F32),
                   SDS((S, LANES), F32), SDS((1, dk), F32)),
        scratch_shapes=[pltpu.VMEM((H, dk, dk), F32)],
        compiler_params=_cparams(VMEM_BIG),
    )(qkv, qkv, qkv, pm, gates, nw, states, t_invs, dog)


def _rows_block(R, C):
    rb = R
    while rb * C * 4 > (1 << 20) and rb % 16 == 0:
        rb //= 2
    return rb


def _sum_slots(st, name, out_dtype=F32):
    n, R, C = st.shape
    rb = _rows_block(R, C)

    def body(s_ref, o_ref):
        acc = s_ref[0].astype(F32)
        for q in range(1, n):
            acc = acc + s_ref[q].astype(F32)
        o_ref[...] = acc.astype(o_ref.dtype)

    return _pcall(body, name=name, grid=(R // rb,), in_specs=[pl.BlockSpec((n, rb, C), lambda i: (0, i, 0))],
                  out_specs=pl.BlockSpec((rb, C), lambda i: (i, 0)), out_shape=SDS((R, C), out_dtype))(st)


def _scalar(v):
    return jnp.reshape(v, (1,)).astype(jnp.int32)


def _place_block(w, layer, ax, chip, dep, name):
    _, R, C = w.shape
    rb = _rows_block(R, C)
    nrb = R // rb
    shp = [R, C]
    shp[ax] *= 4
    omap = (lambda i, c: (c[0] * nrb + i, 0)) if ax == 0 else (lambda i, c: (i, c[0]))

    def body(c_ref, w_ref, dep_ref, o_ref):
        del c_ref, dep_ref
        o_ref[...] = w_ref[...].astype(o_ref.dtype)

    grid_spec = pltpu.PrefetchScalarGridSpec(
        num_scalar_prefetch=1, grid=(nrb,),
        in_specs=[pl.BlockSpec((None, rb, C), lambda i, c: (layer, i, 0)), _ANY],
        out_specs=pl.BlockSpec((rb, C), omap))
    return _pcall(body, name=name, grid_spec=grid_spec, out_shape=SDS(tuple(shp), _MXU))(_scalar(chip), w, dep)


def _sum_pair(own, recv, ic, name, out_dtype):
    N, _, R, C = own.shape
    rb = _rows_block(R, C)

    def body(c_ref, a_ref, b_ref, o_ref):
        del c_ref
        o_ref[...] = (a_ref[...].astype(F32) + b_ref[...].astype(F32)).astype(o_ref.dtype)

    grid_spec = pltpu.PrefetchScalarGridSpec(
        num_scalar_prefetch=1, grid=(N, R // rb),
        in_specs=[pl.BlockSpec((None, None, rb, C), lambda p, i, c: (p, c[0], i, 0)),
                  pl.BlockSpec((None, None, rb, C), lambda p, i, c: (1 - c[0], p, i, 0))],
        out_specs=pl.BlockSpec((None, rb, C), lambda p, i, c: (p, i, 0)))
    return _pcall(body, name=name, grid_spec=grid_spec, out_shape=SDS((N, R, C), out_dtype))(
        _scalar(ic), own, recv)


def _sum_landed(grad, ax, land, chip, name):
    _, R, Cb = land.shape
    rb = _rows_block(R, Cb)
    nrb = R // rb
    if ax == 0:
        own_spec = pl.BlockSpec((rb, Cb), lambda i, c: (c[0] * nrb + i, 0))
    else:
        own_spec = pl.BlockSpec((rb, Cb), lambda i, c: (i, c[0]))
    slot = lambda d: pl.BlockSpec((None, rb, Cb), lambda i, c: ((c[0] + d) % 4, i, 0))

    def body(c_ref, own_ref, r1, r2, r3, o_ref):
        del c_ref
        o_ref[...] = ((own_ref[...].astype(F32) + r1[...].astype(F32)) + r2[...].astype(F32)) + r3[...].astype(F32)

    grid_spec = pltpu.PrefetchScalarGridSpec(
        num_scalar_prefetch=1, grid=(R // rb,), in_specs=[own_spec, slot(1), slot(2), slot(3)],
        out_specs=pl.BlockSpec((rb, Cb), lambda i, c: (i, 0)))
    return _pcall(body, name=name, grid_spec=grid_spec, out_shape=SDS((R, Cb), F32))(
        _scalar(chip), grad, land, land, land)


def _adamw_step(g, w_ref, m_ref, v_ref, g_ref, d_ref, mo_ref, vo_ref):
    m_new = ADAM_B1 * m_ref[...] + (1.0 - ADAM_B1) * g
    v_new = ADAM_B2 * v_ref[...] + (1.0 - ADAM_B2) * (g * g)
    m_hat = m_new / (1.0 - ADAM_B1 ** ADAM_STEP)
    v_hat = v_new / (1.0 - ADAM_B2 ** ADAM_STEP)
    g_ref[...] = g
    d_ref[...] = -ADAM_LR * (m_hat / (jnp.sqrt(v_hat) + ADAM_EPS) + ADAM_WD * w_ref[...])
    mo_ref[...] = m_new
    vo_ref[...] = v_new


def _adamw(w, m, v, st, name):
    R, C = w.shape
    n = st.shape[0]
    rb = _rows_block(R, C)
    spec = pl.BlockSpec((rb, C), lambda i: (i, 0))

    def body(w_ref, m_ref, v_ref, s_ref, *o_refs):
        g = s_ref[0]
        for q in range(1, n):
            g = g + s_ref[q]
        _adamw_step(g, w_ref, m_ref, v_ref, *o_refs)

    return _pcall(body, name=name, grid=(R // rb,),
                  in_specs=[spec, spec, spec, pl.BlockSpec((n, rb, C), lambda i: (0, i, 0))],
                  out_specs=(spec,) * 4, out_shape=(SDS((R, C), F32),) * 4)(w, m, v, st)


def _adamw_pair(w, m, v, layer, own, recv2, ic, bufs, dep, name):
    L, R, C = w.shape
    rb = _rows_block(R, C)
    spec = pl.BlockSpec((None, rb, C), lambda i, c: (layer, i, 0))

    def body(c_ref, w_ref, m_ref, v_ref, own_ref, recv_ref, *rest):
        del c_ref
        _adamw_step(own_ref[...] + recv_ref[...], w_ref, m_ref, v_ref, *rest[-4:])

    nbuf = 0 if bufs is None else 4
    grid_spec = pltpu.PrefetchScalarGridSpec(
        num_scalar_prefetch=1, grid=(R // rb,),
        in_specs=[spec, spec, spec, pl.BlockSpec((rb, C), lambda i, c: (i, 0)),
                  pl.BlockSpec((None, rb, C), lambda i, c: (1 - c[0], i, 0))] + [_ANY] * (nbuf + 1),
        out_specs=(spec,) * 4)
    extra = {} if bufs is None else dict(input_output_aliases={6 + q: q for q in range(4)})
    return _pcall(body, name=name, grid_spec=grid_spec, out_shape=(SDS((L, R, C), F32),) * 4, **extra)(
        _scalar(ic), w, m, v, own, recv2, *([] if bufs is None else bufs), dep)


def _pack(arrs, rows=1):
    flat = jnp.concatenate([a.reshape(-1).astype(F32) for a in arrs])
    quantum = rows * LANES
    pad = (-flat.shape[0]) % quantum
    flat = jnp.pad(flat, (0, pad))
    return flat.reshape(rows, -1)


def _unpack(flat, shapes):
    flat = flat.reshape(-1)
    out, off = [], 0
    for shp in shapes:
        size = 1
        for d in shp:
            size *= d
        out.append(flat[off:off + size].reshape(shp))
        off += size
    return out


def _mlp_fwd(x1, mod, lnp, w1, w2, tag, h2=None, nxt=None):
    if h2 is None:
        h2 = _modulate(x1, mod, 3, 4, f"{tag}_mod")
    a1, a2 = _mm(h2, w1, "nn", name=f"{tag}_up", out_dtypes=(_MXU, _MXU),
                 out_fn=lambda r: (r, jnp.square(jnp.maximum(r, 0.0))))
    y2 = _mm(a2, w2, "nn", name=f"{tag}_down")
    x2, h_next = _combine(x1, y2, mod, 5, lnp, 2, f"{tag}_ln", nxt)
    return x2, (x1, h2, a1, a2, y2), h_next


def _weight_grad(grads, key, a, b, name):
    grads[key] = _mm(a, b, "tn", name=name, out_dtypes=(_MXU,), tk=2 * MM_TILE)


def _mlp_bwd(dx2, saved, mod, lnp, w1, w2, tag, stacks):
    x1, h2, a1, a2, y2 = saved
    dxa, dy2, dp, dss_later = _sublayer_head_bwd(dx2, x1, y2, mod, 5, lnp, 2, tag)
    da1 = _mm(dy2, w2, "nt", name=f"{tag}_down_bx", out_dtypes=(_MXU,), aux=[(a1, "mn")],
              out_fn=lambda r, a: (r * (2.0 * jnp.maximum(a.astype(F32), 0.0)),))
    _weight_grad(stacks, "ff_w2", a2, dy2, f"{tag}_down_bw")
    _weight_grad(stacks, "ff_w1", h2, da1, f"{tag}_up_bw")
    dh2 = _mm(da1, w1, "nt", name=f"{tag}_up_bx")
    return (x1, mod, 4, dh2, dxa), (dss_later, dp)


def _dn_fwd(x, mod, lnp, wts, H, tag, h=None, nxt=None):
    w_main, w_small, conv_w, prm, nw, w_out = wts
    if h is None:
        h = _modulate(x, mod, 0, 1, f"{tag}_mod")
    pm = _mm(h, w_main, "nn", name=f"{tag}_in")
    ps = _mm(h, w_small, "nn", name=f"{tag}_in_s")
    nqkv = conv_w.shape[1] // LANES
    qkv = _conv_silu(pm, conv_w, nqkv, f"{tag}_conv")
    gates = _gates(ps, prm, H, f"{tag}_gates")
    og, *states = _delta_fwd(qkv, pm, gates, nw, H, f"{tag}_delta")
    y = _mm(og, w_out, "nn", name=f"{tag}_out")
    x1, h_next = _combine(x, y, mod, 2, lnp, 0, f"{tag}_ln", nxt)
    return x1, (x, h, pm, ps, qkv, gates, states, og, y), h_next


def _dn_bwd(dx1, saved, mod, lnp, wts, H, tag, stacks):
    w_main, w_small, conv_w, prm, nw, w_out = wts
    x, h, pm, ps, qkv, gates, states, og, y = saved
    dxa, dy, dp, dss_later = _sublayer_head_bwd(dx1, x, y, mod, 2, lnp, 0, tag)
    dog = _mm(dy, w_out, "nt", name=f"{tag}_out_bx")
    _weight_grad(stacks, "dn_w_out", og, dy, f"{tag}_out_bw")
    dq, dk, dv, dpm, dgates, dnw = _delta_bwd(qkv, pm, gates, nw, *states, dog, H, f"{tag}_delta_b")
    dps, dprm = _gates_bwd(ps, prm, dgates, H, f"{tag}_gates_b")
    dcw = []
    nb = dq.shape[1] // LANES
    for part, dpart in enumerate((dq, dk, dv)):
        dpm, dcw_p = _conv_silu_bwd(pm, conv_w, dpart, dpm, part * nb, f"{tag}_conv_b{part}")
        dcw.append(dcw_p)
    dconv_w = jnp.concatenate(dcw, axis=1)
    dw_main = _mm(h, dpm, "tn", name=f"{tag}_in_bw", out_dtypes=(_MXU,))
    dw_small = _mm(h, dps, "tn", name=f"{tag}_in_s_bw", out_dtypes=(_MXU,))
    dh_s = _mm(dps, w_small, "nt", name=f"{tag}_in_s_bx")
    dh = _mm(dpm, w_main, "nt", name=f"{tag}_in_bx", aux=[(dh_s, "mn")], out_fn=lambda r, e: (r + e,))
    return (x, mod, 1, dh, dxa), (dw_main, dw_small, dconv_w, dprm, dnw), (dss_later, dp)


def _cf_fwd(x, mod, lnp, wts, tag, h=None, nxt=None):
    w_in, dw_w, dw_b, cln, w_out = wts
    if h is None:
        h = _modulate(x, mod, 0, 1, f"{tag}_mod")
    p = _mm(h, w_in, "nn", name=f"{tag}_in")
    u2 = _glu_conv(p, dw_w, dw_b, f"{tag}_conv")
    u3 = _ln_silu(u2, cln, f"{tag}_cln")
    y = _mm(u3, w_out, "nn", name=f"{tag}_out")
    x1, h_next = _combine(x, y, mod, 2, lnp, 0, f"{tag}_ln", nxt)
    return x1, (x, h, p, u2, u3, y), h_next


def _cf_bwd(dx1, saved, mod, lnp, wts, tag, stacks):
    w_in, dw_w, dw_b, cln, w_out = wts
    x, h, p, u2, u3, y = saved
    dxa, dy, dp, dss_later = _sublayer_head_bwd(dx1, x, y, mod, 2, lnp, 0, tag)
    du3 = _mm(dy, w_out, "nt", name=f"{tag}_out_bx")
    _weight_grad(stacks, "cf_w_out", u3, dy, f"{tag}_out_bw")
    du2, dcln = _ln_silu_bwd(u2, cln, du3, f"{tag}_cln_b")
    dval, dgate, ddw_w, ddw_b = _glu_conv_bwd(p, dw_w, du2, f"{tag}_conv_b")
    dpp = jnp.concatenate([dval, dgate], axis=1)
    _weight_grad(stacks, "cf_w_in", h, dpp, f"{tag}_in_bw")
    dh = _mm(dpp, w_in, "nt", name=f"{tag}_in_bx")
    return (x, mod, 1, dh, dxa), (ddw_w, ddw_b, dcln), (dss_later, dp)


def _two_d(a):
    return a.reshape(-1, a.shape[-1])


def kernel(x, c, ada_w, ada_b, ln_g, ln_b, dn_w_in, dn_conv_w, dn_a_log, dn_dt_bias, dn_norm_w, dn_w_out, cf_w_in, cf_dw_w, cf_dw_b, cf_ln_g, cf_ln_b, cf_w_out, ff_w1, ff_w2, loss_target, m_ada_w, m_ada_b, m_ln_g, m_ln_b, m_dn_w_in, m_dn_conv_w, m_dn_a_log, m_dn_dt_bias, m_dn_norm_w, m_dn_w_out, m_cf_w_in, m_cf_dw_w, m_cf_dw_b, m_cf_ln_g, m_cf_ln_b, m_cf_w_out, m_ff_w1, m_ff_w2, v_ada_w, v_ada_b, v_ln_g, v_ln_b, v_dn_w_in, v_dn_conv_w, v_dn_a_log, v_dn_dt_bias, v_dn_norm_w, v_dn_w_out, v_cf_w_in, v_cf_dw_w, v_cf_dw_b, v_cf_ln_g, v_cf_ln_b, v_cf_w_out, v_ff_w1, v_ff_w2):
    ix, iy, ic = lax.axis_index("x"), lax.axis_index("y"), lax.axis_index("c")
    chip = 2 * ix + iy
    dev = 4 * ix + 2 * iy + ic
    S, D = x.shape[1], x.shape[2]
    L = ada_w.shape[0]
    LA, LB = dn_w_in.shape[0], cf_w_in.shape[0]
    H = dn_a_log.shape[1]
    NMOD = ada_b.shape[1] // D
    dn_in = dn_w_in.shape[2] * 4
    n_main = dn_in - 2 * H
    assert L == N_LAYERS and 2 * H <= LANES
    x0, tgt = x[0], loss_target[0]

    small_sharded = [ln_g, ln_b, dn_conv_w, cf_dw_w, cf_dw_b, cf_ln_g, cf_ln_b]
    small_axes = [2, 2, 2, 2, 1, 1, 1]
    packed_small = _pack(small_sharded, rows=8)[None]
    big = {"dn_w_in": (dn_w_in, m_dn_w_in, v_dn_w_in, 0), "dn_w_out": (dn_w_out, m_dn_w_out, v_dn_w_out, 0),
           "cf_w_in": (cf_w_in, m_cf_w_in, v_cf_w_in, 1), "cf_w_out": (cf_w_out, m_cf_w_out, v_cf_w_out, 0),
           "ff_w1": (ff_w1, m_ff_w1, v_ff_w1, 1), "ff_w2": (ff_w2, m_ff_w2, v_ff_w2, 0)}

    def group(g):
        l = g // 2
        if g % 2:
            return {"ff_w1": l, "ff_w2": l}
        mixer = ("dn_w_in", "dn_w_out") if l % 2 == 0 else ("cf_w_in", "cf_w_out")
        return {mixer[0]: l // 2, mixer[1]: l // 2}

    def place(l, dep):
        out = []
        for nm, lw in group(l).items():
            out.append(_place_block(big[nm][0], lw, big[nm][3], chip, dep, f"l{l}_place_{nm}"))
            dep = out[-1]
        return out

    def start_gather(l, placed, after):
        names = list(group(l))
        axes = [big[nm][3] for nm in names]
        send, recv, arrs, token = _gather_start(list(zip(placed, axes)), after, f"l{l}_gather_start")
        return names, axes, arrs, send, recv, token

    def finish_gather(l, pending, after):
        names, axes, arrs, send, recv, _ = pending
        arrs = _gather_wait(list(zip(arrs, axes)), send, recv, after, f"l{l}_gather_wait")
        arrs = _gather_forward(list(zip(arrs, axes)), f"l{l}_gather_pass")
        return dict(zip(names, arrs))

    g_small = _exchange([(packed_small, 0)], "xy", "gather", "gather_small")[0]
    shard_shapes = [a.shape for a in small_sharded]
    per_chip = [_unpack(g_small[q], shard_shapes) for q in range(4)]
    ln_g_f, ln_b_f, conv_w_f, dw_w_f, dw_b_f, cln_g_f, cln_b_f = [
        jnp.concatenate([per_chip[q][i] for q in range(4)], axis=small_axes[i]) for i in range(len(small_sharded))]

    c_all = _exchange([(c[None], 0)], "all", "gather", "gather_cond")[0].reshape(8, D)
    c_pad = jnp.pad(c_all, ((0, 8), (0, 0)))
    mod_sh = jnp.stack([_mm(c_pad, (ada_w, l), "nn", name=f"ada_{l}", a_fn=lambda t: t * _sigmoid(t))
                        for l in range(L)])
    mod_all = _exchange([(mod_sh, 2)], "xy", "gather", "gather_mod")[0]
    mod_mine = lax.dynamic_index_in_dim(mod_all, dev, axis=1, keepdims=False) + ada_b
    mods = mod_mine.reshape(L, NMOD, D)

    def lnp_of(l):
        return jnp.stack([ln_g_f[l, 0], ln_b_f[l, 0], ln_g_f[l, 1], ln_b_f[l, 1]])

    def mixer_wts(l, wl):
        j = l // 2
        if l % 2 == 0:
            w_in = jnp.transpose(wl["dn_w_in"].reshape(4, D, dn_in // 4), (1, 0, 2)).reshape(D, dn_in)
            w_small = jnp.pad(w_in[:, n_main:], ((0, 0), (0, LANES - 2 * H)))
            prm = jnp.zeros((2, LANES), F32).at[0, H:2 * H].set(dn_a_log[j]).at[1, H:2 * H].set(dn_dt_bias[j])
            return (w_in[:, :n_main], w_small, conv_w_f[j], prm, dn_norm_w[j][None], wl["dn_w_out"])
        return (wl["cf_w_in"], dw_w_f[j], dw_b_f[j][None], jnp.stack([cln_g_f[j], cln_b_f[j]]), wl["cf_w_out"])

    xs = x0
    saved, wts, mod_of, h_in = [], [], [], None
    pending = start_gather(0, place(0, mods), mods + jnp.minimum(jnp.abs(g_small[0, 0, 0]), 0.0))
    placed = [None]
    for g in range(1, 2 * L):
        placed.append(place(g, pending[5] if g == 1 else placed[-1][-1]))
    for g in range(2 * L):
        l = g // 2
        wl = finish_gather(g, pending, xs if g else placed[-1][-1])
        mod_g = mods[l]
        if g + 1 < 2 * L:
            pending = start_gather(g + 1, placed[g + 1], next(iter(wl.values())))
            mod_g = mod_g + pending[5][0, 0]
        mod_of.append(mod_g)
        nxt = None if g + 1 == 2 * L else ((mods[l], 3, 4) if g % 2 == 0 else (mods[l + 1], 0, 1))
        if g % 2:
            wts.append((wl["ff_w1"], wl["ff_w2"]))
            xs, sv, h_in = _mlp_fwd(xs, mod_g, lnp_of(l), *wts[g], f"l{l}_ff", h_in, nxt)
        elif l % 2 == 0:
            wts.append(mixer_wts(l, wl))
            xs, sv, h_in = _dn_fwd(xs, mod_g, lnp_of(l), wts[g], H, f"l{l}_dn", h_in, nxt)
        else:
            wts.append(mixer_wts(l, wl))
            xs, sv, h_in = _cf_fwd(xs, mod_g, lnp_of(l), wts[g], f"l{l}_cf", h_in, nxt)
        saved.append(sv)
    dx, loss_local = _loss_head(xs, tgt, "loss_head")
    loss = lax.psum(loss_local[0, 0], ("x", "y", "c"))

    def start_scatter(l, grads):
        names = list(group(l))
        axes = [big[nm][3] for nm in names]
        send, recv, sums, lands, token = _scatter_start([(grads[nm], ax) for nm, ax in zip(names, axes)], mods,
                                                        f"l{l}_scatter_start")
        return names, axes, sums, lands, send, recv, token

    def finish_scatter(l, pending, after):
        names, axes, sums, lands, send, recv, _ = pending
        sums, lands = _scatter_wait(list(zip(sums, axes)), lands, send, recv, after, f"l{l}_scatter_wait")
        mine = [_sum_landed(s, ax, ld, chip, f"l{l}_sum_grads_{nm}") for nm, ax, s, ld in zip(names, axes, sums, lands)]
        other = _exchange([(s[None], 0) for s in mine], "c", "gather", f"l{l}_swap_sums", keep_own=False)
        return {nm: (s, o) for nm, s, o in zip(names, mine, other)}

    g_dn = [None] * LA
    g_cf = [None] * LB
    dmods, dlns = [None] * L, [None] * L
    big_sums = {}
    pending, token = None, None
    dss_of, dp_of = {}, {}
    for g in reversed(range(2 * L)):
        l, j = g // 2, g // 4
        mod_g = mod_of[g] if token is None else mod_of[g] + token[0, 0]
        grads = {}
        if g % 2:
            dx, (dss_of[g + 1], dp_of[g]) = _mlp_bwd(dx, saved[g], mod_g, lnp_of(l), *wts[g], f"l{l}_ff", grads)
        elif l % 2 == 0:
            dx, g_dn[j], (dss_of[g + 1], dp_of[g]) = _dn_bwd(dx, saved[g], mod_g, lnp_of(l), wts[g], H,
                                                              f"l{l}_dn", grads)
            dn_in_g = jnp.concatenate([g_dn[j][0], g_dn[j][1][:, :2 * H]], axis=1)
            grads["dn_w_in"] = jnp.transpose(dn_in_g.reshape(D, 4, dn_in // 4), (1, 0, 2)).reshape(4 * D, dn_in // 4)
        else:
            dx, g_cf[j], (dss_of[g + 1], dp_of[g]) = _cf_bwd(dx, saved[g], mod_g, lnp_of(l), wts[g],
                                                              f"l{l}_cf", grads)
        if pending is not None:
            for nm, pair_of in finish_scatter(g + 1, pending, dx[3]).items():
                big_sums[nm, group(g + 1)[nm]] = pair_of
        pending = start_scatter(g, grads)
        token = pending[6]
    x_in, mod_in, r_sc_in, dh_in, dxa_in = dx
    dx, dss_of[0] = _modulate_bwd(x_in, mod_in, r_sc_in, dh_in, dxa_in, "l0_dn_mod_b")
    big_names = ["dn_w_in", "dn_w_out", "cf_w_in", "cf_w_out", "ff_w1", "ff_w2"]
    big_res = {nm: None for nm in big_names}

    def update(nm, layer):
        w, m, v, _ = big[nm]
        own, oth = big_sums[nm, layer]
        big_res[nm] = _adamw_pair(w, m, v, layer, own, oth, ic, big_res[nm], token, f"adamw_{nm}_{layer}")

    for nm in big_names:
        for layer in range(big[nm][0].shape[0]):
            if group(0).get(nm) != layer:
                update(nm, layer)
    for nm, pair_of in finish_scatter(0, pending, big_res["ff_w2"][0]).items():
        big_sums[nm, group(0)[nm]] = pair_of
    for nm, layer in group(0).items():
        update(nm, layer)
    big_out = [big_res[nm] for nm in big_names]
    for l in range(L):
        (dss1, dp1), (dss2, dp2) = (dss_of[2 * l], dp_of[2 * l]), (dss_of[2 * l + 1], dp_of[2 * l + 1])
        dmods[l] = jnp.concatenate([dss1, dp1[0:1], dss2, dp2[0:1]], axis=0)
        dlns[l] = (jnp.stack([dp1[1], dp2[1]]), jnp.stack([dp1[2], dp2[2]]))
    grad_x = dx[None]

    d_ln_g = jnp.stack([dlns[l][0] for l in range(L)])
    d_ln_b = jnp.stack([dlns[l][1] for l in range(L)])
    d_conv_w = jnp.stack([g_dn[j][2] for j in range(LA)])
    d_a_log = jnp.stack([g_dn[j][3][0, H:2 * H] for j in range(LA)])
    d_dt_bias = jnp.stack([g_dn[j][3][1, H:2 * H] for j in range(LA)])
    d_norm_w = jnp.stack([g_dn[j][4][0] for j in range(LA)])
    d_dw_w = jnp.stack([g_cf[j][0] for j in range(LB)])
    d_dw_b = jnp.stack([g_cf[j][1][0] for j in range(LB)])
    d_cln_g = jnp.stack([g_cf[j][2][0] for j in range(LB)])
    d_cln_b = jnp.stack([g_cf[j][2][1] for j in range(LB)])
    d_mod = jnp.stack(dmods).reshape(L, NMOD * D)
    small_full = [d_mod, d_ln_g, d_ln_b, d_conv_w, d_dw_w, d_dw_b, d_cln_g, d_cln_b, d_a_log, d_dt_bias, d_norm_w]
    landed = jnp.minimum(jnp.abs(big_sums["dn_w_out", 0][0][0, 0]), 0.0)
    small_all = _exchange([(_pack(small_full, rows=8)[None] + landed, 0)], "all", "gather", "gather_small_grads")[0]
    small_sum = _sum_slots(small_all, "sum_small_grads")
    (s_ada_b, s_ln_g, s_ln_b, s_conv_w, s_dw_w, s_dw_b, s_cln_g, s_cln_b, s_a_log, s_dt_bias, s_norm_w) = _unpack(
        small_sum, [a.shape for a in small_full])
    d_mod_all = small_all.reshape(8, -1)[:, :L * NMOD * D].reshape(8, L, NMOD * D)

    def shard(a, axis):
        size = a.shape[axis] // 4
        return lax.dynamic_slice_in_dim(a, chip * size, size, axis)

    ncol = ada_w.shape[2]
    d_mod_sh = jnp.pad(lax.dynamic_slice_in_dim(d_mod_all, chip * ncol, ncol, 2), ((0, 8), (0, 0), (0, 0)))
    g_ada_w = None
    for l in range(L):
        g_ada_w = _mm(c_pad, d_mod_sh[:, l], "tn", name=f"ada_bw_{l}", a_fn=lambda t: t * _sigmoid(t),
                      stack=(g_ada_w, l, L))

    ada_out =[r.reshape(ada_w.shape) for r in _adamw(_two_d(ada_w), _two_d(m_ada_w), _two_d(v_ada_w),
                                                     _two_d(g_ada_w)[None], "adamw_ada_w")]

    small_w = [(ada_b, m_ada_b, v_ada_b, s_ada_b), (ln_g, m_ln_g, v_ln_g, shard(s_ln_g, 2)),
               (ln_b, m_ln_b, v_ln_b, shard(s_ln_b, 2)), (dn_conv_w, m_dn_conv_w, v_dn_conv_w, shard(s_conv_w, 2)),
               (dn_a_log, m_dn_a_log, v_dn_a_log, s_a_log), (dn_dt_bias, m_dn_dt_bias, v_dn_dt_bias, s_dt_bias),
               (dn_norm_w, m_dn_norm_w, v_dn_norm_w, s_norm_w), (cf_dw_w, m_cf_dw_w, v_cf_dw_w, shard(s_dw_w, 2)),
               (cf_dw_b, m_cf_dw_b, v_cf_dw_b, shard(s_dw_b, 1)), (cf_ln_g, m_cf_ln_g, v_cf_ln_g, shard(s_cln_g, 1)),
               (cf_ln_b, m_cf_ln_b, v_cf_ln_b, shard(s_cln_b, 1))]
    pk = [_pack([t[i] for t in small_w], rows=8) for i in range(4)]
    small_res = _adamw(pk[0], pk[1], pk[2], pk[3][None], "adamw_small")
    small_shapes = [t[0].shape for t in small_w]
    small_out = [_unpack(r, small_shapes) for r in small_res]

    def kind(k):
        sm = small_out[k]
        bg = [o[k] for o in big_out]
        return [ada_out[k], sm[0], sm[1], sm[2], bg[0], sm[3], sm[4], sm[5], sm[6], bg[1],
                bg[2], sm[7], sm[8], sm[9], sm[10], bg[3], bg[4], bg[5]]

    return (loss, grad_x, *kind(0), *kind(1), *kind(2), *kind(3))
```
